```python
import jax, jax.numpy as jnp
from jax import lax
import numpy as np

D_MODEL = 1024
BATCH = 4
SEQ = 4096
DEPTH = 2

D_POOL = D_MODEL // 2
POOL_WINDOWS = (2, 4, 8, 16)
N_POOL_GROUPS = len(POOL_WINDOWS)
POOL_GROUP = D_POOL // N_POOL_GROUPS
D_ATTN = D_MODEL // 2
HEAD_DIM = 64
N_HEADS = D_ATTN // HEAD_DIM
DILATED_PATTERNS = ((128, 1), (512, 4), (2048, 16))
Q_BLOCK = 128
D_IN = D_POOL + 3 * D_ATTN
D_FF = 2816
N_EXPERTS = 8
TOP_K = 2
D_FF_EXPERT = 3584
EPS = 1e-6
N_DENSE = (DEPTH + 1) // 2
N_MOE = DEPTH // 2

kernel_name = "hybrid_pool_dilated_attn_moe_block"


def rmsnorm(x, g):
    xf = x.astype(jnp.float32)
    y = xf * lax.rsqrt(jnp.mean(xf * xf, axis=-1, keepdims=True) + EPS)
    return (y * g.astype(jnp.float32)).astype(x.dtype)


def pool_mixer(u, w_pool, pool_scale):
    B, S, _ = u.shape
    uf = u.astype(jnp.float32)
    c = jnp.concatenate([jnp.zeros((B, 1, D_POOL), jnp.float32), jnp.cumsum(uf, axis=1)], axis=1)
    t = jnp.arange(S)
    diffs = []
    for gi, w in enumerate(POOL_WINDOWS):
        sl = slice(gi * POOL_GROUP, (gi + 1) * POOL_GROUP)
        cg = c[..., sl]
        lo = jnp.maximum(t + 1 - w, 0)
        win_sum = cg[:, 1:] - cg[:, lo]
        cnt = (t + 1 - lo).astype(jnp.float32)[None, :, None]
        diffs.append(win_sum / cnt - uf[..., sl])
    d = jnp.stack(diffs, axis=2).astype(u.dtype)
    z = jnp.einsum('bsgc,gce->bsge', d, w_pool).reshape(B, S, D_POOL)
    return rmsnorm(z, pool_scale)


def dilated_attention(q, k, v):
    B, H, S, Dh = q.shape
    nb = S // Q_BLOCK
    scale = HEAD_DIM ** -0.5
    qb = q.reshape(B, H, nb, Q_BLOCK, Dh).transpose(2, 0, 1, 3, 4)

    def block(args):
        bi, qblk = args
        pos = bi * Q_BLOCK + jnp.arange(Q_BLOCK)
        outs, lses = [], []
        for window, dil in DILATED_PATTERNS:
            n_keys = window // dil + 1
            kidx = pos[:, None] - dil * jnp.arange(n_keys)[None, :]
            valid = kidx >= 0
            kidx = jnp.maximum(kidx, 0)
            kg = k[:, :, kidx]
            vg = v[:, :, kidx]
            s = jnp.einsum('bhqd,bhqjd->bhqj', qblk, kg).astype(jnp.float32) * scale
            s = jnp.where(valid[None, None], s, -jnp.inf)
            m = jnp.max(s, axis=-1, keepdims=True)
            p = jnp.exp(s - m)
            den = jnp.sum(p, axis=-1, keepdims=True)
            o = jnp.einsum('bhqj,bhqjd->bhqd', p, vg.astype(jnp.float32)) / den
            outs.append(o)
            lses.append(m + jnp.log(den))
        wts = jax.nn.softmax(jnp.concatenate(lses, axis=-1), axis=-1)
        o = jnp.einsum('bhqp,pbhqd->bhqd', wts, jnp.stack(outs, axis=0))
        return o.astype(q.dtype)

    ob = lax.map(block, (jnp.arange(nb), qb))
    return ob.transpose(1, 2, 0, 3, 4).reshape(B, H, S, Dh)


def hybrid_mixer(h, w_in, w_pool, pool_scale, attn_gain, w_out):
    B, S, _ = h.shape
    proj = h @ w_in
    u = proj[..., :D_POOL]
    q, k, v = jnp.split(proj[..., D_POOL:], 3, axis=-1)
    to_heads = lambda t: t.reshape(B, S, N_HEADS, HEAD_DIM).transpose(0, 2, 1, 3)
    y_a = pool_mixer(u, w_pool, pool_scale)
    o = dilated_attention(to_heads(q), to_heads(k), to_heads(v))
    y_b = rmsnorm(o.transpose(0, 2, 1, 3).reshape(B, S, D_ATTN), attn_gain)
    return jnp.concatenate([y_a, y_b], axis=-1) @ w_out


def swiglu(h, wg, wu, wd):
    return (jax.nn.silu(h @ wg) * (h @ wu)) @ wd


def moe_swiglu(h, w_router, wg, wu, wd):
    B, S, D = h.shape
    xt = h.reshape(-1, D)
    logits = (xt @ w_router).astype(jnp.float32)
    top_v, top_i = lax.top_k(logits, TOP_K)
    gates = jax.nn.softmax(top_v, axis=-1)
    gate = jnp.sum(jax.nn.one_hot(top_i, N_EXPERTS, dtype=jnp.float32) * gates[..., None], axis=1)
    y = jnp.zeros_like(xt)
    for e in range(N_EXPERTS):
        y = y + gate[:, e:e + 1].astype(xt.dtype) * swiglu(xt, wg[e], wu[e], wd[e])
    return y.reshape(B, S, D)


def setup_inputs(seed: int = 0) -> dict:
    key = jax.random.key(seed)
    ks = jax.random.split(key, 20)
    f32 = jnp.float32
    nrm = lambda k, shape, fan_in: jax.random.normal(k, shape, f32) * (fan_in ** -0.5)
    gain = lambda k, shape: 1.0 + 0.05 * jax.random.normal(k, shape, f32)
    return {
        "x": jax.random.normal(ks[0], (BATCH, SEQ, D_MODEL), f32),
        "norm_mix": gain(ks[1], (DEPTH, D_MODEL)),
        "w_in": nrm(ks[2], (DEPTH, D_MODEL, D_IN), D_MODEL),
        "w_pool": nrm(ks[3], (DEPTH, N_POOL_GROUPS, POOL_GROUP, POOL_GROUP), POOL_GROUP),
        "pool_scale": gain(ks[4], (DEPTH, D_POOL)),
        "attn_gain": gain(ks[5], (DEPTH, D_ATTN)),
        "w_out": nrm(ks[6], (DEPTH, D_MODEL, D_MODEL), D_MODEL),
        "norm_ffn": gain(ks[7], (DEPTH, D_MODEL)),
        "ffn_wg": nrm(ks[8], (N_DENSE, D_MODEL, D_FF), D_MODEL),
        "ffn_wu": nrm(ks[9], (N_DENSE, D_MODEL, D_FF), D_MODEL),
        "ffn_wd": nrm(ks[10], (N_DENSE, D_FF, D_MODEL), D_FF),
        "w_router": nrm(ks[11], (N_MOE, D_MODEL, N_EXPERTS), D_MODEL),
        "moe_wg": nrm(ks[12], (N_MOE, N_EXPERTS, D_MODEL, D_FF_EXPERT), D_MODEL),
        "moe_wu": nrm(ks[13], (N_MOE, N_EXPERTS, D_MODEL, D_FF_EXPERT), D_MODEL),
        "moe_wd": nrm(ks[14], (N_MOE, N_EXPERTS, D_FF_EXPERT, D_MODEL), D_FF_EXPERT),
        "final_norm": gain(ks[15], (D_MODEL,)),
    }


def reference(x, norm_mix, w_in, w_pool, pool_scale, attn_gain, w_out, norm_ffn,
              ffn_wg, ffn_wu, ffn_wd, w_router, moe_wg, moe_wu, moe_wd, final_norm):
    for l in range(DEPTH):
        h = rmsnorm(x, norm_mix[l])
        x = x + hybrid_mixer(h, w_in[l], w_pool[l], pool_scale[l], attn_gain[l], w_out[l])
        h = rmsnorm(x, norm_ffn[l])
        if l % 2 == 0:
            i = l // 2
            x = x + swiglu(h, ffn_wg[i], ffn_wu[i], ffn_wd[i])
        else:
            i = l // 2
            x = x + moe_swiglu(h, w_router[i], moe_wg[i], moe_wu[i], moe_wd[i])
    return rmsnorm(x, final_norm)
```

```python
import functools

import jax
import jax.numpy as jnp
from jax import lax
from jax.experimental import pallas as pl
from jax.experimental.pallas import tpu as pltpu

F32 = jnp.float32
BF16 = jnp.bfloat16

EPS = 1e-6
LANES = 128
HEAD_DIM = 64
POOL_WINDOWS = (2, 4, 8, 16)
POOL_HALO = 16
DILATED_PATTERNS = ((128, 1), (512, 4), (2048, 16))
Q_BLOCK = 128
TOP_K = 2
MASKED = -1e30
VMEM_LIMIT = 48 * 1024 * 1024


def _rms(x, g):
    return x * lax.rsqrt(jnp.mean(x * x, axis=-1, keepdims=True) + EPS) * g


def _params(*sem):
    return pltpu.CompilerParams(dimension_semantics=sem, vmem_limit_bytes=VMEM_LIMIT)


def _mix_in_kernel(x_ref, g_ref, w_ref, wp_ref, ps_ref, ya_ref, q_ref, k_ref, v_ref, ubuf,
                   *, d_pool, d_attn, ts):
    s = pl.program_id(1)
    h = _rms(x_ref[0], g_ref[...]).astype(BF16)
    proj = jnp.dot(h, w_ref[...], preferred_element_type=F32)
    u = proj[:, :d_pool]
    scale = HEAD_DIM ** -0.5
    q_ref[0] = (proj[:, d_pool:d_pool + d_attn] * scale).astype(BF16)
    k_ref[0] = proj[:, d_pool + d_attn:d_pool + 2 * d_attn].astype(BF16)
    v_ref[0] = proj[:, d_pool + 2 * d_attn:].astype(BF16)

    @pl.when(s == 0)
    def _():
        ubuf[0:POOL_HALO, :] = jnp.zeros((POOL_HALO, d_pool), F32)

    ubuf[POOL_HALO:POOL_HALO + ts, :] = u
    pos = s * ts + lax.broadcasted_iota(jnp.int32, (ts, 1), 0)
    group = d_pool // len(POOL_WINDOWS)
    zs = []
    for gi, w in enumerate(POOL_WINDOWS):
        cols = slice(gi * group, (gi + 1) * group)
        ug = u[:, cols]
        win = ug
        for back in range(1, w):
            win = win + ubuf[POOL_HALO - back:POOL_HALO - back + ts, cols]
        cnt = jnp.minimum(pos + 1, w).astype(F32)
        d = win / cnt - ug
        zs.append(jnp.dot(d.astype(BF16), wp_ref[gi], preferred_element_type=F32))
    z = jnp.concatenate(zs, axis=-1)
    ya_ref[0] = _rms(z, ps_ref[...]).astype(BF16)
    ubuf[0:POOL_HALO, :] = ubuf[ts:ts + POOL_HALO, :]


def _mix_in(x, g, w_in, w_pool, pool_scale, *, ts=512):
    B, S, D = x.shape
    d_pool = pool_scale.shape[-1]
    d_in = w_in.shape[-1]
    d_attn = (d_in - d_pool) // 3
    ts = min(ts, S)
    assert S % ts == 0 and d_pool % (LANES * len(POOL_WINDOWS)) == 0
    seq_spec = lambda c: pl.BlockSpec((1, ts, c), lambda b, s: (b, s, 0))
    full = lambda shape: pl.BlockSpec(shape, lambda b, s: (0,) * len(shape))
    out_sds = lambda c: jax.ShapeDtypeStruct((B, S, c), BF16)
    return pl.pallas_call(
        functools.partial(_mix_in_kernel, d_pool=d_pool, d_attn=d_attn, ts=ts),
        grid=(B, S // ts),
        in_specs=[seq_spec(D), full((1, D)), full((D, d_in)), full(w_pool.shape), full((1, d_pool))],
        out_specs=[seq_spec(d_pool), seq_spec(d_attn), seq_spec(d_attn), seq_spec(d_attn)],
        out_shape=[out_sds(d_pool), out_sds(d_attn), out_sds(d_attn), out_sds(d_attn)],
        scratch_shapes=[pltpu.VMEM((POOL_HALO + ts, d_pool), F32)],
        compiler_params=_params("parallel", "arbitrary"),
        name="mix_in",
    )(x, g.reshape(1, D), w_in, w_pool, pool_scale.reshape(1, d_pool))


def _attn_kernel(q_ref, kp_ref, kc_ref, vp_ref, vc_ref, o_ref, lse_ref, *, d_attn):
    i = pl.program_id(2)
    nq, nk = Q_BLOCK, 2 * Q_BLOCK
    row = lax.broadcasted_iota(jnp.int32, (nq, nk), 0)
    col = lax.broadcasted_iota(jnp.int32, (nq, nk), 1)
    rel = col - row
    valid = (rel >= 0) & (rel <= nq) & ((col >= nq) | (i > 0))
    bias = jnp.where(valid, 0.0, MASKED).astype(F32)
    lane = lax.broadcasted_iota(jnp.int32, (nq, LANES), 1)
    lo_half = lane < HEAD_DIM
    lse_tile = jnp.zeros((nq, LANES), F32)
    for hp in range(d_attn // LANES):
        cols = slice(hp * LANES, (hp + 1) * LANES)
        qp = q_ref[0, :, cols]
        kk = jnp.concatenate([kp_ref[0, :, cols], kc_ref[0, :, cols]], axis=0)
        vv = jnp.concatenate([vp_ref[0, :, cols], vc_ref[0, :, cols]], axis=0)
        outs = []
        for sub in range(2):
            keep = lo_half if sub == 0 else jnp.logical_not(lo_half)
            qm = jnp.where(keep, qp, jnp.zeros_like(qp))
            s = lax.dot_general(qm, kk, (((1,), (1,)), ((), ())), preferred_element_type=F32) + bias
            m = jnp.max(s, axis=-1, keepdims=True)
            p = jnp.exp(s - m)
            l = jnp.sum(p, axis=-1, keepdims=True)
            o = jnp.dot(p.astype(BF16), vv, preferred_element_type=F32)
            outs.append(o / l)
            lse_tile = jnp.where(lane == 2 * hp + sub, m + jnp.log(l), lse_tile)
        o_ref[0, :, cols] = jnp.where(lo_half, outs[0], outs[1]).astype(BF16)
    lse_ref[0] = lse_tile


def _attn_branch(q, k, v, dil):
    B, S, C = q.shape
    Sd = S // dil
    assert S % (dil * Q_BLOCK) == 0 and C % LANES == 0 and C // HEAD_DIM <= LANES
    view = lambda t: t.reshape(B, Sd, dil * C)
    cur = lambda c: pl.BlockSpec((1, Q_BLOCK, c), lambda b, r, i: (b, i, r))
    prev = lambda c: pl.BlockSpec((1, Q_BLOCK, c), lambda b, r, i: (b, jnp.maximum(i - 1, 0), r))
    o, lse = pl.pallas_call(
        functools.partial(_attn_kernel, d_attn=C),
        grid=(B, dil, Sd // Q_BLOCK),
        in_specs=[cur(C), prev(C), cur(C), prev(C), cur(C)],
        out_specs=[cur(C), cur(LANES)],
        out_shape=[jax.ShapeDtypeStruct((B, Sd, dil * C), BF16),
                   jax.ShapeDtypeStruct((B, Sd, dil * LANES), F32)],
        compiler_params=_params("parallel", "parallel", "arbitrary"),
        name=f"attn_d{dil}",
    )(view(q), view(k), view(k), view(v), view(v))
    return o.reshape(B * S, C), lse.reshape(B * S, LANES)


def _mix_out_kernel(x_ref, ya_ref, o1_ref, o2_ref, o3_ref, l1_ref, l2_ref, l3_ref, gain_ref, wo_ref,
                    gffn_ref, exp_ref, x1_ref, h2_ref, *, d_pool):
    lses = [l1_ref[...], l2_ref[...], l3_ref[...]]
    top = jnp.maximum(jnp.maximum(lses[0], lses[1]), lses[2])
    es = [jnp.exp(l - top) for l in lses]
    den = es[0] + es[1] + es[2]
    expand = exp_ref[...]

    def per_lane(w):
        hi = w.astype(BF16)
        lo = (w - hi.astype(F32)).astype(BF16)
        return (jnp.dot(hi, expand, preferred_element_type=F32)
                + jnp.dot(lo, expand, preferred_element_type=F32))

    o = jnp.zeros(o1_ref.shape, F32)
    for e, o_ref in zip(es, (o1_ref, o2_ref, o3_ref)):
        o = o + per_lane(e / den) * o_ref[...].astype(F32)
    yb = _rms(o, gain_ref[...]).astype(BF16)
    y = (jnp.dot(ya_ref[...], wo_ref[:d_pool, :], preferred_element_type=F32)
         + jnp.dot(yb, wo_ref[d_pool:, :], preferred_element_type=F32))
    x1 = x_ref[...] + y
    x1_ref[...] = x1
    h2_ref[...] = _rms(x1, gffn_ref[...]).astype(BF16)


def _mix_out(x, ya, outs, lses, attn_gain, w_out, g_ffn, *, tm=512):
    N, D = x.shape
    d_pool = ya.shape[-1]
    d_attn = outs[0].shape[-1]
    tm = min(tm, N)
    assert N % tm == 0
    head_of_lane = jnp.arange(d_attn) // HEAD_DIM
    expand = (jnp.arange(LANES)[:, None] == head_of_lane[None, :]).astype(BF16)
    row = lambda c: pl.BlockSpec((tm, c), lambda i: (i, 0))
    full = lambda shape: pl.BlockSpec(shape, lambda i: (0,) * len(shape))
    return pl.pallas_call(
        functools.partial(_mix_out_kernel, d_pool=d_pool),
        grid=(N // tm,),
        in_specs=[row(D), row(d_pool)] + [row(d_attn)] * 3 + [row(LANES)] * 3
                 + [full((1, d_attn)), full(w_out.shape), full((1, D)), full(expand.shape)],
        out_specs=[row(D), row(D)],
        out_shape=[jax.ShapeDtypeStruct((N, D), F32), jax.ShapeDtypeStruct((N, D), BF16)],
        compiler_params=_params("parallel"),
        name="mix_out",
    )(x, ya, *outs, *lses, attn_gain.reshape(1, d_attn), w_out, g_ffn.reshape(1, D), expand)


def _swiglu_hidden(h, wg, wu):
    a = jnp.dot(h, wg, preferred_element_type=F32)
    b = jnp.dot(h, wu, preferred_element_type=F32)
    return (a * jax.nn.sigmoid(a) * b).astype(BF16)


def _finish(x, y, gfin_ref):
    out = x + y
    return out if gfin_ref is None else _rms(out, gfin_ref[...])


def _ffn_kernel(*refs, final):
    if final:
        h_ref, x_ref, wg_ref, wu_ref, wd_ref, gfin_ref, o_ref, acc = refs
    else:
        (h_ref, x_ref, wg_ref, wu_ref, wd_ref, o_ref, acc), gfin_ref = refs, None
    f = pl.program_id(1)

    @pl.when(f == 0)
    def _():
        acc[...] = jnp.zeros_like(acc)

    hid = _swiglu_hidden(h_ref[...], wg_ref[...], wu_ref[...])
    acc[...] += jnp.dot(hid, wd_ref[...], preferred_element_type=F32)

    @pl.when(f == pl.num_programs(1) - 1)
    def _():
        o_ref[...] = _finish(x_ref[...], acc[...], gfin_ref)


def _pick_chunk(n, target):
    best = None
    for c in range(LANES, min(n, target) + 1, LANES):
        if n % c == 0:
            best = c
    assert best is not None
    return best


def _ffn(h, x, wg, wu, wd, g_final, *, tm=512, tf_target=1536):
    N, D = x.shape
    F = wg.shape[-1]
    tm = min(tm, N)
    tf = _pick_chunk(F, tf_target)
    final = g_final is not None
    row = pl.BlockSpec((tm, D), lambda i, f: (i, 0))
    in_specs = [row, row,
                pl.BlockSpec((D, tf), lambda i, f: (0, f)),
                pl.BlockSpec((D, tf), lambda i, f: (0, f)),
                pl.BlockSpec((tf, D), lambda i, f: (f, 0))]
    args = [h, x, wg, wu, wd]
    if final:
        in_specs.append(pl.BlockSpec((1, D), lambda i, f: (0, 0)))
        args.append(g_final.reshape(1, D))
    return pl.pallas_call(
        functools.partial(_ffn_kernel, final=final),
        grid=(N // tm, F // tf),
        in_specs=in_specs,
        out_specs=row,
        out_shape=jax.ShapeDtypeStruct((N, D), F32),
        scratch_shapes=[pltpu.VMEM((tm, D), F32)],
        compiler_params=_params("parallel", "arbitrary"),
        name="ffn_dense",
    )(*args)


def _router_kernel(x_ref, g_ref, wr_ref, gate_ref, *, n_experts):
    h = _rms(x_ref[...], g_ref[...])
    logits = jnp.dot(h, wr_ref[...], preferred_element_type=F32, precision=lax.Precision.HIGHEST)
    lane = lax.broadcasted_iota(jnp.int32, logits.shape, 1)
    logits = jnp.where(lane < n_experts, logits, -jnp.inf)
    picks = []
    for _ in range(TOP_K):
        m = jnp.max(logits, axis=-1, keepdims=True)
        idx = jnp.min(jnp.where(logits == m, lane, LANES), axis=-1, keepdims=True)
        picks.append((m, idx))
        logits = jnp.where(lane == idx, -jnp.inf, logits)
    (m1, i1), (m2, i2) = picks
    e2 = jnp.exp(m2 - m1)
    g1 = 1.0 / (1.0 + e2)
    g2 = e2 / (1.0 + e2)
    gate_ref[...] = jnp.where(lane == i1, g1, jnp.where(lane == i2, g2, 0.0))


def _router(x, g_ffn, w_router, *, tm=512):
    N, D = x.shape
    E = w_router.shape[-1]
    assert E <= LANES and TOP_K == 2
    tm = min(tm, N)
    wr = jnp.zeros((D, LANES), F32).at[:, :E].set(w_router)
    return pl.pallas_call(
        functools.partial(_router_kernel, n_experts=E),
        grid=(N // tm,),
        in_specs=[pl.BlockSpec((tm, D), lambda i: (i, 0)),
                  pl.BlockSpec((1, D), lambda i: (0, 0)),
                  pl.BlockSpec((D, LANES), lambda i: (0, 0))],
        out_specs=pl.BlockSpec((tm, LANES), lambda i: (i, 0)),
        out_shape=jax.ShapeDtypeStruct((N, LANES), F32),
        compiler_params=_params("parallel"),
        name="router",
    )(x, g_ffn.reshape(1, D), wr)


def _moe_kernel(*refs, final):
    if final:
        h_ref, x_ref, gate_ref, wg_ref, wu_ref, wd_ref, gfin_ref, o_ref, acc = refs
    else:
        (h_ref, x_ref, gate_ref, wg_ref, wu_ref, wd_ref, o_ref, acc), gfin_ref = refs, None
    e, f = pl.program_id(1), pl.program_id(2)

    @pl.when(jnp.logical_and(e == 0, f == 0))
    def _():
        acc[...] = jnp.zeros_like(acc)

    hid = _swiglu_hidden(h_ref[...], wg_ref[0], wu_ref[0])
    gates = gate_ref[...]
    lane = lax.broadcasted_iota(jnp.int32, gates.shape, 1)
    gate = jnp.sum(jnp.where(lane == e, gates, 0.0), axis=-1, keepdims=True)
    acc[...] += gate * jnp.dot(hid, wd_ref[0], preferred_element_type=F32)

    @pl.when(jnp.logical_and(e == pl.num_programs(1) - 1, f == pl.num_programs(2) - 1))
    def _():
        o_ref[...] = _finish(x_ref[...], acc[...], gfin_ref)


def _moe(h, x, gates, wg, wu, wd, g_final, *, tm=512, tf_target=1024):
    N, D = x.shape
    E, _, F = wg.shape
    tm = min(tm, N)
    tf = _pick_chunk(F, tf_target)
    final = g_final is not None
    row = lambda c: pl.BlockSpec((tm, c), lambda i, e, f: (i, 0))
    in_specs = [row(D), row(D), row(LANES),
                pl.BlockSpec((1, D, tf), lambda i, e, f: (e, 0, f)),
                pl.BlockSpec((1, D, tf), lambda i, e, f: (e, 0, f)),
                pl.BlockSpec((1, tf, D), lambda i, e, f: (e, f, 0))]
    args = [h, x, gates, wg, wu, wd]
    if final:
        in_specs.append(pl.BlockSpec((1, D), lambda i, e, f: (0, 0)))
        args.append(g_final.reshape(1, D))
    return pl.pallas_call(
        functools.partial(_moe_kernel, final=final),
        grid=(N // tm, E, F // tf),
        in_specs=in_specs,
        out_specs=row(D),
        out_shape=jax.ShapeDtypeStruct((N, D), F32),
        scratch_shapes=[pltpu.VMEM((tm, D), F32)],
        compiler_params=_params("parallel", "arbitrary", "arbitrary"),
        name="moe",
    )(*args)


def kernel(x, norm_mix, w_in, w_pool, pool_scale, attn_gain, w_out, norm_ffn, ffn_wg, ffn_wu, ffn_wd,
           w_router, moe_wg, moe_wu, moe_wd, final_norm):
    B, S, D = x.shape
    depth = norm_mix.shape[0]
    bf = lambda t: t.astype(BF16)
    w_in, w_pool, w_out = bf(w_in), bf(w_pool), bf(w_out)
    ffn_wg, ffn_wu, ffn_wd = bf(ffn_wg), bf(ffn_wu), bf(ffn_wd)
    moe_wg, moe_wu, moe_wd = bf(moe_wg), bf(moe_wu), bf(moe_wd)
    for l in range(depth):
        ya, q, k, v = _mix_in(x, norm_mix[l], w_in[l], w_pool[l], pool_scale[l])
        branches = [_attn_branch(q, k, v, dil) for _, dil in DILATED_PATTERNS]
        x1, h2 = _mix_out(x.reshape(B * S, D), ya.reshape(B * S, -1),
                          [o for o, _ in branches], [lse for _, lse in branches],
                          attn_gain[l], w_out[l], norm_ffn[l])
        g_final = final_norm if l == depth - 1 else None
        i = l // 2
        if l % 2 == 0:
            x2 = _ffn(h2, x1, ffn_wg[i], ffn_wu[i], ffn_wd[i], g_final)
        else:
            gates = _router(x1, norm_ffn[l], w_router[i])
            x2 = _moe(h2, x1, gates, moe_wg[i], moe_wu[i], moe_wd[i], g_final)
        x = x2.reshape(B, S, D)
    return x
```

```python
import functools

import jax
import jax.numpy as jnp
from jax import lax
from jax.experimental import pallas as pl
from jax.experimental.pallas import tpu as pltpu

F32 = jnp.float32
BF16 = jnp.bfloat16

EPS = 1e-6
LANES = 128
HEAD_DIM = 64
POOL_WINDOWS = (2, 4, 8, 16)
POOL_HALO = 16
DILATED_PATTERNS = ((128, 1), (512, 4), (2048, 16))
Q_BLOCK = 128
TOP_K = 2
MASKED = -1e30
VMEM_LIMIT = 48 * 1024 * 1024


def _rms(x, g):
    return x * lax.rsqrt(jnp.mean(x * x, axis=-1, keepdims=True) + EPS) * g


def _params(*sem):
    return pltpu.CompilerParams(dimension_semantics=sem, vmem_limit_bytes=VMEM_LIMIT)


def _mix_in_kernel(x_ref, g_ref, w_ref, wp_ref, ps_ref, ya_ref, q_ref, k_ref, v_ref, ubuf,
                   *, d_pool, d_attn, ts):
    s = pl.program_id(1)
    h = _rms(x_ref[0], g_ref[...]).astype(BF16)
    proj = jnp.dot(h, w_ref[...], preferred_element_type=F32)
    u = proj[:, :d_pool]
    scale = HEAD_DIM ** -0.5
    q_ref[0] = (proj[:, d_pool:d_pool + d_attn] * scale).astype(BF16)
    k_ref[0] = proj[:, d_pool + d_attn:d_pool + 2 * d_attn].astype(BF16)
    v_ref[0] = proj[:, d_pool + 2 * d_attn:].astype(BF16)

    @pl.when(s == 0)
    def _():
        ubuf[0:POOL_HALO, :] = jnp.zeros((POOL_HALO, d_pool), F32)

    ubuf[POOL_HALO:POOL_HALO + ts, :] = u
    pos = s * ts + lax.broadcasted_iota(jnp.int32, (ts, 1), 0)
    group = d_pool // len(POOL_WINDOWS)
    zs = []
    for gi, w in enumerate(POOL_WINDOWS):
        cols = slice(gi * group, (gi + 1) * group)
        ug = u[:, cols]
        win = ug
        for back in range(1, w):
            win = win + ubuf[POOL_HALO - back:POOL_HALO - back + ts, cols]
        cnt = jnp.minimum(pos + 1, w).astype(F32)
        d = win / cnt - ug
        zs.append(jnp.dot(d.astype(BF16), wp_ref[gi], preferred_element_type=F32))
    z = jnp.concatenate(zs, axis=-1)
    ya_ref[0] = _rms(z, ps_ref[...]).astype(BF16)
    ubuf[0:POOL_HALO, :] = ubuf[ts:ts + POOL_HALO, :]


def _mix_in(x, g, w_in, w_pool, pool_scale, *, ts=512):
    B, S, D = x.shape
    d_pool = pool_scale.shape[-1]
    d_in = w_in.shape[-1]
    d_attn = (d_in - d_pool) // 3
    ts = min(ts, S)
    assert S % ts == 0 and d_pool % (LANES * len(POOL_WINDOWS)) == 0
    seq_spec = lambda c: pl.BlockSpec((1, ts, c), lambda b, s: (b, s, 0))
    full = lambda shape: pl.BlockSpec(shape, lambda b, s: (0,) * len(shape))
    out_sds = lambda c: jax.ShapeDtypeStruct((B, S, c), BF16)
    return pl.pallas_call(
        functools.partial(_mix_in_kernel, d_pool=d_pool, d_attn=d_attn, ts=ts),
        grid=(B, S // ts),
        in_specs=[seq_spec(D), full((1, D)), full((D, d_in)), full(w_pool.shape), full((1, d_pool))],
        out_specs=[seq_spec(d_pool), seq_spec(d_attn), seq_spec(d_attn), seq_spec(d_attn)],
        out_shape=[out_sds(d_pool), out_sds(d_attn), out_sds(d_attn), out_sds(d_attn)],
        scratch_shapes=[pltpu.VMEM((POOL_HALO + ts, d_pool), F32)],
        compiler_params=_params("parallel", "arbitrary"),
        name="mix_in",
    )(x, g.reshape(1, D), w_in, w_pool, pool_scale.reshape(1, d_pool))


def _attn_kernel(q_ref, kp_ref, kc_ref, vp_ref, vc_ref, o_ref, lse_ref, *, d_attn):
    i = pl.program_id(2)
    nq, nk = Q_BLOCK, 2 * Q_BLOCK
    row = lax.broadcasted_iota(jnp.int32, (nq, nk), 0)
    col = lax.broadcasted_iota(jnp.int32, (nq, nk), 1)
    rel = col - row
    valid = (rel >= 0) & (rel <= nq) & ((col >= nq) | (i > 0))
    bias = jnp.where(valid, 0.0, MASKED).astype(F32)
    lane = lax.broadcasted_iota(jnp.int32, (nq, LANES), 1)
    lo_half = lane < HEAD_DIM
    lse_tile = jnp.zeros((nq, LANES), F32)
    for hp in range(d_attn // LANES):
        cols = slice(hp * LANES, (hp + 1) * LANES)
        qp = q_ref[0, :, cols]
        kk = jnp.concatenate([kp_ref[0, :, cols], kc_ref[0, :, cols]], axis=0)
        vv = jnp.concatenate([vp_ref[0, :, cols], vc_ref[0, :, cols]], axis=0)
        outs = []
        for sub in range(2):
            keep = lo_half if sub == 0 else jnp.logical_not(lo_half)
            qm = jnp.where(keep, qp, jnp.zeros_like(qp))
            s = lax.dot_general(qm, kk, (((1,), (1,)), ((), ())), preferred_element_type=F32) + bias
            m = jnp.max(s, axis=-1, keepdims=True)
            p = jnp.exp(s - m)
            l = jnp.sum(p, axis=-1, keepdims=True)
            o = jnp.dot(p.astype(BF16), vv, preferred_element_type=F32)
            outs.append(o / l)
            lse_tile = jnp.where(lane == 2 * hp + sub, m + jnp.log(l), lse_tile)
        o_ref[0, :, cols] = jnp.where(lo_half, outs[0], outs[1]).astype(BF16)
    lse_ref[0] = lse_tile


def _attn_branch(q, k, v, dil):
    B, S, C = q.shape
    Sd = S // dil
    assert S % (dil * Q_BLOCK) == 0 and C % LANES == 0 and C // HEAD_DIM <= LANES
    view = lambda t: t.reshape(B, Sd, dil * C)
    cur = lambda c: pl.BlockSpec((1, Q_BLOCK, c), lambda b, r, i: (b, i, r))
    prev = lambda c: pl.BlockSpec((1, Q_BLOCK, c), lambda b, r, i: (b, jnp.maximum(i - 1, 0), r))
    o, lse = pl.pallas_call(
        functools.partial(_attn_kernel, d_attn=C),
        grid=(B, dil, Sd // Q_BLOCK),
        in_specs=[cur(C), prev(C), cur(C), prev(C), cur(C)],
        out_specs=[cur(C), cur(LANES)],
        out_shape=[jax.ShapeDtypeStruct((B, Sd, dil * C), BF16),
                   jax.ShapeDtypeStruct((B, Sd, dil * LANES), F32)],
        compiler_params=_params("parallel", "parallel", "arbitrary"),
        name=f"attn_d{dil}",
    )(view(q), view(k), view(k), view(v), view(v))
    return o.reshape(B * S, C), lse.reshape(B * S, LANES)


def _mix_out_kernel(x_ref, ya_ref, o1_ref, o2_ref, o3_ref, l1_ref, l2_ref, l3_ref, gain_ref, wo_ref,
                    gffn_ref, exp_ref, x1_ref, h2_ref, *, d_pool):
    lses = [l1_ref[...], l2_ref[...], l3_ref[...]]
    top = jnp.maximum(jnp.maximum(lses[0], lses[1]), lses[2])
    es = [jnp.exp(l - top) for l in lses]
    den = es[0] + es[1] + es[2]
    expand = exp_ref[...]

    def per_lane(w):
        hi = w.astype(BF16)
        lo = (w - hi.astype(F32)).astype(BF16)
        return (jnp.dot(hi, expand, preferred_element_type=F32)
                + jnp.dot(lo, expand, preferred_element_type=F32))

    o = jnp.zeros(o1_ref.shape, F32)
    for e, o_ref in zip(es, (o1_ref, o2_ref, o3_ref)):
        o = o + per_lane(e / den) * o_ref[...].astype(F32)
    yb = _rms(o, gain_ref[...]).astype(BF16)
    y = (jnp.dot(ya_ref[...], wo_ref[:d_pool, :], preferred_element_type=F32)
         + jnp.dot(yb, wo_ref[d_pool:, :], preferred_element_type=F32))
    x1 = x_ref[...] + y
    x1_ref[...] = x1
    h2_ref[...] = _rms(x1, gffn_ref[...]).astype(BF16)


def _mix_out(x, ya, outs, lses, attn_gain, w_out, g_ffn, *, tm=512):
    N, D = x.shape
    d_pool = ya.shape[-1]
    d_attn = outs[0].shape[-1]
    tm = min(tm, N)
    assert N % tm == 0
    head_of_lane = jnp.arange(d_attn) // HEAD_DIM
    expand = (jnp.arange(LANES)[:, None] == head_of_lane[None, :]).astype(BF16)
    row = lambda c: pl.BlockSpec((tm, c), lambda i: (i, 0))
    full = lambda shape: pl.BlockSpec(shape, lambda i: (0,) * len(shape))
    return pl.pallas_call(
        functools.partial(_mix_out_kernel, d_pool=d_pool),
        grid=(N // tm,),
        in_specs=[row(D), row(d_pool)] + [row(d_attn)] * 3 + [row(LANES)] * 3
                 + [full((1, d_attn)), full(w_out.shape), full((1, D)), full(expand.shape)],
        out_specs=[row(D), row(D)],
        out_shape=[jax.ShapeDtypeStruct((N, D), F32), jax.ShapeDtypeStruct((N, D), BF16)],
        compiler_params=_params("parallel"),
        name="mix_out",
    )(x, ya, *outs, *lses, attn_gain.reshape(1, d_attn), w_out, g_ffn.reshape(1, D), expand)


def _swiglu_hidden(h, wg, wu):
    a = jnp.dot(h, wg, preferred_element_type=F32)
    b = jnp.dot(h, wu, preferred_element_type=F32)
    return (a * jax.nn.sigmoid(a) * b).astype(BF16)


def _finish(x, y, gfin_ref):
    out = x + y
    return out if gfin_ref is None else _rms(out, gfin_ref[...])


def _ffn_kernel(*refs, final):
    if final:
        h_ref, x_ref, wg_ref, wu_ref, wd_ref, gfin_ref, o_ref, acc = refs
    else:
        (h_ref, x_ref, wg_ref, wu_ref, wd_ref, o_ref, acc), gfin_ref = refs, None
    f = pl.program_id(1)

    @pl.when(f == 0)
    def _():
        acc[...] = jnp.zeros_like(acc)

    hid = _swiglu_hidden(h_ref[...], wg_ref[...], wu_ref[...])
    acc[...] += jnp.dot(hid, wd_ref[...], preferred_element_type=F32)

    @pl.when(f == pl.num_programs(1) - 1)
    def _():
        o_ref[...] = _finish(x_ref[...], acc[...], gfin_ref)


def _pick_chunk(n, target):
    best = None
    for c in range(LANES, min(n, target) + 1, LANES):
        if n % c == 0:
            best = c
    assert best is not None
    return best


def _ffn(h, x, wg, wu, wd, g_final, *, tm=512, tf_target=1536):
    N, D = x.shape
    F = wg.shape[-1]
    tm = min(tm, N)
    tf = _pick_chunk(F, tf_target)
    final = g_final is not None
    row = pl.BlockSpec((tm, D), lambda i, f: (i, 0))
    in_specs = [row, row,
                pl.BlockSpec((D, tf), lambda i, f: (0, f)),
                pl.BlockSpec((D, tf), lambda i, f: (0, f)),
                pl.BlockSpec((tf, D), lambda i, f: (f, 0))]
    args = [h, x, wg, wu, wd]
    if final:
        in_specs.append(pl.BlockSpec((1, D), lambda i, f: (0, 0)))
        args.append(g_final.reshape(1, D))
    return pl.pallas_call(
        functools.partial(_ffn_kernel, final=final),
        grid=(N // tm, F // tf),
        in_specs=in_specs,
        out_specs=row,
        out_shape=jax.ShapeDtypeStruct((N, D), F32),
        scratch_shapes=[pltpu.VMEM((tm, D), F32)],
        compiler_params=_params("parallel", "arbitrary"),
        name="ffn_dense",
    )(*args)


def _router_kernel(x_ref, g_ref, wr_ref, gate_ref, *, n_experts):
    h = _rms(x_ref[...], g_ref[...])
    logits = jnp.dot(h, wr_ref[...], preferred_element_type=F32, precision=lax.Precision.HIGHEST)
    lane = lax.broadcasted_iota(jnp.int32, logits.shape, 1)
    logits = jnp.where(lane < n_experts, logits, -jnp.inf)
    picks = []
    for _ in range(TOP_K):
        m = jnp.max(logits, axis=-1, keepdims=True)
        idx = jnp.min(jnp.where(logits == m, lane, LANES), axis=-1, keepdims=True)
        picks.append((m, idx))
        logits = jnp.where(lane == idx, -jnp.inf, logits)
    (m1, i1), (m2, i2) = picks
    e2 = jnp.exp(m2 - m1)
    g1 = 1.0 / (1.0 + e2)
    g2 = e2 / (1.0 + e2)
    picked = jnp.logical_or(lane == i1 + n_experts, lane == i2 + n_experts)
    gate_ref[...] = jnp.where(lane == i1, g1, jnp.where(lane == i2, g2, jnp.where(picked, 1.0, 0.0)))


def _router(x, g_ffn, w_router, *, tm=512):
    N, D = x.shape
    E = w_router.shape[-1]
    assert 2 * E <= LANES and TOP_K == 2
    tm = min(tm, N)
    wr = jnp.zeros((D, LANES), F32).at[:, :E].set(w_router)
    return pl.pallas_call(
        functools.partial(_router_kernel, n_experts=E),
        grid=(N // tm,),
        in_specs=[pl.BlockSpec((tm, D), lambda i: (i, 0)),
                  pl.BlockSpec((1, D), lambda i: (0, 0)),
                  pl.BlockSpec((D, LANES), lambda i: (0, 0))],
        out_specs=pl.BlockSpec((tm, LANES), lambda i: (i, 0)),
        out_shape=jax.ShapeDtypeStruct((N, LANES), F32),
        compiler_params=_params("parallel"),
        name="router",
    )(x, g_ffn.reshape(1, D), wr)


def _plan_routing(route, n_experts, *, tm, rb, ch):
    N = route.shape[0]
    E = n_experts
    i32 = jnp.int32
    n_chunks = N // ch
    p_max = TOP_K * N + E * tm
    n_blocks, n_tiles = p_max // rb, p_max // tm
    n_items = n_blocks + E * n_chunks

    gate = route[:, :E].T
    sel = (route[:, E:2 * E].T > 0.5).astype(i32)
    rank = jnp.cumsum(sel, axis=1) - sel
    chunk_cnt = sel.reshape(E, n_chunks, ch).sum(-1)
    cum = jnp.concatenate([jnp.zeros((E, 1), i32), jnp.cumsum(chunk_cnt, axis=1)], axis=1)
    cnt = cum[:, -1]
    seg_len = (cnt + tm - 1) // tm * tm
    seg_end = jnp.cumsum(seg_len)
    seg_start = seg_end - seg_len
    pos = jnp.where(sel > 0, seg_start[:, None] + rank, -1)
    n_active_tiles = seg_end[-1] // tm
    tile_ids = jnp.arange(n_tiles, dtype=i32)
    tile_expert = jnp.minimum(jnp.sum(seg_end[None, :] <= tile_ids[:, None] * tm, axis=1), E - 1).astype(i32)

    def expand(counts, n_out):
        ends = jnp.cumsum(counts)
        total = ends[-1]
        w = jnp.minimum(jnp.arange(n_out, dtype=i32), total - 1)
        grp = jnp.sum(ends[None, :] <= w[:, None], axis=1).astype(i32)
        off = w - (ends[grp] - counts[grp])
        return grp, off, jnp.arange(n_out, dtype=i32) < total, w

    blk = jnp.arange(n_blocks, dtype=i32)
    b_exp = tile_expert[blk * rb // tm]
    lo = blk * rb - seg_start[b_exp]
    hi = jnp.minimum(lo + rb, cnt[b_exp])
    nonempty = lo < cnt[b_exp]
    c_lo = jnp.sum(cum[b_exp, 1:] <= lo[:, None], axis=1)
    c_hi = jnp.sum(cum[b_exp, :-1] < hi[:, None], axis=1) - 1
    b_items = jnp.where(blk * rb < n_active_tiles * tm, jnp.where(nonempty, c_hi - c_lo + 1, 1), 0).astype(i32)
    c_lo = jnp.where(nonempty, c_lo, 0)
    g_blk, g_off, g_valid, _ = expand(b_items, n_items)
    g_first = jnp.logical_and(g_valid, g_off == 0)
    g_last = jnp.logical_and(g_valid, g_off == b_items[g_blk] - 1)
    gather_plan = jnp.stack([g_blk, c_lo[g_blk] + g_off, b_exp[g_blk],
                             g_first, g_last, g_valid]).astype(i32)

    first_row = seg_start[None, :] + cum[:, :-1].T
    last_row = seg_start[None, :] + cum[:, 1:].T - 1
    ce_items = jnp.where(chunk_cnt.T > 0, last_row // rb - first_row // rb + 1, 0).reshape(-1).astype(i32)
    c_grp, c_off, c_valid, c_w = expand(ce_items, n_items)
    c_chunk = c_grp // E
    c_blk = (first_row // rb).reshape(-1)[c_grp] + c_off
    per_chunk = ce_items.reshape(n_chunks, E).sum(1)
    chunk_end = jnp.cumsum(per_chunk)
    c_first = jnp.logical_and(c_valid, c_w == (chunk_end - per_chunk)[c_chunk])
    c_last = jnp.logical_and(c_valid, c_w == chunk_end[c_chunk] - 1)
    combine_plan = jnp.stack([c_blk, c_chunk, c_grp % E, c_first, c_last, c_valid]).astype(i32)

    token_info = jnp.concatenate([pos.T.astype(F32), gate.T, jnp.zeros((N, LANES - 2 * E), F32)], axis=1)
    return dict(pos=pos.reshape(E, 1, N), token_info=token_info, gather_plan=gather_plan,
                combine_plan=combine_plan, tile_expert=tile_expert,
                n_active_tiles=n_active_tiles.reshape(1).astype(i32), p_max=p_max, n_items=n_items)


PLAN_BLOCK, PLAN_CHUNK, PLAN_EXPERT, PLAN_FIRST, PLAN_LAST, PLAN_VALID = range(6)


def _moe_gather_kernel(plan_ref, pos_ref, h_ref, xs_ref, acc, *, rb):
    w = pl.program_id(0)

    @pl.when(plan_ref[PLAN_FIRST, w] == 1)
    def _():
        acc[...] = jnp.zeros_like(acc)

    @pl.when(plan_ref[PLAN_VALID, w] == 1)
    def _():
        ch = h_ref.shape[0]
        rows = plan_ref[PLAN_BLOCK, w] * rb + lax.broadcasted_iota(jnp.int32, (rb, ch), 0)
        onehot = jnp.where(pos_ref[0] == rows, 1.0, 0.0).astype(BF16)
        acc[...] += jnp.dot(onehot, h_ref[...], preferred_element_type=F32)

    @pl.when(plan_ref[PLAN_LAST, w] == 1)
    def _():
        xs_ref[...] = acc[...].astype(BF16)


def _moe_gather(h, plan, *, rb, ch):
    N, D = h.shape
    grid_spec = pltpu.PrefetchScalarGridSpec(
        num_scalar_prefetch=1,
        grid=(plan["n_items"],),
        in_specs=[pl.BlockSpec((1, 1, ch), lambda w, p: (p[PLAN_EXPERT, w], 0, p[PLAN_CHUNK, w])),
                  pl.BlockSpec((ch, D), lambda w, p: (p[PLAN_CHUNK, w], 0))],
        out_specs=pl.BlockSpec((rb, D), lambda w, p: (p[PLAN_BLOCK, w], 0)),
        scratch_shapes=[pltpu.VMEM((rb, D), F32)],
    )
    return pl.pallas_call(
        functools.partial(_moe_gather_kernel, rb=rb),
        grid_spec=grid_spec,
        out_shape=jax.ShapeDtypeStruct((plan["p_max"], D), BF16),
        compiler_params=_params("arbitrary"),
        name="moe_gather",
    )(plan["gather_plan"], plan["pos"], h)


def _moe_experts_kernel(te_ref, na_ref, xs_ref, wg_ref, wu_ref, wd_ref, o_ref, acc, *, tf):
    @pl.when(pl.program_id(0) < na_ref[0])
    def _():
        x = xs_ref[...]
        for c in range(wg_ref.shape[-1] // tf):
            cols = slice(c * tf, (c + 1) * tf)
            hid = _swiglu_hidden(x, wg_ref[0, :, cols], wu_ref[0, :, cols])
            part = jnp.dot(hid, wd_ref[0, cols, :], preferred_element_type=F32)
            if c == 0:
                acc[...] = part
            else:
                acc[...] += part
        o_ref[...] = acc[...].astype(BF16)


def _moe_experts(xs, plan, wg, wu, wd, *, tm, tf_target=512):
    P, D = xs.shape
    E, _, F = wg.shape
    tf = _pick_chunk(F, tf_target)
    tile = lambda i, te, na: (jnp.minimum(i, na[0] - 1), 0)
    resident = pl.Buffered(1)
    grid_spec = pltpu.PrefetchScalarGridSpec(
        num_scalar_prefetch=2,
        grid=(P // tm,),
        in_specs=[pl.BlockSpec((tm, D), tile),
                  pl.BlockSpec((1, D, F), lambda i, te, na: (te[i], 0, 0), pipeline_mode=resident),
                  pl.BlockSpec((1, D, F), lambda i, te, na: (te[i], 0, 0), pipeline_mode=resident),
                  pl.BlockSpec((1, F, D), lambda i, te, na: (te[i], 0, 0), pipeline_mode=resident)],
        out_specs=pl.BlockSpec((tm, D), tile),
        scratch_shapes=[pltpu.VMEM((tm, D), F32)],
    )
    return pl.pallas_call(
        functools.partial(_moe_experts_kernel, tf=tf),
        grid_spec=grid_spec,
        out_shape=jax.ShapeDtypeStruct((P, D), BF16),
        compiler_params=_params("arbitrary"),
        name="moe_experts",
    )(plan["tile_expert"], plan["n_active_tiles"], xs, wg, wu, wd)


def _moe_combine_kernel(*refs, final, rb, n_experts):
    if final:
        plan_ref, info_ref, ys_ref, x_ref, gfin_ref, o_ref, acc = refs
    else:
        (plan_ref, info_ref, ys_ref, x_ref, o_ref, acc), gfin_ref = refs, None
    w = pl.program_id(0)

    @pl.when(plan_ref[PLAN_FIRST, w] == 1)
    def _():
        acc[...] = jnp.zeros_like(acc)

    @pl.when(plan_ref[PLAN_VALID, w] == 1)
    def _():
        info = info_ref[...]
        ch = info.shape[0]
        e = plan_ref[PLAN_EXPERT, w]
        lane = lax.broadcasted_iota(jnp.int32, info.shape, 1)
        pos = jnp.sum(jnp.where(lane == e, info, 0.0), axis=-1, keepdims=True)
        gate = jnp.sum(jnp.where(lane == e + n_experts, info, 0.0), axis=-1, keepdims=True)
        local = pos - (plan_ref[PLAN_BLOCK, w] * rb).astype(F32)
        cols = lax.broadcasted_iota(jnp.int32, (ch, rb), 1).astype(F32)
        onehot = jnp.where(local == cols, 1.0, 0.0).astype(BF16)
        acc[...] += gate * jnp.dot(onehot, ys_ref[...], preferred_element_type=F32)

    @pl.when(plan_ref[PLAN_LAST, w] == 1)
    def _():
        o_ref[...] = _finish(x_ref[...], acc[...], gfin_ref)


def _moe_combine(ys, x, plan, g_final, *, rb, ch, n_experts):
    N, D = x.shape
    final = g_final is not None
    chunk = lambda c: pl.BlockSpec((ch, c), lambda w, p: (p[PLAN_CHUNK, w], 0))
    in_specs = [chunk(LANES), pl.BlockSpec((rb, D), lambda w, p: (p[PLAN_BLOCK, w], 0)), chunk(D)]
    args = [plan["combine_plan"], plan["token_info"], ys, x]
    if final:
        in_specs.append(pl.BlockSpec((1, D), lambda w, p: (0, 0)))
        args.append(g_final.reshape(1, D))
    grid_spec = pltpu.PrefetchScalarGridSpec(
        num_scalar_prefetch=1,
        grid=(plan["n_items"],),
        in_specs=in_specs,
        out_specs=chunk(D),
        scratch_shapes=[pltpu.VMEM((ch, D), F32)],
    )
    return pl.pallas_call(
        functools.partial(_moe_combine_kernel, final=final, rb=rb, n_experts=n_experts),
        grid_spec=grid_spec,
        out_shape=jax.ShapeDtypeStruct((N, D), F32),
        compiler_params=_params("arbitrary"),
        name="moe_combine",
    )(*args)


def _moe(h, x, route, wg, wu, wd, g_final, *, tm=512, rb=256, ch=512):
    N = x.shape[0]
    E = wg.shape[0]
    tm, ch = min(tm, N), min(ch, N)
    rb = min(rb, tm)
    assert N % ch == 0 and tm % rb == 0
    plan = _plan_routing(route, E, tm=tm, rb=rb, ch=ch)
    xs = _moe_gather(h, plan, rb=rb, ch=ch)
    ys = _moe_experts(xs, plan, wg, wu, wd, tm=tm)
    return _moe_combine(ys, x, plan, g_final, rb=rb, ch=ch, n_experts=E)


def kernel(x, norm_mix, w_in, w_pool, pool_scale, attn_gain, w_out, norm_ffn, ffn_wg, ffn_wu, ffn_wd,
           w_router, moe_wg, moe_wu, moe_wd, final_norm):
    B, S, D = x.shape
    depth = norm_mix.shape[0]
    bf = lambda t: t.astype(BF16)
    w_in, w_pool, w_out = bf(w_in), bf(w_pool), bf(w_out)
    ffn_wg, ffn_wu, ffn_wd = bf(ffn_wg), bf(ffn_wu), bf(ffn_wd)
    moe_wg, moe_wu, moe_wd = bf(moe_wg), bf(moe_wu), bf(moe_wd)
    for l in range(depth):
        ya, q, k, v = _mix_in(x, norm_mix[l], w_in[l], w_pool[l], pool_scale[l])
        branches = [_attn_branch(q, k, v, dil) for _, dil in DILATED_PATTERNS]
        x1, h2 = _mix_out(x.reshape(B * S, D), ya.reshape(B * S, -1),
                          [o for o, _ in branches], [lse for _, lse in branches],
                          attn_gain[l], w_out[l], norm_ffn[l])
        g_final = final_norm if l == depth - 1 else None
        i = l // 2
        if l % 2 == 0:
            x2 = _ffn(h2, x1, ffn_wg[i], ffn_wu[i], ffn_wd[i], g_final)
        else:
            route = _router(x1, norm_ffn[l], w_router[i])
            x2 = _moe(h2, x1, route, moe_wg[i], moe_wu[i], moe_wd[i], g_final)
        x = x2.reshape(B, S, D)
    return x
```

```python
import functools

import jax
import jax.numpy as jnp
from jax import lax
from jax.experimental import pallas as pl
from jax.experimental.pallas import tpu as pltpu

F32 = jnp.float32
BF16 = jnp.bfloat16

EPS = 1e-6
LANES = 128
HEAD_DIM = 64
POOL_WINDOWS = (2, 4, 8, 16)
POOL_HIST = 8
DILATED_PATTERNS = ((128, 1), (512, 4), (2048, 16))
WINDOW_STEPS = 128
RESIDUES = 16
assert all(w // d == WINDOW_STEPS and RESIDUES % d == 0 for w, d in DILATED_PATTERNS)
TOP_K = 2
MASKED = -1e30
VMEM_LIMIT = 48 * 1024 * 1024


def _rms(x, g):
    return x * lax.rsqrt(jnp.mean(x * x, axis=-1, keepdims=True) + EPS) * g


def _params(*sem):
    return pltpu.CompilerParams(dimension_semantics=sem, vmem_limit_bytes=VMEM_LIMIT)


def _mix_in_kernel(x_ref, g_ref, w_ref, wp_ref, ps_ref, ya_ref, q_ref, k_ref, v_ref, ubuf, uprev,
                   *, d_pool, d_attn, ti):
    s = pl.program_id(1)
    R, D = RESIDUES, x_ref.shape[-1]
    rows = R * ti
    blk = lambda t: t.reshape(R, ti, t.shape[-1])
    h = _rms(x_ref[...].reshape(rows, D), g_ref[...]).astype(BF16)
    proj = jnp.dot(h, w_ref[...], preferred_element_type=F32)
    scale = HEAD_DIM ** -0.5
    q_ref[...] = blk((proj[:, d_pool:d_pool + d_attn] * scale).astype(BF16))
    k_ref[...] = blk(proj[:, d_pool + d_attn:d_pool + 2 * d_attn].astype(BF16))
    v_ref[...] = blk(proj[:, d_pool + 2 * d_attn:].astype(BF16))

    @pl.when(s == 0)
    def _():
        ubuf[:, 0:POOL_HIST, :] = jnp.zeros((R, POOL_HIST, d_pool), F32)

    ubuf[:, POOL_HIST:POOL_HIST + ti, :] = blk(proj[:, :d_pool])
    uprev[...] = ubuf[:, POOL_HIST - 1:POOL_HIST - 1 + ti, :]
    at_start = (s * ti + lax.broadcasted_iota(jnp.int32, (ti, 1), 0)) == 0
    group = d_pool // len(POOL_WINDOWS)
    zs = []
    for gi, w in enumerate(POOL_WINDOWS):
        cols = slice(gi * group, (gi + 1) * group)
        ds = []
        for r in range(R):
            ug = ubuf[r, POOL_HIST:POOL_HIST + ti, cols]
            win = ug
            for back in range(1, w):
                rr = r - back
                win = win + (ubuf[rr, POOL_HIST:POOL_HIST + ti, cols] if rr >= 0 else uprev[rr + R, :, cols])
            cnt = jnp.where(at_start, float(min(r + 1, w)), float(w))
            ds.append(win / cnt - ug)
        d = jnp.concatenate(ds, axis=0).astype(BF16)
        zs.append(jnp.dot(d, wp_ref[gi], preferred_element_type=F32))
    z = jnp.concatenate(zs, axis=-1)
    ya_ref[...] = blk(_rms(z, ps_ref[...]).astype(BF16))
    ubuf[:, POOL_HIST - 1:POOL_HIST, :] = ubuf[:, POOL_HIST + ti - 1:POOL_HIST + ti, :]


def _mix_in(x, g, w_in, w_pool, pool_scale, *, ti=32):
    B, R, SI, D = x.shape
    d_pool = pool_scale.shape[-1]
    d_in = w_in.shape[-1]
    d_attn = (d_in - d_pool) // 3
    ti = min(ti, SI)
    assert R == RESIDUES >= max(POOL_WINDOWS) and SI % ti == 0 and ti % 16 == 0
    assert d_pool % (LANES * len(POOL_WINDOWS)) == 0
    seq_spec = lambda c: pl.BlockSpec((None, R, ti, c), lambda b, s: (b, 0, s, 0))
    full = lambda shape: pl.BlockSpec(shape, lambda b, s: (0,) * len(shape))
    out_sds = lambda c: jax.ShapeDtypeStruct((B, R, SI, c), BF16)
    return pl.pallas_call(
        functools.partial(_mix_in_kernel, d_pool=d_pool, d_attn=d_attn, ti=ti),
        grid=(B, SI // ti),
        in_specs=[seq_spec(D), full((1, D)), full((D, d_in)), full(w_pool.shape), full((1, d_pool))],
        out_specs=[seq_spec(d_pool), seq_spec(d_attn), seq_spec(d_attn), seq_spec(d_attn)],
        out_shape=[out_sds(d_pool), out_sds(d_attn), out_sds(d_attn), out_sds(d_attn)],
        scratch_shapes=[pltpu.VMEM((R, POOL_HIST + ti, d_pool), F32), pltpu.VMEM((R, ti, d_pool), F32)],
        compiler_params=_params("parallel", "arbitrary"),
        name="mix_in",
    )(x, g.reshape(1, D), w_in, w_pool, pool_scale.reshape(1, d_pool))


def _lane_group(ref, cols):
    t = ref[(slice(None),) * (len(ref.shape) - 1) + (cols,)]
    return t.reshape(-1, t.shape[-1])


def _attn_kernel(q_ref, kp_ref, kc_ref, vp_ref, vc_ref, o_ref, lse_ref, *, nq, strips, seq_axis, d_attn):
    j = pl.program_id(seq_axis)
    nk = 2 * nq
    L = nq // strips
    shift = L.bit_length() - 1
    place = lambda n: strips * (n & (L - 1)) + (n >> shift)
    row = lax.broadcasted_iota(jnp.int32, (nq, 1), 0)
    col = lax.broadcasted_iota(jnp.int32, (1, nk), 1)
    in_cur = col >= nq
    key_pos = place(col & (nq - 1)) - jnp.where(in_cur, 0, nq)
    rel = place(row) - key_pos
    valid = (rel >= 0) & (rel <= WINDOW_STEPS) & (in_cur | (j > 0))
    bias = jnp.where(valid, 0.0, MASKED).astype(F32)
    lane = lax.broadcasted_iota(jnp.int32, (nq, LANES), 1)
    lo_half = lane < HEAD_DIM
    lse_tile = jnp.zeros((nq, LANES), F32)
    for hp in range(d_attn // LANES):
        cols = slice(hp * LANES, (hp + 1) * LANES)
        qp = _lane_group(q_ref, cols)
        kk = jnp.concatenate([_lane_group(kp_ref, cols), _lane_group(kc_ref, cols)], axis=0)
        vv = jnp.concatenate([_lane_group(vp_ref, cols), _lane_group(vc_ref, cols)], axis=0)
        outs = []
        for sub in range(2):
            keep = lo_half if sub == 0 else jnp.logical_not(lo_half)
            qm = jnp.where(keep, qp, jnp.zeros_like(qp))
            s = lax.dot_general(qm, kk, (((1,), (1,)), ((), ())), preferred_element_type=F32) + bias
            m = jnp.max(s, axis=-1, keepdims=True)
            p = jnp.exp(s - m)
            l = jnp.sum(p, axis=-1, keepdims=True)
            o = jnp.dot(p.astype(BF16), vv, preferred_element_type=F32)
            outs.append(o / l)
            lse_tile = jnp.where(lane == 2 * hp + sub, m + jnp.log(l), lse_tile)
        o_pair = jnp.where(lo_half, outs[0], outs[1]).astype(BF16)
        o_ref[(slice(None),) * (len(o_ref.shape) - 1) + (cols,)] = o_pair.reshape(o_ref.shape[:-1] + (LANES,))
    lse_ref[...] = lse_tile.reshape(lse_ref.shape)


def _attn_branch(q, k, v, dil):
    B, R, SI, C = q.shape
    strips = R // dil
    nq = max(WINDOW_STEPS, strips * 16)
    L = nq // strips
    assert R % dil == 0 and SI % L == 0 and C % LANES == 0 and C // HEAD_DIM <= LANES
    assert nq & (nq - 1) == 0 and L & (L - 1) == 0
    if strips == 1:
        grid = (B, R, SI // L)
        block = lambda c: (None, None, L, c)
        cur = lambda b, r, j: (b, r, j, 0)
        prev = lambda b, r, j: (b, r, jnp.maximum(j - 1, 0), 0)
        view = lambda t: t
    elif strips == R:
        grid = (B, SI // L)
        block = lambda c: (None, R, L, c)
        cur = lambda b, j: (b, 0, j, 0)
        prev = lambda b, j: (b, 0, jnp.maximum(j - 1, 0), 0)
        view = lambda t: t
    else:
        grid = (B, dil, SI // L)
        block = lambda c: (None, strips, None, L, c)
        cur = lambda b, r, j: (b, 0, r, j, 0)
        prev = lambda b, r, j: (b, 0, r, jnp.maximum(j - 1, 0), 0)
        view = lambda t: t.reshape(B, strips, dil, SI, t.shape[-1])
    spec = lambda c, index_map: pl.BlockSpec(block(c), index_map)
    o, lse = pl.pallas_call(
        functools.partial(_attn_kernel, nq=nq, strips=strips, seq_axis=len(grid) - 1, d_attn=C),
        grid=grid,
        in_specs=[spec(C, cur), spec(C, prev), spec(C, cur), spec(C, prev), spec(C, cur)],
        out_specs=[spec(C, cur), spec(LANES, cur)],
        out_shape=[jax.ShapeDtypeStruct(view(q).shape, BF16),
                   jax.ShapeDtypeStruct(view(q).shape[:-1] + (LANES,), F32)],
        compiler_params=_params(*(("parallel",) * (len(grid) - 1) + ("arbitrary",))),
        name=f"attn_d{dil}",
    )(view(q), view(k), view(k), view(v), view(v))
    return o.reshape(B * R * SI, C), lse.reshape(B * R * SI, LANES)


def _mix_out_kernel(x_ref, ya_ref, o1_ref, o2_ref, o3_ref, l1_ref, l2_ref, l3_ref, gain_ref, wo_ref,
                    gffn_ref, exp_ref, x1_ref, h2_ref, *, d_pool):
    lses = [l1_ref[...], l2_ref[...], l3_ref[...]]
    top = jnp.maximum(jnp.maximum(lses[0], lses[1]), lses[2])
    es = [jnp.exp(l - top) for l in lses]
    den = es[0] + es[1] + es[2]
    expand = exp_ref[...]

    def per_lane(w):
        hi = w.astype(BF16)
        lo = (w - hi.astype(F32)).astype(BF16)
        return (jnp.dot(hi, expand, preferred_element_type=F32)
                + jnp.dot(lo, expand, preferred_element_type=F32))

    o = jnp.zeros(o1_ref.shape, F32)
    for e, o_ref in zip(es, (o1_ref, o2_ref, o3_ref)):
        o = o + per_lane(e / den) * o_ref[...].astype(F32)
    yb = _rms(o, gain_ref[...]).astype(BF16)
    y = (jnp.dot(ya_ref[...], wo_ref[:d_pool, :], preferred_element_type=F32)
         + jnp.dot(yb, wo_ref[d_pool:, :], preferred_element_type=F32))
    x1 = x_ref[...] + y
    x1_ref[...] = x1
    h2_ref[...] = _rms(x1, gffn_ref[...]).astype(BF16)


def _mix_out(x, ya, outs, lses, attn_gain, w_out, g_ffn, *, tm=512):
    N, D = x.shape
    d_pool = ya.shape[-1]
    d_attn = outs[0].shape[-1]
    tm = min(tm, N)
    assert N % tm == 0
    head_of_lane = jnp.arange(d_attn) // HEAD_DIM
    expand = (jnp.arange(LANES)[:, None] == head_of_lane[None, :]).astype(BF16)
    row = lambda c: pl.BlockSpec((tm, c), lambda i: (i, 0))
    full = lambda shape: pl.BlockSpec(shape, lambda i: (0,) * len(shape))
    return pl.pallas_call(
        functools.partial(_mix_out_kernel, d_pool=d_pool),
        grid=(N // tm,),
        in_specs=[row(D), row(d_pool)] + [row(d_attn)] * 3 + [row(LANES)] * 3
                 + [full((1, d_attn)), full(w_out.shape), full((1, D)), full(expand.shape)],
        out_specs=[row(D), row(D)],
        out_shape=[jax.ShapeDtypeStruct((N, D), F32), jax.ShapeDtypeStruct((N, D), BF16)],
        compiler_params=_params("parallel"),
        name="mix_out",
    )(x, ya, *outs, *lses, attn_gain.reshape(1, d_attn), w_out, g_ffn.reshape(1, D), expand)


def _swiglu_hidden(h, wg, wu):
    a = jnp.dot(h, wg, preferred_element_type=F32)
    b = jnp.dot(h, wu, preferred_element_type=F32)
    return (a * jax.nn.sigmoid(a) * b).astype(BF16)


def _finish(x, y, gfin_ref):
    out = x + y
    return out if gfin_ref is None else _rms(out, gfin_ref[...])


def _ffn_kernel(*refs, final):
    if final:
        h_ref, x_ref, wg_ref, wu_ref, wd_ref, gfin_ref, o_ref, acc = refs
    else:
        (h_ref, x_ref, wg_ref, wu_ref, wd_ref, o_ref, acc), gfin_ref = refs, None
    f = pl.program_id(1)

    @pl.when(f == 0)
    def _():
        acc[...] = jnp.zeros_like(acc)

    hid = _swiglu_hidden(h_ref[...], wg_ref[...], wu_ref[...])
    acc[...] += jnp.dot(hid, wd_ref[...], preferred_element_type=F32)

    @pl.when(f == pl.num_programs(1) - 1)
    def _():
        o_ref[...] = _finish(x_ref[...], acc[...], gfin_ref)


def _pick_chunk(n, target):
    best = None
    for c in range(LANES, min(n, target) + 1, LANES):
        if n % c == 0:
            best = c
    assert best is not None
    return best


def _ffn(h, x, wg, wu, wd, g_final, *, tm=512, tf_target=1536):
    N, D = x.shape
    F = wg.shape[-1]
    tm = min(tm, N)
    tf = _pick_chunk(F, tf_target)
    final = g_final is not None
    row = pl.BlockSpec((tm, D), lambda i, f: (i, 0))
    in_specs = [row, row,
                pl.BlockSpec((D, tf), lambda i, f: (0, f)),
                pl.BlockSpec((D, tf), lambda i, f: (0, f)),
                pl.BlockSpec((tf, D), lambda i, f: (f, 0))]
    args = [h, x, wg, wu, wd]
    if final:
        in_specs.append(pl.BlockSpec((1, D), lambda i, f: (0, 0)))
        args.append(g_final.reshape(1, D))
    return pl.pallas_call(
        functools.partial(_ffn_kernel, final=final),
        grid=(N // tm, F // tf),
        in_specs=in_specs,
        out_specs=row,
        out_shape=jax.ShapeDtypeStruct((N, D), F32),
        scratch_shapes=[pltpu.VMEM((tm, D), F32)],
        compiler_params=_params("parallel", "arbitrary"),
        name="ffn_dense",
    )(*args)


def _router_kernel(x_ref, g_ref, wr_ref, gate_ref, *, n_experts):
    h = _rms(x_ref[...], g_ref[...])
    logits = jnp.dot(h, wr_ref[...], preferred_element_type=F32, precision=lax.Precision.HIGHEST)
    lane = lax.broadcasted_iota(jnp.int32, logits.shape, 1)
    logits = jnp.where(lane < n_experts, logits, -jnp.inf)
    picks = []
    for _ in range(TOP_K):
        m = jnp.max(logits, axis=-1, keepdims=True)
        idx = jnp.min(jnp.where(logits == m, lane, LANES), axis=-1, keepdims=True)
        picks.append((m, idx))
        logits = jnp.where(lane == idx, -jnp.inf, logits)
    (m1, i1), (m2, i2) = picks
    e2 = jnp.exp(m2 - m1)
    g1 = 1.0 / (1.0 + e2)
    g2 = e2 / (1.0 + e2)
    picked = jnp.logical_or(lane == i1 + n_experts, lane == i2 + n_experts)
    gate_ref[...] = jnp.where(lane == i1, g1, jnp.where(lane == i2, g2, jnp.where(picked, 1.0, 0.0)))


def _router(x, g_ffn, w_router, *, tm=512):
    N, D = x.shape
    E = w_router.shape[-1]
    assert 2 * E <= LANES and TOP_K == 2
    tm = min(tm, N)
    wr = jnp.zeros((D, LANES), F32).at[:, :E].set(w_router)
    return pl.pallas_call(
        functools.partial(_router_kernel, n_experts=E),
        grid=(N // tm,),
        in_specs=[pl.BlockSpec((tm, D), lambda i: (i, 0)),
                  pl.BlockSpec((1, D), lambda i: (0, 0)),
                  pl.BlockSpec((D, LANES), lambda i: (0, 0))],
        out_specs=pl.BlockSpec((tm, LANES), lambda i: (i, 0)),
        out_shape=jax.ShapeDtypeStruct((N, LANES), F32),
        compiler_params=_params("parallel"),
        name="router",
    )(x, g_ffn.reshape(1, D), wr)


def _plan_routing(route, n_experts, *, tm, rb, ch):
    N = route.shape[0]
    E = n_experts
    i32 = jnp.int32
    n_chunks = N // ch
    p_max = TOP_K * N + E * tm
    n_blocks, n_tiles = p_max // rb, p_max // tm
    n_items = n_blocks + E * n_chunks

    gate = route[:, :E].T
    sel = (route[:, E:2 * E].T > 0.5).astype(i32)
    rank = jnp.cumsum(sel, axis=1) - sel
    chunk_cnt = sel.reshape(E, n_chunks, ch).sum(-1)
    cum = jnp.concatenate([jnp.zeros((E, 1), i32), jnp.cumsum(chunk_cnt, axis=1)], axis=1)
    cnt = cum[:, -1]
    seg_len = (cnt + tm - 1) // tm * tm
    seg_end = jnp.cumsum(seg_len)
    seg_start = seg_end - seg_len
    pos = jnp.where(sel > 0, seg_start[:, None] + rank, -1)
    n_active_tiles = seg_end[-1] // tm
    tile_ids = jnp.arange(n_tiles, dtype=i32)
    tile_expert = jnp.minimum(jnp.sum(seg_end[None, :] <= tile_ids[:, None] * tm, axis=1), E - 1).astype(i32)

    def expand(counts, n_out):
        ends = jnp.cumsum(counts)
        total = ends[-1]
        w = jnp.minimum(jnp.arange(n_out, dtype=i32), total - 1)
        grp = jnp.sum(ends[None, :] <= w[:, None], axis=1).astype(i32)
        off = w - (ends[grp] - counts[grp])
        return grp, off, jnp.arange(n_out, dtype=i32) < total, w

    blk = jnp.arange(n_blocks, dtype=i32)
    b_exp = tile_expert[blk * rb // tm]
    lo = blk * rb - seg_start[b_exp]
    hi = jnp.minimum(lo + rb, cnt[b_exp])
    nonempty = lo < cnt[b_exp]
    b_cum = jnp.sum(jnp.where(b_exp[:, None, None] == jnp.arange(E, dtype=i32)[None, :, None], cum[None], 0),
                    axis=1)
    c_lo = jnp.sum(b_cum[:, 1:] <= lo[:, None], axis=1)
    c_hi = jnp.sum(b_cum[:, :-1] < hi[:, None], axis=1) - 1
    b_items = jnp.where(blk * rb < n_active_tiles * tm, jnp.where(nonempty, c_hi - c_lo + 1, 1), 0).astype(i32)
    c_lo = jnp.where(nonempty, c_lo, 0)
    g_blk, g_off, g_valid, _ = expand(b_items, n_items)
    g_first = jnp.logical_and(g_valid, g_off == 0)
    g_last = jnp.logical_and(g_valid, g_off == b_items[g_blk] - 1)
    gather_plan = jnp.stack([g_blk, c_lo[g_blk] + g_off, b_exp[g_blk],
                             g_first, g_last, g_valid]).astype(i32)

    first_row = seg_start[None, :] + cum[:, :-1].T
    last_row = seg_start[None, :] + cum[:, 1:].T - 1
    ce_items = jnp.where(chunk_cnt.T > 0, last_row // rb - first_row // rb + 1, 0).reshape(-1).astype(i32)
    c_grp, c_off, c_valid, c_w = expand(ce_items, n_items)
    c_chunk = c_grp // E
    c_blk = (first_row // rb).reshape(-1)[c_grp] + c_off
    per_chunk = ce_items.reshape(n_chunks, E).sum(1)
    chunk_end = jnp.cumsum(per_chunk)
    c_first = jnp.logical_and(c_valid, c_w == (chunk_end - per_chunk)[c_chunk])
    c_last = jnp.logical_and(c_valid, c_w == chunk_end[c_chunk] - 1)
    combine_plan = jnp.stack([c_blk, c_chunk, c_grp % E, c_first, c_last, c_valid]).astype(i32)

    token_info = jnp.concatenate([pos.T.astype(F32), gate.T, jnp.zeros((N, LANES - 2 * E), F32)], axis=1)
    return dict(pos=pos.reshape(E, 1, N), token_info=token_info, gather_plan=gather_plan,
                combine_plan=combine_plan, tile_expert=tile_expert,
                n_active_tiles=n_active_tiles.reshape(1).astype(i32), p_max=p_max, n_items=n_items)


PLAN_BLOCK, PLAN_CHUNK, PLAN_EXPERT, PLAN_FIRST, PLAN_LAST, PLAN_VALID = range(6)


def _moe_gather_kernel(plan_ref, pos_ref, h_ref, xs_ref, acc, *, rb):
    w = pl.program_id(0)

    @pl.when(plan_ref[PLAN_FIRST, w] == 1)
    def _():
        acc[...] = jnp.zeros_like(acc)

    @pl.when(plan_ref[PLAN_VALID, w] == 1)
    def _():
        ch = h_ref.shape[0]
        rows = plan_ref[PLAN_BLOCK, w] * rb + lax.broadcasted_iota(jnp.int32, (rb, ch), 0)
        onehot = jnp.where(pos_ref[0] == rows, 1.0, 0.0).astype(BF16)
        acc[...] += jnp.dot(onehot, h_ref[...], preferred_element_type=F32)

    @pl.when(plan_ref[PLAN_LAST, w] == 1)
    def _():
        xs_ref[...] = acc[...].astype(BF16)


def _moe_gather(h, plan, *, rb, ch):
    N, D = h.shape
    grid_spec = pltpu.PrefetchScalarGridSpec(
        num_scalar_prefetch=1,
        grid=(plan["n_items"],),
        in_specs=[pl.BlockSpec((1, 1, ch), lambda w, p: (p[PLAN_EXPERT, w], 0, p[PLAN_CHUNK, w])),
                  pl.BlockSpec((ch, D), lambda w, p: (p[PLAN_CHUNK, w], 0))],
        out_specs=pl.BlockSpec((rb, D), lambda w, p: (p[PLAN_BLOCK, w], 0)),
        scratch_shapes=[pltpu.VMEM((rb, D), F32)],
    )
    return pl.pallas_call(
        functools.partial(_moe_gather_kernel, rb=rb),
        grid_spec=grid_spec,
        out_shape=jax.ShapeDtypeStruct((plan["p_max"], D), BF16),
        compiler_params=_params("arbitrary"),
        name="moe_gather",
    )(plan["gather_plan"], plan["pos"], h)


def _moe_experts_kernel(te_ref, na_ref, xs_ref, wg_ref, wu_ref, wd_ref, o_ref, acc, *, tf):
    @pl.when(pl.program_id(0) < na_ref[0])
    def _():
        x = xs_ref[...]
        for c in range(wg_ref.shape[-1] // tf):
            cols = slice(c * tf, (c + 1) * tf)
            hid = _swiglu_hidden(x, wg_ref[0, :, cols], wu_ref[0, :, cols])
            part = jnp.dot(hid, wd_ref[0, cols, :], preferred_element_type=F32)
            if c == 0:
                acc[...] = part
            else:
                acc[...] += part
        o_ref[...] = acc[...].astype(BF16)


def _moe_experts(xs, plan, wg, wu, wd, *, tm, tf_target=512):
    P, D = xs.shape
    E, _, F = wg.shape
    tf = _pick_chunk(F, tf_target)
    tile = lambda i, te, na: (jnp.minimum(i, na[0] - 1), 0)
    resident = pl.Buffered(1)
    grid_spec = pltpu.PrefetchScalarGridSpec(
        num_scalar_prefetch=2,
        grid=(P // tm,),
        in_specs=[pl.BlockSpec((tm, D), tile),
                  pl.BlockSpec((1, D, F), lambda i, te, na: (te[i], 0, 0), pipeline_mode=resident),
                  pl.BlockSpec((1, D, F), lambda i, te, na: (te[i], 0, 0), pipeline_mode=resident),
                  pl.BlockSpec((1, F, D), lambda i, te, na: (te[i], 0, 0), pipeline_mode=resident)],
        out_specs=pl.BlockSpec((tm, D), tile),
        scratch_shapes=[pltpu.VMEM((tm, D), F32)],
    )
    return pl.pallas_call(
        functools.partial(_moe_experts_kernel, tf=tf),
        grid_spec=grid_spec,
        out_shape=jax.ShapeDtypeStruct((P, D), BF16),
        compiler_params=_params("arbitrary"),
        name="moe_experts",
    )(plan["tile_expert"], plan["n_active_tiles"], xs, wg, wu, wd)


def _moe_combine_kernel(*refs, final, rb, n_experts):
    if final:
        plan_ref, info_ref, ys_ref, x_ref, gfin_ref, o_ref, acc = refs
    else:
        (plan_ref, info_ref, ys_ref, x_ref, o_ref, acc), gfin_ref = refs, None
    w = pl.program_id(0)

    @pl.when(plan_ref[PLAN_FIRST, w] == 1)
    def _():
        acc[...] = jnp.zeros_like(acc)

    @pl.when(plan_ref[PLAN_VALID, w] == 1)
    def _():
        info = info_ref[...]
        ch = info.shape[0]
        e = plan_ref[PLAN_EXPERT, w]
        lane = lax.broadcasted_iota(jnp.int32, info.shape, 1)
        pos = jnp.sum(jnp.where(lane == e, info, 0.0), axis=-1, keepdims=True)
        gate = jnp.sum(jnp.where(lane == e + n_experts, info, 0.0), axis=-1, keepdims=True)
        local = pos - (plan_ref[PLAN_BLOCK, w] * rb).astype(F32)
        cols = lax.broadcasted_iota(jnp.int32, (ch, rb), 1).astype(F32)
        onehot = jnp.where(local == cols, 1.0, 0.0).astype(BF16)
        acc[...] += gate * jnp.dot(onehot, ys_ref[...], preferred_element_type=F32)

    @pl.when(plan_ref[PLAN_LAST, w] == 1)
    def _():
        o_ref[...] = _finish(x_ref[...], acc[...], gfin_ref)


def _moe_combine(ys, x, plan, g_final, *, rb, ch, n_experts):
    N, D = x.shape
    final = g_final is not None
    chunk = lambda c: pl.BlockSpec((ch, c), lambda w, p: (p[PLAN_CHUNK, w], 0))
    in_specs = [chunk(LANES), pl.BlockSpec((rb, D), lambda w, p: (p[PLAN_BLOCK, w], 0)), chunk(D)]
    args = [plan["combine_plan"], plan["token_info"], ys, x]
    if final:
        in_specs.append(pl.BlockSpec((1, D), lambda w, p: (0, 0)))
        args.append(g_final.reshape(1, D))
    grid_spec = pltpu.PrefetchScalarGridSpec(
        num_scalar_prefetch=1,
        grid=(plan["n_items"],),
        in_specs=in_specs,
        out_specs=chunk(D),
        scratch_shapes=[pltpu.VMEM((ch, D), F32)],
    )
    return pl.pallas_call(
        functools.partial(_moe_combine_kernel, final=final, rb=rb, n_experts=n_experts),
        grid_spec=grid_spec,
        out_shape=jax.ShapeDtypeStruct((N, D), F32),
        compiler_params=_params("arbitrary"),
        name="moe_combine",
    )(*args)


def _moe(h, x, route, wg, wu, wd, g_final, *, tm=512, rb=256, ch=512):
    N = x.shape[0]
    E = wg.shape[0]
    tm, ch = min(tm, N), min(ch, N)
    rb = min(rb, tm)
    assert N % ch == 0 and tm % rb == 0
    plan = _plan_routing(route, E, tm=tm, rb=rb, ch=ch)
    xs = _moe_gather(h, plan, rb=rb, ch=ch)
    ys = _moe_experts(xs, plan, wg, wu, wd, tm=tm)
    return _moe_combine(ys, x, plan, g_final, rb=rb, ch=ch, n_experts=E)


def kernel(x, norm_mix, w_in, w_pool, pool_scale, attn_gain, w_out, norm_ffn, ffn_wg, ffn_wu, ffn_wd,
           w_router, moe_wg, moe_wu, moe_wd, final_norm):
    B, S, D = x.shape
    depth = norm_mix.shape[0]
    bf = lambda t: t.astype(BF16)
    w_in, w_pool, w_out = bf(w_in), bf(w_pool), bf(w_out)
    ffn_wg, ffn_wu, ffn_wd = bf(ffn_wg), bf(ffn_wu), bf(ffn_wd)
    moe_wg, moe_wu, moe_wd = bf(moe_wg), bf(moe_wu), bf(moe_wd)
    R = RESIDUES
    assert S % R == 0
    x = x.reshape(B, S // R, R, D).transpose(0, 2, 1, 3)
    for l in range(depth):
        ya, q, k, v = _mix_in(x, norm_mix[l], w_in[l], w_pool[l], pool_scale[l])
        branches = [_attn_branch(q, k, v, dil) for _, dil in DILATED_PATTERNS]
        x1, h2 = _mix_out(x.reshape(B * S, D), ya.reshape(B * S, -1),
                          [o for o, _ in branches], [lse for _, lse in branches],
                          attn_gain[l], w_out[l], norm_ffn[l])
        g_final = final_norm if l == depth - 1 else None
        i = l // 2
        if l % 2 == 0:
            x2 = _ffn(h2, x1, ffn_wg[i], ffn_wu[i], ffn_wd[i], g_final)
        else:
            route = _router(x1, norm_ffn[l], w_router[i])
            x2 = _moe(h2, x1, route, moe_wg[i], moe_wu[i], moe_wd[i], g_final)
        x = x2.reshape(B, R, S // R, D)
    return x.transpose(0, 2, 1, 3).reshape(B, S, D)
```

```python
import functools

import jax
import jax.numpy as jnp
from jax import lax
from jax.experimental import pallas as pl
from jax.experimental.pallas import tpu as pltpu

F32 = jnp.float32
BF16 = jnp.bfloat16

EPS = 1e-6
LANES = 128
HEAD_DIM = 64
POOL_WINDOWS = (2, 4, 8, 16)
POOL_HIST = 8
DILATED_PATTERNS = ((128, 1), (512, 4), (2048, 16))
WINDOW_STEPS = 128
RESIDUES = 16
ATTN_STEP_ROWS = 1024
LOG2_E = 1.4426950408889634
assert all(w // d == WINDOW_STEPS and RESIDUES % d == 0 for w, d in DILATED_PATTERNS)
TOP_K = 2
MASKED = -1e30
VMEM_LIMIT = 48 * 1024 * 1024


def _rms(x, g):
    return x * lax.rsqrt(jnp.mean(x * x, axis=-1, keepdims=True) + EPS) * g


def _params(*sem):
    return pltpu.CompilerParams(dimension_semantics=sem, vmem_limit_bytes=VMEM_LIMIT)


def _mix_in_kernel(x_ref, g_ref, w_ref, wp_ref, ps_ref, ya_ref, q_ref, k_ref, v_ref, ubuf, uprev,
                   *, d_pool, d_attn, ti):
    s = pl.program_id(1)
    R, D = RESIDUES, x_ref.shape[-1]
    rows = R * ti
    blk = lambda t: t.reshape(R, ti, t.shape[-1])
    h = _rms(x_ref[...].reshape(rows, D), g_ref[...]).astype(BF16)
    proj = jnp.dot(h, w_ref[...], preferred_element_type=F32)
    scale = HEAD_DIM ** -0.5 * LOG2_E
    q_ref[...] = blk((proj[:, d_pool:d_pool + d_attn] * scale).astype(BF16))
    k_ref[...] = blk(proj[:, d_pool + d_attn:d_pool + 2 * d_attn].astype(BF16))
    v_ref[...] = blk(proj[:, d_pool + 2 * d_attn:].astype(BF16))

    @pl.when(s == 0)
    def _():
        ubuf[:, 0:POOL_HIST, :] = jnp.zeros((R, POOL_HIST, d_pool), F32)

    ubuf[:, POOL_HIST:POOL_HIST + ti, :] = blk(proj[:, :d_pool])
    uprev[...] = ubuf[:, POOL_HIST - 1:POOL_HIST - 1 + ti, :]
    at_start = (s * ti + lax.broadcasted_iota(jnp.int32, (ti, 1), 0)) == 0
    group = d_pool // len(POOL_WINDOWS)
    zs = []
    for gi, w in enumerate(POOL_WINDOWS):
        cols = slice(gi * group, (gi + 1) * group)
        ds = []
        for r in range(R):
            ug = ubuf[r, POOL_HIST:POOL_HIST + ti, cols]
            win = ug
            for back in range(1, w):
                rr = r - back
                win = win + (ubuf[rr, POOL_HIST:POOL_HIST + ti, cols] if rr >= 0 else uprev[rr + R, :, cols])
            cnt = jnp.where(at_start, float(min(r + 1, w)), float(w))
            ds.append(win / cnt - ug)
        d = jnp.concatenate(ds, axis=0).astype(BF16)
        zs.append(jnp.dot(d, wp_ref[gi], preferred_element_type=F32))
    z = jnp.concatenate(zs, axis=-1)
    ya_ref[...] = blk(_rms(z, ps_ref[...]).astype(BF16))
    ubuf[:, POOL_HIST - 1:POOL_HIST, :] = ubuf[:, POOL_HIST + ti - 1:POOL_HIST + ti, :]


def _mix_in(x, g, w_in, w_pool, pool_scale, *, ti=32):
    B, R, SI, D = x.shape
    d_pool = pool_scale.shape[-1]
    d_in = w_in.shape[-1]
    d_attn = (d_in - d_pool) // 3
    ti = min(ti, SI)
    assert R == RESIDUES >= max(POOL_WINDOWS) and SI % ti == 0 and ti % 16 == 0
    assert d_pool % (LANES * len(POOL_WINDOWS)) == 0
    seq_spec = lambda c: pl.BlockSpec((None, R, ti, c), lambda b, s: (b, 0, s, 0))
    full = lambda shape: pl.BlockSpec(shape, lambda b, s: (0,) * len(shape))
    out_sds = lambda c: jax.ShapeDtypeStruct((B, R, SI, c), BF16)
    return pl.pallas_call(
        functools.partial(_mix_in_kernel, d_pool=d_pool, d_attn=d_attn, ti=ti),
        grid=(B, SI // ti),
        in_specs=[seq_spec(D), full((1, D)), full((D, d_in)), full(w_pool.shape), full((1, d_pool))],
        out_specs=[seq_spec(d_pool), seq_spec(d_attn), seq_spec(d_attn), seq_spec(d_attn)],
        out_shape=[out_sds(d_pool), out_sds(d_attn), out_sds(d_attn), out_sds(d_attn)],
        scratch_shapes=[pltpu.VMEM((R, POOL_HIST + ti, d_pool), F32), pltpu.VMEM((R, ti, d_pool), F32)],
        compiler_params=_params("parallel", "arbitrary"),
        name="mix_in",
    )(x, g.reshape(1, D), w_in, w_pool, pool_scale.reshape(1, d_pool))


def _attn_kernel(q_ref, k_ref, v_ref, o_ref, lse_ref, *, nq, strips, d_attn):
    _, classes, SI, _ = q_ref.shape
    L = nq // strips
    nk, n_blocks = 2 * nq, SI // L
    row = lax.broadcasted_iota(jnp.int32, (nq, 1), 0)
    col = lax.broadcasted_iota(jnp.int32, (1, nk), 1)
    q_strip, q_row = row >> (L.bit_length() - 1), row & (L - 1)
    k_strip, k_row = col >> ((2 * L).bit_length() - 1), col & (2 * L - 1)
    back = strips * (q_row - k_row) + (q_strip - k_strip)

    def band(offset):
        rel = back + strips * offset
        return jnp.where((rel >= 0) & (rel <= WINDOW_STEPS), 0.0, MASKED).astype(F32)

    bias_first, bias_rest = band(0), band(L)
    lane = lax.broadcasted_iota(jnp.int32, (nq, LANES), 1)
    lo_half = lane < HEAD_DIM

    def block(it, carry):
        g, j = it // n_blocks, it % n_blocks
        q_rows = pl.ds(pl.multiple_of(j * L, L), L)
        k_rows = pl.ds(pl.multiple_of(jnp.maximum(j - 1, 0) * L, L), 2 * L)
        bias = jnp.where(j > 0, bias_rest, bias_first)
        lse_tile = jnp.zeros((nq, LANES), F32)
        for hp in range(d_attn // LANES):
            cols = slice(hp * LANES, (hp + 1) * LANES)
            qp = q_ref[:, g, q_rows, cols].reshape(nq, LANES)
            kk = k_ref[:, g, k_rows, cols].reshape(nk, LANES)
            vv = v_ref[:, g, k_rows, cols].reshape(nk, LANES)
            outs, lses = [], []
            for sub in range(2):
                keep = lo_half if sub == 0 else jnp.logical_not(lo_half)
                qm = jnp.where(keep, qp, jnp.zeros_like(qp))
                s = lax.dot_general(qm, kk, (((1,), (1,)), ((), ())), preferred_element_type=F32) + bias
                m = jnp.max(s, axis=-1, keepdims=True)
                p = jnp.exp2(s - m)
                l = jnp.sum(p, axis=-1, keepdims=True)
                o = jnp.dot(p.astype(BF16), vv, preferred_element_type=F32)
                outs.append(o / l)
                lses.append(m + jnp.log2(l))
            o_pair = jnp.where(lo_half, outs[0], outs[1]).astype(BF16)
            o_ref[:, g, q_rows, cols] = o_pair.reshape(strips, L, LANES)
            pair_lse = jnp.where(lane == 2 * hp, lses[0], lses[1])
            lse_tile = jnp.where((lane >> 1) == hp, pair_lse, lse_tile)
        lse_ref[:, g, q_rows, :] = lse_tile.reshape(strips, L, LANES)
        return carry

    lax.fori_loop(0, classes * n_blocks, block, 0)


def _attn_branch(q, k, v, dil):
    B, R, SI, C = q.shape
    strips = R // dil
    nq = max(WINDOW_STEPS, strips * 16)
    L = nq // strips
    classes = max(1, min(dil, ATTN_STEP_ROWS // (strips * SI)))
    assert R % dil == 0 and SI % L == 0 and SI >= 2 * L and dil % classes == 0
    assert C % LANES == 0 and C // HEAD_DIM <= LANES and nq & (nq - 1) == 0 and L & (L - 1) == 0
    view = lambda t: t.reshape(B, strips, dil, SI, t.shape[-1])
    spec = lambda c: pl.BlockSpec((None, strips, classes, SI, c), lambda b, g: (b, 0, g, 0, 0))
    o, lse = pl.pallas_call(
        functools.partial(_attn_kernel, nq=nq, strips=strips, d_attn=C),
        grid=(B, dil // classes),
        in_specs=[spec(C)] * 3,
        out_specs=[spec(C), spec(LANES)],
        out_shape=[jax.ShapeDtypeStruct((B, strips, dil, SI, C), BF16),
                   jax.ShapeDtypeStruct((B, strips, dil, SI, LANES), F32)],
        compiler_params=_params("parallel", "parallel"),
        name=f"attn_d{dil}",
    )(view(q), view(k), view(v))
    return o.reshape(B * R * SI, C), lse.reshape(B * R * SI, LANES)


def _mix_out_kernel(x_ref, ya_ref, o1_ref, o2_ref, o3_ref, l1_ref, l2_ref, l3_ref, gain_ref, wo_ref,
                    gffn_ref, exp_ref, x1_ref, h2_ref, *, d_pool):
    lses = [l1_ref[...], l2_ref[...], l3_ref[...]]
    top = jnp.maximum(jnp.maximum(lses[0], lses[1]), lses[2])
    es = [jnp.exp2(l - top) for l in lses]
    den = es[0] + es[1] + es[2]
    expand = exp_ref[...]

    def per_lane(w):
        hi = w.astype(BF16)
        lo = (w - hi.astype(F32)).astype(BF16)
        return (jnp.dot(hi, expand, preferred_element_type=F32)
                + jnp.dot(lo, expand, preferred_element_type=F32))

    o = jnp.zeros(o1_ref.shape, F32)
    for e, o_ref in zip(es, (o1_ref, o2_ref, o3_ref)):
        o = o + per_lane(e / den) * o_ref[...].astype(F32)
    yb = _rms(o, gain_ref[...]).astype(BF16)
    y = (jnp.dot(ya_ref[...], wo_ref[:d_pool, :], preferred_element_type=F32)
         + jnp.dot(yb, wo_ref[d_pool:, :], preferred_element_type=F32))
    x1 = x_ref[...] + y
    x1_ref[...] = x1
    h2_ref[...] = _rms(x1, gffn_ref[...]).astype(BF16)


def _mix_out(x, ya, outs, lses, attn_gain, w_out, g_ffn, *, tm=512):
    N, D = x.shape
    d_pool = ya.shape[-1]
    d_attn = outs[0].shape[-1]
    tm = min(tm, N)
    assert N % tm == 0
    head_of_lane = jnp.arange(d_attn) // HEAD_DIM
    expand = (jnp.arange(LANES)[:, None] == head_of_lane[None, :]).astype(BF16)
    row = lambda c: pl.BlockSpec((tm, c), lambda i: (i, 0))
    full = lambda shape: pl.BlockSpec(shape, lambda i: (0,) * len(shape))
    return pl.pallas_call(
        functools.partial(_mix_out_kernel, d_pool=d_pool),
        grid=(N // tm,),
        in_specs=[row(D), row(d_pool)] + [row(d_attn)] * 3 + [row(LANES)] * 3
                 + [full((1, d_attn)), full(w_out.shape), full((1, D)), full(expand.shape)],
        out_specs=[row(D), row(D)],
        out_shape=[jax.ShapeDtypeStruct((N, D), F32), jax.ShapeDtypeStruct((N, D), BF16)],
        compiler_params=_params("parallel"),
        name="mix_out",
    )(x, ya, *outs, *lses, attn_gain.reshape(1, d_attn), w_out, g_ffn.reshape(1, D), expand)


def _swiglu_hidden(h, wg, wu):
    a = jnp.dot(h, wg, preferred_element_type=F32)
    b = jnp.dot(h, wu, preferred_element_type=F32)
    return (a * jax.nn.sigmoid(a) * b).astype(BF16)


def _finish(x, y, gfin_ref):
    out = x + y
    return out if gfin_ref is None else _rms(out, gfin_ref[...])


def _ffn_kernel(*refs, final):
    if final:
        h_ref, x_ref, wg_ref, wu_ref, wd_ref, gfin_ref, o_ref, acc = refs
    else:
        (h_ref, x_ref, wg_ref, wu_ref, wd_ref, o_ref, acc), gfin_ref = refs, None
    f = pl.program_id(1)

    @pl.when(f == 0)
    def _():
        acc[...] = jnp.zeros_like(acc)

    hid = _swiglu_hidden(h_ref[...], wg_ref[...], wu_ref[...])
    acc[...] += jnp.dot(hid, wd_ref[...], preferred_element_type=F32)

    @pl.when(f == pl.num_programs(1) - 1)
    def _():
        o_ref[...] = _finish(x_ref[...], acc[...], gfin_ref)


def _pick_chunk(n, target):
    best = None
    for c in range(LANES, min(n, target) + 1, LANES):
        if n % c == 0:
            best = c
    assert best is not None
    return best


def _ffn(h, x, wg, wu, wd, g_final, *, tm=512, tf_target=1536):
    N, D = x.shape
    F = wg.shape[-1]
    tm = min(tm, N)
    tf = _pick_chunk(F, tf_target)
    final = g_final is not None
    row = pl.BlockSpec((tm, D), lambda i, f: (i, 0))
    in_specs = [row, row,
                pl.BlockSpec((D, tf), lambda i, f: (0, f)),
                pl.BlockSpec((D, tf), lambda i, f: (0, f)),
                pl.BlockSpec((tf, D), lambda i, f: (f, 0))]
    args = [h, x, wg, wu, wd]
    if final:
        in_specs.append(pl.BlockSpec((1, D), lambda i, f: (0, 0)))
        args.append(g_final.reshape(1, D))
    return pl.pallas_call(
        functools.partial(_ffn_kernel, final=final),
        grid=(N // tm, F // tf),
        in_specs=in_specs,
        out_specs=row,
        out_shape=jax.ShapeDtypeStruct((N, D), F32),
        scratch_shapes=[pltpu.VMEM((tm, D), F32)],
        compiler_params=_params("parallel", "arbitrary"),
        name="ffn_dense",
    )(*args)


def _router_kernel(x_ref, g_ref, wr_ref, gate_ref, *, n_experts):
    h = _rms(x_ref[...], g_ref[...])
    logits = jnp.dot(h, wr_ref[...], preferred_element_type=F32, precision=lax.Precision.HIGHEST)
    lane = lax.broadcasted_iota(jnp.int32, logits.shape, 1)
    logits = jnp.where(lane < n_experts, logits, -jnp.inf)
    picks = []
    for _ in range(TOP_K):
        m = jnp.max(logits, axis=-1, keepdims=True)
        idx = jnp.min(jnp.where(logits == m, lane, LANES), axis=-1, keepdims=True)
        picks.append((m, idx))
        logits = jnp.where(lane == idx, -jnp.inf, logits)
    (m1, i1), (m2, i2) = picks
    e2 = jnp.exp(m2 - m1)
    g1 = 1.0 / (1.0 + e2)
    g2 = e2 / (1.0 + e2)
    picked = jnp.logical_or(lane == i1 + n_experts, lane == i2 + n_experts)
    gate_ref[...] = jnp.where(lane == i1, g1, jnp.where(lane == i2, g2, jnp.where(picked, 1.0, 0.0)))


def _router(x, g_ffn, w_router, *, tm=512):
    N, D = x.shape
    E = w_router.shape[-1]
    assert 2 * E <= LANES and TOP_K == 2
    tm = min(tm, N)
    wr = jnp.zeros((D, LANES), F32).at[:, :E].set(w_router)
    return pl.pallas_call(
        functools.partial(_router_kernel, n_experts=E),
        grid=(N // tm,),
        in_specs=[pl.BlockSpec((tm, D), lambda i: (i, 0)),
                  pl.BlockSpec((1, D), lambda i: (0, 0)),
                  pl.BlockSpec((D, LANES), lambda i: (0, 0))],
        out_specs=pl.BlockSpec((tm, LANES), lambda i: (i, 0)),
        out_shape=jax.ShapeDtypeStruct((N, LANES), F32),
        compiler_params=_params("parallel"),
        name="router",
    )(x, g_ffn.reshape(1, D), wr)


def _plan_routing(route, n_experts, *, tm, rb, ch):
    N = route.shape[0]
    E = n_experts
    i32 = jnp.int32
    n_chunks = N // ch
    p_max = TOP_K * N + E * tm
    n_blocks, n_tiles = p_max // rb, p_max // tm
    n_items = n_blocks + E * n_chunks

    gate = route[:, :E].T
    sel = (route[:, E:2 * E].T > 0.5).astype(i32)
    rank = jnp.cumsum(sel, axis=1) - sel
    chunk_cnt = sel.reshape(E, n_chunks, ch).sum(-1)
    cum = jnp.concatenate([jnp.zeros((E, 1), i32), jnp.cumsum(chunk_cnt, axis=1)], axis=1)
    cnt = cum[:, -1]
    seg_len = (cnt + tm - 1) // tm * tm
    seg_end = jnp.cumsum(seg_len)
    seg_start = seg_end - seg_len
    pos = jnp.where(sel > 0, seg_start[:, None] + rank, -1)
    n_active_tiles = seg_end[-1] // tm
    tile_ids = jnp.arange(n_tiles, dtype=i32)
    tile_expert = jnp.minimum(jnp.sum(seg_end[None, :] <= tile_ids[:, None] * tm, axis=1), E - 1).astype(i32)

    def expand(counts, n_out):
        ends = jnp.cumsum(counts)
        total = ends[-1]
        w = jnp.minimum(jnp.arange(n_out, dtype=i32), total - 1)
        grp = jnp.sum(ends[None, :] <= w[:, None], axis=1).astype(i32)
        off = w - (ends[grp] - counts[grp])
        return grp, off, jnp.arange(n_out, dtype=i32) < total, w

    blk = jnp.arange(n_blocks, dtype=i32)
    b_exp = tile_expert[blk * rb // tm]
    lo = blk * rb - seg_start[b_exp]
    hi = jnp.minimum(lo + rb, cnt[b_exp])
    nonempty = lo < cnt[b_exp]
    b_cum = jnp.sum(jnp.where(b_exp[:, None, None] == jnp.arange(E, dtype=i32)[None, :, None], cum[None], 0),
                    axis=1)
    c_lo = jnp.sum(b_cum[:, 1:] <= lo[:, None], axis=1)
    c_hi = jnp.sum(b_cum[:, :-1] < hi[:, None], axis=1) - 1
    b_items = jnp.where(blk * rb < n_active_tiles * tm, jnp.where(nonempty, c_hi - c_lo + 1, 1), 0).astype(i32)
    c_lo = jnp.where(nonempty, c_lo, 0)
    g_blk, g_off, g_valid, _ = expand(b_items, n_items)
    g_first = jnp.logical_and(g_valid, g_off == 0)
    g_last = jnp.logical_and(g_valid, g_off == b_items[g_blk] - 1)
    gather_plan = jnp.stack([g_blk, c_lo[g_blk] + g_off, b_exp[g_blk],
                             g_first, g_last, g_valid]).astype(i32)

    first_row = seg_start[None, :] + cum[:, :-1].T
    last_row = seg_start[None, :] + cum[:, 1:].T - 1
    ce_items = jnp.where(chunk_cnt.T > 0, last_row // rb - first_row // rb + 1, 0).reshape(-1).astype(i32)
    c_grp, c_off, c_valid, c_w = expand(ce_items, n_items)
    c_chunk = c_grp // E
    c_blk = (first_row // rb).reshape(-1)[c_grp] + c_off
    per_chunk = ce_items.reshape(n_chunks, E).sum(1)
    chunk_end = jnp.cumsum(per_chunk)
    c_first = jnp.logical_and(c_valid, c_w == (chunk_end - per_chunk)[c_chunk])
    c_last = jnp.logical_and(c_valid, c_w == chunk_end[c_chunk] - 1)
    combine_plan = jnp.stack([c_blk, c_chunk, c_grp % E, c_first, c_last, c_valid]).astype(i32)

    token_info = jnp.concatenate([pos.T.astype(F32), gate.T, jnp.zeros((N, LANES - 2 * E), F32)], axis=1)
    return dict(pos=pos.reshape(E, 1, N), token_info=token_info, gather_plan=gather_plan,
                combine_plan=combine_plan, tile_expert=tile_expert,
                n_active_tiles=n_active_tiles.reshape(1).astype(i32), p_max=p_max, n_items=n_items)


PLAN_BLOCK, PLAN_CHUNK, PLAN_EXPERT, PLAN_FIRST, PLAN_LAST, PLAN_VALID = range(6)


def _moe_gather_kernel(plan_ref, pos_ref, h_ref, xs_ref, acc, *, rb):
    w = pl.program_id(0)

    @pl.when(plan_ref[PLAN_FIRST, w] == 1)
    def _():
        acc[...] = jnp.zeros_like(acc)

    @pl.when(plan_ref[PLAN_VALID, w] == 1)
    def _():
        ch = h_ref.shape[0]
        rows = plan_ref[PLAN_BLOCK, w] * rb + lax.broadcasted_iota(jnp.int32, (rb, ch), 0)
        onehot = jnp.where(pos_ref[0] == rows, 1.0, 0.0).astype(BF16)
        acc[...] += jnp.dot(onehot, h_ref[...], preferred_element_type=F32)

    @pl.when(plan_ref[PLAN_LAST, w] == 1)
    def _():
        xs_ref[...] = acc[...].astype(BF16)


def _moe_gather(h, plan, *, rb, ch):
    N, D = h.shape
    grid_spec = pltpu.PrefetchScalarGridSpec(
        num_scalar_prefetch=1,
        grid=(plan["n_items"],),
        in_specs=[pl.BlockSpec((1, 1, ch), lambda w, p: (p[PLAN_EXPERT, w], 0, p[PLAN_CHUNK, w])),
                  pl.BlockSpec((ch, D), lambda w, p: (p[PLAN_CHUNK, w], 0))],
        out_specs=pl.BlockSpec((rb, D), lambda w, p: (p[PLAN_BLOCK, w], 0)),
        scratch_shapes=[pltpu.VMEM((rb, D), F32)],
    )
    return pl.pallas_call(
        functools.partial(_moe_gather_kernel, rb=rb),
        grid_spec=grid_spec,
        out_shape=jax.ShapeDtypeStruct((plan["p_max"], D), BF16),
        compiler_params=_params("arbitrary"),
        name="moe_gather",
    )(plan["gather_plan"], plan["pos"], h)


def _moe_experts_kernel(te_ref, na_ref, xs_ref, wg_ref, wu_ref, wd_ref, o_ref, acc, *, tf):
    @pl.when(pl.program_id(0) < na_ref[0])
    def _():
        x = xs_ref[...]
        for c in range(wg_ref.shape[-1] // tf):
            cols = slice(c * tf, (c + 1) * tf)
            hid = _swiglu_hidden(x, wg_ref[0, :, cols], wu_ref[0, :, cols])
            part = jnp.dot(hid, wd_ref[0, cols, :], preferred_element_type=F32)
            if c == 0:
                acc[...] = part
            else:
                acc[...] += part
        o_ref[...] = acc[...].astype(BF16)


def _moe_experts(xs, plan, wg, wu, wd, *, tm, tf_target=512):
    P, D = xs.shape
    E, _, F = wg.shape
    tf = _pick_chunk(F, tf_target)
    tile = lambda i, te, na: (jnp.minimum(i, na[0] - 1), 0)
    resident = pl.Buffered(1)
    grid_spec = pltpu.PrefetchScalarGridSpec(
        num_scalar_prefetch=2,
        grid=(P // tm,),
        in_specs=[pl.BlockSpec((tm, D), tile),
                  pl.BlockSpec((1, D, F), lambda i, te, na: (te[i], 0, 0), pipeline_mode=resident),
                  pl.BlockSpec((1, D, F), lambda i, te, na: (te[i], 0, 0), pipeline_mode=resident),
                  pl.BlockSpec((1, F, D), lambda i, te, na: (te[i], 0, 0), pipeline_mode=resident)],
        out_specs=pl.BlockSpec((tm, D), tile),
        scratch_shapes=[pltpu.VMEM((tm, D), F32)],
    )
    return pl.pallas_call(
        functools.partial(_moe_experts_kernel, tf=tf),
        grid_spec=grid_spec,
        out_shape=jax.ShapeDtypeStruct((P, D), BF16),
        compiler_params=_params("arbitrary"),
        name="moe_experts",
    )(plan["tile_expert"], plan["n_active_tiles"], xs, wg, wu, wd)


def _moe_combine_kernel(*refs, final, rb, n_experts):
    if final:
        plan_ref, info_ref, ys_ref, x_ref, gfin_ref, o_ref, acc = refs
    else:
        (plan_ref, info_ref, ys_ref, x_ref, o_ref, acc), gfin_ref = refs, None
    w = pl.program_id(0)

    @pl.when(plan_ref[PLAN_FIRST, w] == 1)
    def _():
        acc[...] = jnp.zeros_like(acc)

    @pl.when(plan_ref[PLAN_VALID, w] == 1)
    def _():
        info = info_ref[...]
        ch = info.shape[0]
        e = plan_ref[PLAN_EXPERT, w]
        lane = lax.broadcasted_iota(jnp.int32, info.shape, 1)
        pos = jnp.sum(jnp.where(lane == e, info, 0.0), axis=-1, keepdims=True)
        gate = jnp.sum(jnp.where(lane == e + n_experts, info, 0.0), axis=-1, keepdims=True)
        local = pos - (plan_ref[PLAN_BLOCK, w] * rb).astype(F32)
        cols = lax.broadcasted_iota(jnp.int32, (ch, rb), 1).astype(F32)
        onehot = jnp.where(local == cols, 1.0, 0.0).astype(BF16)
        acc[...] += gate * jnp.dot(onehot, ys_ref[...], preferred_element_type=F32)

    @pl.when(plan_ref[PLAN_LAST, w] == 1)
    def _():
        o_ref[...] = _finish(x_ref[...], acc[...], gfin_ref)


def _moe_combine(ys, x, plan, g_final, *, rb, ch, n_experts):
    N, D = x.shape
    final = g_final is not None
    chunk = lambda c: pl.BlockSpec((ch, c), lambda w, p: (p[PLAN_CHUNK, w], 0))
    in_specs = [chunk(LANES), pl.BlockSpec((rb, D), lambda w, p: (p[PLAN_BLOCK, w], 0)), chunk(D)]
    args = [plan["combine_plan"], plan["token_info"], ys, x]
    if final:
        in_specs.append(pl.BlockSpec((1, D), lambda w, p: (0, 0)))
        args.append(g_final.reshape(1, D))
    grid_spec = pltpu.PrefetchScalarGridSpec(
        num_scalar_prefetch=1,
        grid=(plan["n_items"],),
        in_specs=in_specs,
        out_specs=chunk(D),
        scratch_shapes=[pltpu.VMEM((ch, D), F32)],
    )
    return pl.pallas_call(
        functools.partial(_moe_combine_kernel, final=final, rb=rb, n_experts=n_experts),
        grid_spec=grid_spec,
        out_shape=jax.ShapeDtypeStruct((N, D), F32),
        compiler_params=_params("arbitrary"),
        name="moe_combine",
    )(*args)


def _moe(h, x, route, wg, wu, wd, g_final, *, tm=512, rb=256, ch=512):
    N = x.shape[0]
    E = wg.shape[0]
    tm, ch = min(tm, N), min(ch, N)
    rb = min(rb, tm)
    assert N % ch == 0 and tm % rb == 0
    plan = _plan_routing(route, E, tm=tm, rb=rb, ch=ch)
    xs = _moe_gather(h, plan, rb=rb, ch=ch)
    ys = _moe_experts(xs, plan, wg, wu, wd, tm=tm)
    return _moe_combine(ys, x, plan, g_final, rb=rb, ch=ch, n_experts=E)


def kernel(x, norm_mix, w_in, w_pool, pool_scale, attn_gain, w_out, norm_ffn, ffn_wg, ffn_wu, ffn_wd,
           w_router, moe_wg, moe_wu, moe_wd, final_norm):
    B, S, D = x.shape
    depth = norm_mix.shape[0]
    bf = lambda t: t.astype(BF16)
    w_in, w_pool, w_out = bf(w_in), bf(w_pool), bf(w_out)
    ffn_wg, ffn_wu, ffn_wd = bf(ffn_wg), bf(ffn_wu), bf(ffn_wd)
    moe_wg, moe_wu, moe_wd = bf(moe_wg), bf(moe_wu), bf(moe_wd)
    R = RESIDUES
    assert S % R == 0
    x = x.reshape(B, S // R, R, D).transpose(0, 2, 1, 3)
    for l in range(depth):
        ya, q, k, v = _mix_in(x, norm_mix[l], w_in[l], w_pool[l], pool_scale[l])
        branches = [_attn_branch(q, k, v, dil) for _, dil in DILATED_PATTERNS]
        x1, h2 = _mix_out(x.reshape(B * S, D), ya.reshape(B * S, -1),
                          [o for o, _ in branches], [lse for _, lse in branches],
                          attn_gain[l], w_out[l], norm_ffn[l])
        g_final = final_norm if l == depth - 1 else None
        i = l // 2
        if l % 2 == 0:
            x2 = _ffn(h2, x1, ffn_wg[i], ffn_wu[i], ffn_wd[i], g_final)
        else:
            route = _router(x1, norm_ffn[l], w_router[i])
            x2 = _moe(h2, x1, route, moe_wg[i], moe_wu[i], moe_wd[i], g_final)
        x = x2.reshape(B, R, S // R, D)
    return x.transpose(0, 2, 1, 3).reshape(B, S, D)
```

```python
import functools

import jax
import jax.numpy as jnp
from jax import lax
from jax.experimental import pallas as pl
from jax.experimental.pallas import tpu as pltpu
from jax.experimental.pallas import tpu_sc as plsc

F32 = jnp.float32
BF16 = jnp.bfloat16

EPS = 1e-6
LANES = 128
HEAD_DIM = 64
POOL_WINDOWS = (2, 4, 8, 16)
POOL_HIST = 8
DILATED_PATTERNS = ((128, 1), (512, 4), (2048, 16))
WINDOW_STEPS = 128
RESIDUES = 16
ATTN_STEP_ROWS = 1024
LOG2_E = 1.4426950408889634
SC_CORES, SC_SUBCORES = 2, 16
SC_WORKERS = SC_CORES * SC_SUBCORES
SC_WINDOW = 64
assert all(w // d == WINDOW_STEPS and RESIDUES % d == 0 for w, d in DILATED_PATTERNS)
TOP_K = 2
MASKED = -1e30
VMEM_LIMIT = 48 * 1024 * 1024


def _rms(x, g):
    return x * lax.rsqrt(jnp.mean(x * x, axis=-1, keepdims=True) + EPS) * g


def _params(*sem):
    return pltpu.CompilerParams(dimension_semantics=sem, vmem_limit_bytes=VMEM_LIMIT)


def _mix_in_kernel(x_ref, g_ref, w_ref, wp_ref, ps_ref, ya_ref, q_ref, k_ref, v_ref, ubuf, uprev,
                   *, d_pool, d_attn, ti):
    s = pl.program_id(1)
    R, D = RESIDUES, x_ref.shape[-1]
    rows = R * ti
    blk = lambda t: t.reshape(R, ti, t.shape[-1])
    h = _rms(x_ref[...].reshape(rows, D), g_ref[...]).astype(BF16)
    proj = jnp.dot(h, w_ref[...], preferred_element_type=F32)
    scale = HEAD_DIM ** -0.5 * LOG2_E
    q_ref[...] = blk((proj[:, d_pool:d_pool + d_attn] * scale).astype(BF16))
    k_ref[...] = blk(proj[:, d_pool + d_attn:d_pool + 2 * d_attn].astype(BF16))
    v_ref[...] = blk(proj[:, d_pool + 2 * d_attn:].astype(BF16))

    @pl.when(s == 0)
    def _():
        ubuf[:, 0:POOL_HIST, :] = jnp.zeros((R, POOL_HIST, d_pool), F32)

    ubuf[:, POOL_HIST:POOL_HIST + ti, :] = blk(proj[:, :d_pool])
    uprev[...] = ubuf[:, POOL_HIST - 1:POOL_HIST - 1 + ti, :]
    at_start = (s * ti + lax.broadcasted_iota(jnp.int32, (ti, 1), 0)) == 0
    group = d_pool // len(POOL_WINDOWS)
    zs = []
    for gi, w in enumerate(POOL_WINDOWS):
        cols = slice(gi * group, (gi + 1) * group)
        ds = []
        for r in range(R):
            ug = ubuf[r, POOL_HIST:POOL_HIST + ti, cols]
            win = ug
            for back in range(1, w):
                rr = r - back
                win = win + (ubuf[rr, POOL_HIST:POOL_HIST + ti, cols] if rr >= 0 else uprev[rr + R, :, cols])
            cnt = jnp.where(at_start, float(min(r + 1, w)), float(w))
            ds.append(win / cnt - ug)
        d = jnp.concatenate(ds, axis=0).astype(BF16)
        zs.append(jnp.dot(d, wp_ref[gi], preferred_element_type=F32))
    z = jnp.concatenate(zs, axis=-1)
    ya_ref[...] = blk(_rms(z, ps_ref[...]).astype(BF16))
    ubuf[:, POOL_HIST - 1:POOL_HIST, :] = ubuf[:, POOL_HIST + ti - 1:POOL_HIST + ti, :]


def _mix_in(x, g, w_in, w_pool, pool_scale, *, ti=32):
    B, R, SI, D = x.shape
    d_pool = pool_scale.shape[-1]
    d_in = w_in.shape[-1]
    d_attn = (d_in - d_pool) // 3
    ti = min(ti, SI)
    assert R == RESIDUES >= max(POOL_WINDOWS) and SI % ti == 0 and ti % 16 == 0
    assert d_pool % (LANES * len(POOL_WINDOWS)) == 0
    seq_spec = lambda c: pl.BlockSpec((None, R, ti, c), lambda b, s: (b, 0, s, 0))
    full = lambda shape: pl.BlockSpec(shape, lambda b, s: (0,) * len(shape))
    out_sds = lambda c: jax.ShapeDtypeStruct((B, R, SI, c), BF16)
    return pl.pallas_call(
        functools.partial(_mix_in_kernel, d_pool=d_pool, d_attn=d_attn, ti=ti),
        grid=(B, SI // ti),
        in_specs=[seq_spec(D), full((1, D)), full((D, d_in)), full(w_pool.shape), full((1, d_pool))],
        out_specs=[seq_spec(d_pool), seq_spec(d_attn), seq_spec(d_attn), seq_spec(d_attn)],
        out_shape=[out_sds(d_pool), out_sds(d_attn), out_sds(d_attn), out_sds(d_attn)],
        scratch_shapes=[pltpu.VMEM((R, POOL_HIST + ti, d_pool), F32), pltpu.VMEM((R, ti, d_pool), F32)],
        compiler_params=_params("parallel", "arbitrary"),
        name="mix_in",
    )(x, g.reshape(1, D), w_in, w_pool, pool_scale.reshape(1, d_pool))


def _attn_kernel(q_ref, k_ref, v_ref, o_ref, lse_ref, *, nq, strips, d_attn):
    _, classes, SI, _ = q_ref.shape
    L = nq // strips
    nk, n_blocks = 2 * nq, SI // L
    row = lax.broadcasted_iota(jnp.int32, (nq, 1), 0)
    col = lax.broadcasted_iota(jnp.int32, (1, nk), 1)
    q_strip, q_row = row >> (L.bit_length() - 1), row & (L - 1)
    k_strip, k_row = col >> ((2 * L).bit_length() - 1), col & (2 * L - 1)
    back = strips * (q_row - k_row) + (q_strip - k_strip)

    def band(offset):
        rel = back + strips * offset
        return jnp.where((rel >= 0) & (rel <= WINDOW_STEPS), 0.0, MASKED).astype(F32)

    bias_first, bias_rest = band(0), band(L)
    lane = lax.broadcasted_iota(jnp.int32, (nq, LANES), 1)
    lo_half = lane < HEAD_DIM

    def block(it, carry):
        g, j = it // n_blocks, it % n_blocks
        q_rows = pl.ds(pl.multiple_of(j * L, L), L)
        k_rows = pl.ds(pl.multiple_of(jnp.maximum(j - 1, 0) * L, L), 2 * L)
        bias = jnp.where(j > 0, bias_rest, bias_first)
        lse_tile = jnp.zeros((nq, LANES), F32)
        for hp in range(d_attn // LANES):
            cols = slice(hp * LANES, (hp + 1) * LANES)
            qp = q_ref[:, g, q_rows, cols].reshape(nq, LANES)
            kk = k_ref[:, g, k_rows, cols].reshape(nk, LANES)
            vv = v_ref[:, g, k_rows, cols].reshape(nk, LANES)
            outs, lses = [], []
            for sub in range(2):
                keep = lo_half if sub == 0 else jnp.logical_not(lo_half)
                qm = jnp.where(keep, qp, jnp.zeros_like(qp))
                s = lax.dot_general(qm, kk, (((1,), (1,)), ((), ())), preferred_element_type=F32) + bias
                m = jnp.max(s, axis=-1, keepdims=True)
                p = jnp.exp2(s - m)
                l = jnp.sum(p, axis=-1, keepdims=True)
                o = jnp.dot(p.astype(BF16), vv, preferred_element_type=F32)
                outs.append(o / l)
                lses.append(m + jnp.log2(l))
            o_pair = jnp.where(lo_half, outs[0], outs[1]).astype(BF16)
            o_ref[:, g, q_rows, cols] = o_pair.reshape(strips, L, LANES)
            pair_lse = jnp.where(lane == 2 * hp, lses[0], lses[1])
            lse_tile = jnp.where((lane >> 1) == hp, pair_lse, lse_tile)
        lse_ref[:, g, q_rows, :] = lse_tile.reshape(strips, L, LANES)
        return carry

    lax.fori_loop(0, classes * n_blocks, block, 0)


def _attn_branch(q, k, v, dil):
    B, R, SI, C = q.shape
    strips = R // dil
    nq = max(WINDOW_STEPS, strips * 16)
    L = nq // strips
    classes = max(1, min(dil, ATTN_STEP_ROWS // (strips * SI)))
    assert R % dil == 0 and SI % L == 0 and SI >= 2 * L and dil % classes == 0
    assert C % LANES == 0 and C // HEAD_DIM <= LANES and nq & (nq - 1) == 0 and L & (L - 1) == 0
    view = lambda t: t.reshape(B, strips, dil, SI, t.shape[-1])
    spec = lambda c: pl.BlockSpec((None, strips, classes, SI, c), lambda b, g: (b, 0, g, 0, 0))
    o, lse = pl.pallas_call(
        functools.partial(_attn_kernel, nq=nq, strips=strips, d_attn=C),
        grid=(B, dil // classes),
        in_specs=[spec(C)] * 3,
        out_specs=[spec(C), spec(LANES)],
        out_shape=[jax.ShapeDtypeStruct((B, strips, dil, SI, C), BF16),
                   jax.ShapeDtypeStruct((B, strips, dil, SI, LANES), F32)],
        compiler_params=_params("parallel", "parallel"),
        name=f"attn_d{dil}",
    )(view(q), view(k), view(v))
    return o.reshape(B * R * SI, C), lse.reshape(B * R * SI, LANES)


def _mix_out_kernel(x_ref, ya_ref, o1_ref, o2_ref, o3_ref, l1_ref, l2_ref, l3_ref, gain_ref, wo_ref,
                    gffn_ref, exp_ref, x1_ref, h2_ref, *, d_pool):
    lses = [l1_ref[...], l2_ref[...], l3_ref[...]]
    top = jnp.maximum(jnp.maximum(lses[0], lses[1]), lses[2])
    es = [jnp.exp2(l - top) for l in lses]
    den = es[0] + es[1] + es[2]
    expand = exp_ref[...]

    def per_lane(w):
        hi = w.astype(BF16)
        lo = (w - hi.astype(F32)).astype(BF16)
        return (jnp.dot(hi, expand, preferred_element_type=F32)
                + jnp.dot(lo, expand, preferred_element_type=F32))

    o = jnp.zeros(o1_ref.shape, F32)
    for e, o_ref in zip(es, (o1_ref, o2_ref, o3_ref)):
        o = o + per_lane(e / den) * o_ref[...].astype(F32)
    yb = _rms(o, gain_ref[...]).astype(BF16)
    y = (jnp.dot(ya_ref[...], wo_ref[:d_pool, :], preferred_element_type=F32)
         + jnp.dot(yb, wo_ref[d_pool:, :], preferred_element_type=F32))
    x1 = x_ref[...] + y
    x1_ref[...] = x1
    h2_ref[...] = _rms(x1, gffn_ref[...]).astype(h2_ref.dtype)


def _mix_out(x, ya, outs, lses, attn_gain, w_out, g_ffn, *, h_dtype, tm=512):
    N, D = x.shape
    d_pool = ya.shape[-1]
    d_attn = outs[0].shape[-1]
    tm = min(tm, N)
    assert N % tm == 0
    head_of_lane = jnp.arange(d_attn) // HEAD_DIM
    expand = (jnp.arange(LANES)[:, None] == head_of_lane[None, :]).astype(BF16)
    row = lambda c: pl.BlockSpec((tm, c), lambda i: (i, 0))
    full = lambda shape: pl.BlockSpec(shape, lambda i: (0,) * len(shape))
    return pl.pallas_call(
        functools.partial(_mix_out_kernel, d_pool=d_pool),
        grid=(N // tm,),
        in_specs=[row(D), row(d_pool)] + [row(d_attn)] * 3 + [row(LANES)] * 3
                 + [full((1, d_attn)), full(w_out.shape), full((1, D)), full(expand.shape)],
        out_specs=[row(D), row(D)],
        out_shape=[jax.ShapeDtypeStruct((N, D), F32), jax.ShapeDtypeStruct((N, D), h_dtype)],
        compiler_params=_params("parallel"),
        name="mix_out",
    )(x, ya, *outs, *lses, attn_gain.reshape(1, d_attn), w_out, g_ffn.reshape(1, D), expand)


def _swiglu_hidden(h, wg, wu):
    a = jnp.dot(h, wg, preferred_element_type=F32)
    b = jnp.dot(h, wu, preferred_element_type=F32)
    return (a * jax.nn.sigmoid(a) * b).astype(BF16)


def _finish(x, y, gfin_ref):
    out = x + y
    return out if gfin_ref is None else _rms(out, gfin_ref[...])


def _ffn_kernel(*refs, final):
    if final:
        h_ref, x_ref, wg_ref, wu_ref, wd_ref, gfin_ref, o_ref, acc = refs
    else:
        (h_ref, x_ref, wg_ref, wu_ref, wd_ref, o_ref, acc), gfin_ref = refs, None
    f = pl.program_id(1)

    @pl.when(f == 0)
    def _():
        acc[...] = jnp.zeros_like(acc)

    hid = _swiglu_hidden(h_ref[...], wg_ref[...], wu_ref[...])
    acc[...] += jnp.dot(hid, wd_ref[...], preferred_element_type=F32)

    @pl.when(f == pl.num_programs(1) - 1)
    def _():
        o_ref[...] = _finish(x_ref[...], acc[...], gfin_ref)


def _pick_chunk(n, target):
    best = None
    for c in range(LANES, min(n, target) + 1, LANES):
        if n % c == 0:
            best = c
    assert best is not None
    return best


def _ffn(h, x, wg, wu, wd, g_final, *, tm=512, tf_target=1536):
    N, D = x.shape
    F = wg.shape[-1]
    tm = min(tm, N)
    tf = _pick_chunk(F, tf_target)
    final = g_final is not None
    row = pl.BlockSpec((tm, D), lambda i, f: (i, 0))
    in_specs = [row, row,
                pl.BlockSpec((D, tf), lambda i, f: (0, f)),
                pl.BlockSpec((D, tf), lambda i, f: (0, f)),
                pl.BlockSpec((tf, D), lambda i, f: (f, 0))]
    args = [h, x, wg, wu, wd]
    if final:
        in_specs.append(pl.BlockSpec((1, D), lambda i, f: (0, 0)))
        args.append(g_final.reshape(1, D))
    return pl.pallas_call(
        functools.partial(_ffn_kernel, final=final),
        grid=(N // tm, F // tf),
        in_specs=in_specs,
        out_specs=row,
        out_shape=jax.ShapeDtypeStruct((N, D), F32),
        scratch_shapes=[pltpu.VMEM((tm, D), F32)],
        compiler_params=_params("parallel", "arbitrary"),
        name="ffn_dense",
    )(*args)


def _router_kernel(x_ref, g_ref, wr_ref, gate_ref, *, n_experts):
    h = _rms(x_ref[...], g_ref[...])
    logits = jnp.dot(h, wr_ref[...], preferred_element_type=F32, precision=lax.Precision.HIGHEST)
    lane = lax.broadcasted_iota(jnp.int32, logits.shape, 1)
    logits = jnp.where(lane < n_experts, logits, -jnp.inf)
    picks = []
    for _ in range(TOP_K):
        m = jnp.max(logits, axis=-1, keepdims=True)
        idx = jnp.min(jnp.where(logits == m, lane, LANES), axis=-1, keepdims=True)
        picks.append((m, idx))
        logits = jnp.where(lane == idx, -jnp.inf, logits)
    (m1, i1), (m2, i2) = picks
    e2 = jnp.exp(m2 - m1)
    g1 = 1.0 / (1.0 + e2)
    g2 = e2 / (1.0 + e2)
    picked = jnp.logical_or(lane == i1 + n_experts, lane == i2 + n_experts)
    gate_ref[...] = jnp.where(lane == i1, g1, jnp.where(lane == i2, g2, jnp.where(picked, 1.0, 0.0)))


def _router(x, g_ffn, w_router, *, tm=512):
    N, D = x.shape
    E = w_router.shape[-1]
    assert 2 * E <= LANES and TOP_K == 2
    tm = min(tm, N)
    wr = jnp.zeros((D, LANES), F32).at[:, :E].set(w_router)
    return pl.pallas_call(
        functools.partial(_router_kernel, n_experts=E),
        grid=(N // tm,),
        in_specs=[pl.BlockSpec((tm, D), lambda i: (i, 0)),
                  pl.BlockSpec((1, D), lambda i: (0, 0)),
                  pl.BlockSpec((D, LANES), lambda i: (0, 0))],
        out_specs=pl.BlockSpec((tm, LANES), lambda i: (i, 0)),
        out_shape=jax.ShapeDtypeStruct((N, LANES), F32),
        compiler_params=_params("parallel"),
        name="router",
    )(x, g_ffn.reshape(1, D), wr)


def _plan_routing(route, n_experts, *, tm, rb, ch):
    N = route.shape[0]
    E = n_experts
    i32 = jnp.int32
    n_chunks = N // ch
    p_max = TOP_K * N + E * tm
    n_blocks, n_tiles = p_max // rb, p_max // tm
    n_items = n_blocks + E * n_chunks

    gate = route[:, :E].T
    sel = (route[:, E:2 * E].T > 0.5).astype(i32)
    rank = jnp.cumsum(sel, axis=1) - sel
    chunk_cnt = sel.reshape(E, n_chunks, ch).sum(-1)
    cum = jnp.concatenate([jnp.zeros((E, 1), i32), jnp.cumsum(chunk_cnt, axis=1)], axis=1)
    cnt = cum[:, -1]
    seg_len = (cnt + tm - 1) // tm * tm
    seg_end = jnp.cumsum(seg_len)
    seg_start = seg_end - seg_len
    pos = jnp.where(sel > 0, seg_start[:, None] + rank, -1)
    n_active_tiles = seg_end[-1] // tm
    tile_ids = jnp.arange(n_tiles, dtype=i32)
    tile_expert = jnp.minimum(jnp.sum(seg_end[None, :] <= tile_ids[:, None] * tm, axis=1), E - 1).astype(i32)

    def expand(counts, n_out):
        ends = jnp.cumsum(counts)
        total = ends[-1]
        w = jnp.minimum(jnp.arange(n_out, dtype=i32), total - 1)
        grp = jnp.sum(ends[None, :] <= w[:, None], axis=1).astype(i32)
        off = w - (ends[grp] - counts[grp])
        return grp, off, jnp.arange(n_out, dtype=i32) < total, w

    blk = jnp.arange(n_blocks, dtype=i32)
    b_exp = tile_expert[blk * rb // tm]
    lo = blk * rb - seg_start[b_exp]
    hi = jnp.minimum(lo + rb, cnt[b_exp])
    nonempty = lo < cnt[b_exp]
    b_cum = jnp.sum(jnp.where(b_exp[:, None, None] == jnp.arange(E, dtype=i32)[None, :, None], cum[None], 0),
                    axis=1)
    c_lo = jnp.sum(b_cum[:, 1:] <= lo[:, None], axis=1)
    c_hi = jnp.sum(b_cum[:, :-1] < hi[:, None], axis=1) - 1
    b_items = jnp.where(blk * rb < n_active_tiles * tm, jnp.where(nonempty, c_hi - c_lo + 1, 1), 0).astype(i32)
    c_lo = jnp.where(nonempty, c_lo, 0)
    g_blk, g_off, g_valid, _ = expand(b_items, n_items)
    g_first = jnp.logical_and(g_valid, g_off == 0)
    g_last = jnp.logical_and(g_valid, g_off == b_items[g_blk] - 1)
    gather_plan = jnp.stack([g_blk, c_lo[g_blk] + g_off, b_exp[g_blk],
                             g_first, g_last, g_valid]).astype(i32)

    first_row = seg_start[None, :] + cum[:, :-1].T
    last_row = seg_start[None, :] + cum[:, 1:].T - 1
    ce_items = jnp.where(chunk_cnt.T > 0, last_row // rb - first_row // rb + 1, 0).reshape(-1).astype(i32)
    c_grp, c_off, c_valid, c_w = expand(ce_items, n_items)
    c_chunk = c_grp // E
    c_blk = (first_row // rb).reshape(-1)[c_grp] + c_off
    per_chunk = ce_items.reshape(n_chunks, E).sum(1)
    chunk_end = jnp.cumsum(per_chunk)
    c_first = jnp.logical_and(c_valid, c_w == (chunk_end - per_chunk)[c_chunk])
    c_last = jnp.logical_and(c_valid, c_w == chunk_end[c_chunk] - 1)
    combine_plan = jnp.stack([c_blk, c_chunk, c_grp % E, c_first, c_last, c_valid]).astype(i32)

    token_info = jnp.concatenate([pos.T.astype(F32), gate.T, jnp.zeros((N, LANES - 2 * E), F32)], axis=1)
    row_hi = jnp.max(pos, axis=0)
    row_lo = jnp.sum(jnp.where(sel > 0, pos, 0), axis=0) - row_hi
    gate_hi = jnp.sum(jnp.where(pos == row_hi[None, :], gate, 0.0), axis=0)
    gate_lo = jnp.sum(jnp.where(pos == row_lo[None, :], gate, 0.0), axis=0)
    pair_gates = jnp.zeros((N, LANES), F32).at[:, 0].set(gate_hi).at[:, 1].set(gate_lo)
    return dict(pos=pos.reshape(E, 1, N), token_info=token_info, gather_plan=gather_plan,
                rows=(row_hi.astype(i32), row_lo.astype(i32)), pair_gates=pair_gates,
                combine_plan=combine_plan, tile_expert=tile_expert,
                n_active_tiles=n_active_tiles.reshape(1).astype(i32), p_max=p_max, n_items=n_items)


PLAN_BLOCK, PLAN_CHUNK, PLAN_EXPERT, PLAN_FIRST, PLAN_LAST, PLAN_VALID = range(6)


def _moe_gather_kernel(plan_ref, pos_ref, h_ref, xs_ref, acc, *, rb):
    w = pl.program_id(0)

    @pl.when(plan_ref[PLAN_FIRST, w] == 1)
    def _():
        acc[...] = jnp.zeros_like(acc)

    @pl.when(plan_ref[PLAN_VALID, w] == 1)
    def _():
        ch = h_ref.shape[0]
        rows = plan_ref[PLAN_BLOCK, w] * rb + lax.broadcasted_iota(jnp.int32, (rb, ch), 0)
        onehot = jnp.where(pos_ref[0] == rows, 1.0, 0.0).astype(BF16)
        acc[...] += jnp.dot(onehot, h_ref[...], preferred_element_type=F32)

    @pl.when(plan_ref[PLAN_LAST, w] == 1)
    def _():
        xs_ref[...] = acc[...].astype(BF16)


def _moe_gather(h, plan, *, rb, ch):
    N, D = h.shape
    grid_spec = pltpu.PrefetchScalarGridSpec(
        num_scalar_prefetch=1,
        grid=(plan["n_items"],),
        in_specs=[pl.BlockSpec((1, 1, ch), lambda w, p: (p[PLAN_EXPERT, w], 0, p[PLAN_CHUNK, w])),
                  pl.BlockSpec((ch, D), lambda w, p: (p[PLAN_CHUNK, w], 0))],
        out_specs=pl.BlockSpec((rb, D), lambda w, p: (p[PLAN_BLOCK, w], 0)),
        scratch_shapes=[pltpu.VMEM((rb, D), F32)],
    )
    return pl.pallas_call(
        functools.partial(_moe_gather_kernel, rb=rb),
        grid_spec=grid_spec,
        out_shape=jax.ShapeDtypeStruct((plan["p_max"], D), BF16),
        compiler_params=_params("arbitrary"),
        name="moe_gather",
    )(plan["gather_plan"], plan["pos"], h)


def _moe_experts_kernel(te_ref, na_ref, xs_ref, wg_ref, wu_ref, wd_ref, o_ref, acc, *, tf):
    @pl.when(pl.program_id(0) < na_ref[0])
    def _():
        x = xs_ref[...].astype(BF16)
        for c in range(wg_ref.shape[-1] // tf):
            cols = slice(c * tf, (c + 1) * tf)
            hid = _swiglu_hidden(x, wg_ref[0, :, cols], wu_ref[0, :, cols])
            part = jnp.dot(hid, wd_ref[0, cols, :], preferred_element_type=F32)
            if c == 0:
                acc[...] = part
            else:
                acc[...] += part
        o_ref[...] = acc[...]


def _moe_experts(xs, plan, wg, wu, wd, *, tm, tf_target=512):
    P, D = xs.shape
    E, _, F = wg.shape
    tf = _pick_chunk(F, tf_target)
    tile = lambda i, te, na: (jnp.minimum(i, na[0] - 1), 0)
    resident = pl.Buffered(1)
    grid_spec = pltpu.PrefetchScalarGridSpec(
        num_scalar_prefetch=2,
        grid=(P // tm,),
        in_specs=[pl.BlockSpec((tm, D), tile),
                  pl.BlockSpec((1, D, F), lambda i, te, na: (te[i], 0, 0), pipeline_mode=resident),
                  pl.BlockSpec((1, D, F), lambda i, te, na: (te[i], 0, 0), pipeline_mode=resident),
                  pl.BlockSpec((1, F, D), lambda i, te, na: (te[i], 0, 0), pipeline_mode=resident)],
        out_specs=pl.BlockSpec((tm, D), tile),
        scratch_shapes=[pltpu.VMEM((tm, D), F32)],
    )
    return pl.pallas_call(
        functools.partial(_moe_experts_kernel, tf=tf),
        grid_spec=grid_spec,
        out_shape=jax.ShapeDtypeStruct((P, D), F32),
        compiler_params=_params("arbitrary"),
        name="moe_experts",
    )(plan["tile_expert"], plan["n_active_tiles"], xs, wg, wu, wd)


def _moe_combine_kernel(*refs, final, rb, n_experts):
    if final:
        plan_ref, info_ref, ys_ref, x_ref, gfin_ref, o_ref, acc = refs
    else:
        (plan_ref, info_ref, ys_ref, x_ref, o_ref, acc), gfin_ref = refs, None
    w = pl.program_id(0)

    @pl.when(plan_ref[PLAN_FIRST, w] == 1)
    def _():
        acc[...] = jnp.zeros_like(acc)

    @pl.when(plan_ref[PLAN_VALID, w] == 1)
    def _():
        info = info_ref[...]
        ch = info.shape[0]
        e = plan_ref[PLAN_EXPERT, w]
        lane = lax.broadcasted_iota(jnp.int32, info.shape, 1)
        pos = jnp.sum(jnp.where(lane == e, info, 0.0), axis=-1, keepdims=True)
        gate = jnp.sum(jnp.where(lane == e + n_experts, info, 0.0), axis=-1, keepdims=True)
        local = pos - (plan_ref[PLAN_BLOCK, w] * rb).astype(F32)
        cols = lax.broadcasted_iota(jnp.int32, (ch, rb), 1).astype(F32)
        onehot = jnp.where(local == cols, 1.0, 0.0).astype(BF16)
        acc[...] += gate * jnp.dot(onehot, ys_ref[...], preferred_element_type=F32)

    @pl.when(plan_ref[PLAN_LAST, w] == 1)
    def _():
        o_ref[...] = _finish(x_ref[...], acc[...], gfin_ref)


def _moe_combine(ys, x, plan, g_final, *, rb, ch, n_experts):
    N, D = x.shape
    final = g_final is not None
    chunk = lambda c: pl.BlockSpec((ch, c), lambda w, p: (p[PLAN_CHUNK, w], 0))
    in_specs = [chunk(LANES), pl.BlockSpec((rb, D), lambda w, p: (p[PLAN_BLOCK, w], 0)), chunk(D)]
    args = [plan["combine_plan"], plan["token_info"], ys, x]
    if final:
        in_specs.append(pl.BlockSpec((1, D), lambda w, p: (0, 0)))
        args.append(g_final.reshape(1, D))
    grid_spec = pltpu.PrefetchScalarGridSpec(
        num_scalar_prefetch=1,
        grid=(plan["n_items"],),
        in_specs=in_specs,
        out_specs=chunk(D),
        scratch_shapes=[pltpu.VMEM((ch, D), F32)],
    )
    return pl.pallas_call(
        functools.partial(_moe_combine_kernel, final=final, rb=rb, n_experts=n_experts),
        grid_spec=grid_spec,
        out_shape=jax.ShapeDtypeStruct((N, D), F32),
        compiler_params=_params("arbitrary"),
        name="moe_combine",
    )(*args)


def _sc_rows_kernel(n_rows, n_out, d, scatter):
    per_worker = n_rows // SC_WORKERS
    assert n_rows % (SC_WORKERS * SC_WINDOW) == 0
    mesh = plsc.VectorSubcoreMesh(core_axis_name="c", subcore_axis_name="s")
    out_type = (jax.ShapeDtypeStruct((n_out, d), F32) if scatter
                else [jax.ShapeDtypeStruct((n_rows, d), F32)] * 2)

    def body(*refs):
        if scatter:
            src_hbm, hi_hbm, lo_hbm, out_hbm, idx_v, rows_v, sem = refs
        else:
            src_hbm, hi_hbm, lo_hbm, out_hi_hbm, out_lo_hbm, idx_v, rows_v, sem = refs
        worker = lax.axis_index("s") * SC_CORES + lax.axis_index("c")

        @pl.loop(0, per_worker // SC_WINDOW)
        def _(i):
            window = pl.ds(worker * per_worker + i * SC_WINDOW, SC_WINDOW)
            if scatter:
                pltpu.sync_copy(src_hbm.at[window], rows_v)
                for idx_hbm in (hi_hbm, lo_hbm):
                    pltpu.sync_copy(idx_hbm.at[window], idx_v)
                    pltpu.async_copy(rows_v, out_hbm.at[idx_v], sem).wait()
            else:
                for idx_hbm, dst_hbm in ((hi_hbm, out_hi_hbm), (lo_hbm, out_lo_hbm)):
                    pltpu.sync_copy(idx_hbm.at[window], idx_v)
                    pltpu.async_copy(src_hbm.at[idx_v], rows_v, sem).wait()
                    pltpu.sync_copy(rows_v, dst_hbm.at[window])

    return pl.kernel(body, mesh=mesh, out_type=out_type,
                     scratch_types=[pltpu.VMEM((SC_WINDOW,), jnp.int32), pltpu.VMEM((SC_WINDOW, d), F32),
                                    pltpu.SemaphoreType.DMA])


def _moe_mix_kernel(*refs, final):
    if final:
        x_ref, yh_ref, yl_ref, g_ref, gfin_ref, o_ref = refs
    else:
        (x_ref, yh_ref, yl_ref, g_ref, o_ref), gfin_ref = refs, None
    gates = g_ref[...]
    lane = lax.broadcasted_iota(jnp.int32, gates.shape, 1)
    g_hi = jnp.sum(jnp.where(lane == 0, gates, 0.0), axis=-1, keepdims=True)
    g_lo = jnp.sum(jnp.where(lane == 1, gates, 0.0), axis=-1, keepdims=True)
    o_ref[...] = _finish(x_ref[...], g_hi * yh_ref[...] + g_lo * yl_ref[...], gfin_ref)


def _moe_mix(x, y_hi, y_lo, pair_gates, g_final, *, tm=512):
    N, D = x.shape
    final = g_final is not None
    row = lambda c: pl.BlockSpec((tm, c), lambda i: (i, 0))
    in_specs, args = [row(D), row(D), row(D), row(LANES)], [x, y_hi, y_lo, pair_gates]
    if final:
        in_specs.append(pl.BlockSpec((1, D), lambda i: (0, 0)))
        args.append(g_final.reshape(1, D))
    return pl.pallas_call(
        functools.partial(_moe_mix_kernel, final=final),
        grid=(N // tm,), in_specs=in_specs, out_specs=row(D),
        out_shape=jax.ShapeDtypeStruct((N, D), F32),
        compiler_params=_params("parallel"),
        name="moe_mix",
    )(*args)


def _moe(h, x, route, wg, wu, wd, g_final, *, tm=512, rb=256, ch=512):
    N, D = x.shape
    E = wg.shape[0]
    tm, ch = min(tm, N), min(ch, N)
    rb = min(rb, tm)
    assert N % ch == 0 and tm % rb == 0
    plan = _plan_routing(route, E, tm=tm, rb=rb, ch=ch)
    rows_hi, rows_lo = plan["rows"]
    xs = _sc_rows_kernel(N, plan["p_max"], D, scatter=True)(h, rows_hi, rows_lo)
    ys = _moe_experts(xs, plan, wg, wu, wd, tm=tm)
    y_hi, y_lo = _sc_rows_kernel(N, plan["p_max"], D, scatter=False)(ys, rows_hi, rows_lo)
    return _moe_mix(x, y_hi, y_lo, plan["pair_gates"], g_final)


def kernel(x, norm_mix, w_in, w_pool, pool_scale, attn_gain, w_out, norm_ffn, ffn_wg, ffn_wu, ffn_wd,
           w_router, moe_wg, moe_wu, moe_wd, final_norm):
    B, S, D = x.shape
    depth = norm_mix.shape[0]
    bf = lambda t: t.astype(BF16)
    w_in, w_pool, w_out = bf(w_in), bf(w_pool), bf(w_out)
    ffn_wg, ffn_wu, ffn_wd = bf(ffn_wg), bf(ffn_wu), bf(ffn_wd)
    moe_wg, moe_wu, moe_wd = bf(moe_wg), bf(moe_wu), bf(moe_wd)
    R = RESIDUES
    assert S % R == 0
    x = x.reshape(B, S // R, R, D).transpose(0, 2, 1, 3)
    for l in range(depth):
        ya, q, k, v = _mix_in(x, norm_mix[l], w_in[l], w_pool[l], pool_scale[l])
        branches = [_attn_branch(q, k, v, dil) for _, dil in DILATED_PATTERNS]
        x1, h2 = _mix_out(x.reshape(B * S, D), ya.reshape(B * S, -1),
                          [o for o, _ in branches], [lse for _, lse in branches],
                          attn_gain[l], w_out[l], norm_ffn[l],
                          h_dtype=BF16 if l % 2 == 0 else F32)
        g_final = final_norm if l == depth - 1 else None
        i = l // 2
        if l % 2 == 0:
            x2 = _ffn(h2, x1, ffn_wg[i], ffn_wu[i], ffn_wd[i], g_final)
        else:
            route = _router(x1, norm_ffn[l], w_router[i])
            x2 = _moe(h2, x1, route, moe_wg[i], moe_wu[i], moe_wd[i], g_final)
        x = x2.reshape(B, R, S // R, D)
    return x.transpose(0, 2, 1, 3).reshape(B, S, D)
```

```python
import functools

import jax
import jax.numpy as jnp
from jax import lax
from jax.experimental import pallas as pl
from jax.experimental.pallas import tpu as pltpu
from jax.experimental.pallas import tpu_sc as plsc

F32 = jnp.float32
BF16 = jnp.bfloat16

EPS = 1e-6
LANES = 128
HEAD_DIM = 64
POOL_WINDOWS = (2, 4, 8, 16)
POOL_HIST = 8
DILATED_PATTERNS = ((128, 1), (512, 4), (2048, 16))
WINDOW_STEPS = 128
RESIDUES = 16
ATTN_STEP_ROWS = 1024
ATTN_UNROLL = 2
LOG2_E = 1.4426950408889634
SC_CORES, SC_SUBCORES = 2, 16
SC_WORKERS = SC_CORES * SC_SUBCORES
SC_WINDOW = 64
assert all(w // d == WINDOW_STEPS and RESIDUES % d == 0 for w, d in DILATED_PATTERNS)
TOP_K = 2
MASKED = -1e30
VMEM_LIMIT = 48 * 1024 * 1024


def _rms(x, g):
    return x * lax.rsqrt(jnp.mean(x * x, axis=-1, keepdims=True) + EPS) * g


def _params(*sem):
    return pltpu.CompilerParams(dimension_semantics=sem, vmem_limit_bytes=VMEM_LIMIT)


def _mix_in_kernel(x_ref, g_ref, w_ref, wp_ref, ps_ref, ya_ref, q_ref, k_ref, v_ref, ubuf, uprev,
                   *, d_pool, d_attn, ti):
    s = pl.program_id(1)
    R, D = RESIDUES, x_ref.shape[-1]
    rows = R * ti
    blk = lambda t: t.reshape(R, ti, t.shape[-1])
    h = _rms(x_ref[...].reshape(rows, D), g_ref[...]).astype(BF16)
    proj = jnp.dot(h, w_ref[...], preferred_element_type=F32)
    scale = HEAD_DIM ** -0.5 * LOG2_E
    q_ref[...] = blk((proj[:, d_pool:d_pool + d_attn] * scale).astype(BF16))
    k_ref[...] = blk(proj[:, d_pool + d_attn:d_pool + 2 * d_attn].astype(BF16))
    v_ref[...] = blk(proj[:, d_pool + 2 * d_attn:].astype(BF16))

    @pl.when(s == 0)
    def _():
        ubuf[:, 0:POOL_HIST, :] = jnp.zeros((R, POOL_HIST, d_pool), F32)

    ubuf[:, POOL_HIST:POOL_HIST + ti, :] = blk(proj[:, :d_pool])
    uprev[...] = ubuf[:, POOL_HIST - 1:POOL_HIST - 1 + ti, :]
    at_start = (s * ti + lax.broadcasted_iota(jnp.int32, (ti, 1), 0)) == 0
    group = d_pool // len(POOL_WINDOWS)
    zs = []
    for gi, w in enumerate(POOL_WINDOWS):
        cols = slice(gi * group, (gi + 1) * group)
        ds = []
        for r in range(R):
            ug = ubuf[r, POOL_HIST:POOL_HIST + ti, cols]
            win = ug
            for back in range(1, w):
                rr = r - back
                win = win + (ubuf[rr, POOL_HIST:POOL_HIST + ti, cols] if rr >= 0 else uprev[rr + R, :, cols])
            cnt = jnp.where(at_start, float(min(r + 1, w)), float(w))
            ds.append(win / cnt - ug)
        d = jnp.concatenate(ds, axis=0).astype(BF16)
        zs.append(jnp.dot(d, wp_ref[gi], preferred_element_type=F32))
    z = jnp.concatenate(zs, axis=-1)
    ya_ref[...] = blk(_rms(z, ps_ref[...]).astype(BF16))
    ubuf[:, POOL_HIST - 1:POOL_HIST, :] = ubuf[:, POOL_HIST + ti - 1:POOL_HIST + ti, :]


def _mix_in(x, g, w_in, w_pool, pool_scale, *, ti=32):
    B, R, SI, D = x.shape
    d_pool = pool_scale.shape[-1]
    d_in = w_in.shape[-1]
    d_attn = (d_in - d_pool) // 3
    ti = min(ti, SI)
    assert R == RESIDUES >= max(POOL_WINDOWS) and SI % ti == 0 and ti % 16 == 0
    assert d_pool % (LANES * len(POOL_WINDOWS)) == 0
    seq_spec = lambda c: pl.BlockSpec((None, R, ti, c), lambda b, s: (b, 0, s, 0))
    full = lambda shape: pl.BlockSpec(shape, lambda b, s: (0,) * len(shape))
    out_sds = lambda c: jax.ShapeDtypeStruct((B, R, SI, c), BF16)
    return pl.pallas_call(
        functools.partial(_mix_in_kernel, d_pool=d_pool, d_attn=d_attn, ti=ti),
        grid=(B, SI // ti),
        in_specs=[seq_spec(D), full((1, D)), full((D, d_in)), full(w_pool.shape), full((1, d_pool))],
        out_specs=[seq_spec(d_pool), seq_spec(d_attn), seq_spec(d_attn), seq_spec(d_attn)],
        out_shape=[out_sds(d_pool), out_sds(d_attn), out_sds(d_attn), out_sds(d_attn)],
        scratch_shapes=[pltpu.VMEM((R, POOL_HIST + ti, d_pool), F32), pltpu.VMEM((R, ti, d_pool), F32)],
        compiler_params=_params("parallel", "arbitrary"),
        name="mix_in",
    )(x, g.reshape(1, D), w_in, w_pool, pool_scale.reshape(1, d_pool))


def _attn_kernel(q_ref, k_ref, v_ref, o_ref, lse_ref, bias_ref, *, nq, strips, d_attn):
    _, classes, SI, _ = q_ref.shape
    L = nq // strips
    nk, n_blocks = 2 * nq, SI // L
    row = lax.broadcasted_iota(jnp.int32, (nq, 1), 0)
    col = lax.broadcasted_iota(jnp.int32, (1, nk), 1)
    q_strip, q_row = row >> (L.bit_length() - 1), row & (L - 1)
    k_strip, k_row = col >> ((2 * L).bit_length() - 1), col & (2 * L - 1)
    back = strips * (q_row - k_row) + (q_strip - k_strip)

    def band(offset):
        rel = back + strips * offset
        return jnp.where((rel >= 0) & (rel <= WINDOW_STEPS), 0.0, MASKED).astype(F32)

    bias_ref[0] = band(0)
    bias_ref[1] = band(L)
    lane = lax.broadcasted_iota(jnp.int32, (nq, LANES), 1)
    lo_half = lane < HEAD_DIM

    def block(it, carry):
        g, j = it // n_blocks, it % n_blocks
        q_rows = pl.ds(pl.multiple_of(j * L, L), L)
        k_rows = pl.ds(pl.multiple_of(jnp.maximum(j - 1, 0) * L, L), 2 * L)
        bias = bias_ref[jnp.minimum(j, 1)]
        lse_tile = jnp.zeros((nq, LANES), F32)
        for hp in range(d_attn // LANES):
            cols = slice(hp * LANES, (hp + 1) * LANES)
            qp = q_ref[:, g, q_rows, cols].reshape(nq, LANES)
            kk = k_ref[:, g, k_rows, cols].reshape(nk, LANES)
            vv = v_ref[:, g, k_rows, cols].reshape(nk, LANES)
            outs, lses = [], []
            for sub in range(2):
                keep = lo_half if sub == 0 else jnp.logical_not(lo_half)
                qm = jnp.where(keep, qp, jnp.zeros_like(qp))
                s = lax.dot_general(qm, kk, (((1,), (1,)), ((), ())), preferred_element_type=F32) + bias
                m = jnp.max(s, axis=-1, keepdims=True)
                p = jnp.exp2(s - m)
                l = jnp.sum(p, axis=-1, keepdims=True)
                o = jnp.dot(p.astype(BF16), vv, preferred_element_type=F32)
                outs.append(o / l)
                lses.append(m + jnp.log2(l))
            o_pair = jnp.where(lo_half, outs[0], outs[1]).astype(BF16)
            o_ref[:, g, q_rows, cols] = o_pair.reshape(strips, L, LANES)
            pair_lse = jnp.where(lane == 2 * hp, lses[0], lses[1])
            lse_tile = jnp.where((lane >> 1) == hp, pair_lse, lse_tile)
        lse_ref[:, g, q_rows, :] = lse_tile.reshape(strips, L, LANES)
        return carry

    lax.fori_loop(0, classes * n_blocks, block, 0, unroll=ATTN_UNROLL)


def _attn_branch(q, k, v, dil):
    B, R, SI, C = q.shape
    strips = R // dil
    nq = max(WINDOW_STEPS, strips * 16)
    L = nq // strips
    classes = max(1, min(dil, ATTN_STEP_ROWS // (strips * SI)))
    assert R % dil == 0 and SI % L == 0 and SI >= 2 * L and dil % classes == 0
    assert C % LANES == 0 and C // HEAD_DIM <= LANES and nq & (nq - 1) == 0 and L & (L - 1) == 0
    view = lambda t: t.reshape(B, strips, dil, SI, t.shape[-1])
    spec = lambda c: pl.BlockSpec((None, strips, classes, SI, c), lambda b, g: (b, 0, g, 0, 0))
    o, lse = pl.pallas_call(
        functools.partial(_attn_kernel, nq=nq, strips=strips, d_attn=C),
        scratch_shapes=[pltpu.VMEM((2, nq, 2 * nq), F32)],
        grid=(B, dil // classes),
        in_specs=[spec(C)] * 3,
        out_specs=[spec(C), spec(LANES)],
        out_shape=[jax.ShapeDtypeStruct((B, strips, dil, SI, C), BF16),
                   jax.ShapeDtypeStruct((B, strips, dil, SI, LANES), F32)],
        compiler_params=_params("parallel", "parallel"),
        name=f"attn_d{dil}",
    )(view(q), view(k), view(v))
    return o.reshape(B * R * SI, C), lse.reshape(B * R * SI, LANES)


def _mix_out_kernel(x_ref, ya_ref, o1_ref, o2_ref, o3_ref, l1_ref, l2_ref, l3_ref, gain_ref, wo_ref,
                    gffn_ref, exp_ref, x1_ref, h2_ref, *, d_pool):
    lses = [l1_ref[...], l2_ref[...], l3_ref[...]]
    top = jnp.maximum(jnp.maximum(lses[0], lses[1]), lses[2])
    es = [jnp.exp2(l - top) for l in lses]
    den = es[0] + es[1] + es[2]
    expand = exp_ref[...]

    def per_lane(w):
        hi = w.astype(BF16)
        lo = (w - hi.astype(F32)).astype(BF16)
        return (jnp.dot(hi, expand, preferred_element_type=F32)
                + jnp.dot(lo, expand, preferred_element_type=F32))

    o = jnp.zeros(o1_ref.shape, F32)
    for e, o_ref in zip(es, (o1_ref, o2_ref, o3_ref)):
        o = o + per_lane(e / den) * o_ref[...].astype(F32)
    yb = _rms(o, gain_ref[...]).astype(BF16)
    y = (jnp.dot(ya_ref[...], wo_ref[:d_pool, :], preferred_element_type=F32)
         + jnp.dot(yb, wo_ref[d_pool:, :], preferred_element_type=F32))
    x1 = x_ref[...] + y
    x1_ref[...] = x1
    h2_ref[...] = _rms(x1, gffn_ref[...]).astype(h2_ref.dtype)


def _mix_out(x, ya, outs, lses, attn_gain, w_out, g_ffn, *, h_dtype, tm=512):
    N, D = x.shape
    d_pool = ya.shape[-1]
    d_attn = outs[0].shape[-1]
    tm = min(tm, N)
    assert N % tm == 0
    head_of_lane = jnp.arange(d_attn) // HEAD_DIM
    expand = (jnp.arange(LANES)[:, None] == head_of_lane[None, :]).astype(BF16)
    row = lambda c: pl.BlockSpec((tm, c), lambda i: (i, 0))
    full = lambda shape: pl.BlockSpec(shape, lambda i: (0,) * len(shape))
    return pl.pallas_call(
        functools.partial(_mix_out_kernel, d_pool=d_pool),
        grid=(N // tm,),
        in_specs=[row(D), row(d_pool)] + [row(d_attn)] * 3 + [row(LANES)] * 3
                 + [full((1, d_attn)), full(w_out.shape), full((1, D)), full(expand.shape)],
        out_specs=[row(D), row(D)],
        out_shape=[jax.ShapeDtypeStruct((N, D), F32), jax.ShapeDtypeStruct((N, D), h_dtype)],
        compiler_params=_params("parallel"),
        name="mix_out",
    )(x, ya, *outs, *lses, attn_gain.reshape(1, d_attn), w_out, g_ffn.reshape(1, D), expand)


def _swiglu_hidden(h, wg, wu):
    a = jnp.dot(h, wg, preferred_element_type=F32)
    b = jnp.dot(h, wu, preferred_element_type=F32)
    return (a * jax.nn.sigmoid(a) * b).astype(BF16)


def _finish(x, y, gfin_ref):
    out = x + y
    return out if gfin_ref is None else _rms(out, gfin_ref[...])


def _ffn_kernel(*refs, final):
    if final:
        h_ref, x_ref, wg_ref, wu_ref, wd_ref, gfin_ref, o_ref, acc = refs
    else:
        (h_ref, x_ref, wg_ref, wu_ref, wd_ref, o_ref, acc), gfin_ref = refs, None
    f = pl.program_id(1)

    @pl.when(f == 0)
    def _():
        acc[...] = jnp.zeros_like(acc)

    hid = _swiglu_hidden(h_ref[...], wg_ref[...], wu_ref[...])
    acc[...] += jnp.dot(hid, wd_ref[...], preferred_element_type=F32)

    @pl.when(f == pl.num_programs(1) - 1)
    def _():
        o_ref[...] = _finish(x_ref[...], acc[...], gfin_ref)


def _pick_chunk(n, target):
    best = None
    for c in range(LANES, min(n, target) + 1, LANES):
        if n % c == 0:
            best = c
    assert best is not None
    return best


def _ffn(h, x, wg, wu, wd, g_final, *, tm=512, tf_target=1536):
    N, D = x.shape
    F = wg.shape[-1]
    tm = min(tm, N)
    tf = _pick_chunk(F, tf_target)
    final = g_final is not None
    row = pl.BlockSpec((tm, D), lambda i, f: (i, 0))
    in_specs = [row, row,
                pl.BlockSpec((D, tf), lambda i, f: (0, f)),
                pl.BlockSpec((D, tf), lambda i, f: (0, f)),
                pl.BlockSpec((tf, D), lambda i, f: (f, 0))]
    args = [h, x, wg, wu, wd]
    if final:
        in_specs.append(pl.BlockSpec((1, D), lambda i, f: (0, 0)))
        args.append(g_final.reshape(1, D))
    return pl.pallas_call(
        functools.partial(_ffn_kernel, final=final),
        grid=(N // tm, F // tf),
        in_specs=in_specs,
        out_specs=row,
        out_shape=jax.ShapeDtypeStruct((N, D), F32),
        scratch_shapes=[pltpu.VMEM((tm, D), F32)],
        compiler_params=_params("parallel", "arbitrary"),
        name="ffn_dense",
    )(*args)


def _router_kernel(x_ref, g_ref, wr_ref, gate_ref, *, n_experts):
    h = _rms(x_ref[...], g_ref[...])
    logits = jnp.dot(h, wr_ref[...], preferred_element_type=F32, precision=lax.Precision.HIGHEST)
    lane = lax.broadcasted_iota(jnp.int32, logits.shape, 1)
    logits = jnp.where(lane < n_experts, logits, -jnp.inf)
    picks = []
    for _ in range(TOP_K):
        m = jnp.max(logits, axis=-1, keepdims=True)
        idx = jnp.min(jnp.where(logits == m, lane, LANES), axis=-1, keepdims=True)
        picks.append((m, idx))
        logits = jnp.where(lane == idx, -jnp.inf, logits)
    (m1, i1), (m2, i2) = picks
    e2 = jnp.exp(m2 - m1)
    g1 = 1.0 / (1.0 + e2)
    g2 = e2 / (1.0 + e2)
    picked = jnp.logical_or(lane == i1 + n_experts, lane == i2 + n_experts)
    gate_ref[...] = jnp.where(lane == i1, g1, jnp.where(lane == i2, g2, jnp.where(picked, 1.0, 0.0)))


def _router(x, g_ffn, w_router, *, tm=512):
    N, D = x.shape
    E = w_router.shape[-1]
    assert 2 * E <= LANES and TOP_K == 2
    tm = min(tm, N)
    wr = jnp.zeros((D, LANES), F32).at[:, :E].set(w_router)
    return pl.pallas_call(
        functools.partial(_router_kernel, n_experts=E),
        grid=(N // tm,),
        in_specs=[pl.BlockSpec((tm, D), lambda i: (i, 0)),
                  pl.BlockSpec((1, D), lambda i: (0, 0)),
                  pl.BlockSpec((D, LANES), lambda i: (0, 0))],
        out_specs=pl.BlockSpec((tm, LANES), lambda i: (i, 0)),
        out_shape=jax.ShapeDtypeStruct((N, LANES), F32),
        compiler_params=_params("parallel"),
        name="router",
    )(x, g_ffn.reshape(1, D), wr)


def _plan_routing(route, n_experts, *, tm, rb, ch):
    N = route.shape[0]
    E = n_experts
    i32 = jnp.int32
    n_chunks = N // ch
    p_max = TOP_K * N + E * tm
    n_blocks, n_tiles = p_max // rb, p_max // tm
    n_items = n_blocks + E * n_chunks

    gate = route[:, :E].T
    sel = (route[:, E:2 * E].T > 0.5).astype(i32)
    rank = jnp.cumsum(sel, axis=1) - sel
    chunk_cnt = sel.reshape(E, n_chunks, ch).sum(-1)
    cum = jnp.concatenate([jnp.zeros((E, 1), i32), jnp.cumsum(chunk_cnt, axis=1)], axis=1)
    cnt = cum[:, -1]
    seg_len = (cnt + tm - 1) // tm * tm
    seg_end = jnp.cumsum(seg_len)
    seg_start = seg_end - seg_len
    pos = jnp.where(sel > 0, seg_start[:, None] + rank, -1)
    n_active_tiles = seg_end[-1] // tm
    tile_ids = jnp.arange(n_tiles, dtype=i32)
    tile_expert = jnp.minimum(jnp.sum(seg_end[None, :] <= tile_ids[:, None] * tm, axis=1), E - 1).astype(i32)

    def expand(counts, n_out):
        ends = jnp.cumsum(counts)
        total = ends[-1]
        w = jnp.minimum(jnp.arange(n_out, dtype=i32), total - 1)
        grp = jnp.sum(ends[None, :] <= w[:, None], axis=1).astype(i32)
        off = w - (ends[grp] - counts[grp])
        return grp, off, jnp.arange(n_out, dtype=i32) < total, w

    blk = jnp.arange(n_blocks, dtype=i32)
    b_exp = tile_expert[blk * rb // tm]
    lo = blk * rb - seg_start[b_exp]
    hi = jnp.minimum(lo + rb, cnt[b_exp])
    nonempty = lo < cnt[b_exp]
    b_cum = jnp.sum(jnp.where(b_exp[:, None, None] == jnp.arange(E, dtype=i32)[None, :, None], cum[None], 0),
                    axis=1)
    c_lo = jnp.sum(b_cum[:, 1:] <= lo[:, None], axis=1)
    c_hi = jnp.sum(b_cum[:, :-1] < hi[:, None], axis=1) - 1
    b_items = jnp.where(blk * rb < n_active_tiles * tm, jnp.where(nonempty, c_hi - c_lo + 1, 1), 0).astype(i32)
    c_lo = jnp.where(nonempty, c_lo, 0)
    g_blk, g_off, g_valid, _ = expand(b_items, n_items)
    g_first = jnp.logical_and(g_valid, g_off == 0)
    g_last = jnp.logical_and(g_valid, g_off == b_items[g_blk] - 1)
    gather_plan = jnp.stack([g_blk, c_lo[g_blk] + g_off, b_exp[g_blk],
                             g_first, g_last, g_valid]).astype(i32)

    first_row = seg_start[None, :] + cum[:, :-1].T
    last_row = seg_start[None, :] + cum[:, 1:].T - 1
    ce_items = jnp.where(chunk_cnt.T > 0, last_row // rb - first_row // rb + 1, 0).reshape(-1).astype(i32)
    c_grp, c_off, c_valid, c_w = expand(ce_items, n_items)
    c_chunk = c_grp // E
    c_blk = (first_row // rb).reshape(-1)[c_grp] + c_off
    per_chunk = ce_items.reshape(n_chunks, E).sum(1)
    chunk_end = jnp.cumsum(per_chunk)
    c_first = jnp.logical_and(c_valid, c_w == (chunk_end - per_chunk)[c_chunk])
    c_last = jnp.logical_and(c_valid, c_w == chunk_end[c_chunk] - 1)
    combine_plan = jnp.stack([c_blk, c_chunk, c_grp % E, c_first, c_last, c_valid]).astype(i32)

    token_info = jnp.concatenate([pos.T.astype(F32), gate.T, jnp.zeros((N, LANES - 2 * E), F32)], axis=1)
    row_hi = jnp.max(pos, axis=0)
    row_lo = jnp.sum(jnp.where(sel > 0, pos, 0), axis=0) - row_hi
    gate_hi = jnp.sum(jnp.where(pos == row_hi[None, :], gate, 0.0), axis=0)
    gate_lo = jnp.sum(jnp.where(pos == row_lo[None, :], gate, 0.0), axis=0)
    pair_gates = jnp.concatenate([gate_hi[:, None], gate_lo[:, None], jnp.zeros((N, LANES - 2), F32)], axis=1)
    return dict(pos=pos.reshape(E, 1, N), token_info=token_info, gather_plan=gather_plan,
                rows=(row_hi.astype(i32), row_lo.astype(i32)), pair_gates=pair_gates,
                combine_plan=combine_plan, tile_expert=tile_expert,
                n_active_tiles=n_active_tiles.reshape(1).astype(i32), p_max=p_max, n_items=n_items)


PLAN_BLOCK, PLAN_CHUNK, PLAN_EXPERT, PLAN_FIRST, PLAN_LAST, PLAN_VALID = range(6)


def _moe_gather_kernel(plan_ref, pos_ref, h_ref, xs_ref, acc, *, rb):
    w = pl.program_id(0)

    @pl.when(plan_ref[PLAN_FIRST, w] == 1)
    def _():
        acc[...] = jnp.zeros_like(acc)

    @pl.when(plan_ref[PLAN_VALID, w] == 1)
    def _():
        ch = h_ref.shape[0]
        rows = plan_ref[PLAN_BLOCK, w] * rb + lax.broadcasted_iota(jnp.int32, (rb, ch), 0)
        onehot = jnp.where(pos_ref[0] == rows, 1.0, 0.0).astype(BF16)
        acc[...] += jnp.dot(onehot, h_ref[...], preferred_element_type=F32)

    @pl.when(plan_ref[PLAN_LAST, w] == 1)
    def _():
        xs_ref[...] = acc[...].astype(BF16)


def _moe_gather(h, plan, *, rb, ch):
    N, D = h.shape
    grid_spec = pltpu.PrefetchScalarGridSpec(
        num_scalar_prefetch=1,
        grid=(plan["n_items"],),
        in_specs=[pl.BlockSpec((1, 1, ch), lambda w, p: (p[PLAN_EXPERT, w], 0, p[PLAN_CHUNK, w])),
                  pl.BlockSpec((ch, D), lambda w, p: (p[PLAN_CHUNK, w], 0))],
        out_specs=pl.BlockSpec((rb, D), lambda w, p: (p[PLAN_BLOCK, w], 0)),
        scratch_shapes=[pltpu.VMEM((rb, D), F32)],
    )
    return pl.pallas_call(
        functools.partial(_moe_gather_kernel, rb=rb),
        grid_spec=grid_spec,
        out_shape=jax.ShapeDtypeStruct((plan["p_max"], D), BF16),
        compiler_params=_params("arbitrary"),
        name="moe_gather",
    )(plan["gather_plan"], plan["pos"], h)


def _moe_experts_kernel(te_ref, na_ref, xs_ref, wg_ref, wu_ref, wd_ref, o_ref, acc, *, tf):
    @pl.when(pl.program_id(0) < na_ref[0])
    def _():
        x = xs_ref[...].astype(BF16)
        for c in range(wg_ref.shape[-1] // tf):
            cols = slice(c * tf, (c + 1) * tf)
            hid = _swiglu_hidden(x, wg_ref[0, :, cols], wu_ref[0, :, cols])
            part = jnp.dot(hid, wd_ref[0, cols, :], preferred_element_type=F32)
            if c == 0:
                acc[...] = part
            else:
                acc[...] += part
        o_ref[...] = acc[...]


def _moe_experts(xs, plan, wg, wu, wd, *, tm, tf_target=512):
    P, D = xs.shape
    E, _, F = wg.shape
    tf = _pick_chunk(F, tf_target)
    tile = lambda i, te, na: (jnp.minimum(i, na[0] - 1), 0)
    resident = pl.Buffered(1)
    grid_spec = pltpu.PrefetchScalarGridSpec(
        num_scalar_prefetch=2,
        grid=(P // tm,),
        in_specs=[pl.BlockSpec((tm, D), tile),
                  pl.BlockSpec((1, D, F), lambda i, te, na: (te[i], 0, 0), pipeline_mode=resident),
                  pl.BlockSpec((1, D, F), lambda i, te, na: (te[i], 0, 0), pipeline_mode=resident),
                  pl.BlockSpec((1, F, D), lambda i, te, na: (te[i], 0, 0), pipeline_mode=resident)],
        out_specs=pl.BlockSpec((tm, D), tile),
        scratch_shapes=[pltpu.VMEM((tm, D), F32)],
    )
    return pl.pallas_call(
        functools.partial(_moe_experts_kernel, tf=tf),
        grid_spec=grid_spec,
        out_shape=jax.ShapeDtypeStruct((P, D), F32),
        compiler_params=_params("arbitrary"),
        name="moe_experts",
    )(plan["tile_expert"], plan["n_active_tiles"], xs, wg, wu, wd)


def _moe_combine_kernel(*refs, final, rb, n_experts):
    if final:
        plan_ref, info_ref, ys_ref, x_ref, gfin_ref, o_ref, acc = refs
    else:
        (plan_ref, info_ref, ys_ref, x_ref, o_ref, acc), gfin_ref = refs, None
    w = pl.program_id(0)

    @pl.when(plan_ref[PLAN_FIRST, w] == 1)
    def _():
        acc[...] = jnp.zeros_like(acc)

    @pl.when(plan_ref[PLAN_VALID, w] == 1)
    def _():
        info = info_ref[...]
        ch = info.shape[0]
        e = plan_ref[PLAN_EXPERT, w]
        lane = lax.broadcasted_iota(jnp.int32, info.shape, 1)
        pos = jnp.sum(jnp.where(lane == e, info, 0.0), axis=-1, keepdims=True)
        gate = jnp.sum(jnp.where(lane == e + n_experts, info, 0.0), axis=-1, keepdims=True)
        local = pos - (plan_ref[PLAN_BLOCK, w] * rb).astype(F32)
        cols = lax.broadcasted_iota(jnp.int32, (ch, rb), 1).astype(F32)
        onehot = jnp.where(local == cols, 1.0, 0.0).astype(BF16)
        acc[...] += gate * jnp.dot(onehot, ys_ref[...], preferred_element_type=F32)

    @pl.when(plan_ref[PLAN_LAST, w] == 1)
    def _():
        o_ref[...] = _finish(x_ref[...], acc[...], gfin_ref)


def _moe_combine(ys, x, plan, g_final, *, rb, ch, n_experts):
    N, D = x.shape
    final = g_final is not None
    chunk = lambda c: pl.BlockSpec((ch, c), lambda w, p: (p[PLAN_CHUNK, w], 0))
    in_specs = [chunk(LANES), pl.BlockSpec((rb, D), lambda w, p: (p[PLAN_BLOCK, w], 0)), chunk(D)]
    args = [plan["combine_plan"], plan["token_info"], ys, x]
    if final:
        in_specs.append(pl.BlockSpec((1, D), lambda w, p: (0, 0)))
        args.append(g_final.reshape(1, D))
    grid_spec = pltpu.PrefetchScalarGridSpec(
        num_scalar_prefetch=1,
        grid=(plan["n_items"],),
        in_specs=in_specs,
        out_specs=chunk(D),
        scratch_shapes=[pltpu.VMEM((ch, D), F32)],
    )
    return pl.pallas_call(
        functools.partial(_moe_combine_kernel, final=final, rb=rb, n_experts=n_experts),
        grid_spec=grid_spec,
        out_shape=jax.ShapeDtypeStruct((N, D), F32),
        compiler_params=_params("arbitrary"),
        name="moe_combine",
    )(*args)


def _sc_rows_kernel(n_rows, n_out, d, scatter):
    per_worker = n_rows // SC_WORKERS
    assert n_rows % (SC_WORKERS * SC_WINDOW) == 0
    mesh = plsc.VectorSubcoreMesh(core_axis_name="c", subcore_axis_name="s")
    out_type = (jax.ShapeDtypeStruct((n_out, d), F32) if scatter
                else [jax.ShapeDtypeStruct((n_rows, d), F32)] * 2)

    def body(*refs):
        if scatter:
            src_hbm, hi_hbm, lo_hbm, out_hbm, idx_v, rows_v, sem = refs
        else:
            src_hbm, hi_hbm, lo_hbm, out_hi_hbm, out_lo_hbm, idx_v, rows_v, sem = refs
        worker = lax.axis_index("s") * SC_CORES + lax.axis_index("c")

        @pl.loop(0, per_worker // SC_WINDOW)
        def _(i):
            window = pl.ds(worker * per_worker + i * SC_WINDOW, SC_WINDOW)
            if scatter:
                pltpu.sync_copy(src_hbm.at[window], rows_v)
                for idx_hbm in (hi_hbm, lo_hbm):
                    pltpu.sync_copy(idx_hbm.at[window], idx_v)
                    pltpu.async_copy(rows_v, out_hbm.at[idx_v], sem).wait()
            else:
                for idx_hbm, dst_hbm in ((hi_hbm, out_hi_hbm), (lo_hbm, out_lo_hbm)):
                    pltpu.sync_copy(idx_hbm.at[window], idx_v)
                    pltpu.async_copy(src_hbm.at[idx_v], rows_v, sem).wait()
                    pltpu.sync_copy(rows_v, dst_hbm.at[window])

    return pl.kernel(body, mesh=mesh, out_type=out_type,
                     scratch_types=[pltpu.VMEM((SC_WINDOW,), jnp.int32), pltpu.VMEM((SC_WINDOW, d), F32),
                                    pltpu.SemaphoreType.DMA])


def _moe_mix_kernel(*refs, final):
    if final:
        x_ref, yh_ref, yl_ref, g_ref, gfin_ref, o_ref = refs
    else:
        (x_ref, yh_ref, yl_ref, g_ref, o_ref), gfin_ref = refs, None
    gates = g_ref[...]
    lane = lax.broadcasted_iota(jnp.int32, gates.shape, 1)
    g_hi = jnp.sum(jnp.where(lane == 0, gates, 0.0), axis=-1, keepdims=True)
    g_lo = jnp.sum(jnp.where(lane == 1, gates, 0.0), axis=-1, keepdims=True)
    o_ref[...] = _finish(x_ref[...], g_hi * yh_ref[...] + g_lo * yl_ref[...], gfin_ref)


def _moe_mix(x, y_hi, y_lo, pair_gates, g_final, *, tm=512):
    N, D = x.shape
    final = g_final is not None
    row = lambda c: pl.BlockSpec((tm, c), lambda i: (i, 0))
    in_specs, args = [row(D), row(D), row(D), row(LANES)], [x, y_hi, y_lo, pair_gates]
    if final:
        in_specs.append(pl.BlockSpec((1, D), lambda i: (0, 0)))
        args.append(g_final.reshape(1, D))
    return pl.pallas_call(
        functools.partial(_moe_mix_kernel, final=final),
        grid=(N // tm,), in_specs=in_specs, out_specs=row(D),
        out_shape=jax.ShapeDtypeStruct((N, D), F32),
        compiler_params=_params("parallel"),
        name="moe_mix",
    )(*args)


def _moe(h, x, route, wg, wu, wd, g_final, *, tm=512, rb=256, ch=512):
    N, D = x.shape
    E = wg.shape[0]
    tm, ch = min(tm, N), min(ch, N)
    rb = min(rb, tm)
    assert N % ch == 0 and tm % rb == 0
    plan = _plan_routing(route, E, tm=tm, rb=rb, ch=ch)
    rows_hi, rows_lo = plan["rows"]
    xs = _sc_rows_kernel(N, plan["p_max"], D, scatter=True)(h, rows_hi, rows_lo)
    ys = _moe_experts(xs, plan, wg, wu, wd, tm=tm)
    y_hi, y_lo = _sc_rows_kernel(N, plan["p_max"], D, scatter=False)(ys, rows_hi, rows_lo)
    return _moe_mix(x, y_hi, y_lo, plan["pair_gates"], g_final)


def kernel(x, norm_mix, w_in, w_pool, pool_scale, attn_gain, w_out, norm_ffn, ffn_wg, ffn_wu, ffn_wd,
           w_router, moe_wg, moe_wu, moe_wd, final_norm):
    B, S, D = x.shape
    depth = norm_mix.shape[0]
    bf = lambda t: t.astype(BF16)
    w_in, w_pool, w_out = bf(w_in), bf(w_pool), bf(w_out)
    ffn_wg, ffn_wu, ffn_wd = bf(ffn_wg), bf(ffn_wu), bf(ffn_wd)
    moe_wg, moe_wu, moe_wd = bf(moe_wg), bf(moe_wu), bf(moe_wd)
    R = RESIDUES
    assert S % R == 0
    x = x.reshape(B, S // R, R, D).transpose(0, 2, 1, 3)
    for l in range(depth):
        ya, q, k, v = _mix_in(x, norm_mix[l], w_in[l], w_pool[l], pool_scale[l])
        branches = [_attn_branch(q, k, v, dil) for _, dil in DILATED_PATTERNS]
        x1, h2 = _mix_out(x.reshape(B * S, D), ya.reshape(B * S, -1),
                          [o for o, _ in branches], [lse for _, lse in branches],
                          attn_gain[l], w_out[l], norm_ffn[l],
                          h_dtype=BF16 if l % 2 == 0 else F32)
        g_final = final_norm if l == depth - 1 else None
        i = l // 2
        if l % 2 == 0:
            x2 = _ffn(h2, x1, ffn_wg[i], ffn_wu[i], ffn_wd[i], g_final)
        else:
            route = _router(x1, norm_ffn[l], w_router[i])
            x2 = _moe(h2, x1, route, moe_wg[i], moe_wu[i], moe_wd[i], g_final)
        x = x2.reshape(B, R, S // R, D)
    return x.transpose(0, 2, 1, 3).reshape(B, S, D)
```

```python
import functools

import jax
import jax.numpy as jnp
from jax import lax
from jax.experimental import pallas as pl
from jax.experimental.pallas import tpu as pltpu
from jax.experimental.pallas import tpu_sc as plsc

F32 = jnp.float32
BF16 = jnp.bfloat16

EPS = 1e-6
LANES = 128
HEAD_DIM = 64
POOL_WINDOWS = (2, 4, 8, 16)
POOL_HIST = 8
DILATED_PATTERNS = ((128, 1), (512, 4), (2048, 16))
WINDOW_STEPS = 128
RESIDUES = 16
ATTN_STEP_ROWS = 1024
ATTN_UNROLL_QUERIES = 512
LOG2_E = 1.4426950408889634
SC_CORES, SC_SUBCORES = 2, 16
SC_WORKERS = SC_CORES * SC_SUBCORES
SC_WINDOW = 32
assert all(w // d == WINDOW_STEPS and RESIDUES % d == 0 for w, d in DILATED_PATTERNS)
TOP_K = 2
MASKED = -1e30
VMEM_LIMIT = 48 * 1024 * 1024


def _rms(x, g):
    return x * lax.rsqrt(jnp.mean(x * x, axis=-1, keepdims=True) + EPS) * g


def _params(*sem):
    return pltpu.CompilerParams(dimension_semantics=sem, vmem_limit_bytes=VMEM_LIMIT)


def _mix_in_kernel(x_ref, g_ref, w_ref, wp_ref, ps_ref, ya_ref, q_ref, k_ref, v_ref, ubuf, uprev,
                   *, d_pool, d_attn, ti):
    s = pl.program_id(1)
    R, D = RESIDUES, x_ref.shape[-1]
    rows = R * ti
    blk = lambda t: t.reshape(R, ti, t.shape[-1])
    h = _rms(x_ref[...].reshape(rows, D), g_ref[...]).astype(BF16)
    proj = jnp.dot(h, w_ref[...], preferred_element_type=F32)
    scale = HEAD_DIM ** -0.5 * LOG2_E
    q_ref[...] = blk((proj[:, d_pool:d_pool + d_attn] * scale).astype(BF16))
    k_ref[...] = blk(proj[:, d_pool + d_attn:d_pool + 2 * d_attn].astype(BF16))
    v_ref[...] = blk(proj[:, d_pool + 2 * d_attn:].astype(BF16))

    @pl.when(s == 0)
    def _():
        ubuf[:, 0:POOL_HIST, :] = jnp.zeros((R, POOL_HIST, d_pool), F32)

    ubuf[:, POOL_HIST:POOL_HIST + ti, :] = blk(proj[:, :d_pool])
    uprev[...] = ubuf[:, POOL_HIST - 1:POOL_HIST - 1 + ti, :]
    at_start = (s * ti + lax.broadcasted_iota(jnp.int32, (ti, 1), 0)) == 0
    group = d_pool // len(POOL_WINDOWS)
    zs = []
    for gi, w in enumerate(POOL_WINDOWS):
        cols = slice(gi * group, (gi + 1) * group)
        ds = []
        for r in range(R):
            ug = ubuf[r, POOL_HIST:POOL_HIST + ti, cols]
            win = ug
            for back in range(1, w):
                rr = r - back
                win = win + (ubuf[rr, POOL_HIST:POOL_HIST + ti, cols] if rr >= 0 else uprev[rr + R, :, cols])
            cnt = jnp.where(at_start, float(min(r + 1, w)), float(w))
            ds.append(win / cnt - ug)
        d = jnp.concatenate(ds, axis=0).astype(BF16)
        zs.append(jnp.dot(d, wp_ref[gi], preferred_element_type=F32))
    z = jnp.concatenate(zs, axis=-1)
    ya_ref[...] = blk(_rms(z, ps_ref[...]).astype(BF16))
    ubuf[:, POOL_HIST - 1:POOL_HIST, :] = ubuf[:, POOL_HIST + ti - 1:POOL_HIST + ti, :]


def _mix_in(x, g, w_in, w_pool, pool_scale, *, ti=32):
    B, R, SI, D = x.shape
    d_pool = pool_scale.shape[-1]
    d_in = w_in.shape[-1]
    d_attn = (d_in - d_pool) // 3
    ti = min(ti, SI)
    assert R == RESIDUES >= max(POOL_WINDOWS) and SI % ti == 0 and ti % 16 == 0
    assert d_pool % (LANES * len(POOL_WINDOWS)) == 0
    seq_spec = lambda c: pl.BlockSpec((None, R, ti, c), lambda b, s: (b, 0, s, 0))
    full = lambda shape: pl.BlockSpec(shape, lambda b, s: (0,) * len(shape))
    out_sds = lambda c: jax.ShapeDtypeStruct((B, R, SI, c), BF16)
    return pl.pallas_call(
        functools.partial(_mix_in_kernel, d_pool=d_pool, d_attn=d_attn, ti=ti),
        grid=(B, SI // ti),
        in_specs=[seq_spec(D), full((1, D)), full((D, d_in)), full(w_pool.shape), full((1, d_pool))],
        out_specs=[seq_spec(d_pool), seq_spec(d_attn), seq_spec(d_attn), seq_spec(d_attn)],
        out_shape=[out_sds(d_pool), out_sds(d_attn), out_sds(d_attn), out_sds(d_attn)],
        scratch_shapes=[pltpu.VMEM((R, POOL_HIST + ti, d_pool), F32), pltpu.VMEM((R, ti, d_pool), F32)],
        compiler_params=_params("parallel", "arbitrary"),
        name="mix_in",
    )(x, g.reshape(1, D), w_in, w_pool, pool_scale.reshape(1, d_pool))


def _attn_kernel(q_ref, k_ref, v_ref, o_ref, lse_ref, bias_ref, *, nq, strips, d_attn):
    _, classes, SI, _ = q_ref.shape
    L = nq // strips
    nk, n_blocks = 2 * nq, SI // L
    row = lax.broadcasted_iota(jnp.int32, (nq, 1), 0)
    col = lax.broadcasted_iota(jnp.int32, (1, nk), 1)
    q_strip, q_row = row >> (L.bit_length() - 1), row & (L - 1)
    k_strip, k_row = col >> ((2 * L).bit_length() - 1), col & (2 * L - 1)
    back = strips * (q_row - k_row) + (q_strip - k_strip)

    def band(offset):
        rel = back + strips * offset
        return jnp.where((rel >= 0) & (rel <= WINDOW_STEPS), 0.0, MASKED).astype(F32)

    bias_ref[0] = band(0)
    bias_ref[1] = band(L)
    lane = lax.broadcasted_iota(jnp.int32, (nq, LANES), 1)
    lo_half = lane < HEAD_DIM

    def block(it, carry):
        g, j = it // n_blocks, it % n_blocks
        q_rows = pl.ds(pl.multiple_of(j * L, L), L)
        k_rows = pl.ds(pl.multiple_of(jnp.maximum(j - 1, 0) * L, L), 2 * L)
        bias = bias_ref[jnp.minimum(j, 1)]
        lse_tile = jnp.zeros((nq, LANES), F32)
        for hp in range(d_attn // LANES):
            cols = slice(hp * LANES, (hp + 1) * LANES)
            qp = q_ref[:, g, q_rows, cols].reshape(nq, LANES)
            kk = k_ref[:, g, k_rows, cols].reshape(nk, LANES)
            vv = v_ref[:, g, k_rows, cols].reshape(nk, LANES)
            outs, lses = [], []
            for sub in range(2):
                keep = lo_half if sub == 0 else jnp.logical_not(lo_half)
                qm = jnp.where(keep, qp, jnp.zeros_like(qp))
                s = lax.dot_general(qm, kk, (((1,), (1,)), ((), ())), preferred_element_type=F32) + bias
                m = jnp.max(s, axis=-1, keepdims=True)
                p = jnp.exp2(s - m)
                l = jnp.sum(p, axis=-1, keepdims=True)
                o = jnp.dot(p.astype(BF16), vv, preferred_element_type=F32)
                outs.append(o / l)
                lses.append(m + jnp.log2(l))
            o_pair = jnp.where(lo_half, outs[0], outs[1]).astype(BF16)
            o_ref[:, g, q_rows, cols] = o_pair.reshape(strips, L, LANES)
            pair_lse = jnp.where(lane == 2 * hp, lses[0], lses[1])
            lse_tile = jnp.where((lane >> 1) == hp, pair_lse, lse_tile)
        lse_ref[:, g, q_rows, :] = lse_tile.reshape(strips, L, LANES)
        return carry

    lax.fori_loop(0, classes * n_blocks, block, 0, unroll=ATTN_UNROLL_QUERIES // nq)


def _attn_branch(q, k, v, dil):
    B, R, SI, C = q.shape
    strips = R // dil
    nq = max(WINDOW_STEPS, strips * 16)
    L = nq // strips
    classes = max(1, min(dil, ATTN_STEP_ROWS // (strips * SI)))
    assert R % dil == 0 and SI % L == 0 and SI >= 2 * L and dil % classes == 0
    assert C % LANES == 0 and C // HEAD_DIM <= LANES and nq & (nq - 1) == 0 and L & (L - 1) == 0
    view = lambda t: t.reshape(B, strips, dil, SI, t.shape[-1])
    spec = lambda c: pl.BlockSpec((None, strips, classes, SI, c), lambda b, g: (b, 0, g, 0, 0))
    o, lse = pl.pallas_call(
        functools.partial(_attn_kernel, nq=nq, strips=strips, d_attn=C),
        scratch_shapes=[pltpu.VMEM((2, nq, 2 * nq), F32)],
        grid=(B, dil // classes),
        in_specs=[spec(C)] * 3,
        out_specs=[spec(C), spec(LANES)],
        out_shape=[jax.ShapeDtypeStruct((B, strips, dil, SI, C), BF16),
                   jax.ShapeDtypeStruct((B, strips, dil, SI, LANES), F32)],
        compiler_params=_params("parallel", "parallel"),
        name=f"attn_d{dil}",
    )(view(q), view(k), view(v))
    return o.reshape(B * R * SI, C), lse.reshape(B * R * SI, LANES)


def _mix_out_kernel(x_ref, ya_ref, o1_ref, o2_ref, o3_ref, l1_ref, l2_ref, l3_ref, gain_ref, wo_ref,
                    gffn_ref, exp_ref, x1_ref, h2_ref, *, d_pool):
    lses = [l1_ref[...], l2_ref[...], l3_ref[...]]
    top = jnp.maximum(jnp.maximum(lses[0], lses[1]), lses[2])
    es = [jnp.exp2(l - top) for l in lses]
    den = es[0] + es[1] + es[2]
    expand = exp_ref[...]

    def per_lane(w):
        hi = w.astype(BF16)
        lo = (w - hi.astype(F32)).astype(BF16)
        return (jnp.dot(hi, expand, preferred_element_type=F32)
                + jnp.dot(lo, expand, preferred_element_type=F32))

    o = jnp.zeros(o1_ref.shape, F32)
    for e, o_ref in zip(es, (o1_ref, o2_ref, o3_ref)):
        o = o + per_lane(e / den) * o_ref[...].astype(F32)
    yb = _rms(o, gain_ref[...]).astype(BF16)
    y = (jnp.dot(ya_ref[...], wo_ref[:d_pool, :], preferred_element_type=F32)
         + jnp.dot(yb, wo_ref[d_pool:, :], preferred_element_type=F32))
    x1 = x_ref[...] + y
    x1_ref[...] = x1
    h2_ref[...] = _rms(x1, gffn_ref[...]).astype(h2_ref.dtype)


def _mix_out(x, ya, outs, lses, attn_gain, w_out, g_ffn, *, h_dtype, tm=512):
    N, D = x.shape
    d_pool = ya.shape[-1]
    d_attn = outs[0].shape[-1]
    tm = min(tm, N)
    assert N % tm == 0
    head_of_lane = jnp.arange(d_attn) // HEAD_DIM
    expand = (jnp.arange(LANES)[:, None] == head_of_lane[None, :]).astype(BF16)
    row = lambda c: pl.BlockSpec((tm, c), lambda i: (i, 0))
    full = lambda shape: pl.BlockSpec(shape, lambda i: (0,) * len(shape))
    return pl.pallas_call(
        functools.partial(_mix_out_kernel, d_pool=d_pool),
        grid=(N // tm,),
        in_specs=[row(D), row(d_pool)] + [row(d_attn)] * 3 + [row(LANES)] * 3
                 + [full((1, d_attn)), full(w_out.shape), full((1, D)), full(expand.shape)],
        out_specs=[row(D), row(D)],
        out_shape=[jax.ShapeDtypeStruct((N, D), F32), jax.ShapeDtypeStruct((N, D), h_dtype)],
        compiler_params=_params("parallel"),
        name="mix_out",
    )(x, ya, *outs, *lses, attn_gain.reshape(1, d_attn), w_out, g_ffn.reshape(1, D), expand)


def _swiglu_hidden(h, wg, wu):
    a = jnp.dot(h, wg, preferred_element_type=F32)
    b = jnp.dot(h, wu, preferred_element_type=F32)
    return (a * jax.nn.sigmoid(a) * b).astype(BF16)


def _finish(x, y, gfin_ref):
    out = x + y
    return out if gfin_ref is None else _rms(out, gfin_ref[...])


def _ffn_kernel(*refs, final, tf):
    if final:
        h_ref, x_ref, wg_ref, wu_ref, wd_ref, gfin_ref, o_ref, acc = refs
    else:
        (h_ref, x_ref, wg_ref, wu_ref, wd_ref, o_ref, acc), gfin_ref = refs, None
    h = h_ref[...]
    for c in range(wg_ref.shape[-1] // tf):
        cols = slice(c * tf, (c + 1) * tf)
        hid = _swiglu_hidden(h, wg_ref[:, cols], wu_ref[:, cols])
        part = jnp.dot(hid, wd_ref[cols, :], preferred_element_type=F32)
        if c == 0:
            acc[...] = part
        else:
            acc[...] += part
    o_ref[...] = _finish(x_ref[...], acc[...], gfin_ref)


def _pick_chunk(n, target):
    best = None
    for c in range(LANES, min(n, target) + 1, LANES):
        if n % c == 0:
            best = c
    assert best is not None
    return best


def _ffn(h, x, wg, wu, wd, g_final, *, tm=512, tf_target=1536):
    N, D = x.shape
    F = wg.shape[-1]
    tm = min(tm, N)
    tf = _pick_chunk(F, tf_target)
    final = g_final is not None
    row = pl.BlockSpec((tm, D), lambda i: (i, 0))
    resident = lambda shape: pl.BlockSpec(shape, lambda i: (0, 0), pipeline_mode=pl.Buffered(1))
    in_specs = [row, row, resident((D, F)), resident((D, F)), resident((F, D))]
    args = [h, x, wg, wu, wd]
    if final:
        in_specs.append(pl.BlockSpec((1, D), lambda i: (0, 0)))
        args.append(g_final.reshape(1, D))
    return pl.pallas_call(
        functools.partial(_ffn_kernel, final=final, tf=tf),
        grid=(N // tm,),
        in_specs=in_specs,
        out_specs=row,
        out_shape=jax.ShapeDtypeStruct((N, D), F32),
        scratch_shapes=[pltpu.VMEM((tm, D), F32)],
        compiler_params=_params("parallel"),
        name="ffn_dense",
    )(*args)


def _router_kernel(x_ref, g_ref, wr_ref, gate_ref, *, n_experts):
    h = _rms(x_ref[...], g_ref[...])
    logits = jnp.dot(h, wr_ref[...], preferred_element_type=F32, precision=lax.Precision.HIGHEST)
    lane = lax.broadcasted_iota(jnp.int32, logits.shape, 1)
    logits = jnp.where(lane < n_experts, logits, -jnp.inf)
    picks = []
    for _ in range(TOP_K):
        m = jnp.max(logits, axis=-1, keepdims=True)
        idx = jnp.min(jnp.where(logits == m, lane, LANES), axis=-1, keepdims=True)
        picks.append((m, idx))
        logits = jnp.where(lane == idx, -jnp.inf, logits)
    (m1, i1), (m2, i2) = picks
    e2 = jnp.exp(m2 - m1)
    g1 = 1.0 / (1.0 + e2)
    g2 = e2 / (1.0 + e2)
    picked = jnp.logical_or(lane == i1 + n_experts, lane == i2 + n_experts)
    gate_ref[...] = jnp.where(lane == i1, g1, jnp.where(lane == i2, g2, jnp.where(picked, 1.0, 0.0)))


def _router(x, g_ffn, w_router, *, tm=512):
    N, D = x.shape
    E = w_router.shape[-1]
    assert 2 * E <= LANES and TOP_K == 2
    tm = min(tm, N)
    wr = jnp.zeros((D, LANES), F32).at[:, :E].set(w_router)
    return pl.pallas_call(
        functools.partial(_router_kernel, n_experts=E),
        grid=(N // tm,),
        in_specs=[pl.BlockSpec((tm, D), lambda i: (i, 0)),
                  pl.BlockSpec((1, D), lambda i: (0, 0)),
                  pl.BlockSpec((D, LANES), lambda i: (0, 0))],
        out_specs=pl.BlockSpec((tm, LANES), lambda i: (i, 0)),
        out_shape=jax.ShapeDtypeStruct((N, LANES), F32),
        compiler_params=_params("parallel"),
        name="router",
    )(x, g_ffn.reshape(1, D), wr)


def _plan_routing(route, n_experts, *, tm, rb, ch):
    N = route.shape[0]
    E = n_experts
    i32 = jnp.int32
    n_chunks = N // ch
    p_max = TOP_K * N + E * tm
    n_blocks, n_tiles = p_max // rb, p_max // tm
    n_items = n_blocks + E * n_chunks

    gate = route[:, :E].T
    sel = (route[:, E:2 * E].T > 0.5).astype(i32)
    rank = jnp.cumsum(sel, axis=1) - sel
    chunk_cnt = sel.reshape(E, n_chunks, ch).sum(-1)
    cum = jnp.concatenate([jnp.zeros((E, 1), i32), jnp.cumsum(chunk_cnt, axis=1)], axis=1)
    cnt = cum[:, -1]
    seg_len = (cnt + tm - 1) // tm * tm
    seg_end = jnp.cumsum(seg_len)
    seg_start = seg_end - seg_len
    pos = jnp.where(sel > 0, seg_start[:, None] + rank, -1)
    n_active_tiles = seg_end[-1] // tm
    tile_ids = jnp.arange(n_tiles, dtype=i32)
    tile_expert = jnp.minimum(jnp.sum(seg_end[None, :] <= tile_ids[:, None] * tm, axis=1), E - 1).astype(i32)

    def expand(counts, n_out):
        ends = jnp.cumsum(counts)
        total = ends[-1]
        w = jnp.minimum(jnp.arange(n_out, dtype=i32), total - 1)
        grp = jnp.sum(ends[None, :] <= w[:, None], axis=1).astype(i32)
        off = w - (ends[grp] - counts[grp])
        return grp, off, jnp.arange(n_out, dtype=i32) < total, w

    blk = jnp.arange(n_blocks, dtype=i32)
    b_exp = tile_expert[blk * rb // tm]
    lo = blk * rb - seg_start[b_exp]
    hi = jnp.minimum(lo + rb, cnt[b_exp])
    nonempty = lo < cnt[b_exp]
    b_cum = jnp.sum(jnp.where(b_exp[:, None, None] == jnp.arange(E, dtype=i32)[None, :, None], cum[None], 0),
                    axis=1)
    c_lo = jnp.sum(b_cum[:, 1:] <= lo[:, None], axis=1)
    c_hi = jnp.sum(b_cum[:, :-1] < hi[:, None], axis=1) - 1
    b_items = jnp.where(blk * rb < n_active_tiles * tm, jnp.where(nonempty, c_hi - c_lo + 1, 1), 0).astype(i32)
    c_lo = jnp.where(nonempty, c_lo, 0)
    g_blk, g_off, g_valid, _ = expand(b_items, n_items)
    g_first = jnp.logical_and(g_valid, g_off == 0)
    g_last = jnp.logical_and(g_valid, g_off == b_items[g_blk] - 1)
    gather_plan = jnp.stack([g_blk, c_lo[g_blk] + g_off, b_exp[g_blk],
                             g_first, g_last, g_valid]).astype(i32)

    first_row = seg_start[None, :] + cum[:, :-1].T
    last_row = seg_start[None, :] + cum[:, 1:].T - 1
    ce_items = jnp.where(chunk_cnt.T > 0, last_row // rb - first_row // rb + 1, 0).reshape(-1).astype(i32)
    c_grp, c_off, c_valid, c_w = expand(ce_items, n_items)
    c_chunk = c_grp // E
    c_blk = (first_row // rb).reshape(-1)[c_grp] + c_off
    per_chunk = ce_items.reshape(n_chunks, E).sum(1)
    chunk_end = jnp.cumsum(per_chunk)
    c_first = jnp.logical_and(c_valid, c_w == (chunk_end - per_chunk)[c_chunk])
    c_last = jnp.logical_and(c_valid, c_w == chunk_end[c_chunk] - 1)
    combine_plan = jnp.stack([c_blk, c_chunk, c_grp % E, c_first, c_last, c_valid]).astype(i32)

    token_info = jnp.concatenate([pos.T.astype(F32), gate.T, jnp.zeros((N, LANES - 2 * E), F32)], axis=1)
    row_hi = jnp.max(pos, axis=0)
    row_lo = jnp.sum(jnp.where(sel > 0, pos, 0), axis=0) - row_hi
    gate_hi = jnp.sum(jnp.where(pos == row_hi[None, :], gate, 0.0), axis=0)
    gate_lo = jnp.sum(jnp.where(pos == row_lo[None, :], gate, 0.0), axis=0)
    pair_gates = jnp.concatenate([gate_hi[:, None], gate_lo[:, None], jnp.zeros((N, LANES - 2), F32)], axis=1)
    return dict(pos=pos.reshape(E, 1, N), token_info=token_info, gather_plan=gather_plan,
                rows=(row_hi.astype(i32), row_lo.astype(i32)), pair_gates=pair_gates,
                combine_plan=combine_plan, tile_expert=tile_expert,
                n_active_tiles=n_active_tiles.reshape(1).astype(i32), p_max=p_max, n_items=n_items)


PLAN_BLOCK, PLAN_CHUNK, PLAN_EXPERT, PLAN_FIRST, PLAN_LAST, PLAN_VALID = range(6)


def _moe_gather_kernel(plan_ref, pos_ref, h_ref, xs_ref, acc, *, rb):
    w = pl.program_id(0)

    @pl.when(plan_ref[PLAN_FIRST, w] == 1)
    def _():
        acc[...] = jnp.zeros_like(acc)

    @pl.when(plan_ref[PLAN_VALID, w] == 1)
    def _():
        ch = h_ref.shape[0]
        rows = plan_ref[PLAN_BLOCK, w] * rb + lax.broadcasted_iota(jnp.int32, (rb, ch), 0)
        onehot = jnp.where(pos_ref[0] == rows, 1.0, 0.0).astype(BF16)
        acc[...] += jnp.dot(onehot, h_ref[...], preferred_element_type=F32)

    @pl.when(plan_ref[PLAN_LAST, w] == 1)
    def _():
        xs_ref[...] = acc[...].astype(BF16)


def _moe_gather(h, plan, *, rb, ch):
    N, D = h.shape
    grid_spec = pltpu.PrefetchScalarGridSpec(
        num_scalar_prefetch=1,
        grid=(plan["n_items"],),
        in_specs=[pl.BlockSpec((1, 1, ch), lambda w, p: (p[PLAN_EXPERT, w], 0, p[PLAN_CHUNK, w])),
                  pl.BlockSpec((ch, D), lambda w, p: (p[PLAN_CHUNK, w], 0))],
        out_specs=pl.BlockSpec((rb, D), lambda w, p: (p[PLAN_BLOCK, w], 0)),
        scratch_shapes=[pltpu.VMEM((rb, D), F32)],
    )
    return pl.pallas_call(
        functools.partial(_moe_gather_kernel, rb=rb),
        grid_spec=grid_spec,
        out_shape=jax.ShapeDtypeStruct((plan["p_max"], D), BF16),
        compiler_params=_params("arbitrary"),
        name="moe_gather",
    )(plan["gather_plan"], plan["pos"], h)


def _moe_experts_kernel(te_ref, na_ref, xs_ref, wg_ref, wu_ref, wd_ref, o_ref, acc, *, tf):
    @pl.when(pl.program_id(0) < na_ref[0])
    def _():
        x = xs_ref[...].astype(BF16)
        for c in range(wg_ref.shape[-1] // tf):
            cols = slice(c * tf, (c + 1) * tf)
            hid = _swiglu_hidden(x, wg_ref[0, :, cols], wu_ref[0, :, cols])
            part = jnp.dot(hid, wd_ref[0, cols, :], preferred_element_type=F32)
            if c == 0:
                acc[...] = part
            else:
                acc[...] += part
        o_ref[...] = acc[...]


def _moe_experts(xs, plan, wg, wu, wd, *, tm, tf_target=512):
    P, D = xs.shape
    E, _, F = wg.shape
    tf = _pick_chunk(F, tf_target)
    tile = lambda i, te, na: (jnp.minimum(i, na[0] - 1), 0)
    resident = pl.Buffered(1)
    grid_spec = pltpu.PrefetchScalarGridSpec(
        num_scalar_prefetch=2,
        grid=(P // tm,),
        in_specs=[pl.BlockSpec((tm, D), tile),
                  pl.BlockSpec((1, D, F), lambda i, te, na: (te[i], 0, 0), pipeline_mode=resident),
                  pl.BlockSpec((1, D, F), lambda i, te, na: (te[i], 0, 0), pipeline_mode=resident),
                  pl.BlockSpec((1, F, D), lambda i, te, na: (te[i], 0, 0), pipeline_mode=resident)],
        out_specs=pl.BlockSpec((tm, D), tile),
        scratch_shapes=[pltpu.VMEM((tm, D), F32)],
    )
    return pl.pallas_call(
        functools.partial(_moe_experts_kernel, tf=tf),
        grid_spec=grid_spec,
        out_shape=jax.ShapeDtypeStruct((P, D), F32),
        compiler_params=_params("arbitrary"),
        name="moe_experts",
    )(plan["tile_expert"], plan["n_active_tiles"], xs, wg, wu, wd)


def _moe_combine_kernel(*refs, final, rb, n_experts):
    if final:
        plan_ref, info_ref, ys_ref, x_ref, gfin_ref, o_ref, acc = refs
    else:
        (plan_ref, info_ref, ys_ref, x_ref, o_ref, acc), gfin_ref = refs, None
    w = pl.program_id(0)

    @pl.when(plan_ref[PLAN_FIRST, w] == 1)
    def _():
        acc[...] = jnp.zeros_like(acc)

    @pl.when(plan_ref[PLAN_VALID, w] == 1)
    def _():
        info = info_ref[...]
        ch = info.shape[0]
        e = plan_ref[PLAN_EXPERT, w]
        lane = lax.broadcasted_iota(jnp.int32, info.shape, 1)
        pos = jnp.sum(jnp.where(lane == e, info, 0.0), axis=-1, keepdims=True)
        gate = jnp.sum(jnp.where(lane == e + n_experts, info, 0.0), axis=-1, keepdims=True)
        local = pos - (plan_ref[PLAN_BLOCK, w] * rb).astype(F32)
        cols = lax.broadcasted_iota(jnp.int32, (ch, rb), 1).astype(F32)
        onehot = jnp.where(local == cols, 1.0, 0.0).astype(BF16)
        acc[...] += gate * jnp.dot(onehot, ys_ref[...], preferred_element_type=F32)

    @pl.when(plan_ref[PLAN_LAST, w] == 1)
    def _():
        o_ref[...] = _finish(x_ref[...], acc[...], gfin_ref)


def _moe_combine(ys, x, plan, g_final, *, rb, ch, n_experts):
    N, D = x.shape
    final = g_final is not None
    chunk = lambda c: pl.BlockSpec((ch, c), lambda w, p: (p[PLAN_CHUNK, w], 0))
    in_specs = [chunk(LANES), pl.BlockSpec((rb, D), lambda w, p: (p[PLAN_BLOCK, w], 0)), chunk(D)]
    args = [plan["combine_plan"], plan["token_info"], ys, x]
    if final:
        in_specs.append(pl.BlockSpec((1, D), lambda w, p: (0, 0)))
        args.append(g_final.reshape(1, D))
    grid_spec = pltpu.PrefetchScalarGridSpec(
        num_scalar_prefetch=1,
        grid=(plan["n_items"],),
        in_specs=in_specs,
        out_specs=chunk(D),
        scratch_shapes=[pltpu.VMEM((ch, D), F32)],
    )
    return pl.pallas_call(
        functools.partial(_moe_combine_kernel, final=final, rb=rb, n_experts=n_experts),
        grid_spec=grid_spec,
        out_shape=jax.ShapeDtypeStruct((N, D), F32),
        compiler_params=_params("arbitrary"),
        name="moe_combine",
    )(*args)


def _sc_rows_kernel(n_rows, n_out, d, scatter):
    per_worker = n_rows // SC_WORKERS
    assert n_rows % (SC_WORKERS * SC_WINDOW) == 0
    mesh = plsc.VectorSubcoreMesh(core_axis_name="c", subcore_axis_name="s")
    out_type = (jax.ShapeDtypeStruct((n_out, d), F32) if scatter
                else [jax.ShapeDtypeStruct((n_rows, d), F32)] * 2)

    def body(*refs):
        if scatter:
            src_hbm, hi_hbm, lo_hbm, out_hbm, idx_hi, idx_lo, rows_hi, rows_lo, sem_hi, sem_lo = refs
        else:
            src_hbm, hi_hbm, lo_hbm, out_hi_hbm, out_lo_hbm, idx_hi, idx_lo, rows_hi, rows_lo, sem_hi, sem_lo = refs
        worker = lax.axis_index("s") * SC_CORES + lax.axis_index("c")

        @pl.loop(0, per_worker // SC_WINDOW)
        def _(i):
            window = pl.ds(worker * per_worker + i * SC_WINDOW, SC_WINDOW)
            pltpu.sync_copy(hi_hbm.at[window], idx_hi)
            pltpu.sync_copy(lo_hbm.at[window], idx_lo)
            if scatter:
                pltpu.sync_copy(src_hbm.at[window], rows_hi)
                to_hi = pltpu.async_copy(rows_hi, out_hbm.at[idx_hi], sem_hi)
                to_lo = pltpu.async_copy(rows_hi, out_hbm.at[idx_lo], sem_lo)
                to_hi.wait()
                to_lo.wait()
            else:
                from_hi = pltpu.async_copy(src_hbm.at[idx_hi], rows_hi, sem_hi)
                from_lo = pltpu.async_copy(src_hbm.at[idx_lo], rows_lo, sem_lo)
                from_hi.wait()
                pltpu.sync_copy(rows_hi, out_hi_hbm.at[window])
                from_lo.wait()
                pltpu.sync_copy(rows_lo, out_lo_hbm.at[window])

    index_vec, row_buf = pltpu.VMEM((SC_WINDOW,), jnp.int32), pltpu.VMEM((SC_WINDOW, d), F32)
    return pl.kernel(body, mesh=mesh, out_type=out_type,
                     scratch_types=[index_vec, index_vec, row_buf, row_buf,
                                    pltpu.SemaphoreType.DMA, pltpu.SemaphoreType.DMA])


def _moe_mix_kernel(*refs, final):
    if final:
        x_ref, yh_ref, yl_ref, g_ref, gfin_ref, o_ref = refs
    else:
        (x_ref, yh_ref, yl_ref, g_ref, o_ref), gfin_ref = refs, None
    gates = g_ref[...]
    lane = lax.broadcasted_iota(jnp.int32, gates.shape, 1)
    g_hi = jnp.sum(jnp.where(lane == 0, gates, 0.0), axis=-1, keepdims=True)
    g_lo = jnp.sum(jnp.where(lane == 1, gates, 0.0), axis=-1, keepdims=True)
    o_ref[...] = _finish(x_ref[...], g_hi * yh_ref[...] + g_lo * yl_ref[...], gfin_ref)


def _moe_mix(x, y_hi, y_lo, pair_gates, g_final, *, tm=512):
    N, D = x.shape
    final = g_final is not None
    row = lambda c: pl.BlockSpec((tm, c), lambda i: (i, 0))
    in_specs, args = [row(D), row(D), row(D), row(LANES)], [x, y_hi, y_lo, pair_gates]
    if final:
        in_specs.append(pl.BlockSpec((1, D), lambda i: (0, 0)))
        args.append(g_final.reshape(1, D))
    return pl.pallas_call(
        functools.partial(_moe_mix_kernel, final=final),
        grid=(N // tm,), in_specs=in_specs, out_specs=row(D),
        out_shape=jax.ShapeDtypeStruct((N, D), F32),
        compiler_params=_params("parallel"),
        name="moe_mix",
    )(*args)


def _moe(h, x, route, wg, wu, wd, g_final, *, tm=512, rb=256, ch=512):
    N, D = x.shape
    E = wg.shape[0]
    tm, ch = min(tm, N), min(ch, N)
    rb = min(rb, tm)
    assert N % ch == 0 and tm % rb == 0
    plan = _plan_routing(route, E, tm=tm, rb=rb, ch=ch)
    rows_hi, rows_lo = plan["rows"]
    xs = _sc_rows_kernel(N, plan["p_max"], D, scatter=True)(h, rows_hi, rows_lo)
    ys = _moe_experts(xs, plan, wg, wu, wd, tm=tm)
    y_hi, y_lo = _sc_rows_kernel(N, plan["p_max"], D, scatter=False)(ys, rows_hi, rows_lo)
    return _moe_mix(x, y_hi, y_lo, plan["pair_gates"], g_final)


def kernel(x, norm_mix, w_in, w_pool, pool_scale, attn_gain, w_out, norm_ffn, ffn_wg, ffn_wu, ffn_wd,
           w_router, moe_wg, moe_wu, moe_wd, final_norm):
    B, S, D = x.shape
    depth = norm_mix.shape[0]
    bf = lambda t: t.astype(BF16)
    w_in, w_pool, w_out = bf(w_in), bf(w_pool), bf(w_out)
    ffn_wg, ffn_wu, ffn_wd = bf(ffn_wg), bf(ffn_wu), bf(ffn_wd)
    moe_wg, moe_wu, moe_wd = bf(moe_wg), bf(moe_wu), bf(moe_wd)
    R = RESIDUES
    assert S % R == 0
    x = x.reshape(B, S // R, R, D).transpose(0, 2, 1, 3)
    for l in range(depth):
        ya, q, k, v = _mix_in(x, norm_mix[l], w_in[l], w_pool[l], pool_scale[l])
        branches = [_attn_branch(q, k, v, dil) for _, dil in DILATED_PATTERNS]
        x1, h2 = _mix_out(x.reshape(B * S, D), ya.reshape(B * S, -1),
                          [o for o, _ in branches], [lse for _, lse in branches],
                          attn_gain[l], w_out[l], norm_ffn[l],
                          h_dtype=BF16 if l % 2 == 0 else F32)
        g_final = final_norm if l == depth - 1 else None
        i = l // 2
        if l % 2 == 0:
            x2 = _ffn(h2, x1, ffn_wg[i], ffn_wu[i], ffn_wd[i], g_final)
        else:
            route = _router(x1, norm_ffn[l], w_router[i])
            x2 = _moe(h2, x1, route, moe_wg[i], moe_wu[i], moe_wd[i], g_final)
        x = x2.reshape(B, R, S // R, D)
    return x.transpose(0, 2, 1, 3).reshape(B, S, D)
```

```python
import functools

import jax
import jax.numpy as jnp
from jax import lax
from jax.experimental import pallas as pl
from jax.experimental.pallas import tpu as pltpu
from jax.experimental.pallas import tpu_sc as plsc

F32 = jnp.float32
BF16 = jnp.bfloat16

EPS = 1e-6
LANES = 128
HEAD_DIM = 64
POOL_WINDOWS = (2, 4, 8, 16)
POOL_HIST = 8
DILATED_PATTERNS = ((128, 1), (512, 4), (2048, 16))
WINDOW_STEPS = 128
RESIDUES = 16
ATTN_STEP_ROWS = 1024
ATTN_UNROLL_QUERIES = 512
LOG2_E = 1.4426950408889634
SC_CORES, SC_SUBCORES = 2, 16
SC_WORKERS = SC_CORES * SC_SUBCORES
SC_WINDOW = 32
assert all(w // d == WINDOW_STEPS and RESIDUES % d == 0 for w, d in DILATED_PATTERNS)
TOP_K = 2
MASKED = -1e30
VMEM_LIMIT = 48 * 1024 * 1024


def _rms(x, g):
    return x * lax.rsqrt(jnp.mean(x * x, axis=-1, keepdims=True) + EPS) * g


def _params(*sem):
    return pltpu.CompilerParams(dimension_semantics=sem, vmem_limit_bytes=VMEM_LIMIT)


def _mix_in_kernel(x_ref, g_ref, w_ref, wp_ref, ps_ref, ya_ref, q_ref, k_ref, v_ref, q32_ref, k32_ref, v32_ref,
                   ubuf, uprev, *, d_pool, d_attn, ti):
    s = pl.program_id(1)
    R, D = RESIDUES, x_ref.shape[-1]
    rows = R * ti
    blk = lambda t: t.reshape(R, ti, t.shape[-1])
    h = _rms(x_ref[...].reshape(rows, D), g_ref[...]).astype(BF16)
    proj = jnp.dot(h, w_ref[...], preferred_element_type=F32)
    scale = HEAD_DIM ** -0.5 * LOG2_E
    qkv = (proj[:, d_pool:d_pool + d_attn] * scale, proj[:, d_pool + d_attn:d_pool + 2 * d_attn],
           proj[:, d_pool + 2 * d_attn:])
    for t, ref, ref32 in zip(qkv, (q_ref, k_ref, v_ref), (q32_ref, k32_ref, v32_ref)):
        ref[...] = blk(t.astype(BF16))
        ref32[...] = blk(t)

    @pl.when(s == 0)
    def _():
        ubuf[:, 0:POOL_HIST, :] = jnp.zeros((R, POOL_HIST, d_pool), F32)

    ubuf[:, POOL_HIST:POOL_HIST + ti, :] = blk(proj[:, :d_pool])
    uprev[...] = ubuf[:, POOL_HIST - 1:POOL_HIST - 1 + ti, :]
    at_start = (s * ti + lax.broadcasted_iota(jnp.int32, (ti, 1), 0)) == 0
    group = d_pool // len(POOL_WINDOWS)
    zs = []
    for gi, w in enumerate(POOL_WINDOWS):
        cols = slice(gi * group, (gi + 1) * group)
        ds = []
        for r in range(R):
            ug = ubuf[r, POOL_HIST:POOL_HIST + ti, cols]
            win = ug
            for back in range(1, w):
                rr = r - back
                win = win + (ubuf[rr, POOL_HIST:POOL_HIST + ti, cols] if rr >= 0 else uprev[rr + R, :, cols])
            cnt = jnp.where(at_start, float(min(r + 1, w)), float(w))
            ds.append(win / cnt - ug)
        d = jnp.concatenate(ds, axis=0).astype(BF16)
        zs.append(jnp.dot(d, wp_ref[gi], preferred_element_type=F32))
    z = jnp.concatenate(zs, axis=-1)
    ya_ref[...] = blk(_rms(z, ps_ref[...]).astype(BF16))
    ubuf[:, POOL_HIST - 1:POOL_HIST, :] = ubuf[:, POOL_HIST + ti - 1:POOL_HIST + ti, :]


def _mix_in(x, g, w_in, w_pool, pool_scale, *, ti=32):
    B, R, SI, D = x.shape
    d_pool = pool_scale.shape[-1]
    d_in = w_in.shape[-1]
    d_attn = (d_in - d_pool) // 3
    ti = min(ti, SI)
    assert R == RESIDUES >= max(POOL_WINDOWS) and SI % ti == 0 and ti % 16 == 0
    assert d_pool % (LANES * len(POOL_WINDOWS)) == 0
    seq_spec = lambda c: pl.BlockSpec((None, R, ti, c), lambda b, s: (b, 0, s, 0))
    full = lambda shape: pl.BlockSpec(shape, lambda b, s: (0,) * len(shape))
    out_sds = lambda c, dtype=BF16: jax.ShapeDtypeStruct((B, R, SI, c), dtype)
    return pl.pallas_call(
        functools.partial(_mix_in_kernel, d_pool=d_pool, d_attn=d_attn, ti=ti),
        grid=(B, SI // ti),
        in_specs=[seq_spec(D), full((1, D)), full((D, d_in)), full(w_pool.shape), full((1, d_pool))],
        out_specs=[seq_spec(d_pool)] + [seq_spec(d_attn)] * 6,
        out_shape=[out_sds(d_pool)] + [out_sds(d_attn)] * 3 + [out_sds(d_attn, F32)] * 3,
        scratch_shapes=[pltpu.VMEM((R, POOL_HIST + ti, d_pool), F32), pltpu.VMEM((R, ti, d_pool), F32)],
        compiler_params=_params("parallel", "arbitrary"),
        name="mix_in",
    )(x, g.reshape(1, D), w_in, w_pool, pool_scale.reshape(1, d_pool))


def _attn_kernel(q_ref, k_ref, v_ref, o_ref, lse_ref, bias_ref, *, nq, strips, d_attn):
    _, classes, chunk, _ = q_ref.shape
    L = nq // strips
    nk, n_blocks = 2 * nq, chunk // L
    first_block = pl.program_id(2) * n_blocks
    row = lax.broadcasted_iota(jnp.int32, (nq, 1), 0)
    col = lax.broadcasted_iota(jnp.int32, (1, nk), 1)
    q_strip, q_row = row >> (L.bit_length() - 1), row & (L - 1)
    k_strip, k_row = col >> ((2 * L).bit_length() - 1), col & (2 * L - 1)
    back = strips * (q_row - k_row) + (q_strip - k_strip)

    def band(offset):
        rel = back + strips * offset
        return jnp.where((rel >= 0) & (rel <= WINDOW_STEPS), 0.0, MASKED).astype(F32)

    bias_ref[0] = band(0)
    bias_ref[1] = band(L)
    lane = lax.broadcasted_iota(jnp.int32, (nq, LANES), 1)
    lo_half = lane < HEAD_DIM

    def block(it, carry):
        g, j_local = it // n_blocks, it % n_blocks
        j = first_block + j_local
        q_rows = pl.ds(pl.multiple_of(j_local * L, L), L)
        k_rows = pl.ds(pl.multiple_of(jnp.maximum(j - 1, 0) * L, L), 2 * L)
        bias = bias_ref[jnp.minimum(j, 1)]
        lse_tile = jnp.zeros((nq, LANES), F32)
        for hp in range(d_attn // LANES):
            cols = slice(hp * LANES, (hp + 1) * LANES)
            qp = q_ref[:, g, q_rows, cols].reshape(nq, LANES).astype(BF16)
            kk = k_ref[:, g, k_rows, cols].reshape(nk, LANES).astype(BF16)
            vv = v_ref[:, g, k_rows, cols].reshape(nk, LANES).astype(BF16)
            outs, lses = [], []
            for sub in range(2):
                keep = lo_half if sub == 0 else jnp.logical_not(lo_half)
                qm = jnp.where(keep, qp, jnp.zeros_like(qp))
                s = lax.dot_general(qm, kk, (((1,), (1,)), ((), ())), preferred_element_type=F32) + bias
                m = jnp.max(s, axis=-1, keepdims=True)
                p = jnp.exp2(s - m)
                l = jnp.sum(p, axis=-1, keepdims=True)
                o = jnp.dot(p.astype(BF16), vv, preferred_element_type=F32)
                outs.append(o / l)
                lses.append(m + jnp.log2(l))
            o_pair = jnp.where(lo_half, outs[0], outs[1]).astype(o_ref.dtype)
            o_ref[:, g, q_rows, cols] = o_pair.reshape(strips, L, LANES)
            pair_lse = jnp.where(lane == 2 * hp, lses[0], lses[1])
            lse_tile = jnp.where((lane >> 1) == hp, pair_lse, lse_tile)
        lse_ref[:, g, q_rows, :] = lse_tile.reshape(strips, L, LANES)
        return carry

    lax.fori_loop(0, classes * n_blocks, block, 0, unroll=ATTN_UNROLL_QUERIES // nq)


def _attn_branch(q, k, v, dil):
    B, R, SI, C = q.shape
    strips = R // dil
    row_tile = 8 * 4 // q.dtype.itemsize
    nq = max(WINDOW_STEPS, strips * row_tile)
    L = nq // strips
    classes = max(1, min(dil, ATTN_STEP_ROWS // (strips * SI)))
    chunk = min(SI, max(L, ATTN_STEP_ROWS // (strips * classes)))
    assert R % dil == 0 and SI % chunk == 0 and chunk % L == 0 and SI >= 2 * L and dil % classes == 0
    assert C % LANES == 0 and C // HEAD_DIM <= LANES and nq & (nq - 1) == 0 and L & (L - 1) == 0
    view = lambda t: t.reshape(B, strips, dil, SI, t.shape[-1])
    q_spec = lambda c: pl.BlockSpec((None, strips, classes, chunk, c), lambda b, g, s: (b, 0, g, s, 0))
    kv_spec = pl.BlockSpec((None, strips, classes, SI, C), lambda b, g, s: (b, 0, g, 0, 0),
                           pipeline_mode=pl.Buffered(1 if SI > chunk else 2))
    o, lse = pl.pallas_call(
        functools.partial(_attn_kernel, nq=nq, strips=strips, d_attn=C),
        scratch_shapes=[pltpu.VMEM((2, nq, 2 * nq), F32)],
        grid=(B, dil // classes, SI // chunk),
        in_specs=[q_spec(C), kv_spec, kv_spec],
        out_specs=[q_spec(C), q_spec(LANES)],
        out_shape=[jax.ShapeDtypeStruct((B, strips, dil, SI, C), q.dtype),
                   jax.ShapeDtypeStruct((B, strips, dil, SI, LANES), F32)],
        compiler_params=_params("parallel", "parallel", "arbitrary"),
        name=f"attn_d{dil}",
    )(view(q), view(k), view(v))
    return o.reshape(B * R * SI, C), lse.reshape(B * R * SI, LANES)


def _mix_out_kernel(x_ref, ya_ref, o1_ref, o2_ref, o3_ref, l1_ref, l2_ref, l3_ref, gain_ref, wo_ref,
                    gffn_ref, exp_ref, x1_ref, h2_ref, *, d_pool):
    lses = [l1_ref[...], l2_ref[...], l3_ref[...]]
    top = jnp.maximum(jnp.maximum(lses[0], lses[1]), lses[2])
    es = [jnp.exp2(l - top) for l in lses]
    den = es[0] + es[1] + es[2]
    expand = exp_ref[...]

    def per_lane(w):
        hi = w.astype(BF16)
        lo = (w - hi.astype(F32)).astype(BF16)
        return (jnp.dot(hi, expand, preferred_element_type=F32)
                + jnp.dot(lo, expand, preferred_element_type=F32))

    o = jnp.zeros(o1_ref.shape, F32)
    for e, o_ref in zip(es, (o1_ref, o2_ref, o3_ref)):
        o = o + per_lane(e / den) * o_ref[...].astype(F32)
    yb = _rms(o, gain_ref[...]).astype(BF16)
    y = (jnp.dot(ya_ref[...], wo_ref[:d_pool, :], preferred_element_type=F32)
         + jnp.dot(yb, wo_ref[d_pool:, :], preferred_element_type=F32))
    x1 = x_ref[...] + y
    x1_ref[...] = x1
    h2_ref[...] = _rms(x1, gffn_ref[...]).astype(h2_ref.dtype)


def _mix_out(x, ya, outs, lses, attn_gain, w_out, g_ffn, *, h_dtype, tm=512):
    N, D = x.shape
    d_pool = ya.shape[-1]
    d_attn = outs[0].shape[-1]
    tm = min(tm, N)
    assert N % tm == 0
    head_of_lane = jnp.arange(d_attn) // HEAD_DIM
    expand = (jnp.arange(LANES)[:, None] == head_of_lane[None, :]).astype(BF16)
    row = lambda c: pl.BlockSpec((tm, c), lambda i: (i, 0))
    full = lambda shape: pl.BlockSpec(shape, lambda i: (0,) * len(shape))
    return pl.pallas_call(
        functools.partial(_mix_out_kernel, d_pool=d_pool),
        grid=(N // tm,),
        in_specs=[row(D), row(d_pool)] + [row(d_attn)] * 3 + [row(LANES)] * 3
                 + [full((1, d_attn)), full(w_out.shape), full((1, D)), full(expand.shape)],
        out_specs=[row(D), row(D)],
        out_shape=[jax.ShapeDtypeStruct((N, D), F32), jax.ShapeDtypeStruct((N, D), h_dtype)],
        compiler_params=_params("parallel"),
        name="mix_out",
    )(x, ya, *outs, *lses, attn_gain.reshape(1, d_attn), w_out, g_ffn.reshape(1, D), expand)


def _swiglu_hidden(h, wg, wu):
    a = jnp.dot(h, wg, preferred_element_type=F32)
    b = jnp.dot(h, wu, preferred_element_type=F32)
    return (a * jax.nn.sigmoid(a) * b).astype(BF16)


def _finish(x, y, gfin_ref):
    out = x + y
    return out if gfin_ref is None else _rms(out, gfin_ref[...])


def _ffn_kernel(*refs, final, tf):
    if final:
        h_ref, x_ref, wg_ref, wu_ref, wd_ref, gfin_ref, o_ref, acc = refs
    else:
        (h_ref, x_ref, wg_ref, wu_ref, wd_ref, o_ref, acc), gfin_ref = refs, None
    h = h_ref[...]
    for c in range(wg_ref.shape[-1] // tf):
        cols = slice(c * tf, (c + 1) * tf)
        hid = _swiglu_hidden(h, wg_ref[:, cols], wu_ref[:, cols])
        part = jnp.dot(hid, wd_ref[cols, :], preferred_element_type=F32)
        if c == 0:
            acc[...] = part
        else:
            acc[...] += part
    o_ref[...] = _finish(x_ref[...], acc[...], gfin_ref)


def _pick_chunk(n, target):
    best = None
    for c in range(LANES, min(n, target) + 1, LANES):
        if n % c == 0:
            best = c
    assert best is not None
    return best


def _ffn(h, x, wg, wu, wd, g_final, *, tm=512, tf_target=1536):
    N, D = x.shape
    F = wg.shape[-1]
    tm = min(tm, N)
    tf = _pick_chunk(F, tf_target)
    final = g_final is not None
    row = pl.BlockSpec((tm, D), lambda i: (i, 0))
    resident = lambda shape: pl.BlockSpec(shape, lambda i: (0, 0), pipeline_mode=pl.Buffered(1))
    in_specs = [row, row, resident((D, F)), resident((D, F)), resident((F, D))]
    args = [h, x, wg, wu, wd]
    if final:
        in_specs.append(pl.BlockSpec((1, D), lambda i: (0, 0)))
        args.append(g_final.reshape(1, D))
    return pl.pallas_call(
        functools.partial(_ffn_kernel, final=final, tf=tf),
        grid=(N // tm,),
        in_specs=in_specs,
        out_specs=row,
        out_shape=jax.ShapeDtypeStruct((N, D), F32),
        scratch_shapes=[pltpu.VMEM((tm, D), F32)],
        compiler_params=_params("parallel"),
        name="ffn_dense",
    )(*args)


def _router_kernel(x_ref, g_ref, wr_ref, gate_ref, *, n_experts):
    h = _rms(x_ref[...], g_ref[...])
    logits = jnp.dot(h, wr_ref[...], preferred_element_type=F32, precision=lax.Precision.HIGHEST)
    lane = lax.broadcasted_iota(jnp.int32, logits.shape, 1)
    logits = jnp.where(lane < n_experts, logits, -jnp.inf)
    picks = []
    for _ in range(TOP_K):
        m = jnp.max(logits, axis=-1, keepdims=True)
        idx = jnp.min(jnp.where(logits == m, lane, LANES), axis=-1, keepdims=True)
        picks.append((m, idx))
        logits = jnp.where(lane == idx, -jnp.inf, logits)
    (m1, i1), (m2, i2) = picks
    e2 = jnp.exp(m2 - m1)
    g1 = 1.0 / (1.0 + e2)
    g2 = e2 / (1.0 + e2)
    picked = jnp.logical_or(lane == i1 + n_experts, lane == i2 + n_experts)
    gate_ref[...] = jnp.where(lane == i1, g1, jnp.where(lane == i2, g2, jnp.where(picked, 1.0, 0.0)))


def _router(x, g_ffn, w_router, *, tm=512):
    N, D = x.shape
    E = w_router.shape[-1]
    assert 2 * E <= LANES and TOP_K == 2
    tm = min(tm, N)
    wr = jnp.zeros((D, LANES), F32).at[:, :E].set(w_router)
    return pl.pallas_call(
        functools.partial(_router_kernel, n_experts=E),
        grid=(N // tm,),
        in_specs=[pl.BlockSpec((tm, D), lambda i: (i, 0)),
                  pl.BlockSpec((1, D), lambda i: (0, 0)),
                  pl.BlockSpec((D, LANES), lambda i: (0, 0))],
        out_specs=pl.BlockSpec((tm, LANES), lambda i: (i, 0)),
        out_shape=jax.ShapeDtypeStruct((N, LANES), F32),
        compiler_params=_params("parallel"),
        name="router",
    )(x, g_ffn.reshape(1, D), wr)


def _plan_routing(route, n_experts, *, tm, rb, ch):
    N = route.shape[0]
    E = n_experts
    i32 = jnp.int32
    n_chunks = N // ch
    p_max = TOP_K * N + E * tm
    n_blocks, n_tiles = p_max // rb, p_max // tm
    n_items = n_blocks + E * n_chunks

    gate = route[:, :E].T
    sel = (route[:, E:2 * E].T > 0.5).astype(i32)
    rank = jnp.cumsum(sel, axis=1) - sel
    chunk_cnt = sel.reshape(E, n_chunks, ch).sum(-1)
    cum = jnp.concatenate([jnp.zeros((E, 1), i32), jnp.cumsum(chunk_cnt, axis=1)], axis=1)
    cnt = cum[:, -1]
    seg_len = (cnt + tm - 1) // tm * tm
    seg_end = jnp.cumsum(seg_len)
    seg_start = seg_end - seg_len
    pos = jnp.where(sel > 0, seg_start[:, None] + rank, -1)
    n_active_tiles = seg_end[-1] // tm
    tile_ids = jnp.arange(n_tiles, dtype=i32)
    tile_expert = jnp.minimum(jnp.sum(seg_end[None, :] <= tile_ids[:, None] * tm, axis=1), E - 1).astype(i32)

    def expand(counts, n_out):
        ends = jnp.cumsum(counts)
        total = ends[-1]
        w = jnp.minimum(jnp.arange(n_out, dtype=i32), total - 1)
        grp = jnp.sum(ends[None, :] <= w[:, None], axis=1).astype(i32)
        off = w - (ends[grp] - counts[grp])
        return grp, off, jnp.arange(n_out, dtype=i32) < total, w

    blk = jnp.arange(n_blocks, dtype=i32)
    b_exp = tile_expert[blk * rb // tm]
    lo = blk * rb - seg_start[b_exp]
    hi = jnp.minimum(lo + rb, cnt[b_exp])
    nonempty = lo < cnt[b_exp]
    b_cum = jnp.sum(jnp.where(b_exp[:, None, None] == jnp.arange(E, dtype=i32)[None, :, None], cum[None], 0),
                    axis=1)
    c_lo = jnp.sum(b_cum[:, 1:] <= lo[:, None], axis=1)
    c_hi = jnp.sum(b_cum[:, :-1] < hi[:, None], axis=1) - 1
    b_items = jnp.where(blk * rb < n_active_tiles * tm, jnp.where(nonempty, c_hi - c_lo + 1, 1), 0).astype(i32)
    c_lo = jnp.where(nonempty, c_lo, 0)
    g_blk, g_off, g_valid, _ = expand(b_items, n_items)
    g_first = jnp.logical_and(g_valid, g_off == 0)
    g_last = jnp.logical_and(g_valid, g_off == b_items[g_blk] - 1)
    gather_plan = jnp.stack([g_blk, c_lo[g_blk] + g_off, b_exp[g_blk],
                             g_first, g_last, g_valid]).astype(i32)

    first_row = seg_start[None, :] + cum[:, :-1].T
    last_row = seg_start[None, :] + cum[:, 1:].T - 1
    ce_items = jnp.where(chunk_cnt.T > 0, last_row // rb - first_row // rb + 1, 0).reshape(-1).astype(i32)
    c_grp, c_off, c_valid, c_w = expand(ce_items, n_items)
    c_chunk = c_grp // E
    c_blk = (first_row // rb).reshape(-1)[c_grp] + c_off
    per_chunk = ce_items.reshape(n_chunks, E).sum(1)
    chunk_end = jnp.cumsum(per_chunk)
    c_first = jnp.logical_and(c_valid, c_w == (chunk_end - per_chunk)[c_chunk])
    c_last = jnp.logical_and(c_valid, c_w == chunk_end[c_chunk] - 1)
    combine_plan = jnp.stack([c_blk, c_chunk, c_grp % E, c_first, c_last, c_valid]).astype(i32)

    token_info = jnp.concatenate([pos.T.astype(F32), gate.T, jnp.zeros((N, LANES - 2 * E), F32)], axis=1)
    row_hi = jnp.max(pos, axis=0)
    row_lo = jnp.sum(jnp.where(sel > 0, pos, 0), axis=0) - row_hi
    gate_hi = jnp.sum(jnp.where(pos == row_hi[None, :], gate, 0.0), axis=0)
    gate_lo = jnp.sum(jnp.where(pos == row_lo[None, :], gate, 0.0), axis=0)
    pair_gates = jnp.concatenate([gate_hi[:, None], gate_lo[:, None], jnp.zeros((N, LANES - 2), F32)], axis=1)
    return dict(pos=pos.reshape(E, 1, N), token_info=token_info, gather_plan=gather_plan,
                rows=(row_hi.astype(i32), row_lo.astype(i32)), pair_gates=pair_gates,
                combine_plan=combine_plan, tile_expert=tile_expert,
                n_active_tiles=n_active_tiles.reshape(1).astype(i32), p_max=p_max, n_items=n_items)


PLAN_BLOCK, PLAN_CHUNK, PLAN_EXPERT, PLAN_FIRST, PLAN_LAST, PLAN_VALID = range(6)


def _moe_gather_kernel(plan_ref, pos_ref, h_ref, xs_ref, acc, *, rb):
    w = pl.program_id(0)

    @pl.when(plan_ref[PLAN_FIRST, w] == 1)
    def _():
        acc[...] = jnp.zeros_like(acc)

    @pl.when(plan_ref[PLAN_VALID, w] == 1)
    def _():
        ch = h_ref.shape[0]
        rows = plan_ref[PLAN_BLOCK, w] * rb + lax.broadcasted_iota(jnp.int32, (rb, ch), 0)
        onehot = jnp.where(pos_ref[0] == rows, 1.0, 0.0).astype(BF16)
        acc[...] += jnp.dot(onehot, h_ref[...], preferred_element_type=F32)

    @pl.when(plan_ref[PLAN_LAST, w] == 1)
    def _():
        xs_ref[...] = acc[...].astype(BF16)


def _moe_gather(h, plan, *, rb, ch):
    N, D = h.shape
    grid_spec = pltpu.PrefetchScalarGridSpec(
        num_scalar_prefetch=1,
        grid=(plan["n_items"],),
        in_specs=[pl.BlockSpec((1, 1, ch), lambda w, p: (p[PLAN_EXPERT, w], 0, p[PLAN_CHUNK, w])),
                  pl.BlockSpec((ch, D), lambda w, p: (p[PLAN_CHUNK, w], 0))],
        out_specs=pl.BlockSpec((rb, D), lambda w, p: (p[PLAN_BLOCK, w], 0)),
        scratch_shapes=[pltpu.VMEM((rb, D), F32)],
    )
    return pl.pallas_call(
        functools.partial(_moe_gather_kernel, rb=rb),
        grid_spec=grid_spec,
        out_shape=jax.ShapeDtypeStruct((plan["p_max"], D), BF16),
        compiler_params=_params("arbitrary"),
        name="moe_gather",
    )(plan["gather_plan"], plan["pos"], h)


def _moe_experts_kernel(te_ref, na_ref, xs_ref, wg_ref, wu_ref, wd_ref, o_ref, acc, *, tf):
    @pl.when(pl.program_id(0) < na_ref[0])
    def _():
        x = xs_ref[...].astype(BF16)
        for c in range(wg_ref.shape[-1] // tf):
            cols = slice(c * tf, (c + 1) * tf)
            hid = _swiglu_hidden(x, wg_ref[0, :, cols], wu_ref[0, :, cols])
            part = jnp.dot(hid, wd_ref[0, cols, :], preferred_element_type=F32)
            if c == 0:
                acc[...] = part
            else:
                acc[...] += part
        o_ref[...] = acc[...]


def _moe_experts(xs, plan, wg, wu, wd, *, tm, tf_target=512):
    P, D = xs.shape
    E, _, F = wg.shape
    tf = _pick_chunk(F, tf_target)
    tile = lambda i, te, na: (jnp.minimum(i, na[0] - 1), 0)
    resident = pl.Buffered(1)
    grid_spec = pltpu.PrefetchScalarGridSpec(
        num_scalar_prefetch=2,
        grid=(P // tm,),
        in_specs=[pl.BlockSpec((tm, D), tile),
                  pl.BlockSpec((1, D, F), lambda i, te, na: (te[i], 0, 0), pipeline_mode=resident),
                  pl.BlockSpec((1, D, F), lambda i, te, na: (te[i], 0, 0), pipeline_mode=resident),
                  pl.BlockSpec((1, F, D), lambda i, te, na: (te[i], 0, 0), pipeline_mode=resident)],
        out_specs=pl.BlockSpec((tm, D), tile),
        scratch_shapes=[pltpu.VMEM((tm, D), F32)],
    )
    return pl.pallas_call(
        functools.partial(_moe_experts_kernel, tf=tf),
        grid_spec=grid_spec,
        out_shape=jax.ShapeDtypeStruct((P, D), F32),
        compiler_params=_params("arbitrary"),
        name="moe_experts",
    )(plan["tile_expert"], plan["n_active_tiles"], xs, wg, wu, wd)


def _moe_combine_kernel(*refs, final, rb, n_experts):
    if final:
        plan_ref, info_ref, ys_ref, x_ref, gfin_ref, o_ref, acc = refs
    else:
        (plan_ref, info_ref, ys_ref, x_ref, o_ref, acc), gfin_ref = refs, None
    w = pl.program_id(0)

    @pl.when(plan_ref[PLAN_FIRST, w] == 1)
    def _():
        acc[...] = jnp.zeros_like(acc)

    @pl.when(plan_ref[PLAN_VALID, w] == 1)
    def _():
        info = info_ref[...]
        ch = info.shape[0]
        e = plan_ref[PLAN_EXPERT, w]
        lane = lax.broadcasted_iota(jnp.int32, info.shape, 1)
        pos = jnp.sum(jnp.where(lane == e, info, 0.0), axis=-1, keepdims=True)
        gate = jnp.sum(jnp.where(lane == e + n_experts, info, 0.0), axis=-1, keepdims=True)
        local = pos - (plan_ref[PLAN_BLOCK, w] * rb).astype(F32)
        cols = lax.broadcasted_iota(jnp.int32, (ch, rb), 1).astype(F32)
        onehot = jnp.where(local == cols, 1.0, 0.0).astype(BF16)
        acc[...] += gate * jnp.dot(onehot, ys_ref[...], preferred_element_type=F32)

    @pl.when(plan_ref[PLAN_LAST, w] == 1)
    def _():
        o_ref[...] = _finish(x_ref[...], acc[...], gfin_ref)


def _moe_combine(ys, x, plan, g_final, *, rb, ch, n_experts):
    N, D = x.shape
    final = g_final is not None
    chunk = lambda c: pl.BlockSpec((ch, c), lambda w, p: (p[PLAN_CHUNK, w], 0))
    in_specs = [chunk(LANES), pl.BlockSpec((rb, D), lambda w, p: (p[PLAN_BLOCK, w], 0)), chunk(D)]
    args = [plan["combine_plan"], plan["token_info"], ys, x]
    if final:
        in_specs.append(pl.BlockSpec((1, D), lambda w, p: (0, 0)))
        args.append(g_final.reshape(1, D))
    grid_spec = pltpu.PrefetchScalarGridSpec(
        num_scalar_prefetch=1,
        grid=(plan["n_items"],),
        in_specs=in_specs,
        out_specs=chunk(D),
        scratch_shapes=[pltpu.VMEM((ch, D), F32)],
    )
    return pl.pallas_call(
        functools.partial(_moe_combine_kernel, final=final, rb=rb, n_experts=n_experts),
        grid_spec=grid_spec,
        out_shape=jax.ShapeDtypeStruct((N, D), F32),
        compiler_params=_params("arbitrary"),
        name="moe_combine",
    )(*args)


def _sc_rows_kernel(n_rows, n_out, d, scatter):
    per_worker = n_rows // SC_WORKERS
    assert n_rows % (SC_WORKERS * SC_WINDOW) == 0
    mesh = plsc.VectorSubcoreMesh(core_axis_name="c", subcore_axis_name="s")
    out_type = (jax.ShapeDtypeStruct((n_out, d), F32) if scatter
                else [jax.ShapeDtypeStruct((n_rows, d), F32)] * 2)

    def body(*refs):
        if scatter:
            src_hbm, hi_hbm, lo_hbm, out_hbm, idx_hi, idx_lo, rows_hi, rows_lo, sem_hi, sem_lo = refs
        else:
            src_hbm, hi_hbm, lo_hbm, out_hi_hbm, out_lo_hbm, idx_hi, idx_lo, rows_hi, rows_lo, sem_hi, sem_lo = refs
        worker = lax.axis_index("s") * SC_CORES + lax.axis_index("c")

        @pl.loop(0, per_worker // SC_WINDOW)
        def _(i):
            window = pl.ds(worker * per_worker + i * SC_WINDOW, SC_WINDOW)
            pltpu.sync_copy(hi_hbm.at[window], idx_hi)
            pltpu.sync_copy(lo_hbm.at[window], idx_lo)
            if scatter:
                pltpu.sync_copy(src_hbm.at[window], rows_hi)
                to_hi = pltpu.async_copy(rows_hi, out_hbm.at[idx_hi], sem_hi)
                to_lo = pltpu.async_copy(rows_hi, out_hbm.at[idx_lo], sem_lo)
                to_hi.wait()
                to_lo.wait()
            else:
                from_hi = pltpu.async_copy(src_hbm.at[idx_hi], rows_hi, sem_hi)
                from_lo = pltpu.async_copy(src_hbm.at[idx_lo], rows_lo, sem_lo)
                from_hi.wait()
                pltpu.sync_copy(rows_hi, out_hi_hbm.at[window])
                from_lo.wait()
                pltpu.sync_copy(rows_lo, out_lo_hbm.at[window])

    index_vec, row_buf = pltpu.VMEM((SC_WINDOW,), jnp.int32), pltpu.VMEM((SC_WINDOW, d), F32)
    return pl.kernel(body, mesh=mesh, out_type=out_type,
                     scratch_types=[index_vec, index_vec, row_buf, row_buf,
                                    pltpu.SemaphoreType.DMA, pltpu.SemaphoreType.DMA])


def _moe_mix_kernel(*refs, final):
    if final:
        x_ref, yh_ref, yl_ref, g_ref, gfin_ref, o_ref = refs
    else:
        (x_ref, yh_ref, yl_ref, g_ref, o_ref), gfin_ref = refs, None
    gates = g_ref[...]
    lane = lax.broadcasted_iota(jnp.int32, gates.shape, 1)
    g_hi = jnp.sum(jnp.where(lane == 0, gates, 0.0), axis=-1, keepdims=True)
    g_lo = jnp.sum(jnp.where(lane == 1, gates, 0.0), axis=-1, keepdims=True)
    o_ref[...] = _finish(x_ref[...], g_hi * yh_ref[...] + g_lo * yl_ref[...], gfin_ref)


def _moe_mix(x, y_hi, y_lo, pair_gates, g_final, *, tm=512):
    N, D = x.shape
    final = g_final is not None
    row = lambda c: pl.BlockSpec((tm, c), lambda i: (i, 0))
    in_specs, args = [row(D), row(D), row(D), row(LANES)], [x, y_hi, y_lo, pair_gates]
    if final:
        in_specs.append(pl.BlockSpec((1, D), lambda i: (0, 0)))
        args.append(g_final.reshape(1, D))
    return pl.pallas_call(
        functools.partial(_moe_mix_kernel, final=final),
        grid=(N // tm,), in_specs=in_specs, out_specs=row(D),
        out_shape=jax.ShapeDtypeStruct((N, D), F32),
        compiler_params=_params("parallel"),
        name="moe_mix",
    )(*args)


def _moe(h, x, route, wg, wu, wd, g_final, *, tm=512, rb=256, ch=512):
    N, D = x.shape
    E = wg.shape[0]
    tm, ch = min(tm, N), min(ch, N)
    rb = min(rb, tm)
    assert N % ch == 0 and tm % rb == 0
    plan = _plan_routing(route, E, tm=tm, rb=rb, ch=ch)
    rows_hi, rows_lo = plan["rows"]
    xs = _sc_rows_kernel(N, plan["p_max"], D, scatter=True)(h, rows_hi, rows_lo)
    ys = _moe_experts(xs, plan, wg, wu, wd, tm=tm)
    y_hi, y_lo = _sc_rows_kernel(N, plan["p_max"], D, scatter=False)(ys, rows_hi, rows_lo)
    return _moe_mix(x, y_hi, y_lo, plan["pair_gates"], g_final)


def kernel(x, norm_mix, w_in, w_pool, pool_scale, attn_gain, w_out, norm_ffn, ffn_wg, ffn_wu, ffn_wd,
           w_router, moe_wg, moe_wu, moe_wd, final_norm):
    B, S, D = x.shape
    depth = norm_mix.shape[0]
    bf = lambda t: t.astype(BF16)
    w_in, w_pool, w_out = bf(w_in), bf(w_pool), bf(w_out)
    ffn_wg, ffn_wu, ffn_wd = bf(ffn_wg), bf(ffn_wu), bf(ffn_wd)
    moe_wg, moe_wu, moe_wd = bf(moe_wg), bf(moe_wu), bf(moe_wd)
    R = RESIDUES
    assert S % R == 0
    x = x.reshape(B, S // R, R, D).transpose(0, 2, 1, 3)
    for l in range(depth):
        ya, q, k, v, q32, k32, v32 = _mix_in(x, norm_mix[l], w_in[l], w_pool[l], pool_scale[l])
        narrow = lambda dil: (R // dil) * 16 > WINDOW_STEPS
        branches = [_attn_branch(q32, k32, v32, dil) if narrow(dil) else _attn_branch(q, k, v, dil)
                    for _, dil in DILATED_PATTERNS]
        x1, h2 = _mix_out(x.reshape(B * S, D), ya.reshape(B * S, -1),
                          [o for o, _ in branches], [lse for _, lse in branches],
                          attn_gain[l], w_out[l], norm_ffn[l],
                          h_dtype=BF16 if l % 2 == 0 else F32)
        g_final = final_norm if l == depth - 1 else None
        i = l // 2
        if l % 2 == 0:
            x2 = _ffn(h2, x1, ffn_wg[i], ffn_wu[i], ffn_wd[i], g_final)
        else:
            route = _router(x1, norm_ffn[l], w_router[i])
            x2 = _moe(h2, x1, route, moe_wg[i], moe_wu[i], moe_wd[i], g_final)
        x = x2.reshape(B, R, S // R, D)
    return x.transpose(0, 2, 1, 3).reshape(B, S, D)
```

```python
import functools

import jax
import jax.numpy as jnp
from jax import lax
from jax.experimental import pallas as pl
from jax.experimental.pallas import tpu as pltpu
from jax.experimental.pallas import tpu_sc as plsc

F32 = jnp.float32
BF16 = jnp.bfloat16

EPS = 1e-6
LANES = 128
HEAD_DIM = 64
POOL_WINDOWS = (2, 4, 8, 16)
POOL_HIST = 8
DILATED_PATTERNS = ((128, 1), (512, 4), (2048, 16))
WINDOW_STEPS = 128
RESIDUES = 16
ATTN_STEP_ROWS = 1024
ATTN_UNROLL_QUERIES = 512
LOG2_E = 1.4426950408889634
SC_CORES, SC_SUBCORES = 2, 16
SC_WORKERS = SC_CORES * SC_SUBCORES
SC_WINDOW = 32
assert all(w // d == WINDOW_STEPS and RESIDUES % d == 0 for w, d in DILATED_PATTERNS)
TOP_K = 2
MASKED = -1e30
VMEM_LIMIT = 48 * 1024 * 1024


def _rms(x, g):
    return x * lax.rsqrt(jnp.mean(x * x, axis=-1, keepdims=True) + EPS) * g


def _params(*sem):
    return pltpu.CompilerParams(dimension_semantics=sem, vmem_limit_bytes=VMEM_LIMIT)


def _mix_in_kernel(x_ref, g_ref, w_ref, wp_ref, ps_ref, ya_ref, q_ref, k_ref, v_ref, q32_ref, k32_ref, v32_ref,
                   ubuf, uprev, *, d_pool, d_attn, ti, natural_x):
    s = pl.program_id(1)
    R, D = RESIDUES, x_ref.shape[-1]
    rows = R * ti
    blk = lambda t: t.reshape(R, ti, t.shape[-1])
    h = _rms(_token_rows(x_ref, natural_x), g_ref[...]).astype(BF16)
    proj = jnp.dot(h, w_ref[...], preferred_element_type=F32)
    scale = HEAD_DIM ** -0.5 * LOG2_E
    qkv = (proj[:, d_pool:d_pool + d_attn] * scale, proj[:, d_pool + d_attn:d_pool + 2 * d_attn],
           proj[:, d_pool + 2 * d_attn:])
    for t, ref, ref32 in zip(qkv, (q_ref, k_ref, v_ref), (q32_ref, k32_ref, v32_ref)):
        ref[...] = blk(t.astype(BF16))
        ref32[...] = blk(t)

    @pl.when(s == 0)
    def _():
        ubuf[:, 0:POOL_HIST, :] = jnp.zeros((R, POOL_HIST, d_pool), F32)

    ubuf[:, POOL_HIST:POOL_HIST + ti, :] = blk(proj[:, :d_pool])
    uprev[...] = ubuf[:, POOL_HIST - 1:POOL_HIST - 1 + ti, :]
    at_start = (s * ti + lax.broadcasted_iota(jnp.int32, (ti, 1), 0)) == 0
    group = d_pool // len(POOL_WINDOWS)
    zs = []
    for gi, w in enumerate(POOL_WINDOWS):
        cols = slice(gi * group, (gi + 1) * group)
        ds = []
        for r in range(R):
            ug = ubuf[r, POOL_HIST:POOL_HIST + ti, cols]
            win = ug
            for back in range(1, w):
                rr = r - back
                win = win + (ubuf[rr, POOL_HIST:POOL_HIST + ti, cols] if rr >= 0 else uprev[rr + R, :, cols])
            cnt = jnp.where(at_start, float(min(r + 1, w)), float(w))
            ds.append(win / cnt - ug)
        d = jnp.concatenate(ds, axis=0).astype(BF16)
        zs.append(jnp.dot(d, wp_ref[gi], preferred_element_type=F32))
    z = jnp.concatenate(zs, axis=-1)
    ya_ref[...] = blk(_rms(z, ps_ref[...]).astype(BF16))
    ubuf[:, POOL_HIST - 1:POOL_HIST, :] = ubuf[:, POOL_HIST + ti - 1:POOL_HIST + ti, :]


def _mix_in(x, g, w_in, w_pool, pool_scale, *, natural_x, ti=32):
    B, R, SI, D = x.shape
    if natural_x:
        R, SI = SI, R
    d_pool = pool_scale.shape[-1]
    d_in = w_in.shape[-1]
    d_attn = (d_in - d_pool) // 3
    ti = min(ti, SI)
    assert R == RESIDUES >= max(POOL_WINDOWS) and SI % ti == 0 and ti % 16 == 0
    assert d_pool % (LANES * len(POOL_WINDOWS)) == 0
    seq_spec = lambda c: pl.BlockSpec((None, R, ti, c), lambda b, s: (b, 0, s, 0))
    full = lambda shape: pl.BlockSpec(shape, lambda b, s: (0,) * len(shape))
    out_sds = lambda c, dtype=BF16: jax.ShapeDtypeStruct((B, R, SI, c), dtype)
    return pl.pallas_call(
        functools.partial(_mix_in_kernel, d_pool=d_pool, d_attn=d_attn, ti=ti, natural_x=natural_x),
        grid=(B, SI // ti),
        in_specs=[pl.BlockSpec((None, ti, R, D), lambda b, s: (b, s, 0, 0)) if natural_x else seq_spec(D),
                  full((1, D)), full((D, d_in)), full(w_pool.shape), full((1, d_pool))],
        out_specs=[seq_spec(d_pool)] + [seq_spec(d_attn)] * 6,
        out_shape=[out_sds(d_pool)] + [out_sds(d_attn)] * 3 + [out_sds(d_attn, F32)] * 3,
        scratch_shapes=[pltpu.VMEM((R, POOL_HIST + ti, d_pool), F32), pltpu.VMEM((R, ti, d_pool), F32)],
        compiler_params=_params("parallel", "arbitrary"),
        name="mix_in",
    )(x, g.reshape(1, D), w_in, w_pool, pool_scale.reshape(1, d_pool))


def _attn_kernel(q_ref, k_ref, v_ref, o_ref, lse_ref, bias_ref, *, nq, strips, d_attn):
    _, classes, chunk, _ = q_ref.shape
    L = nq // strips
    nk, n_blocks = 2 * nq, chunk // L
    first_block = pl.program_id(2) * n_blocks
    row = lax.broadcasted_iota(jnp.int32, (nq, 1), 0)
    col = lax.broadcasted_iota(jnp.int32, (1, nk), 1)
    q_strip, q_row = row >> (L.bit_length() - 1), row & (L - 1)
    k_strip, k_row = col >> ((2 * L).bit_length() - 1), col & (2 * L - 1)
    back = strips * (q_row - k_row) + (q_strip - k_strip)

    def band(offset):
        rel = back + strips * offset
        return jnp.where((rel >= 0) & (rel <= WINDOW_STEPS), 0.0, MASKED).astype(F32)

    bias_ref[0] = band(0)
    bias_ref[1] = band(L)
    lane = lax.broadcasted_iota(jnp.int32, (nq, LANES), 1)
    lo_half = lane < HEAD_DIM

    def block(it, carry):
        g, j_local = it // n_blocks, it % n_blocks
        j = first_block + j_local
        q_rows = pl.ds(pl.multiple_of(j_local * L, L), L)
        k_rows = pl.ds(pl.multiple_of(jnp.maximum(j - 1, 0) * L, L), 2 * L)
        bias = bias_ref[jnp.minimum(j, 1)]
        lse_tile = jnp.zeros((nq, LANES), F32)
        for hp in range(d_attn // LANES):
            cols = slice(hp * LANES, (hp + 1) * LANES)
            qp = q_ref[:, g, q_rows, cols].reshape(nq, LANES).astype(BF16)
            kk = k_ref[:, g, k_rows, cols].reshape(nk, LANES).astype(BF16)
            vv = v_ref[:, g, k_rows, cols].reshape(nk, LANES).astype(BF16)
            outs, lses = [], []
            for sub in range(2):
                keep = lo_half if sub == 0 else jnp.logical_not(lo_half)
                qm = jnp.where(keep, qp, jnp.zeros_like(qp))
                s = lax.dot_general(qm, kk, (((1,), (1,)), ((), ())), preferred_element_type=F32) + bias
                m = jnp.max(s, axis=-1, keepdims=True)
                p = jnp.exp2(s - m)
                l = jnp.sum(p, axis=-1, keepdims=True)
                o = jnp.dot(p.astype(BF16), vv, preferred_element_type=F32)
                outs.append(o / l)
                lses.append(m + jnp.log2(l))
            o_pair = jnp.where(lo_half, outs[0], outs[1]).astype(o_ref.dtype)
            o_ref[:, g, q_rows, cols] = o_pair.reshape(strips, L, LANES)
            pair_lse = jnp.where(lane == 2 * hp, lses[0], lses[1])
            lse_tile = jnp.where((lane >> 1) == hp, pair_lse, lse_tile)
        lse_ref[:, g, q_rows, :] = lse_tile.reshape(strips, L, LANES)
        return carry

    lax.fori_loop(0, classes * n_blocks, block, 0, unroll=ATTN_UNROLL_QUERIES // nq)


def _attn_branch(q, k, v, dil):
    B, R, SI, C = q.shape
    strips = R // dil
    row_tile = 8 * 4 // q.dtype.itemsize
    nq = max(WINDOW_STEPS, strips * row_tile)
    L = nq // strips
    classes = max(1, min(dil, ATTN_STEP_ROWS // (strips * SI)))
    chunk = min(SI, max(L, ATTN_STEP_ROWS // (strips * classes)))
    assert R % dil == 0 and SI % chunk == 0 and chunk % L == 0 and SI >= 2 * L and dil % classes == 0
    assert C % LANES == 0 and C // HEAD_DIM <= LANES and nq & (nq - 1) == 0 and L & (L - 1) == 0
    view = lambda t: t.reshape(B, strips, dil, SI, t.shape[-1])
    q_spec = lambda c: pl.BlockSpec((None, strips, classes, chunk, c), lambda b, g, s: (b, 0, g, s, 0))
    kv_spec = pl.BlockSpec((None, strips, classes, SI, C), lambda b, g, s: (b, 0, g, 0, 0),
                           pipeline_mode=pl.Buffered(1 if SI > chunk else 2))
    o, lse = pl.pallas_call(
        functools.partial(_attn_kernel, nq=nq, strips=strips, d_attn=C),
        scratch_shapes=[pltpu.VMEM((2, nq, 2 * nq), F32)],
        grid=(B, dil // classes, SI // chunk),
        in_specs=[q_spec(C), kv_spec, kv_spec],
        out_specs=[q_spec(C), q_spec(LANES)],
        out_shape=[jax.ShapeDtypeStruct((B, strips, dil, SI, C), q.dtype),
                   jax.ShapeDtypeStruct((B, strips, dil, SI, LANES), F32)],
        compiler_params=_params("parallel", "parallel", "arbitrary"),
        name=f"attn_d{dil}",
    )(view(q), view(k), view(v))
    return o.reshape(B, R, SI, C), lse.reshape(B, R, SI, LANES)


def _token_rows(ref, natural=False):
    t = ref[...]
    if natural:
        t = pltpu.einshape("ird->rid", t)
    return t.reshape(-1, t.shape[-1])


def _mix_out_kernel(x_ref, ya_ref, o1_ref, o2_ref, o3_ref, l1_ref, l2_ref, l3_ref, gain_ref, wo_ref,
                    gffn_ref, exp_ref, x1_ref, h2_ref, *, d_pool, natural_x):
    lses = [_token_rows(l) for l in (l1_ref, l2_ref, l3_ref)]
    top = jnp.maximum(jnp.maximum(lses[0], lses[1]), lses[2])
    es = [jnp.exp2(l - top) for l in lses]
    den = es[0] + es[1] + es[2]
    expand = exp_ref[...]

    def per_lane(w):
        hi = w.astype(BF16)
        lo = (w - hi.astype(F32)).astype(BF16)
        return (jnp.dot(hi, expand, preferred_element_type=F32)
                + jnp.dot(lo, expand, preferred_element_type=F32))

    o = 0.0
    for e, o_ref in zip(es, (o1_ref, o2_ref, o3_ref)):
        o = o + per_lane(e / den) * _token_rows(o_ref).astype(F32)
    yb = _rms(o, gain_ref[...]).astype(BF16)
    y = (jnp.dot(_token_rows(ya_ref), wo_ref[:d_pool, :], preferred_element_type=F32)
         + jnp.dot(yb, wo_ref[d_pool:, :], preferred_element_type=F32))
    x1 = _token_rows(x_ref, natural_x) + y
    x1_ref[...] = x1.reshape(x1_ref.shape)
    h2_ref[...] = _rms(x1, gffn_ref[...]).astype(h2_ref.dtype).reshape(h2_ref.shape)


def _mix_out(x, ya, outs, lses, attn_gain, w_out, g_ffn, *, h_dtype, natural_x, ti=32):
    B, R, SI, d_pool = ya.shape
    D = x.shape[-1]
    d_attn = outs[0].shape[-1]
    ti = min(ti, SI)
    head_of_lane = jnp.arange(d_attn) // HEAD_DIM
    expand = (jnp.arange(LANES)[:, None] == head_of_lane[None, :]).astype(BF16)
    seq = lambda c: pl.BlockSpec((None, R, ti, c), lambda b, s: (b, 0, s, 0))
    x_spec = pl.BlockSpec((None, ti, R, D), lambda b, s: (b, s, 0, 0)) if natural_x else seq(D)
    full = lambda shape: pl.BlockSpec(shape, lambda b, s: (0,) * len(shape))
    return pl.pallas_call(
        functools.partial(_mix_out_kernel, d_pool=d_pool, natural_x=natural_x),
        grid=(B, SI // ti),
        in_specs=[x_spec, seq(d_pool)] + [seq(d_attn)] * 3 + [seq(LANES)] * 3
                 + [full((1, d_attn)), full(w_out.shape), full((1, D)), full(expand.shape)],
        out_specs=[seq(D), seq(D)],
        out_shape=[jax.ShapeDtypeStruct((B, R, SI, D), F32), jax.ShapeDtypeStruct((B, R, SI, D), h_dtype)],
        compiler_params=_params("parallel", "parallel"),
        name="mix_out",
    )(x, ya, *outs, *lses, attn_gain.reshape(1, d_attn), w_out, g_ffn.reshape(1, D), expand)


def _swiglu_hidden(h, wg, wu):
    a = jnp.dot(h, wg, preferred_element_type=F32)
    b = jnp.dot(h, wu, preferred_element_type=F32)
    return (a * jax.nn.sigmoid(a) * b).astype(BF16)


def _finish(x, y, gfin_ref):
    out = x + y
    return out if gfin_ref is None else _rms(out, gfin_ref[...])


def _ffn_kernel(*refs, final, tf):
    if final:
        h_ref, x_ref, wg_ref, wu_ref, wd_ref, gfin_ref, o_ref, acc = refs
    else:
        (h_ref, x_ref, wg_ref, wu_ref, wd_ref, o_ref, acc), gfin_ref = refs, None
    h = h_ref[...]
    for c in range(wg_ref.shape[-1] // tf):
        cols = slice(c * tf, (c + 1) * tf)
        hid = _swiglu_hidden(h, wg_ref[:, cols], wu_ref[:, cols])
        part = jnp.dot(hid, wd_ref[cols, :], preferred_element_type=F32)
        if c == 0:
            acc[...] = part
        else:
            acc[...] += part
    o_ref[...] = _finish(x_ref[...], acc[...], gfin_ref)


def _pick_chunk(n, target):
    best = None
    for c in range(LANES, min(n, target) + 1, LANES):
        if n % c == 0:
            best = c
    assert best is not None
    return best


def _ffn(h, x, wg, wu, wd, g_final, *, tm=512, tf_target=1536):
    N, D = x.shape
    F = wg.shape[-1]
    tm = min(tm, N)
    tf = _pick_chunk(F, tf_target)
    final = g_final is not None
    row = pl.BlockSpec((tm, D), lambda i: (i, 0))
    resident = lambda shape: pl.BlockSpec(shape, lambda i: (0, 0), pipeline_mode=pl.Buffered(1))
    in_specs = [row, row, resident((D, F)), resident((D, F)), resident((F, D))]
    args = [h, x, wg, wu, wd]
    if final:
        in_specs.append(pl.BlockSpec((1, D), lambda i: (0, 0)))
        args.append(g_final.reshape(1, D))
    return pl.pallas_call(
        functools.partial(_ffn_kernel, final=final, tf=tf),
        grid=(N // tm,),
        in_specs=in_specs,
        out_specs=row,
        out_shape=jax.ShapeDtypeStruct((N, D), F32),
        scratch_shapes=[pltpu.VMEM((tm, D), F32)],
        compiler_params=_params("parallel"),
        name="ffn_dense",
    )(*args)


def _router_kernel(x_ref, g_ref, wr_ref, gate_ref, *, n_experts):
    h = _rms(x_ref[...], g_ref[...])
    logits = jnp.dot(h, wr_ref[...], preferred_element_type=F32, precision=lax.Precision.HIGHEST)
    lane = lax.broadcasted_iota(jnp.int32, logits.shape, 1)
    logits = jnp.where(lane < n_experts, logits, -jnp.inf)
    picks = []
    for _ in range(TOP_K):
        m = jnp.max(logits, axis=-1, keepdims=True)
        idx = jnp.min(jnp.where(logits == m, lane, LANES), axis=-1, keepdims=True)
        picks.append((m, idx))
        logits = jnp.where(lane == idx, -jnp.inf, logits)
    (m1, i1), (m2, i2) = picks
    e2 = jnp.exp(m2 - m1)
    g1 = 1.0 / (1.0 + e2)
    g2 = e2 / (1.0 + e2)
    picked = jnp.logical_or(lane == i1 + n_experts, lane == i2 + n_experts)
    gate_ref[...] = jnp.where(lane == i1, g1, jnp.where(lane == i2, g2, jnp.where(picked, 1.0, 0.0)))


def _router(x, g_ffn, w_router, *, tm=512):
    N, D = x.shape
    E = w_router.shape[-1]
    assert 2 * E <= LANES and TOP_K == 2
    tm = min(tm, N)
    wr = jnp.zeros((D, LANES), F32).at[:, :E].set(w_router)
    return pl.pallas_call(
        functools.partial(_router_kernel, n_experts=E),
        grid=(N // tm,),
        in_specs=[pl.BlockSpec((tm, D), lambda i: (i, 0)),
                  pl.BlockSpec((1, D), lambda i: (0, 0)),
                  pl.BlockSpec((D, LANES), lambda i: (0, 0))],
        out_specs=pl.BlockSpec((tm, LANES), lambda i: (i, 0)),
        out_shape=jax.ShapeDtypeStruct((N, LANES), F32),
        compiler_params=_params("parallel"),
        name="router",
    )(x, g_ffn.reshape(1, D), wr)


def _plan_routing(route, n_experts, *, tm, rb, ch):
    N = route.shape[0]
    E = n_experts
    i32 = jnp.int32
    n_chunks = N // ch
    p_max = TOP_K * N + E * tm
    n_blocks, n_tiles = p_max // rb, p_max // tm
    n_items = n_blocks + E * n_chunks

    gate = route[:, :E].T
    sel = (route[:, E:2 * E].T > 0.5).astype(i32)
    rank = jnp.cumsum(sel, axis=1) - sel
    chunk_cnt = sel.reshape(E, n_chunks, ch).sum(-1)
    cum = jnp.concatenate([jnp.zeros((E, 1), i32), jnp.cumsum(chunk_cnt, axis=1)], axis=1)
    cnt = cum[:, -1]
    seg_len = (cnt + tm - 1) // tm * tm
    seg_end = jnp.cumsum(seg_len)
    seg_start = seg_end - seg_len
    pos = jnp.where(sel > 0, seg_start[:, None] + rank, -1)
    n_active_tiles = seg_end[-1] // tm
    tile_ids = jnp.arange(n_tiles, dtype=i32)
    tile_expert = jnp.minimum(jnp.sum(seg_end[None, :] <= tile_ids[:, None] * tm, axis=1), E - 1).astype(i32)

    def expand(counts, n_out):
        ends = jnp.cumsum(counts)
        total = ends[-1]
        w = jnp.minimum(jnp.arange(n_out, dtype=i32), total - 1)
        grp = jnp.sum(ends[None, :] <= w[:, None], axis=1).astype(i32)
        off = w - (ends[grp] - counts[grp])
        return grp, off, jnp.arange(n_out, dtype=i32) < total, w

    blk = jnp.arange(n_blocks, dtype=i32)
    b_exp = tile_expert[blk * rb // tm]
    lo = blk * rb - seg_start[b_exp]
    hi = jnp.minimum(lo + rb, cnt[b_exp])
    nonempty = lo < cnt[b_exp]
    b_cum = jnp.sum(jnp.where(b_exp[:, None, None] == jnp.arange(E, dtype=i32)[None, :, None], cum[None], 0),
                    axis=1)
    c_lo = jnp.sum(b_cum[:, 1:] <= lo[:, None], axis=1)
    c_hi = jnp.sum(b_cum[:, :-1] < hi[:, None], axis=1) - 1
    b_items = jnp.where(blk * rb < n_active_tiles * tm, jnp.where(nonempty, c_hi - c_lo + 1, 1), 0).astype(i32)
    c_lo = jnp.where(nonempty, c_lo, 0)
    g_blk, g_off, g_valid, _ = expand(b_items, n_items)
    g_first = jnp.logical_and(g_valid, g_off == 0)
    g_last = jnp.logical_and(g_valid, g_off == b_items[g_blk] - 1)
    gather_plan = jnp.stack([g_blk, c_lo[g_blk] + g_off, b_exp[g_blk],
                             g_first, g_last, g_valid]).astype(i32)

    first_row = seg_start[None, :] + cum[:, :-1].T
    last_row = seg_start[None, :] + cum[:, 1:].T - 1
    ce_items = jnp.where(chunk_cnt.T > 0, last_row // rb - first_row // rb + 1, 0).reshape(-1).astype(i32)
    c_grp, c_off, c_valid, c_w = expand(ce_items, n_items)
    c_chunk = c_grp // E
    c_blk = (first_row // rb).reshape(-1)[c_grp] + c_off
    per_chunk = ce_items.reshape(n_chunks, E).sum(1)
    chunk_end = jnp.cumsum(per_chunk)
    c_first = jnp.logical_and(c_valid, c_w == (chunk_end - per_chunk)[c_chunk])
    c_last = jnp.logical_and(c_valid, c_w == chunk_end[c_chunk] - 1)
    combine_plan = jnp.stack([c_blk, c_chunk, c_grp % E, c_first, c_last, c_valid]).astype(i32)

    token_info = jnp.concatenate([pos.T.astype(F32), gate.T, jnp.zeros((N, LANES - 2 * E), F32)], axis=1)
    row_hi = jnp.max(pos, axis=0)
    row_lo = jnp.sum(jnp.where(sel > 0, pos, 0), axis=0) - row_hi
    gate_hi = jnp.sum(jnp.where(pos == row_hi[None, :], gate, 0.0), axis=0)
    gate_lo = jnp.sum(jnp.where(pos == row_lo[None, :], gate, 0.0), axis=0)
    pair_gates = jnp.concatenate([gate_hi[:, None], gate_lo[:, None], jnp.zeros((N, LANES - 2), F32)], axis=1)
    return dict(pos=pos.reshape(E, 1, N), token_info=token_info, gather_plan=gather_plan,
                rows=(row_hi.astype(i32), row_lo.astype(i32)), pair_gates=pair_gates,
                combine_plan=combine_plan, tile_expert=tile_expert,
                n_active_tiles=n_active_tiles.reshape(1).astype(i32), p_max=p_max, n_items=n_items)


PLAN_BLOCK, PLAN_CHUNK, PLAN_EXPERT, PLAN_FIRST, PLAN_LAST, PLAN_VALID = range(6)


def _moe_gather_kernel(plan_ref, pos_ref, h_ref, xs_ref, acc, *, rb):
    w = pl.program_id(0)

    @pl.when(plan_ref[PLAN_FIRST, w] == 1)
    def _():
        acc[...] = jnp.zeros_like(acc)

    @pl.when(plan_ref[PLAN_VALID, w] == 1)
    def _():
        ch = h_ref.shape[0]
        rows = plan_ref[PLAN_BLOCK, w] * rb + lax.broadcasted_iota(jnp.int32, (rb, ch), 0)
        onehot = jnp.where(pos_ref[0] == rows, 1.0, 0.0).astype(BF16)
        acc[...] += jnp.dot(onehot, h_ref[...], preferred_element_type=F32)

    @pl.when(plan_ref[PLAN_LAST, w] == 1)
    def _():
        xs_ref[...] = acc[...].astype(BF16)


def _moe_gather(h, plan, *, rb, ch):
    N, D = h.shape
    grid_spec = pltpu.PrefetchScalarGridSpec(
        num_scalar_prefetch=1,
        grid=(plan["n_items"],),
        in_specs=[pl.BlockSpec((1, 1, ch), lambda w, p: (p[PLAN_EXPERT, w], 0, p[PLAN_CHUNK, w])),
                  pl.BlockSpec((ch, D), lambda w, p: (p[PLAN_CHUNK, w], 0))],
        out_specs=pl.BlockSpec((rb, D), lambda w, p: (p[PLAN_BLOCK, w], 0)),
        scratch_shapes=[pltpu.VMEM((rb, D), F32)],
    )
    return pl.pallas_call(
        functools.partial(_moe_gather_kernel, rb=rb),
        grid_spec=grid_spec,
        out_shape=jax.ShapeDtypeStruct((plan["p_max"], D), BF16),
        compiler_params=_params("arbitrary"),
        name="moe_gather",
    )(plan["gather_plan"], plan["pos"], h)


def _moe_experts_kernel(te_ref, na_ref, xs_ref, wg_ref, wu_ref, wd_ref, o_ref, acc, *, tf):
    @pl.when(pl.program_id(0) < na_ref[0])
    def _():
        x = xs_ref[...].astype(BF16)
        for c in range(wg_ref.shape[-1] // tf):
            cols = slice(c * tf, (c + 1) * tf)
            hid = _swiglu_hidden(x, wg_ref[0, :, cols], wu_ref[0, :, cols])
            part = jnp.dot(hid, wd_ref[0, cols, :], preferred_element_type=F32)
            if c == 0:
                acc[...] = part
            else:
                acc[...] += part
        o_ref[...] = acc[...]


def _moe_experts(xs, plan, wg, wu, wd, *, tm, tf_target=512):
    P, D = xs.shape
    E, _, F = wg.shape
    tf = _pick_chunk(F, tf_target)
    tile = lambda i, te, na: (jnp.minimum(i, na[0] - 1), 0)
    resident = pl.Buffered(1)
    grid_spec = pltpu.PrefetchScalarGridSpec(
        num_scalar_prefetch=2,
        grid=(P // tm,),
        in_specs=[pl.BlockSpec((tm, D), tile),
                  pl.BlockSpec((1, D, F), lambda i, te, na: (te[i], 0, 0), pipeline_mode=resident),
                  pl.BlockSpec((1, D, F), lambda i, te, na: (te[i], 0, 0), pipeline_mode=resident),
                  pl.BlockSpec((1, F, D), lambda i, te, na: (te[i], 0, 0), pipeline_mode=resident)],
        out_specs=pl.BlockSpec((tm, D), tile),
        scratch_shapes=[pltpu.VMEM((tm, D), F32)],
    )
    return pl.pallas_call(
        functools.partial(_moe_experts_kernel, tf=tf),
        grid_spec=grid_spec,
        out_shape=jax.ShapeDtypeStruct((P, D), F32),
        compiler_params=_params("arbitrary"),
        name="moe_experts",
    )(plan["tile_expert"], plan["n_active_tiles"], xs, wg, wu, wd)


def _moe_combine_kernel(*refs, final, rb, n_experts):
    if final:
        plan_ref, info_ref, ys_ref, x_ref, gfin_ref, o_ref, acc = refs
    else:
        (plan_ref, info_ref, ys_ref, x_ref, o_ref, acc), gfin_ref = refs, None
    w = pl.program_id(0)

    @pl.when(plan_ref[PLAN_FIRST, w] == 1)
    def _():
        acc[...] = jnp.zeros_like(acc)

    @pl.when(plan_ref[PLAN_VALID, w] == 1)
    def _():
        info = info_ref[...]
        ch = info.shape[0]
        e = plan_ref[PLAN_EXPERT, w]
        lane = lax.broadcasted_iota(jnp.int32, info.shape, 1)
        pos = jnp.sum(jnp.where(lane == e, info, 0.0), axis=-1, keepdims=True)
        gate = jnp.sum(jnp.where(lane == e + n_experts, info, 0.0), axis=-1, keepdims=True)
        local = pos - (plan_ref[PLAN_BLOCK, w] * rb).astype(F32)
        cols = lax.broadcasted_iota(jnp.int32, (ch, rb), 1).astype(F32)
        onehot = jnp.where(local == cols, 1.0, 0.0).astype(BF16)
        acc[...] += gate * jnp.dot(onehot, ys_ref[...], preferred_element_type=F32)

    @pl.when(plan_ref[PLAN_LAST, w] == 1)
    def _():
        o_ref[...] = _finish(x_ref[...], acc[...], gfin_ref)


def _moe_combine(ys, x, plan, g_final, *, rb, ch, n_experts):
    N, D = x.shape
    final = g_final is not None
    chunk = lambda c: pl.BlockSpec((ch, c), lambda w, p: (p[PLAN_CHUNK, w], 0))
    in_specs = [chunk(LANES), pl.BlockSpec((rb, D), lambda w, p: (p[PLAN_BLOCK, w], 0)), chunk(D)]
    args = [plan["combine_plan"], plan["token_info"], ys, x]
    if final:
        in_specs.append(pl.BlockSpec((1, D), lambda w, p: (0, 0)))
        args.append(g_final.reshape(1, D))
    grid_spec = pltpu.PrefetchScalarGridSpec(
        num_scalar_prefetch=1,
        grid=(plan["n_items"],),
        in_specs=in_specs,
        out_specs=chunk(D),
        scratch_shapes=[pltpu.VMEM((ch, D), F32)],
    )
    return pl.pallas_call(
        functools.partial(_moe_combine_kernel, final=final, rb=rb, n_experts=n_experts),
        grid_spec=grid_spec,
        out_shape=jax.ShapeDtypeStruct((N, D), F32),
        compiler_params=_params("arbitrary"),
        name="moe_combine",
    )(*args)


def _sc_rows_kernel(n_rows, n_out, d, scatter):
    per_worker = n_rows // SC_WORKERS
    assert n_rows % (SC_WORKERS * SC_WINDOW) == 0
    mesh = plsc.VectorSubcoreMesh(core_axis_name="c", subcore_axis_name="s")
    out_type = (jax.ShapeDtypeStruct((n_out, d), F32) if scatter
                else [jax.ShapeDtypeStruct((n_rows, d), F32)] * 2)

    def body(*refs):
        if scatter:
            src_hbm, hi_hbm, lo_hbm, out_hbm, idx_hi, idx_lo, rows_hi, rows_lo, sem_hi, sem_lo = refs
        else:
            src_hbm, hi_hbm, lo_hbm, out_hi_hbm, out_lo_hbm, idx_hi, idx_lo, rows_hi, rows_lo, sem_hi, sem_lo = refs
        worker = lax.axis_index("s") * SC_CORES + lax.axis_index("c")

        @pl.loop(0, per_worker // SC_WINDOW)
        def _(i):
            window = pl.ds(worker * per_worker + i * SC_WINDOW, SC_WINDOW)
            pltpu.sync_copy(hi_hbm.at[window], idx_hi)
            pltpu.sync_copy(lo_hbm.at[window], idx_lo)
            if scatter:
                pltpu.sync_copy(src_hbm.at[window], rows_hi)
                to_hi = pltpu.async_copy(rows_hi, out_hbm.at[idx_hi], sem_hi)
                to_lo = pltpu.async_copy(rows_hi, out_hbm.at[idx_lo], sem_lo)
                to_hi.wait()
                to_lo.wait()
            else:
                from_hi = pltpu.async_copy(src_hbm.at[idx_hi], rows_hi, sem_hi)
                from_lo = pltpu.async_copy(src_hbm.at[idx_lo], rows_lo, sem_lo)
                from_hi.wait()
                pltpu.sync_copy(rows_hi, out_hi_hbm.at[window])
                from_lo.wait()
                pltpu.sync_copy(rows_lo, out_lo_hbm.at[window])

    index_vec, row_buf = pltpu.VMEM((SC_WINDOW,), jnp.int32), pltpu.VMEM((SC_WINDOW, d), F32)
    return pl.kernel(body, mesh=mesh, out_type=out_type,
                     scratch_types=[index_vec, index_vec, row_buf, row_buf,
                                    pltpu.SemaphoreType.DMA, pltpu.SemaphoreType.DMA])


def _moe_mix_kernel(*refs, final, natural_out):
    if final:
        x_ref, yh_ref, yl_ref, g_ref, gfin_ref, o_ref = refs
    else:
        (x_ref, yh_ref, yl_ref, g_ref, o_ref), gfin_ref = refs, None
    gates = _token_rows(g_ref)
    lane = lax.broadcasted_iota(jnp.int32, gates.shape, 1)
    g_hi = jnp.sum(jnp.where(lane == 0, gates, 0.0), axis=-1, keepdims=True)
    g_lo = jnp.sum(jnp.where(lane == 1, gates, 0.0), axis=-1, keepdims=True)
    out = _finish(_token_rows(x_ref), g_hi * _token_rows(yh_ref) + g_lo * _token_rows(yl_ref), gfin_ref)
    out = out.reshape(x_ref.shape)
    o_ref[...] = pltpu.einshape("rid->ird", out) if natural_out else out


def _moe_mix(x, y_hi, y_lo, pair_gates, g_final, *, natural_out, ti=32):
    B, R, SI, D = x.shape
    ti = min(ti, SI)
    final = g_final is not None
    seq = lambda c: pl.BlockSpec((None, R, ti, c), lambda b, s: (b, 0, s, 0))
    as_tokens = lambda t: t.reshape(B, R, SI, t.shape[-1])
    in_specs, args = [seq(D), seq(D), seq(D), seq(LANES)], [x, as_tokens(y_hi), as_tokens(y_lo), as_tokens(pair_gates)]
    if final:
        in_specs.append(pl.BlockSpec((1, D), lambda b, s: (0, 0)))
        args.append(g_final.reshape(1, D))
    out_spec = pl.BlockSpec((None, ti, R, D), lambda b, s: (b, s, 0, 0)) if natural_out else seq(D)
    return pl.pallas_call(
        functools.partial(_moe_mix_kernel, final=final, natural_out=natural_out),
        grid=(B, SI // ti), in_specs=in_specs, out_specs=out_spec,
        out_shape=jax.ShapeDtypeStruct((B, SI, R, D) if natural_out else (B, R, SI, D), F32),
        compiler_params=_params("parallel", "parallel"),
        name="moe_mix",
    )(*args)


def _moe(h, x, route, wg, wu, wd, g_final, *, natural_out, tm=512, rb=256, ch=512):
    D = x.shape[-1]
    N = h.shape[0]
    E = wg.shape[0]
    tm, ch = min(tm, N), min(ch, N)
    rb = min(rb, tm)
    assert N % ch == 0 and tm % rb == 0
    plan = _plan_routing(route, E, tm=tm, rb=rb, ch=ch)
    rows_hi, rows_lo = plan["rows"]
    xs = _sc_rows_kernel(N, plan["p_max"], D, scatter=True)(h, rows_hi, rows_lo)
    ys = _moe_experts(xs, plan, wg, wu, wd, tm=tm)
    y_hi, y_lo = _sc_rows_kernel(N, plan["p_max"], D, scatter=False)(ys, rows_hi, rows_lo)
    return _moe_mix(x, y_hi, y_lo, plan["pair_gates"], g_final, natural_out=natural_out)


def kernel(x, norm_mix, w_in, w_pool, pool_scale, attn_gain, w_out, norm_ffn, ffn_wg, ffn_wu, ffn_wd,
           w_router, moe_wg, moe_wu, moe_wd, final_norm):
    B, S, D = x.shape
    depth = norm_mix.shape[0]
    bf = lambda t: t.astype(BF16)
    w_in, w_pool, w_out = bf(w_in), bf(w_pool), bf(w_out)
    ffn_wg, ffn_wu, ffn_wd = bf(ffn_wg), bf(ffn_wu), bf(ffn_wd)
    moe_wg, moe_wu, moe_wd = bf(moe_wg), bf(moe_wu), bf(moe_wd)
    R = RESIDUES
    assert S % R == 0
    N, SI = B * S, S // R
    x = x.reshape(B, SI, R, D)
    for l in range(depth):
        first, last = l == 0, l == depth - 1
        ya, q, k, v, q32, k32, v32 = _mix_in(x, norm_mix[l], w_in[l], w_pool[l], pool_scale[l], natural_x=first)
        narrow = lambda dil: (R // dil) * 16 > WINDOW_STEPS
        branches = [_attn_branch(q32, k32, v32, dil) if narrow(dil) else _attn_branch(q, k, v, dil)
                    for _, dil in DILATED_PATTERNS]
        x1, h2 = _mix_out(x, ya, [o for o, _ in branches], [lse for _, lse in branches],
                          attn_gain[l], w_out[l], norm_ffn[l], natural_x=first,
                          h_dtype=BF16 if l % 2 == 0 else F32)
        g_final = final_norm if last else None
        i = l // 2
        if l % 2 == 0:
            x = _ffn(h2.reshape(N, D), x1.reshape(N, D), ffn_wg[i], ffn_wu[i], ffn_wd[i], g_final)
            x = x.reshape(B, R, SI, D)
            if last:
                x = x.transpose(0, 2, 1, 3)
        else:
            route = _router(x1.reshape(N, D), norm_ffn[l], w_router[i])
            x = _moe(h2.reshape(N, D), x1, route, moe_wg[i], moe_wu[i], moe_wd[i], g_final, natural_out=last)
    return x.reshape(B, S, D)
```

```python
import functools

import jax
import jax.numpy as jnp
from jax import lax
from jax.experimental import pallas as pl
from jax.experimental.pallas import tpu as pltpu
from jax.experimental.pallas import tpu_sc as plsc

F32 = jnp.float32
BF16 = jnp.bfloat16

EPS = 1e-6
LANES = 128
HEAD_DIM = 64
POOL_WINDOWS = (2, 4, 8, 16)
POOL_HIST = 8
DILATED_PATTERNS = ((128, 1), (512, 4), (2048, 16))
WINDOW_STEPS = 128
RESIDUES = 16
ATTN_STEP_ROWS = 1024
ATTN_UNROLL_QUERIES = 512
LOG2_E = 1.4426950408889634
SC_CORES, SC_SUBCORES = 2, 16
SC_WORKERS = SC_CORES * SC_SUBCORES
SC_WINDOW = 32
assert all(w // d == WINDOW_STEPS and RESIDUES % d == 0 for w, d in DILATED_PATTERNS)
TOP_K = 2
MASKED = -1e30
VMEM_LIMIT = 48 * 1024 * 1024
MOE_VMEM_LIMIT = 56 * 1024 * 1024


def _rms(x, g):
    return x * lax.rsqrt(jnp.mean(x * x, axis=-1, keepdims=True) + EPS) * g


def _params(*sem, vmem_limit=VMEM_LIMIT):
    return pltpu.CompilerParams(dimension_semantics=sem, vmem_limit_bytes=vmem_limit)


def _mix_in_kernel(x_ref, g_ref, w_ref, wp_ref, ps_ref, ya_ref, q_ref, k_ref, v_ref, q32_ref, k32_ref, v32_ref,
                   ubuf, uprev, *, d_pool, d_attn, ti, natural_x):
    s = pl.program_id(1)
    R, D = RESIDUES, x_ref.shape[-1]
    rows = R * ti
    blk = lambda t: t.reshape(R, ti, t.shape[-1])
    h = _rms(_token_rows(x_ref, natural_x), g_ref[...]).astype(BF16)
    proj = jnp.dot(h, w_ref[...], preferred_element_type=F32)
    scale = HEAD_DIM ** -0.5 * LOG2_E
    qkv = (proj[:, d_pool:d_pool + d_attn] * scale, proj[:, d_pool + d_attn:d_pool + 2 * d_attn],
           proj[:, d_pool + 2 * d_attn:])
    for t, ref, ref32 in zip(qkv, (q_ref, k_ref, v_ref), (q32_ref, k32_ref, v32_ref)):
        ref[...] = blk(t.astype(BF16))
        ref32[...] = blk(t)

    @pl.when(s == 0)
    def _():
        ubuf[:, 0:POOL_HIST, :] = jnp.zeros((R, POOL_HIST, d_pool), F32)

    ubuf[:, POOL_HIST:POOL_HIST + ti, :] = blk(proj[:, :d_pool])
    uprev[...] = ubuf[:, POOL_HIST - 1:POOL_HIST - 1 + ti, :]
    at_start = (s * ti + lax.broadcasted_iota(jnp.int32, (ti, 1), 0)) == 0
    group = d_pool // len(POOL_WINDOWS)
    zs = []
    for gi, w in enumerate(POOL_WINDOWS):
        cols = slice(gi * group, (gi + 1) * group)
        ds = []
        for r in range(R):
            ug = ubuf[r, POOL_HIST:POOL_HIST + ti, cols]
            win = ug
            for back in range(1, w):
                rr = r - back
                win = win + (ubuf[rr, POOL_HIST:POOL_HIST + ti, cols] if rr >= 0 else uprev[rr + R, :, cols])
            cnt = jnp.where(at_start, float(min(r + 1, w)), float(w))
            ds.append(win / cnt - ug)
        d = jnp.concatenate(ds, axis=0).astype(BF16)
        zs.append(jnp.dot(d, wp_ref[gi], preferred_element_type=F32))
    z = jnp.concatenate(zs, axis=-1)
    ya_ref[...] = blk(_rms(z, ps_ref[...]).astype(BF16))
    ubuf[:, POOL_HIST - 1:POOL_HIST, :] = ubuf[:, POOL_HIST + ti - 1:POOL_HIST + ti, :]


def _mix_in(x, g, w_in, w_pool, pool_scale, *, natural_x, ti=32):
    B, R, SI, D = x.shape
    if natural_x:
        R, SI = SI, R
    d_pool = pool_scale.shape[-1]
    d_in = w_in.shape[-1]
    d_attn = (d_in - d_pool) // 3
    ti = min(ti, SI)
    assert R == RESIDUES >= max(POOL_WINDOWS) and SI % ti == 0 and ti % 16 == 0
    assert d_pool % (LANES * len(POOL_WINDOWS)) == 0
    seq_spec = lambda c: pl.BlockSpec((None, R, ti, c), lambda b, s: (b, 0, s, 0))
    full = lambda shape: pl.BlockSpec(shape, lambda b, s: (0,) * len(shape))
    out_sds = lambda c, dtype=BF16: jax.ShapeDtypeStruct((B, R, SI, c), dtype)
    return pl.pallas_call(
        functools.partial(_mix_in_kernel, d_pool=d_pool, d_attn=d_attn, ti=ti, natural_x=natural_x),
        grid=(B, SI // ti),
        in_specs=[pl.BlockSpec((None, ti, R, D), lambda b, s: (b, s, 0, 0)) if natural_x else seq_spec(D),
                  full((1, D)), full((D, d_in)), full(w_pool.shape), full((1, d_pool))],
        out_specs=[seq_spec(d_pool)] + [seq_spec(d_attn)] * 6,
        out_shape=[out_sds(d_pool)] + [out_sds(d_attn)] * 3 + [out_sds(d_attn, F32)] * 3,
        scratch_shapes=[pltpu.VMEM((R, POOL_HIST + ti, d_pool), F32), pltpu.VMEM((R, ti, d_pool), F32)],
        compiler_params=_params("parallel", "arbitrary"),
        name="mix_in",
    )(x, g.reshape(1, D), w_in, w_pool, pool_scale.reshape(1, d_pool))


def _attn_kernel(q_ref, k_ref, v_ref, o_ref, lse_ref, bias_ref, *, nq, strips, d_attn):
    _, classes, chunk, _ = q_ref.shape
    L = nq // strips
    nk, n_blocks = 2 * nq, chunk // L
    first_block = pl.program_id(2) * n_blocks
    row = lax.broadcasted_iota(jnp.int32, (nq, 1), 0)
    col = lax.broadcasted_iota(jnp.int32, (1, nk), 1)
    q_strip, q_row = row >> (L.bit_length() - 1), row & (L - 1)
    k_strip, k_row = col >> ((2 * L).bit_length() - 1), col & (2 * L - 1)
    back = strips * (q_row - k_row) + (q_strip - k_strip)

    def band(offset):
        rel = back + strips * offset
        return jnp.where((rel >= 0) & (rel <= WINDOW_STEPS), 0.0, MASKED).astype(F32)

    bias_ref[0] = band(0)
    bias_ref[1] = band(L)
    lane = lax.broadcasted_iota(jnp.int32, (nq, LANES), 1)
    lo_half = lane < HEAD_DIM

    def block(it, carry):
        g, j_local = it // n_blocks, it % n_blocks
        j = first_block + j_local
        q_rows = pl.ds(pl.multiple_of(j_local * L, L), L)
        k_rows = pl.ds(pl.multiple_of(jnp.maximum(j - 1, 0) * L, L), 2 * L)
        bias = bias_ref[jnp.minimum(j, 1)]
        lse_tile = jnp.zeros((nq, LANES), F32)
        for hp in range(d_attn // LANES):
            cols = slice(hp * LANES, (hp + 1) * LANES)
            qp = q_ref[:, g, q_rows, cols].reshape(nq, LANES).astype(BF16)
            kk = k_ref[:, g, k_rows, cols].reshape(nk, LANES).astype(BF16)
            vv = v_ref[:, g, k_rows, cols].reshape(nk, LANES).astype(BF16)
            outs, lses = [], []
            for sub in range(2):
                keep = lo_half if sub == 0 else jnp.logical_not(lo_half)
                qm = jnp.where(keep, qp, jnp.zeros_like(qp))
                s = lax.dot_general(qm, kk, (((1,), (1,)), ((), ())), preferred_element_type=F32) + bias
                m = jnp.max(s, axis=-1, keepdims=True)
                p = jnp.exp2(s - m)
                l = jnp.sum(p, axis=-1, keepdims=True)
                o = jnp.dot(p.astype(BF16), vv, preferred_element_type=F32)
                outs.append(o / l)
                lses.append(m + jnp.log2(l))
            o_pair = jnp.where(lo_half, outs[0], outs[1]).astype(o_ref.dtype)
            o_ref[:, g, q_rows, cols] = o_pair.reshape(strips, L, LANES)
            pair_lse = jnp.where(lane == 2 * hp, lses[0], lses[1])
            lse_tile = jnp.where((lane >> 1) == hp, pair_lse, lse_tile)
        lse_ref[:, g, q_rows, :] = lse_tile.reshape(strips, L, LANES)
        return carry

    lax.fori_loop(0, classes * n_blocks, block, 0, unroll=ATTN_UNROLL_QUERIES // nq)


def _attn_branch(q, k, v, dil):
    B, R, SI, C = q.shape
    strips = R // dil
    row_tile = 8 * 4 // q.dtype.itemsize
    nq = max(WINDOW_STEPS, strips * row_tile)
    L = nq // strips
    classes = max(1, min(dil, ATTN_STEP_ROWS // (strips * SI)))
    chunk = min(SI, max(L, ATTN_STEP_ROWS // (strips * classes)))
    assert R % dil == 0 and SI % chunk == 0 and chunk % L == 0 and SI >= 2 * L and dil % classes == 0
    assert C % LANES == 0 and C // HEAD_DIM <= LANES and nq & (nq - 1) == 0 and L & (L - 1) == 0
    view = lambda t: t.reshape(B, strips, dil, SI, t.shape[-1])
    q_spec = lambda c: pl.BlockSpec((None, strips, classes, chunk, c), lambda b, g, s: (b, 0, g, s, 0))
    kv_spec = pl.BlockSpec((None, strips, classes, SI, C), lambda b, g, s: (b, 0, g, 0, 0),
                           pipeline_mode=pl.Buffered(1 if SI > chunk else 2))
    o, lse = pl.pallas_call(
        functools.partial(_attn_kernel, nq=nq, strips=strips, d_attn=C),
        scratch_shapes=[pltpu.VMEM((2, nq, 2 * nq), F32)],
        grid=(B, dil // classes, SI // chunk),
        in_specs=[q_spec(C), kv_spec, kv_spec],
        out_specs=[q_spec(C), q_spec(LANES)],
        out_shape=[jax.ShapeDtypeStruct((B, strips, dil, SI, C), q.dtype),
                   jax.ShapeDtypeStruct((B, strips, dil, SI, LANES), F32)],
        compiler_params=_params("parallel", "parallel", "arbitrary"),
        name=f"attn_d{dil}",
    )(view(q), view(k), view(v))
    return o.reshape(B, R, SI, C), lse.reshape(B, R, SI, LANES)


def _token_rows(ref, natural=False):
    t = ref[...]
    if natural:
        t = pltpu.einshape("ird->rid", t)
    return t.reshape(-1, t.shape[-1])


def _mix_out_kernel(x_ref, ya_ref, o1_ref, o2_ref, o3_ref, l1_ref, l2_ref, l3_ref, gain_ref, wo_ref,
                    gffn_ref, exp_ref, x1_ref, h2_ref, *, d_pool, natural_x):
    lses = [_token_rows(l) for l in (l1_ref, l2_ref, l3_ref)]
    top = jnp.maximum(jnp.maximum(lses[0], lses[1]), lses[2])
    es = [jnp.exp2(l - top) for l in lses]
    den = es[0] + es[1] + es[2]
    expand = exp_ref[...]

    def per_lane(w):
        hi = w.astype(BF16)
        lo = (w - hi.astype(F32)).astype(BF16)
        return (jnp.dot(hi, expand, preferred_element_type=F32)
                + jnp.dot(lo, expand, preferred_element_type=F32))

    o = 0.0
    for e, o_ref in zip(es, (o1_ref, o2_ref, o3_ref)):
        o = o + per_lane(e / den) * _token_rows(o_ref).astype(F32)
    yb = _rms(o, gain_ref[...]).astype(BF16)
    y = (jnp.dot(_token_rows(ya_ref), wo_ref[:d_pool, :], preferred_element_type=F32)
         + jnp.dot(yb, wo_ref[d_pool:, :], preferred_element_type=F32))
    x1 = _token_rows(x_ref, natural_x) + y
    x1_ref[...] = x1.reshape(x1_ref.shape)
    h2_ref[...] = _rms(x1, gffn_ref[...]).astype(h2_ref.dtype).reshape(h2_ref.shape)


def _mix_out(x, ya, outs, lses, attn_gain, w_out, g_ffn, *, h_dtype, natural_x, ti=32):
    B, R, SI, d_pool = ya.shape
    D = x.shape[-1]
    d_attn = outs[0].shape[-1]
    ti = min(ti, SI)
    head_of_lane = jnp.arange(d_attn) // HEAD_DIM
    expand = (jnp.arange(LANES)[:, None] == head_of_lane[None, :]).astype(BF16)
    seq = lambda c: pl.BlockSpec((None, R, ti, c), lambda b, s: (b, 0, s, 0))
    x_spec = pl.BlockSpec((None, ti, R, D), lambda b, s: (b, s, 0, 0)) if natural_x else seq(D)
    full = lambda shape: pl.BlockSpec(shape, lambda b, s: (0,) * len(shape))
    return pl.pallas_call(
        functools.partial(_mix_out_kernel, d_pool=d_pool, natural_x=natural_x),
        grid=(B, SI // ti),
        in_specs=[x_spec, seq(d_pool)] + [seq(d_attn)] * 3 + [seq(LANES)] * 3
                 + [full((1, d_attn)), full(w_out.shape), full((1, D)), full(expand.shape)],
        out_specs=[seq(D), seq(D)],
        out_shape=[jax.ShapeDtypeStruct((B, R, SI, D), F32), jax.ShapeDtypeStruct((B, R, SI, D), h_dtype)],
        compiler_params=_params("parallel", "parallel"),
        name="mix_out",
    )(x, ya, *outs, *lses, attn_gain.reshape(1, d_attn), w_out, g_ffn.reshape(1, D), expand)


def _swiglu_hidden(h, wg, wu):
    a = jnp.dot(h, wg, preferred_element_type=F32)
    b = jnp.dot(h, wu, preferred_element_type=F32)
    return (a * jax.nn.sigmoid(a) * b).astype(BF16)


def _finish(x, y, gfin_ref):
    out = x + y
    return out if gfin_ref is None else _rms(out, gfin_ref[...])


def _ffn_kernel(*refs, final, tf):
    if final:
        h_ref, x_ref, wg_ref, wu_ref, wd_ref, gfin_ref, o_ref, acc = refs
    else:
        (h_ref, x_ref, wg_ref, wu_ref, wd_ref, o_ref, acc), gfin_ref = refs, None
    h = h_ref[...]
    for c in range(wg_ref.shape[-1] // tf):
        cols = slice(c * tf, (c + 1) * tf)
        hid = _swiglu_hidden(h, wg_ref[:, cols], wu_ref[:, cols])
        part = jnp.dot(hid, wd_ref[cols, :], preferred_element_type=F32)
        if c == 0:
            acc[...] = part
        else:
            acc[...] += part
    o_ref[...] = _finish(x_ref[...], acc[...], gfin_ref)


def _pick_chunk(n, target):
    best = None
    for c in range(LANES, min(n, target) + 1, LANES):
        if n % c == 0:
            best = c
    assert best is not None
    return best


def _ffn(h, x, wg, wu, wd, g_final, *, tm=512, tf_target=1536):
    N, D = x.shape
    F = wg.shape[-1]
    tm = min(tm, N)
    tf = _pick_chunk(F, tf_target)
    final = g_final is not None
    row = pl.BlockSpec((tm, D), lambda i: (i, 0))
    resident = lambda shape: pl.BlockSpec(shape, lambda i: (0, 0), pipeline_mode=pl.Buffered(1))
    in_specs = [row, row, resident((D, F)), resident((D, F)), resident((F, D))]
    args = [h, x, wg, wu, wd]
    if final:
        in_specs.append(pl.BlockSpec((1, D), lambda i: (0, 0)))
        args.append(g_final.reshape(1, D))
    return pl.pallas_call(
        functools.partial(_ffn_kernel, final=final, tf=tf),
        grid=(N // tm,),
        in_specs=in_specs,
        out_specs=row,
        out_shape=jax.ShapeDtypeStruct((N, D), F32),
        scratch_shapes=[pltpu.VMEM((tm, D), F32)],
        compiler_params=_params("parallel"),
        name="ffn_dense",
    )(*args)


def _router_kernel(x_ref, g_ref, wr_ref, gate_ref, *, n_experts):
    h = _rms(x_ref[...], g_ref[...])
    logits = jnp.dot(h, wr_ref[...], preferred_element_type=F32, precision=lax.Precision.HIGHEST)
    lane = lax.broadcasted_iota(jnp.int32, logits.shape, 1)
    logits = jnp.where(lane < n_experts, logits, -jnp.inf)
    picks = []
    for _ in range(TOP_K):
        m = jnp.max(logits, axis=-1, keepdims=True)
        idx = jnp.min(jnp.where(logits == m, lane, LANES), axis=-1, keepdims=True)
        picks.append((m, idx))
        logits = jnp.where(lane == idx, -jnp.inf, logits)
    (m1, i1), (m2, i2) = picks
    e2 = jnp.exp(m2 - m1)
    g1 = 1.0 / (1.0 + e2)
    g2 = e2 / (1.0 + e2)
    picked = jnp.logical_or(lane == i1 + n_experts, lane == i2 + n_experts)
    gate_ref[...] = jnp.where(lane == i1, g1, jnp.where(lane == i2, g2, jnp.where(picked, 1.0, 0.0)))


def _router(x, g_ffn, w_router, *, tm=512):
    N, D = x.shape
    E = w_router.shape[-1]
    assert 2 * E <= LANES and TOP_K == 2
    tm = min(tm, N)
    wr = jnp.zeros((D, LANES), F32).at[:, :E].set(w_router)
    return pl.pallas_call(
        functools.partial(_router_kernel, n_experts=E),
        grid=(N // tm,),
        in_specs=[pl.BlockSpec((tm, D), lambda i: (i, 0)),
                  pl.BlockSpec((1, D), lambda i: (0, 0)),
                  pl.BlockSpec((D, LANES), lambda i: (0, 0))],
        out_specs=pl.BlockSpec((tm, LANES), lambda i: (i, 0)),
        out_shape=jax.ShapeDtypeStruct((N, LANES), F32),
        compiler_params=_params("parallel"),
        name="router",
    )(x, g_ffn.reshape(1, D), wr)


def _plan_routing(route, n_experts, *, tm, rb, ch):
    N = route.shape[0]
    E = n_experts
    i32 = jnp.int32
    n_chunks = N // ch
    p_max = TOP_K * N + E * tm
    n_blocks, n_tiles = p_max // rb, p_max // tm
    n_items = n_blocks + E * n_chunks

    gate = route[:, :E].T
    sel = (route[:, E:2 * E].T > 0.5).astype(i32)
    rank = jnp.cumsum(sel, axis=1) - sel
    chunk_cnt = sel.reshape(E, n_chunks, ch).sum(-1)
    cum = jnp.concatenate([jnp.zeros((E, 1), i32), jnp.cumsum(chunk_cnt, axis=1)], axis=1)
    cnt = cum[:, -1]
    seg_len = (cnt + tm - 1) // tm * tm
    seg_end = jnp.cumsum(seg_len)
    seg_start = seg_end - seg_len
    pos = jnp.where(sel > 0, seg_start[:, None] + rank, -1)
    n_active_tiles = seg_end[-1] // tm
    tile_ids = jnp.arange(n_tiles, dtype=i32)
    tile_expert = jnp.minimum(jnp.sum(seg_end[None, :] <= tile_ids[:, None] * tm, axis=1), E - 1).astype(i32)

    def expand(counts, n_out):
        ends = jnp.cumsum(counts)
        total = ends[-1]
        w = jnp.minimum(jnp.arange(n_out, dtype=i32), total - 1)
        grp = jnp.sum(ends[None, :] <= w[:, None], axis=1).astype(i32)
        off = w - (ends[grp] - counts[grp])
        return grp, off, jnp.arange(n_out, dtype=i32) < total, w

    blk = jnp.arange(n_blocks, dtype=i32)
    b_exp = tile_expert[blk * rb // tm]
    lo = blk * rb - seg_start[b_exp]
    hi = jnp.minimum(lo + rb, cnt[b_exp])
    nonempty = lo < cnt[b_exp]
    b_cum = jnp.sum(jnp.where(b_exp[:, None, None] == jnp.arange(E, dtype=i32)[None, :, None], cum[None], 0),
                    axis=1)
    c_lo = jnp.sum(b_cum[:, 1:] <= lo[:, None], axis=1)
    c_hi = jnp.sum(b_cum[:, :-1] < hi[:, None], axis=1) - 1
    b_items = jnp.where(blk * rb < n_active_tiles * tm, jnp.where(nonempty, c_hi - c_lo + 1, 1), 0).astype(i32)
    c_lo = jnp.where(nonempty, c_lo, 0)
    g_blk, g_off, g_valid, _ = expand(b_items, n_items)
    g_first = jnp.logical_and(g_valid, g_off == 0)
    g_last = jnp.logical_and(g_valid, g_off == b_items[g_blk] - 1)
    gather_plan = jnp.stack([g_blk, c_lo[g_blk] + g_off, b_exp[g_blk],
                             g_first, g_last, g_valid]).astype(i32)

    first_row = seg_start[None, :] + cum[:, :-1].T
    last_row = seg_start[None, :] + cum[:, 1:].T - 1
    ce_items = jnp.where(chunk_cnt.T > 0, last_row // rb - first_row // rb + 1, 0).reshape(-1).astype(i32)
    c_grp, c_off, c_valid, c_w = expand(ce_items, n_items)
    c_chunk = c_grp // E
    c_blk = (first_row // rb).reshape(-1)[c_grp] + c_off
    per_chunk = ce_items.reshape(n_chunks, E).sum(1)
    chunk_end = jnp.cumsum(per_chunk)
    c_first = jnp.logical_and(c_valid, c_w == (chunk_end - per_chunk)[c_chunk])
    c_last = jnp.logical_and(c_valid, c_w == chunk_end[c_chunk] - 1)
    combine_plan = jnp.stack([c_blk, c_chunk, c_grp % E, c_first, c_last, c_valid]).astype(i32)

    token_info = jnp.concatenate([pos.T.astype(F32), gate.T, jnp.zeros((N, LANES - 2 * E), F32)], axis=1)
    row_hi = jnp.max(pos, axis=0)
    row_lo = jnp.sum(jnp.where(sel > 0, pos, 0), axis=0) - row_hi
    gate_hi = jnp.sum(jnp.where(pos == row_hi[None, :], gate, 0.0), axis=0)
    gate_lo = jnp.sum(jnp.where(pos == row_lo[None, :], gate, 0.0), axis=0)
    pair_gates = jnp.concatenate([gate_hi[:, None], gate_lo[:, None], jnp.zeros((N, LANES - 2), F32)], axis=1)
    return dict(pos=pos.reshape(E, 1, N), token_info=token_info, gather_plan=gather_plan,
                rows=(row_hi.astype(i32), row_lo.astype(i32)), pair_gates=pair_gates,
                combine_plan=combine_plan, tile_expert=tile_expert,
                n_active_tiles=n_active_tiles.reshape(1).astype(i32), p_max=p_max, n_items=n_items)


PLAN_BLOCK, PLAN_CHUNK, PLAN_EXPERT, PLAN_FIRST, PLAN_LAST, PLAN_VALID = range(6)


def _moe_gather_kernel(plan_ref, pos_ref, h_ref, xs_ref, acc, *, rb):
    w = pl.program_id(0)

    @pl.when(plan_ref[PLAN_FIRST, w] == 1)
    def _():
        acc[...] = jnp.zeros_like(acc)

    @pl.when(plan_ref[PLAN_VALID, w] == 1)
    def _():
        ch = h_ref.shape[0]
        rows = plan_ref[PLAN_BLOCK, w] * rb + lax.broadcasted_iota(jnp.int32, (rb, ch), 0)
        onehot = jnp.where(pos_ref[0] == rows, 1.0, 0.0).astype(BF16)
        acc[...] += jnp.dot(onehot, h_ref[...], preferred_element_type=F32)

    @pl.when(plan_ref[PLAN_LAST, w] == 1)
    def _():
        xs_ref[...] = acc[...].astype(BF16)


def _moe_gather(h, plan, *, rb, ch):
    N, D = h.shape
    grid_spec = pltpu.PrefetchScalarGridSpec(
        num_scalar_prefetch=1,
        grid=(plan["n_items"],),
        in_specs=[pl.BlockSpec((1, 1, ch), lambda w, p: (p[PLAN_EXPERT, w], 0, p[PLAN_CHUNK, w])),
                  pl.BlockSpec((ch, D), lambda w, p: (p[PLAN_CHUNK, w], 0))],
        out_specs=pl.BlockSpec((rb, D), lambda w, p: (p[PLAN_BLOCK, w], 0)),
        scratch_shapes=[pltpu.VMEM((rb, D), F32)],
    )
    return pl.pallas_call(
        functools.partial(_moe_gather_kernel, rb=rb),
        grid_spec=grid_spec,
        out_shape=jax.ShapeDtypeStruct((plan["p_max"], D), BF16),
        compiler_params=_params("arbitrary"),
        name="moe_gather",
    )(plan["gather_plan"], plan["pos"], h)


def _moe_experts_kernel(te_ref, na_ref, xs_ref, wg_hbm, wu_hbm, wd_hbm, o_ref,
                        cache_g, cache_u, cache_d, stage_g, stage_u, stage_d, sems, acc, *, tf):
    i = pl.program_id(0)
    e = te_ref[i]
    n_chunks = cache_g.shape[-1] // tf
    active = i < na_ref[0]
    new_expert = jnp.logical_or(i == 0, e != te_ref[jnp.maximum(i - 1, 0)])

    def chunk_copies(c, slot):
        cols = pl.ds(c * tf, tf)
        return (pltpu.make_async_copy(wg_hbm.at[e, :, cols], stage_g.at[slot], sems.at[0, slot]),
                pltpu.make_async_copy(wu_hbm.at[e, :, cols], stage_u.at[slot], sems.at[1, slot]),
                pltpu.make_async_copy(wd_hbm.at[e, cols, :], stage_d.at[slot], sems.at[2, slot]))

    def tile_ffn(load_weights):
        x = xs_ref[...].astype(BF16)
        if load_weights:
            for cp in chunk_copies(0, 0):
                cp.start()
        for c in range(n_chunks):
            cols = slice(c * tf, (c + 1) * tf)
            if load_weights:
                slot = c % 2
                if c + 1 < n_chunks:
                    for cp in chunk_copies(c + 1, 1 - slot):
                        cp.start()
                for cp in chunk_copies(c, slot):
                    cp.wait()
                cache_g[:, cols] = stage_g[slot].astype(BF16)
                cache_u[:, cols] = stage_u[slot].astype(BF16)
                cache_d[cols, :] = stage_d[slot].astype(BF16)
            hid = _swiglu_hidden(x, cache_g[:, cols], cache_u[:, cols])
            part = jnp.dot(hid, cache_d[cols, :], preferred_element_type=F32)
            if c == 0:
                acc[...] = part
            else:
                acc[...] += part
        o_ref[...] = acc[...]

    pl.when(jnp.logical_and(active, new_expert))(lambda: tile_ffn(True))
    pl.when(jnp.logical_and(active, jnp.logical_not(new_expert)))(lambda: tile_ffn(False))


def _moe_experts(xs, plan, wg, wu, wd, *, tm, tf_target=512):
    P, D = xs.shape
    E, _, F = wg.shape
    tf = _pick_chunk(F, tf_target)
    tile = lambda i, te, na: (jnp.minimum(i, na[0] - 1), 0)
    in_hbm = pl.BlockSpec(memory_space=pl.ANY)
    grid_spec = pltpu.PrefetchScalarGridSpec(
        num_scalar_prefetch=2,
        grid=(P // tm,),
        in_specs=[pl.BlockSpec((tm, D), tile), in_hbm, in_hbm, in_hbm],
        out_specs=pl.BlockSpec((tm, D), tile),
        scratch_shapes=[pltpu.VMEM((D, F), BF16), pltpu.VMEM((D, F), BF16), pltpu.VMEM((F, D), BF16),
                        pltpu.VMEM((2, D, tf), F32), pltpu.VMEM((2, D, tf), F32), pltpu.VMEM((2, tf, D), F32),
                        pltpu.SemaphoreType.DMA((3, 2)), pltpu.VMEM((tm, D), F32)],
    )
    return pl.pallas_call(
        functools.partial(_moe_experts_kernel, tf=tf),
        grid_spec=grid_spec,
        out_shape=jax.ShapeDtypeStruct((P, D), F32),
        compiler_params=_params("arbitrary", vmem_limit=MOE_VMEM_LIMIT),
        name="moe_experts",
    )(plan["tile_expert"], plan["n_active_tiles"], xs, wg, wu, wd)


def _moe_combine_kernel(*refs, final, rb, n_experts):
    if final:
        plan_ref, info_ref, ys_ref, x_ref, gfin_ref, o_ref, acc = refs
    else:
        (plan_ref, info_ref, ys_ref, x_ref, o_ref, acc), gfin_ref = refs, None
    w = pl.program_id(0)

    @pl.when(plan_ref[PLAN_FIRST, w] == 1)
    def _():
        acc[...] = jnp.zeros_like(acc)

    @pl.when(plan_ref[PLAN_VALID, w] == 1)
    def _():
        info = info_ref[...]
        ch = info.shape[0]
        e = plan_ref[PLAN_EXPERT, w]
        lane = lax.broadcasted_iota(jnp.int32, info.shape, 1)
        pos = jnp.sum(jnp.where(lane == e, info, 0.0), axis=-1, keepdims=True)
        gate = jnp.sum(jnp.where(lane == e + n_experts, info, 0.0), axis=-1, keepdims=True)
        local = pos - (plan_ref[PLAN_BLOCK, w] * rb).astype(F32)
        cols = lax.broadcasted_iota(jnp.int32, (ch, rb), 1).astype(F32)
        onehot = jnp.where(local == cols, 1.0, 0.0).astype(BF16)
        acc[...] += gate * jnp.dot(onehot, ys_ref[...], preferred_element_type=F32)

    @pl.when(plan_ref[PLAN_LAST, w] == 1)
    def _():
        o_ref[...] = _finish(x_ref[...], acc[...], gfin_ref)


def _moe_combine(ys, x, plan, g_final, *, rb, ch, n_experts):
    N, D = x.shape
    final = g_final is not None
    chunk = lambda c: pl.BlockSpec((ch, c), lambda w, p: (p[PLAN_CHUNK, w], 0))
    in_specs = [chunk(LANES), pl.BlockSpec((rb, D), lambda w, p: (p[PLAN_BLOCK, w], 0)), chunk(D)]
    args = [plan["combine_plan"], plan["token_info"], ys, x]
    if final:
        in_specs.append(pl.BlockSpec((1, D), lambda w, p: (0, 0)))
        args.append(g_final.reshape(1, D))
    grid_spec = pltpu.PrefetchScalarGridSpec(
        num_scalar_prefetch=1,
        grid=(plan["n_items"],),
        in_specs=in_specs,
        out_specs=chunk(D),
        scratch_shapes=[pltpu.VMEM((ch, D), F32)],
    )
    return pl.pallas_call(
        functools.partial(_moe_combine_kernel, final=final, rb=rb, n_experts=n_experts),
        grid_spec=grid_spec,
        out_shape=jax.ShapeDtypeStruct((N, D), F32),
        compiler_params=_params("arbitrary"),
        name="moe_combine",
    )(*args)


def _sc_rows_kernel(n_rows, n_out, d, scatter):
    per_worker = n_rows // SC_WORKERS
    assert n_rows % (SC_WORKERS * SC_WINDOW) == 0
    mesh = plsc.VectorSubcoreMesh(core_axis_name="c", subcore_axis_name="s")
    out_type = (jax.ShapeDtypeStruct((n_out, d), F32) if scatter
                else [jax.ShapeDtypeStruct((n_rows, d), F32)] * 2)

    def body(*refs):
        if scatter:
            src_hbm, hi_hbm, lo_hbm, out_hbm, idx_hi, idx_lo, rows_hi, rows_lo, sem_hi, sem_lo = refs
        else:
            src_hbm, hi_hbm, lo_hbm, out_hi_hbm, out_lo_hbm, idx_hi, idx_lo, rows_hi, rows_lo, sem_hi, sem_lo = refs
        worker = lax.axis_index("s") * SC_CORES + lax.axis_index("c")

        @pl.loop(0, per_worker // SC_WINDOW)
        def _(i):
            window = pl.ds(worker * per_worker + i * SC_WINDOW, SC_WINDOW)
            pltpu.sync_copy(hi_hbm.at[window], idx_hi)
            pltpu.sync_copy(lo_hbm.at[window], idx_lo)
            if scatter:
                pltpu.sync_copy(src_hbm.at[window], rows_hi)
                to_hi = pltpu.async_copy(rows_hi, out_hbm.at[idx_hi], sem_hi)
                to_lo = pltpu.async_copy(rows_hi, out_hbm.at[idx_lo], sem_lo)
                to_hi.wait()
                to_lo.wait()
            else:
                from_hi = pltpu.async_copy(src_hbm.at[idx_hi], rows_hi, sem_hi)
                from_lo = pltpu.async_copy(src_hbm.at[idx_lo], rows_lo, sem_lo)
                from_hi.wait()
                pltpu.sync_copy(rows_hi, out_hi_hbm.at[window])
                from_lo.wait()
                pltpu.sync_copy(rows_lo, out_lo_hbm.at[window])

    index_vec, row_buf = pltpu.VMEM((SC_WINDOW,), jnp.int32), pltpu.VMEM((SC_WINDOW, d), F32)
    return pl.kernel(body, mesh=mesh, out_type=out_type,
                     scratch_types=[index_vec, index_vec, row_buf, row_buf,
                                    pltpu.SemaphoreType.DMA, pltpu.SemaphoreType.DMA])


def _moe_mix_kernel(*refs, final, natural_out):
    if final:
        x_ref, yh_ref, yl_ref, g_ref, gfin_ref, o_ref = refs
    else:
        (x_ref, yh_ref, yl_ref, g_ref, o_ref), gfin_ref = refs, None
    gates = _token_rows(g_ref)
    lane = lax.broadcasted_iota(jnp.int32, gates.shape, 1)
    g_hi = jnp.sum(jnp.where(lane == 0, gates, 0.0), axis=-1, keepdims=True)
    g_lo = jnp.sum(jnp.where(lane == 1, gates, 0.0), axis=-1, keepdims=True)
    out = _finish(_token_rows(x_ref), g_hi * _token_rows(yh_ref) + g_lo * _token_rows(yl_ref), gfin_ref)
    out = out.reshape(x_ref.shape)
    o_ref[...] = pltpu.einshape("rid->ird", out) if natural_out else out


def _moe_mix(x, y_hi, y_lo, pair_gates, g_final, *, natural_out, ti=32):
    B, R, SI, D = x.shape
    ti = min(ti, SI)
    final = g_final is not None
    seq = lambda c: pl.BlockSpec((None, R, ti, c), lambda b, s: (b, 0, s, 0))
    as_tokens = lambda t: t.reshape(B, R, SI, t.shape[-1])
    in_specs, args = [seq(D), seq(D), seq(D), seq(LANES)], [x, as_tokens(y_hi), as_tokens(y_lo), as_tokens(pair_gates)]
    if final:
        in_specs.append(pl.BlockSpec((1, D), lambda b, s: (0, 0)))
        args.append(g_final.reshape(1, D))
    out_spec = pl.BlockSpec((None, ti, R, D), lambda b, s: (b, s, 0, 0)) if natural_out else seq(D)
    return pl.pallas_call(
        functools.partial(_moe_mix_kernel, final=final, natural_out=natural_out),
        grid=(B, SI // ti), in_specs=in_specs, out_specs=out_spec,
        out_shape=jax.ShapeDtypeStruct((B, SI, R, D) if natural_out else (B, R, SI, D), F32),
        compiler_params=_params("parallel", "parallel"),
        name="moe_mix",
    )(*args)


def _moe(h, x, route, wg, wu, wd, g_final, *, natural_out, tm=512, rb=256, ch=512):
    D = x.shape[-1]
    N = h.shape[0]
    E = wg.shape[0]
    tm, ch = min(tm, N), min(ch, N)
    rb = min(rb, tm)
    assert N % ch == 0 and tm % rb == 0
    plan = _plan_routing(route, E, tm=tm, rb=rb, ch=ch)
    rows_hi, rows_lo = plan["rows"]
    xs = _sc_rows_kernel(N, plan["p_max"], D, scatter=True)(h, rows_hi, rows_lo)
    ys = _moe_experts(xs, plan, wg, wu, wd, tm=tm)
    y_hi, y_lo = _sc_rows_kernel(N, plan["p_max"], D, scatter=False)(ys, rows_hi, rows_lo)
    return _moe_mix(x, y_hi, y_lo, plan["pair_gates"], g_final, natural_out=natural_out)


def kernel(x, norm_mix, w_in, w_pool, pool_scale, attn_gain, w_out, norm_ffn, ffn_wg, ffn_wu, ffn_wd,
           w_router, moe_wg, moe_wu, moe_wd, final_norm):
    B, S, D = x.shape
    depth = norm_mix.shape[0]
    bf = lambda t: t.astype(BF16)
    w_in, w_pool, w_out = bf(w_in), bf(w_pool), bf(w_out)
    ffn_wg, ffn_wu, ffn_wd = bf(ffn_wg), bf(ffn_wu), bf(ffn_wd)
    R = RESIDUES
    assert S % R == 0
    N, SI = B * S, S // R
    x = x.reshape(B, SI, R, D)
    for l in range(depth):
        first, last = l == 0, l == depth - 1
        ya, q, k, v, q32, k32, v32 = _mix_in(x, norm_mix[l], w_in[l], w_pool[l], pool_scale[l], natural_x=first)
        narrow = lambda dil: (R // dil) * 16 > WINDOW_STEPS
        branches = [_attn_branch(q32, k32, v32, dil) if narrow(dil) else _attn_branch(q, k, v, dil)
                    for _, dil in DILATED_PATTERNS]
        x1, h2 = _mix_out(x, ya, [o for o, _ in branches], [lse for _, lse in branches],
                          attn_gain[l], w_out[l], norm_ffn[l], natural_x=first,
                          h_dtype=BF16 if l % 2 == 0 else F32)
        g_final = final_norm if last else None
        i = l // 2
        if l % 2 == 0:
            x = _ffn(h2.reshape(N, D), x1.reshape(N, D), ffn_wg[i], ffn_wu[i], ffn_wd[i], g_final)
            x = x.reshape(B, R, SI, D)
            if last:
                x = x.transpose(0, 2, 1, 3)
        else:
            route = _router(x1.reshape(N, D), norm_ffn[l], w_router[i])
            x = _moe(h2.reshape(N, D), x1, route, moe_wg[i], moe_wu[i], moe_wd[i], g_final, natural_out=last)
    return x.reshape(B, S, D)
```

```python
import functools

import jax
import jax.numpy as jnp
import numpy as np
from jax import lax
from jax.experimental import pallas as pl
from jax.experimental.pallas import tpu as pltpu
from jax.experimental.pallas import tpu_sc as plsc

F32 = jnp.float32
BF16 = jnp.bfloat16
U32 = jnp.uint32
HIGH_HALF = np.uint32(0xFFFF0000)

EPS = 1e-6
LANES = 128
HEAD_DIM = 64
POOL_WINDOWS = (2, 4, 8, 16)
POOL_HIST = 8
DILATED_PATTERNS = ((128, 1), (512, 4), (2048, 16))
WINDOW_STEPS = 128
RESIDUES = 16
ATTN_STEP_ROWS = 1024
ATTN_UNROLL_QUERIES = 512
LOG2_E = 1.4426950408889634
SC_CORES, SC_SUBCORES = 2, 16
SC_WORKERS = SC_CORES * SC_SUBCORES
SC_WINDOW = 64
assert all(w // d == WINDOW_STEPS and RESIDUES % d == 0 for w, d in DILATED_PATTERNS)
TOP_K = 2
MASKED = -1e30
VMEM_LIMIT = 48 * 1024 * 1024
MOE_VMEM_LIMIT = 56 * 1024 * 1024


def _rms(x, g):
    return x * lax.rsqrt(jnp.mean(x * x, axis=-1, keepdims=True) + EPS) * g


def _params(*sem, vmem_limit=VMEM_LIMIT):
    return pltpu.CompilerParams(dimension_semantics=sem, vmem_limit_bytes=vmem_limit)


def _token_rows(ref, natural=False):
    t = ref[...]
    if natural:
        t = pltpu.einshape("ird->rid", t)
    return t.reshape(-1, t.shape[-1])


def _pack_halves(t):
    bits = lax.bitcast_convert_type(t, U32)
    bits = (bits + np.uint32(0x7FFF) + ((bits >> 16) & np.uint32(1))) & HIGH_HALF
    half = t.shape[-1] // 2
    return (bits[:, :half] >> 16) | bits[:, half:]


def _unpack_halves(words):
    halves = [lax.bitcast_convert_type(w, F32) for w in (words << 16, words & HIGH_HALF)]
    return jnp.concatenate(halves, axis=-1)


def _mix_in_kernel(x_ref, g_ref, w_ref, wp_ref, ps_ref, ya_ref, q_ref, k_ref, v_ref, q32_ref, k32_ref, v32_ref,
                   ubuf, uprev, *, d_pool, d_attn, ti, natural_x):
    s = pl.program_id(1)
    R = RESIDUES
    blk = lambda t: t.reshape(R, ti, t.shape[-1])
    h = _rms(_token_rows(x_ref, natural_x), g_ref[...]).astype(BF16)
    proj = jnp.dot(h, w_ref[...], preferred_element_type=F32)
    scale = HEAD_DIM ** -0.5 * LOG2_E
    qkv = (proj[:, d_pool:d_pool + d_attn] * scale, proj[:, d_pool + d_attn:d_pool + 2 * d_attn],
           proj[:, d_pool + 2 * d_attn:])
    for t, ref, ref32 in zip(qkv, (q_ref, k_ref, v_ref), (q32_ref, k32_ref, v32_ref)):
        ref[...] = blk(t.astype(BF16))
        ref32[...] = blk(t)

    @pl.when(s == 0)
    def _():
        ubuf[:, 0:POOL_HIST, :] = jnp.zeros((R, POOL_HIST, d_pool), F32)

    ubuf[:, POOL_HIST:POOL_HIST + ti, :] = blk(proj[:, :d_pool])
    uprev[...] = ubuf[:, POOL_HIST - 1:POOL_HIST - 1 + ti, :]
    at_start = (s * ti + lax.broadcasted_iota(jnp.int32, (ti, 1), 0)) == 0
    group = d_pool // len(POOL_WINDOWS)
    zs = []
    for gi, w in enumerate(POOL_WINDOWS):
        cols = slice(gi * group, (gi + 1) * group)
        ds = []
        for r in range(R):
            ug = ubuf[r, POOL_HIST:POOL_HIST + ti, cols]
            win = ug
            for back in range(1, w):
                rr = r - back
                win = win + (ubuf[rr, POOL_HIST:POOL_HIST + ti, cols] if rr >= 0 else uprev[rr + R, :, cols])
            cnt = jnp.where(at_start, float(min(r + 1, w)), float(w))
            ds.append(win / cnt - ug)
        d = jnp.concatenate(ds, axis=0).astype(BF16)
        zs.append(jnp.dot(d, wp_ref[gi], preferred_element_type=F32))
    z = jnp.concatenate(zs, axis=-1)
    ya_ref[...] = blk(_rms(z, ps_ref[...]).astype(BF16))
    ubuf[:, POOL_HIST - 1:POOL_HIST, :] = ubuf[:, POOL_HIST + ti - 1:POOL_HIST + ti, :]


def _mix_in(x, g, w_in, w_pool, pool_scale, *, natural_x, ti=32):
    B, R, SI, D = x.shape
    if natural_x:
        R, SI = SI, R
    d_pool = pool_scale.shape[-1]
    d_in = w_in.shape[-1]
    d_attn = (d_in - d_pool) // 3
    ti = min(ti, SI)
    assert R == RESIDUES >= max(POOL_WINDOWS) and SI % ti == 0 and ti % 16 == 0
    assert d_pool % (LANES * len(POOL_WINDOWS)) == 0
    seq_spec = lambda c: pl.BlockSpec((None, R, ti, c), lambda b, s: (b, 0, s, 0))
    full = lambda shape: pl.BlockSpec(shape, lambda b, s: (0,) * len(shape))
    out_sds = lambda c, dtype=BF16: jax.ShapeDtypeStruct((B, R, SI, c), dtype)
    return pl.pallas_call(
        functools.partial(_mix_in_kernel, d_pool=d_pool, d_attn=d_attn, ti=ti, natural_x=natural_x),
        grid=(B, SI // ti),
        in_specs=[pl.BlockSpec((None, ti, R, D), lambda b, s: (b, s, 0, 0)) if natural_x else seq_spec(D),
                  full((1, D)), full((D, d_in)), full(w_pool.shape), full((1, d_pool))],
        out_specs=[seq_spec(d_pool)] + [seq_spec(d_attn)] * 6,
        out_shape=[out_sds(d_pool)] + [out_sds(d_attn)] * 3 + [out_sds(d_attn, F32)] * 3,
        scratch_shapes=[pltpu.VMEM((R, POOL_HIST + ti, d_pool), F32), pltpu.VMEM((R, ti, d_pool), F32)],
        compiler_params=_params("parallel", "arbitrary"),
        name="mix_in",
    )(x, g.reshape(1, D), w_in, w_pool, pool_scale.reshape(1, d_pool))


def _attn_kernel(q_ref, k_ref, v_ref, o_ref, lse_ref, bias_ref, *, nq, strips, d_attn):
    _, classes, chunk, _ = q_ref.shape
    L = nq // strips
    nk, n_blocks = 2 * nq, chunk // L
    first_block = pl.program_id(2) * n_blocks
    row = lax.broadcasted_iota(jnp.int32, (nq, 1), 0)
    col = lax.broadcasted_iota(jnp.int32, (1, nk), 1)
    q_strip, q_row = row >> (L.bit_length() - 1), row & (L - 1)
    k_strip, k_row = col >> ((2 * L).bit_length() - 1), col & (2 * L - 1)
    back = strips * (q_row - k_row) + (q_strip - k_strip)

    def band(offset):
        rel = back + strips * offset
        return jnp.where((rel >= 0) & (rel <= WINDOW_STEPS), 0.0, MASKED).astype(F32)

    bias_ref[0] = band(0)
    bias_ref[1] = band(L)
    lane = lax.broadcasted_iota(jnp.int32, (nq, LANES), 1)
    lo_half = lane < HEAD_DIM

    def block(it, carry):
        g, j_local = it // n_blocks, it % n_blocks
        j = first_block + j_local
        q_rows = pl.ds(pl.multiple_of(j_local * L, L), L)
        k_rows = pl.ds(pl.multiple_of(jnp.maximum(j - 1, 0) * L, L), 2 * L)
        bias = bias_ref[jnp.minimum(j, 1)]
        lse_tile = jnp.zeros((nq, LANES), F32)
        for hp in range(d_attn // LANES):
            cols = slice(hp * LANES, (hp + 1) * LANES)
            qp = q_ref[:, g, q_rows, cols].reshape(nq, LANES).astype(BF16)
            kk = k_ref[:, g, k_rows, cols].reshape(nk, LANES).astype(BF16)
            vv = v_ref[:, g, k_rows, cols].reshape(nk, LANES).astype(BF16)
            outs, lses = [], []
            for sub in range(2):
                keep = lo_half if sub == 0 else jnp.logical_not(lo_half)
                qm = jnp.where(keep, qp, jnp.zeros_like(qp))
                s = lax.dot_general(qm, kk, (((1,), (1,)), ((), ())), preferred_element_type=F32) + bias
                m = jnp.max(s, axis=-1, keepdims=True)
                p = jnp.exp2(s - m)
                l = jnp.sum(p, axis=-1, keepdims=True)
                o = jnp.dot(p.astype(BF16), vv, preferred_element_type=F32)
                outs.append(o / l)
                lses.append(m + jnp.log2(l))
            o_pair = jnp.where(lo_half, outs[0], outs[1]).astype(o_ref.dtype)
            o_ref[:, g, q_rows, cols] = o_pair.reshape(strips, L, LANES)
            pair_lse = jnp.where(lane == 2 * hp, lses[0], lses[1])
            lse_tile = jnp.where((lane >> 1) == hp, pair_lse, lse_tile)
        lse_ref[:, g, q_rows, :] = lse_tile.reshape(strips, L, LANES)
        return carry

    lax.fori_loop(0, classes * n_blocks, block, 0, unroll=ATTN_UNROLL_QUERIES // nq)


def _attn_branch(q, k, v, dil):
    B, R, SI, C = q.shape
    strips = R // dil
    row_tile = 8 * 4 // q.dtype.itemsize
    nq = max(WINDOW_STEPS, strips * row_tile)
    L = nq // strips
    classes = max(1, min(dil, ATTN_STEP_ROWS // (strips * SI)))
    chunk = min(SI, max(L, ATTN_STEP_ROWS // (strips * classes)))
    assert R % dil == 0 and SI % chunk == 0 and chunk % L == 0 and SI >= 2 * L and dil % classes == 0
    assert C % LANES == 0 and C // HEAD_DIM <= LANES and nq & (nq - 1) == 0 and L & (L - 1) == 0
    view = lambda t: t.reshape(B, strips, dil, SI, t.shape[-1])
    q_spec = lambda c: pl.BlockSpec((None, strips, classes, chunk, c), lambda b, g, s: (b, 0, g, s, 0))
    kv_spec = pl.BlockSpec((None, strips, classes, SI, C), lambda b, g, s: (b, 0, g, 0, 0),
                           pipeline_mode=pl.Buffered(1 if SI > chunk else 2))
    o, lse = pl.pallas_call(
        functools.partial(_attn_kernel, nq=nq, strips=strips, d_attn=C),
        scratch_shapes=[pltpu.VMEM((2, nq, 2 * nq), F32)],
        grid=(B, dil // classes, SI // chunk),
        in_specs=[q_spec(C), kv_spec, kv_spec],
        out_specs=[q_spec(C), q_spec(LANES)],
        out_shape=[jax.ShapeDtypeStruct((B, strips, dil, SI, C), q.dtype),
                   jax.ShapeDtypeStruct((B, strips, dil, SI, LANES), F32)],
        compiler_params=_params("parallel", "parallel", "arbitrary"),
        name=f"attn_d{dil}",
    )(view(q), view(k), view(v))
    return o.reshape(B, R, SI, C), lse.reshape(B, R, SI, LANES)


def _mix_out_kernel(x_ref, ya_ref, o1_ref, o2_ref, o3_ref, l1_ref, l2_ref, l3_ref, gain_ref, wo_ref,
                    gffn_ref, exp_ref, x1_ref, h2_ref, *, d_pool, natural_x):
    lses = [_token_rows(l) for l in (l1_ref, l2_ref, l3_ref)]
    top = jnp.maximum(jnp.maximum(lses[0], lses[1]), lses[2])
    es = [jnp.exp2(l - top) for l in lses]
    den = es[0] + es[1] + es[2]
    expand = exp_ref[...]

    def per_lane(w):
        hi = w.astype(BF16)
        lo = (w - hi.astype(F32)).astype(BF16)
        return (jnp.dot(hi, expand, preferred_element_type=F32)
                + jnp.dot(lo, expand, preferred_element_type=F32))

    o = 0.0
    for e, o_ref in zip(es, (o1_ref, o2_ref, o3_ref)):
        o = o + per_lane(e / den) * _token_rows(o_ref).astype(F32)
    yb = _rms(o, gain_ref[...]).astype(BF16)
    y = (jnp.dot(_token_rows(ya_ref), wo_ref[:d_pool, :], preferred_element_type=F32)
         + jnp.dot(yb, wo_ref[d_pool:, :], preferred_element_type=F32))
    x1 = _token_rows(x_ref, natural_x) + y
    x1_ref[...] = x1.reshape(x1_ref.shape)
    h2 = _rms(x1, gffn_ref[...])
    h2 = _pack_halves(h2) if h2_ref.dtype == U32 else h2.astype(h2_ref.dtype)
    h2_ref[...] = h2.reshape(h2_ref.shape)


def _mix_out(x, ya, outs, lses, attn_gain, w_out, g_ffn, *, pack_h, natural_x, ti=32):
    B, R, SI, d_pool = ya.shape
    D = x.shape[-1]
    d_attn = outs[0].shape[-1]
    ti = min(ti, SI)
    head_of_lane = jnp.arange(d_attn) // HEAD_DIM
    expand = (jnp.arange(LANES)[:, None] == head_of_lane[None, :]).astype(BF16)
    seq = lambda c: pl.BlockSpec((None, R, ti, c), lambda b, s: (b, 0, s, 0))
    x_spec = pl.BlockSpec((None, ti, R, D), lambda b, s: (b, s, 0, 0)) if natural_x else seq(D)
    full = lambda shape: pl.BlockSpec(shape, lambda b, s: (0,) * len(shape))
    h_sds = jax.ShapeDtypeStruct((B, R, SI, D // 2), U32) if pack_h else jax.ShapeDtypeStruct((B, R, SI, D), BF16)
    return pl.pallas_call(
        functools.partial(_mix_out_kernel, d_pool=d_pool, natural_x=natural_x),
        grid=(B, SI // ti),
        in_specs=[x_spec, seq(d_pool)] + [seq(d_attn)] * 3 + [seq(LANES)] * 3
                 + [full((1, d_attn)), full(w_out.shape), full((1, D)), full(expand.shape)],
        out_specs=[seq(D), seq(h_sds.shape[-1])],
        out_shape=[jax.ShapeDtypeStruct((B, R, SI, D), F32), h_sds],
        compiler_params=_params("parallel", "parallel"),
        name="mix_out",
    )(x, ya, *outs, *lses, attn_gain.reshape(1, d_attn), w_out, g_ffn.reshape(1, D), expand)


def _swiglu_hidden(h, wg, wu):
    a = jnp.dot(h, wg, preferred_element_type=F32)
    b = jnp.dot(h, wu, preferred_element_type=F32)
    return (a * jax.nn.sigmoid(a) * b).astype(BF16)


def _finish(x, y, gfin_ref):
    out = x + y
    return out if gfin_ref is None else _rms(out, gfin_ref[...])


def _ffn_kernel(*refs, final, tf):
    if final:
        h_ref, x_ref, wg_ref, wu_ref, wd_ref, gfin_ref, o_ref, acc = refs
    else:
        (h_ref, x_ref, wg_ref, wu_ref, wd_ref, o_ref, acc), gfin_ref = refs, None
    h = h_ref[...]
    for c in range(wg_ref.shape[-1] // tf):
        cols = slice(c * tf, (c + 1) * tf)
        hid = _swiglu_hidden(h, wg_ref[:, cols], wu_ref[:, cols])
        part = jnp.dot(hid, wd_ref[cols, :], preferred_element_type=F32)
        if c == 0:
            acc[...] = part
        else:
            acc[...] += part
    o_ref[...] = _finish(x_ref[...], acc[...], gfin_ref)


def _pick_chunk(n, target):
    best = None
    for c in range(LANES, min(n, target) + 1, LANES):
        if n % c == 0:
            best = c
    assert best is not None
    return best


def _ffn(h, x, wg, wu, wd, g_final, *, tm=512, tf_target=1536):
    N, D = x.shape
    F = wg.shape[-1]
    tm = min(tm, N)
    tf = _pick_chunk(F, tf_target)
    final = g_final is not None
    row = pl.BlockSpec((tm, D), lambda i: (i, 0))
    resident = lambda shape: pl.BlockSpec(shape, lambda i: (0, 0), pipeline_mode=pl.Buffered(1))
    in_specs = [row, row, resident((D, F)), resident((D, F)), resident((F, D))]
    args = [h, x, wg, wu, wd]
    if final:
        in_specs.append(pl.BlockSpec((1, D), lambda i: (0, 0)))
        args.append(g_final.reshape(1, D))
    return pl.pallas_call(
        functools.partial(_ffn_kernel, final=final, tf=tf),
        grid=(N // tm,),
        in_specs=in_specs,
        out_specs=row,
        out_shape=jax.ShapeDtypeStruct((N, D), F32),
        scratch_shapes=[pltpu.VMEM((tm, D), F32)],
        compiler_params=_params("parallel"),
        name="ffn_dense",
    )(*args)


def _router_kernel(x_ref, g_ref, wr_ref, gate_ref, *, n_experts):
    h = _rms(x_ref[...], g_ref[...])
    logits = jnp.dot(h, wr_ref[...], preferred_element_type=F32, precision=lax.Precision.HIGHEST)
    lane = lax.broadcasted_iota(jnp.int32, logits.shape, 1)
    logits = jnp.where(lane < n_experts, logits, -jnp.inf)
    picks = []
    for _ in range(TOP_K):
        m = jnp.max(logits, axis=-1, keepdims=True)
        idx = jnp.min(jnp.where(logits == m, lane, LANES), axis=-1, keepdims=True)
        picks.append((m, idx))
        logits = jnp.where(lane == idx, -jnp.inf, logits)
    (m1, i1), (m2, i2) = picks
    e2 = jnp.exp(m2 - m1)
    g1 = 1.0 / (1.0 + e2)
    g2 = e2 / (1.0 + e2)
    picked = jnp.logical_or(lane == i1 + n_experts, lane == i2 + n_experts)
    gate_ref[...] = jnp.where(lane == i1, g1, jnp.where(lane == i2, g2, jnp.where(picked, 1.0, 0.0)))


def _router(x, g_ffn, w_router, *, tm=512):
    N, D = x.shape
    E = w_router.shape[-1]
    assert 2 * E <= LANES and TOP_K == 2
    tm = min(tm, N)
    wr = jnp.zeros((D, LANES), F32).at[:, :E].set(w_router)
    return pl.pallas_call(
        functools.partial(_router_kernel, n_experts=E),
        grid=(N // tm,),
        in_specs=[pl.BlockSpec((tm, D), lambda i: (i, 0)),
                  pl.BlockSpec((1, D), lambda i: (0, 0)),
                  pl.BlockSpec((D, LANES), lambda i: (0, 0))],
        out_specs=pl.BlockSpec((tm, LANES), lambda i: (i, 0)),
        out_shape=jax.ShapeDtypeStruct((N, LANES), F32),
        compiler_params=_params("parallel"),
        name="router",
    )(x, g_ffn.reshape(1, D), wr)


def _plan_rows(route, n_experts, *, tm):
    N = route.shape[0]
    E = n_experts
    i32 = jnp.int32
    p_max = TOP_K * N + E * tm
    gate = route[:, :E].T
    sel = (route[:, E:2 * E].T > 0.5).astype(i32)
    rank = jnp.cumsum(sel, axis=1) - sel
    seg_len = (jnp.sum(sel, axis=1) + tm - 1) // tm * tm
    seg_end = jnp.cumsum(seg_len)
    pos = jnp.where(sel > 0, (seg_end - seg_len)[:, None] + rank, -1)
    tile_start = jnp.arange(p_max // tm, dtype=i32) * tm
    tile_expert = jnp.minimum(jnp.sum(seg_end[None, :] <= tile_start[:, None], axis=1), E - 1).astype(i32)
    row_hi = jnp.max(pos, axis=0)
    row_lo = jnp.sum(jnp.where(sel > 0, pos, 0), axis=0) - row_hi
    gate_hi = jnp.sum(jnp.where(pos == row_hi[None, :], gate, 0.0), axis=0)
    gate_lo = jnp.sum(jnp.where(pos == row_lo[None, :], gate, 0.0), axis=0)
    pair_gates = jnp.concatenate([gate_hi[:, None], gate_lo[:, None], jnp.zeros((N, LANES - 2), F32)], axis=1)
    return dict(rows=(row_hi.astype(i32), row_lo.astype(i32)), pair_gates=pair_gates, tile_expert=tile_expert,
                n_active_tiles=(seg_end[-1] // tm).reshape(1).astype(i32), p_max=p_max)


def _sc_rows_kernel(n_rows, n_out, d, scatter):
    per_worker = n_rows // SC_WORKERS
    assert n_rows % (SC_WORKERS * SC_WINDOW) == 0
    mesh = plsc.VectorSubcoreMesh(core_axis_name="c", subcore_axis_name="s")
    out_type = (jax.ShapeDtypeStruct((n_out, d), U32) if scatter
                else [jax.ShapeDtypeStruct((n_rows, d), U32)] * 2)

    def body(*refs):
        if scatter:
            src_hbm, hi_hbm, lo_hbm, out_hbm, idx_hi, idx_lo, rows_hi, rows_lo, sem_hi, sem_lo = refs
        else:
            src_hbm, hi_hbm, lo_hbm, out_hi_hbm, out_lo_hbm, idx_hi, idx_lo, rows_hi, rows_lo, sem_hi, sem_lo = refs
        worker = lax.axis_index("s") * SC_CORES + lax.axis_index("c")

        @pl.loop(0, per_worker // SC_WINDOW)
        def _(i):
            window = pl.ds(worker * per_worker + i * SC_WINDOW, SC_WINDOW)
            pltpu.sync_copy(hi_hbm.at[window], idx_hi)
            pltpu.sync_copy(lo_hbm.at[window], idx_lo)
            if scatter:
                pltpu.sync_copy(src_hbm.at[window], rows_hi)
                to_hi = pltpu.async_copy(rows_hi, out_hbm.at[idx_hi], sem_hi)
                to_lo = pltpu.async_copy(rows_hi, out_hbm.at[idx_lo], sem_lo)
                to_hi.wait()
                to_lo.wait()
            else:
                from_hi = pltpu.async_copy(src_hbm.at[idx_hi], rows_hi, sem_hi)
                from_lo = pltpu.async_copy(src_hbm.at[idx_lo], rows_lo, sem_lo)
                from_hi.wait()
                pltpu.sync_copy(rows_hi, out_hi_hbm.at[window])
                from_lo.wait()
                pltpu.sync_copy(rows_lo, out_lo_hbm.at[window])

    index_vec, row_buf = pltpu.VMEM((SC_WINDOW,), jnp.int32), pltpu.VMEM((SC_WINDOW, d), U32)
    return pl.kernel(body, mesh=mesh, out_type=out_type,
                     scratch_types=[index_vec, index_vec, row_buf, row_buf,
                                    pltpu.SemaphoreType.DMA, pltpu.SemaphoreType.DMA])


def _moe_experts_kernel(te_ref, na_ref, xs_ref, wg_hbm, wu_hbm, wd_hbm, o_ref,
                        cache_g, cache_u, cache_d, stage_g, stage_u, stage_d, sems, acc, *, tf):
    i = pl.program_id(0)
    e = te_ref[i]
    n_chunks = cache_g.shape[-1] // tf
    active = i < na_ref[0]
    new_expert = jnp.logical_or(i == 0, e != te_ref[jnp.maximum(i - 1, 0)])

    def chunk_copies(c, slot):
        cols = pl.ds(c * tf, tf)
        return (pltpu.make_async_copy(wg_hbm.at[e, :, cols], stage_g.at[slot], sems.at[0, slot]),
                pltpu.make_async_copy(wu_hbm.at[e, :, cols], stage_u.at[slot], sems.at[1, slot]),
                pltpu.make_async_copy(wd_hbm.at[e, cols, :], stage_d.at[slot], sems.at[2, slot]))

    def tile_ffn(load_weights):
        x = _unpack_halves(xs_ref[...]).astype(BF16)
        if load_weights:
            for cp in chunk_copies(0, 0):
                cp.start()
        for c in range(n_chunks):
            cols = slice(c * tf, (c + 1) * tf)
            if load_weights:
                slot = c % 2
                if c + 1 < n_chunks:
                    for cp in chunk_copies(c + 1, 1 - slot):
                        cp.start()
                for cp in chunk_copies(c, slot):
                    cp.wait()
                cache_g[:, cols] = stage_g[slot].astype(BF16)
                cache_u[:, cols] = stage_u[slot].astype(BF16)
                cache_d[cols, :] = stage_d[slot].astype(BF16)
            hid = _swiglu_hidden(x, cache_g[:, cols], cache_u[:, cols])
            part = jnp.dot(hid, cache_d[cols, :], preferred_element_type=F32)
            if c == 0:
                acc[...] = part
            else:
                acc[...] += part
        o_ref[...] = _pack_halves(acc[...])

    pl.when(jnp.logical_and(active, new_expert))(lambda: tile_ffn(True))
    pl.when(jnp.logical_and(active, jnp.logical_not(new_expert)))(lambda: tile_ffn(False))


def _moe_experts(xs, plan, wg, wu, wd, *, tm, tf_target=512):
    P = xs.shape[0]
    E, D, F = wg.shape
    tf = _pick_chunk(F, tf_target)
    tile = lambda i, te, na: (jnp.minimum(i, na[0] - 1), 0)
    in_hbm = pl.BlockSpec(memory_space=pl.ANY)
    grid_spec = pltpu.PrefetchScalarGridSpec(
        num_scalar_prefetch=2,
        grid=(P // tm,),
        in_specs=[pl.BlockSpec((tm, D // 2), tile), in_hbm, in_hbm, in_hbm],
        out_specs=pl.BlockSpec((tm, D // 2), tile),
        scratch_shapes=[pltpu.VMEM((D, F), BF16), pltpu.VMEM((D, F), BF16), pltpu.VMEM((F, D), BF16),
                        pltpu.VMEM((2, D, tf), F32), pltpu.VMEM((2, D, tf), F32), pltpu.VMEM((2, tf, D), F32),
                        pltpu.SemaphoreType.DMA((3, 2)), pltpu.VMEM((tm, D), F32)],
    )
    return pl.pallas_call(
        functools.partial(_moe_experts_kernel, tf=tf),
        grid_spec=grid_spec,
        out_shape=jax.ShapeDtypeStruct((P, D // 2), U32),
        compiler_params=_params("arbitrary", vmem_limit=MOE_VMEM_LIMIT),
        name="moe_experts",
    )(plan["tile_expert"], plan["n_active_tiles"], xs, wg, wu, wd)


def _moe_mix_kernel(*refs, final, natural_out):
    if final:
        x_ref, yh_ref, yl_ref, g_ref, gfin_ref, o_ref = refs
    else:
        (x_ref, yh_ref, yl_ref, g_ref, o_ref), gfin_ref = refs, None
    gates = _token_rows(g_ref)
    lane = lax.broadcasted_iota(jnp.int32, gates.shape, 1)
    g_hi = jnp.sum(jnp.where(lane == 0, gates, 0.0), axis=-1, keepdims=True)
    g_lo = jnp.sum(jnp.where(lane == 1, gates, 0.0), axis=-1, keepdims=True)
    y = g_hi * _unpack_halves(_token_rows(yh_ref)) + g_lo * _unpack_halves(_token_rows(yl_ref))
    out = _finish(_token_rows(x_ref), y, gfin_ref).reshape(x_ref.shape)
    o_ref[...] = pltpu.einshape("rid->ird", out) if natural_out else out


def _moe_mix(x, y_hi, y_lo, pair_gates, g_final, *, natural_out, ti=32):
    B, R, SI, D = x.shape
    ti = min(ti, SI)
    final = g_final is not None
    seq = lambda c: pl.BlockSpec((None, R, ti, c), lambda b, s: (b, 0, s, 0))
    as_tokens = lambda t: t.reshape(B, R, SI, t.shape[-1])
    in_specs = [seq(D), seq(D // 2), seq(D // 2), seq(LANES)]
    args = [x, as_tokens(y_hi), as_tokens(y_lo), as_tokens(pair_gates)]
    if final:
        in_specs.append(pl.BlockSpec((1, D), lambda b, s: (0, 0)))
        args.append(g_final.reshape(1, D))
    out_spec = pl.BlockSpec((None, ti, R, D), lambda b, s: (b, s, 0, 0)) if natural_out else seq(D)
    return pl.pallas_call(
        functools.partial(_moe_mix_kernel, final=final, natural_out=natural_out),
        grid=(B, SI // ti), in_specs=in_specs, out_specs=out_spec,
        out_shape=jax.ShapeDtypeStruct((B, SI, R, D) if natural_out else (B, R, SI, D), F32),
        compiler_params=_params("parallel", "parallel"),
        name="moe_mix",
    )(*args)


def _moe(h, x, route, wg, wu, wd, g_final, *, natural_out, tm=512):
    N, half = h.shape
    plan = _plan_rows(route, wg.shape[0], tm=tm)
    rows_hi, rows_lo = plan["rows"]
    xs = _sc_rows_kernel(N, plan["p_max"], half, scatter=True)(h, rows_hi, rows_lo)
    ys = _moe_experts(xs, plan, wg, wu, wd, tm=tm)
    y_hi, y_lo = _sc_rows_kernel(N, plan["p_max"], half, scatter=False)(ys, rows_hi, rows_lo)
    return _moe_mix(x, y_hi, y_lo, plan["pair_gates"], g_final, natural_out=natural_out)


def kernel(x, norm_mix, w_in, w_pool, pool_scale, attn_gain, w_out, norm_ffn, ffn_wg, ffn_wu, ffn_wd,
           w_router, moe_wg, moe_wu, moe_wd, final_norm):
    B, S, D = x.shape
    depth = norm_mix.shape[0]
    bf = lambda t: t.astype(BF16)
    w_in, w_pool, w_out = bf(w_in), bf(w_pool), bf(w_out)
    ffn_wg, ffn_wu, ffn_wd = bf(ffn_wg), bf(ffn_wu), bf(ffn_wd)
    R = RESIDUES
    assert S % R == 0
    N, SI = B * S, S // R
    x = x.reshape(B, SI, R, D)
    for l in range(depth):
        first, last = l == 0, l == depth - 1
        ya, q, k, v, q32, k32, v32 = _mix_in(x, norm_mix[l], w_in[l], w_pool[l], pool_scale[l], natural_x=first)
        narrow = lambda dil: (R // dil) * 16 > WINDOW_STEPS
        branches = [_attn_branch(q32, k32, v32, dil) if narrow(dil) else _attn_branch(q, k, v, dil)
                    for _, dil in DILATED_PATTERNS]
        x1, h2 = _mix_out(x, ya, [o for o, _ in branches], [lse for _, lse in branches],
                          attn_gain[l], w_out[l], norm_ffn[l], natural_x=first,
                          pack_h=l % 2 == 1)
        g_final = final_norm if last else None
        i = l // 2
        if l % 2 == 0:
            x = _ffn(h2.reshape(N, D), x1.reshape(N, D), ffn_wg[i], ffn_wu[i], ffn_wd[i], g_final)
            x = x.reshape(B, R, SI, D)
            if last:
                x = x.transpose(0, 2, 1, 3)
        else:
            route = _router(x1.reshape(N, D), norm_ffn[l], w_router[i])
            x = _moe(h2.reshape(N, D // 2), x1, route, moe_wg[i], moe_wu[i], moe_wd[i], g_final, natural_out=last)
    return x.reshape(B, S, D)
```

```python
import functools

import jax
import jax.numpy as jnp
import numpy as np
from jax import lax
from jax.experimental import pallas as pl
from jax.experimental.pallas import tpu as pltpu
from jax.experimental.pallas import tpu_sc as plsc

F32 = jnp.float32
BF16 = jnp.bfloat16
U32 = jnp.uint32
HIGH_HALF = np.uint32(0xFFFF0000)

EPS = 1e-6
LANES = 128
HEAD_DIM = 64
POOL_WINDOWS = (2, 4, 8, 16)
POOL_HIST = 8
DILATED_PATTERNS = ((128, 1), (512, 4), (2048, 16))
WINDOW_STEPS = 128
RESIDUES = 16
ATTN_STEP_ROWS = 1024
ATTN_UNROLL_QUERIES = 512
LOG2_E = 1.4426950408889634
SC_CORES, SC_SUBCORES = 2, 16
SC_WORKERS = SC_CORES * SC_SUBCORES
SC_WINDOW = 64
assert all(w // d == WINDOW_STEPS and RESIDUES % d == 0 for w, d in DILATED_PATTERNS)
TOP_K = 2
MASKED = -1e30
VMEM_LIMIT = 48 * 1024 * 1024
MOE_VMEM_LIMIT = 56 * 1024 * 1024


def _rms(x, g):
    return x * lax.rsqrt(jnp.mean(x * x, axis=-1, keepdims=True) + EPS) * g


def _params(*sem, vmem_limit=VMEM_LIMIT):
    return pltpu.CompilerParams(dimension_semantics=sem, vmem_limit_bytes=vmem_limit)


def _token_rows(ref, natural=False):
    t = ref[...]
    if natural:
        t = pltpu.einshape("ird->rid", t)
    return t.reshape(-1, t.shape[-1])


def _pack_halves(t):
    bits = lax.bitcast_convert_type(t, U32)
    bits = (bits + np.uint32(0x7FFF) + ((bits >> 16) & np.uint32(1))) & HIGH_HALF
    half = t.shape[-1] // 2
    return (bits[:, :half] >> 16) | bits[:, half:]


def _unpack_halves(words):
    halves = [lax.bitcast_convert_type(w, F32) for w in (words << 16, words & HIGH_HALF)]
    return jnp.concatenate(halves, axis=-1)


def _mix_in_kernel(x_ref, g_ref, w_ref, wp_ref, ps_ref, ya_ref, q_ref, k_ref, v_ref, q32_ref, k32_ref, v32_ref,
                   ubuf, uprev, *, d_pool, d_attn, ti, natural_x):
    s = pl.program_id(1)
    R = RESIDUES
    blk = lambda t: t.reshape(R, ti, t.shape[-1])
    h = _rms(_token_rows(x_ref, natural_x), g_ref[...]).astype(BF16)
    proj = jnp.dot(h, w_ref[...], preferred_element_type=F32)
    scale = HEAD_DIM ** -0.5 * LOG2_E
    qkv = (proj[:, d_pool:d_pool + d_attn] * scale, proj[:, d_pool + d_attn:d_pool + 2 * d_attn],
           proj[:, d_pool + 2 * d_attn:])
    for t, ref, ref32 in zip(qkv, (q_ref, k_ref, v_ref), (q32_ref, k32_ref, v32_ref)):
        ref[...] = blk(t.astype(BF16))
        ref32[...] = blk(t)

    @pl.when(s == 0)
    def _():
        ubuf[:, 0:POOL_HIST, :] = jnp.zeros((R, POOL_HIST, d_pool), F32)

    ubuf[:, POOL_HIST:POOL_HIST + ti, :] = blk(proj[:, :d_pool])
    uprev[...] = ubuf[:, POOL_HIST - 1:POOL_HIST - 1 + ti, :]
    at_start = (s * ti + lax.broadcasted_iota(jnp.int32, (ti, 1), 0)) == 0
    group = d_pool // len(POOL_WINDOWS)
    zs = []
    for gi, w in enumerate(POOL_WINDOWS):
        cols = slice(gi * group, (gi + 1) * group)
        ds = []
        for r in range(R):
            ug = ubuf[r, POOL_HIST:POOL_HIST + ti, cols]
            win = ug
            for back in range(1, w):
                rr = r - back
                win = win + (ubuf[rr, POOL_HIST:POOL_HIST + ti, cols] if rr >= 0 else uprev[rr + R, :, cols])
            cnt = jnp.where(at_start, float(min(r + 1, w)), float(w))
            ds.append(win / cnt - ug)
        d = jnp.concatenate(ds, axis=0).astype(BF16)
        zs.append(jnp.dot(d, wp_ref[gi], preferred_element_type=F32))
    z = jnp.concatenate(zs, axis=-1)
    ya_ref[...] = blk(_rms(z, ps_ref[...]).astype(BF16))
    ubuf[:, POOL_HIST - 1:POOL_HIST, :] = ubuf[:, POOL_HIST + ti - 1:POOL_HIST + ti, :]


def _mix_in(x, g, w_in, w_pool, pool_scale, *, natural_x, ti=32):
    B, R, SI, D = x.shape
    if natural_x:
        R, SI = SI, R
    d_pool = pool_scale.shape[-1]
    d_in = w_in.shape[-1]
    d_attn = (d_in - d_pool) // 3
    ti = min(ti, SI)
    assert R == RESIDUES >= max(POOL_WINDOWS) and SI % ti == 0 and ti % 16 == 0
    assert d_pool % (LANES * len(POOL_WINDOWS)) == 0
    seq_spec = lambda c: pl.BlockSpec((None, R, ti, c), lambda b, s: (b, 0, s, 0))
    full = lambda shape: pl.BlockSpec(shape, lambda b, s: (0,) * len(shape))
    out_sds = lambda c, dtype=BF16: jax.ShapeDtypeStruct((B, R, SI, c), dtype)
    return pl.pallas_call(
        functools.partial(_mix_in_kernel, d_pool=d_pool, d_attn=d_attn, ti=ti, natural_x=natural_x),
        grid=(B, SI // ti),
        in_specs=[pl.BlockSpec((None, ti, R, D), lambda b, s: (b, s, 0, 0)) if natural_x else seq_spec(D),
                  full((1, D)), full((D, d_in)), full(w_pool.shape), full((1, d_pool))],
        out_specs=[seq_spec(d_pool)] + [seq_spec(d_attn)] * 6,
        out_shape=[out_sds(d_pool)] + [out_sds(d_attn)] * 3 + [out_sds(d_attn, F32)] * 3,
        scratch_shapes=[pltpu.VMEM((R, POOL_HIST + ti, d_pool), F32), pltpu.VMEM((R, ti, d_pool), F32)],
        compiler_params=_params("parallel", "arbitrary"),
        name="mix_in",
    )(x, g.reshape(1, D), w_in, w_pool, pool_scale.reshape(1, d_pool))


def _attn_kernel(q_ref, k_ref, v_ref, o_ref, lse_ref, bias_ref, *, nq, strips, d_attn):
    _, classes, chunk, _ = q_ref.shape
    L = nq // strips
    nk, n_blocks = 2 * nq, chunk // L
    first_block = pl.program_id(2) * n_blocks
    row = lax.broadcasted_iota(jnp.int32, (nq, 1), 0)
    col = lax.broadcasted_iota(jnp.int32, (1, nk), 1)
    q_strip, q_row = row >> (L.bit_length() - 1), row & (L - 1)
    k_strip, k_row = col >> ((2 * L).bit_length() - 1), col & (2 * L - 1)
    back = strips * (q_row - k_row) + (q_strip - k_strip)

    def band(offset):
        rel = back + strips * offset
        return jnp.where((rel >= 0) & (rel <= WINDOW_STEPS), 0.0, MASKED).astype(F32)

    bias_ref[0] = band(0)
    bias_ref[1] = band(L)
    lane = lax.broadcasted_iota(jnp.int32, (nq, LANES), 1)
    lo_half = lane < HEAD_DIM

    def block(it, carry):
        g, j_local = it // n_blocks, it % n_blocks
        j = first_block + j_local
        q_rows = pl.ds(pl.multiple_of(j_local * L, L), L)
        k_rows = pl.ds(pl.multiple_of(jnp.maximum(j - 1, 0) * L, L), 2 * L)
        bias = bias_ref[jnp.minimum(j, 1)]
        lse_tile = jnp.zeros((nq, LANES), F32)
        for hp in range(d_attn // LANES):
            cols = slice(hp * LANES, (hp + 1) * LANES)
            qp = q_ref[:, g, q_rows, cols].reshape(nq, LANES).astype(BF16)
            kk = k_ref[:, g, k_rows, cols].reshape(nk, LANES).astype(BF16)
            vv = v_ref[:, g, k_rows, cols].reshape(nk, LANES).astype(BF16)
            outs, lses = [], []
            for sub in range(2):
                keep = lo_half if sub == 0 else jnp.logical_not(lo_half)
                qm = jnp.where(keep, qp, jnp.zeros_like(qp))
                s = lax.dot_general(qm, kk, (((1,), (1,)), ((), ())), preferred_element_type=F32) + bias
                m = jnp.max(s, axis=-1, keepdims=True)
                p = jnp.exp2(s - m)
                l = jnp.sum(p, axis=-1, keepdims=True)
                o = jnp.dot(p.astype(BF16), vv, preferred_element_type=F32)
                outs.append(o / l)
                lses.append(m + jnp.log2(l))
            o_pair = jnp.where(lo_half, outs[0], outs[1]).astype(o_ref.dtype)
            o_ref[:, g, q_rows, cols] = o_pair.reshape(strips, L, LANES)
            pair_lse = jnp.where(lane == 2 * hp, lses[0], lses[1])
            lse_tile = jnp.where((lane >> 1) == hp, pair_lse, lse_tile)
        lse_ref[:, g, q_rows, :] = lse_tile.reshape(strips, L, LANES)
        return carry

    lax.fori_loop(0, classes * n_blocks, block, 0, unroll=ATTN_UNROLL_QUERIES // nq)


def _attn_branch(q, k, v, dil):
    B, R, SI, C = q.shape
    strips = R // dil
    row_tile = 8 * 4 // q.dtype.itemsize
    nq = max(WINDOW_STEPS, strips * row_tile)
    L = nq // strips
    classes = max(1, min(dil, ATTN_STEP_ROWS // (strips * SI)))
    chunk = min(SI, max(L, ATTN_STEP_ROWS // (strips * classes)))
    assert R % dil == 0 and SI % chunk == 0 and chunk % L == 0 and SI >= 2 * L and dil % classes == 0
    assert C % LANES == 0 and C // HEAD_DIM <= LANES and nq & (nq - 1) == 0 and L & (L - 1) == 0
    view = lambda t: t.reshape(B, strips, dil, SI, t.shape[-1])
    q_spec = lambda c: pl.BlockSpec((None, strips, classes, chunk, c), lambda b, g, s: (b, 0, g, s, 0))
    kv_spec = pl.BlockSpec((None, strips, classes, SI, C), lambda b, g, s: (b, 0, g, 0, 0),
                           pipeline_mode=pl.Buffered(1 if SI > chunk else 2))
    o, lse = pl.pallas_call(
        functools.partial(_attn_kernel, nq=nq, strips=strips, d_attn=C),
        scratch_shapes=[pltpu.VMEM((2, nq, 2 * nq), F32)],
        grid=(B, dil // classes, SI // chunk),
        in_specs=[q_spec(C), kv_spec, kv_spec],
        out_specs=[q_spec(C), q_spec(LANES)],
        out_shape=[jax.ShapeDtypeStruct((B, strips, dil, SI, C), q.dtype),
                   jax.ShapeDtypeStruct((B, strips, dil, SI, LANES), F32)],
        compiler_params=_params("parallel", "parallel", "arbitrary"),
        name=f"attn_d{dil}",
    )(view(q), view(k), view(v))
    return o.reshape(B, R, SI, C), lse.reshape(B, R, SI, LANES)


def _route(h, wr, n_experts):
    h_hi = h.astype(BF16)
    h_lo = (h - h_hi.astype(F32)).astype(BF16)
    both = jnp.dot(h_hi, wr, preferred_element_type=F32)
    logits = both[:, :LANES] + both[:, LANES:] + jnp.dot(h_lo, wr[:, :LANES], preferred_element_type=F32)
    rows = h.shape[0]
    scores = logits.T[:n_experts]
    expert = lax.broadcasted_iota(jnp.int32, scores.shape, 0)
    picks = []
    for _ in range(TOP_K):
        m = jnp.max(scores, axis=0, keepdims=True)
        idx = jnp.min(jnp.where(scores == m, expert, n_experts), axis=0, keepdims=True)
        picks.append((m, idx))
        scores = jnp.where(expert == idx, -jnp.inf, scores)
    (m1, i1), (m2, i2) = picks
    e2 = jnp.exp(m2 - m1)
    g1 = 1.0 / (1.0 + e2)
    g2 = e2 / (1.0 + e2)
    gates = jnp.where(expert == i1, g1, jnp.where(expert == i2, g2, 0.0))
    picked = jnp.where(jnp.logical_or(expert == i1, expert == i2), 1.0, 0.0)
    tile = jnp.concatenate([gates, picked, jnp.zeros((LANES - 2 * n_experts, rows), F32)], axis=0)
    return tile.T


def _mix_out_kernel(*refs, d_pool, natural_x, n_experts):
    if n_experts:
        (x_ref, ya_ref, o1_ref, o2_ref, o3_ref, l1_ref, l2_ref, l3_ref, gain_ref, wo_ref, gffn_ref, exp_ref,
         wr_ref, x1_ref, h2_ref, route_ref) = refs
    else:
        (x_ref, ya_ref, o1_ref, o2_ref, o3_ref, l1_ref, l2_ref, l3_ref, gain_ref, wo_ref, gffn_ref, exp_ref,
         x1_ref, h2_ref) = refs
    lses = [_token_rows(l) for l in (l1_ref, l2_ref, l3_ref)]
    top = jnp.maximum(jnp.maximum(lses[0], lses[1]), lses[2])
    es = [jnp.exp2(l - top) for l in lses]
    den = es[0] + es[1] + es[2]
    expand = exp_ref[...]

    def per_lane(w):
        hi = w.astype(BF16)
        lo = (w - hi.astype(F32)).astype(BF16)
        return (jnp.dot(hi, expand, preferred_element_type=F32)
                + jnp.dot(lo, expand, preferred_element_type=F32))

    o = 0.0
    for e, o_ref in zip(es, (o1_ref, o2_ref, o3_ref)):
        o = o + per_lane(e / den) * _token_rows(o_ref).astype(F32)
    yb = _rms(o, gain_ref[...]).astype(BF16)
    y = (jnp.dot(_token_rows(ya_ref), wo_ref[:d_pool, :], preferred_element_type=F32)
         + jnp.dot(yb, wo_ref[d_pool:, :], preferred_element_type=F32))
    x1 = _token_rows(x_ref, natural_x) + y
    x1_ref[...] = x1.reshape(x1_ref.shape)
    h2 = _rms(x1, gffn_ref[...])
    if n_experts:
        route_ref[...] = _route(h2, wr_ref[...], n_experts).reshape(route_ref.shape)
        h2_ref[...] = _pack_halves(h2).reshape(h2_ref.shape)
    else:
        h2_ref[...] = h2.astype(h2_ref.dtype).reshape(h2_ref.shape)


def _mix_out(x, ya, outs, lses, attn_gain, w_out, g_ffn, w_router, *, natural_x, ti=32):
    B, R, SI, d_pool = ya.shape
    D = x.shape[-1]
    d_attn = outs[0].shape[-1]
    ti = min(ti, SI)
    head_of_lane = jnp.arange(d_attn) // HEAD_DIM
    expand = (jnp.arange(LANES)[:, None] == head_of_lane[None, :]).astype(BF16)
    seq = lambda c: pl.BlockSpec((None, R, ti, c), lambda b, s: (b, 0, s, 0))
    x_spec = pl.BlockSpec((None, ti, R, D), lambda b, s: (b, s, 0, 0)) if natural_x else seq(D)
    full = lambda shape: pl.BlockSpec(shape, lambda b, s: (0,) * len(shape))
    tokens = lambda c, dtype: jax.ShapeDtypeStruct((B, R, SI, c), dtype)
    in_specs = ([x_spec, seq(d_pool)] + [seq(d_attn)] * 3 + [seq(LANES)] * 3
                + [full((1, d_attn)), full(w_out.shape), full((1, D)), full(expand.shape)])
    args = [x, ya, *outs, *lses, attn_gain.reshape(1, d_attn), w_out, g_ffn.reshape(1, D), expand]
    if w_router is None:
        n_experts = 0
        out_specs, out_shape = [seq(D), seq(D)], [tokens(D, F32), tokens(D, BF16)]
    else:
        n_experts = w_router.shape[-1]
        assert 2 * n_experts <= LANES and TOP_K == 2
        wr = jnp.zeros((D, LANES), F32).at[:, :n_experts].set(w_router)
        wr_hi = wr.astype(BF16)
        in_specs.append(full((D, 2 * LANES)))
        args.append(jnp.concatenate([wr_hi, (wr - wr_hi.astype(F32)).astype(BF16)], axis=1))
        out_specs = [seq(D), seq(D // 2), seq(LANES)]
        out_shape = [tokens(D, F32), tokens(D // 2, U32), tokens(LANES, F32)]
    return pl.pallas_call(
        functools.partial(_mix_out_kernel, d_pool=d_pool, natural_x=natural_x, n_experts=n_experts),
        grid=(B, SI // ti),
        in_specs=in_specs, out_specs=out_specs, out_shape=out_shape,
        compiler_params=_params("parallel", "parallel"),
        name="mix_out",
    )(*args)


def _swiglu_hidden(h, wg, wu):
    a = jnp.dot(h, wg, preferred_element_type=F32)
    b = jnp.dot(h, wu, preferred_element_type=F32)
    return (a * jax.nn.sigmoid(a) * b).astype(BF16)


def _finish(x, y, gfin_ref):
    out = x + y
    return out if gfin_ref is None else _rms(out, gfin_ref[...])


def _ffn_kernel(*refs, final, tf):
    if final:
        h_ref, x_ref, wg_ref, wu_ref, wd_ref, gfin_ref, o_ref, acc = refs
    else:
        (h_ref, x_ref, wg_ref, wu_ref, wd_ref, o_ref, acc), gfin_ref = refs, None
    h = h_ref[...]
    for c in range(wg_ref.shape[-1] // tf):
        cols = slice(c * tf, (c + 1) * tf)
        hid = _swiglu_hidden(h, wg_ref[:, cols], wu_ref[:, cols])
        part = jnp.dot(hid, wd_ref[cols, :], preferred_element_type=F32)
        if c == 0:
            acc[...] = part
        else:
            acc[...] += part
    o_ref[...] = _finish(x_ref[...], acc[...], gfin_ref)


def _pick_chunk(n, target):
    best = None
    for c in range(LANES, min(n, target) + 1, LANES):
        if n % c == 0:
            best = c
    assert best is not None
    return best


def _ffn(h, x, wg, wu, wd, g_final, *, tm=512, tf_target=1536):
    N, D = x.shape
    F = wg.shape[-1]
    tm = min(tm, N)
    tf = _pick_chunk(F, tf_target)
    final = g_final is not None
    row = pl.BlockSpec((tm, D), lambda i: (i, 0))
    resident = lambda shape: pl.BlockSpec(shape, lambda i: (0, 0), pipeline_mode=pl.Buffered(1))
    in_specs = [row, row, resident((D, F)), resident((D, F)), resident((F, D))]
    args = [h, x, wg, wu, wd]
    if final:
        in_specs.append(pl.BlockSpec((1, D), lambda i: (0, 0)))
        args.append(g_final.reshape(1, D))
    return pl.pallas_call(
        functools.partial(_ffn_kernel, final=final, tf=tf),
        grid=(N // tm,),
        in_specs=in_specs,
        out_specs=row,
        out_shape=jax.ShapeDtypeStruct((N, D), F32),
        scratch_shapes=[pltpu.VMEM((tm, D), F32)],
        compiler_params=_params("parallel"),
        name="ffn_dense",
    )(*args)


def _plan_rows(route, n_experts, *, tm):
    N = route.shape[0]
    E = n_experts
    i32 = jnp.int32
    p_max = TOP_K * N + E * tm
    gate = route[:, :E].T
    sel = (route[:, E:2 * E].T > 0.5).astype(i32)
    rank = jnp.cumsum(sel, axis=1) - sel
    seg_len = (jnp.sum(sel, axis=1) + tm - 1) // tm * tm
    seg_end = jnp.cumsum(seg_len)
    pos = jnp.where(sel > 0, (seg_end - seg_len)[:, None] + rank, -1)
    tile_start = jnp.arange(p_max // tm, dtype=i32) * tm
    tile_expert = jnp.minimum(jnp.sum(seg_end[None, :] <= tile_start[:, None], axis=1), E - 1).astype(i32)
    row_hi = jnp.max(pos, axis=0)
    row_lo = jnp.sum(jnp.where(sel > 0, pos, 0), axis=0) - row_hi
    gate_hi = jnp.sum(jnp.where(pos == row_hi[None, :], gate, 0.0), axis=0)
    gate_lo = jnp.sum(jnp.where(pos == row_lo[None, :], gate, 0.0), axis=0)
    pair_gates = jnp.concatenate([gate_hi[:, None], gate_lo[:, None], jnp.zeros((N, LANES - 2), F32)], axis=1)
    return dict(rows=(row_hi.astype(i32), row_lo.astype(i32)), pair_gates=pair_gates, tile_expert=tile_expert,
                n_active_tiles=(seg_end[-1] // tm).reshape(1).astype(i32), p_max=p_max)


def _sc_rows_kernel(n_rows, n_out, d, scatter):
    per_worker = n_rows // SC_WORKERS
    assert n_rows % (SC_WORKERS * SC_WINDOW) == 0
    mesh = plsc.VectorSubcoreMesh(core_axis_name="c", subcore_axis_name="s")
    out_type = (jax.ShapeDtypeStruct((n_out, d), U32) if scatter
                else [jax.ShapeDtypeStruct((n_rows, d), U32)] * 2)

    def body(*refs):
        if scatter:
            src_hbm, hi_hbm, lo_hbm, out_hbm, idx_hi, idx_lo, rows_hi, rows_lo, sem_hi, sem_lo = refs
        else:
            src_hbm, hi_hbm, lo_hbm, out_hi_hbm, out_lo_hbm, idx_hi, idx_lo, rows_hi, rows_lo, sem_hi, sem_lo = refs
        worker = lax.axis_index("s") * SC_CORES + lax.axis_index("c")

        @pl.loop(0, per_worker // SC_WINDOW)
        def _(i):
            window = pl.ds(worker * per_worker + i * SC_WINDOW, SC_WINDOW)
            pltpu.sync_copy(hi_hbm.at[window], idx_hi)
            pltpu.sync_copy(lo_hbm.at[window], idx_lo)
            if scatter:
                pltpu.sync_copy(src_hbm.at[window], rows_hi)
                to_hi = pltpu.async_copy(rows_hi, out_hbm.at[idx_hi], sem_hi)
                to_lo = pltpu.async_copy(rows_hi, out_hbm.at[idx_lo], sem_lo)
                to_hi.wait()
                to_lo.wait()
            else:
                from_hi = pltpu.async_copy(src_hbm.at[idx_hi], rows_hi, sem_hi)
                from_lo = pltpu.async_copy(src_hbm.at[idx_lo], rows_lo, sem_lo)
                from_hi.wait()
                pltpu.sync_copy(rows_hi, out_hi_hbm.at[window])
                from_lo.wait()
                pltpu.sync_copy(rows_lo, out_lo_hbm.at[window])

    index_vec, row_buf = pltpu.VMEM((SC_WINDOW,), jnp.int32), pltpu.VMEM((SC_WINDOW, d), U32)
    return pl.kernel(body, mesh=mesh, out_type=out_type,
                     scratch_types=[index_vec, index_vec, row_buf, row_buf,
                                    pltpu.SemaphoreType.DMA, pltpu.SemaphoreType.DMA])


def _moe_experts_kernel(te_ref, na_ref, xs_ref, wg_hbm, wu_hbm, wd_hbm, o_ref,
                        cache_g, cache_u, cache_d, stage_g, stage_u, stage_d, sems, acc, *, tf):
    i = pl.program_id(0)
    e = te_ref[i]
    n_chunks = cache_g.shape[-1] // tf
    active = i < na_ref[0]
    new_expert = jnp.logical_or(i == 0, e != te_ref[jnp.maximum(i - 1, 0)])

    def chunk_copies(c, slot):
        cols = pl.ds(c * tf, tf)
        return (pltpu.make_async_copy(wg_hbm.at[e, :, cols], stage_g.at[slot], sems.at[0, slot]),
                pltpu.make_async_copy(wu_hbm.at[e, :, cols], stage_u.at[slot], sems.at[1, slot]),
                pltpu.make_async_copy(wd_hbm.at[e, cols, :], stage_d.at[slot], sems.at[2, slot]))

    def tile_ffn(load_weights):
        x = _unpack_halves(xs_ref[...]).astype(BF16)
        if load_weights:
            for cp in chunk_copies(0, 0):
                cp.start()
        for c in range(n_chunks):
            cols = slice(c * tf, (c + 1) * tf)
            if load_weights:
                slot = c % 2
                if c + 1 < n_chunks:
                    for cp in chunk_copies(c + 1, 1 - slot):
                        cp.start()
                for cp in chunk_copies(c, slot):
                    cp.wait()
                cache_g[:, cols] = stage_g[slot].astype(BF16)
                cache_u[:, cols] = stage_u[slot].astype(BF16)
                cache_d[cols, :] = stage_d[slot].astype(BF16)
            hid = _swiglu_hidden(x, cache_g[:, cols], cache_u[:, cols])
            part = jnp.dot(hid, cache_d[cols, :], preferred_element_type=F32)
            if c == 0:
                acc[...] = part
            else:
                acc[...] += part
        o_ref[...] = _pack_halves(acc[...])

    pl.when(jnp.logical_and(active, new_expert))(lambda: tile_ffn(True))
    pl.when(jnp.logical_and(active, jnp.logical_not(new_expert)))(lambda: tile_ffn(False))


def _moe_experts(xs, plan, wg, wu, wd, *, tm, tf_target=512):
    P = xs.shape[0]
    E, D, F = wg.shape
    tf = _pick_chunk(F, tf_target)
    tile = lambda i, te, na: (jnp.minimum(i, na[0] - 1), 0)
    in_hbm = pl.BlockSpec(memory_space=pl.ANY)
    grid_spec = pltpu.PrefetchScalarGridSpec(
        num_scalar_prefetch=2,
        grid=(P // tm,),
        in_specs=[pl.BlockSpec((tm, D // 2), tile), in_hbm, in_hbm, in_hbm],
        out_specs=pl.BlockSpec((tm, D // 2), tile),
        scratch_shapes=[pltpu.VMEM((D, F), BF16), pltpu.VMEM((D, F), BF16), pltpu.VMEM((F, D), BF16),
                        pltpu.VMEM((2, D, tf), F32), pltpu.VMEM((2, D, tf), F32), pltpu.VMEM((2, tf, D), F32),
                        pltpu.SemaphoreType.DMA((3, 2)), pltpu.VMEM((tm, D), F32)],
    )
    return pl.pallas_call(
        functools.partial(_moe_experts_kernel, tf=tf),
        grid_spec=grid_spec,
        out_shape=jax.ShapeDtypeStruct((P, D // 2), U32),
        compiler_params=_params("arbitrary", vmem_limit=MOE_VMEM_LIMIT),
        name="moe_experts",
    )(plan["tile_expert"], plan["n_active_tiles"], xs, wg, wu, wd)


def _moe_mix_kernel(*refs, final, natural_out):
    if final:
        x_ref, yh_ref, yl_ref, g_ref, gfin_ref, o_ref = refs
    else:
        (x_ref, yh_ref, yl_ref, g_ref, o_ref), gfin_ref = refs, None
    gates = _token_rows(g_ref)
    lane = lax.broadcasted_iota(jnp.int32, gates.shape, 1)
    g_hi = jnp.sum(jnp.where(lane == 0, gates, 0.0), axis=-1, keepdims=True)
    g_lo = jnp.sum(jnp.where(lane == 1, gates, 0.0), axis=-1, keepdims=True)
    y = g_hi * _unpack_halves(_token_rows(yh_ref)) + g_lo * _unpack_halves(_token_rows(yl_ref))
    out = _finish(_token_rows(x_ref), y, gfin_ref).reshape(x_ref.shape)
    o_ref[...] = pltpu.einshape("rid->ird", out) if natural_out else out


def _moe_mix(x, y_hi, y_lo, pair_gates, g_final, *, natural_out, ti=32):
    B, R, SI, D = x.shape
    ti = min(ti, SI)
    final = g_final is not None
    seq = lambda c: pl.BlockSpec((None, R, ti, c), lambda b, s: (b, 0, s, 0))
    as_tokens = lambda t: t.reshape(B, R, SI, t.shape[-1])
    in_specs = [seq(D), seq(D // 2), seq(D // 2), seq(LANES)]
    args = [x, as_tokens(y_hi), as_tokens(y_lo), as_tokens(pair_gates)]
    if final:
        in_specs.append(pl.BlockSpec((1, D), lambda b, s: (0, 0)))
        args.append(g_final.reshape(1, D))
    out_spec = pl.BlockSpec((None, ti, R, D), lambda b, s: (b, s, 0, 0)) if natural_out else seq(D)
    return pl.pallas_call(
        functools.partial(_moe_mix_kernel, final=final, natural_out=natural_out),
        grid=(B, SI // ti), in_specs=in_specs, out_specs=out_spec,
        out_shape=jax.ShapeDtypeStruct((B, SI, R, D) if natural_out else (B, R, SI, D), F32),
        compiler_params=_params("parallel", "parallel"),
        name="moe_mix",
    )(*args)


def _moe(h, x, route, wg, wu, wd, g_final, *, natural_out, tm=512):
    N, half = h.shape
    plan = _plan_rows(route, wg.shape[0], tm=tm)
    rows_hi, rows_lo = plan["rows"]
    xs = _sc_rows_kernel(N, plan["p_max"], half, scatter=True)(h, rows_hi, rows_lo)
    ys = _moe_experts(xs, plan, wg, wu, wd, tm=tm)
    y_hi, y_lo = _sc_rows_kernel(N, plan["p_max"], half, scatter=False)(ys, rows_hi, rows_lo)
    return _moe_mix(x, y_hi, y_lo, plan["pair_gates"], g_final, natural_out=natural_out)


def kernel(x, norm_mix, w_in, w_pool, pool_scale, attn_gain, w_out, norm_ffn, ffn_wg, ffn_wu, ffn_wd,
           w_router, moe_wg, moe_wu, moe_wd, final_norm):
    B, S, D = x.shape
    depth = norm_mix.shape[0]
    bf = lambda t: t.astype(BF16)
    w_in, w_pool, w_out = bf(w_in), bf(w_pool), bf(w_out)
    ffn_wg, ffn_wu, ffn_wd = bf(ffn_wg), bf(ffn_wu), bf(ffn_wd)
    R = RESIDUES
    assert S % R == 0
    N, SI = B * S, S // R
    x = x.reshape(B, SI, R, D)
    for l in range(depth):
        first, last = l == 0, l == depth - 1
        ya, q, k, v, q32, k32, v32 = _mix_in(x, norm_mix[l], w_in[l], w_pool[l], pool_scale[l], natural_x=first)
        narrow = lambda dil: (R // dil) * 16 > WINDOW_STEPS
        branches = [_attn_branch(q32, k32, v32, dil) if narrow(dil) else _attn_branch(q, k, v, dil)
                    for _, dil in DILATED_PATTERNS]
        i, routed = l // 2, l % 2 == 1
        x1, h2, *route = _mix_out(x, ya, [o for o, _ in branches], [lse for _, lse in branches],
                                  attn_gain[l], w_out[l], norm_ffn[l], w_router[i] if routed else None,
                                  natural_x=first)
        g_final = final_norm if last else None
        if routed:
            x = _moe(h2.reshape(N, D // 2), x1, route[0].reshape(N, LANES), moe_wg[i], moe_wu[i], moe_wd[i],
                     g_final, natural_out=last)
        else:
            x = _ffn(h2.reshape(N, D), x1.reshape(N, D), ffn_wg[i], ffn_wu[i], ffn_wd[i], g_final)
            x = x.reshape(B, R, SI, D)
            if last:
                x = x.transpose(0, 2, 1, 3)
    return x.reshape(B, S, D)
```

```python
import functools

import jax
import jax.numpy as jnp
import numpy as np
from jax import lax
from jax.experimental import pallas as pl
from jax.experimental.pallas import tpu as pltpu
from jax.experimental.pallas import tpu_sc as plsc

F32 = jnp.float32
BF16 = jnp.bfloat16
U32 = jnp.uint32
HIGH_HALF = np.uint32(0xFFFF0000)

EPS = 1e-6
LANES = 128
HEAD_DIM = 64
POOL_WINDOWS = (2, 4, 8, 16)
POOL_HIST = 8
DILATED_PATTERNS = ((128, 1), (512, 4), (2048, 16))
WINDOW_STEPS = 128
RESIDUES = 16
ATTN_STEP_ROWS = 1024
ATTN_UNROLL_QUERIES = 512
LOG2_E = 1.4426950408889634
SC_CORES, SC_SUBCORES = 2, 16
SC_WORKERS = SC_CORES * SC_SUBCORES
SC_WINDOW = 64
assert all(w // d == WINDOW_STEPS and RESIDUES % d == 0 for w, d in DILATED_PATTERNS)
TOP_K = 2
MASKED = -1e30
VMEM_LIMIT = 48 * 1024 * 1024
MOE_VMEM_LIMIT = 56 * 1024 * 1024


def _rms(x, g):
    return x * lax.rsqrt(jnp.mean(x * x, axis=-1, keepdims=True) + EPS) * g


def _params(*sem, vmem_limit=VMEM_LIMIT):
    return pltpu.CompilerParams(dimension_semantics=sem, vmem_limit_bytes=vmem_limit)


def _token_rows(ref, natural=False):
    t = ref[...]
    if natural:
        t = pltpu.einshape("ird->rid", t)
    return t.reshape(-1, t.shape[-1])


def _pack_halves(t):
    bits = lax.bitcast_convert_type(t, U32)
    bits = (bits + np.uint32(0x7FFF) + ((bits >> 16) & np.uint32(1))) & HIGH_HALF
    half = t.shape[-1] // 2
    return (bits[:, :half] >> 16) | bits[:, half:]


def _unpack_halves(words):
    halves = [lax.bitcast_convert_type(w, F32) for w in (words << 16, words & HIGH_HALF)]
    return jnp.concatenate(halves, axis=-1)


def _mix_in_kernel(x_ref, g_ref, w_ref, wp_ref, ps_ref, ya_ref, q_ref, k_ref, v_ref, q32_ref, k32_ref, v32_ref,
                   ubuf, uprev, *, d_pool, d_attn, ti, natural_x):
    s = pl.program_id(1)
    R = RESIDUES
    blk = lambda t: t.reshape(R, ti, t.shape[-1])
    h = _rms(_token_rows(x_ref, natural_x), g_ref[...]).astype(BF16)
    proj = jnp.dot(h, w_ref[...], preferred_element_type=F32)
    scale = HEAD_DIM ** -0.5 * LOG2_E
    qkv = (proj[:, d_pool:d_pool + d_attn] * scale, proj[:, d_pool + d_attn:d_pool + 2 * d_attn],
           proj[:, d_pool + 2 * d_attn:])
    for t, ref, ref32 in zip(qkv, (q_ref, k_ref, v_ref), (q32_ref, k32_ref, v32_ref)):
        ref[...] = blk(t.astype(BF16))
        ref32[...] = blk(t)

    @pl.when(s == 0)
    def _():
        ubuf[:, 0:POOL_HIST, :] = jnp.zeros((R, POOL_HIST, d_pool), F32)

    ubuf[:, POOL_HIST:POOL_HIST + ti, :] = blk(proj[:, :d_pool])
    uprev[...] = ubuf[:, POOL_HIST - 1:POOL_HIST - 1 + ti, :]
    at_start = (s * ti + lax.broadcasted_iota(jnp.int32, (ti, 1), 0)) == 0
    group = d_pool // len(POOL_WINDOWS)
    zs = []
    for gi, w in enumerate(POOL_WINDOWS):
        cols = slice(gi * group, (gi + 1) * group)
        ds = []
        for r in range(R):
            ug = ubuf[r, POOL_HIST:POOL_HIST + ti, cols]
            win = ug
            for back in range(1, w):
                rr = r - back
                win = win + (ubuf[rr, POOL_HIST:POOL_HIST + ti, cols] if rr >= 0 else uprev[rr + R, :, cols])
            cnt = jnp.where(at_start, float(min(r + 1, w)), float(w))
            ds.append(win / cnt - ug)
        d = jnp.concatenate(ds, axis=0).astype(BF16)
        zs.append(jnp.dot(d, wp_ref[gi], preferred_element_type=F32))
    z = jnp.concatenate(zs, axis=-1)
    ya_ref[...] = blk(_rms(z, ps_ref[...]).astype(BF16))
    ubuf[:, POOL_HIST - 1:POOL_HIST, :] = ubuf[:, POOL_HIST + ti - 1:POOL_HIST + ti, :]


def _mix_in(x, g, w_in, w_pool, pool_scale, *, natural_x, ti=32):
    B, R, SI, D = x.shape
    if natural_x:
        R, SI = SI, R
    d_pool = pool_scale.shape[-1]
    d_in = w_in.shape[-1]
    d_attn = (d_in - d_pool) // 3
    ti = min(ti, SI)
    assert R == RESIDUES >= max(POOL_WINDOWS) and SI % ti == 0 and ti % 16 == 0
    assert d_pool % (LANES * len(POOL_WINDOWS)) == 0
    seq_spec = lambda c: pl.BlockSpec((None, R, ti, c), lambda b, s: (b, 0, s, 0))
    full = lambda shape: pl.BlockSpec(shape, lambda b, s: (0,) * len(shape))
    out_sds = lambda c, dtype=BF16: jax.ShapeDtypeStruct((B, R, SI, c), dtype)
    return pl.pallas_call(
        functools.partial(_mix_in_kernel, d_pool=d_pool, d_attn=d_attn, ti=ti, natural_x=natural_x),
        grid=(B, SI // ti),
        in_specs=[pl.BlockSpec((None, ti, R, D), lambda b, s: (b, s, 0, 0)) if natural_x else seq_spec(D),
                  full((1, D)), full((D, d_in)), full(w_pool.shape), full((1, d_pool))],
        out_specs=[seq_spec(d_pool)] + [seq_spec(d_attn)] * 6,
        out_shape=[out_sds(d_pool)] + [out_sds(d_attn)] * 3 + [out_sds(d_attn, F32)] * 3,
        scratch_shapes=[pltpu.VMEM((R, POOL_HIST + ti, d_pool), F32), pltpu.VMEM((R, ti, d_pool), F32)],
        compiler_params=_params("parallel", "arbitrary"),
        name="mix_in",
    )(x, g.reshape(1, D), w_in, w_pool, pool_scale.reshape(1, d_pool))


def _attn_kernel(q_ref, k_ref, v_ref, o_ref, lse_ref, bias_ref, *, nq, strips, d_attn):
    _, classes, chunk, _ = q_ref.shape
    L = nq // strips
    nk, n_blocks = 2 * nq, chunk // L
    first_block = pl.program_id(2) * n_blocks
    row = lax.broadcasted_iota(jnp.int32, (nq, 1), 0)
    col = lax.broadcasted_iota(jnp.int32, (1, nk), 1)
    q_strip, q_row = row >> (L.bit_length() - 1), row & (L - 1)
    k_strip, k_row = col >> ((2 * L).bit_length() - 1), col & (2 * L - 1)
    back = strips * (q_row - k_row) + (q_strip - k_strip)

    def band(offset):
        rel = back + strips * offset
        return jnp.where((rel >= 0) & (rel <= WINDOW_STEPS), 0.0, MASKED).astype(F32)

    bias_ref[0] = band(0)
    bias_ref[1] = band(L)
    lane = lax.broadcasted_iota(jnp.int32, (nq, LANES), 1)
    lo_half = lane < HEAD_DIM

    def block(it, carry):
        g, j_local = it // n_blocks, it % n_blocks
        j = first_block + j_local
        q_rows = pl.ds(pl.multiple_of(j_local * L, L), L)
        k_rows = pl.ds(pl.multiple_of(jnp.maximum(j - 1, 0) * L, L), 2 * L)
        bias = bias_ref[jnp.minimum(j, 1)]
        lse_tile = jnp.zeros((nq, LANES), F32)
        for hp in range(d_attn // LANES):
            cols = slice(hp * LANES, (hp + 1) * LANES)
            qp = q_ref[:, g, q_rows, cols].reshape(nq, LANES).astype(BF16)
            kk = k_ref[:, g, k_rows, cols].reshape(nk, LANES).astype(BF16)
            vv = v_ref[:, g, k_rows, cols].reshape(nk, LANES).astype(BF16)
            outs, lses = [], []
            for sub in range(2):
                keep = lo_half if sub == 0 else jnp.logical_not(lo_half)
                qm = jnp.where(keep, qp, jnp.zeros_like(qp))
                s = lax.dot_general(qm, kk, (((1,), (1,)), ((), ())), preferred_element_type=F32) + bias
                m = jnp.max(s, axis=-1, keepdims=True)
                p = jnp.exp2(s - m)
                l = jnp.sum(p, axis=-1, keepdims=True)
                o = jnp.dot(p.astype(BF16), vv, preferred_element_type=F32)
                outs.append(o / l)
                lses.append(m + jnp.log2(l))
            o_pair = jnp.where(lo_half, outs[0], outs[1]).astype(o_ref.dtype)
            o_ref[:, g, q_rows, cols] = o_pair.reshape(strips, L, LANES)
            pair_lse = jnp.where(lane == 2 * hp, lses[0], lses[1])
            lse_tile = jnp.where((lane >> 1) == hp, pair_lse, lse_tile)
        lse_ref[:, g, q_rows, :] = lse_tile.reshape(strips, L, LANES)
        return carry

    lax.fori_loop(0, classes * n_blocks, block, 0, unroll=ATTN_UNROLL_QUERIES // nq)


def _attn_branch(q, k, v, dil):
    B, R, SI, C = q.shape
    strips = R // dil
    row_tile = 8 * 4 // q.dtype.itemsize
    nq = max(WINDOW_STEPS, strips * row_tile)
    L = nq // strips
    classes = max(1, min(dil, ATTN_STEP_ROWS // (strips * SI)))
    chunk = min(SI, max(L, ATTN_STEP_ROWS // (strips * classes)))
    assert R % dil == 0 and SI % chunk == 0 and chunk % L == 0 and SI >= 2 * L and dil % classes == 0
    assert C % LANES == 0 and C // HEAD_DIM <= LANES and nq & (nq - 1) == 0 and L & (L - 1) == 0
    view = lambda t: t.reshape(B, strips, dil, SI, t.shape[-1])
    q_spec = lambda c: pl.BlockSpec((None, strips, classes, chunk, c), lambda b, g, s: (b, 0, g, s, 0))
    kv_spec = pl.BlockSpec((None, strips, classes, SI, C), lambda b, g, s: (b, 0, g, 0, 0),
                           pipeline_mode=pl.Buffered(1 if SI > chunk else 2))
    o, lse = pl.pallas_call(
        functools.partial(_attn_kernel, nq=nq, strips=strips, d_attn=C),
        scratch_shapes=[pltpu.VMEM((2, nq, 2 * nq), F32)],
        grid=(B, dil // classes, SI // chunk),
        in_specs=[q_spec(C), kv_spec, kv_spec],
        out_specs=[q_spec(C), q_spec(LANES)],
        out_shape=[jax.ShapeDtypeStruct((B, strips, dil, SI, C), q.dtype),
                   jax.ShapeDtypeStruct((B, strips, dil, SI, LANES), F32)],
        compiler_params=_params("parallel", "parallel", "arbitrary"),
        name=f"attn_d{dil}",
    )(view(q), view(k), view(v))
    return o.reshape(B, R, SI, C), lse.reshape(B, R, SI, LANES)


def _route(h, wr, n_experts):
    h_hi = h.astype(BF16)
    h_lo = (h - h_hi.astype(F32)).astype(BF16)
    both = jnp.dot(h_hi, wr, preferred_element_type=F32)
    logits = both[:, :LANES] + both[:, LANES:] + jnp.dot(h_lo, wr[:, :LANES], preferred_element_type=F32)
    rows = h.shape[0]
    scores = logits.T[:n_experts]
    expert = lax.broadcasted_iota(jnp.int32, scores.shape, 0)
    picks = []
    for _ in range(TOP_K):
        m = jnp.max(scores, axis=0, keepdims=True)
        idx = jnp.min(jnp.where(scores == m, expert, n_experts), axis=0, keepdims=True)
        picks.append((m, idx))
        scores = jnp.where(expert == idx, -jnp.inf, scores)
    (m1, i1), (m2, i2) = picks
    e2 = jnp.exp(m2 - m1)
    g1 = 1.0 / (1.0 + e2)
    g2 = e2 / (1.0 + e2)
    gates = jnp.where(expert == i1, g1, jnp.where(expert == i2, g2, 0.0))
    picked = jnp.where(jnp.logical_or(expert == i1, expert == i2), 1.0, 0.0)
    tile = jnp.concatenate([gates, picked, jnp.zeros((LANES - 2 * n_experts, rows), F32)], axis=0)
    return tile.T


def _mix_out_kernel(*refs, d_pool, natural_x, n_experts):
    if n_experts:
        (x_ref, ya_ref, o1_ref, o2_ref, o3_ref, l1_ref, l2_ref, l3_ref, gain_ref, wo_ref, gffn_ref, exp_ref,
         wr_ref, x1_ref, h2_ref, route_ref) = refs
    else:
        (x_ref, ya_ref, o1_ref, o2_ref, o3_ref, l1_ref, l2_ref, l3_ref, gain_ref, wo_ref, gffn_ref, exp_ref,
         x1_ref, h2_ref) = refs
    lses = [_token_rows(l) for l in (l1_ref, l2_ref, l3_ref)]
    top = jnp.maximum(jnp.maximum(lses[0], lses[1]), lses[2])
    es = [jnp.exp2(l - top) for l in lses]
    den = es[0] + es[1] + es[2]
    expand = exp_ref[...]

    def per_lane(w):
        hi = w.astype(BF16)
        lo = (w - hi.astype(F32)).astype(BF16)
        return (jnp.dot(hi, expand, preferred_element_type=F32)
                + jnp.dot(lo, expand, preferred_element_type=F32))

    o = 0.0
    for e, o_ref in zip(es, (o1_ref, o2_ref, o3_ref)):
        o = o + per_lane(e / den) * _token_rows(o_ref).astype(F32)
    yb = _rms(o, gain_ref[...]).astype(BF16)
    y = (jnp.dot(_token_rows(ya_ref), wo_ref[:d_pool, :], preferred_element_type=F32)
         + jnp.dot(yb, wo_ref[d_pool:, :], preferred_element_type=F32))
    x1 = _token_rows(x_ref, natural_x) + y
    x1_ref[...] = x1.reshape(x1_ref.shape)
    h2 = _rms(x1, gffn_ref[...])
    if n_experts:
        route_ref[...] = _route(h2, wr_ref[...], n_experts).reshape(route_ref.shape)
        h2_ref[...] = _pack_halves(h2).reshape(h2_ref.shape)
    else:
        h2_ref[...] = h2.astype(h2_ref.dtype).reshape(h2_ref.shape)


def _mix_out(x, ya, outs, lses, attn_gain, w_out, g_ffn, w_router, *, natural_x, ti=32):
    B, R, SI, d_pool = ya.shape
    D = x.shape[-1]
    d_attn = outs[0].shape[-1]
    ti = min(ti, SI)
    head_of_lane = jnp.arange(d_attn) // HEAD_DIM
    expand = (jnp.arange(LANES)[:, None] == head_of_lane[None, :]).astype(BF16)
    seq = lambda c: pl.BlockSpec((None, R, ti, c), lambda b, s: (b, 0, s, 0))
    x_spec = pl.BlockSpec((None, ti, R, D), lambda b, s: (b, s, 0, 0)) if natural_x else seq(D)
    full = lambda shape: pl.BlockSpec(shape, lambda b, s: (0,) * len(shape))
    tokens = lambda c, dtype: jax.ShapeDtypeStruct((B, R, SI, c), dtype)
    in_specs = ([x_spec, seq(d_pool)] + [seq(d_attn)] * 3 + [seq(LANES)] * 3
                + [full((1, d_attn)), full(w_out.shape), full((1, D)), full(expand.shape)])
    args = [x, ya, *outs, *lses, attn_gain.reshape(1, d_attn), w_out, g_ffn.reshape(1, D), expand]
    if w_router is None:
        n_experts = 0
        out_specs, out_shape = [seq(D), seq(D)], [tokens(D, F32), tokens(D, BF16)]
    else:
        n_experts = w_router.shape[-1]
        assert 2 * n_experts <= LANES and TOP_K == 2
        wr = jnp.zeros((D, LANES), F32).at[:, :n_experts].set(w_router)
        wr_hi = wr.astype(BF16)
        in_specs.append(full((D, 2 * LANES)))
        args.append(jnp.concatenate([wr_hi, (wr - wr_hi.astype(F32)).astype(BF16)], axis=1))
        out_specs = [seq(D), seq(D // 2), seq(LANES)]
        out_shape = [tokens(D, F32), tokens(D // 2, U32), tokens(LANES, F32)]
    return pl.pallas_call(
        functools.partial(_mix_out_kernel, d_pool=d_pool, natural_x=natural_x, n_experts=n_experts),
        grid=(B, SI // ti),
        in_specs=in_specs, out_specs=out_specs, out_shape=out_shape,
        compiler_params=_params("parallel", "parallel"),
        name="mix_out",
    )(*args)


def _swiglu_hidden(h, wg, wu):
    a = jnp.dot(h, wg, preferred_element_type=F32)
    b = jnp.dot(h, wu, preferred_element_type=F32)
    return (a * jax.nn.sigmoid(a) * b).astype(BF16)


def _finish(x, y, gfin_ref):
    out = x + y
    return out if gfin_ref is None else _rms(out, gfin_ref[...])


def _ffn_kernel(*refs, final, tf):
    if final:
        h_ref, x_ref, wg_ref, wu_ref, wd_ref, gfin_ref, o_ref, acc = refs
    else:
        (h_ref, x_ref, wg_ref, wu_ref, wd_ref, o_ref, acc), gfin_ref = refs, None
    h = h_ref[...]
    for c in range(wg_ref.shape[-1] // tf):
        cols = slice(c * tf, (c + 1) * tf)
        hid = _swiglu_hidden(h, wg_ref[:, cols], wu_ref[:, cols])
        part = jnp.dot(hid, wd_ref[cols, :], preferred_element_type=F32)
        if c == 0:
            acc[...] = part
        else:
            acc[...] += part
    o_ref[...] = _finish(x_ref[...], acc[...], gfin_ref)


def _pick_chunk(n, target):
    best = None
    for c in range(LANES, min(n, target) + 1, LANES):
        if n % c == 0:
            best = c
    assert best is not None
    return best


def _ffn(h, x, wg, wu, wd, g_final, *, tm=512, tf_target=1536):
    N, D = x.shape
    F = wg.shape[-1]
    tm = min(tm, N)
    tf = _pick_chunk(F, tf_target)
    final = g_final is not None
    row = pl.BlockSpec((tm, D), lambda i: (i, 0))
    resident = lambda shape: pl.BlockSpec(shape, lambda i: (0, 0), pipeline_mode=pl.Buffered(1))
    in_specs = [row, row, resident((D, F)), resident((D, F)), resident((F, D))]
    args = [h, x, wg, wu, wd]
    if final:
        in_specs.append(pl.BlockSpec((1, D), lambda i: (0, 0)))
        args.append(g_final.reshape(1, D))
    return pl.pallas_call(
        functools.partial(_ffn_kernel, final=final, tf=tf),
        grid=(N // tm,),
        in_specs=in_specs,
        out_specs=row,
        out_shape=jax.ShapeDtypeStruct((N, D), F32),
        scratch_shapes=[pltpu.VMEM((tm, D), F32)],
        compiler_params=_params("parallel"),
        name="ffn_dense",
    )(*args)


def _plan_rows(route, n_experts, *, tm):
    N = route.shape[0]
    E = n_experts
    i32 = jnp.int32
    p_max = TOP_K * N + E * tm
    lane = jnp.arange(LANES, dtype=i32)[None, :]
    sel = jnp.where((lane >= E) & (lane < 2 * E), route > 0.5, False)
    blk = min(512, N)
    assert N % blk == 0
    blocks = sel.reshape(N // blk, blk, LANES)
    earlier = (jnp.arange(blk)[:, None] > jnp.arange(blk)[None, :]).astype(BF16)
    within = jnp.einsum("ts,bsl->btl", earlier, blocks.astype(BF16), preferred_element_type=F32)
    block_total = jnp.sum(blocks, axis=1, dtype=i32)
    block_start = jnp.cumsum(block_total, axis=0) - block_total
    rank = (within.astype(i32) + block_start[:, None, :]).reshape(N, LANES)
    seg_len = (jnp.sum(block_total, axis=0) + tm - 1) // tm * tm
    seg_end = jnp.cumsum(seg_len)
    pos = jnp.where(sel, (seg_end - seg_len)[None, :] + rank, -1)
    row_hi = jnp.max(pos, axis=1, keepdims=True)
    row_lo = jnp.sum(jnp.where(sel, pos, 0), axis=1, keepdims=True) - row_hi
    gates = jnp.roll(route, E, axis=1)
    gate_hi = jnp.sum(jnp.where(pos == row_hi, gates, 0.0), axis=1, keepdims=True)
    gate_lo = jnp.sum(jnp.where(pos == row_lo, gates, 0.0), axis=1, keepdims=True)
    pair_gates = jnp.where(lane == 0, gate_hi, jnp.where(lane == 1, gate_lo, 0.0))
    expert_end = seg_end[E:2 * E]
    tile_start = jnp.arange(p_max // tm, dtype=i32) * tm
    tile_expert = jnp.minimum(jnp.sum(expert_end[None, :] <= tile_start[:, None], axis=1), E - 1).astype(i32)
    return dict(rows=(row_hi[:, 0], row_lo[:, 0]), pair_gates=pair_gates, tile_expert=tile_expert,
                n_active_tiles=(expert_end[-1] // tm).reshape(1).astype(i32), p_max=p_max)


def _sc_rows_kernel(n_rows, n_out, d, scatter):
    per_worker = n_rows // SC_WORKERS
    assert n_rows % (SC_WORKERS * SC_WINDOW) == 0
    mesh = plsc.VectorSubcoreMesh(core_axis_name="c", subcore_axis_name="s")
    out_type = (jax.ShapeDtypeStruct((n_out, d), U32) if scatter
                else [jax.ShapeDtypeStruct((n_rows, d), U32)] * 2)

    def body(*refs):
        if scatter:
            src_hbm, hi_hbm, lo_hbm, out_hbm, idx_hi, idx_lo, rows_hi, rows_lo, sem_hi, sem_lo = refs
        else:
            src_hbm, hi_hbm, lo_hbm, out_hi_hbm, out_lo_hbm, idx_hi, idx_lo, rows_hi, rows_lo, sem_hi, sem_lo = refs
        worker = lax.axis_index("s") * SC_CORES + lax.axis_index("c")

        @pl.loop(0, per_worker // SC_WINDOW)
        def _(i):
            window = pl.ds(worker * per_worker + i * SC_WINDOW, SC_WINDOW)
            pltpu.sync_copy(hi_hbm.at[window], idx_hi)
            pltpu.sync_copy(lo_hbm.at[window], idx_lo)
            if scatter:
                pltpu.sync_copy(src_hbm.at[window], rows_hi)
                to_hi = pltpu.async_copy(rows_hi, out_hbm.at[idx_hi], sem_hi)
                to_lo = pltpu.async_copy(rows_hi, out_hbm.at[idx_lo], sem_lo)
                to_hi.wait()
                to_lo.wait()
            else:
                from_hi = pltpu.async_copy(src_hbm.at[idx_hi], rows_hi, sem_hi)
                from_lo = pltpu.async_copy(src_hbm.at[idx_lo], rows_lo, sem_lo)
                from_hi.wait()
                pltpu.sync_copy(rows_hi, out_hi_hbm.at[window])
                from_lo.wait()
                pltpu.sync_copy(rows_lo, out_lo_hbm.at[window])

    index_vec, row_buf = pltpu.VMEM((SC_WINDOW,), jnp.int32), pltpu.VMEM((SC_WINDOW, d), U32)
    return pl.kernel(body, mesh=mesh, out_type=out_type,
                     scratch_types=[index_vec, index_vec, row_buf, row_buf,
                                    pltpu.SemaphoreType.DMA, pltpu.SemaphoreType.DMA])


def _moe_experts_kernel(te_ref, na_ref, xs_ref, wg_hbm, wu_hbm, wd_hbm, o_ref,
                        cache_g, cache_u, cache_d, stage_g, stage_u, stage_d, sems, acc, *, tf):
    i = pl.program_id(0)
    e = te_ref[i]
    n_chunks = cache_g.shape[-1] // tf
    active = i < na_ref[0]
    new_expert = jnp.logical_or(i == 0, e != te_ref[jnp.maximum(i - 1, 0)])

    def chunk_copies(c, slot):
        cols = pl.ds(c * tf, tf)
        return (pltpu.make_async_copy(wg_hbm.at[e, :, cols], stage_g.at[slot], sems.at[0, slot]),
                pltpu.make_async_copy(wu_hbm.at[e, :, cols], stage_u.at[slot], sems.at[1, slot]),
                pltpu.make_async_copy(wd_hbm.at[e, cols, :], stage_d.at[slot], sems.at[2, slot]))

    def tile_ffn(load_weights):
        x = _unpack_halves(xs_ref[...]).astype(BF16)
        if load_weights:
            for cp in chunk_copies(0, 0):
                cp.start()
        for c in range(n_chunks):
            cols = slice(c * tf, (c + 1) * tf)
            if load_weights:
                slot = c % 2
                if c + 1 < n_chunks:
                    for cp in chunk_copies(c + 1, 1 - slot):
                        cp.start()
                for cp in chunk_copies(c, slot):
                    cp.wait()
                cache_g[:, cols] = stage_g[slot].astype(BF16)
                cache_u[:, cols] = stage_u[slot].astype(BF16)
                cache_d[cols, :] = stage_d[slot].astype(BF16)
            hid = _swiglu_hidden(x, cache_g[:, cols], cache_u[:, cols])
            part = jnp.dot(hid, cache_d[cols, :], preferred_element_type=F32)
            if c == 0:
                acc[...] = part
            else:
                acc[...] += part
        o_ref[...] = _pack_halves(acc[...])

    pl.when(jnp.logical_and(active, new_expert))(lambda: tile_ffn(True))
    pl.when(jnp.logical_and(active, jnp.logical_not(new_expert)))(lambda: tile_ffn(False))


def _moe_experts(xs, plan, wg, wu, wd, *, tm, tf_target=512):
    P = xs.shape[0]
    E, D, F = wg.shape
    tf = _pick_chunk(F, tf_target)
    tile = lambda i, te, na: (jnp.minimum(i, na[0] - 1), 0)
    in_hbm = pl.BlockSpec(memory_space=pl.ANY)
    grid_spec = pltpu.PrefetchScalarGridSpec(
        num_scalar_prefetch=2,
        grid=(P // tm,),
        in_specs=[pl.BlockSpec((tm, D // 2), tile), in_hbm, in_hbm, in_hbm],
        out_specs=pl.BlockSpec((tm, D // 2), tile),
        scratch_shapes=[pltpu.VMEM((D, F), BF16), pltpu.VMEM((D, F), BF16), pltpu.VMEM((F, D), BF16),
                        pltpu.VMEM((2, D, tf), F32), pltpu.VMEM((2, D, tf), F32), pltpu.VMEM((2, tf, D), F32),
                        pltpu.SemaphoreType.DMA((3, 2)), pltpu.VMEM((tm, D), F32)],
    )
    return pl.pallas_call(
        functools.partial(_moe_experts_kernel, tf=tf),
        grid_spec=grid_spec,
        out_shape=jax.ShapeDtypeStruct((P, D // 2), U32),
        compiler_params=_params("arbitrary", vmem_limit=MOE_VMEM_LIMIT),
        name="moe_experts",
    )(plan["tile_expert"], plan["n_active_tiles"], xs, wg, wu, wd)


def _moe_mix_kernel(*refs, final, natural_out):
    if final:
        x_ref, yh_ref, yl_ref, g_ref, gfin_ref, o_ref = refs
    else:
        (x_ref, yh_ref, yl_ref, g_ref, o_ref), gfin_ref = refs, None
    gates = _token_rows(g_ref)
    lane = lax.broadcasted_iota(jnp.int32, gates.shape, 1)
    g_hi = jnp.sum(jnp.where(lane == 0, gates, 0.0), axis=-1, keepdims=True)
    g_lo = jnp.sum(jnp.where(lane == 1, gates, 0.0), axis=-1, keepdims=True)
    y = g_hi * _unpack_halves(_token_rows(yh_ref)) + g_lo * _unpack_halves(_token_rows(yl_ref))
    out = _finish(_token_rows(x_ref), y, gfin_ref).reshape(x_ref.shape)
    o_ref[...] = pltpu.einshape("rid->ird", out) if natural_out else out


def _moe_mix(x, y_hi, y_lo, pair_gates, g_final, *, natural_out, ti=32):
    B, R, SI, D = x.shape
    ti = min(ti, SI)
    final = g_final is not None
    seq = lambda c: pl.BlockSpec((None, R, ti, c), lambda b, s: (b, 0, s, 0))
    as_tokens = lambda t: t.reshape(B, R, SI, t.shape[-1])
    in_specs = [seq(D), seq(D // 2), seq(D // 2), seq(LANES)]
    args = [x, as_tokens(y_hi), as_tokens(y_lo), as_tokens(pair_gates)]
    if final:
        in_specs.append(pl.BlockSpec((1, D), lambda b, s: (0, 0)))
        args.append(g_final.reshape(1, D))
    out_spec = pl.BlockSpec((None, ti, R, D), lambda b, s: (b, s, 0, 0)) if natural_out else seq(D)
    return pl.pallas_call(
        functools.partial(_moe_mix_kernel, final=final, natural_out=natural_out),
        grid=(B, SI // ti), in_specs=in_specs, out_specs=out_spec,
        out_shape=jax.ShapeDtypeStruct((B, SI, R, D) if natural_out else (B, R, SI, D), F32),
        compiler_params=_params("parallel", "parallel"),
        name="moe_mix",
    )(*args)


def _moe(h, x, route, wg, wu, wd, g_final, *, natural_out, tm=512):
    N, half = h.shape
    plan = _plan_rows(route, wg.shape[0], tm=tm)
    rows_hi, rows_lo = plan["rows"]
    xs = _sc_rows_kernel(N, plan["p_max"], half, scatter=True)(h, rows_hi, rows_lo)
    ys = _moe_experts(xs, plan, wg, wu, wd, tm=tm)
    y_hi, y_lo = _sc_rows_kernel(N, plan["p_max"], half, scatter=False)(ys, rows_hi, rows_lo)
    return _moe_mix(x, y_hi, y_lo, plan["pair_gates"], g_final, natural_out=natural_out)


def kernel(x, norm_mix, w_in, w_pool, pool_scale, attn_gain, w_out, norm_ffn, ffn_wg, ffn_wu, ffn_wd,
           w_router, moe_wg, moe_wu, moe_wd, final_norm):
    B, S, D = x.shape
    depth = norm_mix.shape[0]
    bf = lambda t: t.astype(BF16)
    w_in, w_pool, w_out = bf(w_in), bf(w_pool), bf(w_out)
    ffn_wg, ffn_wu, ffn_wd = bf(ffn_wg), bf(ffn_wu), bf(ffn_wd)
    R = RESIDUES
    assert S % R == 0
    N, SI = B * S, S // R
    x = x.reshape(B, SI, R, D)
    for l in range(depth):
        first, last = l == 0, l == depth - 1
        ya, q, k, v, q32, k32, v32 = _mix_in(x, norm_mix[l], w_in[l], w_pool[l], pool_scale[l], natural_x=first)
        narrow = lambda dil: (R // dil) * 16 > WINDOW_STEPS
        branches = [_attn_branch(q32, k32, v32, dil) if narrow(dil) else _attn_branch(q, k, v, dil)
                    for _, dil in DILATED_PATTERNS]
        i, routed = l // 2, l % 2 == 1
        x1, h2, *route = _mix_out(x, ya, [o for o, _ in branches], [lse for _, lse in branches],
                                  attn_gain[l], w_out[l], norm_ffn[l], w_router[i] if routed else None,
                                  natural_x=first)
        g_final = final_norm if last else None
        if routed:
            x = _moe(h2.reshape(N, D // 2), x1, route[0].reshape(N, LANES), moe_wg[i], moe_wu[i], moe_wd[i],
                     g_final, natural_out=last)
        else:
            x = _ffn(h2.reshape(N, D), x1.reshape(N, D), ffn_wg[i], ffn_wu[i], ffn_wd[i], g_final)
            x = x.reshape(B, R, SI, D)
            if last:
                x = x.transpose(0, 2, 1, 3)
    return x.reshape(B, S, D)
```

```python
import functools

import jax
import jax.numpy as jnp
import numpy as np
from jax import lax
from jax.experimental import pallas as pl
from jax.experimental.pallas import tpu as pltpu
from jax.experimental.pallas import tpu_sc as plsc

F32 = jnp.float32
BF16 = jnp.bfloat16
U32 = jnp.uint32
HIGH_HALF = np.uint32(0xFFFF0000)

EPS = 1e-6
LANES = 128
HEAD_DIM = 64
POOL_WINDOWS = (2, 4, 8, 16)
POOL_HIST = 8
DILATED_PATTERNS = ((128, 1), (512, 4), (2048, 16))
WINDOW_STEPS = 128
RESIDUES = 16
ATTN_STEP_ROWS = 1024
ATTN_UNROLL_QUERIES = 512
LOG2_E = 1.4426950408889634
SC_CORES, SC_SUBCORES = 2, 16
SC_WORKERS = SC_CORES * SC_SUBCORES
SC_WINDOW = 64
assert all(w // d == WINDOW_STEPS and RESIDUES % d == 0 for w, d in DILATED_PATTERNS)
TOP_K = 2
MASKED = -1e30
VMEM_LIMIT = 48 * 1024 * 1024
MOE_VMEM_LIMIT = 56 * 1024 * 1024


def _rms(x, g):
    return x * lax.rsqrt(jnp.mean(x * x, axis=-1, keepdims=True) + EPS) * g


def _params(*sem, vmem_limit=VMEM_LIMIT):
    return pltpu.CompilerParams(dimension_semantics=sem, vmem_limit_bytes=vmem_limit)


def _token_rows(ref, natural=False):
    t = ref[...]
    if natural:
        t = pltpu.einshape("ird->rid", t)
    return t.reshape(-1, t.shape[-1])


def _pack_halves(t):
    bits = lax.bitcast_convert_type(t, U32)
    bits = (bits + np.uint32(0x7FFF) + ((bits >> 16) & np.uint32(1))) & HIGH_HALF
    half = t.shape[-1] // 2
    return (bits[:, :half] >> 16) | bits[:, half:]


def _unpack_halves(words):
    halves = [lax.bitcast_convert_type(w, F32) for w in (words << 16, words & HIGH_HALF)]
    return jnp.concatenate(halves, axis=-1)


def _mix_in_kernel(x_ref, g_ref, w_ref, wp_ref, ps_ref, ya_ref, q_ref, k_ref, v_ref, q32_ref, k32_ref, v32_ref,
                   ubuf, uprev, *, d_pool, d_attn, ti, natural_x):
    s = pl.program_id(1)
    R = RESIDUES
    blk = lambda t: t.reshape(R, ti, t.shape[-1])
    h = _rms(_token_rows(x_ref, natural_x), g_ref[...]).astype(BF16)
    u = jnp.dot(h, w_ref[:, :d_pool], preferred_element_type=F32)
    scale = HEAD_DIM ** -0.5 * LOG2_E
    for n, (ref, ref32) in enumerate(zip((q_ref, k_ref, v_ref), (q32_ref, k32_ref, v32_ref))):
        t = jnp.dot(h, w_ref[:, d_pool + n * d_attn:d_pool + (n + 1) * d_attn], preferred_element_type=F32)
        t = t * scale if n == 0 else t
        ref[...] = blk(t.astype(BF16))
        ref32[...] = blk(t)

    history = ubuf[:, POOL_HIST - 1:POOL_HIST, :]
    ubuf[:, POOL_HIST - 1:POOL_HIST, :] = jnp.where(s == 0, 0.0, history)
    ubuf[:, POOL_HIST:POOL_HIST + ti, :] = blk(u)
    uprev[...] = ubuf[:, POOL_HIST - 1:POOL_HIST - 1 + ti, :]
    at_start = (s * ti + lax.broadcasted_iota(jnp.int32, (ti, 1), 0)) == 0
    group = d_pool // len(POOL_WINDOWS)
    zs = []
    for gi, w in enumerate(POOL_WINDOWS):
        cols = slice(gi * group, (gi + 1) * group)
        ds = []
        for r in range(R):
            ug = ubuf[r, POOL_HIST:POOL_HIST + ti, cols]
            win = ug
            for back in range(1, w):
                rr = r - back
                win = win + (ubuf[rr, POOL_HIST:POOL_HIST + ti, cols] if rr >= 0 else uprev[rr + R, :, cols])
            cnt = jnp.where(at_start, float(min(r + 1, w)), float(w))
            ds.append(win / cnt - ug)
        d = jnp.concatenate(ds, axis=0).astype(BF16)
        zs.append(jnp.dot(d, wp_ref[gi], preferred_element_type=F32))
    z = jnp.concatenate(zs, axis=-1)
    ya_ref[...] = blk(_rms(z, ps_ref[...]).astype(BF16))
    ubuf[:, POOL_HIST - 1:POOL_HIST, :] = ubuf[:, POOL_HIST + ti - 1:POOL_HIST + ti, :]


def _mix_in(x, g, w_in, w_pool, pool_scale, *, natural_x, ti=32):
    B, R, SI, D = x.shape
    if natural_x:
        R, SI = SI, R
    d_pool = pool_scale.shape[-1]
    d_in = w_in.shape[-1]
    d_attn = (d_in - d_pool) // 3
    ti = min(ti, SI)
    assert R == RESIDUES >= max(POOL_WINDOWS) and SI % ti == 0 and ti % 16 == 0
    assert d_pool % (LANES * len(POOL_WINDOWS)) == 0
    seq_spec = lambda c: pl.BlockSpec((None, R, ti, c), lambda b, s: (b, 0, s, 0))
    full = lambda shape: pl.BlockSpec(shape, lambda b, s: (0,) * len(shape))
    out_sds = lambda c, dtype=BF16: jax.ShapeDtypeStruct((B, R, SI, c), dtype)
    return pl.pallas_call(
        functools.partial(_mix_in_kernel, d_pool=d_pool, d_attn=d_attn, ti=ti, natural_x=natural_x),
        grid=(B, SI // ti),
        in_specs=[pl.BlockSpec((None, ti, R, D), lambda b, s: (b, s, 0, 0)) if natural_x else seq_spec(D),
                  full((1, D)), full((D, d_in)), full(w_pool.shape), full((1, d_pool))],
        out_specs=[seq_spec(d_pool)] + [seq_spec(d_attn)] * 6,
        out_shape=[out_sds(d_pool)] + [out_sds(d_attn)] * 3 + [out_sds(d_attn, F32)] * 3,
        scratch_shapes=[pltpu.VMEM((R, POOL_HIST + ti, d_pool), F32), pltpu.VMEM((R, ti, d_pool), F32)],
        compiler_params=_params("parallel", "arbitrary"),
        name="mix_in",
    )(x, g.reshape(1, D), w_in, w_pool, pool_scale.reshape(1, d_pool))


def _attn_kernel(q_ref, k_ref, v_ref, o_ref, lse_ref, bias_ref, *, nq, strips, d_attn):
    _, classes, chunk, _ = q_ref.shape
    L = nq // strips
    nk, n_blocks = 2 * nq, chunk // L
    first_block = pl.program_id(2) * n_blocks
    row = lax.broadcasted_iota(jnp.int32, (nq, 1), 0)
    col = lax.broadcasted_iota(jnp.int32, (1, nk), 1)
    q_strip, q_row = row >> (L.bit_length() - 1), row & (L - 1)
    k_strip, k_row = col >> ((2 * L).bit_length() - 1), col & (2 * L - 1)
    back = strips * (q_row - k_row) + (q_strip - k_strip)

    def band(offset):
        rel = back + strips * offset
        return jnp.where((rel >= 0) & (rel <= WINDOW_STEPS), 0.0, MASKED).astype(F32)

    bias_ref[0] = band(0)
    bias_ref[1] = band(L)
    lane = lax.broadcasted_iota(jnp.int32, (nq, LANES), 1)
    lo_half = lane < HEAD_DIM

    def block(it, carry):
        g, j_local = it // n_blocks, it % n_blocks
        j = first_block + j_local
        q_rows = pl.ds(pl.multiple_of(j_local * L, L), L)
        k_rows = pl.ds(pl.multiple_of(jnp.maximum(j - 1, 0) * L, L), 2 * L)
        bias = bias_ref[jnp.minimum(j, 1)]
        lse_tile = jnp.zeros((nq, LANES), F32)
        for hp in range(d_attn // LANES):
            cols = slice(hp * LANES, (hp + 1) * LANES)
            qp = q_ref[:, g, q_rows, cols].reshape(nq, LANES).astype(BF16)
            kk = k_ref[:, g, k_rows, cols].reshape(nk, LANES).astype(BF16)
            vv = v_ref[:, g, k_rows, cols].reshape(nk, LANES).astype(BF16)
            outs, lses = [], []
            for sub in range(2):
                keep = lo_half if sub == 0 else jnp.logical_not(lo_half)
                qm = jnp.where(keep, qp, jnp.zeros_like(qp))
                s = lax.dot_general(qm, kk, (((1,), (1,)), ((), ())), preferred_element_type=F32) + bias
                m = jnp.max(s, axis=-1, keepdims=True)
                p = jnp.exp2(s - m)
                l = jnp.sum(p, axis=-1, keepdims=True)
                o = jnp.dot(p.astype(BF16), vv, preferred_element_type=F32)
                outs.append(o / l)
                lses.append(m + jnp.log2(l))
            o_pair = jnp.where(lo_half, outs[0], outs[1]).astype(o_ref.dtype)
            o_ref[:, g, q_rows, cols] = o_pair.reshape(strips, L, LANES)
            pair_lse = jnp.where(lane == 2 * hp, lses[0], lses[1])
            lse_tile = jnp.where((lane >> 1) == hp, pair_lse, lse_tile)
        lse_ref[:, g, q_rows, :] = lse_tile.reshape(strips, L, LANES)
        return carry

    lax.fori_loop(0, classes * n_blocks, block, 0, unroll=ATTN_UNROLL_QUERIES // nq)


def _attn_branch(q, k, v, dil):
    B, R, SI, C = q.shape
    strips = R // dil
    row_tile = 8 * 4 // q.dtype.itemsize
    nq = max(WINDOW_STEPS, strips * row_tile)
    L = nq // strips
    classes = max(1, min(dil, ATTN_STEP_ROWS // (strips * SI)))
    chunk = min(SI, max(L, ATTN_STEP_ROWS // (strips * classes)))
    assert R % dil == 0 and SI % chunk == 0 and chunk % L == 0 and SI >= 2 * L and dil % classes == 0
    assert C % LANES == 0 and C // HEAD_DIM <= LANES and nq & (nq - 1) == 0 and L & (L - 1) == 0
    view = lambda t: t.reshape(B, strips, dil, SI, t.shape[-1])
    q_spec = lambda c: pl.BlockSpec((None, strips, classes, chunk, c), lambda b, g, s: (b, 0, g, s, 0))
    kv_spec = pl.BlockSpec((None, strips, classes, SI, C), lambda b, g, s: (b, 0, g, 0, 0),
                           pipeline_mode=pl.Buffered(1 if SI > chunk else 2))
    o, lse = pl.pallas_call(
        functools.partial(_attn_kernel, nq=nq, strips=strips, d_attn=C),
        scratch_shapes=[pltpu.VMEM((2, nq, 2 * nq), F32)],
        grid=(B, dil // classes, SI // chunk),
        in_specs=[q_spec(C), kv_spec, kv_spec],
        out_specs=[q_spec(C), q_spec(LANES)],
        out_shape=[jax.ShapeDtypeStruct((B, strips, dil, SI, C), q.dtype),
                   jax.ShapeDtypeStruct((B, strips, dil, SI, LANES), F32)],
        compiler_params=_params("parallel", "parallel", "arbitrary"),
        name=f"attn_d{dil}",
    )(view(q), view(k), view(v))
    return o.reshape(B, R, SI, C), lse.reshape(B, R, SI, LANES)


def _route(h, wr, n_experts):
    h_hi = h.astype(BF16)
    h_lo = (h - h_hi.astype(F32)).astype(BF16)
    both = jnp.dot(h_hi, wr, preferred_element_type=F32)
    logits = both[:, :LANES] + both[:, LANES:] + jnp.dot(h_lo, wr[:, :LANES], preferred_element_type=F32)
    rows = h.shape[0]
    scores = logits.T[:n_experts]
    expert = lax.broadcasted_iota(jnp.int32, scores.shape, 0)
    picks = []
    for _ in range(TOP_K):
        m = jnp.max(scores, axis=0, keepdims=True)
        idx = jnp.min(jnp.where(scores == m, expert, n_experts), axis=0, keepdims=True)
        picks.append((m, idx))
        scores = jnp.where(expert == idx, -jnp.inf, scores)
    (m1, i1), (m2, i2) = picks
    e2 = jnp.exp(m2 - m1)
    g1 = 1.0 / (1.0 + e2)
    g2 = e2 / (1.0 + e2)
    gates = jnp.where(expert == i1, g1, jnp.where(expert == i2, g2, 0.0))
    picked = jnp.where(jnp.logical_or(expert == i1, expert == i2), 1.0, 0.0)
    tile = jnp.concatenate([gates, picked, jnp.zeros((LANES - 2 * n_experts, rows), F32)], axis=0)
    return tile.T


def _mix_out_kernel(*refs, d_pool, natural_x, n_experts):
    if n_experts:
        (x_ref, ya_ref, o1_ref, o2_ref, o3_ref, l1_ref, l2_ref, l3_ref, gain_ref, wo_ref, gffn_ref, exp_ref,
         wr_ref, x1_ref, h2_ref, route_ref) = refs
    else:
        (x_ref, ya_ref, o1_ref, o2_ref, o3_ref, l1_ref, l2_ref, l3_ref, gain_ref, wo_ref, gffn_ref, exp_ref,
         x1_ref, h2_ref) = refs
    lses = [_token_rows(l) for l in (l1_ref, l2_ref, l3_ref)]
    top = jnp.maximum(jnp.maximum(lses[0], lses[1]), lses[2])
    es = [jnp.exp2(l - top) for l in lses]
    den = es[0] + es[1] + es[2]
    expand = exp_ref[...]

    def per_lane(w):
        hi = w.astype(BF16)
        lo = (w - hi.astype(F32)).astype(BF16)
        return jnp.dot(jnp.concatenate([hi, lo], axis=1), expand, preferred_element_type=F32)

    o = 0.0
    for e, o_ref in zip(es, (o1_ref, o2_ref, o3_ref)):
        o = o + per_lane(e / den) * _token_rows(o_ref).astype(F32)
    yb = _rms(o, gain_ref[...]).astype(BF16)
    y = (jnp.dot(_token_rows(ya_ref), wo_ref[:d_pool, :], preferred_element_type=F32)
         + jnp.dot(yb, wo_ref[d_pool:, :], preferred_element_type=F32))
    x1 = _token_rows(x_ref, natural_x) + y
    x1_ref[...] = x1.reshape(x1_ref.shape)
    h2 = _rms(x1, gffn_ref[...])
    if n_experts:
        route_ref[...] = _route(h2, wr_ref[...], n_experts).reshape(route_ref.shape)
        h2_ref[...] = _pack_halves(h2).reshape(h2_ref.shape)
    else:
        h2_ref[...] = h2.astype(h2_ref.dtype).reshape(h2_ref.shape)


def _mix_out(x, ya, outs, lses, attn_gain, w_out, g_ffn, w_router, *, natural_x, ti=32):
    B, R, SI, d_pool = ya.shape
    D = x.shape[-1]
    d_attn = outs[0].shape[-1]
    ti = min(ti, SI)
    head_of_lane = jnp.arange(d_attn) // HEAD_DIM
    expand = (jnp.arange(2 * LANES)[:, None] % LANES == head_of_lane[None, :]).astype(BF16)
    seq = lambda c: pl.BlockSpec((None, R, ti, c), lambda b, s: (b, 0, s, 0))
    x_spec = pl.BlockSpec((None, ti, R, D), lambda b, s: (b, s, 0, 0)) if natural_x else seq(D)
    full = lambda shape: pl.BlockSpec(shape, lambda b, s: (0,) * len(shape))
    tokens = lambda c, dtype: jax.ShapeDtypeStruct((B, R, SI, c), dtype)
    in_specs = ([x_spec, seq(d_pool)] + [seq(d_attn)] * 3 + [seq(LANES)] * 3
                + [full((1, d_attn)), full(w_out.shape), full((1, D)), full(expand.shape)])
    args = [x, ya, *outs, *lses, attn_gain.reshape(1, d_attn), w_out, g_ffn.reshape(1, D), expand]
    if w_router is None:
        n_experts = 0
        out_specs, out_shape = [seq(D), seq(D)], [tokens(D, F32), tokens(D, BF16)]
    else:
        n_experts = w_router.shape[-1]
        assert 2 * n_experts <= LANES and TOP_K == 2
        wr = jnp.zeros((D, LANES), F32).at[:, :n_experts].set(w_router)
        wr_hi = wr.astype(BF16)
        in_specs.append(full((D, 2 * LANES)))
        args.append(jnp.concatenate([wr_hi, (wr - wr_hi.astype(F32)).astype(BF16)], axis=1))
        out_specs = [seq(D), seq(D // 2), seq(LANES)]
        out_shape = [tokens(D, F32), tokens(D // 2, U32), tokens(LANES, F32)]
    return pl.pallas_call(
        functools.partial(_mix_out_kernel, d_pool=d_pool, natural_x=natural_x, n_experts=n_experts),
        grid=(B, SI // ti),
        in_specs=in_specs, out_specs=out_specs, out_shape=out_shape,
        compiler_params=_params("parallel", "parallel"),
        name="mix_out",
    )(*args)


def _swiglu_hidden(h, wg, wu):
    a = jnp.dot(h, wg, preferred_element_type=F32)
    b = jnp.dot(h, wu, preferred_element_type=F32)
    return (a * jax.nn.sigmoid(a) * b).astype(BF16)


def _finish(x, y, gfin_ref):
    out = x + y
    return out if gfin_ref is None else _rms(out, gfin_ref[...])


def _ffn_kernel(*refs, final, tf):
    if final:
        h_ref, x_ref, wg_ref, wu_ref, wd_ref, gfin_ref, o_ref, acc = refs
    else:
        (h_ref, x_ref, wg_ref, wu_ref, wd_ref, o_ref, acc), gfin_ref = refs, None
    h = h_ref[...]
    for c in range(wg_ref.shape[-1] // tf):
        cols = slice(c * tf, (c + 1) * tf)
        hid = _swiglu_hidden(h, wg_ref[:, cols], wu_ref[:, cols])
        part = jnp.dot(hid, wd_ref[cols, :], preferred_element_type=F32)
        if c == 0:
            acc[...] = part
        else:
            acc[...] += part
    o_ref[...] = _finish(x_ref[...], acc[...], gfin_ref)


def _pick_chunk(n, target):
    best = None
    for c in range(LANES, min(n, target) + 1, LANES):
        if n % c == 0:
            best = c
    assert best is not None
    return best


def _ffn(h, x, wg, wu, wd, g_final, *, tm=512, tf_target=1536):
    N, D = x.shape
    F = wg.shape[-1]
    tm = min(tm, N)
    tf = _pick_chunk(F, tf_target)
    final = g_final is not None
    row = pl.BlockSpec((tm, D), lambda i: (i, 0))
    resident = lambda shape: pl.BlockSpec(shape, lambda i: (0, 0), pipeline_mode=pl.Buffered(1))
    in_specs = [row, row, resident((D, F)), resident((D, F)), resident((F, D))]
    args = [h, x, wg, wu, wd]
    if final:
        in_specs.append(pl.BlockSpec((1, D), lambda i: (0, 0)))
        args.append(g_final.reshape(1, D))
    return pl.pallas_call(
        functools.partial(_ffn_kernel, final=final, tf=tf),
        grid=(N // tm,),
        in_specs=in_specs,
        out_specs=row,
        out_shape=jax.ShapeDtypeStruct((N, D), F32),
        scratch_shapes=[pltpu.VMEM((tm, D), F32)],
        compiler_params=_params("parallel"),
        name="ffn_dense",
    )(*args)


def _plan_rows(route, n_experts, *, tm):
    N = route.shape[0]
    E = n_experts
    i32 = jnp.int32
    p_max = TOP_K * N + E * tm
    lane = jnp.arange(LANES, dtype=i32)[None, :]
    sel = jnp.where((lane >= E) & (lane < 2 * E), route > 0.5, False)
    blk = min(512, N)
    assert N % blk == 0
    blocks = sel.reshape(N // blk, blk, LANES)
    earlier = (jnp.arange(blk)[:, None] > jnp.arange(blk)[None, :]).astype(BF16)
    within = jnp.einsum("ts,bsl->btl", earlier, blocks.astype(BF16), preferred_element_type=F32)
    block_total = jnp.sum(blocks, axis=1, dtype=i32)
    block_start = jnp.cumsum(block_total, axis=0) - block_total
    rank = (within.astype(i32) + block_start[:, None, :]).reshape(N, LANES)
    seg_len = (jnp.sum(block_total, axis=0) + tm - 1) // tm * tm
    seg_end = jnp.cumsum(seg_len)
    pos = jnp.where(sel, (seg_end - seg_len)[None, :] + rank, -1)
    row_hi = jnp.max(pos, axis=1, keepdims=True)
    row_lo = jnp.sum(jnp.where(sel, pos, 0), axis=1, keepdims=True) - row_hi
    gates = jnp.roll(route, E, axis=1)
    gate_hi = jnp.sum(jnp.where(pos == row_hi, gates, 0.0), axis=1, keepdims=True)
    gate_lo = jnp.sum(jnp.where(pos == row_lo, gates, 0.0), axis=1, keepdims=True)
    pair_gates = jnp.where(lane == 0, gate_hi, jnp.where(lane == 1, gate_lo, 0.0))
    expert_end = seg_end[E:2 * E]
    tile_start = jnp.arange(p_max // tm, dtype=i32) * tm
    tile_expert = jnp.minimum(jnp.sum(expert_end[None, :] <= tile_start[:, None], axis=1), E - 1).astype(i32)
    return dict(rows=(row_hi[:, 0], row_lo[:, 0]), pair_gates=pair_gates, tile_expert=tile_expert,
                n_active_tiles=(expert_end[-1] // tm).reshape(1).astype(i32), p_max=p_max)


def _sc_rows_kernel(n_rows, n_out, d, scatter):
    per_worker = n_rows // SC_WORKERS
    assert n_rows % (SC_WORKERS * SC_WINDOW) == 0
    mesh = plsc.VectorSubcoreMesh(core_axis_name="c", subcore_axis_name="s")
    out_type = (jax.ShapeDtypeStruct((n_out, d), U32) if scatter
                else [jax.ShapeDtypeStruct((n_rows, d), U32)] * 2)

    def body(*refs):
        if scatter:
            src_hbm, hi_hbm, lo_hbm, out_hbm, idx_hi, idx_lo, rows_hi, rows_lo, sem_hi, sem_lo = refs
        else:
            src_hbm, hi_hbm, lo_hbm, out_hi_hbm, out_lo_hbm, idx_hi, idx_lo, rows_hi, rows_lo, sem_hi, sem_lo = refs
        worker = lax.axis_index("s") * SC_CORES + lax.axis_index("c")

        @pl.loop(0, per_worker // SC_WINDOW)
        def _(i):
            window = pl.ds(worker * per_worker + i * SC_WINDOW, SC_WINDOW)
            pltpu.sync_copy(hi_hbm.at[window], idx_hi)
            pltpu.sync_copy(lo_hbm.at[window], idx_lo)
            if scatter:
                pltpu.sync_copy(src_hbm.at[window], rows_hi)
                to_hi = pltpu.async_copy(rows_hi, out_hbm.at[idx_hi], sem_hi)
                to_lo = pltpu.async_copy(rows_hi, out_hbm.at[idx_lo], sem_lo)
                to_hi.wait()
                to_lo.wait()
            else:
                from_hi = pltpu.async_copy(src_hbm.at[idx_hi], rows_hi, sem_hi)
                from_lo = pltpu.async_copy(src_hbm.at[idx_lo], rows_lo, sem_lo)
                from_hi.wait()
                pltpu.sync_copy(rows_hi, out_hi_hbm.at[window])
                from_lo.wait()
                pltpu.sync_copy(rows_lo, out_lo_hbm.at[window])

    index_vec, row_buf = pltpu.VMEM((SC_WINDOW,), jnp.int32), pltpu.VMEM((SC_WINDOW, d), U32)
    return pl.kernel(body, mesh=mesh, out_type=out_type,
                     scratch_types=[index_vec, index_vec, row_buf, row_buf,
                                    pltpu.SemaphoreType.DMA, pltpu.SemaphoreType.DMA])


def _moe_experts_kernel(te_ref, na_ref, xs_ref, wg_hbm, wu_hbm, wd_hbm, o_ref,
                        cache_g, cache_u, cache_d, stage_g, stage_u, stage_d, sems, acc, *, tf):
    i = pl.program_id(0)
    e = te_ref[i]
    n_chunks = cache_g.shape[-1] // tf
    active = i < na_ref[0]
    new_expert = jnp.logical_or(i == 0, e != te_ref[jnp.maximum(i - 1, 0)])

    def chunk_copies(c, slot):
        cols = pl.ds(c * tf, tf)
        return (pltpu.make_async_copy(wg_hbm.at[e, :, cols], stage_g.at[slot], sems.at[0, slot]),
                pltpu.make_async_copy(wu_hbm.at[e, :, cols], stage_u.at[slot], sems.at[1, slot]),
                pltpu.make_async_copy(wd_hbm.at[e, cols, :], stage_d.at[slot], sems.at[2, slot]))

    def tile_ffn(load_weights):
        x = _unpack_halves(xs_ref[...]).astype(BF16)
        if load_weights:
            for cp in chunk_copies(0, 0):
                cp.start()
        for c in range(n_chunks):
            cols = slice(c * tf, (c + 1) * tf)
            if load_weights:
                slot = c % 2
                if c + 1 < n_chunks:
                    for cp in chunk_copies(c + 1, 1 - slot):
                        cp.start()
                for cp in chunk_copies(c, slot):
                    cp.wait()
                cache_g[:, cols] = stage_g[slot].astype(BF16)
                cache_u[:, cols] = stage_u[slot].astype(BF16)
                cache_d[cols, :] = stage_d[slot].astype(BF16)
            hid = _swiglu_hidden(x, cache_g[:, cols], cache_u[:, cols])
            part = jnp.dot(hid, cache_d[cols, :], preferred_element_type=F32)
            if c == 0:
                acc[...] = part
            else:
                acc[...] += part
        o_ref[...] = _pack_halves(acc[...])

    pl.when(jnp.logical_and(active, new_expert))(lambda: tile_ffn(True))
    pl.when(jnp.logical_and(active, jnp.logical_not(new_expert)))(lambda: tile_ffn(False))


def _moe_experts(xs, plan, wg, wu, wd, *, tm, tf_target=512):
    P = xs.shape[0]
    E, D, F = wg.shape
    tf = _pick_chunk(F, tf_target)
    tile = lambda i, te, na: (jnp.minimum(i, na[0] - 1), 0)
    in_hbm = pl.BlockSpec(memory_space=pl.ANY)
    grid_spec = pltpu.PrefetchScalarGridSpec(
        num_scalar_prefetch=2,
        grid=(P // tm,),
        in_specs=[pl.BlockSpec((tm, D // 2), tile), in_hbm, in_hbm, in_hbm],
        out_specs=pl.BlockSpec((tm, D // 2), tile),
        scratch_shapes=[pltpu.VMEM((D, F), BF16), pltpu.VMEM((D, F), BF16), pltpu.VMEM((F, D), BF16),
                        pltpu.VMEM((2, D, tf), F32), pltpu.VMEM((2, D, tf), F32), pltpu.VMEM((2, tf, D), F32),
                        pltpu.SemaphoreType.DMA((3, 2)), pltpu.VMEM((tm, D), F32)],
    )
    return pl.pallas_call(
        functools.partial(_moe_experts_kernel, tf=tf),
        grid_spec=grid_spec,
        out_shape=jax.ShapeDtypeStruct((P, D // 2), U32),
        compiler_params=_params("arbitrary", vmem_limit=MOE_VMEM_LIMIT),
        name="moe_experts",
    )(plan["tile_expert"], plan["n_active_tiles"], xs, wg, wu, wd)


def _moe_mix_kernel(*refs, final, natural_out):
    if final:
        x_ref, yh_ref, yl_ref, g_ref, gfin_ref, o_ref = refs
    else:
        (x_ref, yh_ref, yl_ref, g_ref, o_ref), gfin_ref = refs, None
    gates = _token_rows(g_ref)
    lane = lax.broadcasted_iota(jnp.int32, gates.shape, 1)
    g_hi = jnp.sum(jnp.where(lane == 0, gates, 0.0), axis=-1, keepdims=True)
    g_lo = jnp.sum(jnp.where(lane == 1, gates, 0.0), axis=-1, keepdims=True)
    y = g_hi * _unpack_halves(_token_rows(yh_ref)) + g_lo * _unpack_halves(_token_rows(yl_ref))
    out = _finish(_token_rows(x_ref), y, gfin_ref).reshape(x_ref.shape)
    o_ref[...] = pltpu.einshape("rid->ird", out) if natural_out else out


def _moe_mix(x, y_hi, y_lo, pair_gates, g_final, *, natural_out, ti=32):
    B, R, SI, D = x.shape
    ti = min(ti, SI)
    final = g_final is not None
    seq = lambda c: pl.BlockSpec((None, R, ti, c), lambda b, s: (b, 0, s, 0))
    as_tokens = lambda t: t.reshape(B, R, SI, t.shape[-1])
    in_specs = [seq(D), seq(D // 2), seq(D // 2), seq(LANES)]
    args = [x, as_tokens(y_hi), as_tokens(y_lo), as_tokens(pair_gates)]
    if final:
        in_specs.append(pl.BlockSpec((1, D), lambda b, s: (0, 0)))
        args.append(g_final.reshape(1, D))
    out_spec = pl.BlockSpec((None, ti, R, D), lambda b, s: (b, s, 0, 0)) if natural_out else seq(D)
    return pl.pallas_call(
        functools.partial(_moe_mix_kernel, final=final, natural_out=natural_out),
        grid=(B, SI // ti), in_specs=in_specs, out_specs=out_spec,
        out_shape=jax.ShapeDtypeStruct((B, SI, R, D) if natural_out else (B, R, SI, D), F32),
        compiler_params=_params("parallel", "parallel"),
        name="moe_mix",
    )(*args)


def _moe(h, x, route, wg, wu, wd, g_final, *, natural_out, tm=512):
    N, half = h.shape
    plan = _plan_rows(route, wg.shape[0], tm=tm)
    rows_hi, rows_lo = plan["rows"]
    xs = _sc_rows_kernel(N, plan["p_max"], half, scatter=True)(h, rows_hi, rows_lo)
    ys = _moe_experts(xs, plan, wg, wu, wd, tm=tm)
    y_hi, y_lo = _sc_rows_kernel(N, plan["p_max"], half, scatter=False)(ys, rows_hi, rows_lo)
    return _moe_mix(x, y_hi, y_lo, plan["pair_gates"], g_final, natural_out=natural_out)


def kernel(x, norm_mix, w_in, w_pool, pool_scale, attn_gain, w_out, norm_ffn, ffn_wg, ffn_wu, ffn_wd,
           w_router, moe_wg, moe_wu, moe_wd, final_norm):
    B, S, D = x.shape
    depth = norm_mix.shape[0]
    bf = lambda t: t.astype(BF16)
    w_in, w_pool, w_out = bf(w_in), bf(w_pool), bf(w_out)
    ffn_wg, ffn_wu, ffn_wd = bf(ffn_wg), bf(ffn_wu), bf(ffn_wd)
    R = RESIDUES
    assert S % R == 0
    N, SI = B * S, S // R
    x = x.reshape(B, SI, R, D)
    for l in range(depth):
        first, last = l == 0, l == depth - 1
        ya, q, k, v, q32, k32, v32 = _mix_in(x, norm_mix[l], w_in[l], w_pool[l], pool_scale[l], natural_x=first)
        narrow = lambda dil: (R // dil) * 16 > WINDOW_STEPS
        branches = [_attn_branch(q32, k32, v32, dil) if narrow(dil) else _attn_branch(q, k, v, dil)
                    for _, dil in DILATED_PATTERNS]
        i, routed = l // 2, l % 2 == 1
        x1, h2, *route = _mix_out(x, ya, [o for o, _ in branches], [lse for _, lse in branches],
                                  attn_gain[l], w_out[l], norm_ffn[l], w_router[i] if routed else None,
                                  natural_x=first)
        g_final = final_norm if last else None
        if routed:
            x = _moe(h2.reshape(N, D // 2), x1, route[0].reshape(N, LANES), moe_wg[i], moe_wu[i], moe_wd[i],
                     g_final, natural_out=last)
        else:
            x = _ffn(h2.reshape(N, D), x1.reshape(N, D), ffn_wg[i], ffn_wu[i], ffn_wd[i], g_final)
            x = x.reshape(B, R, SI, D)
            if last:
                x = x.transpose(0, 2, 1, 3)
    return x.reshape(B, S, D)
```

```python
import functools

import jax
import jax.numpy as jnp
import numpy as np
from jax import lax
from jax.experimental import pallas as pl
from jax.experimental.pallas import tpu as pltpu
from jax.experimental.pallas import tpu_sc as plsc

F32 = jnp.float32
BF16 = jnp.bfloat16
U32 = jnp.uint32
HIGH_HALF = np.uint32(0xFFFF0000)

EPS = 1e-6
LANES = 128
HEAD_DIM = 64
POOL_WINDOWS = (2, 4, 8, 16)
POOL_HIST = 8
DILATED_PATTERNS = ((128, 1), (512, 4), (2048, 16))
WINDOW_STEPS = 128
RESIDUES = 16
ATTN_STEP_ROWS = 1024
ATTN_UNROLL_QUERIES = 512
LOG2_E = 1.4426950408889634
SC_CORES, SC_SUBCORES = 2, 16
SC_WORKERS = SC_CORES * SC_SUBCORES
SC_WINDOW = 64
assert all(w // d == WINDOW_STEPS and RESIDUES % d == 0 for w, d in DILATED_PATTERNS)
TOP_K = 2
MASKED = -1e30
VMEM_LIMIT = 48 * 1024 * 1024
MOE_VMEM_LIMIT = 56 * 1024 * 1024


def _rms(x, g):
    return x * lax.rsqrt(jnp.mean(x * x, axis=-1, keepdims=True) + EPS) * g


def _params(*sem, vmem_limit=VMEM_LIMIT):
    return pltpu.CompilerParams(dimension_semantics=sem, vmem_limit_bytes=vmem_limit)


def _token_rows(ref, natural=False):
    t = ref[...]
    if natural:
        t = pltpu.einshape("ird->rid", t)
    return t.reshape(-1, t.shape[-1])


def _pack_halves(t):
    bits = lax.bitcast_convert_type(t, U32)
    bits = (bits + np.uint32(0x7FFF) + ((bits >> 16) & np.uint32(1))) & HIGH_HALF
    half = t.shape[-1] // 2
    return (bits[:, :half] >> 16) | bits[:, half:]


def _unpack_halves(words):
    halves = [lax.bitcast_convert_type(w, F32) for w in (words << 16, words & HIGH_HALF)]
    return jnp.concatenate(halves, axis=-1)


def _mix_in_kernel(x_ref, g_ref, w_ref, wp_ref, ps_ref, ya_ref, q_ref, k_ref, v_ref, q32_ref, k32_ref, v32_ref,
                   ubuf, uprev, *, d_pool, d_attn, ti, natural_x):
    s = pl.program_id(1)
    R = RESIDUES
    blk = lambda t: t.reshape(R, ti, t.shape[-1])
    h = _rms(_token_rows(x_ref, natural_x), g_ref[...]).astype(BF16)
    u = jnp.dot(h, w_ref[:, :d_pool], preferred_element_type=F32)
    scale = HEAD_DIM ** -0.5 * LOG2_E
    for n, (ref, ref32) in enumerate(zip((q_ref, k_ref, v_ref), (q32_ref, k32_ref, v32_ref))):
        t = jnp.dot(h, w_ref[:, d_pool + n * d_attn:d_pool + (n + 1) * d_attn], preferred_element_type=F32)
        t = t * scale if n == 0 else t
        ref[...] = blk(t.astype(BF16))
        ref32[...] = blk(t)

    history = ubuf[:, POOL_HIST - 1:POOL_HIST, :]
    ubuf[:, POOL_HIST - 1:POOL_HIST, :] = jnp.where(s == 0, 0.0, history)
    ubuf[:, POOL_HIST:POOL_HIST + ti, :] = blk(u)
    uprev[...] = ubuf[:, POOL_HIST - 1:POOL_HIST - 1 + ti, :]
    at_start = (s * ti + lax.broadcasted_iota(jnp.int32, (ti, 1), 0)) == 0
    group = d_pool // len(POOL_WINDOWS)
    zs = []
    for gi, w in enumerate(POOL_WINDOWS):
        cols = slice(gi * group, (gi + 1) * group)
        ds = []
        for r in range(R):
            ug = ubuf[r, POOL_HIST:POOL_HIST + ti, cols]
            win = ug
            for back in range(1, w):
                rr = r - back
                win = win + (ubuf[rr, POOL_HIST:POOL_HIST + ti, cols] if rr >= 0 else uprev[rr + R, :, cols])
            cnt = jnp.where(at_start, float(min(r + 1, w)), float(w))
            ds.append(win / cnt - ug)
        d = jnp.concatenate(ds, axis=0).astype(BF16)
        zs.append(jnp.dot(d, wp_ref[gi], preferred_element_type=F32))
    z = jnp.concatenate(zs, axis=-1)
    ya_ref[...] = blk(_rms(z, ps_ref[...]).astype(BF16))
    ubuf[:, POOL_HIST - 1:POOL_HIST, :] = ubuf[:, POOL_HIST + ti - 1:POOL_HIST + ti, :]


def _mix_in(x, g, w_in, w_pool, pool_scale, *, natural_x, ti=32):
    B, R, SI, D = x.shape
    if natural_x:
        R, SI = SI, R
    d_pool = pool_scale.shape[-1]
    d_in = w_in.shape[-1]
    d_attn = (d_in - d_pool) // 3
    ti = min(ti, SI)
    assert R == RESIDUES >= max(POOL_WINDOWS) and SI % ti == 0 and ti % 16 == 0
    assert d_pool % (LANES * len(POOL_WINDOWS)) == 0
    seq_spec = lambda c: pl.BlockSpec((None, R, ti, c), lambda b, s: (b, 0, s, 0))
    full = lambda shape: pl.BlockSpec(shape, lambda b, s: (0,) * len(shape))
    out_sds = lambda c, dtype=BF16: jax.ShapeDtypeStruct((B, R, SI, c), dtype)
    return pl.pallas_call(
        functools.partial(_mix_in_kernel, d_pool=d_pool, d_attn=d_attn, ti=ti, natural_x=natural_x),
        grid=(B, SI // ti),
        in_specs=[pl.BlockSpec((None, ti, R, D), lambda b, s: (b, s, 0, 0)) if natural_x else seq_spec(D),
                  full((1, D)), full((D, d_in)), full(w_pool.shape), full((1, d_pool))],
        out_specs=[seq_spec(d_pool)] + [seq_spec(d_attn)] * 6,
        out_shape=[out_sds(d_pool)] + [out_sds(d_attn)] * 3 + [out_sds(d_attn, F32)] * 3,
        scratch_shapes=[pltpu.VMEM((R, POOL_HIST + ti, d_pool), F32), pltpu.VMEM((R, ti, d_pool), F32)],
        compiler_params=_params("parallel", "arbitrary"),
        name="mix_in",
    )(x, g.reshape(1, D), w_in, w_pool, pool_scale.reshape(1, d_pool))


def _attn_kernel(q_ref, k_ref, v_ref, o_ref, lse_ref, bias_ref, *, nq, strips, d_attn):
    _, classes, chunk, _ = q_ref.shape
    L = nq // strips
    nk, n_blocks = 2 * nq, chunk // L
    first_block = pl.program_id(2) * n_blocks
    row = lax.broadcasted_iota(jnp.int32, (nq, 1), 0)
    col = lax.broadcasted_iota(jnp.int32, (1, nk), 1)
    q_strip, q_row = row >> (L.bit_length() - 1), row & (L - 1)
    k_strip, k_row = col >> ((2 * L).bit_length() - 1), col & (2 * L - 1)
    back = strips * (q_row - k_row) + (q_strip - k_strip)

    def band(offset):
        rel = back + strips * offset
        return jnp.where((rel >= 0) & (rel <= WINDOW_STEPS), 0.0, MASKED).astype(F32)

    bias_ref[0] = band(0)
    bias_ref[1] = band(L)
    lane = lax.broadcasted_iota(jnp.int32, (nq, LANES), 1)
    lo_half = lane < HEAD_DIM

    def block(it, carry):
        g, j_local = it // n_blocks, it % n_blocks
        j = first_block + j_local
        q_rows = pl.ds(pl.multiple_of(j_local * L, L), L)
        k_rows = pl.ds(pl.multiple_of(jnp.maximum(j - 1, 0) * L, L), 2 * L)
        bias = bias_ref[jnp.minimum(j, 1)]
        lse_tile = jnp.zeros((nq, LANES), F32)
        for hp in range(d_attn // LANES):
            cols = slice(hp * LANES, (hp + 1) * LANES)
            qp = q_ref[:, g, q_rows, cols].reshape(nq, LANES).astype(BF16)
            kk = k_ref[:, g, k_rows, cols].reshape(nk, LANES).astype(BF16)
            vv = v_ref[:, g, k_rows, cols].reshape(nk, LANES).astype(BF16)
            outs, lses = [], []
            for sub in range(2):
                keep = lo_half if sub == 0 else jnp.logical_not(lo_half)
                qm = jnp.where(keep, qp, jnp.zeros_like(qp))
                s = lax.dot_general(qm, kk, (((1,), (1,)), ((), ())), preferred_element_type=F32) + bias
                m = jnp.max(s, axis=-1, keepdims=True)
                p = jnp.exp2(s - m)
                l = jnp.sum(p, axis=-1, keepdims=True)
                o = jnp.dot(p.astype(BF16), vv, preferred_element_type=F32)
                outs.append(o / l)
                lses.append(m + jnp.log2(l))
            o_pair = jnp.where(lo_half, outs[0], outs[1]).astype(o_ref.dtype)
            o_ref[:, g, q_rows, cols] = o_pair.reshape(strips, L, LANES)
            pair_lse = jnp.where(lane == 2 * hp, lses[0], lses[1])
            lse_tile = jnp.where((lane >> 1) == hp, pair_lse, lse_tile)
        lse_ref[:, g, q_rows, :] = lse_tile.reshape(strips, L, LANES)
        return carry

    lax.fori_loop(0, classes * n_blocks, block, 0, unroll=ATTN_UNROLL_QUERIES // nq)


def _attn_branch(q, k, v, dil):
    B, R, SI, C = q.shape
    strips = R // dil
    row_tile = 8 * 4 // q.dtype.itemsize
    nq = max(WINDOW_STEPS, strips * row_tile)
    L = nq // strips
    classes = max(1, min(dil, ATTN_STEP_ROWS // (strips * SI)))
    chunk = min(SI, max(L, ATTN_STEP_ROWS // (strips * classes)))
    assert R % dil == 0 and SI % chunk == 0 and chunk % L == 0 and SI >= 2 * L and dil % classes == 0
    assert C % LANES == 0 and C // HEAD_DIM <= LANES and nq & (nq - 1) == 0 and L & (L - 1) == 0
    view = lambda t: t.reshape(B, strips, dil, SI, t.shape[-1])
    q_spec = lambda c: pl.BlockSpec((None, strips, classes, chunk, c), lambda b, g, s: (b, 0, g, s, 0))
    kv_spec = pl.BlockSpec((None, strips, classes, SI, C), lambda b, g, s: (b, 0, g, 0, 0),
                           pipeline_mode=pl.Buffered(1 if SI > chunk else 2))
    o, lse = pl.pallas_call(
        functools.partial(_attn_kernel, nq=nq, strips=strips, d_attn=C),
        scratch_shapes=[pltpu.VMEM((2, nq, 2 * nq), F32)],
        grid=(B, dil // classes, SI // chunk),
        in_specs=[q_spec(C), kv_spec, kv_spec],
        out_specs=[q_spec(C), q_spec(LANES)],
        out_shape=[jax.ShapeDtypeStruct((B, strips, dil, SI, C), q.dtype),
                   jax.ShapeDtypeStruct((B, strips, dil, SI, LANES), F32)],
        compiler_params=_params("parallel", "parallel", "arbitrary"),
        name=f"attn_d{dil}",
    )(view(q), view(k), view(v))
    return o.reshape(B, R, SI, C), lse.reshape(B, R, SI, LANES)


def _route(h, wr, earlier, n_experts):
    h_hi = h.astype(BF16)
    h_lo = (h - h_hi.astype(F32)).astype(BF16)
    both = jnp.dot(h_hi, wr, preferred_element_type=F32)
    logits = both[:, :LANES] + both[:, LANES:] + jnp.dot(h_lo, wr[:, :LANES], preferred_element_type=F32)
    rows = h.shape[0]
    scores = logits.T[:n_experts]
    expert = lax.broadcasted_iota(jnp.int32, scores.shape, 0)
    picks = []
    for _ in range(TOP_K):
        m = jnp.max(scores, axis=0, keepdims=True)
        idx = jnp.min(jnp.where(scores == m, expert, n_experts), axis=0, keepdims=True)
        picks.append((m, idx))
        scores = jnp.where(expert == idx, -jnp.inf, scores)
    (m1, i1), (m2, i2) = picks
    e2 = jnp.exp(m2 - m1)
    g1 = 1.0 / (1.0 + e2)
    g2 = e2 / (1.0 + e2)
    picked = jnp.where(jnp.logical_or(expert == i1, expert == i2), 1.0, 0.0)
    rank = jnp.dot(picked.astype(BF16), earlier, preferred_element_type=F32)
    first_is_hi = i1 > i2
    e_hi, e_lo = jnp.maximum(i1, i2), jnp.minimum(i1, i2)
    rank_of = lambda e: jnp.sum(jnp.where(expert == e, rank, 0.0), axis=0, keepdims=True).astype(jnp.int32)
    plan = jnp.concatenate([e_hi, rank_of(e_hi), e_lo, rank_of(e_lo), jnp.zeros((4, rows), jnp.int32)], axis=0)
    gate_rows = jnp.concatenate([jnp.where(first_is_hi, g1, g2), jnp.where(first_is_hi, g2, g1),
                                 jnp.zeros((LANES - 2, rows), F32)], axis=0)
    return gate_rows.T, plan, jnp.sum(picked, axis=1, keepdims=True)


def _mix_out_kernel(*refs, d_pool, natural_x, n_experts):
    if n_experts:
        (x_ref, ya_ref, o1_ref, o2_ref, o3_ref, l1_ref, l2_ref, l3_ref, gain_ref, wo_ref, gffn_ref, exp_ref,
         wr_ref, earlier_ref, x1_ref, h2_ref, gates_ref, plan_ref, counts_ref) = refs
    else:
        (x_ref, ya_ref, o1_ref, o2_ref, o3_ref, l1_ref, l2_ref, l3_ref, gain_ref, wo_ref, gffn_ref, exp_ref,
         x1_ref, h2_ref) = refs
    lses = [_token_rows(l) for l in (l1_ref, l2_ref, l3_ref)]
    top = jnp.maximum(jnp.maximum(lses[0], lses[1]), lses[2])
    es = [jnp.exp2(l - top) for l in lses]
    den = es[0] + es[1] + es[2]
    expand = exp_ref[...]

    def per_lane(w):
        hi = w.astype(BF16)
        lo = (w - hi.astype(F32)).astype(BF16)
        return jnp.dot(jnp.concatenate([hi, lo], axis=1), expand, preferred_element_type=F32)

    o = 0.0
    for e, o_ref in zip(es, (o1_ref, o2_ref, o3_ref)):
        o = o + per_lane(e / den) * _token_rows(o_ref).astype(F32)
    yb = _rms(o, gain_ref[...]).astype(BF16)
    y = (jnp.dot(_token_rows(ya_ref), wo_ref[:d_pool, :], preferred_element_type=F32)
         + jnp.dot(yb, wo_ref[d_pool:, :], preferred_element_type=F32))
    x1 = _token_rows(x_ref, natural_x) + y
    x1_ref[...] = x1.reshape(x1_ref.shape)
    h2 = _rms(x1, gffn_ref[...])
    if n_experts:
        gate_tile, plan, counts = _route(h2, wr_ref[...], earlier_ref[...], n_experts)
        gates_ref[...] = gate_tile.reshape(gates_ref.shape)
        plan_ref[...] = plan
        counts_ref[...] = jnp.broadcast_to(counts, counts_ref.shape)
        h2_ref[...] = _pack_halves(h2).reshape(h2_ref.shape)
    else:
        h2_ref[...] = h2.astype(h2_ref.dtype).reshape(h2_ref.shape)


def _mix_out(x, ya, outs, lses, attn_gain, w_out, g_ffn, w_router, *, natural_x, ti=32):
    B, R, SI, d_pool = ya.shape
    D = x.shape[-1]
    d_attn = outs[0].shape[-1]
    ti = min(ti, SI)
    head_of_lane = jnp.arange(d_attn) // HEAD_DIM
    expand = (jnp.arange(2 * LANES)[:, None] % LANES == head_of_lane[None, :]).astype(BF16)
    seq = lambda c: pl.BlockSpec((None, R, ti, c), lambda b, s: (b, 0, s, 0))
    x_spec = pl.BlockSpec((None, ti, R, D), lambda b, s: (b, s, 0, 0)) if natural_x else seq(D)
    full = lambda shape: pl.BlockSpec(shape, lambda b, s: (0,) * len(shape))
    tokens = lambda c, dtype: jax.ShapeDtypeStruct((B, R, SI, c), dtype)
    in_specs = ([x_spec, seq(d_pool)] + [seq(d_attn)] * 3 + [seq(LANES)] * 3
                + [full((1, d_attn)), full(w_out.shape), full((1, D)), full(expand.shape)])
    args = [x, ya, *outs, *lses, attn_gain.reshape(1, d_attn), w_out, g_ffn.reshape(1, D), expand]
    if w_router is None:
        n_experts = 0
        out_specs, out_shape = [seq(D), seq(D)], [tokens(D, F32), tokens(D, BF16)]
    else:
        n_experts = w_router.shape[-1]
        assert 2 * n_experts <= LANES and TOP_K == 2
        wr = jnp.zeros((D, LANES), F32).at[:, :n_experts].set(w_router)
        wr_hi = wr.astype(BF16)
        rows = R * ti
        earlier = (jnp.arange(rows)[:, None] < jnp.arange(rows)[None, :]).astype(BF16)
        in_specs += [full((D, 2 * LANES)), full((rows, rows))]
        args += [jnp.concatenate([wr_hi, (wr - wr_hi.astype(F32)).astype(BF16)], axis=1), earlier]
        per_block = lambda r, c: pl.BlockSpec((None, None, r, c), lambda b, s: (b, s, 0, 0))
        out_specs = [seq(D), seq(D // 2), seq(LANES), per_block(8, rows), per_block(n_experts, LANES)]
        out_shape = [tokens(D, F32), tokens(D // 2, U32), tokens(LANES, F32),
                     jax.ShapeDtypeStruct((B, SI // ti, 8, rows), jnp.int32),
                     jax.ShapeDtypeStruct((B, SI // ti, n_experts, LANES), F32)]
    return pl.pallas_call(
        functools.partial(_mix_out_kernel, d_pool=d_pool, natural_x=natural_x, n_experts=n_experts),
        grid=(B, SI // ti),
        in_specs=in_specs, out_specs=out_specs, out_shape=out_shape,
        compiler_params=_params("parallel", "parallel"),
        name="mix_out",
    )(*args)


def _swiglu_hidden(h, wg, wu):
    a = jnp.dot(h, wg, preferred_element_type=F32)
    b = jnp.dot(h, wu, preferred_element_type=F32)
    return (a * jax.nn.sigmoid(a) * b).astype(BF16)


def _finish(x, y, gfin_ref):
    out = x + y
    return out if gfin_ref is None else _rms(out, gfin_ref[...])


def _ffn_kernel(*refs, final, tf):
    if final:
        h_ref, x_ref, wg_ref, wu_ref, wd_ref, gfin_ref, o_ref, hid = refs
    else:
        (h_ref, x_ref, wg_ref, wu_ref, wd_ref, o_ref, hid), gfin_ref = refs, None
    h = h_ref[...]
    for c in range(wg_ref.shape[-1] // tf):
        cols = slice(c * tf, (c + 1) * tf)
        hid[:, cols] = _swiglu_hidden(h, wg_ref[:, cols], wu_ref[:, cols])
    y = jnp.dot(hid[...], wd_ref[...], preferred_element_type=F32)
    o_ref[...] = _finish(x_ref[...], y, gfin_ref)


def _pick_chunk(n, target):
    best = None
    for c in range(LANES, min(n, target) + 1, LANES):
        if n % c == 0:
            best = c
    assert best is not None
    return best


def _ffn(h, x, wg, wu, wd, g_final, *, tm=512, tf_target=1536):
    N, D = x.shape
    F = wg.shape[-1]
    tm = min(tm, N)
    tf = _pick_chunk(F, tf_target)
    final = g_final is not None
    row = pl.BlockSpec((tm, D), lambda i: (i, 0))
    resident = lambda shape: pl.BlockSpec(shape, lambda i: (0, 0), pipeline_mode=pl.Buffered(1))
    in_specs = [row, row, resident((D, F)), resident((D, F)), resident((F, D))]
    args = [h, x, wg, wu, wd]
    if final:
        in_specs.append(pl.BlockSpec((1, D), lambda i: (0, 0)))
        args.append(g_final.reshape(1, D))
    return pl.pallas_call(
        functools.partial(_ffn_kernel, final=final, tf=tf),
        grid=(N // tm,),
        in_specs=in_specs,
        out_specs=row,
        out_shape=jax.ShapeDtypeStruct((N, D), F32),
        scratch_shapes=[pltpu.VMEM((tm, F), BF16)],
        compiler_params=_params("parallel"),
        name="ffn_dense",
    )(*args)


def _plan_rows(block_plan, block_counts, token_shape, *, tm):
    B, R, SI = token_shape
    i32 = jnp.int32
    _, per_row, _, rows = block_plan.shape
    E = block_counts.shape[2]
    p_max = TOP_K * B * R * SI + E * tm
    counts = block_counts[..., 0].astype(i32).reshape(-1, E)
    block_start = jnp.cumsum(counts, axis=0) - counts
    seg_len = (jnp.sum(counts, axis=0) + tm - 1) // tm * tm
    seg_end = jnp.cumsum(seg_len)
    base = (seg_end - seg_len)[None, :] + block_start
    plan = block_plan.reshape(-1, 8, rows)
    experts = jnp.arange(E, dtype=i32)[None, None, :]

    def sorted_rows(expert, rank):
        row = jnp.sum(jnp.where(expert[:, :, None] == experts, base[:, None, :], 0), axis=-1) + rank
        return row.reshape(B, per_row, R, rows // R).transpose(0, 2, 1, 3).reshape(-1)

    tile_start = jnp.arange(p_max // tm, dtype=i32) * tm
    tile_expert = jnp.minimum(jnp.sum(seg_end[None, :] <= tile_start[:, None], axis=1), E - 1).astype(i32)
    return dict(rows=(sorted_rows(plan[:, 0], plan[:, 1]), sorted_rows(plan[:, 2], plan[:, 3])),
                tile_expert=tile_expert, n_active_tiles=(seg_end[-1] // tm).reshape(1).astype(i32), p_max=p_max)


def _sc_rows_kernel(n_rows, n_out, d, scatter):
    per_worker = n_rows // SC_WORKERS
    assert n_rows % (SC_WORKERS * SC_WINDOW) == 0
    mesh = plsc.VectorSubcoreMesh(core_axis_name="c", subcore_axis_name="s")
    out_type = (jax.ShapeDtypeStruct((n_out, d), U32) if scatter
                else [jax.ShapeDtypeStruct((n_rows, d), U32)] * 2)

    def body(*refs):
        if scatter:
            src_hbm, hi_hbm, lo_hbm, out_hbm, idx_hi, idx_lo, rows_hi, rows_lo, sem_hi, sem_lo = refs
        else:
            src_hbm, hi_hbm, lo_hbm, out_hi_hbm, out_lo_hbm, idx_hi, idx_lo, rows_hi, rows_lo, sem_hi, sem_lo = refs
        worker = lax.axis_index("s") * SC_CORES + lax.axis_index("c")

        @pl.loop(0, per_worker // SC_WINDOW)
        def _(i):
            window = pl.ds(worker * per_worker + i * SC_WINDOW, SC_WINDOW)
            pltpu.sync_copy(hi_hbm.at[window], idx_hi)
            pltpu.sync_copy(lo_hbm.at[window], idx_lo)
            if scatter:
                pltpu.sync_copy(src_hbm.at[window], rows_hi)
                to_hi = pltpu.async_copy(rows_hi, out_hbm.at[idx_hi], sem_hi)
                to_lo = pltpu.async_copy(rows_hi, out_hbm.at[idx_lo], sem_lo)
                to_hi.wait()
                to_lo.wait()
            else:
                from_hi = pltpu.async_copy(src_hbm.at[idx_hi], rows_hi, sem_hi)
                from_lo = pltpu.async_copy(src_hbm.at[idx_lo], rows_lo, sem_lo)
                from_hi.wait()
                pltpu.sync_copy(rows_hi, out_hi_hbm.at[window])
                from_lo.wait()
                pltpu.sync_copy(rows_lo, out_lo_hbm.at[window])

    index_vec, row_buf = pltpu.VMEM((SC_WINDOW,), jnp.int32), pltpu.VMEM((SC_WINDOW, d), U32)
    return pl.kernel(body, mesh=mesh, out_type=out_type,
                     scratch_types=[index_vec, index_vec, row_buf, row_buf,
                                    pltpu.SemaphoreType.DMA, pltpu.SemaphoreType.DMA])


def _moe_experts_kernel(te_ref, na_ref, xs_ref, wg_hbm, wu_hbm, wd_hbm, o_ref,
                        cache_g, cache_u, cache_d, stage_g, stage_u, stage_d, sems, acc, *, tf):
    i = pl.program_id(0)
    e = te_ref[i]
    n_chunks = cache_g.shape[-1] // tf
    active = i < na_ref[0]
    new_expert = jnp.logical_or(i == 0, e != te_ref[jnp.maximum(i - 1, 0)])

    def chunk_copies(c, slot):
        cols = pl.ds(c * tf, tf)
        return (pltpu.make_async_copy(wg_hbm.at[e, :, cols], stage_g.at[slot], sems.at[0, slot]),
                pltpu.make_async_copy(wu_hbm.at[e, :, cols], stage_u.at[slot], sems.at[1, slot]),
                pltpu.make_async_copy(wd_hbm.at[e, cols, :], stage_d.at[slot], sems.at[2, slot]))

    def tile_ffn(load_weights):
        x = _unpack_halves(xs_ref[...]).astype(BF16)
        if load_weights:
            for cp in chunk_copies(0, 0):
                cp.start()
        for c in range(n_chunks):
            cols = slice(c * tf, (c + 1) * tf)
            if load_weights:
                slot = c % 2
                if c + 1 < n_chunks:
                    for cp in chunk_copies(c + 1, 1 - slot):
                        cp.start()
                for cp in chunk_copies(c, slot):
                    cp.wait()
                cache_g[:, cols] = stage_g[slot].astype(BF16)
                cache_u[:, cols] = stage_u[slot].astype(BF16)
                cache_d[cols, :] = stage_d[slot].astype(BF16)
            hid = _swiglu_hidden(x, cache_g[:, cols], cache_u[:, cols])
            part = jnp.dot(hid, cache_d[cols, :], preferred_element_type=F32)
            if c == 0:
                acc[...] = part
            else:
                acc[...] += part
        o_ref[...] = _pack_halves(acc[...])

    pl.when(jnp.logical_and(active, new_expert))(lambda: tile_ffn(True))
    pl.when(jnp.logical_and(active, jnp.logical_not(new_expert)))(lambda: tile_ffn(False))


def _moe_experts(xs, plan, wg, wu, wd, *, tm, tf_target=512):
    P = xs.shape[0]
    E, D, F = wg.shape
    tf = _pick_chunk(F, tf_target)
    tile = lambda i, te, na: (jnp.minimum(i, na[0] - 1), 0)
    in_hbm = pl.BlockSpec(memory_space=pl.ANY)
    grid_spec = pltpu.PrefetchScalarGridSpec(
        num_scalar_prefetch=2,
        grid=(P // tm,),
        in_specs=[pl.BlockSpec((tm, D // 2), tile), in_hbm, in_hbm, in_hbm],
        out_specs=pl.BlockSpec((tm, D // 2), tile),
        scratch_shapes=[pltpu.VMEM((D, F), BF16), pltpu.VMEM((D, F), BF16), pltpu.VMEM((F, D), BF16),
                        pltpu.VMEM((2, D, tf), F32), pltpu.VMEM((2, D, tf), F32), pltpu.VMEM((2, tf, D), F32),
                        pltpu.SemaphoreType.DMA((3, 2)), pltpu.VMEM((tm, D), F32)],
    )
    return pl.pallas_call(
        functools.partial(_moe_experts_kernel, tf=tf),
        grid_spec=grid_spec,
        out_shape=jax.ShapeDtypeStruct((P, D // 2), U32),
        compiler_params=_params("arbitrary", vmem_limit=MOE_VMEM_LIMIT),
        name="moe_experts",
    )(plan["tile_expert"], plan["n_active_tiles"], xs, wg, wu, wd)


def _moe_mix_kernel(*refs, final, natural_out):
    if final:
        x_ref, yh_ref, yl_ref, g_ref, gfin_ref, o_ref = refs
    else:
        (x_ref, yh_ref, yl_ref, g_ref, o_ref), gfin_ref = refs, None
    gates = _token_rows(g_ref)
    lane = lax.broadcasted_iota(jnp.int32, gates.shape, 1)
    g_hi = jnp.sum(jnp.where(lane == 0, gates, 0.0), axis=-1, keepdims=True)
    g_lo = jnp.sum(jnp.where(lane == 1, gates, 0.0), axis=-1, keepdims=True)
    y = g_hi * _unpack_halves(_token_rows(yh_ref)) + g_lo * _unpack_halves(_token_rows(yl_ref))
    out = _finish(_token_rows(x_ref), y, gfin_ref).reshape(x_ref.shape)
    o_ref[...] = pltpu.einshape("rid->ird", out) if natural_out else out


def _moe_mix(x, y_hi, y_lo, pair_gates, g_final, *, natural_out, ti=32):
    B, R, SI, D = x.shape
    ti = min(ti, SI)
    final = g_final is not None
    seq = lambda c: pl.BlockSpec((None, R, ti, c), lambda b, s: (b, 0, s, 0))
    as_tokens = lambda t: t.reshape(B, R, SI, t.shape[-1])
    in_specs = [seq(D), seq(D // 2), seq(D // 2), seq(LANES)]
    args = [x, as_tokens(y_hi), as_tokens(y_lo), as_tokens(pair_gates)]
    if final:
        in_specs.append(pl.BlockSpec((1, D), lambda b, s: (0, 0)))
        args.append(g_final.reshape(1, D))
    out_spec = pl.BlockSpec((None, ti, R, D), lambda b, s: (b, s, 0, 0)) if natural_out else seq(D)
    return pl.pallas_call(
        functools.partial(_moe_mix_kernel, final=final, natural_out=natural_out),
        grid=(B, SI // ti), in_specs=in_specs, out_specs=out_spec,
        out_shape=jax.ShapeDtypeStruct((B, SI, R, D) if natural_out else (B, R, SI, D), F32),
        compiler_params=_params("parallel", "parallel"),
        name="moe_mix",
    )(*args)


def _moe(h, x, routing, wg, wu, wd, g_final, *, natural_out, tm=512):
    pair_gates, block_plan, block_counts = routing
    B, R, SI, half = h.shape
    N = B * R * SI
    plan = _plan_rows(block_plan, block_counts, (B, R, SI), tm=tm)
    rows_hi, rows_lo = plan["rows"]
    xs = _sc_rows_kernel(N, plan["p_max"], half, scatter=True)(h.reshape(N, half), rows_hi, rows_lo)
    ys = _moe_experts(xs, plan, wg, wu, wd, tm=tm)
    y_hi, y_lo = _sc_rows_kernel(N, plan["p_max"], half, scatter=False)(ys, rows_hi, rows_lo)
    return _moe_mix(x, y_hi, y_lo, pair_gates, g_final, natural_out=natural_out)


def kernel(x, norm_mix, w_in, w_pool, pool_scale, attn_gain, w_out, norm_ffn, ffn_wg, ffn_wu, ffn_wd,
           w_router, moe_wg, moe_wu, moe_wd, final_norm):
    B, S, D = x.shape
    depth = norm_mix.shape[0]
    bf = lambda t: t.astype(BF16)
    w_in, w_pool, w_out = bf(w_in), bf(w_pool), bf(w_out)
    ffn_wg, ffn_wu, ffn_wd = bf(ffn_wg), bf(ffn_wu), bf(ffn_wd)
    R = RESIDUES
    assert S % R == 0
    N, SI = B * S, S // R
    x = x.reshape(B, SI, R, D)
    for l in range(depth):
        first, last = l == 0, l == depth - 1
        ya, q, k, v, q32, k32, v32 = _mix_in(x, norm_mix[l], w_in[l], w_pool[l], pool_scale[l], natural_x=first)
        narrow = lambda dil: (R // dil) * 16 > WINDOW_STEPS
        branches = [_attn_branch(q32, k32, v32, dil) if narrow(dil) else _attn_branch(q, k, v, dil)
                    for _, dil in DILATED_PATTERNS]
        i, routed = l // 2, l % 2 == 1
        x1, h2, *routing = _mix_out(x, ya, [o for o, _ in branches], [lse for _, lse in branches],
                                    attn_gain[l], w_out[l], norm_ffn[l], w_router[i] if routed else None,
                                    natural_x=first)
        g_final = final_norm if last else None
        if routed:
            x = _moe(h2, x1, routing, moe_wg[i], moe_wu[i], moe_wd[i], g_final, natural_out=last)
        else:
            x = _ffn(h2.reshape(N, D), x1.reshape(N, D), ffn_wg[i], ffn_wu[i], ffn_wd[i], g_final)
            x = x.reshape(B, R, SI, D)
            if last:
                x = x.transpose(0, 2, 1, 3)
    return x.reshape(B, S, D)
```

```python
import functools

import jax
import jax.numpy as jnp
import numpy as np
from jax import lax
from jax.experimental import pallas as pl
from jax.experimental.pallas import tpu as pltpu
from jax.experimental.pallas import tpu_sc as plsc

F32 = jnp.float32
BF16 = jnp.bfloat16
U32 = jnp.uint32
HIGH_HALF = np.uint32(0xFFFF0000)

EPS = 1e-6
LANES = 128
HEAD_DIM = 64
POOL_WINDOWS = (2, 4, 8, 16)
POOL_HIST = 8
DILATED_PATTERNS = ((128, 1), (512, 4), (2048, 16))
WINDOW_STEPS = 128
RESIDUES = 16
ATTN_STEP_ROWS = 1024
ATTN_UNROLL_QUERIES = 1024
LOG2_E = 1.4426950408889634
SC_CORES, SC_SUBCORES = 2, 16
SC_WORKERS = SC_CORES * SC_SUBCORES
SC_WINDOW = 64
MOE_RETURN_GROUPS = 2
assert all(w // d == WINDOW_STEPS and RESIDUES % d == 0 for w, d in DILATED_PATTERNS)
TOP_K = 2
MASKED = -1e30
VMEM_LIMIT = 48 * 1024 * 1024
MOE_VMEM_LIMIT = 56 * 1024 * 1024


def _rms(x, g):
    return x * lax.rsqrt(jnp.mean(x * x, axis=-1, keepdims=True) + EPS) * g


def _params(*sem, vmem_limit=VMEM_LIMIT):
    return pltpu.CompilerParams(dimension_semantics=sem, vmem_limit_bytes=vmem_limit)


def _token_rows(ref, natural=False):
    t = ref[...]
    if natural:
        t = pltpu.einshape("ird->rid", t)
    return t.reshape(-1, t.shape[-1])


def _pack_halves(t):
    bits = lax.bitcast_convert_type(t, U32)
    bits = (bits + np.uint32(0x7FFF) + ((bits >> 16) & np.uint32(1))) & HIGH_HALF
    half = t.shape[-1] // 2
    return (bits[:, :half] >> 16) | bits[:, half:]


def _unpack_halves(words):
    halves = [lax.bitcast_convert_type(w, F32) for w in (words << 16, words & HIGH_HALF)]
    return jnp.concatenate(halves, axis=-1)


def _mix_in_kernel(x_ref, g_ref, w_ref, wp_ref, ps_ref, ya_ref, q_ref, k_ref, v_ref, q32_ref, k32_ref, v32_ref,
                   ubuf, uprev, *, d_pool, d_attn, ti, natural_x):
    s = pl.program_id(1)
    R = RESIDUES
    blk = lambda t: t.reshape(R, ti, t.shape[-1])
    h = _rms(_token_rows(x_ref, natural_x), g_ref[...]).astype(BF16)
    u = jnp.dot(h, w_ref[:, :d_pool], preferred_element_type=F32)
    scale = HEAD_DIM ** -0.5 * LOG2_E
    for n, (ref, ref32) in enumerate(zip((q_ref, k_ref, v_ref), (q32_ref, k32_ref, v32_ref))):
        t = jnp.dot(h, w_ref[:, d_pool + n * d_attn:d_pool + (n + 1) * d_attn], preferred_element_type=F32)
        t = t * scale if n == 0 else t
        ref[...] = blk(t.astype(BF16))
        ref32[...] = blk(t)

    history = ubuf[:, POOL_HIST - 1:POOL_HIST, :]
    ubuf[:, POOL_HIST - 1:POOL_HIST, :] = jnp.where(s == 0, 0.0, history)
    ubuf[:, POOL_HIST:POOL_HIST + ti, :] = blk(u)
    uprev[...] = ubuf[:, POOL_HIST - 1:POOL_HIST - 1 + ti, :]
    at_start = (s * ti + lax.broadcasted_iota(jnp.int32, (ti, 1), 0)) == 0
    group = d_pool // len(POOL_WINDOWS)
    zs = []
    for gi, w in enumerate(POOL_WINDOWS):
        cols = slice(gi * group, (gi + 1) * group)
        ds = []
        for r in range(R):
            ug = ubuf[r, POOL_HIST:POOL_HIST + ti, cols]
            win = ug
            for back in range(1, w):
                rr = r - back
                win = win + (ubuf[rr, POOL_HIST:POOL_HIST + ti, cols] if rr >= 0 else uprev[rr + R, :, cols])
            cnt = jnp.where(at_start, float(min(r + 1, w)), float(w))
            ds.append(win / cnt - ug)
        d = jnp.concatenate(ds, axis=0).astype(BF16)
        zs.append(jnp.dot(d, wp_ref[gi], preferred_element_type=F32))
    z = jnp.concatenate(zs, axis=-1)
    ya_ref[...] = blk(_rms(z, ps_ref[...]).astype(BF16))
    ubuf[:, POOL_HIST - 1:POOL_HIST, :] = ubuf[:, POOL_HIST + ti - 1:POOL_HIST + ti, :]


def _mix_in(x, g, w_in, w_pool, pool_scale, *, natural_x, ti=32):
    B, R, SI, D = x.shape
    if natural_x:
        R, SI = SI, R
    d_pool = pool_scale.shape[-1]
    d_in = w_in.shape[-1]
    d_attn = (d_in - d_pool) // 3
    ti = min(ti, SI)
    assert R == RESIDUES >= max(POOL_WINDOWS) and SI % ti == 0 and ti % 16 == 0
    assert d_pool % (LANES * len(POOL_WINDOWS)) == 0
    seq_spec = lambda c: pl.BlockSpec((None, R, ti, c), lambda b, s: (b, 0, s, 0))
    full = lambda shape: pl.BlockSpec(shape, lambda b, s: (0,) * len(shape))
    out_sds = lambda c, dtype=BF16: jax.ShapeDtypeStruct((B, R, SI, c), dtype)
    return pl.pallas_call(
        functools.partial(_mix_in_kernel, d_pool=d_pool, d_attn=d_attn, ti=ti, natural_x=natural_x),
        grid=(B, SI // ti),
        in_specs=[pl.BlockSpec((None, ti, R, D), lambda b, s: (b, s, 0, 0)) if natural_x else seq_spec(D),
                  full((1, D)), full((D, d_in)), full(w_pool.shape), full((1, d_pool))],
        out_specs=[seq_spec(d_pool)] + [seq_spec(d_attn)] * 6,
        out_shape=[out_sds(d_pool)] + [out_sds(d_attn)] * 3 + [out_sds(d_attn, F32)] * 3,
        scratch_shapes=[pltpu.VMEM((R, POOL_HIST + ti, d_pool), F32), pltpu.VMEM((R, ti, d_pool), F32)],
        compiler_params=_params("parallel", "arbitrary"),
        name="mix_in",
    )(x, g.reshape(1, D), w_in, w_pool, pool_scale.reshape(1, d_pool))


def _attn_kernel(q_ref, k_ref, v_ref, o_ref, lse_ref, bias_ref, *, nq, strips, d_attn):
    _, classes, chunk, _ = q_ref.shape
    L = nq // strips
    nk, n_blocks = 2 * nq, chunk // L
    first_block = pl.program_id(2) * n_blocks
    row = lax.broadcasted_iota(jnp.int32, (nq, 1), 0)
    col = lax.broadcasted_iota(jnp.int32, (1, nk), 1)
    q_strip, q_row = row >> (L.bit_length() - 1), row & (L - 1)
    k_strip, k_row = col >> ((2 * L).bit_length() - 1), col & (2 * L - 1)
    back = strips * (q_row - k_row) + (q_strip - k_strip)

    def band(offset):
        rel = back + strips * offset
        return jnp.where((rel >= 0) & (rel <= WINDOW_STEPS), 0.0, MASKED).astype(F32)

    bias_ref[0] = band(0)
    bias_ref[1] = band(L)
    lane = lax.broadcasted_iota(jnp.int32, (nq, LANES), 1)
    lo_half = lane < HEAD_DIM

    def block(it, carry):
        g, j_local = it // n_blocks, it % n_blocks
        j = first_block + j_local
        q_rows = pl.ds(pl.multiple_of(j_local * L, L), L)
        k_rows = pl.ds(pl.multiple_of(jnp.maximum(j - 1, 0) * L, L), 2 * L)
        bias = bias_ref[jnp.minimum(j, 1)]
        lse_tile = jnp.zeros((nq, LANES), F32)
        for hp in range(d_attn // LANES):
            cols = slice(hp * LANES, (hp + 1) * LANES)
            qp = q_ref[:, g, q_rows, cols].reshape(nq, LANES).astype(BF16)
            kk = k_ref[:, g, k_rows, cols].reshape(nk, LANES).astype(BF16)
            vv = v_ref[:, g, k_rows, cols].reshape(nk, LANES).astype(BF16)
            outs, lses = [], []
            for sub in range(2):
                keep = lo_half if sub == 0 else jnp.logical_not(lo_half)
                qm = jnp.where(keep, qp, jnp.zeros_like(qp))
                s = lax.dot_general(qm, kk, (((1,), (1,)), ((), ())), preferred_element_type=F32) + bias
                m = jnp.max(s, axis=-1, keepdims=True)
                p = jnp.exp2(s - m)
                l = jnp.sum(p, axis=-1, keepdims=True)
                o = jnp.dot(p.astype(BF16), vv, preferred_element_type=F32)
                outs.append(o / l)
                lses.append(m + jnp.log2(l))
            o_pair = jnp.where(lo_half, outs[0], outs[1]).astype(o_ref.dtype)
            o_ref[:, g, q_rows, cols] = o_pair.reshape(strips, L, LANES)
            pair_lse = jnp.where(lane == 2 * hp, lses[0], lses[1])
            lse_tile = jnp.where((lane >> 1) == hp, pair_lse, lse_tile)
        lse_ref[:, g, q_rows, :] = lse_tile.reshape(strips, L, LANES)
        return carry

    lax.fori_loop(0, classes * n_blocks, block, 0, unroll=ATTN_UNROLL_QUERIES // nq)


def _attn_branch(q, k, v, dil):
    B, R, SI, C = q.shape
    strips = R // dil
    row_tile = 8 * 4 // q.dtype.itemsize
    nq = max(WINDOW_STEPS, strips * row_tile)
    L = nq // strips
    classes = max(1, min(dil, ATTN_STEP_ROWS // (strips * SI)))
    chunk = min(SI, max(L, ATTN_STEP_ROWS // (strips * classes)))
    assert R % dil == 0 and SI % chunk == 0 and chunk % L == 0 and SI >= 2 * L and dil % classes == 0
    assert C % LANES == 0 and C // HEAD_DIM <= LANES and nq & (nq - 1) == 0 and L & (L - 1) == 0
    view = lambda t: t.reshape(B, strips, dil, SI, t.shape[-1])
    q_spec = lambda c: pl.BlockSpec((None, strips, classes, chunk, c), lambda b, g, s: (b, 0, g, s, 0))
    kv_spec = pl.BlockSpec((None, strips, classes, SI, C), lambda b, g, s: (b, 0, g, 0, 0),
                           pipeline_mode=pl.Buffered(1 if SI > chunk else 2))
    o, lse = pl.pallas_call(
        functools.partial(_attn_kernel, nq=nq, strips=strips, d_attn=C),
        scratch_shapes=[pltpu.VMEM((2, nq, 2 * nq), F32)],
        grid=(B, dil // classes, SI // chunk),
        in_specs=[q_spec(C), kv_spec, kv_spec],
        out_specs=[q_spec(C), q_spec(LANES)],
        out_shape=[jax.ShapeDtypeStruct((B, strips, dil, SI, C), q.dtype),
                   jax.ShapeDtypeStruct((B, strips, dil, SI, LANES), F32)],
        compiler_params=_params("parallel", "parallel", "arbitrary"),
        name=f"attn_d{dil}",
    )(view(q), view(k), view(v))
    return o.reshape(B, R, SI, C), lse.reshape(B, R, SI, LANES)


def _route(h, wr, earlier, n_experts):
    h_hi = h.astype(BF16)
    h_lo = (h - h_hi.astype(F32)).astype(BF16)
    both = jnp.dot(h_hi, wr, preferred_element_type=F32)
    logits = both[:, :LANES] + both[:, LANES:] + jnp.dot(h_lo, wr[:, :LANES], preferred_element_type=F32)
    rows = h.shape[0]
    scores = logits.T[:n_experts]
    expert = lax.broadcasted_iota(jnp.int32, scores.shape, 0)
    picks = []
    for _ in range(TOP_K):
        m = jnp.max(scores, axis=0, keepdims=True)
        idx = jnp.min(jnp.where(scores == m, expert, n_experts), axis=0, keepdims=True)
        picks.append((m, idx))
        scores = jnp.where(expert == idx, -jnp.inf, scores)
    (m1, i1), (m2, i2) = picks
    e2 = jnp.exp(m2 - m1)
    g1 = 1.0 / (1.0 + e2)
    g2 = e2 / (1.0 + e2)
    picked = jnp.where(jnp.logical_or(expert == i1, expert == i2), 1.0, 0.0)
    rank = jnp.dot(picked.astype(BF16), earlier, preferred_element_type=F32)
    first_is_hi = i1 > i2
    e_hi, e_lo = jnp.maximum(i1, i2), jnp.minimum(i1, i2)
    rank_of = lambda e: jnp.sum(jnp.where(expert == e, rank, 0.0), axis=0, keepdims=True).astype(jnp.int32)
    plan = jnp.concatenate([e_hi, rank_of(e_hi), e_lo, rank_of(e_lo), jnp.zeros((4, rows), jnp.int32)], axis=0)
    gate_rows = jnp.concatenate([jnp.where(first_is_hi, g1, g2), jnp.where(first_is_hi, g2, g1),
                                 jnp.zeros((LANES - 2, rows), F32)], axis=0)
    return gate_rows.T, plan, jnp.sum(picked, axis=1, keepdims=True)


def _mix_out_kernel(*refs, d_pool, natural_x, n_experts):
    if n_experts:
        (x_ref, ya_ref, o1_ref, o2_ref, o3_ref, l1_ref, l2_ref, l3_ref, gain_ref, wo_ref, gffn_ref, exp_ref,
         wr_ref, earlier_ref, x1_ref, h2_ref, gates_ref, plan_ref, counts_ref) = refs
    else:
        (x_ref, ya_ref, o1_ref, o2_ref, o3_ref, l1_ref, l2_ref, l3_ref, gain_ref, wo_ref, gffn_ref, exp_ref,
         x1_ref, h2_ref) = refs
    lses = [_token_rows(l) for l in (l1_ref, l2_ref, l3_ref)]
    top = jnp.maximum(jnp.maximum(lses[0], lses[1]), lses[2])
    es = [jnp.exp2(l - top) for l in lses]
    den = es[0] + es[1] + es[2]
    expand = exp_ref[...]

    def per_lane(w):
        hi = w.astype(BF16)
        lo = (w - hi.astype(F32)).astype(BF16)
        return jnp.dot(jnp.concatenate([hi, lo], axis=1), expand, preferred_element_type=F32)

    o = 0.0
    for e, o_ref in zip(es, (o1_ref, o2_ref, o3_ref)):
        o = o + per_lane(e / den) * _token_rows(o_ref).astype(F32)
    yb = _rms(o, gain_ref[...]).astype(BF16)
    y = (jnp.dot(_token_rows(ya_ref), wo_ref[:d_pool, :], preferred_element_type=F32)
         + jnp.dot(yb, wo_ref[d_pool:, :], preferred_element_type=F32))
    x1 = _token_rows(x_ref, natural_x) + y
    x1_ref[...] = x1.reshape(x1_ref.shape)
    h2 = _rms(x1, gffn_ref[...])
    if n_experts:
        gate_tile, plan, counts = _route(h2, wr_ref[...], earlier_ref[...], n_experts)
        gates_ref[...] = gate_tile.reshape(gates_ref.shape)
        plan_ref[...] = plan
        counts_ref[...] = jnp.broadcast_to(counts, counts_ref.shape)
        h2_ref[...] = _pack_halves(h2).reshape(h2_ref.shape)
    else:
        h2_ref[...] = h2.astype(h2_ref.dtype).reshape(h2_ref.shape)


def _mix_out(x, ya, outs, lses, attn_gain, w_out, g_ffn, w_router, *, natural_x, ti=32):
    B, R, SI, d_pool = ya.shape
    D = x.shape[-1]
    d_attn = outs[0].shape[-1]
    ti = min(ti, SI)
    head_of_lane = jnp.arange(d_attn) // HEAD_DIM
    expand = (jnp.arange(2 * LANES)[:, None] % LANES == head_of_lane[None, :]).astype(BF16)
    seq = lambda c: pl.BlockSpec((None, R, ti, c), lambda b, s: (b, 0, s, 0))
    x_spec = pl.BlockSpec((None, ti, R, D), lambda b, s: (b, s, 0, 0)) if natural_x else seq(D)
    full = lambda shape: pl.BlockSpec(shape, lambda b, s: (0,) * len(shape))
    tokens = lambda c, dtype: jax.ShapeDtypeStruct((B, R, SI, c), dtype)
    in_specs = ([x_spec, seq(d_pool)] + [seq(d_attn)] * 3 + [seq(LANES)] * 3
                + [full((1, d_attn)), full(w_out.shape), full((1, D)), full(expand.shape)])
    args = [x, ya, *outs, *lses, attn_gain.reshape(1, d_attn), w_out, g_ffn.reshape(1, D), expand]
    if w_router is None:
        n_experts = 0
        out_specs, out_shape = [seq(D), seq(D)], [tokens(D, F32), tokens(D, BF16)]
    else:
        n_experts = w_router.shape[-1]
        assert 2 * n_experts <= LANES and TOP_K == 2
        wr = jnp.zeros((D, LANES), F32).at[:, :n_experts].set(w_router)
        wr_hi = wr.astype(BF16)
        rows = R * ti
        earlier = (jnp.arange(rows)[:, None] < jnp.arange(rows)[None, :]).astype(BF16)
        in_specs += [full((D, 2 * LANES)), full((rows, rows))]
        args += [jnp.concatenate([wr_hi, (wr - wr_hi.astype(F32)).astype(BF16)], axis=1), earlier]
        per_block = lambda r, c: pl.BlockSpec((None, None, r, c), lambda b, s: (b, s, 0, 0))
        out_specs = [seq(D), seq(D // 2), seq(LANES), per_block(8, rows), per_block(n_experts, LANES)]
        out_shape = [tokens(D, F32), tokens(D // 2, U32), tokens(LANES, F32),
                     jax.ShapeDtypeStruct((B, SI // ti, 8, rows), jnp.int32),
                     jax.ShapeDtypeStruct((B, SI // ti, n_experts, LANES), F32)]
    return pl.pallas_call(
        functools.partial(_mix_out_kernel, d_pool=d_pool, natural_x=natural_x, n_experts=n_experts),
        grid=(B, SI // ti),
        in_specs=in_specs, out_specs=out_specs, out_shape=out_shape,
        compiler_params=_params("parallel", "parallel"),
        name="mix_out",
    )(*args)


def _swiglu_hidden(h, wg, wu):
    a = jnp.dot(h, wg, preferred_element_type=F32)
    b = jnp.dot(h, wu, preferred_element_type=F32)
    return (a * jax.nn.sigmoid(a) * b).astype(BF16)


def _finish(x, y, gfin_ref):
    out = x + y
    return out if gfin_ref is None else _rms(out, gfin_ref[...])


def _ffn_kernel(*refs, final, tf):
    if final:
        h_ref, x_ref, wg_ref, wu_ref, wd_ref, gfin_ref, o_ref, hid = refs
    else:
        (h_ref, x_ref, wg_ref, wu_ref, wd_ref, o_ref, hid), gfin_ref = refs, None
    h = h_ref[...]
    for c in range(wg_ref.shape[-1] // tf):
        cols = slice(c * tf, (c + 1) * tf)
        hid[:, cols] = _swiglu_hidden(h, wg_ref[:, cols], wu_ref[:, cols])
    y = jnp.dot(hid[...], wd_ref[...], preferred_element_type=F32)
    o_ref[...] = _finish(x_ref[...], y, gfin_ref)


def _pick_chunk(n, target):
    best = None
    for c in range(LANES, min(n, target) + 1, LANES):
        if n % c == 0:
            best = c
    assert best is not None
    return best


def _ffn(h, x, wg, wu, wd, g_final, *, tm=512, tf_target=1536):
    N, D = x.shape
    F = wg.shape[-1]
    tm = min(tm, N)
    tf = _pick_chunk(F, tf_target)
    final = g_final is not None
    row = pl.BlockSpec((tm, D), lambda i: (i, 0))
    resident = lambda shape: pl.BlockSpec(shape, lambda i: (0, 0), pipeline_mode=pl.Buffered(1))
    in_specs = [row, row, resident((D, F)), resident((D, F)), resident((F, D))]
    args = [h, x, wg, wu, wd]
    if final:
        in_specs.append(pl.BlockSpec((1, D), lambda i: (0, 0)))
        args.append(g_final.reshape(1, D))
    return pl.pallas_call(
        functools.partial(_ffn_kernel, final=final, tf=tf),
        grid=(N // tm,),
        in_specs=in_specs,
        out_specs=row,
        out_shape=jax.ShapeDtypeStruct((N, D), F32),
        scratch_shapes=[pltpu.VMEM((tm, F), BF16)],
        compiler_params=_params("parallel"),
        name="ffn_dense",
    )(*args)


def _plan_rows(block_plan, block_counts, token_shape, *, tm):
    B, R, SI = token_shape
    i32 = jnp.int32
    _, per_row, _, rows = block_plan.shape
    E = block_counts.shape[2]
    p_max = TOP_K * B * R * SI + E * tm
    counts = block_counts[..., 0].astype(i32).reshape(-1, E)
    block_start = jnp.cumsum(counts, axis=0) - counts
    seg_len = (jnp.sum(counts, axis=0) + tm - 1) // tm * tm
    seg_end = jnp.cumsum(seg_len)
    base = (seg_end - seg_len)[None, :] + block_start
    plan = block_plan.reshape(-1, 8, rows)
    experts = jnp.arange(E, dtype=i32)[None, None, :]

    def sorted_rows(expert, rank):
        row = jnp.sum(jnp.where(expert[:, :, None] == experts, base[:, None, :], 0), axis=-1) + rank
        return row.reshape(B, per_row, R, rows // R).transpose(0, 2, 1, 3).reshape(-1)

    tile_start = jnp.arange(p_max // tm, dtype=i32) * tm
    tile_expert = jnp.minimum(jnp.sum(seg_end[None, :] <= tile_start[:, None], axis=1), E - 1).astype(i32)
    return dict(rows=(sorted_rows(plan[:, 0], plan[:, 1]), sorted_rows(plan[:, 2], plan[:, 3])),
                tile_expert=tile_expert, n_active_tiles=(seg_end[-1] // tm).reshape(1).astype(i32), p_max=p_max)


def _sc_rows_kernel(n_rows, n_out, d, scatter):
    per_worker = n_rows // SC_WORKERS
    assert n_rows % (SC_WORKERS * SC_WINDOW) == 0
    mesh = plsc.VectorSubcoreMesh(core_axis_name="c", subcore_axis_name="s")
    out_type = (jax.ShapeDtypeStruct((n_out, d), U32) if scatter
                else [jax.ShapeDtypeStruct((n_rows, d), U32)] * 2)

    def body(*refs):
        if scatter:
            src_hbm, hi_hbm, lo_hbm, out_hbm, idx_hi, idx_lo, rows_hi, rows_lo, sem_hi, sem_lo = refs
        else:
            src_hbm, hi_hbm, lo_hbm, out_hi_hbm, out_lo_hbm, idx_hi, idx_lo, rows_hi, rows_lo, sem_hi, sem_lo = refs
        worker = lax.axis_index("s") * SC_CORES + lax.axis_index("c")

        @pl.loop(0, per_worker // SC_WINDOW)
        def _(i):
            window = pl.ds(worker * per_worker + i * SC_WINDOW, SC_WINDOW)
            pltpu.sync_copy(hi_hbm.at[window], idx_hi)
            pltpu.sync_copy(lo_hbm.at[window], idx_lo)
            if scatter:
                pltpu.sync_copy(src_hbm.at[window], rows_hi)
                to_hi = pltpu.async_copy(rows_hi, out_hbm.at[idx_hi], sem_hi)
                to_lo = pltpu.async_copy(rows_hi, out_hbm.at[idx_lo], sem_lo)
                to_hi.wait()
                to_lo.wait()
            else:
                from_hi = pltpu.async_copy(src_hbm.at[idx_hi], rows_hi, sem_hi)
                from_lo = pltpu.async_copy(src_hbm.at[idx_lo], rows_lo, sem_lo)
                from_hi.wait()
                pltpu.sync_copy(rows_hi, out_hi_hbm.at[window])
                from_lo.wait()
                pltpu.sync_copy(rows_lo, out_lo_hbm.at[window])

    index_vec, row_buf = pltpu.VMEM((SC_WINDOW,), jnp.int32), pltpu.VMEM((SC_WINDOW, d), U32)
    return pl.kernel(body, mesh=mesh, out_type=out_type,
                     scratch_types=[index_vec, index_vec, row_buf, row_buf,
                                    pltpu.SemaphoreType.DMA, pltpu.SemaphoreType.DMA])


def _moe_experts_kernel(te_ref, na_ref, xs_ref, wg_hbm, wu_hbm, wd_hbm, o_ref,
                        cache_g, cache_u, cache_d, stage_g, stage_u, stage_d, sems, acc, *, tf):
    i = pl.program_id(0)
    e = te_ref[i]
    n_chunks = cache_g.shape[-1] // tf
    active = i < na_ref[0]
    new_expert = jnp.logical_or(i == 0, e != te_ref[jnp.maximum(i - 1, 0)])

    def chunk_copies(c, slot):
        cols = pl.ds(c * tf, tf)
        return (pltpu.make_async_copy(wg_hbm.at[e, :, cols], stage_g.at[slot], sems.at[0, slot]),
                pltpu.make_async_copy(wu_hbm.at[e, :, cols], stage_u.at[slot], sems.at[1, slot]),
                pltpu.make_async_copy(wd_hbm.at[e, cols, :], stage_d.at[slot], sems.at[2, slot]))

    def tile_ffn(load_weights):
        x = _unpack_halves(xs_ref[...]).astype(BF16)
        if load_weights:
            for cp in chunk_copies(0, 0):
                cp.start()
        for c in range(n_chunks):
            cols = slice(c * tf, (c + 1) * tf)
            if load_weights:
                slot = c % 2
                if c + 1 < n_chunks:
                    for cp in chunk_copies(c + 1, 1 - slot):
                        cp.start()
                for cp in chunk_copies(c, slot):
                    cp.wait()
                cache_g[:, cols] = stage_g[slot].astype(BF16)
                cache_u[:, cols] = stage_u[slot].astype(BF16)
                cache_d[cols, :] = stage_d[slot].astype(BF16)
            hid = _swiglu_hidden(x, cache_g[:, cols], cache_u[:, cols])
            part = jnp.dot(hid, cache_d[cols, :], preferred_element_type=F32)
            if c == 0:
                acc[...] = part
            else:
                acc[...] += part
        o_ref[...] = _pack_halves(acc[...])

    pl.when(jnp.logical_and(active, new_expert))(lambda: tile_ffn(True))
    pl.when(jnp.logical_and(active, jnp.logical_not(new_expert)))(lambda: tile_ffn(False))


def _moe_experts(xs, plan, wg, wu, wd, *, tm, tf_target=512):
    P = xs.shape[0]
    E, D, F = wg.shape
    tf = _pick_chunk(F, tf_target)
    tile = lambda i, te, na: (jnp.minimum(i, na[0] - 1), 0)
    in_hbm = pl.BlockSpec(memory_space=pl.ANY)
    grid_spec = pltpu.PrefetchScalarGridSpec(
        num_scalar_prefetch=2,
        grid=(P // tm,),
        in_specs=[pl.BlockSpec((tm, D // 2), tile), in_hbm, in_hbm, in_hbm],
        out_specs=pl.BlockSpec((tm, D // 2), tile),
        scratch_shapes=[pltpu.VMEM((D, F), BF16), pltpu.VMEM((D, F), BF16), pltpu.VMEM((F, D), BF16),
                        pltpu.VMEM((2, D, tf), F32), pltpu.VMEM((2, D, tf), F32), pltpu.VMEM((2, tf, D), F32),
                        pltpu.SemaphoreType.DMA((3, 2)), pltpu.VMEM((tm, D), F32)],
    )
    return pl.pallas_call(
        functools.partial(_moe_experts_kernel, tf=tf),
        grid_spec=grid_spec,
        out_shape=jax.ShapeDtypeStruct((P, D // 2), U32),
        compiler_params=_params("arbitrary", vmem_limit=MOE_VMEM_LIMIT),
        name="moe_experts",
    )(plan["tile_expert"], plan["n_active_tiles"], xs, wg, wu, wd)


def _moe_mix_kernel(*refs, final, natural_out, carried):
    refs = list(refs)
    o_ref = refs.pop()
    if carried:
        refs.pop()
    x_ref, yh_ref, yl_ref, g_ref = refs[:4]
    gfin_ref = refs[4] if final else None
    gates = _token_rows(g_ref)
    lane = lax.broadcasted_iota(jnp.int32, gates.shape, 1)
    g_hi = jnp.sum(jnp.where(lane == 0, gates, 0.0), axis=-1, keepdims=True)
    g_lo = jnp.sum(jnp.where(lane == 1, gates, 0.0), axis=-1, keepdims=True)
    y = g_hi * _unpack_halves(_token_rows(yh_ref)) + g_lo * _unpack_halves(_token_rows(yl_ref))
    out = _finish(_token_rows(x_ref), y, gfin_ref).reshape(x_ref.shape)
    o_ref[...] = pltpu.einshape("rid->ird", out) if natural_out else out


def _moe_mix(x, y_hi, y_lo, pair_gates, g_final, *, natural_out, first_row, partial_out, ti=32):
    B, R, SI, D = x.shape
    n_rows = y_hi.shape[0] // (R * SI)
    ti = min(ti, SI)
    final = g_final is not None
    seq = lambda c: pl.BlockSpec((None, R, ti, c), lambda b, s: (first_row + b, 0, s, 0))
    local = lambda c: pl.BlockSpec((None, R, ti, c), lambda b, s: (b, 0, s, 0))
    as_tokens = lambda t: t.reshape(n_rows, R, SI, t.shape[-1])
    in_specs = [seq(D), local(D // 2), local(D // 2), seq(LANES)]
    args = [x, as_tokens(y_hi), as_tokens(y_lo), pair_gates]
    if final:
        in_specs.append(pl.BlockSpec((1, D), lambda b, s: (0, 0)))
        args.append(g_final.reshape(1, D))
    aliases = {}
    if partial_out is not None:
        aliases = {len(args): 0}
        in_specs.append(pl.BlockSpec(memory_space=pl.ANY))
        args.append(partial_out)
    out_spec = pl.BlockSpec((None, ti, R, D), lambda b, s: (first_row + b, s, 0, 0)) if natural_out else seq(D)
    return pl.pallas_call(
        functools.partial(_moe_mix_kernel, final=final, natural_out=natural_out, carried=partial_out is not None),
        grid=(n_rows, SI // ti), in_specs=in_specs, out_specs=out_spec,
        out_shape=jax.ShapeDtypeStruct((B, SI, R, D) if natural_out else (B, R, SI, D), F32),
        input_output_aliases=aliases,
        compiler_params=_params("parallel", "parallel"),
        name="moe_mix",
    )(*args)


def _moe(h, x, routing, wg, wu, wd, g_final, *, natural_out, tm=512):
    pair_gates, block_plan, block_counts = routing
    B, R, SI, half = h.shape
    N = B * R * SI
    plan = _plan_rows(block_plan, block_counts, (B, R, SI), tm=tm)
    rows_hi, rows_lo = plan["rows"]
    xs = _sc_rows_kernel(N, plan["p_max"], half, scatter=True)(h.reshape(N, half), rows_hi, rows_lo)
    ys = _moe_experts(xs, plan, wg, wu, wd, tm=tm)
    groups = MOE_RETURN_GROUPS if B % MOE_RETURN_GROUPS == 0 else 1
    n = N // groups
    gather = _sc_rows_kernel(n, plan["p_max"], half, scatter=False)
    fetched = [gather(ys, rows_hi[g * n:(g + 1) * n], rows_lo[g * n:(g + 1) * n]) for g in range(groups)]
    out = None
    for g, (y_hi, y_lo) in enumerate(fetched):
        out = _moe_mix(x, y_hi, y_lo, pair_gates, g_final, natural_out=natural_out,
                       first_row=g * (B // groups), partial_out=out)
    return out


def kernel(x, norm_mix, w_in, w_pool, pool_scale, attn_gain, w_out, norm_ffn, ffn_wg, ffn_wu, ffn_wd,
           w_router, moe_wg, moe_wu, moe_wd, final_norm):
    B, S, D = x.shape
    depth = norm_mix.shape[0]
    bf = lambda t: t.astype(BF16)
    w_in, w_pool, w_out = bf(w_in), bf(w_pool), bf(w_out)
    ffn_wg, ffn_wu, ffn_wd = bf(ffn_wg), bf(ffn_wu), bf(ffn_wd)
    R = RESIDUES
    assert S % R == 0
    N, SI = B * S, S // R
    x = x.reshape(B, SI, R, D)
    for l in range(depth):
        first, last = l == 0, l == depth - 1
        ya, q, k, v, q32, k32, v32 = _mix_in(x, norm_mix[l], w_in[l], w_pool[l], pool_scale[l], natural_x=first)
        narrow = lambda dil: (R // dil) * 16 > WINDOW_STEPS
        branches = [_attn_branch(q32, k32, v32, dil) if narrow(dil) else _attn_branch(q, k, v, dil)
                    for _, dil in DILATED_PATTERNS]
        i, routed = l // 2, l % 2 == 1
        x1, h2, *routing = _mix_out(x, ya, [o for o, _ in branches], [lse for _, lse in branches],
                                    attn_gain[l], w_out[l], norm_ffn[l], w_router[i] if routed else None,
                                    natural_x=first)
        g_final = final_norm if last else None
        if routed:
            x = _moe(h2, x1, routing, moe_wg[i], moe_wu[i], moe_wd[i], g_final, natural_out=last)
        else:
            x = _ffn(h2.reshape(N, D), x1.reshape(N, D), ffn_wg[i], ffn_wu[i], ffn_wd[i], g_final)
            x = x.reshape(B, R, SI, D)
            if last:
                x = x.transpose(0, 2, 1, 3)
    return x.reshape(B, S, D)
```

```python
import functools

import jax
import jax.numpy as jnp
import numpy as np
from jax import lax
from jax.experimental import pallas as pl
from jax.experimental.pallas import tpu as pltpu
from jax.experimental.pallas import tpu_sc as plsc

F32 = jnp.float32
BF16 = jnp.bfloat16
U32 = jnp.uint32
HIGH_HALF = np.uint32(0xFFFF0000)

EPS = 1e-6
LANES = 128
HEAD_DIM = 64
POOL_WINDOWS = (2, 4, 8, 16)
POOL_HIST = 8
DILATED_PATTERNS = ((128, 1), (512, 4), (2048, 16))
WINDOW_STEPS = 128
RESIDUES = 16
ATTN_STEP_ROWS = 1024
ATTN_UNROLL_QUERIES = 1024
LOG2_E = 1.4426950408889634
SC_CORES, SC_SUBCORES = 2, 16
SC_WORKERS = SC_CORES * SC_SUBCORES
SC_WINDOW = 64
MOE_RETURN_GROUPS = 2
assert all(w // d == WINDOW_STEPS and RESIDUES % d == 0 for w, d in DILATED_PATTERNS)
TOP_K = 2
MASKED = -1e30
VMEM_LIMIT = 48 * 1024 * 1024
MOE_VMEM_LIMIT = 56 * 1024 * 1024


def _rms(x, g):
    return x * lax.rsqrt(jnp.mean(x * x, axis=-1, keepdims=True) + EPS) * g


def _params(*sem, vmem_limit=VMEM_LIMIT):
    return pltpu.CompilerParams(dimension_semantics=sem, vmem_limit_bytes=vmem_limit)


def _token_rows(ref, natural=False):
    t = ref[...]
    if natural:
        t = pltpu.einshape("ird->rid", t)
    return t.reshape(-1, t.shape[-1])


def _pack_halves(t):
    bits = lax.bitcast_convert_type(t, U32)
    bits = (bits + np.uint32(0x7FFF) + ((bits >> 16) & np.uint32(1))) & HIGH_HALF
    half = t.shape[-1] // 2
    return (bits[:, :half] >> 16) | bits[:, half:]


def _unpack_halves(words):
    halves = [lax.bitcast_convert_type(w, F32) for w in (words << 16, words & HIGH_HALF)]
    return jnp.concatenate(halves, axis=-1)


def _mix_in_kernel(x_ref, g_ref, w_ref, wp_ref, ps_ref, ya_ref, q_ref, k_ref, v_ref, q32_ref, k32_ref, v32_ref,
                   ubuf, uprev, *, d_pool, d_attn, ti, natural_x):
    s = pl.program_id(1)
    R = RESIDUES
    blk = lambda t: t.reshape(R, ti, t.shape[-1])
    h = _rms(_token_rows(x_ref, natural_x), g_ref[...]).astype(BF16)
    u = jnp.dot(h, w_ref[:, :d_pool], preferred_element_type=F32)
    scale = HEAD_DIM ** -0.5 * LOG2_E
    for n, (ref, ref32) in enumerate(zip((q_ref, k_ref, v_ref), (q32_ref, k32_ref, v32_ref))):
        t = jnp.dot(h, w_ref[:, d_pool + n * d_attn:d_pool + (n + 1) * d_attn], preferred_element_type=F32)
        t = t * scale if n == 0 else t
        ref[...] = blk(t.astype(BF16))
        ref32[...] = blk(t)

    history = ubuf[:, POOL_HIST - 1:POOL_HIST, :]
    ubuf[:, POOL_HIST - 1:POOL_HIST, :] = jnp.where(s == 0, 0.0, history)
    ubuf[:, POOL_HIST:POOL_HIST + ti, :] = blk(u)
    uprev[...] = ubuf[:, POOL_HIST - 1:POOL_HIST - 1 + ti, :]
    at_start = (s * ti + lax.broadcasted_iota(jnp.int32, (ti, 1), 0)) == 0
    group = d_pool // len(POOL_WINDOWS)
    zs = []
    for gi, w in enumerate(POOL_WINDOWS):
        cols = slice(gi * group, (gi + 1) * group)
        ds = []
        for r in range(R):
            ug = ubuf[r, POOL_HIST:POOL_HIST + ti, cols]
            win = ug
            for back in range(1, w):
                rr = r - back
                win = win + (ubuf[rr, POOL_HIST:POOL_HIST + ti, cols] if rr >= 0 else uprev[rr + R, :, cols])
            cnt = jnp.where(at_start, float(min(r + 1, w)), float(w))
            ds.append(win / cnt - ug)
        d = jnp.concatenate(ds, axis=0).astype(BF16)
        zs.append(jnp.dot(d, wp_ref[gi], preferred_element_type=F32))
    z = jnp.concatenate(zs, axis=-1)
    ya_ref[...] = blk(_rms(z, ps_ref[...]).astype(BF16))
    ubuf[:, POOL_HIST - 1:POOL_HIST, :] = ubuf[:, POOL_HIST + ti - 1:POOL_HIST + ti, :]


def _mix_in(x, g, w_in, w_pool, pool_scale, *, natural_x, ti=32):
    B, R, SI, D = x.shape
    if natural_x:
        R, SI = SI, R
    d_pool = pool_scale.shape[-1]
    d_in = w_in.shape[-1]
    d_attn = (d_in - d_pool) // 3
    ti = min(ti, SI)
    assert R == RESIDUES >= max(POOL_WINDOWS) and SI % ti == 0 and ti % 16 == 0
    assert d_pool % (LANES * len(POOL_WINDOWS)) == 0
    seq_spec = lambda c: pl.BlockSpec((None, R, ti, c), lambda b, s: (b, 0, s, 0))
    full = lambda shape: pl.BlockSpec(shape, lambda b, s: (0,) * len(shape))
    out_sds = lambda c, dtype=BF16: jax.ShapeDtypeStruct((B, R, SI, c), dtype)
    return pl.pallas_call(
        functools.partial(_mix_in_kernel, d_pool=d_pool, d_attn=d_attn, ti=ti, natural_x=natural_x),
        grid=(B, SI // ti),
        in_specs=[pl.BlockSpec((None, ti, R, D), lambda b, s: (b, s, 0, 0)) if natural_x else seq_spec(D),
                  full((1, D)), full((D, d_in)), full(w_pool.shape), full((1, d_pool))],
        out_specs=[seq_spec(d_pool)] + [seq_spec(d_attn)] * 6,
        out_shape=[out_sds(d_pool)] + [out_sds(d_attn)] * 3 + [out_sds(d_attn, F32)] * 3,
        scratch_shapes=[pltpu.VMEM((R, POOL_HIST + ti, d_pool), F32), pltpu.VMEM((R, ti, d_pool), F32)],
        compiler_params=_params("parallel", "arbitrary"),
        name="mix_in",
    )(x, g.reshape(1, D), w_in, w_pool, pool_scale.reshape(1, d_pool))


def _attn_kernel(q_ref, k_ref, v_ref, o_ref, lse_ref, bias_ref, *, nq, strips, d_attn):
    _, classes, chunk, _ = q_ref.shape
    L = nq // strips
    nk, n_blocks = 2 * nq, chunk // L
    first_block = pl.program_id(2) * n_blocks
    row = lax.broadcasted_iota(jnp.int32, (nq, 1), 0)
    col = lax.broadcasted_iota(jnp.int32, (1, nk), 1)
    q_strip, q_row = row >> (L.bit_length() - 1), row & (L - 1)
    k_strip, k_row = col >> ((2 * L).bit_length() - 1), col & (2 * L - 1)
    back = strips * (q_row - k_row) + (q_strip - k_strip)

    def band(offset):
        rel = back + strips * offset
        return jnp.where((rel >= 0) & (rel <= WINDOW_STEPS), 0.0, MASKED).astype(F32)

    bias_ref[0] = band(0)
    bias_ref[1] = band(L)
    lane = lax.broadcasted_iota(jnp.int32, (nq, LANES), 1)
    lo_half = lane < HEAD_DIM

    def block(it, carry):
        g, j_local = it // n_blocks, it % n_blocks
        j = first_block + j_local
        q_rows = pl.ds(pl.multiple_of(j_local * L, L), L)
        k_rows = pl.ds(pl.multiple_of(jnp.maximum(j - 1, 0) * L, L), 2 * L)
        bias = bias_ref[jnp.minimum(j, 1)]
        lse_tile = jnp.zeros((nq, LANES), F32)
        for hp in range(d_attn // LANES):
            cols = slice(hp * LANES, (hp + 1) * LANES)
            qp = q_ref[:, g, q_rows, cols].reshape(nq, LANES).astype(BF16)
            kk = k_ref[:, g, k_rows, cols].reshape(nk, LANES).astype(BF16)
            vv = v_ref[:, g, k_rows, cols].reshape(nk, LANES).astype(BF16)
            outs, lses = [], []
            for sub in range(2):
                keep = lo_half if sub == 0 else jnp.logical_not(lo_half)
                qm = jnp.where(keep, qp, jnp.zeros_like(qp))
                s = lax.dot_general(qm, kk, (((1,), (1,)), ((), ())), preferred_element_type=F32) + bias
                m = jnp.max(s, axis=-1, keepdims=True)
                p = jnp.exp2(s - m)
                l = jnp.sum(p, axis=-1, keepdims=True)
                o = jnp.dot(p.astype(BF16), vv, preferred_element_type=F32)
                outs.append(o / l)
                lses.append(m + jnp.log2(l))
            o_pair = jnp.where(lo_half, outs[0], outs[1]).astype(o_ref.dtype)
            o_ref[:, g, q_rows, cols] = o_pair.reshape(strips, L, LANES)
            pair_lse = jnp.where(lane == 2 * hp, lses[0], lses[1])
            lse_tile = jnp.where((lane >> 1) == hp, pair_lse, lse_tile)
        lse_ref[:, g, q_rows, :] = lse_tile.reshape(strips, L, LANES)
        return carry

    lax.fori_loop(0, classes * n_blocks, block, 0, unroll=ATTN_UNROLL_QUERIES // nq)


def _attn_branch(q, k, v, dil):
    B, R, SI, C = q.shape
    strips = R // dil
    row_tile = 8 * 4 // q.dtype.itemsize
    nq = max(WINDOW_STEPS, strips * row_tile)
    L = nq // strips
    classes = max(1, min(dil, ATTN_STEP_ROWS // (strips * SI)))
    chunk = min(SI, max(L, ATTN_STEP_ROWS // (strips * classes)))
    assert R % dil == 0 and SI % chunk == 0 and chunk % L == 0 and SI >= 2 * L and dil % classes == 0
    assert C % LANES == 0 and C // HEAD_DIM <= LANES and nq & (nq - 1) == 0 and L & (L - 1) == 0
    view = lambda t: t.reshape(B, strips, dil, SI, t.shape[-1])
    q_spec = lambda c: pl.BlockSpec((None, strips, classes, chunk, c), lambda b, g, s: (b, 0, g, s, 0))
    kv_spec = pl.BlockSpec((None, strips, classes, SI, C), lambda b, g, s: (b, 0, g, 0, 0),
                           pipeline_mode=pl.Buffered(1 if SI > chunk else 2))
    o, lse = pl.pallas_call(
        functools.partial(_attn_kernel, nq=nq, strips=strips, d_attn=C),
        scratch_shapes=[pltpu.VMEM((2, nq, 2 * nq), F32)],
        grid=(B, dil // classes, SI // chunk),
        in_specs=[q_spec(C), kv_spec, kv_spec],
        out_specs=[q_spec(C), q_spec(LANES)],
        out_shape=[jax.ShapeDtypeStruct((B, strips, dil, SI, C), q.dtype),
                   jax.ShapeDtypeStruct((B, strips, dil, SI, LANES), F32)],
        compiler_params=_params("parallel", "parallel", "arbitrary"),
        name=f"attn_d{dil}",
    )(view(q), view(k), view(v))
    return o.reshape(B, R, SI, C), lse.reshape(B, R, SI, LANES)


def _route(h, wr, earlier, n_experts):
    h_hi = h.astype(BF16)
    h_lo = (h - h_hi.astype(F32)).astype(BF16)
    both = jnp.dot(h_hi, wr, preferred_element_type=F32)
    logits = both[:, :LANES] + both[:, LANES:] + jnp.dot(h_lo, wr[:, :LANES], preferred_element_type=F32)
    rows = h.shape[0]
    scores = logits.T[:n_experts]
    expert = lax.broadcasted_iota(jnp.int32, scores.shape, 0)
    picks = []
    for _ in range(TOP_K):
        m = jnp.max(scores, axis=0, keepdims=True)
        idx = jnp.min(jnp.where(scores == m, expert, n_experts), axis=0, keepdims=True)
        picks.append((m, idx))
        scores = jnp.where(expert == idx, -jnp.inf, scores)
    (m1, i1), (m2, i2) = picks
    e2 = jnp.exp(m2 - m1)
    g1 = 1.0 / (1.0 + e2)
    g2 = e2 / (1.0 + e2)
    picked = jnp.where(jnp.logical_or(expert == i1, expert == i2), 1.0, 0.0)
    rank = jnp.dot(picked.astype(BF16), earlier, preferred_element_type=F32)
    first_is_hi = i1 > i2
    e_hi, e_lo = jnp.maximum(i1, i2), jnp.minimum(i1, i2)
    rank_of = lambda e: jnp.sum(jnp.where(expert == e, rank, 0.0), axis=0, keepdims=True).astype(jnp.int32)
    plan = jnp.concatenate([e_hi, rank_of(e_hi), e_lo, rank_of(e_lo), jnp.zeros((4, rows), jnp.int32)], axis=0)
    gate_rows = jnp.concatenate([jnp.where(first_is_hi, g1, g2), jnp.where(first_is_hi, g2, g1),
                                 jnp.zeros((LANES - 2, rows), F32)], axis=0)
    return gate_rows.T, plan, jnp.sum(picked, axis=1, keepdims=True)


def _mix_out_kernel(*refs, d_pool, natural_x, n_experts):
    if n_experts:
        (x_ref, ya_ref, o1_ref, o2_ref, o3_ref, l1_ref, l2_ref, l3_ref, gain_ref, wo_ref, gffn_ref, exp_ref,
         wr_ref, earlier_ref, x1_ref, h2_ref, gates_ref, plan_ref, counts_ref) = refs
    else:
        (x_ref, ya_ref, o1_ref, o2_ref, o3_ref, l1_ref, l2_ref, l3_ref, gain_ref, wo_ref, gffn_ref, exp_ref,
         x1_ref, h2_ref) = refs
    lses = [_token_rows(l) for l in (l1_ref, l2_ref, l3_ref)]
    top = jnp.maximum(jnp.maximum(lses[0], lses[1]), lses[2])
    es = [jnp.exp2(l - top) for l in lses]
    den = es[0] + es[1] + es[2]
    expand = exp_ref[...]

    def per_lane(w):
        hi = w.astype(BF16)
        lo = (w - hi.astype(F32)).astype(BF16)
        return jnp.dot(jnp.concatenate([hi, lo], axis=1), expand, preferred_element_type=F32)

    o = 0.0
    for e, o_ref in zip(es, (o1_ref, o2_ref, o3_ref)):
        o = o + per_lane(e / den) * _token_rows(o_ref).astype(F32)
    yb = _rms(o, gain_ref[...]).astype(BF16)
    y = (jnp.dot(_token_rows(ya_ref), wo_ref[:d_pool, :], preferred_element_type=F32)
         + jnp.dot(yb, wo_ref[d_pool:, :], preferred_element_type=F32))
    x1 = _token_rows(x_ref, natural_x) + y
    x1_ref[...] = x1.reshape(x1_ref.shape)
    h2 = _rms(x1, gffn_ref[...])
    if n_experts:
        gate_tile, plan, counts = _route(h2, wr_ref[...], earlier_ref[...], n_experts)
        gates_ref[...] = gate_tile.reshape(gates_ref.shape)
        plan_ref[...] = plan
        counts_ref[...] = jnp.broadcast_to(counts, counts_ref.shape)
        h2_ref[...] = _pack_halves(h2).reshape(h2_ref.shape)
    else:
        h2_ref[...] = h2.astype(h2_ref.dtype).reshape(h2_ref.shape)


def _mix_out(x, ya, outs, lses, attn_gain, w_out, g_ffn, w_router, *, natural_x, ti=32):
    B, R, SI, d_pool = ya.shape
    D = x.shape[-1]
    d_attn = outs[0].shape[-1]
    ti = min(ti, SI)
    head_of_lane = jnp.arange(d_attn) // HEAD_DIM
    expand = (jnp.arange(2 * LANES)[:, None] % LANES == head_of_lane[None, :]).astype(BF16)
    seq = lambda c: pl.BlockSpec((None, R, ti, c), lambda b, s: (b, 0, s, 0))
    x_spec = pl.BlockSpec((None, ti, R, D), lambda b, s: (b, s, 0, 0)) if natural_x else seq(D)
    full = lambda shape: pl.BlockSpec(shape, lambda b, s: (0,) * len(shape))
    tokens = lambda c, dtype: jax.ShapeDtypeStruct((B, R, SI, c), dtype)
    in_specs = ([x_spec, seq(d_pool)] + [seq(d_attn)] * 3 + [seq(LANES)] * 3
                + [full((1, d_attn)), full(w_out.shape), full((1, D)), full(expand.shape)])
    args = [x, ya, *outs, *lses, attn_gain.reshape(1, d_attn), w_out, g_ffn.reshape(1, D), expand]
    if w_router is None:
        n_experts = 0
        out_specs, out_shape = [seq(D), seq(D)], [tokens(D, F32), tokens(D, BF16)]
    else:
        n_experts = w_router.shape[-1]
        assert 2 * n_experts <= LANES and TOP_K == 2
        wr = jnp.zeros((D, LANES), F32).at[:, :n_experts].set(w_router)
        wr_hi = wr.astype(BF16)
        rows = R * ti
        earlier = (jnp.arange(rows)[:, None] < jnp.arange(rows)[None, :]).astype(BF16)
        in_specs += [full((D, 2 * LANES)), full((rows, rows))]
        args += [jnp.concatenate([wr_hi, (wr - wr_hi.astype(F32)).astype(BF16)], axis=1), earlier]
        per_block = lambda r, c: pl.BlockSpec((None, None, r, c), lambda b, s: (b, s, 0, 0))
        out_specs = [seq(D), seq(D // 2), seq(LANES), per_block(8, rows), per_block(n_experts, LANES)]
        out_shape = [tokens(D, F32), tokens(D // 2, U32), tokens(LANES, F32),
                     jax.ShapeDtypeStruct((B, SI // ti, 8, rows), jnp.int32),
                     jax.ShapeDtypeStruct((B, SI // ti, n_experts, LANES), F32)]
    return pl.pallas_call(
        functools.partial(_mix_out_kernel, d_pool=d_pool, natural_x=natural_x, n_experts=n_experts),
        grid=(B, SI // ti),
        in_specs=in_specs, out_specs=out_specs, out_shape=out_shape,
        compiler_params=_params("parallel", "parallel"),
        name="mix_out",
    )(*args)


def _swiglu_hidden(h, wg, wu):
    a = jnp.dot(h, wg, preferred_element_type=F32)
    b = jnp.dot(h, wu, preferred_element_type=F32)
    return (a * jax.nn.sigmoid(a) * b).astype(BF16)


def _finish(x, y, gfin_ref):
    out = x + y
    return out if gfin_ref is None else _rms(out, gfin_ref[...])


def _ffn_kernel(*refs, final, tf):
    if final:
        h_ref, x_ref, wg_ref, wu_ref, wd_ref, gfin_ref, o_ref, hid = refs
    else:
        (h_ref, x_ref, wg_ref, wu_ref, wd_ref, o_ref, hid), gfin_ref = refs, None
    h = h_ref[...]
    for c in range(wg_ref.shape[-1] // tf):
        cols = slice(c * tf, (c + 1) * tf)
        hid[:, cols] = _swiglu_hidden(h, wg_ref[:, cols], wu_ref[:, cols])
    y = jnp.dot(hid[...], wd_ref[...], preferred_element_type=F32)
    o_ref[...] = _finish(x_ref[...], y, gfin_ref)


def _pick_chunk(n, target):
    best = None
    for c in range(LANES, min(n, target) + 1, LANES):
        if n % c == 0:
            best = c
    assert best is not None
    return best


def _ffn(h, x, wg, wu, wd, g_final, *, tm=512, tf_target=256):
    N, D = x.shape
    F = wg.shape[-1]
    tm = min(tm, N)
    tf = _pick_chunk(F, tf_target)
    final = g_final is not None
    row = pl.BlockSpec((tm, D), lambda i: (i, 0))
    resident = lambda shape: pl.BlockSpec(shape, lambda i: (0, 0), pipeline_mode=pl.Buffered(1))
    in_specs = [row, row, resident((D, F)), resident((D, F)), resident((F, D))]
    args = [h, x, wg, wu, wd]
    if final:
        in_specs.append(pl.BlockSpec((1, D), lambda i: (0, 0)))
        args.append(g_final.reshape(1, D))
    return pl.pallas_call(
        functools.partial(_ffn_kernel, final=final, tf=tf),
        grid=(N // tm,),
        in_specs=in_specs,
        out_specs=row,
        out_shape=jax.ShapeDtypeStruct((N, D), F32),
        scratch_shapes=[pltpu.VMEM((tm, F), BF16)],
        compiler_params=_params("parallel"),
        name="ffn_dense",
    )(*args)


def _plan_rows(block_plan, block_counts, token_shape, *, tm):
    B, R, SI = token_shape
    i32 = jnp.int32
    _, per_row, _, rows = block_plan.shape
    E = block_counts.shape[2]
    p_max = TOP_K * B * R * SI + E * tm
    counts = block_counts[..., 0].astype(i32).reshape(-1, E)
    block_start = jnp.cumsum(counts, axis=0) - counts
    seg_len = (jnp.sum(counts, axis=0) + tm - 1) // tm * tm
    seg_end = jnp.cumsum(seg_len)
    base = (seg_end - seg_len)[None, :] + block_start
    plan = block_plan.reshape(-1, 8, rows)
    experts = jnp.arange(E, dtype=i32)[None, None, :]

    def sorted_rows(expert, rank):
        row = jnp.sum(jnp.where(expert[:, :, None] == experts, base[:, None, :], 0), axis=-1) + rank
        return row.reshape(B, per_row, R, rows // R).transpose(0, 2, 1, 3).reshape(-1)

    tile_start = jnp.arange(p_max // tm, dtype=i32) * tm
    tile_expert = jnp.minimum(jnp.sum(seg_end[None, :] <= tile_start[:, None], axis=1), E - 1).astype(i32)
    return dict(rows=(sorted_rows(plan[:, 0], plan[:, 1]), sorted_rows(plan[:, 2], plan[:, 3])),
                tile_expert=tile_expert, n_active_tiles=(seg_end[-1] // tm).reshape(1).astype(i32), p_max=p_max)


def _sc_rows_kernel(n_rows, n_out, d, scatter):
    per_worker = n_rows // SC_WORKERS
    assert n_rows % (SC_WORKERS * SC_WINDOW) == 0
    mesh = plsc.VectorSubcoreMesh(core_axis_name="c", subcore_axis_name="s")
    out_type = (jax.ShapeDtypeStruct((n_out, d), U32) if scatter
                else [jax.ShapeDtypeStruct((n_rows, d), U32)] * 2)

    def body(*refs):
        if scatter:
            src_hbm, hi_hbm, lo_hbm, out_hbm, idx_hi, idx_lo, rows_hi, rows_lo, sem_hi, sem_lo = refs
        else:
            src_hbm, hi_hbm, lo_hbm, out_hi_hbm, out_lo_hbm, idx_hi, idx_lo, rows_hi, rows_lo, sem_hi, sem_lo = refs
        worker = lax.axis_index("s") * SC_CORES + lax.axis_index("c")

        @pl.loop(0, per_worker // SC_WINDOW)
        def _(i):
            window = pl.ds(worker * per_worker + i * SC_WINDOW, SC_WINDOW)
            pltpu.sync_copy(hi_hbm.at[window], idx_hi)
            pltpu.sync_copy(lo_hbm.at[window], idx_lo)
            if scatter:
                pltpu.sync_copy(src_hbm.at[window], rows_hi)
                to_hi = pltpu.async_copy(rows_hi, out_hbm.at[idx_hi], sem_hi)
                to_lo = pltpu.async_copy(rows_hi, out_hbm.at[idx_lo], sem_lo)
                to_hi.wait()
                to_lo.wait()
            else:
                from_hi = pltpu.async_copy(src_hbm.at[idx_hi], rows_hi, sem_hi)
                from_lo = pltpu.async_copy(src_hbm.at[idx_lo], rows_lo, sem_lo)
                from_hi.wait()
                pltpu.sync_copy(rows_hi, out_hi_hbm.at[window])
                from_lo.wait()
                pltpu.sync_copy(rows_lo, out_lo_hbm.at[window])

    index_vec, row_buf = pltpu.VMEM((SC_WINDOW,), jnp.int32), pltpu.VMEM((SC_WINDOW, d), U32)
    return pl.kernel(body, mesh=mesh, out_type=out_type,
                     scratch_types=[index_vec, index_vec, row_buf, row_buf,
                                    pltpu.SemaphoreType.DMA, pltpu.SemaphoreType.DMA])


def _moe_experts_kernel(te_ref, na_ref, xs_ref, wg_hbm, wu_hbm, wd_hbm, o_ref,
                        cache_g, cache_u, cache_d, stage_g, stage_u, stage_d, sems, acc, *, tf):
    i = pl.program_id(0)
    e = te_ref[i]
    n_chunks = cache_g.shape[-1] // tf
    active = i < na_ref[0]
    new_expert = jnp.logical_or(i == 0, e != te_ref[jnp.maximum(i - 1, 0)])

    def chunk_copies(c, slot):
        cols = pl.ds(c * tf, tf)
        return (pltpu.make_async_copy(wg_hbm.at[e, :, cols], stage_g.at[slot], sems.at[0, slot]),
                pltpu.make_async_copy(wu_hbm.at[e, :, cols], stage_u.at[slot], sems.at[1, slot]),
                pltpu.make_async_copy(wd_hbm.at[e, cols, :], stage_d.at[slot], sems.at[2, slot]))

    def tile_ffn(load_weights):
        x = _unpack_halves(xs_ref[...]).astype(BF16)
        if load_weights:
            for cp in chunk_copies(0, 0):
                cp.start()
        for c in range(n_chunks):
            cols = slice(c * tf, (c + 1) * tf)
            if load_weights:
                slot = c % 2
                if c + 1 < n_chunks:
                    for cp in chunk_copies(c + 1, 1 - slot):
                        cp.start()
                for cp in chunk_copies(c, slot):
                    cp.wait()
                cache_g[:, cols] = stage_g[slot].astype(BF16)
                cache_u[:, cols] = stage_u[slot].astype(BF16)
                cache_d[cols, :] = stage_d[slot].astype(BF16)
            hid = _swiglu_hidden(x, cache_g[:, cols], cache_u[:, cols])
            part = jnp.dot(hid, cache_d[cols, :], preferred_element_type=F32)
            if c == 0:
                acc[...] = part
            else:
                acc[...] += part
        o_ref[...] = _pack_halves(acc[...])

    pl.when(jnp.logical_and(active, new_expert))(lambda: tile_ffn(True))
    pl.when(jnp.logical_and(active, jnp.logical_not(new_expert)))(lambda: tile_ffn(False))


def _moe_experts(xs, plan, wg, wu, wd, *, tm, tf_target=512):
    P = xs.shape[0]
    E, D, F = wg.shape
    tf = _pick_chunk(F, tf_target)
    tile = lambda i, te, na: (jnp.minimum(i, na[0] - 1), 0)
    in_hbm = pl.BlockSpec(memory_space=pl.ANY)
    grid_spec = pltpu.PrefetchScalarGridSpec(
        num_scalar_prefetch=2,
        grid=(P // tm,),
        in_specs=[pl.BlockSpec((tm, D // 2), tile), in_hbm, in_hbm, in_hbm],
        out_specs=pl.BlockSpec((tm, D // 2), tile),
        scratch_shapes=[pltpu.VMEM((D, F), BF16), pltpu.VMEM((D, F), BF16), pltpu.VMEM((F, D), BF16),
                        pltpu.VMEM((2, D, tf), F32), pltpu.VMEM((2, D, tf), F32), pltpu.VMEM((2, tf, D), F32),
                        pltpu.SemaphoreType.DMA((3, 2)), pltpu.VMEM((tm, D), F32)],
    )
    return pl.pallas_call(
        functools.partial(_moe_experts_kernel, tf=tf),
        grid_spec=grid_spec,
        out_shape=jax.ShapeDtypeStruct((P, D // 2), U32),
        compiler_params=_params("arbitrary", vmem_limit=MOE_VMEM_LIMIT),
        name="moe_experts",
    )(plan["tile_expert"], plan["n_active_tiles"], xs, wg, wu, wd)


def _moe_mix_kernel(*refs, final, natural_out, carried):
    refs = list(refs)
    o_ref = refs.pop()
    if carried:
        refs.pop()
    x_ref, yh_ref, yl_ref, g_ref = refs[:4]
    gfin_ref = refs[4] if final else None
    gates = _token_rows(g_ref)
    lane = lax.broadcasted_iota(jnp.int32, gates.shape, 1)
    g_hi = jnp.sum(jnp.where(lane == 0, gates, 0.0), axis=-1, keepdims=True)
    g_lo = jnp.sum(jnp.where(lane == 1, gates, 0.0), axis=-1, keepdims=True)
    y = g_hi * _unpack_halves(_token_rows(yh_ref)) + g_lo * _unpack_halves(_token_rows(yl_ref))
    out = _finish(_token_rows(x_ref), y, gfin_ref).reshape(x_ref.shape)
    o_ref[...] = pltpu.einshape("rid->ird", out) if natural_out else out


def _moe_mix(x, y_hi, y_lo, pair_gates, g_final, *, natural_out, first_row, partial_out, ti=32):
    B, R, SI, D = x.shape
    n_rows = y_hi.shape[0] // (R * SI)
    ti = min(ti, SI)
    final = g_final is not None
    seq = lambda c: pl.BlockSpec((None, R, ti, c), lambda b, s: (first_row + b, 0, s, 0))
    local = lambda c: pl.BlockSpec((None, R, ti, c), lambda b, s: (b, 0, s, 0))
    as_tokens = lambda t: t.reshape(n_rows, R, SI, t.shape[-1])
    in_specs = [seq(D), local(D // 2), local(D // 2), seq(LANES)]
    args = [x, as_tokens(y_hi), as_tokens(y_lo), pair_gates]
    if final:
        in_specs.append(pl.BlockSpec((1, D), lambda b, s: (0, 0)))
        args.append(g_final.reshape(1, D))
    aliases = {}
    if partial_out is not None:
        aliases = {len(args): 0}
        in_specs.append(pl.BlockSpec(memory_space=pl.ANY))
        args.append(partial_out)
    out_spec = pl.BlockSpec((None, ti, R, D), lambda b, s: (first_row + b, s, 0, 0)) if natural_out else seq(D)
    return pl.pallas_call(
        functools.partial(_moe_mix_kernel, final=final, natural_out=natural_out, carried=partial_out is not None),
        grid=(n_rows, SI // ti), in_specs=in_specs, out_specs=out_spec,
        out_shape=jax.ShapeDtypeStruct((B, SI, R, D) if natural_out else (B, R, SI, D), F32),
        input_output_aliases=aliases,
        compiler_params=_params("parallel", "parallel"),
        name="moe_mix",
    )(*args)


def _moe(h, x, routing, wg, wu, wd, g_final, *, natural_out, tm=512):
    pair_gates, block_plan, block_counts = routing
    B, R, SI, half = h.shape
    N = B * R * SI
    plan = _plan_rows(block_plan, block_counts, (B, R, SI), tm=tm)
    rows_hi, rows_lo = plan["rows"]
    xs = _sc_rows_kernel(N, plan["p_max"], half, scatter=True)(h.reshape(N, half), rows_hi, rows_lo)
    ys = _moe_experts(xs, plan, wg, wu, wd, tm=tm)
    groups = MOE_RETURN_GROUPS if B % MOE_RETURN_GROUPS == 0 else 1
    n = N // groups
    gather = _sc_rows_kernel(n, plan["p_max"], half, scatter=False)
    fetched = [gather(ys, rows_hi[g * n:(g + 1) * n], rows_lo[g * n:(g + 1) * n]) for g in range(groups)]
    out = None
    for g, (y_hi, y_lo) in enumerate(fetched):
        out = _moe_mix(x, y_hi, y_lo, pair_gates, g_final, natural_out=natural_out,
                       first_row=g * (B // groups), partial_out=out)
    return out


def kernel(x, norm_mix, w_in, w_pool, pool_scale, attn_gain, w_out, norm_ffn, ffn_wg, ffn_wu, ffn_wd,
           w_router, moe_wg, moe_wu, moe_wd, final_norm):
    B, S, D = x.shape
    depth = norm_mix.shape[0]
    bf = lambda t: t.astype(BF16)
    w_in, w_pool, w_out = bf(w_in), bf(w_pool), bf(w_out)
    ffn_wg, ffn_wu, ffn_wd = bf(ffn_wg), bf(ffn_wu), bf(ffn_wd)
    R = RESIDUES
    assert S % R == 0
    N, SI = B * S, S // R
    x = x.reshape(B, SI, R, D)
    for l in range(depth):
        first, last = l == 0, l == depth - 1
        ya, q, k, v, q32, k32, v32 = _mix_in(x, norm_mix[l], w_in[l], w_pool[l], pool_scale[l], natural_x=first)
        narrow = lambda dil: (R // dil) * 16 > WINDOW_STEPS
        branches = [_attn_branch(q32, k32, v32, dil) if narrow(dil) else _attn_branch(q, k, v, dil)
                    for _, dil in DILATED_PATTERNS]
        i, routed = l // 2, l % 2 == 1
        x1, h2, *routing = _mix_out(x, ya, [o for o, _ in branches], [lse for _, lse in branches],
                                    attn_gain[l], w_out[l], norm_ffn[l], w_router[i] if routed else None,
                                    natural_x=first)
        g_final = final_norm if last else None
        if routed:
            x = _moe(h2, x1, routing, moe_wg[i], moe_wu[i], moe_wd[i], g_final, natural_out=last)
        else:
            x = _ffn(h2.reshape(N, D), x1.reshape(N, D), ffn_wg[i], ffn_wu[i], ffn_wd[i], g_final)
            x = x.reshape(B, R, SI, D)
            if last:
                x = x.transpose(0, 2, 1, 3)
    return x.reshape(B, S, D)
```

```python
import functools

import jax
import jax.numpy as jnp
import numpy as np
from jax import lax
from jax.experimental import pallas as pl
from jax.experimental.pallas import tpu as pltpu
from jax.experimental.pallas import tpu_sc as plsc

F32 = jnp.float32
BF16 = jnp.bfloat16
U32 = jnp.uint32
HIGH_HALF = np.uint32(0xFFFF0000)

EPS = 1e-6
LANES = 128
HEAD_DIM = 64
POOL_WINDOWS = (2, 4, 8, 16)
POOL_HIST = 8
DILATED_PATTERNS = ((128, 1), (512, 4), (2048, 16))
WINDOW_STEPS = 128
RESIDUES = 16
ATTN_STEP_ROWS = 1024
ATTN_UNROLL_QUERIES = 1024
LOG2_E = 1.4426950408889634
SC_CORES, SC_SUBCORES = 2, 16
SC_WORKERS = SC_CORES * SC_SUBCORES
SC_SCATTER_WINDOW = 128
SC_GATHER_WINDOW = 64
MOE_RETURN_GROUPS = 2
assert all(w // d == WINDOW_STEPS and RESIDUES % d == 0 for w, d in DILATED_PATTERNS)
TOP_K = 2
MASKED = -1e30
VMEM_LIMIT = 48 * 1024 * 1024
MOE_VMEM_LIMIT = 56 * 1024 * 1024


def _rms(x, g):
    return x * lax.rsqrt(jnp.mean(x * x, axis=-1, keepdims=True) + EPS) * g


def _params(*sem, vmem_limit=VMEM_LIMIT):
    return pltpu.CompilerParams(dimension_semantics=sem, vmem_limit_bytes=vmem_limit)


def _token_rows(ref, natural=False):
    t = ref[...]
    if natural:
        t = pltpu.einshape("ird->rid", t)
    return t.reshape(-1, t.shape[-1])


def _pack_halves(t):
    bits = lax.bitcast_convert_type(t, U32)
    bits = (bits + np.uint32(0x7FFF) + ((bits >> 16) & np.uint32(1))) & HIGH_HALF
    half = t.shape[-1] // 2
    return (bits[:, :half] >> 16) | bits[:, half:]


def _unpack_halves(words):
    halves = [lax.bitcast_convert_type(w, F32) for w in (words << 16, words & HIGH_HALF)]
    return jnp.concatenate(halves, axis=-1)


def _mix_in_kernel(x_ref, g_ref, w_ref, wp_ref, ps_ref, ya_ref, q_ref, k_ref, v_ref, q32_ref, k32_ref, v32_ref,
                   ubuf, uprev, *, d_pool, d_attn, ti, natural_x):
    s = pl.program_id(1)
    R = RESIDUES
    blk = lambda t: t.reshape(R, ti, t.shape[-1])
    h = _rms(_token_rows(x_ref, natural_x), g_ref[...]).astype(BF16)
    u = jnp.dot(h, w_ref[:, :d_pool], preferred_element_type=F32)
    scale = HEAD_DIM ** -0.5 * LOG2_E
    for n, (ref, ref32) in enumerate(zip((q_ref, k_ref, v_ref), (q32_ref, k32_ref, v32_ref))):
        t = jnp.dot(h, w_ref[:, d_pool + n * d_attn:d_pool + (n + 1) * d_attn], preferred_element_type=F32)
        t = t * scale if n == 0 else t
        ref[...] = blk(t.astype(BF16))
        ref32[...] = blk(t)

    history = ubuf[:, POOL_HIST - 1:POOL_HIST, :]
    ubuf[:, POOL_HIST - 1:POOL_HIST, :] = jnp.where(s == 0, 0.0, history)
    ubuf[:, POOL_HIST:POOL_HIST + ti, :] = blk(u)
    uprev[...] = ubuf[:, POOL_HIST - 1:POOL_HIST - 1 + ti, :]
    at_start = (s * ti + lax.broadcasted_iota(jnp.int32, (ti, 1), 0)) == 0
    group = d_pool // len(POOL_WINDOWS)
    zs = []
    for gi, w in enumerate(POOL_WINDOWS):
        cols = slice(gi * group, (gi + 1) * group)
        ds = []
        for r in range(R):
            ug = ubuf[r, POOL_HIST:POOL_HIST + ti, cols]
            win = ug
            for back in range(1, w):
                rr = r - back
                win = win + (ubuf[rr, POOL_HIST:POOL_HIST + ti, cols] if rr >= 0 else uprev[rr + R, :, cols])
            cnt = jnp.where(at_start, float(min(r + 1, w)), float(w))
            ds.append(win / cnt - ug)
        d = jnp.concatenate(ds, axis=0).astype(BF16)
        zs.append(jnp.dot(d, wp_ref[gi], preferred_element_type=F32))
    z = jnp.concatenate(zs, axis=-1)
    ya_ref[...] = blk(_rms(z, ps_ref[...]).astype(BF16))
    ubuf[:, POOL_HIST - 1:POOL_HIST, :] = ubuf[:, POOL_HIST + ti - 1:POOL_HIST + ti, :]


def _mix_in(x, g, w_in, w_pool, pool_scale, *, natural_x, ti=32):
    B, R, SI, D = x.shape
    if natural_x:
        R, SI = SI, R
    d_pool = pool_scale.shape[-1]
    d_in = w_in.shape[-1]
    d_attn = (d_in - d_pool) // 3
    ti = min(ti, SI)
    assert R == RESIDUES >= max(POOL_WINDOWS) and SI % ti == 0 and ti % 16 == 0
    assert d_pool % (LANES * len(POOL_WINDOWS)) == 0
    seq_spec = lambda c: pl.BlockSpec((None, R, ti, c), lambda b, s: (b, 0, s, 0))
    full = lambda shape: pl.BlockSpec(shape, lambda b, s: (0,) * len(shape))
    out_sds = lambda c, dtype=BF16: jax.ShapeDtypeStruct((B, R, SI, c), dtype)
    return pl.pallas_call(
        functools.partial(_mix_in_kernel, d_pool=d_pool, d_attn=d_attn, ti=ti, natural_x=natural_x),
        grid=(B, SI // ti),
        in_specs=[pl.BlockSpec((None, ti, R, D), lambda b, s: (b, s, 0, 0)) if natural_x else seq_spec(D),
                  full((1, D)), full((D, d_in)), full(w_pool.shape), full((1, d_pool))],
        out_specs=[seq_spec(d_pool)] + [seq_spec(d_attn)] * 6,
        out_shape=[out_sds(d_pool)] + [out_sds(d_attn)] * 3 + [out_sds(d_attn, F32)] * 3,
        scratch_shapes=[pltpu.VMEM((R, POOL_HIST + ti, d_pool), F32), pltpu.VMEM((R, ti, d_pool), F32)],
        compiler_params=_params("parallel", "arbitrary"),
        name="mix_in",
    )(x, g.reshape(1, D), w_in, w_pool, pool_scale.reshape(1, d_pool))


def _attn_kernel(q_ref, k_ref, v_ref, o_ref, lse_ref, bias_ref, *, nq, strips, d_attn):
    _, classes, chunk, _ = q_ref.shape
    L = nq // strips
    nk, n_blocks = 2 * nq, chunk // L
    first_block = pl.program_id(2) * n_blocks
    row = lax.broadcasted_iota(jnp.int32, (nq, 1), 0)
    col = lax.broadcasted_iota(jnp.int32, (1, nk), 1)
    q_strip, q_row = row >> (L.bit_length() - 1), row & (L - 1)
    k_strip, k_row = col >> ((2 * L).bit_length() - 1), col & (2 * L - 1)
    back = strips * (q_row - k_row) + (q_strip - k_strip)

    def band(offset):
        rel = back + strips * offset
        return jnp.where((rel >= 0) & (rel <= WINDOW_STEPS), 0.0, MASKED).astype(F32)

    bias_ref[0] = band(0)
    bias_ref[1] = band(L)
    lane = lax.broadcasted_iota(jnp.int32, (nq, LANES), 1)
    lo_half = lane < HEAD_DIM

    def block(it, carry):
        g, j_local = it // n_blocks, it % n_blocks
        j = first_block + j_local
        q_rows = pl.ds(pl.multiple_of(j_local * L, L), L)
        k_rows = pl.ds(pl.multiple_of(jnp.maximum(j - 1, 0) * L, L), 2 * L)
        bias = bias_ref[jnp.minimum(j, 1)]
        lse_tile = jnp.zeros((nq, LANES), F32)
        for hp in range(d_attn // LANES):
            cols = slice(hp * LANES, (hp + 1) * LANES)
            qp = q_ref[:, g, q_rows, cols].reshape(nq, LANES).astype(BF16)
            kk = k_ref[:, g, k_rows, cols].reshape(nk, LANES).astype(BF16)
            vv = v_ref[:, g, k_rows, cols].reshape(nk, LANES).astype(BF16)
            outs, lses = [], []
            for sub in range(2):
                keep = lo_half if sub == 0 else jnp.logical_not(lo_half)
                qm = jnp.where(keep, qp, jnp.zeros_like(qp))
                s = lax.dot_general(qm, kk, (((1,), (1,)), ((), ())), preferred_element_type=F32) + bias
                m = jnp.max(s, axis=-1, keepdims=True)
                p = jnp.exp2(s - m)
                l = jnp.sum(p, axis=-1, keepdims=True)
                o = jnp.dot(p.astype(BF16), vv, preferred_element_type=F32)
                outs.append(o / l)
                lses.append(m + jnp.log2(l))
            o_pair = jnp.where(lo_half, outs[0], outs[1]).astype(o_ref.dtype)
            o_ref[:, g, q_rows, cols] = o_pair.reshape(strips, L, LANES)
            pair_lse = jnp.where(lane == 2 * hp, lses[0], lses[1])
            lse_tile = jnp.where((lane >> 1) == hp, pair_lse, lse_tile)
        lse_ref[:, g, q_rows, :] = lse_tile.reshape(strips, L, LANES)
        return carry

    lax.fori_loop(0, classes * n_blocks, block, 0, unroll=ATTN_UNROLL_QUERIES // nq)


def _attn_branch(q, k, v, dil):
    B, R, SI, C = q.shape
    strips = R // dil
    row_tile = 8 * 4 // q.dtype.itemsize
    nq = max(WINDOW_STEPS, strips * row_tile)
    L = nq // strips
    classes = max(1, min(dil, ATTN_STEP_ROWS // (strips * SI)))
    chunk = min(SI, max(L, ATTN_STEP_ROWS // (strips * classes)))
    assert R % dil == 0 and SI % chunk == 0 and chunk % L == 0 and SI >= 2 * L and dil % classes == 0
    assert C % LANES == 0 and C // HEAD_DIM <= LANES and nq & (nq - 1) == 0 and L & (L - 1) == 0
    view = lambda t: t.reshape(B, strips, dil, SI, t.shape[-1])
    q_spec = lambda c: pl.BlockSpec((None, strips, classes, chunk, c), lambda b, g, s: (b, 0, g, s, 0))
    kv_spec = pl.BlockSpec((None, strips, classes, SI, C), lambda b, g, s: (b, 0, g, 0, 0),
                           pipeline_mode=pl.Buffered(1 if SI > chunk else 2))
    o, lse = pl.pallas_call(
        functools.partial(_attn_kernel, nq=nq, strips=strips, d_attn=C),
        scratch_shapes=[pltpu.VMEM((2, nq, 2 * nq), F32)],
        grid=(B, dil // classes, SI // chunk),
        in_specs=[q_spec(C), kv_spec, kv_spec],
        out_specs=[q_spec(C), q_spec(LANES)],
        out_shape=[jax.ShapeDtypeStruct((B, strips, dil, SI, C), q.dtype),
                   jax.ShapeDtypeStruct((B, strips, dil, SI, LANES), F32)],
        compiler_params=_params("parallel", "parallel", "arbitrary"),
        name=f"attn_d{dil}",
    )(view(q), view(k), view(v))
    return o.reshape(B, R, SI, C), lse.reshape(B, R, SI, LANES)


def _route(h, wr, earlier, n_experts):
    h_hi = h.astype(BF16)
    h_lo = (h - h_hi.astype(F32)).astype(BF16)
    both = jnp.dot(h_hi, wr, preferred_element_type=F32)
    logits = both[:, :LANES] + both[:, LANES:] + jnp.dot(h_lo, wr[:, :LANES], preferred_element_type=F32)
    rows = h.shape[0]
    scores = logits.T[:n_experts]
    expert = lax.broadcasted_iota(jnp.int32, scores.shape, 0)
    picks = []
    for _ in range(TOP_K):
        m = jnp.max(scores, axis=0, keepdims=True)
        idx = jnp.min(jnp.where(scores == m, expert, n_experts), axis=0, keepdims=True)
        picks.append((m, idx))
        scores = jnp.where(expert == idx, -jnp.inf, scores)
    (m1, i1), (m2, i2) = picks
    e2 = jnp.exp(m2 - m1)
    g1 = 1.0 / (1.0 + e2)
    g2 = e2 / (1.0 + e2)
    picked = jnp.where(jnp.logical_or(expert == i1, expert == i2), 1.0, 0.0)
    rank = jnp.dot(picked.astype(BF16), earlier, preferred_element_type=F32)
    first_is_hi = i1 > i2
    e_hi, e_lo = jnp.maximum(i1, i2), jnp.minimum(i1, i2)
    rank_of = lambda e: jnp.sum(jnp.where(expert == e, rank, 0.0), axis=0, keepdims=True).astype(jnp.int32)
    plan = jnp.concatenate([e_hi, rank_of(e_hi), e_lo, rank_of(e_lo), jnp.zeros((4, rows), jnp.int32)], axis=0)
    gate_rows = jnp.concatenate([jnp.where(first_is_hi, g1, g2), jnp.where(first_is_hi, g2, g1),
                                 jnp.zeros((LANES - 2, rows), F32)], axis=0)
    return gate_rows.T, plan, jnp.sum(picked, axis=1, keepdims=True)


def _mix_out_kernel(*refs, d_pool, natural_x, n_experts):
    if n_experts:
        (x_ref, ya_ref, o1_ref, o2_ref, o3_ref, l1_ref, l2_ref, l3_ref, gain_ref, wo_ref, gffn_ref, exp_ref,
         wr_ref, earlier_ref, x1_ref, h2_ref, gates_ref, plan_ref, counts_ref) = refs
    else:
        (x_ref, ya_ref, o1_ref, o2_ref, o3_ref, l1_ref, l2_ref, l3_ref, gain_ref, wo_ref, gffn_ref, exp_ref,
         x1_ref, h2_ref) = refs
    lses = [_token_rows(l) for l in (l1_ref, l2_ref, l3_ref)]
    top = jnp.maximum(jnp.maximum(lses[0], lses[1]), lses[2])
    es = [jnp.exp2(l - top) for l in lses]
    den = es[0] + es[1] + es[2]
    expand = exp_ref[...]

    def per_lane(w):
        hi = w.astype(BF16)
        lo = (w - hi.astype(F32)).astype(BF16)
        return jnp.dot(jnp.concatenate([hi, lo], axis=1), expand, preferred_element_type=F32)

    o = 0.0
    for e, o_ref in zip(es, (o1_ref, o2_ref, o3_ref)):
        o = o + per_lane(e / den) * _token_rows(o_ref).astype(F32)
    yb = _rms(o, gain_ref[...]).astype(BF16)
    y = (jnp.dot(_token_rows(ya_ref), wo_ref[:d_pool, :], preferred_element_type=F32)
         + jnp.dot(yb, wo_ref[d_pool:, :], preferred_element_type=F32))
    x1 = _token_rows(x_ref, natural_x) + y
    x1_ref[...] = x1.reshape(x1_ref.shape)
    h2 = _rms(x1, gffn_ref[...])
    if n_experts:
        gate_tile, plan, counts = _route(h2, wr_ref[...], earlier_ref[...], n_experts)
        gates_ref[...] = gate_tile.reshape(gates_ref.shape)
        plan_ref[...] = plan
        counts_ref[...] = jnp.broadcast_to(counts, counts_ref.shape)
        h2_ref[...] = _pack_halves(h2).reshape(h2_ref.shape)
    else:
        h2_ref[...] = h2.astype(h2_ref.dtype).reshape(h2_ref.shape)


def _mix_out(x, ya, outs, lses, attn_gain, w_out, g_ffn, w_router, *, natural_x, ti=32):
    B, R, SI, d_pool = ya.shape
    D = x.shape[-1]
    d_attn = outs[0].shape[-1]
    ti = min(ti, SI)
    head_of_lane = jnp.arange(d_attn) // HEAD_DIM
    expand = (jnp.arange(2 * LANES)[:, None] % LANES == head_of_lane[None, :]).astype(BF16)
    seq = lambda c: pl.BlockSpec((None, R, ti, c), lambda b, s: (b, 0, s, 0))
    x_spec = pl.BlockSpec((None, ti, R, D), lambda b, s: (b, s, 0, 0)) if natural_x else seq(D)
    full = lambda shape: pl.BlockSpec(shape, lambda b, s: (0,) * len(shape))
    tokens = lambda c, dtype: jax.ShapeDtypeStruct((B, R, SI, c), dtype)
    in_specs = ([x_spec, seq(d_pool)] + [seq(d_attn)] * 3 + [seq(LANES)] * 3
                + [full((1, d_attn)), full(w_out.shape), full((1, D)), full(expand.shape)])
    args = [x, ya, *outs, *lses, attn_gain.reshape(1, d_attn), w_out, g_ffn.reshape(1, D), expand]
    if w_router is None:
        n_experts = 0
        out_specs, out_shape = [seq(D), seq(D)], [tokens(D, F32), tokens(D, BF16)]
    else:
        n_experts = w_router.shape[-1]
        assert 2 * n_experts <= LANES and TOP_K == 2
        wr = jnp.zeros((D, LANES), F32).at[:, :n_experts].set(w_router)
        wr_hi = wr.astype(BF16)
        rows = R * ti
        earlier = (jnp.arange(rows)[:, None] < jnp.arange(rows)[None, :]).astype(BF16)
        in_specs += [full((D, 2 * LANES)), full((rows, rows))]
        args += [jnp.concatenate([wr_hi, (wr - wr_hi.astype(F32)).astype(BF16)], axis=1), earlier]
        per_block = lambda r, c: pl.BlockSpec((None, None, r, c), lambda b, s: (b, s, 0, 0))
        out_specs = [seq(D), seq(D // 2), seq(LANES), per_block(8, rows), per_block(n_experts, LANES)]
        out_shape = [tokens(D, F32), tokens(D // 2, U32), tokens(LANES, F32),
                     jax.ShapeDtypeStruct((B, SI // ti, 8, rows), jnp.int32),
                     jax.ShapeDtypeStruct((B, SI // ti, n_experts, LANES), F32)]
    return pl.pallas_call(
        functools.partial(_mix_out_kernel, d_pool=d_pool, natural_x=natural_x, n_experts=n_experts),
        grid=(B, SI // ti),
        in_specs=in_specs, out_specs=out_specs, out_shape=out_shape,
        compiler_params=_params("parallel", "parallel"),
        name="mix_out",
    )(*args)


def _swiglu_hidden(h, wg, wu):
    a = jnp.dot(h, wg, preferred_element_type=F32)
    b = jnp.dot(h, wu, preferred_element_type=F32)
    return (a * jax.nn.sigmoid(a) * b).astype(BF16)


def _finish(x, y, gfin_ref):
    out = x + y
    return out if gfin_ref is None else _rms(out, gfin_ref[...])


def _ffn_kernel(*refs, final, tf):
    if final:
        h_ref, x_ref, wg_ref, wu_ref, wd_ref, gfin_ref, o_ref, hid = refs
    else:
        (h_ref, x_ref, wg_ref, wu_ref, wd_ref, o_ref, hid), gfin_ref = refs, None
    h = h_ref[...]
    for c in range(wg_ref.shape[-1] // tf):
        cols = slice(c * tf, (c + 1) * tf)
        hid[:, cols] = _swiglu_hidden(h, wg_ref[:, cols], wu_ref[:, cols])
    y = jnp.dot(hid[...], wd_ref[...], preferred_element_type=F32)
    o_ref[...] = _finish(x_ref[...], y, gfin_ref)


def _pick_chunk(n, target):
    best = None
    for c in range(LANES, min(n, target) + 1, LANES):
        if n % c == 0:
            best = c
    assert best is not None
    return best


def _ffn(h, x, wg, wu, wd, g_final, *, tm=512, tf_target=256):
    N, D = x.shape
    F = wg.shape[-1]
    tm = min(tm, N)
    tf = _pick_chunk(F, tf_target)
    final = g_final is not None
    row = pl.BlockSpec((tm, D), lambda i: (i, 0))
    resident = lambda shape: pl.BlockSpec(shape, lambda i: (0, 0), pipeline_mode=pl.Buffered(1))
    in_specs = [row, row, resident((D, F)), resident((D, F)), resident((F, D))]
    args = [h, x, wg, wu, wd]
    if final:
        in_specs.append(pl.BlockSpec((1, D), lambda i: (0, 0)))
        args.append(g_final.reshape(1, D))
    return pl.pallas_call(
        functools.partial(_ffn_kernel, final=final, tf=tf),
        grid=(N // tm,),
        in_specs=in_specs,
        out_specs=row,
        out_shape=jax.ShapeDtypeStruct((N, D), F32),
        scratch_shapes=[pltpu.VMEM((tm, F), BF16)],
        compiler_params=_params("parallel"),
        name="ffn_dense",
    )(*args)


def _plan_rows(block_plan, block_counts, token_shape, *, tm):
    B, R, SI = token_shape
    i32 = jnp.int32
    _, per_row, _, rows = block_plan.shape
    E = block_counts.shape[2]
    p_max = TOP_K * B * R * SI + E * tm
    counts = block_counts[..., 0].astype(i32).reshape(-1, E)
    block_start = jnp.cumsum(counts, axis=0) - counts
    seg_len = (jnp.sum(counts, axis=0) + tm - 1) // tm * tm
    seg_end = jnp.cumsum(seg_len)
    base = (seg_end - seg_len)[None, :] + block_start
    plan = block_plan.reshape(-1, 8, rows)
    experts = jnp.arange(E, dtype=i32)[None, None, :]

    def sorted_rows(expert, rank):
        row = jnp.sum(jnp.where(expert[:, :, None] == experts, base[:, None, :], 0), axis=-1) + rank
        return row.reshape(B, per_row, R, rows // R).transpose(0, 2, 1, 3).reshape(-1)

    tile_start = jnp.arange(p_max // tm, dtype=i32) * tm
    tile_expert = jnp.minimum(jnp.sum(seg_end[None, :] <= tile_start[:, None], axis=1), E - 1).astype(i32)
    return dict(rows=(sorted_rows(plan[:, 0], plan[:, 1]), sorted_rows(plan[:, 2], plan[:, 3])),
                tile_expert=tile_expert, n_active_tiles=(seg_end[-1] // tm).reshape(1).astype(i32), p_max=p_max)


def _sc_rows_kernel(n_rows, n_out, d, scatter):
    window_rows = SC_SCATTER_WINDOW if scatter else SC_GATHER_WINDOW
    per_worker = n_rows // SC_WORKERS
    assert n_rows % (SC_WORKERS * window_rows) == 0
    mesh = plsc.VectorSubcoreMesh(core_axis_name="c", subcore_axis_name="s")
    out_type = (jax.ShapeDtypeStruct((n_out, d), U32) if scatter
                else [jax.ShapeDtypeStruct((n_rows, d), U32)] * 2)

    def body(*refs):
        if scatter:
            src_hbm, hi_hbm, lo_hbm, out_hbm, idx_hi, idx_lo, rows, sem_hi, sem_lo = refs
        else:
            (src_hbm, hi_hbm, lo_hbm, out_hi_hbm, out_lo_hbm, idx_hi, idx_lo, rows_hi, rows_lo,
             sem_hi, sem_lo, sem_out_hi, sem_out_lo) = refs
        worker = lax.axis_index("s") * SC_CORES + lax.axis_index("c")

        @pl.loop(0, per_worker // window_rows)
        def _(i):
            window = pl.ds(worker * per_worker + i * window_rows, window_rows)
            pltpu.sync_copy(hi_hbm.at[window], idx_hi)
            pltpu.sync_copy(lo_hbm.at[window], idx_lo)
            if scatter:
                pltpu.sync_copy(src_hbm.at[window], rows)
                to_hi = pltpu.async_copy(rows, out_hbm.at[idx_hi], sem_hi)
                to_lo = pltpu.async_copy(rows, out_hbm.at[idx_lo], sem_lo)
                to_hi.wait()
                to_lo.wait()
            else:
                from_hi = pltpu.async_copy(src_hbm.at[idx_hi], rows_hi, sem_hi)
                from_lo = pltpu.async_copy(src_hbm.at[idx_lo], rows_lo, sem_lo)
                from_hi.wait()
                out_hi = pltpu.async_copy(rows_hi, out_hi_hbm.at[window], sem_out_hi)
                from_lo.wait()
                out_lo = pltpu.async_copy(rows_lo, out_lo_hbm.at[window], sem_out_lo)
                out_hi.wait()
                out_lo.wait()

    index_vec, row_buf = pltpu.VMEM((window_rows,), jnp.int32), pltpu.VMEM((window_rows, d), U32)
    dma_sem = pltpu.SemaphoreType.DMA
    scratch = ([index_vec, index_vec, row_buf, dma_sem, dma_sem] if scatter
               else [index_vec, index_vec, row_buf, row_buf, dma_sem, dma_sem, dma_sem, dma_sem])
    return pl.kernel(body, mesh=mesh, out_type=out_type, scratch_types=scratch)


def _moe_experts_kernel(te_ref, na_ref, xs_ref, wg_hbm, wu_hbm, wd_hbm, o_ref,
                        cache_g, cache_u, cache_d, stage_g, stage_u, stage_d, sems, acc, *, tf):
    i = pl.program_id(0)
    e = te_ref[i]
    n_chunks = cache_g.shape[-1] // tf
    active = i < na_ref[0]
    new_expert = jnp.logical_or(i == 0, e != te_ref[jnp.maximum(i - 1, 0)])

    def chunk_copies(c, slot):
        cols = pl.ds(c * tf, tf)
        return (pltpu.make_async_copy(wg_hbm.at[e, :, cols], stage_g.at[slot], sems.at[0, slot]),
                pltpu.make_async_copy(wu_hbm.at[e, :, cols], stage_u.at[slot], sems.at[1, slot]),
                pltpu.make_async_copy(wd_hbm.at[e, cols, :], stage_d.at[slot], sems.at[2, slot]))

    def tile_ffn(load_weights):
        x = _unpack_halves(xs_ref[...]).astype(BF16)
        if load_weights:
            for cp in chunk_copies(0, 0):
                cp.start()
        for c in range(n_chunks):
            cols = slice(c * tf, (c + 1) * tf)
            if load_weights:
                slot = c % 2
                if c + 1 < n_chunks:
                    for cp in chunk_copies(c + 1, 1 - slot):
                        cp.start()
                for cp in chunk_copies(c, slot):
                    cp.wait()
                cache_g[:, cols] = stage_g[slot].astype(BF16)
                cache_u[:, cols] = stage_u[slot].astype(BF16)
                cache_d[cols, :] = stage_d[slot].astype(BF16)
            hid = _swiglu_hidden(x, cache_g[:, cols], cache_u[:, cols])
            part = jnp.dot(hid, cache_d[cols, :], preferred_element_type=F32)
            if c == 0:
                acc[...] = part
            else:
                acc[...] += part
        o_ref[...] = _pack_halves(acc[...])

    pl.when(jnp.logical_and(active, new_expert))(lambda: tile_ffn(True))
    pl.when(jnp.logical_and(active, jnp.logical_not(new_expert)))(lambda: tile_ffn(False))


def _moe_experts(xs, plan, wg, wu, wd, *, tm, tf_target=512):
    P = xs.shape[0]
    E, D, F = wg.shape
    tf = _pick_chunk(F, tf_target)
    tile = lambda i, te, na: (jnp.minimum(i, na[0] - 1), 0)
    in_hbm = pl.BlockSpec(memory_space=pl.ANY)
    grid_spec = pltpu.PrefetchScalarGridSpec(
        num_scalar_prefetch=2,
        grid=(P // tm,),
        in_specs=[pl.BlockSpec((tm, D // 2), tile), in_hbm, in_hbm, in_hbm],
        out_specs=pl.BlockSpec((tm, D // 2), tile),
        scratch_shapes=[pltpu.VMEM((D, F), BF16), pltpu.VMEM((D, F), BF16), pltpu.VMEM((F, D), BF16),
                        pltpu.VMEM((2, D, tf), F32), pltpu.VMEM((2, D, tf), F32), pltpu.VMEM((2, tf, D), F32),
                        pltpu.SemaphoreType.DMA((3, 2)), pltpu.VMEM((tm, D), F32)],
    )
    return pl.pallas_call(
        functools.partial(_moe_experts_kernel, tf=tf),
        grid_spec=grid_spec,
        out_shape=jax.ShapeDtypeStruct((P, D // 2), U32),
        compiler_params=_params("arbitrary", vmem_limit=MOE_VMEM_LIMIT),
        name="moe_experts",
    )(plan["tile_expert"], plan["n_active_tiles"], xs, wg, wu, wd)


def _moe_mix_kernel(*refs, final, natural_out, carried):
    refs = list(refs)
    o_ref = refs.pop()
    if carried:
        refs.pop()
    x_ref, yh_ref, yl_ref, g_ref = refs[:4]
    gfin_ref = refs[4] if final else None
    gates = _token_rows(g_ref)
    lane = lax.broadcasted_iota(jnp.int32, gates.shape, 1)
    g_hi = jnp.sum(jnp.where(lane == 0, gates, 0.0), axis=-1, keepdims=True)
    g_lo = jnp.sum(jnp.where(lane == 1, gates, 0.0), axis=-1, keepdims=True)
    y = g_hi * _unpack_halves(_token_rows(yh_ref)) + g_lo * _unpack_halves(_token_rows(yl_ref))
    out = _finish(_token_rows(x_ref), y, gfin_ref).reshape(x_ref.shape)
    o_ref[...] = pltpu.einshape("rid->ird", out) if natural_out else out


def _moe_mix(x, y_hi, y_lo, pair_gates, g_final, *, natural_out, first_row, partial_out, ti=32):
    B, R, SI, D = x.shape
    n_rows = y_hi.shape[0] // (R * SI)
    ti = min(ti, SI)
    final = g_final is not None
    seq = lambda c: pl.BlockSpec((None, R, ti, c), lambda b, s: (first_row + b, 0, s, 0))
    local = lambda c: pl.BlockSpec((None, R, ti, c), lambda b, s: (b, 0, s, 0))
    as_tokens = lambda t: t.reshape(n_rows, R, SI, t.shape[-1])
    in_specs = [seq(D), local(D // 2), local(D // 2), seq(LANES)]
    args = [x, as_tokens(y_hi), as_tokens(y_lo), pair_gates]
    if final:
        in_specs.append(pl.BlockSpec((1, D), lambda b, s: (0, 0)))
        args.append(g_final.reshape(1, D))
    aliases = {}
    if partial_out is not None:
        aliases = {len(args): 0}
        in_specs.append(pl.BlockSpec(memory_space=pl.ANY))
        args.append(partial_out)
    out_spec = pl.BlockSpec((None, ti, R, D), lambda b, s: (first_row + b, s, 0, 0)) if natural_out else seq(D)
    return pl.pallas_call(
        functools.partial(_moe_mix_kernel, final=final, natural_out=natural_out, carried=partial_out is not None),
        grid=(n_rows, SI // ti), in_specs=in_specs, out_specs=out_spec,
        out_shape=jax.ShapeDtypeStruct((B, SI, R, D) if natural_out else (B, R, SI, D), F32),
        input_output_aliases=aliases,
        compiler_params=_params("parallel", "parallel"),
        name="moe_mix",
    )(*args)


def _moe(h, x, routing, wg, wu, wd, g_final, *, natural_out, tm=512):
    pair_gates, block_plan, block_counts = routing
    B, R, SI, half = h.shape
    N = B * R * SI
    plan = _plan_rows(block_plan, block_counts, (B, R, SI), tm=tm)
    rows_hi, rows_lo = plan["rows"]
    xs = _sc_rows_kernel(N, plan["p_max"], half, scatter=True)(h.reshape(N, half), rows_hi, rows_lo)
    ys = _moe_experts(xs, plan, wg, wu, wd, tm=tm)
    groups = MOE_RETURN_GROUPS if B % MOE_RETURN_GROUPS == 0 else 1
    n = N // groups
    gather = _sc_rows_kernel(n, plan["p_max"], half, scatter=False)
    fetched = [gather(ys, rows_hi[g * n:(g + 1) * n], rows_lo[g * n:(g + 1) * n]) for g in range(groups)]
    out = None
    for g, (y_hi, y_lo) in enumerate(fetched):
        out = _moe_mix(x, y_hi, y_lo, pair_gates, g_final, natural_out=natural_out,
                       first_row=g * (B // groups), partial_out=out)
    return out


def kernel(x, norm_mix, w_in, w_pool, pool_scale, attn_gain, w_out, norm_ffn, ffn_wg, ffn_wu, ffn_wd,
           w_router, moe_wg, moe_wu, moe_wd, final_norm):
    B, S, D = x.shape
    depth = norm_mix.shape[0]
    bf = lambda t: t.astype(BF16)
    w_in, w_pool, w_out = bf(w_in), bf(w_pool), bf(w_out)
    ffn_wg, ffn_wu, ffn_wd = bf(ffn_wg), bf(ffn_wu), bf(ffn_wd)
    R = RESIDUES
    assert S % R == 0
    N, SI = B * S, S // R
    x = x.reshape(B, SI, R, D)
    for l in range(depth):
        first, last = l == 0, l == depth - 1
        ya, q, k, v, q32, k32, v32 = _mix_in(x, norm_mix[l], w_in[l], w_pool[l], pool_scale[l], natural_x=first)
        narrow = lambda dil: (R // dil) * 16 > WINDOW_STEPS
        branches = [_attn_branch(q32, k32, v32, dil) if narrow(dil) else _attn_branch(q, k, v, dil)
                    for _, dil in DILATED_PATTERNS]
        i, routed = l // 2, l % 2 == 1
        x1, h2, *routing = _mix_out(x, ya, [o for o, _ in branches], [lse for _, lse in branches],
                                    attn_gain[l], w_out[l], norm_ffn[l], w_router[i] if routed else None,
                                    natural_x=first)
        g_final = final_norm if last else None
        if routed:
            x = _moe(h2, x1, routing, moe_wg[i], moe_wu[i], moe_wd[i], g_final, natural_out=last)
        else:
            x = _ffn(h2.reshape(N, D), x1.reshape(N, D), ffn_wg[i], ffn_wu[i], ffn_wd[i], g_final)
            x = x.reshape(B, R, SI, D)
            if last:
                x = x.transpose(0, 2, 1, 3)
    return x.reshape(B, S, D)
```

```python
import functools

import jax
import jax.numpy as jnp
import numpy as np
from jax import lax
from jax.experimental import pallas as pl
from jax.experimental.pallas import tpu as pltpu
from jax.experimental.pallas import tpu_sc as plsc

F32 = jnp.float32
BF16 = jnp.bfloat16
U32 = jnp.uint32
HIGH_HALF = np.uint32(0xFFFF0000)

EPS = 1e-6
LANES = 128
HEAD_DIM = 64
POOL_WINDOWS = (2, 4, 8, 16)
POOL_HIST = 8
DILATED_PATTERNS = ((128, 1), (512, 4), (2048, 16))
WINDOW_STEPS = 128
RESIDUES = 16
ATTN_STEP_ROWS = 1024
ATTN_UNROLL_QUERIES = 1024
LOG2_E = 1.4426950408889634
SC_CORES, SC_SUBCORES = 2, 16
SC_WORKERS = SC_CORES * SC_SUBCORES
SC_SCATTER_WINDOW = 128
SC_GATHER_WINDOW = 64
MOE_RETURN_GROUPS = 2
assert all(w // d == WINDOW_STEPS and RESIDUES % d == 0 for w, d in DILATED_PATTERNS)
TOP_K = 2
MASKED = -1e30
VMEM_LIMIT = 48 * 1024 * 1024
MOE_VMEM_LIMIT = 56 * 1024 * 1024


def _rms(x, g):
    return x * lax.rsqrt(jnp.mean(x * x, axis=-1, keepdims=True) + EPS) * g


def _params(*sem, vmem_limit=VMEM_LIMIT):
    return pltpu.CompilerParams(dimension_semantics=sem, vmem_limit_bytes=vmem_limit)


def _token_rows(ref, natural=False):
    t = ref[...]
    if natural:
        t = pltpu.einshape("ird->rid", t)
    return t.reshape(-1, t.shape[-1])


def _pack_halves(t):
    bits = lax.bitcast_convert_type(t, U32)
    bits = (bits + np.uint32(0x7FFF) + ((bits >> 16) & np.uint32(1))) & HIGH_HALF
    half = t.shape[-1] // 2
    return (bits[:, :half] >> 16) | bits[:, half:]


def _unpack_halves(words):
    halves = [lax.bitcast_convert_type(w, F32) for w in (words << 16, words & HIGH_HALF)]
    return jnp.concatenate(halves, axis=-1)


def _mix_in_kernel(x_ref, g_ref, w_ref, wp_ref, ps_ref, ya_ref, q_ref, k_ref, v_ref, q32_ref, k32_ref, v32_ref,
                   ubuf, uprev, *, d_pool, d_attn, ti, natural_x):
    s = pl.program_id(1)
    R = RESIDUES
    blk = lambda t: t.reshape(R, ti, t.shape[-1])
    h = _rms(_token_rows(x_ref, natural_x), g_ref[...]).astype(BF16)
    u = jnp.dot(h, w_ref[:, :d_pool], preferred_element_type=F32)
    scale = HEAD_DIM ** -0.5 * LOG2_E
    for n, (ref, ref32) in enumerate(zip((q_ref, k_ref, v_ref), (q32_ref, k32_ref, v32_ref))):
        t = jnp.dot(h, w_ref[:, d_pool + n * d_attn:d_pool + (n + 1) * d_attn], preferred_element_type=F32)
        t = t * scale if n == 0 else t
        ref[...] = blk(t.astype(BF16))
        ref32[...] = blk(t)

    history = ubuf[:, POOL_HIST - 1:POOL_HIST, :]
    ubuf[:, POOL_HIST - 1:POOL_HIST, :] = jnp.where(s == 0, 0.0, history)
    ubuf[:, POOL_HIST:POOL_HIST + ti, :] = blk(u)
    uprev[...] = ubuf[:, POOL_HIST - 1:POOL_HIST - 1 + ti, :]
    at_start = (s * ti + lax.broadcasted_iota(jnp.int32, (ti, 1), 0)) == 0
    group = d_pool // len(POOL_WINDOWS)
    zs = []
    for gi, w in enumerate(POOL_WINDOWS):
        cols = slice(gi * group, (gi + 1) * group)
        ds = []
        for r in range(R):
            ug = ubuf[r, POOL_HIST:POOL_HIST + ti, cols]
            win = ug
            for back in range(1, w):
                rr = r - back
                win = win + (ubuf[rr, POOL_HIST:POOL_HIST + ti, cols] if rr >= 0 else uprev[rr + R, :, cols])
            cnt = jnp.where(at_start, float(min(r + 1, w)), float(w))
            ds.append(win / cnt - ug)
        d = jnp.concatenate(ds, axis=0).astype(BF16)
        zs.append(jnp.dot(d, wp_ref[gi], preferred_element_type=F32))
    z = jnp.concatenate(zs, axis=-1)
    ya_ref[...] = blk(_rms(z, ps_ref[...]).astype(BF16))
    ubuf[:, POOL_HIST - 1:POOL_HIST, :] = ubuf[:, POOL_HIST + ti - 1:POOL_HIST + ti, :]


def _mix_in(x, g, w_in, w_pool, pool_scale, *, natural_x, ti=32):
    B, R, SI, D = x.shape
    if natural_x:
        R, SI = SI, R
    d_pool = pool_scale.shape[-1]
    d_in = w_in.shape[-1]
    d_attn = (d_in - d_pool) // 3
    ti = min(ti, SI)
    assert R == RESIDUES >= max(POOL_WINDOWS) and SI % ti == 0 and ti % 16 == 0
    assert d_pool % (LANES * len(POOL_WINDOWS)) == 0
    seq_spec = lambda c: pl.BlockSpec((None, R, ti, c), lambda b, s: (b, 0, s, 0))
    full = lambda shape: pl.BlockSpec(shape, lambda b, s: (0,) * len(shape))
    out_sds = lambda c, dtype=BF16: jax.ShapeDtypeStruct((B, R, SI, c), dtype)
    return pl.pallas_call(
        functools.partial(_mix_in_kernel, d_pool=d_pool, d_attn=d_attn, ti=ti, natural_x=natural_x),
        grid=(B, SI // ti),
        in_specs=[pl.BlockSpec((None, ti, R, D), lambda b, s: (b, s, 0, 0)) if natural_x else seq_spec(D),
                  full((1, D)), full((D, d_in)), full(w_pool.shape), full((1, d_pool))],
        out_specs=[seq_spec(d_pool)] + [seq_spec(d_attn)] * 6,
        out_shape=[out_sds(d_pool)] + [out_sds(d_attn)] * 3 + [out_sds(d_attn, F32)] * 3,
        scratch_shapes=[pltpu.VMEM((R, POOL_HIST + ti, d_pool), F32), pltpu.VMEM((R, ti, d_pool), F32)],
        compiler_params=_params("parallel", "arbitrary"),
        name="mix_in",
    )(x, g.reshape(1, D), w_in, w_pool, pool_scale.reshape(1, d_pool))


def _attn_kernel(q_ref, k_ref, v_ref, o_ref, stat_ref, bias_ref, *, nq, strips, d_attn):
    _, classes, chunk, _ = q_ref.shape
    L = nq // strips
    nk, n_blocks = 2 * nq, chunk // L
    first_block = pl.program_id(2) * n_blocks
    row = lax.broadcasted_iota(jnp.int32, (nq, 1), 0)
    col = lax.broadcasted_iota(jnp.int32, (1, nk), 1)
    q_strip, q_row = row >> (L.bit_length() - 1), row & (L - 1)
    k_strip, k_row = col >> ((2 * L).bit_length() - 1), col & (2 * L - 1)
    back = strips * (q_row - k_row) + (q_strip - k_strip)

    def band(offset):
        rel = back + strips * offset
        return jnp.where((rel >= 0) & (rel <= WINDOW_STEPS), 0.0, MASKED).astype(F32)

    bias_ref[0] = band(0)
    bias_ref[1] = band(L)
    lane = lax.broadcasted_iota(jnp.int32, (nq, LANES), 1)
    lo_half = lane < HEAD_DIM
    n_heads = d_attn // HEAD_DIM

    def block(it, carry):
        g, j_local = it // n_blocks, it % n_blocks
        j = first_block + j_local
        q_rows = pl.ds(pl.multiple_of(j_local * L, L), L)
        k_rows = pl.ds(pl.multiple_of(jnp.maximum(j - 1, 0) * L, L), 2 * L)
        bias = bias_ref[jnp.minimum(j, 1)]
        stats = jnp.zeros((nq, LANES), F32)
        for hp in range(d_attn // LANES):
            cols = slice(hp * LANES, (hp + 1) * LANES)
            qp = q_ref[:, g, q_rows, cols].reshape(nq, LANES).astype(BF16)
            kk = k_ref[:, g, k_rows, cols].reshape(nk, LANES).astype(BF16)
            vv = v_ref[:, g, k_rows, cols].reshape(nk, LANES).astype(BF16)
            outs, head_stats = [], []
            for sub in range(2):
                keep = lo_half if sub == 0 else jnp.logical_not(lo_half)
                qm = jnp.where(keep, qp, jnp.zeros_like(qp))
                s = lax.dot_general(qm, kk, (((1,), (1,)), ((), ())), preferred_element_type=F32) + bias
                m = jnp.max(s, axis=-1, keepdims=True)
                p = jnp.exp2(s - m)
                l = jnp.sum(p, axis=-1, keepdims=True)
                outs.append(jnp.dot(p.astype(BF16), vv, preferred_element_type=F32))
                head_stats.append(jnp.where(lane < n_heads, m, l))
            o_pair = jnp.where(lo_half, outs[0], outs[1]).astype(o_ref.dtype)
            o_ref[:, g, q_rows, cols] = o_pair.reshape(strips, L, LANES)
            pair_stats = jnp.where((lane & 1) == 0, head_stats[0], head_stats[1])
            stats = jnp.where(((lane & (n_heads - 1)) >> 1) == hp, pair_stats, stats)
        stat_ref[:, g, q_rows, :] = stats.reshape(strips, L, LANES)
        return carry

    lax.fori_loop(0, classes * n_blocks, block, 0, unroll=ATTN_UNROLL_QUERIES // nq)


def _attn_branch(q, k, v, dil):
    B, R, SI, C = q.shape
    strips = R // dil
    row_tile = 8 * 4 // q.dtype.itemsize
    nq = max(WINDOW_STEPS, strips * row_tile)
    L = nq // strips
    classes = max(1, min(dil, ATTN_STEP_ROWS // (strips * SI)))
    chunk = min(SI, max(L, ATTN_STEP_ROWS // (strips * classes)))
    assert R % dil == 0 and SI % chunk == 0 and chunk % L == 0 and SI >= 2 * L and dil % classes == 0
    n_heads = C // HEAD_DIM
    assert C % LANES == 0 and 2 * n_heads <= LANES and n_heads & (n_heads - 1) == 0
    assert nq & (nq - 1) == 0 and L & (L - 1) == 0
    view = lambda t: t.reshape(B, strips, dil, SI, t.shape[-1])
    q_spec = lambda c: pl.BlockSpec((None, strips, classes, chunk, c), lambda b, g, s: (b, 0, g, s, 0))
    kv_spec = pl.BlockSpec((None, strips, classes, SI, C), lambda b, g, s: (b, 0, g, 0, 0),
                           pipeline_mode=pl.Buffered(1 if SI > chunk else 2))
    o, stats = pl.pallas_call(
        functools.partial(_attn_kernel, nq=nq, strips=strips, d_attn=C),
        scratch_shapes=[pltpu.VMEM((2, nq, 2 * nq), F32)],
        grid=(B, dil // classes, SI // chunk),
        in_specs=[q_spec(C), kv_spec, kv_spec],
        out_specs=[q_spec(C), q_spec(LANES)],
        out_shape=[jax.ShapeDtypeStruct((B, strips, dil, SI, C), q.dtype),
                   jax.ShapeDtypeStruct((B, strips, dil, SI, LANES), F32)],
        compiler_params=_params("parallel", "parallel", "arbitrary"),
        name=f"attn_d{dil}",
    )(view(q), view(k), view(v))
    return o.reshape(B, R, SI, C), stats.reshape(B, R, SI, LANES)


def _route(h, wr, earlier, n_experts):
    h_hi = h.astype(BF16)
    h_lo = (h - h_hi.astype(F32)).astype(BF16)
    both = jnp.dot(h_hi, wr, preferred_element_type=F32)
    logits = both[:, :LANES] + both[:, LANES:] + jnp.dot(h_lo, wr[:, :LANES], preferred_element_type=F32)
    rows = h.shape[0]
    scores = logits.T[:n_experts]
    expert = lax.broadcasted_iota(jnp.int32, scores.shape, 0)
    picks = []
    for _ in range(TOP_K):
        m = jnp.max(scores, axis=0, keepdims=True)
        idx = jnp.min(jnp.where(scores == m, expert, n_experts), axis=0, keepdims=True)
        picks.append((m, idx))
        scores = jnp.where(expert == idx, -jnp.inf, scores)
    (m1, i1), (m2, i2) = picks
    e2 = jnp.exp(m2 - m1)
    g1 = 1.0 / (1.0 + e2)
    g2 = e2 / (1.0 + e2)
    picked = jnp.where(jnp.logical_or(expert == i1, expert == i2), 1.0, 0.0)
    rank = jnp.dot(picked.astype(BF16), earlier, preferred_element_type=F32)
    first_is_hi = i1 > i2
    e_hi, e_lo = jnp.maximum(i1, i2), jnp.minimum(i1, i2)
    rank_of = lambda e: jnp.sum(jnp.where(expert == e, rank, 0.0), axis=0, keepdims=True).astype(jnp.int32)
    plan = jnp.concatenate([e_hi, rank_of(e_hi), e_lo, rank_of(e_lo), jnp.zeros((4, rows), jnp.int32)], axis=0)
    gate_rows = jnp.concatenate([jnp.where(first_is_hi, g1, g2), jnp.where(first_is_hi, g2, g1),
                                 jnp.zeros((LANES - 2, rows), F32)], axis=0)
    return gate_rows.T, plan, jnp.sum(picked, axis=1, keepdims=True)


def _mix_out_kernel(*refs, d_pool, natural_x, n_experts):
    if n_experts:
        (x_ref, ya_ref, o1_ref, o2_ref, o3_ref, l1_ref, l2_ref, l3_ref, gain_ref, wo_ref, gffn_ref, exp_ref,
         wr_ref, earlier_ref, x1_ref, h2_ref, gates_ref, plan_ref, counts_ref) = refs
    else:
        (x_ref, ya_ref, o1_ref, o2_ref, o3_ref, l1_ref, l2_ref, l3_ref, gain_ref, wo_ref, gffn_ref, exp_ref,
         x1_ref, h2_ref) = refs
    n_heads = o1_ref.shape[-1] // HEAD_DIM
    stats = [_token_rows(l) for l in (l1_ref, l2_ref, l3_ref)]
    top = jnp.maximum(jnp.maximum(stats[0], stats[1]), stats[2])
    es = [jnp.exp2(t - top) for t in stats]
    sums = [pltpu.roll(t, LANES - n_heads, axis=1) for t in stats]
    den = es[0] * sums[0] + es[1] * sums[1] + es[2] * sums[2]
    head_lane = lax.broadcasted_iota(jnp.int32, den.shape, 1) < n_heads
    expand = exp_ref[...]

    def per_lane(w):
        hi = w.astype(BF16)
        lo = (w - hi.astype(F32)).astype(BF16)
        return jnp.dot(jnp.concatenate([hi, lo], axis=1), expand, preferred_element_type=F32)

    o = 0.0
    for e, o_ref in zip(es, (o1_ref, o2_ref, o3_ref)):
        o = o + per_lane(jnp.where(head_lane, e / den, 0.0)) * _token_rows(o_ref).astype(F32)
    yb = _rms(o, gain_ref[...]).astype(BF16)
    y = (jnp.dot(_token_rows(ya_ref), wo_ref[:d_pool, :], preferred_element_type=F32)
         + jnp.dot(yb, wo_ref[d_pool:, :], preferred_element_type=F32))
    x1 = _token_rows(x_ref, natural_x) + y
    x1_ref[...] = x1.reshape(x1_ref.shape)
    h2 = _rms(x1, gffn_ref[...])
    if n_experts:
        gate_tile, plan, counts = _route(h2, wr_ref[...], earlier_ref[...], n_experts)
        gates_ref[...] = gate_tile.reshape(gates_ref.shape)
        plan_ref[...] = plan
        counts_ref[...] = jnp.broadcast_to(counts, counts_ref.shape)
        h2_ref[...] = _pack_halves(h2).reshape(h2_ref.shape)
    else:
        h2_ref[...] = h2.astype(h2_ref.dtype).reshape(h2_ref.shape)


def _mix_out(x, ya, outs, lses, attn_gain, w_out, g_ffn, w_router, *, natural_x, ti=32):
    B, R, SI, d_pool = ya.shape
    D = x.shape[-1]
    d_attn = outs[0].shape[-1]
    ti = min(ti, SI)
    head_of_lane = jnp.arange(d_attn) // HEAD_DIM
    expand = (jnp.arange(2 * LANES)[:, None] % LANES == head_of_lane[None, :]).astype(BF16)
    seq = lambda c: pl.BlockSpec((None, R, ti, c), lambda b, s: (b, 0, s, 0))
    x_spec = pl.BlockSpec((None, ti, R, D), lambda b, s: (b, s, 0, 0)) if natural_x else seq(D)
    full = lambda shape: pl.BlockSpec(shape, lambda b, s: (0,) * len(shape))
    tokens = lambda c, dtype: jax.ShapeDtypeStruct((B, R, SI, c), dtype)
    in_specs = ([x_spec, seq(d_pool)] + [seq(d_attn)] * 3 + [seq(LANES)] * 3
                + [full((1, d_attn)), full(w_out.shape), full((1, D)), full(expand.shape)])
    args = [x, ya, *outs, *lses, attn_gain.reshape(1, d_attn), w_out, g_ffn.reshape(1, D), expand]
    if w_router is None:
        n_experts = 0
        out_specs, out_shape = [seq(D), seq(D)], [tokens(D, F32), tokens(D, BF16)]
    else:
        n_experts = w_router.shape[-1]
        assert 2 * n_experts <= LANES and TOP_K == 2
        wr = jnp.zeros((D, LANES), F32).at[:, :n_experts].set(w_router)
        wr_hi = wr.astype(BF16)
        rows = R * ti
        earlier = (jnp.arange(rows)[:, None] < jnp.arange(rows)[None, :]).astype(BF16)
        in_specs += [full((D, 2 * LANES)), full((rows, rows))]
        args += [jnp.concatenate([wr_hi, (wr - wr_hi.astype(F32)).astype(BF16)], axis=1), earlier]
        per_block = lambda r, c: pl.BlockSpec((None, None, r, c), lambda b, s: (b, s, 0, 0))
        out_specs = [seq(D), seq(D // 2), seq(LANES), per_block(8, rows), per_block(n_experts, LANES)]
        out_shape = [tokens(D, F32), tokens(D // 2, U32), tokens(LANES, F32),
                     jax.ShapeDtypeStruct((B, SI // ti, 8, rows), jnp.int32),
                     jax.ShapeDtypeStruct((B, SI // ti, n_experts, LANES), F32)]
    return pl.pallas_call(
        functools.partial(_mix_out_kernel, d_pool=d_pool, natural_x=natural_x, n_experts=n_experts),
        grid=(B, SI // ti),
        in_specs=in_specs, out_specs=out_specs, out_shape=out_shape,
        compiler_params=_params("parallel", "parallel"),
        name="mix_out",
    )(*args)


def _swiglu_hidden(h, wg, wu):
    a = jnp.dot(h, wg, preferred_element_type=F32)
    b = jnp.dot(h, wu, preferred_element_type=F32)
    return (a * jax.nn.sigmoid(a) * b).astype(BF16)


def _finish(x, y, gfin_ref):
    out = x + y
    return out if gfin_ref is None else _rms(out, gfin_ref[...])


def _ffn_kernel(*refs, final, tf):
    if final:
        h_ref, x_ref, wg_ref, wu_ref, wd_ref, gfin_ref, o_ref, hid = refs
    else:
        (h_ref, x_ref, wg_ref, wu_ref, wd_ref, o_ref, hid), gfin_ref = refs, None
    h = h_ref[...]
    for c in range(wg_ref.shape[-1] // tf):
        cols = slice(c * tf, (c + 1) * tf)
        hid[:, cols] = _swiglu_hidden(h, wg_ref[:, cols], wu_ref[:, cols])
    y = jnp.dot(hid[...], wd_ref[...], preferred_element_type=F32)
    o_ref[...] = _finish(x_ref[...], y, gfin_ref)


def _pick_chunk(n, target):
    best = None
    for c in range(LANES, min(n, target) + 1, LANES):
        if n % c == 0:
            best = c
    assert best is not None
    return best


def _ffn(h, x, wg, wu, wd, g_final, *, tm=512, tf_target=256):
    N, D = x.shape
    F = wg.shape[-1]
    tm = min(tm, N)
    tf = _pick_chunk(F, tf_target)
    final = g_final is not None
    row = pl.BlockSpec((tm, D), lambda i: (i, 0))
    resident = lambda shape: pl.BlockSpec(shape, lambda i: (0, 0), pipeline_mode=pl.Buffered(1))
    in_specs = [row, row, resident((D, F)), resident((D, F)), resident((F, D))]
    args = [h, x, wg, wu, wd]
    if final:
        in_specs.append(pl.BlockSpec((1, D), lambda i: (0, 0)))
        args.append(g_final.reshape(1, D))
    return pl.pallas_call(
        functools.partial(_ffn_kernel, final=final, tf=tf),
        grid=(N // tm,),
        in_specs=in_specs,
        out_specs=row,
        out_shape=jax.ShapeDtypeStruct((N, D), F32),
        scratch_shapes=[pltpu.VMEM((tm, F), BF16)],
        compiler_params=_params("parallel"),
        name="ffn_dense",
    )(*args)


def _plan_rows(block_plan, block_counts, token_shape, *, tm):
    B, R, SI = token_shape
    i32 = jnp.int32
    _, per_row, _, rows = block_plan.shape
    E = block_counts.shape[2]
    p_max = TOP_K * B * R * SI + E * tm
    counts = block_counts[..., 0].astype(i32).reshape(-1, E)
    block_start = jnp.cumsum(counts, axis=0) - counts
    seg_len = (jnp.sum(counts, axis=0) + tm - 1) // tm * tm
    seg_end = jnp.cumsum(seg_len)
    base = (seg_end - seg_len)[None, :] + block_start
    plan = block_plan.reshape(-1, 8, rows)
    experts = jnp.arange(E, dtype=i32)[None, None, :]

    def sorted_rows(expert, rank):
        row = jnp.sum(jnp.where(expert[:, :, None] == experts, base[:, None, :], 0), axis=-1) + rank
        return row.reshape(B, per_row, R, rows // R).transpose(0, 2, 1, 3).reshape(-1)

    tile_start = jnp.arange(p_max // tm, dtype=i32) * tm
    tile_expert = jnp.minimum(jnp.sum(seg_end[None, :] <= tile_start[:, None], axis=1), E - 1).astype(i32)
    return dict(rows=(sorted_rows(plan[:, 0], plan[:, 1]), sorted_rows(plan[:, 2], plan[:, 3])),
                tile_expert=tile_expert, n_active_tiles=(seg_end[-1] // tm).reshape(1).astype(i32), p_max=p_max)


def _sc_rows_kernel(n_rows, n_out, d, scatter):
    window_rows = SC_SCATTER_WINDOW if scatter else SC_GATHER_WINDOW
    per_worker = n_rows // SC_WORKERS
    assert n_rows % (SC_WORKERS * window_rows) == 0
    mesh = plsc.VectorSubcoreMesh(core_axis_name="c", subcore_axis_name="s")
    out_type = (jax.ShapeDtypeStruct((n_out, d), U32) if scatter
                else [jax.ShapeDtypeStruct((n_rows, d), U32)] * 2)

    def body(*refs):
        if scatter:
            src_hbm, hi_hbm, lo_hbm, out_hbm, idx_hi, idx_lo, rows, sem_hi, sem_lo = refs
        else:
            (src_hbm, hi_hbm, lo_hbm, out_hi_hbm, out_lo_hbm, idx_hi, idx_lo, rows_hi, rows_lo,
             sem_hi, sem_lo, sem_out_hi, sem_out_lo) = refs
        worker = lax.axis_index("s") * SC_CORES + lax.axis_index("c")

        @pl.loop(0, per_worker // window_rows)
        def _(i):
            window = pl.ds(worker * per_worker + i * window_rows, window_rows)
            pltpu.sync_copy(hi_hbm.at[window], idx_hi)
            pltpu.sync_copy(lo_hbm.at[window], idx_lo)
            if scatter:
                pltpu.sync_copy(src_hbm.at[window], rows)
                to_hi = pltpu.async_copy(rows, out_hbm.at[idx_hi], sem_hi)
                to_lo = pltpu.async_copy(rows, out_hbm.at[idx_lo], sem_lo)
                to_hi.wait()
                to_lo.wait()
            else:
                from_hi = pltpu.async_copy(src_hbm.at[idx_hi], rows_hi, sem_hi)
                from_lo = pltpu.async_copy(src_hbm.at[idx_lo], rows_lo, sem_lo)
                from_hi.wait()
                out_hi = pltpu.async_copy(rows_hi, out_hi_hbm.at[window], sem_out_hi)
                from_lo.wait()
                out_lo = pltpu.async_copy(rows_lo, out_lo_hbm.at[window], sem_out_lo)
                out_hi.wait()
                out_lo.wait()

    index_vec, row_buf = pltpu.VMEM((window_rows,), jnp.int32), pltpu.VMEM((window_rows, d), U32)
    dma_sem = pltpu.SemaphoreType.DMA
    scratch = ([index_vec, index_vec, row_buf, dma_sem, dma_sem] if scatter
               else [index_vec, index_vec, row_buf, row_buf, dma_sem, dma_sem, dma_sem, dma_sem])
    return pl.kernel(body, mesh=mesh, out_type=out_type, scratch_types=scratch)


def _moe_experts_kernel(te_ref, na_ref, xs_ref, wg_hbm, wu_hbm, wd_hbm, o_ref,
                        cache_g, cache_u, cache_d, stage_g, stage_u, stage_d, sems, acc, *, tf):
    i = pl.program_id(0)
    e = te_ref[i]
    n_chunks = cache_g.shape[-1] // tf
    active = i < na_ref[0]
    new_expert = jnp.logical_or(i == 0, e != te_ref[jnp.maximum(i - 1, 0)])

    def chunk_copies(c, slot):
        cols = pl.ds(c * tf, tf)
        return (pltpu.make_async_copy(wg_hbm.at[e, :, cols], stage_g.at[slot], sems.at[0, slot]),
                pltpu.make_async_copy(wu_hbm.at[e, :, cols], stage_u.at[slot], sems.at[1, slot]),
                pltpu.make_async_copy(wd_hbm.at[e, cols, :], stage_d.at[slot], sems.at[2, slot]))

    def tile_ffn(load_weights):
        x = _unpack_halves(xs_ref[...]).astype(BF16)
        if load_weights:
            for cp in chunk_copies(0, 0):
                cp.start()
        for c in range(n_chunks):
            cols = slice(c * tf, (c + 1) * tf)
            if load_weights:
                slot = c % 2
                if c + 1 < n_chunks:
                    for cp in chunk_copies(c + 1, 1 - slot):
                        cp.start()
                for cp in chunk_copies(c, slot):
                    cp.wait()
                cache_g[:, cols] = stage_g[slot].astype(BF16)
                cache_u[:, cols] = stage_u[slot].astype(BF16)
                cache_d[cols, :] = stage_d[slot].astype(BF16)
            hid = _swiglu_hidden(x, cache_g[:, cols], cache_u[:, cols])
            part = jnp.dot(hid, cache_d[cols, :], preferred_element_type=F32)
            if c == 0:
                acc[...] = part
            else:
                acc[...] += part
        o_ref[...] = _pack_halves(acc[...])

    pl.when(jnp.logical_and(active, new_expert))(lambda: tile_ffn(True))
    pl.when(jnp.logical_and(active, jnp.logical_not(new_expert)))(lambda: tile_ffn(False))


def _moe_experts(xs, plan, wg, wu, wd, *, tm, tf_target=512):
    P = xs.shape[0]
    E, D, F = wg.shape
    tf = _pick_chunk(F, tf_target)
    tile = lambda i, te, na: (jnp.minimum(i, na[0] - 1), 0)
    in_hbm = pl.BlockSpec(memory_space=pl.ANY)
    grid_spec = pltpu.PrefetchScalarGridSpec(
        num_scalar_prefetch=2,
        grid=(P // tm,),
        in_specs=[pl.BlockSpec((tm, D // 2), tile), in_hbm, in_hbm, in_hbm],
        out_specs=pl.BlockSpec((tm, D // 2), tile),
        scratch_shapes=[pltpu.VMEM((D, F), BF16), pltpu.VMEM((D, F), BF16), pltpu.VMEM((F, D), BF16),
                        pltpu.VMEM((2, D, tf), F32), pltpu.VMEM((2, D, tf), F32), pltpu.VMEM((2, tf, D), F32),
                        pltpu.SemaphoreType.DMA((3, 2)), pltpu.VMEM((tm, D), F32)],
    )
    return pl.pallas_call(
        functools.partial(_moe_experts_kernel, tf=tf),
        grid_spec=grid_spec,
        out_shape=jax.ShapeDtypeStruct((P, D // 2), U32),
        compiler_params=_params("arbitrary", vmem_limit=MOE_VMEM_LIMIT),
        name="moe_experts",
    )(plan["tile_expert"], plan["n_active_tiles"], xs, wg, wu, wd)


def _moe_mix_kernel(*refs, final, natural_out, carried):
    refs = list(refs)
    o_ref = refs.pop()
    if carried:
        refs.pop()
    x_ref, yh_ref, yl_ref, g_ref = refs[:4]
    gfin_ref = refs[4] if final else None
    gates = _token_rows(g_ref)
    lane = lax.broadcasted_iota(jnp.int32, gates.shape, 1)
    g_hi = jnp.sum(jnp.where(lane == 0, gates, 0.0), axis=-1, keepdims=True)
    g_lo = jnp.sum(jnp.where(lane == 1, gates, 0.0), axis=-1, keepdims=True)
    y = g_hi * _unpack_halves(_token_rows(yh_ref)) + g_lo * _unpack_halves(_token_rows(yl_ref))
    out = _finish(_token_rows(x_ref), y, gfin_ref).reshape(x_ref.shape)
    o_ref[...] = pltpu.einshape("rid->ird", out) if natural_out else out


def _moe_mix(x, y_hi, y_lo, pair_gates, g_final, *, natural_out, first_row, partial_out, ti=32):
    B, R, SI, D = x.shape
    n_rows = y_hi.shape[0] // (R * SI)
    ti = min(ti, SI)
    final = g_final is not None
    seq = lambda c: pl.BlockSpec((None, R, ti, c), lambda b, s: (first_row + b, 0, s, 0))
    local = lambda c: pl.BlockSpec((None, R, ti, c), lambda b, s: (b, 0, s, 0))
    as_tokens = lambda t: t.reshape(n_rows, R, SI, t.shape[-1])
    in_specs = [seq(D), local(D // 2), local(D // 2), seq(LANES)]
    args = [x, as_tokens(y_hi), as_tokens(y_lo), pair_gates]
    if final:
        in_specs.append(pl.BlockSpec((1, D), lambda b, s: (0, 0)))
        args.append(g_final.reshape(1, D))
    aliases = {}
    if partial_out is not None:
        aliases = {len(args): 0}
        in_specs.append(pl.BlockSpec(memory_space=pl.ANY))
        args.append(partial_out)
    out_spec = pl.BlockSpec((None, ti, R, D), lambda b, s: (first_row + b, s, 0, 0)) if natural_out else seq(D)
    return pl.pallas_call(
        functools.partial(_moe_mix_kernel, final=final, natural_out=natural_out, carried=partial_out is not None),
        grid=(n_rows, SI // ti), in_specs=in_specs, out_specs=out_spec,
        out_shape=jax.ShapeDtypeStruct((B, SI, R, D) if natural_out else (B, R, SI, D), F32),
        input_output_aliases=aliases,
        compiler_params=_params("parallel", "parallel"),
        name="moe_mix",
    )(*args)


def _moe(h, x, routing, wg, wu, wd, g_final, *, natural_out, tm=512):
    pair_gates, block_plan, block_counts = routing
    B, R, SI, half = h.shape
    N = B * R * SI
    plan = _plan_rows(block_plan, block_counts, (B, R, SI), tm=tm)
    rows_hi, rows_lo = plan["rows"]
    xs = _sc_rows_kernel(N, plan["p_max"], half, scatter=True)(h.reshape(N, half), rows_hi, rows_lo)
    ys = _moe_experts(xs, plan, wg, wu, wd, tm=tm)
    groups = MOE_RETURN_GROUPS if B % MOE_RETURN_GROUPS == 0 else 1
    n = N // groups
    gather = _sc_rows_kernel(n, plan["p_max"], half, scatter=False)
    fetched = [gather(ys, rows_hi[g * n:(g + 1) * n], rows_lo[g * n:(g + 1) * n]) for g in range(groups)]
    out = None
    for g, (y_hi, y_lo) in enumerate(fetched):
        out = _moe_mix(x, y_hi, y_lo, pair_gates, g_final, natural_out=natural_out,
                       first_row=g * (B // groups), partial_out=out)
    return out


def kernel(x, norm_mix, w_in, w_pool, pool_scale, attn_gain, w_out, norm_ffn, ffn_wg, ffn_wu, ffn_wd,
           w_router, moe_wg, moe_wu, moe_wd, final_norm):
    B, S, D = x.shape
    depth = norm_mix.shape[0]
    bf = lambda t: t.astype(BF16)
    w_in, w_pool, w_out = bf(w_in), bf(w_pool), bf(w_out)
    ffn_wg, ffn_wu, ffn_wd = bf(ffn_wg), bf(ffn_wu), bf(ffn_wd)
    R = RESIDUES
    assert S % R == 0
    N, SI = B * S, S // R
    x = x.reshape(B, SI, R, D)
    for l in range(depth):
        first, last = l == 0, l == depth - 1
        ya, q, k, v, q32, k32, v32 = _mix_in(x, norm_mix[l], w_in[l], w_pool[l], pool_scale[l], natural_x=first)
        narrow = lambda dil: (R // dil) * 16 > WINDOW_STEPS
        branches = [_attn_branch(q32, k32, v32, dil) if narrow(dil) else _attn_branch(q, k, v, dil)
                    for _, dil in DILATED_PATTERNS]
        i, routed = l // 2, l % 2 == 1
        x1, h2, *routing = _mix_out(x, ya, [o for o, _ in branches], [lse for _, lse in branches],
                                    attn_gain[l], w_out[l], norm_ffn[l], w_router[i] if routed else None,
                                    natural_x=first)
        g_final = final_norm if last else None
        if routed:
            x = _moe(h2, x1, routing, moe_wg[i], moe_wu[i], moe_wd[i], g_final, natural_out=last)
        else:
            x = _ffn(h2.reshape(N, D), x1.reshape(N, D), ffn_wg[i], ffn_wu[i], ffn_wd[i], g_final)
            x = x.reshape(B, R, SI, D)
            if last:
                x = x.transpose(0, 2, 1, 3)
    return x.reshape(B, S, D)
```

```python
import functools

import jax
import jax.numpy as jnp
import numpy as np
from jax import lax
from jax.experimental import pallas as pl
from jax.experimental.pallas import tpu as pltpu
from jax.experimental.pallas import tpu_sc as plsc

F32 = jnp.float32
BF16 = jnp.bfloat16
U32 = jnp.uint32
HIGH_HALF = np.uint32(0xFFFF0000)

EPS = 1e-6
LANES = 128
HEAD_DIM = 64
POOL_WINDOWS = (2, 4, 8, 16)
POOL_HIST = 8
DILATED_PATTERNS = ((128, 1), (512, 4), (2048, 16))
WINDOW_STEPS = 128
RESIDUES = 16
ATTN_STEP_ROWS = 1024
ATTN_UNROLL_QUERIES = 1024
LOG2_E = 1.4426950408889634
SC_CORES, SC_SUBCORES = 2, 16
SC_WORKERS = SC_CORES * SC_SUBCORES
SC_SCATTER_WINDOW = 128
SC_GATHER_WINDOW = 64
MOE_RETURN_GROUPS = 2
assert all(w // d == WINDOW_STEPS and RESIDUES % d == 0 for w, d in DILATED_PATTERNS)
TOP_K = 2
MASKED = -1e30
VMEM_LIMIT = 48 * 1024 * 1024
MOE_VMEM_LIMIT = 56 * 1024 * 1024


def _rms(x, g):
    return x * lax.rsqrt(jnp.mean(x * x, axis=-1, keepdims=True) + EPS) * g


def _params(*sem, vmem_limit=VMEM_LIMIT):
    return pltpu.CompilerParams(dimension_semantics=sem, vmem_limit_bytes=vmem_limit)


def _token_rows(ref, natural=False):
    t = ref[...]
    if natural:
        t = pltpu.einshape("ird->rid", t)
    return t.reshape(-1, t.shape[-1])


def _pack_halves(t):
    bits = lax.bitcast_convert_type(t, U32)
    bits = (bits + np.uint32(0x7FFF) + ((bits >> 16) & np.uint32(1))) & HIGH_HALF
    half = t.shape[-1] // 2
    return (bits[:, :half] >> 16) | bits[:, half:]


def _unpack_halves(words):
    halves = [lax.bitcast_convert_type(w, F32) for w in (words << 16, words & HIGH_HALF)]
    return jnp.concatenate(halves, axis=-1)


def _mix_in_kernel(x_ref, g_ref, w_ref, wp_ref, ps_ref, ya_ref, q_ref, k_ref, v_ref, q32_ref, k32_ref, v32_ref,
                   ubuf, uprev, *, d_pool, d_attn, ti, natural_x):
    s = pl.program_id(1)
    R = RESIDUES
    blk = lambda t: t.reshape(R, ti, t.shape[-1])
    h = _rms(_token_rows(x_ref, natural_x), g_ref[...]).astype(BF16)
    u = jnp.dot(h, w_ref[:, :d_pool], preferred_element_type=F32)
    scale = HEAD_DIM ** -0.5 * LOG2_E
    for n, (ref, ref32) in enumerate(zip((q_ref, k_ref, v_ref), (q32_ref, k32_ref, v32_ref))):
        t = jnp.dot(h, w_ref[:, d_pool + n * d_attn:d_pool + (n + 1) * d_attn], preferred_element_type=F32)
        t = t * scale if n == 0 else t
        ref[...] = blk(t.astype(BF16))
        ref32[...] = blk(t)

    history = ubuf[:, POOL_HIST - 1:POOL_HIST, :]
    ubuf[:, POOL_HIST - 1:POOL_HIST, :] = jnp.where(s == 0, 0.0, history)
    ubuf[:, POOL_HIST:POOL_HIST + ti, :] = blk(u)
    uprev[...] = ubuf[:, POOL_HIST - 1:POOL_HIST - 1 + ti, :]
    at_start = (s * ti + lax.broadcasted_iota(jnp.int32, (ti, 1), 0)) == 0
    group = d_pool // len(POOL_WINDOWS)
    zs = []
    for gi, w in enumerate(POOL_WINDOWS):
        cols = slice(gi * group, (gi + 1) * group)
        ds = []
        for r in range(R):
            ug = ubuf[r, POOL_HIST:POOL_HIST + ti, cols]
            win = ug
            for back in range(1, w):
                rr = r - back
                win = win + (ubuf[rr, POOL_HIST:POOL_HIST + ti, cols] if rr >= 0 else uprev[rr + R, :, cols])
            cnt = jnp.where(at_start, float(min(r + 1, w)), float(w))
            ds.append(win / cnt - ug)
        d = jnp.concatenate(ds, axis=0).astype(BF16)
        zs.append(jnp.dot(d, wp_ref[gi], preferred_element_type=F32))
    z = jnp.concatenate(zs, axis=-1)
    ya_ref[...] = blk(_rms(z, ps_ref[...]).astype(BF16))
    ubuf[:, POOL_HIST - 1:POOL_HIST, :] = ubuf[:, POOL_HIST + ti - 1:POOL_HIST + ti, :]


def _mix_in(x, g, w_in, w_pool, pool_scale, *, natural_x, ti=64):
    B, R, SI, D = x.shape
    if natural_x:
        R, SI = SI, R
    d_pool = pool_scale.shape[-1]
    d_in = w_in.shape[-1]
    d_attn = (d_in - d_pool) // 3
    ti = min(ti, SI)
    assert R == RESIDUES >= max(POOL_WINDOWS) and SI % ti == 0 and ti % 16 == 0
    assert d_pool % (LANES * len(POOL_WINDOWS)) == 0
    seq_spec = lambda c: pl.BlockSpec((None, R, ti, c), lambda b, s: (b, 0, s, 0))
    full = lambda shape: pl.BlockSpec(shape, lambda b, s: (0,) * len(shape))
    out_sds = lambda c, dtype=BF16: jax.ShapeDtypeStruct((B, R, SI, c), dtype)
    return pl.pallas_call(
        functools.partial(_mix_in_kernel, d_pool=d_pool, d_attn=d_attn, ti=ti, natural_x=natural_x),
        grid=(B, SI // ti),
        in_specs=[pl.BlockSpec((None, ti, R, D), lambda b, s: (b, s, 0, 0)) if natural_x else seq_spec(D),
                  full((1, D)), full((D, d_in)), full(w_pool.shape), full((1, d_pool))],
        out_specs=[seq_spec(d_pool)] + [seq_spec(d_attn)] * 6,
        out_shape=[out_sds(d_pool)] + [out_sds(d_attn)] * 3 + [out_sds(d_attn, F32)] * 3,
        scratch_shapes=[pltpu.VMEM((R, POOL_HIST + ti, d_pool), F32), pltpu.VMEM((R, ti, d_pool), F32)],
        compiler_params=_params("parallel", "arbitrary"),
        name="mix_in",
    )(x, g.reshape(1, D), w_in, w_pool, pool_scale.reshape(1, d_pool))


def _attn_kernel(q_ref, k_ref, v_ref, o_ref, stat_ref, bias_ref, *, nq, strips, d_attn):
    _, classes, chunk, _ = q_ref.shape
    L = nq // strips
    nk, n_blocks = 2 * nq, chunk // L
    first_block = pl.program_id(2) * n_blocks
    row = lax.broadcasted_iota(jnp.int32, (nq, 1), 0)
    col = lax.broadcasted_iota(jnp.int32, (1, nk), 1)
    q_strip, q_row = row >> (L.bit_length() - 1), row & (L - 1)
    k_strip, k_row = col >> ((2 * L).bit_length() - 1), col & (2 * L - 1)
    back = strips * (q_row - k_row) + (q_strip - k_strip)

    def band(offset):
        rel = back + strips * offset
        return jnp.where((rel >= 0) & (rel <= WINDOW_STEPS), 0.0, MASKED).astype(F32)

    bias_ref[0] = band(0)
    bias_ref[1] = band(L)
    lane = lax.broadcasted_iota(jnp.int32, (nq, LANES), 1)
    lo_half = lane < HEAD_DIM
    n_heads = d_attn // HEAD_DIM

    def block(it, carry):
        g, j_local = it // n_blocks, it % n_blocks
        j = first_block + j_local
        q_rows = pl.ds(pl.multiple_of(j_local * L, L), L)
        k_rows = pl.ds(pl.multiple_of(jnp.maximum(j - 1, 0) * L, L), 2 * L)
        bias = bias_ref[jnp.minimum(j, 1)]
        stats = jnp.zeros((nq, LANES), F32)
        for hp in range(d_attn // LANES):
            cols = slice(hp * LANES, (hp + 1) * LANES)
            qp = q_ref[:, g, q_rows, cols].reshape(nq, LANES).astype(BF16)
            kk = k_ref[:, g, k_rows, cols].reshape(nk, LANES).astype(BF16)
            vv = v_ref[:, g, k_rows, cols].reshape(nk, LANES).astype(BF16)
            outs, head_stats = [], []
            for sub in range(2):
                keep = lo_half if sub == 0 else jnp.logical_not(lo_half)
                qm = jnp.where(keep, qp, jnp.zeros_like(qp))
                s = lax.dot_general(qm, kk, (((1,), (1,)), ((), ())), preferred_element_type=F32) + bias
                m = jnp.max(s, axis=-1, keepdims=True)
                p = jnp.exp2(s - m)
                l = jnp.sum(p, axis=-1, keepdims=True)
                outs.append(jnp.dot(p.astype(BF16), vv, preferred_element_type=F32))
                head_stats.append(jnp.where(lane < n_heads, m, l))
            o_pair = jnp.where(lo_half, outs[0], outs[1]).astype(o_ref.dtype)
            o_ref[:, g, q_rows, cols] = o_pair.reshape(strips, L, LANES)
            pair_stats = jnp.where((lane & 1) == 0, head_stats[0], head_stats[1])
            stats = jnp.where(((lane & (n_heads - 1)) >> 1) == hp, pair_stats, stats)
        stat_ref[:, g, q_rows, :] = stats.reshape(strips, L, LANES)
        return carry

    lax.fori_loop(0, classes * n_blocks, block, 0, unroll=ATTN_UNROLL_QUERIES // nq)


def _attn_branch(q, k, v, dil):
    B, R, SI, C = q.shape
    strips = R // dil
    row_tile = 8 * 4 // q.dtype.itemsize
    nq = max(WINDOW_STEPS, strips * row_tile)
    L = nq // strips
    classes = max(1, min(dil, ATTN_STEP_ROWS // (strips * SI)))
    chunk = min(SI, max(L, ATTN_STEP_ROWS // (strips * classes)))
    assert R % dil == 0 and SI % chunk == 0 and chunk % L == 0 and SI >= 2 * L and dil % classes == 0
    n_heads = C // HEAD_DIM
    assert C % LANES == 0 and 2 * n_heads <= LANES and n_heads & (n_heads - 1) == 0
    assert nq & (nq - 1) == 0 and L & (L - 1) == 0
    view = lambda t: t.reshape(B, strips, dil, SI, t.shape[-1])
    q_spec = lambda c: pl.BlockSpec((None, strips, classes, chunk, c), lambda b, g, s: (b, 0, g, s, 0))
    kv_spec = pl.BlockSpec((None, strips, classes, SI, C), lambda b, g, s: (b, 0, g, 0, 0),
                           pipeline_mode=pl.Buffered(1 if SI > chunk else 2))
    o, stats = pl.pallas_call(
        functools.partial(_attn_kernel, nq=nq, strips=strips, d_attn=C),
        scratch_shapes=[pltpu.VMEM((2, nq, 2 * nq), F32)],
        grid=(B, dil // classes, SI // chunk),
        in_specs=[q_spec(C), kv_spec, kv_spec],
        out_specs=[q_spec(C), q_spec(LANES)],
        out_shape=[jax.ShapeDtypeStruct((B, strips, dil, SI, C), q.dtype),
                   jax.ShapeDtypeStruct((B, strips, dil, SI, LANES), F32)],
        compiler_params=_params("parallel", "parallel", "arbitrary"),
        name=f"attn_d{dil}",
    )(view(q), view(k), view(v))
    return o.reshape(B, R, SI, C), stats.reshape(B, R, SI, LANES)


def _route(h, wr, earlier, n_experts):
    h_hi = h.astype(BF16)
    h_lo = (h - h_hi.astype(F32)).astype(BF16)
    both = jnp.dot(h_hi, wr, preferred_element_type=F32)
    logits = both[:, :LANES] + both[:, LANES:] + jnp.dot(h_lo, wr[:, :LANES], preferred_element_type=F32)
    rows = h.shape[0]
    scores = logits.T[:n_experts]
    expert = lax.broadcasted_iota(jnp.int32, scores.shape, 0)
    picks = []
    for _ in range(TOP_K):
        m = jnp.max(scores, axis=0, keepdims=True)
        idx = jnp.min(jnp.where(scores == m, expert, n_experts), axis=0, keepdims=True)
        picks.append((m, idx))
        scores = jnp.where(expert == idx, -jnp.inf, scores)
    (m1, i1), (m2, i2) = picks
    e2 = jnp.exp(m2 - m1)
    g1 = 1.0 / (1.0 + e2)
    g2 = e2 / (1.0 + e2)
    picked = jnp.where(jnp.logical_or(expert == i1, expert == i2), 1.0, 0.0)
    rank = jnp.dot(picked.astype(BF16), earlier, preferred_element_type=F32)
    first_is_hi = i1 > i2
    e_hi, e_lo = jnp.maximum(i1, i2), jnp.minimum(i1, i2)
    rank_of = lambda e: jnp.sum(jnp.where(expert == e, rank, 0.0), axis=0, keepdims=True).astype(jnp.int32)
    plan = jnp.concatenate([e_hi, rank_of(e_hi), e_lo, rank_of(e_lo), jnp.zeros((4, rows), jnp.int32)], axis=0)
    gate_rows = jnp.concatenate([jnp.where(first_is_hi, g1, g2), jnp.where(first_is_hi, g2, g1),
                                 jnp.zeros((LANES - 2, rows), F32)], axis=0)
    return gate_rows.T, plan, jnp.sum(picked, axis=1, keepdims=True)


def _mix_out_kernel(*refs, d_pool, natural_x, n_experts):
    if n_experts:
        (x_ref, ya_ref, o1_ref, o2_ref, o3_ref, l1_ref, l2_ref, l3_ref, gain_ref, wo_ref, gffn_ref, exp_ref,
         wr_ref, earlier_ref, x1_ref, h2_ref, gates_ref, plan_ref, counts_ref) = refs
    else:
        (x_ref, ya_ref, o1_ref, o2_ref, o3_ref, l1_ref, l2_ref, l3_ref, gain_ref, wo_ref, gffn_ref, exp_ref,
         x1_ref, h2_ref) = refs
    n_heads = o1_ref.shape[-1] // HEAD_DIM
    stats = [_token_rows(l) for l in (l1_ref, l2_ref, l3_ref)]
    top = jnp.maximum(jnp.maximum(stats[0], stats[1]), stats[2])
    es = [jnp.exp2(t - top) for t in stats]
    sums = [pltpu.roll(t, LANES - n_heads, axis=1) for t in stats]
    den = es[0] * sums[0] + es[1] * sums[1] + es[2] * sums[2]
    head_lane = lax.broadcasted_iota(jnp.int32, den.shape, 1) < n_heads
    expand = exp_ref[...]

    def per_lane(w):
        hi = w.astype(BF16)
        lo = (w - hi.astype(F32)).astype(BF16)
        return jnp.dot(jnp.concatenate([hi, lo], axis=1), expand, preferred_element_type=F32)

    o = 0.0
    for e, o_ref in zip(es, (o1_ref, o2_ref, o3_ref)):
        o = o + per_lane(jnp.where(head_lane, e / den, 0.0)) * _token_rows(o_ref).astype(F32)
    yb = _rms(o, gain_ref[...]).astype(BF16)
    y = jnp.dot(jnp.concatenate([_token_rows(ya_ref), yb], axis=1), wo_ref[...], preferred_element_type=F32)
    x1 = _token_rows(x_ref, natural_x) + y
    x1_ref[...] = x1.reshape(x1_ref.shape)
    h2 = _rms(x1, gffn_ref[...])
    if n_experts:
        gate_tile, plan, counts = _route(h2, wr_ref[...], earlier_ref[...], n_experts)
        gates_ref[...] = gate_tile.reshape(gates_ref.shape)
        plan_ref[...] = plan
        counts_ref[...] = jnp.broadcast_to(counts, counts_ref.shape)
        h2_ref[...] = _pack_halves(h2).reshape(h2_ref.shape)
    else:
        h2_ref[...] = h2.astype(h2_ref.dtype).reshape(h2_ref.shape)


def _mix_out(x, ya, outs, lses, attn_gain, w_out, g_ffn, w_router, *, natural_x, ti=64):
    B, R, SI, d_pool = ya.shape
    D = x.shape[-1]
    d_attn = outs[0].shape[-1]
    ti = min(ti, SI)
    head_of_lane = jnp.arange(d_attn) // HEAD_DIM
    expand = (jnp.arange(2 * LANES)[:, None] % LANES == head_of_lane[None, :]).astype(BF16)
    seq = lambda c: pl.BlockSpec((None, R, ti, c), lambda b, s: (b, 0, s, 0))
    x_spec = pl.BlockSpec((None, ti, R, D), lambda b, s: (b, s, 0, 0)) if natural_x else seq(D)
    full = lambda shape: pl.BlockSpec(shape, lambda b, s: (0,) * len(shape))
    tokens = lambda c, dtype: jax.ShapeDtypeStruct((B, R, SI, c), dtype)
    in_specs = ([x_spec, seq(d_pool)] + [seq(d_attn)] * 3 + [seq(LANES)] * 3
                + [full((1, d_attn)), full(w_out.shape), full((1, D)), full(expand.shape)])
    args = [x, ya, *outs, *lses, attn_gain.reshape(1, d_attn), w_out, g_ffn.reshape(1, D), expand]
    if w_router is None:
        n_experts = 0
        out_specs, out_shape = [seq(D), seq(D)], [tokens(D, F32), tokens(D, BF16)]
    else:
        n_experts = w_router.shape[-1]
        assert 2 * n_experts <= LANES and TOP_K == 2
        wr = jnp.zeros((D, LANES), F32).at[:, :n_experts].set(w_router)
        wr_hi = wr.astype(BF16)
        rows = R * ti
        earlier = (jnp.arange(rows)[:, None] < jnp.arange(rows)[None, :]).astype(BF16)
        in_specs += [full((D, 2 * LANES)), full((rows, rows))]
        args += [jnp.concatenate([wr_hi, (wr - wr_hi.astype(F32)).astype(BF16)], axis=1), earlier]
        per_block = lambda r, c: pl.BlockSpec((None, None, r, c), lambda b, s: (b, s, 0, 0))
        out_specs = [seq(D), seq(D // 2), seq(LANES), per_block(8, rows), per_block(n_experts, LANES)]
        out_shape = [tokens(D, F32), tokens(D // 2, U32), tokens(LANES, F32),
                     jax.ShapeDtypeStruct((B, SI // ti, 8, rows), jnp.int32),
                     jax.ShapeDtypeStruct((B, SI // ti, n_experts, LANES), F32)]
    return pl.pallas_call(
        functools.partial(_mix_out_kernel, d_pool=d_pool, natural_x=natural_x, n_experts=n_experts),
        grid=(B, SI // ti),
        in_specs=in_specs, out_specs=out_specs, out_shape=out_shape,
        compiler_params=_params("parallel", "parallel"),
        name="mix_out",
    )(*args)


def _swiglu_hidden(h, wg, wu):
    a = jnp.dot(h, wg, preferred_element_type=F32)
    b = jnp.dot(h, wu, preferred_element_type=F32)
    return (a * jax.nn.sigmoid(a) * b).astype(BF16)


def _finish(x, y, gfin_ref):
    out = x + y
    return out if gfin_ref is None else _rms(out, gfin_ref[...])


def _ffn_kernel(*refs, final, tf):
    if final:
        h_ref, x_ref, wg_ref, wu_ref, wd_ref, gfin_ref, o_ref, hid = refs
    else:
        (h_ref, x_ref, wg_ref, wu_ref, wd_ref, o_ref, hid), gfin_ref = refs, None
    h = h_ref[...]
    for c in range(wg_ref.shape[-1] // tf):
        cols = slice(c * tf, (c + 1) * tf)
        hid[:, cols] = _swiglu_hidden(h, wg_ref[:, cols], wu_ref[:, cols])
    y = jnp.dot(hid[...], wd_ref[...], preferred_element_type=F32)
    o_ref[...] = _finish(x_ref[...], y, gfin_ref)


def _pick_chunk(n, target):
    best = None
    for c in range(LANES, min(n, target) + 1, LANES):
        if n % c == 0:
            best = c
    assert best is not None
    return best


def _ffn(h, x, wg, wu, wd, g_final, *, tm=512, tf_target=256):
    N, D = x.shape
    F = wg.shape[-1]
    tm = min(tm, N)
    tf = _pick_chunk(F, tf_target)
    final = g_final is not None
    row = pl.BlockSpec((tm, D), lambda i: (i, 0))
    resident = lambda shape: pl.BlockSpec(shape, lambda i: (0, 0), pipeline_mode=pl.Buffered(1))
    in_specs = [row, row, resident((D, F)), resident((D, F)), resident((F, D))]
    args = [h, x, wg, wu, wd]
    if final:
        in_specs.append(pl.BlockSpec((1, D), lambda i: (0, 0)))
        args.append(g_final.reshape(1, D))
    return pl.pallas_call(
        functools.partial(_ffn_kernel, final=final, tf=tf),
        grid=(N // tm,),
        in_specs=in_specs,
        out_specs=row,
        out_shape=jax.ShapeDtypeStruct((N, D), F32),
        scratch_shapes=[pltpu.VMEM((tm, F), BF16)],
        compiler_params=_params("parallel"),
        name="ffn_dense",
    )(*args)


def _plan_rows(block_plan, block_counts, token_shape, *, tm):
    B, R, SI = token_shape
    i32 = jnp.int32
    _, per_row, _, rows = block_plan.shape
    E = block_counts.shape[2]
    p_max = TOP_K * B * R * SI + E * tm
    counts = block_counts[..., 0].astype(i32).reshape(-1, E)
    block_start = jnp.cumsum(counts, axis=0) - counts
    seg_len = (jnp.sum(counts, axis=0) + tm - 1) // tm * tm
    seg_end = jnp.cumsum(seg_len)
    base = (seg_end - seg_len)[None, :] + block_start
    plan = block_plan.reshape(-1, 8, rows)
    experts = jnp.arange(E, dtype=i32)[None, None, :]

    def sorted_rows(expert, rank):
        row = jnp.sum(jnp.where(expert[:, :, None] == experts, base[:, None, :], 0), axis=-1) + rank
        return row.reshape(B, per_row, R, rows // R).transpose(0, 2, 1, 3).reshape(-1)

    tile_start = jnp.arange(p_max // tm, dtype=i32) * tm
    tile_expert = jnp.minimum(jnp.sum(seg_end[None, :] <= tile_start[:, None], axis=1), E - 1).astype(i32)
    return dict(rows=(sorted_rows(plan[:, 0], plan[:, 1]), sorted_rows(plan[:, 2], plan[:, 3])),
                tile_expert=tile_expert, n_active_tiles=(seg_end[-1] // tm).reshape(1).astype(i32), p_max=p_max)


def _sc_rows_kernel(n_rows, n_out, d, scatter):
    window_rows = SC_SCATTER_WINDOW if scatter else SC_GATHER_WINDOW
    per_worker = n_rows // SC_WORKERS
    assert n_rows % (SC_WORKERS * window_rows) == 0
    mesh = plsc.VectorSubcoreMesh(core_axis_name="c", subcore_axis_name="s")
    out_type = (jax.ShapeDtypeStruct((n_out, d), U32) if scatter
                else [jax.ShapeDtypeStruct((n_rows, d), U32)] * 2)

    def body(*refs):
        if scatter:
            src_hbm, hi_hbm, lo_hbm, out_hbm, idx_hi, idx_lo, rows, sem_hi, sem_lo = refs
        else:
            (src_hbm, hi_hbm, lo_hbm, out_hi_hbm, out_lo_hbm, idx_hi, idx_lo, rows_hi, rows_lo,
             sem_hi, sem_lo, sem_out_hi, sem_out_lo) = refs
        worker = lax.axis_index("s") * SC_CORES + lax.axis_index("c")

        @pl.loop(0, per_worker // window_rows)
        def _(i):
            window = pl.ds(worker * per_worker + i * window_rows, window_rows)
            pltpu.sync_copy(hi_hbm.at[window], idx_hi)
            pltpu.sync_copy(lo_hbm.at[window], idx_lo)
            if scatter:
                pltpu.sync_copy(src_hbm.at[window], rows)
                to_hi = pltpu.async_copy(rows, out_hbm.at[idx_hi], sem_hi)
                to_lo = pltpu.async_copy(rows, out_hbm.at[idx_lo], sem_lo)
                to_hi.wait()
                to_lo.wait()
            else:
                from_hi = pltpu.async_copy(src_hbm.at[idx_hi], rows_hi, sem_hi)
                from_lo = pltpu.async_copy(src_hbm.at[idx_lo], rows_lo, sem_lo)
                from_hi.wait()
                out_hi = pltpu.async_copy(rows_hi, out_hi_hbm.at[window], sem_out_hi)
                from_lo.wait()
                out_lo = pltpu.async_copy(rows_lo, out_lo_hbm.at[window], sem_out_lo)
                out_hi.wait()
                out_lo.wait()

    index_vec, row_buf = pltpu.VMEM((window_rows,), jnp.int32), pltpu.VMEM((window_rows, d), U32)
    dma_sem = pltpu.SemaphoreType.DMA
    scratch = ([index_vec, index_vec, row_buf, dma_sem, dma_sem] if scatter
               else [index_vec, index_vec, row_buf, row_buf, dma_sem, dma_sem, dma_sem, dma_sem])
    return pl.kernel(body, mesh=mesh, out_type=out_type, scratch_types=scratch)


def _moe_experts_kernel(te_ref, na_ref, xs_ref, wg_hbm, wu_hbm, wd_hbm, o_ref,
                        cache_g, cache_u, cache_d, stage_g, stage_u, stage_d, sems, acc, *, tf):
    i = pl.program_id(0)
    e = te_ref[i]
    n_chunks = cache_g.shape[-1] // tf
    active = i < na_ref[0]
    new_expert = jnp.logical_or(i == 0, e != te_ref[jnp.maximum(i - 1, 0)])

    def chunk_copies(c, slot):
        cols = pl.ds(c * tf, tf)
        return (pltpu.make_async_copy(wg_hbm.at[e, :, cols], stage_g.at[slot], sems.at[0, slot]),
                pltpu.make_async_copy(wu_hbm.at[e, :, cols], stage_u.at[slot], sems.at[1, slot]),
                pltpu.make_async_copy(wd_hbm.at[e, cols, :], stage_d.at[slot], sems.at[2, slot]))

    def tile_ffn(load_weights):
        x = _unpack_halves(xs_ref[...]).astype(BF16)
        if load_weights:
            for cp in chunk_copies(0, 0):
                cp.start()
        for c in range(n_chunks):
            cols = slice(c * tf, (c + 1) * tf)
            if load_weights:
                slot = c % 2
                if c + 1 < n_chunks:
                    for cp in chunk_copies(c + 1, 1 - slot):
                        cp.start()
                for cp in chunk_copies(c, slot):
                    cp.wait()
                cache_g[:, cols] = stage_g[slot].astype(BF16)
                cache_u[:, cols] = stage_u[slot].astype(BF16)
                cache_d[cols, :] = stage_d[slot].astype(BF16)
            hid = _swiglu_hidden(x, cache_g[:, cols], cache_u[:, cols])
            part = jnp.dot(hid, cache_d[cols, :], preferred_element_type=F32)
            if c == 0:
                acc[...] = part
            else:
                acc[...] += part
        o_ref[...] = _pack_halves(acc[...])

    pl.when(jnp.logical_and(active, new_expert))(lambda: tile_ffn(True))
    pl.when(jnp.logical_and(active, jnp.logical_not(new_expert)))(lambda: tile_ffn(False))


def _moe_experts(xs, plan, wg, wu, wd, *, tm, tf_target=512):
    P = xs.shape[0]
    E, D, F = wg.shape
    tf = _pick_chunk(F, tf_target)
    tile = lambda i, te, na: (jnp.minimum(i, na[0] - 1), 0)
    in_hbm = pl.BlockSpec(memory_space=pl.ANY)
    grid_spec = pltpu.PrefetchScalarGridSpec(
        num_scalar_prefetch=2,
        grid=(P // tm,),
        in_specs=[pl.BlockSpec((tm, D // 2), tile), in_hbm, in_hbm, in_hbm],
        out_specs=pl.BlockSpec((tm, D // 2), tile),
        scratch_shapes=[pltpu.VMEM((D, F), BF16), pltpu.VMEM((D, F), BF16), pltpu.VMEM((F, D), BF16),
                        pltpu.VMEM((2, D, tf), F32), pltpu.VMEM((2, D, tf), F32), pltpu.VMEM((2, tf, D), F32),
                        pltpu.SemaphoreType.DMA((3, 2)), pltpu.VMEM((tm, D), F32)],
    )
    return pl.pallas_call(
        functools.partial(_moe_experts_kernel, tf=tf),
        grid_spec=grid_spec,
        out_shape=jax.ShapeDtypeStruct((P, D // 2), U32),
        compiler_params=_params("arbitrary", vmem_limit=MOE_VMEM_LIMIT),
        name="moe_experts",
    )(plan["tile_expert"], plan["n_active_tiles"], xs, wg, wu, wd)


def _moe_mix_kernel(*refs, final, natural_out, carried):
    refs = list(refs)
    o_ref = refs.pop()
    if carried:
        refs.pop()
    x_ref, yh_ref, yl_ref, g_ref = refs[:4]
    gfin_ref = refs[4] if final else None
    gates = _token_rows(g_ref)
    lane = lax.broadcasted_iota(jnp.int32, gates.shape, 1)
    g_hi = jnp.sum(jnp.where(lane == 0, gates, 0.0), axis=-1, keepdims=True)
    g_lo = jnp.sum(jnp.where(lane == 1, gates, 0.0), axis=-1, keepdims=True)
    y = g_hi * _unpack_halves(_token_rows(yh_ref)) + g_lo * _unpack_halves(_token_rows(yl_ref))
    out = _finish(_token_rows(x_ref), y, gfin_ref).reshape(x_ref.shape)
    o_ref[...] = pltpu.einshape("rid->ird", out) if natural_out else out


def _moe_mix(x, y_hi, y_lo, pair_gates, g_final, *, natural_out, first_row, partial_out, ti=64):
    B, R, SI, D = x.shape
    n_rows = y_hi.shape[0] // (R * SI)
    ti = min(ti, SI)
    final = g_final is not None
    seq = lambda c: pl.BlockSpec((None, R, ti, c), lambda b, s: (first_row + b, 0, s, 0))
    local = lambda c: pl.BlockSpec((None, R, ti, c), lambda b, s: (b, 0, s, 0))
    as_tokens = lambda t: t.reshape(n_rows, R, SI, t.shape[-1])
    in_specs = [seq(D), local(D // 2), local(D // 2), seq(LANES)]
    args = [x, as_tokens(y_hi), as_tokens(y_lo), pair_gates]
    if final:
        in_specs.append(pl.BlockSpec((1, D), lambda b, s: (0, 0)))
        args.append(g_final.reshape(1, D))
    aliases = {}
    if partial_out is not None:
        aliases = {len(args): 0}
        in_specs.append(pl.BlockSpec(memory_space=pl.ANY))
        args.append(partial_out)
    out_spec = pl.BlockSpec((None, ti, R, D), lambda b, s: (first_row + b, s, 0, 0)) if natural_out else seq(D)
    return pl.pallas_call(
        functools.partial(_moe_mix_kernel, final=final, natural_out=natural_out, carried=partial_out is not None),
        grid=(n_rows, SI // ti), in_specs=in_specs, out_specs=out_spec,
        out_shape=jax.ShapeDtypeStruct((B, SI, R, D) if natural_out else (B, R, SI, D), F32),
        input_output_aliases=aliases,
        compiler_params=_params("parallel", "parallel"),
        name="moe_mix",
    )(*args)


def _moe(h, x, routing, wg, wu, wd, g_final, *, natural_out, tm=512):
    pair_gates, block_plan, block_counts = routing
    B, R, SI, half = h.shape
    N = B * R * SI
    plan = _plan_rows(block_plan, block_counts, (B, R, SI), tm=tm)
    rows_hi, rows_lo = plan["rows"]
    xs = _sc_rows_kernel(N, plan["p_max"], half, scatter=True)(h.reshape(N, half), rows_hi, rows_lo)
    ys = _moe_experts(xs, plan, wg, wu, wd, tm=tm)
    groups = MOE_RETURN_GROUPS if B % MOE_RETURN_GROUPS == 0 else 1
    n = N // groups
    gather = _sc_rows_kernel(n, plan["p_max"], half, scatter=False)
    fetched = [gather(ys, rows_hi[g * n:(g + 1) * n], rows_lo[g * n:(g + 1) * n]) for g in range(groups)]
    out = None
    for g, (y_hi, y_lo) in enumerate(fetched):
        out = _moe_mix(x, y_hi, y_lo, pair_gates, g_final, natural_out=natural_out,
                       first_row=g * (B // groups), partial_out=out)
    return out


def kernel(x, norm_mix, w_in, w_pool, pool_scale, attn_gain, w_out, norm_ffn, ffn_wg, ffn_wu, ffn_wd,
           w_router, moe_wg, moe_wu, moe_wd, final_norm):
    B, S, D = x.shape
    depth = norm_mix.shape[0]
    bf = lambda t: t.astype(BF16)
    w_in, w_pool, w_out = bf(w_in), bf(w_pool), bf(w_out)
    ffn_wg, ffn_wu, ffn_wd = bf(ffn_wg), bf(ffn_wu), bf(ffn_wd)
    R = RESIDUES
    assert S % R == 0
    N, SI = B * S, S // R
    x = x.reshape(B, SI, R, D)
    for l in range(depth):
        first, last = l == 0, l == depth - 1
        ya, q, k, v, q32, k32, v32 = _mix_in(x, norm_mix[l], w_in[l], w_pool[l], pool_scale[l], natural_x=first)
        narrow = lambda dil: (R // dil) * 16 > WINDOW_STEPS
        branches = [_attn_branch(q32, k32, v32, dil) if narrow(dil) else _attn_branch(q, k, v, dil)
                    for _, dil in DILATED_PATTERNS]
        i, routed = l // 2, l % 2 == 1
        x1, h2, *routing = _mix_out(x, ya, [o for o, _ in branches], [lse for _, lse in branches],
                                    attn_gain[l], w_out[l], norm_ffn[l], w_router[i] if routed else None,
                                    natural_x=first)
        g_final = final_norm if last else None
        if routed:
            x = _moe(h2, x1, routing, moe_wg[i], moe_wu[i], moe_wd[i], g_final, natural_out=last)
        else:
            x = _ffn(h2.reshape(N, D), x1.reshape(N, D), ffn_wg[i], ffn_wu[i], ffn_wd[i], g_final)
            x = x.reshape(B, R, SI, D)
            if last:
                x = x.transpose(0, 2, 1, 3)
    return x.reshape(B, S, D)
```

```python
import functools

import jax
import jax.numpy as jnp
import numpy as np
from jax import lax
from jax.experimental import pallas as pl
from jax.experimental.pallas import tpu as pltpu
from jax.experimental.pallas import tpu_sc as plsc

F32 = jnp.float32
BF16 = jnp.bfloat16
U32 = jnp.uint32
HIGH_HALF = np.uint32(0xFFFF0000)

EPS = 1e-6
LANES = 128
HEAD_DIM = 64
POOL_WINDOWS = (2, 4, 8, 16)
POOL_HIST = 8
DILATED_PATTERNS = ((128, 1), (512, 4), (2048, 16))
WINDOW_STEPS = 128
RESIDUES = 16
ATTN_STEP_ROWS = 1024
ATTN_UNROLL_QUERIES = 1024
LOG2_E = 1.4426950408889634
SC_CORES, SC_SUBCORES = 2, 16
SC_WORKERS = SC_CORES * SC_SUBCORES
SC_SCATTER_WINDOW = 128
SC_GATHER_WINDOW = 64
MOE_RETURN_GROUPS = 2
assert all(w // d == WINDOW_STEPS and RESIDUES % d == 0 for w, d in DILATED_PATTERNS)
TOP_K = 2
MASKED = -1e30
VMEM_LIMIT = 48 * 1024 * 1024
MOE_VMEM_LIMIT = 56 * 1024 * 1024


def _rms(x, g):
    return x * lax.rsqrt(jnp.mean(x * x, axis=-1, keepdims=True) + EPS) * g


def _params(*sem, vmem_limit=VMEM_LIMIT):
    return pltpu.CompilerParams(dimension_semantics=sem, vmem_limit_bytes=vmem_limit)


def _token_rows(ref, natural=False):
    t = ref[...]
    if natural:
        t = pltpu.einshape("ird->rid", t)
    return t.reshape(-1, t.shape[-1])


def _pack_halves(t):
    bits = lax.bitcast_convert_type(t, U32)
    bits = (bits + np.uint32(0x7FFF) + ((bits >> 16) & np.uint32(1))) & HIGH_HALF
    half = t.shape[-1] // 2
    return (bits[:, :half] >> 16) | bits[:, half:]


def _unpack_halves(words):
    halves = [lax.bitcast_convert_type(w, F32) for w in (words << 16, words & HIGH_HALF)]
    return jnp.concatenate(halves, axis=-1)


def _mix_in_kernel(x_ref, g_ref, w_ref, wp_ref, ps_ref, ya_ref, q_ref, k_ref, v_ref, q32_ref, k32_ref, v32_ref,
                   ubuf, uprev, *, d_pool, d_attn, ti, natural_x):
    s = pl.program_id(1)
    R = RESIDUES
    blk = lambda t: t.reshape(R, ti, t.shape[-1])
    h = _rms(_token_rows(x_ref, natural_x), g_ref[...]).astype(BF16)
    u = jnp.dot(h, w_ref[:, :d_pool], preferred_element_type=F32)
    scale = HEAD_DIM ** -0.5 * LOG2_E
    for n, (ref, ref32) in enumerate(zip((q_ref, k_ref, v_ref), (q32_ref, k32_ref, v32_ref))):
        t = jnp.dot(h, w_ref[:, d_pool + n * d_attn:d_pool + (n + 1) * d_attn], preferred_element_type=F32)
        t = t * scale if n == 0 else t
        ref[...] = blk(t.astype(BF16))
        ref32[...] = blk(t)

    history = ubuf[:, POOL_HIST - 1:POOL_HIST, :]
    ubuf[:, POOL_HIST - 1:POOL_HIST, :] = jnp.where(s == 0, 0.0, history)
    ubuf[:, POOL_HIST:POOL_HIST + ti, :] = blk(u)
    uprev[...] = ubuf[:, POOL_HIST - 1:POOL_HIST - 1 + ti, :]
    at_start = (s * ti + lax.broadcasted_iota(jnp.int32, (ti, 1), 0)) == 0
    group = d_pool // len(POOL_WINDOWS)
    zs = []
    for gi, w in enumerate(POOL_WINDOWS):
        cols = slice(gi * group, (gi + 1) * group)
        ds = []
        for r in range(R):
            ug = ubuf[r, POOL_HIST:POOL_HIST + ti, cols]
            win = ug
            for back in range(1, w):
                rr = r - back
                win = win + (ubuf[rr, POOL_HIST:POOL_HIST + ti, cols] if rr >= 0 else uprev[rr + R, :, cols])
            cnt = jnp.where(at_start, float(min(r + 1, w)), float(w))
            ds.append(win / cnt - ug)
        d = jnp.concatenate(ds, axis=0).astype(BF16)
        zs.append(jnp.dot(d, wp_ref[gi], preferred_element_type=F32))
    z = jnp.concatenate(zs, axis=-1)
    ya_ref[...] = blk(_rms(z, ps_ref[...]).astype(BF16))
    ubuf[:, POOL_HIST - 1:POOL_HIST, :] = ubuf[:, POOL_HIST + ti - 1:POOL_HIST + ti, :]


def _mix_in(x, g, w_in, w_pool, pool_scale, *, natural_x, ti=64):
    B, R, SI, D = x.shape
    if natural_x:
        R, SI = SI, R
    d_pool = pool_scale.shape[-1]
    d_in = w_in.shape[-1]
    d_attn = (d_in - d_pool) // 3
    ti = min(ti, SI)
    assert R == RESIDUES >= max(POOL_WINDOWS) and SI % ti == 0 and ti % 16 == 0
    assert d_pool % (LANES * len(POOL_WINDOWS)) == 0
    seq_spec = lambda c: pl.BlockSpec((None, R, ti, c), lambda b, s: (b, 0, s, 0))
    full = lambda shape: pl.BlockSpec(shape, lambda b, s: (0,) * len(shape))
    out_sds = lambda c, dtype=BF16: jax.ShapeDtypeStruct((B, R, SI, c), dtype)
    return pl.pallas_call(
        functools.partial(_mix_in_kernel, d_pool=d_pool, d_attn=d_attn, ti=ti, natural_x=natural_x),
        grid=(B, SI // ti),
        in_specs=[pl.BlockSpec((None, ti, R, D), lambda b, s: (b, s, 0, 0)) if natural_x else seq_spec(D),
                  full((1, D)), full((D, d_in)), full(w_pool.shape), full((1, d_pool))],
        out_specs=[seq_spec(d_pool)] + [seq_spec(d_attn)] * 6,
        out_shape=[out_sds(d_pool)] + [out_sds(d_attn)] * 3 + [out_sds(d_attn, F32)] * 3,
        scratch_shapes=[pltpu.VMEM((R, POOL_HIST + ti, d_pool), F32), pltpu.VMEM((R, ti, d_pool), F32)],
        compiler_params=_params("parallel", "arbitrary"),
        name="mix_in",
    )(x, g.reshape(1, D), w_in, w_pool, pool_scale.reshape(1, d_pool))


def _attn_kernel(q_ref, k_ref, v_ref, o_ref, stat_ref, bias_ref, *, nq, strips, d_attn):
    _, classes, chunk, _ = q_ref.shape
    L = nq // strips
    nk, n_blocks = 2 * nq, chunk // L
    first_block = pl.program_id(2) * n_blocks
    row = lax.broadcasted_iota(jnp.int32, (nq, 1), 0)
    col = lax.broadcasted_iota(jnp.int32, (1, nk), 1)
    q_strip, q_row = row >> (L.bit_length() - 1), row & (L - 1)
    k_strip, k_row = col >> ((2 * L).bit_length() - 1), col & (2 * L - 1)
    back = strips * (q_row - k_row) + (q_strip - k_strip)

    def band(offset):
        rel = back + strips * offset
        return jnp.where((rel >= 0) & (rel <= WINDOW_STEPS), 0.0, MASKED).astype(F32)

    bias_ref[0] = band(0)
    bias_ref[1] = band(L)
    lane = lax.broadcasted_iota(jnp.int32, (nq, LANES), 1)
    lo_half = lane < HEAD_DIM
    n_heads = d_attn // HEAD_DIM

    def block(it, carry):
        g, j_local = it // n_blocks, it % n_blocks
        j = first_block + j_local
        q_rows = pl.ds(pl.multiple_of(j_local * L, L), L)
        k_rows = pl.ds(pl.multiple_of(jnp.maximum(j - 1, 0) * L, L), 2 * L)
        bias = bias_ref[jnp.minimum(j, 1)]
        stats = jnp.zeros((nq, LANES), F32)
        for hp in range(d_attn // LANES):
            cols = slice(hp * LANES, (hp + 1) * LANES)
            qp = q_ref[:, g, q_rows, cols].reshape(nq, LANES).astype(BF16)
            kk = k_ref[:, g, k_rows, cols].reshape(nk, LANES).astype(BF16)
            vv = v_ref[:, g, k_rows, cols].reshape(nk, LANES).astype(BF16)
            outs, head_stats = [], []
            for sub in range(2):
                keep = lo_half if sub == 0 else jnp.logical_not(lo_half)
                qm = jnp.where(keep, qp, jnp.zeros_like(qp))
                s = lax.dot_general(qm, kk, (((1,), (1,)), ((), ())), preferred_element_type=F32) + bias
                m = jnp.max(s, axis=-1, keepdims=True)
                p = jnp.exp2(s - m)
                l = jnp.sum(p, axis=-1, keepdims=True)
                outs.append(jnp.dot(p.astype(BF16), vv, preferred_element_type=F32))
                head_stats.append(jnp.where(lane < n_heads, m, l))
            o_pair = jnp.where(lo_half, outs[0], outs[1]).astype(o_ref.dtype)
            o_ref[:, g, q_rows, cols] = o_pair.reshape(strips, L, LANES)
            pair_stats = jnp.where((lane & 1) == 0, head_stats[0], head_stats[1])
            stats = jnp.where(((lane & (n_heads - 1)) >> 1) == hp, pair_stats, stats)
        stat_ref[:, g, q_rows, :] = stats.reshape(strips, L, LANES)
        return carry

    lax.fori_loop(0, classes * n_blocks, block, 0, unroll=ATTN_UNROLL_QUERIES // nq)


def _attn_branch(q, k, v, dil):
    B, R, SI, C = q.shape
    strips = R // dil
    row_tile = 8 * 4 // q.dtype.itemsize
    nq = max(WINDOW_STEPS, strips * row_tile)
    L = nq // strips
    classes = max(1, min(dil, ATTN_STEP_ROWS // (strips * SI)))
    chunk = min(SI, max(L, ATTN_STEP_ROWS // (strips * classes)))
    assert R % dil == 0 and SI % chunk == 0 and chunk % L == 0 and SI >= 2 * L and dil % classes == 0
    n_heads = C // HEAD_DIM
    assert C % LANES == 0 and 2 * n_heads <= LANES and n_heads & (n_heads - 1) == 0
    assert nq & (nq - 1) == 0 and L & (L - 1) == 0
    view = lambda t: t.reshape(B, strips, dil, SI, t.shape[-1])
    q_spec = lambda c: pl.BlockSpec((None, strips, classes, chunk, c), lambda b, g, s: (b, 0, g, s, 0))
    kv_spec = pl.BlockSpec((None, strips, classes, SI, C), lambda b, g, s: (b, 0, g, 0, 0),
                           pipeline_mode=pl.Buffered(1 if SI > chunk else 2))
    o, stats = pl.pallas_call(
        functools.partial(_attn_kernel, nq=nq, strips=strips, d_attn=C),
        scratch_shapes=[pltpu.VMEM((2, nq, 2 * nq), F32)],
        grid=(B, dil // classes, SI // chunk),
        in_specs=[q_spec(C), kv_spec, kv_spec],
        out_specs=[q_spec(C), q_spec(LANES)],
        out_shape=[jax.ShapeDtypeStruct((B, strips, dil, SI, C), q.dtype),
                   jax.ShapeDtypeStruct((B, strips, dil, SI, LANES), F32)],
        compiler_params=_params("parallel", "parallel", "arbitrary"),
        name=f"attn_d{dil}",
    )(view(q), view(k), view(v))
    return o.reshape(B, R, SI, C), stats.reshape(B, R, SI, LANES)


def _route(h, wr, earlier, n_experts):
    h_hi = h.astype(BF16)
    h_lo = (h - h_hi.astype(F32)).astype(BF16)
    both = jnp.dot(h_hi, wr, preferred_element_type=F32)
    logits = both[:, :LANES] + both[:, LANES:] + jnp.dot(h_lo, wr[:, :LANES], preferred_element_type=F32)
    rows = h.shape[0]
    scores = logits.T[:n_experts]
    expert = lax.broadcasted_iota(jnp.int32, scores.shape, 0)
    picks = []
    for _ in range(TOP_K):
        m = jnp.max(scores, axis=0, keepdims=True)
        idx = jnp.min(jnp.where(scores == m, expert, n_experts), axis=0, keepdims=True)
        picks.append((m, idx))
        scores = jnp.where(expert == idx, -jnp.inf, scores)
    (m1, i1), (m2, i2) = picks
    e2 = jnp.exp(m2 - m1)
    g1 = 1.0 / (1.0 + e2)
    g2 = e2 / (1.0 + e2)
    picked = jnp.where(jnp.logical_or(expert == i1, expert == i2), 1.0, 0.0)
    rank = jnp.dot(picked.astype(BF16), earlier, preferred_element_type=F32)
    first_is_hi = i1 > i2
    e_hi, e_lo = jnp.maximum(i1, i2), jnp.minimum(i1, i2)
    rank_of = lambda e: jnp.sum(jnp.where(expert == e, rank, 0.0), axis=0, keepdims=True).astype(jnp.int32)
    plan = jnp.concatenate([e_hi, rank_of(e_hi), e_lo, rank_of(e_lo), jnp.zeros((4, rows), jnp.int32)], axis=0)
    gate_rows = jnp.concatenate([jnp.where(first_is_hi, g1, g2), jnp.where(first_is_hi, g2, g1),
                                 jnp.zeros((LANES - 2, rows), F32)], axis=0)
    return gate_rows.T, plan, jnp.sum(picked, axis=1, keepdims=True)


def _mix_out_kernel(*refs, d_pool, natural_x, n_experts):
    if n_experts:
        (x_ref, ya_ref, o1_ref, o2_ref, o3_ref, l1_ref, l2_ref, l3_ref, gain_ref, wo_ref, gffn_ref, exp_ref,
         wr_ref, earlier_ref, x1_ref, h2_ref, gates_ref, plan_ref, counts_ref) = refs
    else:
        (x_ref, ya_ref, o1_ref, o2_ref, o3_ref, l1_ref, l2_ref, l3_ref, gain_ref, wo_ref, gffn_ref, exp_ref,
         x1_ref, h2_ref) = refs
    n_heads = o1_ref.shape[-1] // HEAD_DIM
    stats = [_token_rows(l) for l in (l1_ref, l2_ref, l3_ref)]
    top = jnp.maximum(jnp.maximum(stats[0], stats[1]), stats[2])
    es = [jnp.exp2(t - top) for t in stats]
    sums = [pltpu.roll(t, LANES - n_heads, axis=1) for t in stats]
    den = es[0] * sums[0] + es[1] * sums[1] + es[2] * sums[2]
    head_lane = lax.broadcasted_iota(jnp.int32, den.shape, 1) < n_heads
    expand = exp_ref[...]

    def per_lane(w):
        hi = w.astype(BF16)
        lo = (w - hi.astype(F32)).astype(BF16)
        return jnp.dot(jnp.concatenate([hi, lo], axis=1), expand, preferred_element_type=F32)

    o = 0.0
    for e, o_ref in zip(es, (o1_ref, o2_ref, o3_ref)):
        o = o + per_lane(jnp.where(head_lane, e / den, 0.0)) * _token_rows(o_ref).astype(F32)
    yb = _rms(o, gain_ref[...]).astype(BF16)
    y = jnp.dot(jnp.concatenate([_token_rows(ya_ref), yb], axis=1), wo_ref[...], preferred_element_type=F32)
    x1 = _token_rows(x_ref, natural_x) + y
    x1_ref[...] = x1.reshape(x1_ref.shape)
    h2 = _rms(x1, gffn_ref[...])
    if n_experts:
        gate_tile, plan, counts = _route(h2, wr_ref[...], earlier_ref[...], n_experts)
        gates_ref[...] = gate_tile.reshape(gates_ref.shape)
        plan_ref[...] = plan
        counts_ref[...] = jnp.broadcast_to(counts, counts_ref.shape)
        h2_ref[...] = _pack_halves(h2).reshape(h2_ref.shape)
    else:
        h2_ref[...] = h2.astype(h2_ref.dtype).reshape(h2_ref.shape)


def _mix_out(x, ya, outs, lses, attn_gain, w_out, g_ffn, w_router, *, natural_x, ti=64):
    B, R, SI, d_pool = ya.shape
    D = x.shape[-1]
    d_attn = outs[0].shape[-1]
    ti = min(ti, SI)
    head_of_lane = jnp.arange(d_attn) // HEAD_DIM
    expand = (jnp.arange(2 * LANES)[:, None] % LANES == head_of_lane[None, :]).astype(BF16)
    seq = lambda c: pl.BlockSpec((None, R, ti, c), lambda b, s: (b, 0, s, 0))
    x_spec = pl.BlockSpec((None, ti, R, D), lambda b, s: (b, s, 0, 0)) if natural_x else seq(D)
    full = lambda shape: pl.BlockSpec(shape, lambda b, s: (0,) * len(shape))
    tokens = lambda c, dtype: jax.ShapeDtypeStruct((B, R, SI, c), dtype)
    in_specs = ([x_spec, seq(d_pool)] + [seq(d_attn)] * 3 + [seq(LANES)] * 3
                + [full((1, d_attn)), full(w_out.shape), full((1, D)), full(expand.shape)])
    args = [x, ya, *outs, *lses, attn_gain.reshape(1, d_attn), w_out, g_ffn.reshape(1, D), expand]
    if w_router is None:
        n_experts = 0
        out_specs, out_shape = [seq(D), seq(D)], [tokens(D, F32), tokens(D, BF16)]
    else:
        n_experts = w_router.shape[-1]
        assert 2 * n_experts <= LANES and TOP_K == 2
        wr = jnp.zeros((D, LANES), F32).at[:, :n_experts].set(w_router)
        wr_hi = wr.astype(BF16)
        rows = R * ti
        earlier = (jnp.arange(rows)[:, None] < jnp.arange(rows)[None, :]).astype(BF16)
        in_specs += [full((D, 2 * LANES)), full((rows, rows))]
        args += [jnp.concatenate([wr_hi, (wr - wr_hi.astype(F32)).astype(BF16)], axis=1), earlier]
        per_block = lambda r, c: pl.BlockSpec((None, None, r, c), lambda b, s: (b, s, 0, 0))
        out_specs = [seq(D), seq(D // 2), seq(LANES), per_block(8, rows), per_block(n_experts, LANES)]
        out_shape = [tokens(D, F32), tokens(D // 2, U32), tokens(LANES, F32),
                     jax.ShapeDtypeStruct((B, SI // ti, 8, rows), jnp.int32),
                     jax.ShapeDtypeStruct((B, SI // ti, n_experts, LANES), F32)]
    return pl.pallas_call(
        functools.partial(_mix_out_kernel, d_pool=d_pool, natural_x=natural_x, n_experts=n_experts),
        grid=(B, SI // ti),
        in_specs=in_specs, out_specs=out_specs, out_shape=out_shape,
        compiler_params=_params("parallel", "parallel"),
        name="mix_out",
    )(*args)


def _swiglu_hidden(h, wg, wu):
    a = jnp.dot(h, wg, preferred_element_type=F32)
    b = jnp.dot(h, wu, preferred_element_type=F32)
    return (a * jax.nn.sigmoid(a) * b).astype(BF16)


def _finish(x, y, gfin_ref):
    out = x + y
    return out if gfin_ref is None else _rms(out, gfin_ref[...])


def _ffn_kernel(*refs, final, tf):
    if final:
        h_ref, x_ref, wg_ref, wu_ref, wd_ref, gfin_ref, o_ref, hid = refs
    else:
        (h_ref, x_ref, wg_ref, wu_ref, wd_ref, o_ref, hid), gfin_ref = refs, None
    h = h_ref[...]
    for c in range(wg_ref.shape[-1] // tf):
        cols = slice(c * tf, (c + 1) * tf)
        hid[:, cols] = _swiglu_hidden(h, wg_ref[:, cols], wu_ref[:, cols])
    y = jnp.dot(hid[...], wd_ref[...], preferred_element_type=F32)
    o_ref[...] = _finish(x_ref[...], y, gfin_ref)


def _pick_chunk(n, target):
    best = None
    for c in range(LANES, min(n, target) + 1, LANES):
        if n % c == 0:
            best = c
    assert best is not None
    return best


def _ffn(h, x, wg, wu, wd, g_final, *, tm=1024, tf_target=256):
    N, D = x.shape
    F = wg.shape[-1]
    tm = min(tm, N)
    tf = _pick_chunk(F, tf_target)
    final = g_final is not None
    row = pl.BlockSpec((tm, D), lambda i: (i, 0))
    resident = lambda shape: pl.BlockSpec(shape, lambda i: (0, 0), pipeline_mode=pl.Buffered(1))
    in_specs = [row, row, resident((D, F)), resident((D, F)), resident((F, D))]
    args = [h, x, wg, wu, wd]
    if final:
        in_specs.append(pl.BlockSpec((1, D), lambda i: (0, 0)))
        args.append(g_final.reshape(1, D))
    return pl.pallas_call(
        functools.partial(_ffn_kernel, final=final, tf=tf),
        grid=(N // tm,),
        in_specs=in_specs,
        out_specs=row,
        out_shape=jax.ShapeDtypeStruct((N, D), F32),
        scratch_shapes=[pltpu.VMEM((tm, F), BF16)],
        compiler_params=_params("parallel"),
        name="ffn_dense",
    )(*args)


def _plan_rows(block_plan, block_counts, token_shape, *, tm):
    B, R, SI = token_shape
    i32 = jnp.int32
    _, per_row, _, rows = block_plan.shape
    E = block_counts.shape[2]
    p_max = TOP_K * B * R * SI + E * tm
    counts = block_counts[..., 0].astype(i32).reshape(-1, E)
    block_start = jnp.cumsum(counts, axis=0) - counts
    seg_len = (jnp.sum(counts, axis=0) + tm - 1) // tm * tm
    seg_end = jnp.cumsum(seg_len)
    base = (seg_end - seg_len)[None, :] + block_start
    plan = block_plan.reshape(-1, 8, rows)
    experts = jnp.arange(E, dtype=i32)[None, None, :]

    def sorted_rows(expert, rank):
        row = jnp.sum(jnp.where(expert[:, :, None] == experts, base[:, None, :], 0), axis=-1) + rank
        return row.reshape(B, per_row, R, rows // R).transpose(0, 2, 1, 3).reshape(-1)

    tile_start = jnp.arange(p_max // tm, dtype=i32) * tm
    tile_expert = jnp.minimum(jnp.sum(seg_end[None, :] <= tile_start[:, None], axis=1), E - 1).astype(i32)
    return dict(rows=(sorted_rows(plan[:, 0], plan[:, 1]), sorted_rows(plan[:, 2], plan[:, 3])),
                tile_expert=tile_expert, n_active_tiles=(seg_end[-1] // tm).reshape(1).astype(i32), p_max=p_max)


def _sc_rows_kernel(n_rows, n_out, d, scatter):
    window_rows = SC_SCATTER_WINDOW if scatter else SC_GATHER_WINDOW
    per_worker = n_rows // SC_WORKERS
    assert n_rows % (SC_WORKERS * window_rows) == 0
    mesh = plsc.VectorSubcoreMesh(core_axis_name="c", subcore_axis_name="s")
    out_type = (jax.ShapeDtypeStruct((n_out, d), U32) if scatter
                else [jax.ShapeDtypeStruct((n_rows, d), U32)] * 2)

    def body(*refs):
        if scatter:
            src_hbm, hi_hbm, lo_hbm, out_hbm, idx_hi, idx_lo, rows, sem_hi, sem_lo = refs
        else:
            (src_hbm, hi_hbm, lo_hbm, out_hi_hbm, out_lo_hbm, idx_hi, idx_lo, rows_hi, rows_lo,
             sem_hi, sem_lo, sem_out_hi, sem_out_lo) = refs
        worker = lax.axis_index("s") * SC_CORES + lax.axis_index("c")

        @pl.loop(0, per_worker // window_rows)
        def _(i):
            window = pl.ds(worker * per_worker + i * window_rows, window_rows)
            pltpu.sync_copy(hi_hbm.at[window], idx_hi)
            pltpu.sync_copy(lo_hbm.at[window], idx_lo)
            if scatter:
                pltpu.sync_copy(src_hbm.at[window], rows)
                to_hi = pltpu.async_copy(rows, out_hbm.at[idx_hi], sem_hi)
                to_lo = pltpu.async_copy(rows, out_hbm.at[idx_lo], sem_lo)
                to_hi.wait()
                to_lo.wait()
            else:
                from_hi = pltpu.async_copy(src_hbm.at[idx_hi], rows_hi, sem_hi)
                from_lo = pltpu.async_copy(src_hbm.at[idx_lo], rows_lo, sem_lo)
                from_hi.wait()
                out_hi = pltpu.async_copy(rows_hi, out_hi_hbm.at[window], sem_out_hi)
                from_lo.wait()
                out_lo = pltpu.async_copy(rows_lo, out_lo_hbm.at[window], sem_out_lo)
                out_hi.wait()
                out_lo.wait()

    index_vec, row_buf = pltpu.VMEM((window_rows,), jnp.int32), pltpu.VMEM((window_rows, d), U32)
    dma_sem = pltpu.SemaphoreType.DMA
    scratch = ([index_vec, index_vec, row_buf, dma_sem, dma_sem] if scatter
               else [index_vec, index_vec, row_buf, row_buf, dma_sem, dma_sem, dma_sem, dma_sem])
    return pl.kernel(body, mesh=mesh, out_type=out_type, scratch_types=scratch)


def _moe_experts_kernel(te_ref, na_ref, xs_ref, wg_hbm, wu_hbm, wd_hbm, o_ref,
                        cache_g, cache_u, cache_d, stage_g, stage_u, stage_d, sems, acc, *, tf):
    i = pl.program_id(0)
    e = te_ref[i]
    n_chunks = cache_g.shape[-1] // tf
    active = i < na_ref[0]
    new_expert = jnp.logical_or(i == 0, e != te_ref[jnp.maximum(i - 1, 0)])

    def chunk_copies(c, slot):
        cols = pl.ds(c * tf, tf)
        return (pltpu.make_async_copy(wg_hbm.at[e, :, cols], stage_g.at[slot], sems.at[0, slot]),
                pltpu.make_async_copy(wu_hbm.at[e, :, cols], stage_u.at[slot], sems.at[1, slot]),
                pltpu.make_async_copy(wd_hbm.at[e, cols, :], stage_d.at[slot], sems.at[2, slot]))

    def tile_ffn(load_weights):
        x = _unpack_halves(xs_ref[...]).astype(BF16)
        if load_weights:
            for cp in chunk_copies(0, 0):
                cp.start()
        for c in range(n_chunks):
            cols = slice(c * tf, (c + 1) * tf)
            if load_weights:
                slot = c % 2
                if c + 1 < n_chunks:
                    for cp in chunk_copies(c + 1, 1 - slot):
                        cp.start()
                for cp in chunk_copies(c, slot):
                    cp.wait()
                cache_g[:, cols] = stage_g[slot].astype(BF16)
                cache_u[:, cols] = stage_u[slot].astype(BF16)
                cache_d[cols, :] = stage_d[slot].astype(BF16)
            hid = _swiglu_hidden(x, cache_g[:, cols], cache_u[:, cols])
            part = jnp.dot(hid, cache_d[cols, :], preferred_element_type=F32)
            if c == 0:
                acc[...] = part
            else:
                acc[...] += part
        o_ref[...] = _pack_halves(acc[...])

    pl.when(jnp.logical_and(active, new_expert))(lambda: tile_ffn(True))
    pl.when(jnp.logical_and(active, jnp.logical_not(new_expert)))(lambda: tile_ffn(False))


def _moe_experts(xs, plan, wg, wu, wd, *, tm, tf_target=512):
    P = xs.shape[0]
    E, D, F = wg.shape
    tf = _pick_chunk(F, tf_target)
    tile = lambda i, te, na: (jnp.minimum(i, na[0] - 1), 0)
    in_hbm = pl.BlockSpec(memory_space=pl.ANY)
    grid_spec = pltpu.PrefetchScalarGridSpec(
        num_scalar_prefetch=2,
        grid=(P // tm,),
        in_specs=[pl.BlockSpec((tm, D // 2), tile), in_hbm, in_hbm, in_hbm],
        out_specs=pl.BlockSpec((tm, D // 2), tile),
        scratch_shapes=[pltpu.VMEM((D, F), BF16), pltpu.VMEM((D, F), BF16), pltpu.VMEM((F, D), BF16),
                        pltpu.VMEM((2, D, tf), F32), pltpu.VMEM((2, D, tf), F32), pltpu.VMEM((2, tf, D), F32),
                        pltpu.SemaphoreType.DMA((3, 2)), pltpu.VMEM((tm, D), F32)],
    )
    return pl.pallas_call(
        functools.partial(_moe_experts_kernel, tf=tf),
        grid_spec=grid_spec,
        out_shape=jax.ShapeDtypeStruct((P, D // 2), U32),
        compiler_params=_params("arbitrary", vmem_limit=MOE_VMEM_LIMIT),
        name="moe_experts",
    )(plan["tile_expert"], plan["n_active_tiles"], xs, wg, wu, wd)


def _moe_mix_kernel(*refs, final, natural_out, carried):
    refs = list(refs)
    o_ref = refs.pop()
    if carried:
        refs.pop()
    x_ref, yh_ref, yl_ref, g_ref = refs[:4]
    gfin_ref = refs[4] if final else None
    gates = _token_rows(g_ref)
    lane = lax.broadcasted_iota(jnp.int32, gates.shape, 1)
    g_hi = jnp.sum(jnp.where(lane == 0, gates, 0.0), axis=-1, keepdims=True)
    g_lo = jnp.sum(jnp.where(lane == 1, gates, 0.0), axis=-1, keepdims=True)
    y = g_hi * _unpack_halves(_token_rows(yh_ref)) + g_lo * _unpack_halves(_token_rows(yl_ref))
    out = _finish(_token_rows(x_ref), y, gfin_ref).reshape(x_ref.shape)
    o_ref[...] = pltpu.einshape("rid->ird", out) if natural_out else out


def _moe_mix(x, y_hi, y_lo, pair_gates, g_final, *, natural_out, first_row, partial_out, ti=64):
    B, R, SI, D = x.shape
    n_rows = y_hi.shape[0] // (R * SI)
    ti = min(ti, SI)
    final = g_final is not None
    seq = lambda c: pl.BlockSpec((None, R, ti, c), lambda b, s: (first_row + b, 0, s, 0))
    local = lambda c: pl.BlockSpec((None, R, ti, c), lambda b, s: (b, 0, s, 0))
    as_tokens = lambda t: t.reshape(n_rows, R, SI, t.shape[-1])
    in_specs = [seq(D), local(D // 2), local(D // 2), seq(LANES)]
    args = [x, as_tokens(y_hi), as_tokens(y_lo), pair_gates]
    if final:
        in_specs.append(pl.BlockSpec((1, D), lambda b, s: (0, 0)))
        args.append(g_final.reshape(1, D))
    aliases = {}
    if partial_out is not None:
        aliases = {len(args): 0}
        in_specs.append(pl.BlockSpec(memory_space=pl.ANY))
        args.append(partial_out)
    out_spec = pl.BlockSpec((None, ti, R, D), lambda b, s: (first_row + b, s, 0, 0)) if natural_out else seq(D)
    return pl.pallas_call(
        functools.partial(_moe_mix_kernel, final=final, natural_out=natural_out, carried=partial_out is not None),
        grid=(n_rows, SI // ti), in_specs=in_specs, out_specs=out_spec,
        out_shape=jax.ShapeDtypeStruct((B, SI, R, D) if natural_out else (B, R, SI, D), F32),
        input_output_aliases=aliases,
        compiler_params=_params("parallel", "parallel"),
        name="moe_mix",
    )(*args)


def _moe(h, x, routing, wg, wu, wd, g_final, *, natural_out, tm=512):
    pair_gates, block_plan, block_counts = routing
    B, R, SI, half = h.shape
    N = B * R * SI
    plan = _plan_rows(block_plan, block_counts, (B, R, SI), tm=tm)
    rows_hi, rows_lo = plan["rows"]
    xs = _sc_rows_kernel(N, plan["p_max"], half, scatter=True)(h.reshape(N, half), rows_hi, rows_lo)
    ys = _moe_experts(xs, plan, wg, wu, wd, tm=tm)
    groups = MOE_RETURN_GROUPS if B % MOE_RETURN_GROUPS == 0 else 1
    n = N // groups
    gather = _sc_rows_kernel(n, plan["p_max"], half, scatter=False)
    fetched = [gather(ys, rows_hi[g * n:(g + 1) * n], rows_lo[g * n:(g + 1) * n]) for g in range(groups)]
    out = None
    for g, (y_hi, y_lo) in enumerate(fetched):
        out = _moe_mix(x, y_hi, y_lo, pair_gates, g_final, natural_out=natural_out,
                       first_row=g * (B // groups), partial_out=out)
    return out


def kernel(x, norm_mix, w_in, w_pool, pool_scale, attn_gain, w_out, norm_ffn, ffn_wg, ffn_wu, ffn_wd,
           w_router, moe_wg, moe_wu, moe_wd, final_norm):
    B, S, D = x.shape
    depth = norm_mix.shape[0]
    bf = lambda t: t.astype(BF16)
    w_in, w_pool, w_out = bf(w_in), bf(w_pool), bf(w_out)
    ffn_wg, ffn_wu, ffn_wd = bf(ffn_wg), bf(ffn_wu), bf(ffn_wd)
    R = RESIDUES
    assert S % R == 0
    N, SI = B * S, S // R
    x = x.reshape(B, SI, R, D)
    for l in range(depth):
        first, last = l == 0, l == depth - 1
        ya, q, k, v, q32, k32, v32 = _mix_in(x, norm_mix[l], w_in[l], w_pool[l], pool_scale[l], natural_x=first)
        narrow = lambda dil: (R // dil) * 16 > WINDOW_STEPS
        branches = [_attn_branch(q32, k32, v32, dil) if narrow(dil) else _attn_branch(q, k, v, dil)
                    for _, dil in DILATED_PATTERNS]
        i, routed = l // 2, l % 2 == 1
        x1, h2, *routing = _mix_out(x, ya, [o for o, _ in branches], [lse for _, lse in branches],
                                    attn_gain[l], w_out[l], norm_ffn[l], w_router[i] if routed else None,
                                    natural_x=first)
        g_final = final_norm if last else None
        if routed:
            x = _moe(h2, x1, routing, moe_wg[i], moe_wu[i], moe_wd[i], g_final, natural_out=last)
        else:
            x = _ffn(h2.reshape(N, D), x1.reshape(N, D), ffn_wg[i], ffn_wu[i], ffn_wd[i], g_final)
            x = x.reshape(B, R, SI, D)
            if last:
                x = x.transpose(0, 2, 1, 3)
    return x.reshape(B, S, D)
```

```python
import functools

import jax
import jax.numpy as jnp
from jax import lax
from jax.experimental import pallas as pl
from jax.experimental.pallas import tpu as pltpu
from jax.experimental.pallas import tpu_sc as plsc

F32 = jnp.float32
BF16 = jnp.bfloat16
U32 = jnp.uint32

EPS = 1e-6
LANES = 128
HEAD_DIM = 64
POOL_WINDOWS = (2, 4, 8, 16)
POOL_HIST = 8
DILATED_PATTERNS = ((128, 1), (512, 4), (2048, 16))
WINDOW_STEPS = 128
RESIDUES = 16
ATTN_STEP_ROWS = 1024
ATTN_UNROLL_QUERIES = 1024
LOG2_E = 1.4426950408889634
SC_CORES, SC_SUBCORES = 2, 16
SC_WORKERS = SC_CORES * SC_SUBCORES
SC_SCATTER_WINDOW = 128
SC_GATHER_WINDOW = 64
MOE_RETURN_GROUPS = 2
assert all(w // d == WINDOW_STEPS and RESIDUES % d == 0 for w, d in DILATED_PATTERNS)
TOP_K = 2
MASKED = -1e30
VMEM_LIMIT = 48 * 1024 * 1024
MOE_VMEM_LIMIT = 56 * 1024 * 1024


def _rms(x, g):
    return x * lax.rsqrt(jnp.mean(x * x, axis=-1, keepdims=True) + EPS) * g


def _params(*sem, vmem_limit=VMEM_LIMIT):
    return pltpu.CompilerParams(dimension_semantics=sem, vmem_limit_bytes=vmem_limit)


def _token_rows(ref, natural=False):
    t = ref[...]
    if natural:
        t = pltpu.einshape("ird->rid", t)
    return t.reshape(-1, t.shape[-1])


def _pack_halves(t):
    half = t.shape[-1] // 2
    return pltpu.pack_elementwise([t[:, :half], t[:, half:]], packed_dtype=BF16)


def _unpack_halves(words):
    halves = [pltpu.unpack_elementwise(words, index=k, packed_dtype=BF16, unpacked_dtype=F32) for k in range(2)]
    return jnp.concatenate(halves, axis=-1)


def _mix_in_kernel(x_ref, g_ref, w_ref, wp_ref, ps_ref, ya_ref, q_ref, k_ref, v_ref, q32_ref, k32_ref, v32_ref,
                   ubuf, uprev, *, d_pool, d_attn, ti, natural_x):
    s = pl.program_id(1)
    R = RESIDUES
    blk = lambda t: t.reshape(R, ti, t.shape[-1])
    h = _rms(_token_rows(x_ref, natural_x), g_ref[...]).astype(BF16)
    u = jnp.dot(h, w_ref[:, :d_pool], preferred_element_type=F32)
    scale = HEAD_DIM ** -0.5 * LOG2_E
    for n, (ref, ref32) in enumerate(zip((q_ref, k_ref, v_ref), (q32_ref, k32_ref, v32_ref))):
        t = jnp.dot(h, w_ref[:, d_pool + n * d_attn:d_pool + (n + 1) * d_attn], preferred_element_type=F32)
        t = t * scale if n == 0 else t
        ref[...] = blk(t.astype(BF16))
        ref32[...] = blk(t)

    history = ubuf[:, POOL_HIST - 1:POOL_HIST, :]
    ubuf[:, POOL_HIST - 1:POOL_HIST, :] = jnp.where(s == 0, 0.0, history)
    ubuf[:, POOL_HIST:POOL_HIST + ti, :] = blk(u)
    uprev[...] = ubuf[:, POOL_HIST - 1:POOL_HIST - 1 + ti, :]
    at_start = (s * ti + lax.broadcasted_iota(jnp.int32, (ti, 1), 0)) == 0
    group = d_pool // len(POOL_WINDOWS)
    zs = []
    for gi, w in enumerate(POOL_WINDOWS):
        cols = slice(gi * group, (gi + 1) * group)
        ds = []
        for r in range(R):
            ug = ubuf[r, POOL_HIST:POOL_HIST + ti, cols]
            win = ug
            for back in range(1, w):
                rr = r - back
                win = win + (ubuf[rr, POOL_HIST:POOL_HIST + ti, cols] if rr >= 0 else uprev[rr + R, :, cols])
            cnt = jnp.where(at_start, float(min(r + 1, w)), float(w))
            ds.append(win / cnt - ug)
        d = jnp.concatenate(ds, axis=0).astype(BF16)
        zs.append(jnp.dot(d, wp_ref[gi], preferred_element_type=F32))
    z = jnp.concatenate(zs, axis=-1)
    ya_ref[...] = blk(_rms(z, ps_ref[...]).astype(BF16))
    ubuf[:, POOL_HIST - 1:POOL_HIST, :] = ubuf[:, POOL_HIST + ti - 1:POOL_HIST + ti, :]


def _mix_in(x, g, w_in, w_pool, pool_scale, *, natural_x, ti=64):
    B, R, SI, D = x.shape
    if natural_x:
        R, SI = SI, R
    d_pool = pool_scale.shape[-1]
    d_in = w_in.shape[-1]
    d_attn = (d_in - d_pool) // 3
    ti = min(ti, SI)
    assert R == RESIDUES >= max(POOL_WINDOWS) and SI % ti == 0 and ti % 16 == 0
    assert d_pool % (LANES * len(POOL_WINDOWS)) == 0
    seq_spec = lambda c: pl.BlockSpec((None, R, ti, c), lambda b, s: (b, 0, s, 0))
    full = lambda shape: pl.BlockSpec(shape, lambda b, s: (0,) * len(shape))
    out_sds = lambda c, dtype=BF16: jax.ShapeDtypeStruct((B, R, SI, c), dtype)
    return pl.pallas_call(
        functools.partial(_mix_in_kernel, d_pool=d_pool, d_attn=d_attn, ti=ti, natural_x=natural_x),
        grid=(B, SI // ti),
        in_specs=[pl.BlockSpec((None, ti, R, D), lambda b, s: (b, s, 0, 0)) if natural_x else seq_spec(D),
                  full((1, D)), full((D, d_in)), full(w_pool.shape), full((1, d_pool))],
        out_specs=[seq_spec(d_pool)] + [seq_spec(d_attn)] * 6,
        out_shape=[out_sds(d_pool)] + [out_sds(d_attn)] * 3 + [out_sds(d_attn, F32)] * 3,
        scratch_shapes=[pltpu.VMEM((R, POOL_HIST + ti, d_pool), F32), pltpu.VMEM((R, ti, d_pool), F32)],
        compiler_params=_params("parallel", "arbitrary"),
        name="mix_in",
    )(x, g.reshape(1, D), w_in, w_pool, pool_scale.reshape(1, d_pool))


def _attn_kernel(q_ref, k_ref, v_ref, o_ref, stat_ref, bias_ref, *, nq, strips, d_attn):
    _, classes, chunk, _ = q_ref.shape
    L = nq // strips
    nk, n_blocks = 2 * nq, chunk // L
    first_block = pl.program_id(2) * n_blocks
    row = lax.broadcasted_iota(jnp.int32, (nq, 1), 0)
    col = lax.broadcasted_iota(jnp.int32, (1, nk), 1)
    q_strip, q_row = row >> (L.bit_length() - 1), row & (L - 1)
    k_strip, k_row = col >> ((2 * L).bit_length() - 1), col & (2 * L - 1)
    back = strips * (q_row - k_row) + (q_strip - k_strip)

    def band(offset):
        rel = back + strips * offset
        return jnp.where((rel >= 0) & (rel <= WINDOW_STEPS), 0.0, MASKED).astype(F32)

    bias_ref[0] = band(0)
    bias_ref[1] = band(L)
    lane = lax.broadcasted_iota(jnp.int32, (nq, LANES), 1)
    lo_half = lane < HEAD_DIM
    n_heads = d_attn // HEAD_DIM

    def block(it, carry):
        g, j_local = it // n_blocks, it % n_blocks
        j = first_block + j_local
        q_rows = pl.ds(pl.multiple_of(j_local * L, L), L)
        k_rows = pl.ds(pl.multiple_of(jnp.maximum(j - 1, 0) * L, L), 2 * L)
        bias = bias_ref[jnp.minimum(j, 1)]
        stats = jnp.zeros((nq, LANES), F32)
        for hp in range(d_attn // LANES):
            cols = slice(hp * LANES, (hp + 1) * LANES)
            qp = q_ref[:, g, q_rows, cols].reshape(nq, LANES).astype(BF16)
            kk = k_ref[:, g, k_rows, cols].reshape(nk, LANES).astype(BF16)
            vv = v_ref[:, g, k_rows, cols].reshape(nk, LANES).astype(BF16)
            outs, head_stats = [], []
            for sub in range(2):
                keep = lo_half if sub == 0 else jnp.logical_not(lo_half)
                qm = jnp.where(keep, qp, jnp.zeros_like(qp))
                s = lax.dot_general(qm, kk, (((1,), (1,)), ((), ())), preferred_element_type=F32) + bias
                m = jnp.max(s, axis=-1, keepdims=True)
                p = jnp.exp2(s - m)
                l = jnp.sum(p, axis=-1, keepdims=True)
                outs.append(jnp.dot(p.astype(BF16), vv, preferred_element_type=F32))
                head_stats.append(jnp.where(lane < n_heads, m, l))
            o_pair = jnp.where(lo_half, outs[0], outs[1]).astype(o_ref.dtype)
            o_ref[:, g, q_rows, cols] = o_pair.reshape(strips, L, LANES)
            pair_stats = jnp.where((lane & 1) == 0, head_stats[0], head_stats[1])
            stats = jnp.where(((lane & (n_heads - 1)) >> 1) == hp, pair_stats, stats)
        stat_ref[:, g, q_rows, :] = stats.reshape(strips, L, LANES)
        return carry

    lax.fori_loop(0, classes * n_blocks, block, 0, unroll=ATTN_UNROLL_QUERIES // nq)


def _attn_branch(q, k, v, dil):
    B, R, SI, C = q.shape
    strips = R // dil
    row_tile = 8 * 4 // q.dtype.itemsize
    nq = max(WINDOW_STEPS, strips * row_tile)
    L = nq // strips
    classes = max(1, min(dil, ATTN_STEP_ROWS // (strips * SI)))
    chunk = min(SI, max(L, ATTN_STEP_ROWS // (strips * classes)))
    assert R % dil == 0 and SI % chunk == 0 and chunk % L == 0 and SI >= 2 * L and dil % classes == 0
    n_heads = C // HEAD_DIM
    assert C % LANES == 0 and 2 * n_heads <= LANES and n_heads & (n_heads - 1) == 0
    assert nq & (nq - 1) == 0 and L & (L - 1) == 0
    view = lambda t: t.reshape(B, strips, dil, SI, t.shape[-1])
    q_spec = lambda c: pl.BlockSpec((None, strips, classes, chunk, c), lambda b, g, s: (b, 0, g, s, 0))
    kv_spec = pl.BlockSpec((None, strips, classes, SI, C), lambda b, g, s: (b, 0, g, 0, 0),
                           pipeline_mode=pl.Buffered(1 if SI > chunk else 2))
    o, stats = pl.pallas_call(
        functools.partial(_attn_kernel, nq=nq, strips=strips, d_attn=C),
        scratch_shapes=[pltpu.VMEM((2, nq, 2 * nq), F32)],
        grid=(B, dil // classes, SI // chunk),
        in_specs=[q_spec(C), kv_spec, kv_spec],
        out_specs=[q_spec(C), q_spec(LANES)],
        out_shape=[jax.ShapeDtypeStruct((B, strips, dil, SI, C), q.dtype),
                   jax.ShapeDtypeStruct((B, strips, dil, SI, LANES), F32)],
        compiler_params=_params("parallel", "parallel", "arbitrary"),
        name=f"attn_d{dil}",
    )(view(q), view(k), view(v))
    return o.reshape(B, R, SI, C), stats.reshape(B, R, SI, LANES)


def _route(h, wr, earlier, n_experts):
    h_hi = h.astype(BF16)
    h_lo = (h - h_hi.astype(F32)).astype(BF16)
    both = jnp.dot(h_hi, wr, preferred_element_type=F32)
    logits = both[:, :LANES] + both[:, LANES:] + jnp.dot(h_lo, wr[:, :LANES], preferred_element_type=F32)
    rows = h.shape[0]
    scores = logits.T[:n_experts]
    expert = lax.broadcasted_iota(jnp.int32, scores.shape, 0)
    picks = []
    for _ in range(TOP_K):
        m = jnp.max(scores, axis=0, keepdims=True)
        idx = jnp.min(jnp.where(scores == m, expert, n_experts), axis=0, keepdims=True)
        picks.append((m, idx))
        scores = jnp.where(expert == idx, -jnp.inf, scores)
    (m1, i1), (m2, i2) = picks
    e2 = jnp.exp(m2 - m1)
    g1 = 1.0 / (1.0 + e2)
    g2 = e2 / (1.0 + e2)
    picked = jnp.where(jnp.logical_or(expert == i1, expert == i2), 1.0, 0.0)
    rank = jnp.dot(picked.astype(BF16), earlier, preferred_element_type=F32)
    first_is_hi = i1 > i2
    e_hi, e_lo = jnp.maximum(i1, i2), jnp.minimum(i1, i2)
    rank_of = lambda e: jnp.sum(jnp.where(expert == e, rank, 0.0), axis=0, keepdims=True).astype(jnp.int32)
    plan = jnp.concatenate([e_hi, rank_of(e_hi), e_lo, rank_of(e_lo), jnp.zeros((4, rows), jnp.int32)], axis=0)
    gate_rows = jnp.concatenate([jnp.where(first_is_hi, g1, g2), jnp.where(first_is_hi, g2, g1),
                                 jnp.zeros((LANES - 2, rows), F32)], axis=0)
    return gate_rows.T, plan, jnp.sum(picked, axis=1, keepdims=True)


def _mix_out_kernel(*refs, d_pool, natural_x, n_experts):
    if n_experts:
        (x_ref, ya_ref, o1_ref, o2_ref, o3_ref, l1_ref, l2_ref, l3_ref, gain_ref, wo_ref, gffn_ref, exp_ref,
         wr_ref, earlier_ref, x1_ref, h2_ref, gates_ref, plan_ref, counts_ref) = refs
    else:
        (x_ref, ya_ref, o1_ref, o2_ref, o3_ref, l1_ref, l2_ref, l3_ref, gain_ref, wo_ref, gffn_ref, exp_ref,
         x1_ref, h2_ref) = refs
    n_heads = o1_ref.shape[-1] // HEAD_DIM
    stats = [_token_rows(l) for l in (l1_ref, l2_ref, l3_ref)]
    top = jnp.maximum(jnp.maximum(stats[0], stats[1]), stats[2])
    es = [jnp.exp2(t - top) for t in stats]
    sums = [pltpu.roll(t, LANES - n_heads, axis=1) for t in stats]
    den = es[0] * sums[0] + es[1] * sums[1] + es[2] * sums[2]
    head_lane = lax.broadcasted_iota(jnp.int32, den.shape, 1) < n_heads
    expand = exp_ref[...]

    def per_lane(w):
        hi = w.astype(BF16)
        lo = (w - hi.astype(F32)).astype(BF16)
        return jnp.dot(jnp.concatenate([hi, lo], axis=1), expand, preferred_element_type=F32)

    o = 0.0
    for e, o_ref in zip(es, (o1_ref, o2_ref, o3_ref)):
        o = o + per_lane(jnp.where(head_lane, e / den, 0.0)) * _token_rows(o_ref).astype(F32)
    yb = _rms(o, gain_ref[...]).astype(BF16)
    y = jnp.dot(jnp.concatenate([_token_rows(ya_ref), yb], axis=1), wo_ref[...], preferred_element_type=F32)
    x1 = _token_rows(x_ref, natural_x) + y
    x1_ref[...] = x1.reshape(x1_ref.shape)
    h2 = _rms(x1, gffn_ref[...])
    if n_experts:
        gate_tile, plan, counts = _route(h2, wr_ref[...], earlier_ref[...], n_experts)
        gates_ref[...] = gate_tile.reshape(gates_ref.shape)
        plan_ref[...] = plan
        counts_ref[...] = jnp.broadcast_to(counts, counts_ref.shape)
        h2_ref[...] = _pack_halves(h2).reshape(h2_ref.shape)
    else:
        h2_ref[...] = h2.astype(h2_ref.dtype).reshape(h2_ref.shape)


def _mix_out(x, ya, outs, lses, attn_gain, w_out, g_ffn, w_router, *, natural_x, ti=64):
    B, R, SI, d_pool = ya.shape
    D = x.shape[-1]
    d_attn = outs[0].shape[-1]
    ti = min(ti, SI)
    head_of_lane = jnp.arange(d_attn) // HEAD_DIM
    expand = (jnp.arange(2 * LANES)[:, None] % LANES == head_of_lane[None, :]).astype(BF16)
    seq = lambda c: pl.BlockSpec((None, R, ti, c), lambda b, s: (b, 0, s, 0))
    x_spec = pl.BlockSpec((None, ti, R, D), lambda b, s: (b, s, 0, 0)) if natural_x else seq(D)
    full = lambda shape: pl.BlockSpec(shape, lambda b, s: (0,) * len(shape))
    tokens = lambda c, dtype: jax.ShapeDtypeStruct((B, R, SI, c), dtype)
    in_specs = ([x_spec, seq(d_pool)] + [seq(d_attn)] * 3 + [seq(LANES)] * 3
                + [full((1, d_attn)), full(w_out.shape), full((1, D)), full(expand.shape)])
    args = [x, ya, *outs, *lses, attn_gain.reshape(1, d_attn), w_out, g_ffn.reshape(1, D), expand]
    if w_router is None:
        n_experts = 0
        out_specs, out_shape = [seq(D), seq(D)], [tokens(D, F32), tokens(D, BF16)]
    else:
        n_experts = w_router.shape[-1]
        assert 2 * n_experts <= LANES and TOP_K == 2
        wr = jnp.zeros((D, LANES), F32).at[:, :n_experts].set(w_router)
        wr_hi = wr.astype(BF16)
        rows = R * ti
        earlier = (jnp.arange(rows)[:, None] < jnp.arange(rows)[None, :]).astype(BF16)
        in_specs += [full((D, 2 * LANES)), full((rows, rows))]
        args += [jnp.concatenate([wr_hi, (wr - wr_hi.astype(F32)).astype(BF16)], axis=1), earlier]
        per_block = lambda r, c: pl.BlockSpec((None, None, r, c), lambda b, s: (b, s, 0, 0))
        out_specs = [seq(D), seq(D // 2), seq(LANES), per_block(8, rows), per_block(n_experts, LANES)]
        out_shape = [tokens(D, F32), tokens(D // 2, U32), tokens(LANES, F32),
                     jax.ShapeDtypeStruct((B, SI // ti, 8, rows), jnp.int32),
                     jax.ShapeDtypeStruct((B, SI // ti, n_experts, LANES), F32)]
    return pl.pallas_call(
        functools.partial(_mix_out_kernel, d_pool=d_pool, natural_x=natural_x, n_experts=n_experts),
        grid=(B, SI // ti),
        in_specs=in_specs, out_specs=out_specs, out_shape=out_shape,
        compiler_params=_params("parallel", "parallel"),
        name="mix_out",
    )(*args)


def _swiglu_hidden(h, wg, wu):
    a = jnp.dot(h, wg, preferred_element_type=F32)
    b = jnp.dot(h, wu, preferred_element_type=F32)
    return (a * jax.nn.sigmoid(a) * b).astype(BF16)


def _finish(x, y, gfin_ref):
    out = x + y
    return out if gfin_ref is None else _rms(out, gfin_ref[...])


def _ffn_kernel(*refs, final, tf):
    if final:
        h_ref, x_ref, wg_ref, wu_ref, wd_ref, gfin_ref, o_ref, hid = refs
    else:
        (h_ref, x_ref, wg_ref, wu_ref, wd_ref, o_ref, hid), gfin_ref = refs, None
    h = h_ref[...]
    for c in range(wg_ref.shape[-1] // tf):
        cols = slice(c * tf, (c + 1) * tf)
        hid[:, cols] = _swiglu_hidden(h, wg_ref[:, cols], wu_ref[:, cols])
    y = jnp.dot(hid[...], wd_ref[...], preferred_element_type=F32)
    o_ref[...] = _finish(x_ref[...], y, gfin_ref)


def _pick_chunk(n, target):
    best = None
    for c in range(LANES, min(n, target) + 1, LANES):
        if n % c == 0:
            best = c
    assert best is not None
    return best


def _ffn(h, x, wg, wu, wd, g_final, *, tm=1024, tf_target=256):
    N, D = x.shape
    F = wg.shape[-1]
    tm = min(tm, N)
    tf = _pick_chunk(F, tf_target)
    final = g_final is not None
    row = pl.BlockSpec((tm, D), lambda i: (i, 0))
    resident = lambda shape: pl.BlockSpec(shape, lambda i: (0, 0), pipeline_mode=pl.Buffered(1))
    in_specs = [row, row, resident((D, F)), resident((D, F)), resident((F, D))]
    args = [h, x, wg, wu, wd]
    if final:
        in_specs.append(pl.BlockSpec((1, D), lambda i: (0, 0)))
        args.append(g_final.reshape(1, D))
    return pl.pallas_call(
        functools.partial(_ffn_kernel, final=final, tf=tf),
        grid=(N // tm,),
        in_specs=in_specs,
        out_specs=row,
        out_shape=jax.ShapeDtypeStruct((N, D), F32),
        scratch_shapes=[pltpu.VMEM((tm, F), BF16)],
        compiler_params=_params("parallel"),
        name="ffn_dense",
    )(*args)


def _plan_rows(block_plan, block_counts, token_shape, *, tm):
    B, R, SI = token_shape
    i32 = jnp.int32
    _, per_row, _, rows = block_plan.shape
    E = block_counts.shape[2]
    p_max = TOP_K * B * R * SI + E * tm
    counts = block_counts[..., 0].astype(i32).reshape(-1, E)
    block_start = jnp.cumsum(counts, axis=0) - counts
    seg_len = (jnp.sum(counts, axis=0) + tm - 1) // tm * tm
    seg_end = jnp.cumsum(seg_len)
    base = (seg_end - seg_len)[None, :] + block_start
    plan = block_plan.reshape(-1, 8, rows)
    experts = jnp.arange(E, dtype=i32)[None, None, :]

    def sorted_rows(expert, rank):
        row = jnp.sum(jnp.where(expert[:, :, None] == experts, base[:, None, :], 0), axis=-1) + rank
        return row.reshape(B, per_row, R, rows // R).transpose(0, 2, 1, 3).reshape(-1)

    tile_start = jnp.arange(p_max // tm, dtype=i32) * tm
    tile_expert = jnp.minimum(jnp.sum(seg_end[None, :] <= tile_start[:, None], axis=1), E - 1).astype(i32)
    return dict(rows=(sorted_rows(plan[:, 0], plan[:, 1]), sorted_rows(plan[:, 2], plan[:, 3])),
                tile_expert=tile_expert, n_active_tiles=(seg_end[-1] // tm).reshape(1).astype(i32), p_max=p_max)


def _sc_rows_kernel(n_rows, n_out, d, scatter):
    window_rows = SC_SCATTER_WINDOW if scatter else SC_GATHER_WINDOW
    per_worker = n_rows // SC_WORKERS
    assert n_rows % (SC_WORKERS * window_rows) == 0
    mesh = plsc.VectorSubcoreMesh(core_axis_name="c", subcore_axis_name="s")
    out_type = (jax.ShapeDtypeStruct((n_out, d), U32) if scatter
                else [jax.ShapeDtypeStruct((n_rows, d), U32)] * 2)

    def body(*refs):
        if scatter:
            src_hbm, hi_hbm, lo_hbm, out_hbm, idx_hi, idx_lo, rows, sem_hi, sem_lo = refs
        else:
            (src_hbm, hi_hbm, lo_hbm, out_hi_hbm, out_lo_hbm, idx_hi, idx_lo, rows_hi, rows_lo,
             sem_hi, sem_lo, sem_out_hi, sem_out_lo) = refs
        worker = lax.axis_index("s") * SC_CORES + lax.axis_index("c")

        @pl.loop(0, per_worker // window_rows)
        def _(i):
            window = pl.ds(worker * per_worker + i * window_rows, window_rows)
            pltpu.sync_copy(hi_hbm.at[window], idx_hi)
            pltpu.sync_copy(lo_hbm.at[window], idx_lo)
            if scatter:
                pltpu.sync_copy(src_hbm.at[window], rows)
                to_hi = pltpu.async_copy(rows, out_hbm.at[idx_hi], sem_hi)
                to_lo = pltpu.async_copy(rows, out_hbm.at[idx_lo], sem_lo)
                to_hi.wait()
                to_lo.wait()
            else:
                from_hi = pltpu.async_copy(src_hbm.at[idx_hi], rows_hi, sem_hi)
                from_lo = pltpu.async_copy(src_hbm.at[idx_lo], rows_lo, sem_lo)
                from_hi.wait()
                out_hi = pltpu.async_copy(rows_hi, out_hi_hbm.at[window], sem_out_hi)
                from_lo.wait()
                out_lo = pltpu.async_copy(rows_lo, out_lo_hbm.at[window], sem_out_lo)
                out_hi.wait()
                out_lo.wait()

    index_vec, row_buf = pltpu.VMEM((window_rows,), jnp.int32), pltpu.VMEM((window_rows, d), U32)
    dma_sem = pltpu.SemaphoreType.DMA
    scratch = ([index_vec, index_vec, row_buf, dma_sem, dma_sem] if scatter
               else [index_vec, index_vec, row_buf, row_buf, dma_sem, dma_sem, dma_sem, dma_sem])
    return pl.kernel(body, mesh=mesh, out_type=out_type, scratch_types=scratch)


def _moe_experts_kernel(te_ref, na_ref, xs_ref, wg_hbm, wu_hbm, wd_hbm, o_ref,
                        cache_g, cache_u, cache_d, stage_g, stage_u, stage_d, sems, acc, *, tf):
    i = pl.program_id(0)
    e = te_ref[i]
    n_chunks = cache_g.shape[-1] // tf
    active = i < na_ref[0]
    new_expert = jnp.logical_or(i == 0, e != te_ref[jnp.maximum(i - 1, 0)])

    def chunk_copies(c, slot):
        cols = pl.ds(c * tf, tf)
        return (pltpu.make_async_copy(wg_hbm.at[e, :, cols], stage_g.at[slot], sems.at[0, slot]),
                pltpu.make_async_copy(wu_hbm.at[e, :, cols], stage_u.at[slot], sems.at[1, slot]),
                pltpu.make_async_copy(wd_hbm.at[e, cols, :], stage_d.at[slot], sems.at[2, slot]))

    def tile_ffn(load_weights):
        x = _unpack_halves(xs_ref[...]).astype(BF16)
        if load_weights:
            for cp in chunk_copies(0, 0):
                cp.start()
        for c in range(n_chunks):
            cols = slice(c * tf, (c + 1) * tf)
            if load_weights:
                slot = c % 2
                if c + 1 < n_chunks:
                    for cp in chunk_copies(c + 1, 1 - slot):
                        cp.start()
                for cp in chunk_copies(c, slot):
                    cp.wait()
                cache_g[:, cols] = stage_g[slot].astype(BF16)
                cache_u[:, cols] = stage_u[slot].astype(BF16)
                cache_d[cols, :] = stage_d[slot].astype(BF16)
            hid = _swiglu_hidden(x, cache_g[:, cols], cache_u[:, cols])
            part = jnp.dot(hid, cache_d[cols, :], preferred_element_type=F32)
            if c == 0:
                acc[...] = part
            else:
                acc[...] += part
        o_ref[...] = _pack_halves(acc[...])

    pl.when(jnp.logical_and(active, new_expert))(lambda: tile_ffn(True))
    pl.when(jnp.logical_and(active, jnp.logical_not(new_expert)))(lambda: tile_ffn(False))


def _moe_experts(xs, plan, wg, wu, wd, *, tm, tf_target=512):
    P = xs.shape[0]
    E, D, F = wg.shape
    tf = _pick_chunk(F, tf_target)
    tile = lambda i, te, na: (jnp.minimum(i, na[0] - 1), 0)
    in_hbm = pl.BlockSpec(memory_space=pl.ANY)
    grid_spec = pltpu.PrefetchScalarGridSpec(
        num_scalar_prefetch=2,
        grid=(P // tm,),
        in_specs=[pl.BlockSpec((tm, D // 2), tile), in_hbm, in_hbm, in_hbm],
        out_specs=pl.BlockSpec((tm, D // 2), tile),
        scratch_shapes=[pltpu.VMEM((D, F), BF16), pltpu.VMEM((D, F), BF16), pltpu.VMEM((F, D), BF16),
                        pltpu.VMEM((2, D, tf), F32), pltpu.VMEM((2, D, tf), F32), pltpu.VMEM((2, tf, D), F32),
                        pltpu.SemaphoreType.DMA((3, 2)), pltpu.VMEM((tm, D), F32)],
    )
    return pl.pallas_call(
        functools.partial(_moe_experts_kernel, tf=tf),
        grid_spec=grid_spec,
        out_shape=jax.ShapeDtypeStruct((P, D // 2), U32),
        compiler_params=_params("arbitrary", vmem_limit=MOE_VMEM_LIMIT),
        name="moe_experts",
    )(plan["tile_expert"], plan["n_active_tiles"], xs, wg, wu, wd)


def _moe_mix_kernel(*refs, final, natural_out, carried):
    refs = list(refs)
    o_ref = refs.pop()
    if carried:
        refs.pop()
    x_ref, yh_ref, yl_ref, g_ref = refs[:4]
    gfin_ref = refs[4] if final else None
    gates = _token_rows(g_ref)
    lane = lax.broadcasted_iota(jnp.int32, gates.shape, 1)
    g_hi = jnp.sum(jnp.where(lane == 0, gates, 0.0), axis=-1, keepdims=True)
    g_lo = jnp.sum(jnp.where(lane == 1, gates, 0.0), axis=-1, keepdims=True)
    y = g_hi * _unpack_halves(_token_rows(yh_ref)) + g_lo * _unpack_halves(_token_rows(yl_ref))
    out = _finish(_token_rows(x_ref), y, gfin_ref).reshape(x_ref.shape)
    o_ref[...] = pltpu.einshape("rid->ird", out) if natural_out else out


def _moe_mix(x, y_hi, y_lo, pair_gates, g_final, *, natural_out, first_row, partial_out, ti=64):
    B, R, SI, D = x.shape
    n_rows = y_hi.shape[0] // (R * SI)
    ti = min(ti, SI)
    final = g_final is not None
    seq = lambda c: pl.BlockSpec((None, R, ti, c), lambda b, s: (first_row + b, 0, s, 0))
    local = lambda c: pl.BlockSpec((None, R, ti, c), lambda b, s: (b, 0, s, 0))
    as_tokens = lambda t: t.reshape(n_rows, R, SI, t.shape[-1])
    in_specs = [seq(D), local(D // 2), local(D // 2), seq(LANES)]
    args = [x, as_tokens(y_hi), as_tokens(y_lo), pair_gates]
    if final:
        in_specs.append(pl.BlockSpec((1, D), lambda b, s: (0, 0)))
        args.append(g_final.reshape(1, D))
    aliases = {}
    if partial_out is not None:
        aliases = {len(args): 0}
        in_specs.append(pl.BlockSpec(memory_space=pl.ANY))
        args.append(partial_out)
    out_spec = pl.BlockSpec((None, ti, R, D), lambda b, s: (first_row + b, s, 0, 0)) if natural_out else seq(D)
    return pl.pallas_call(
        functools.partial(_moe_mix_kernel, final=final, natural_out=natural_out, carried=partial_out is not None),
        grid=(n_rows, SI // ti), in_specs=in_specs, out_specs=out_spec,
        out_shape=jax.ShapeDtypeStruct((B, SI, R, D) if natural_out else (B, R, SI, D), F32),
        input_output_aliases=aliases,
        compiler_params=_params("parallel", "parallel"),
        name="moe_mix",
    )(*args)


def _moe(h, x, routing, wg, wu, wd, g_final, *, natural_out, tm=512):
    pair_gates, block_plan, block_counts = routing
    B, R, SI, half = h.shape
    N = B * R * SI
    plan = _plan_rows(block_plan, block_counts, (B, R, SI), tm=tm)
    rows_hi, rows_lo = plan["rows"]
    xs = _sc_rows_kernel(N, plan["p_max"], half, scatter=True)(h.reshape(N, half), rows_hi, rows_lo)
    ys = _moe_experts(xs, plan, wg, wu, wd, tm=tm)
    groups = MOE_RETURN_GROUPS if B % MOE_RETURN_GROUPS == 0 else 1
    n = N // groups
    gather = _sc_rows_kernel(n, plan["p_max"], half, scatter=False)
    fetched = [gather(ys, rows_hi[g * n:(g + 1) * n], rows_lo[g * n:(g + 1) * n]) for g in range(groups)]
    out = None
    for g, (y_hi, y_lo) in enumerate(fetched):
        out = _moe_mix(x, y_hi, y_lo, pair_gates, g_final, natural_out=natural_out,
                       first_row=g * (B // groups), partial_out=out)
    return out


def kernel(x, norm_mix, w_in, w_pool, pool_scale, attn_gain, w_out, norm_ffn, ffn_wg, ffn_wu, ffn_wd,
           w_router, moe_wg, moe_wu, moe_wd, final_norm):
    B, S, D = x.shape
    depth = norm_mix.shape[0]
    bf = lambda t: t.astype(BF16)
    w_in, w_pool, w_out = bf(w_in), bf(w_pool), bf(w_out)
    ffn_wg, ffn_wu, ffn_wd = bf(ffn_wg), bf(ffn_wu), bf(ffn_wd)
    R = RESIDUES
    assert S % R == 0
    N, SI = B * S, S // R
    x = x.reshape(B, SI, R, D)
    for l in range(depth):
        first, last = l == 0, l == depth - 1
        ya, q, k, v, q32, k32, v32 = _mix_in(x, norm_mix[l], w_in[l], w_pool[l], pool_scale[l], natural_x=first)
        narrow = lambda dil: (R // dil) * 16 > WINDOW_STEPS
        branches = [_attn_branch(q32, k32, v32, dil) if narrow(dil) else _attn_branch(q, k, v, dil)
                    for _, dil in DILATED_PATTERNS]
        i, routed = l // 2, l % 2 == 1
        x1, h2, *routing = _mix_out(x, ya, [o for o, _ in branches], [lse for _, lse in branches],
                                    attn_gain[l], w_out[l], norm_ffn[l], w_router[i] if routed else None,
                                    natural_x=first)
        g_final = final_norm if last else None
        if routed:
            x = _moe(h2, x1, routing, moe_wg[i], moe_wu[i], moe_wd[i], g_final, natural_out=last)
        else:
            x = _ffn(h2.reshape(N, D), x1.reshape(N, D), ffn_wg[i], ffn_wu[i], ffn_wd[i], g_final)
            x = x.reshape(B, R, SI, D)
            if last:
                x = x.transpose(0, 2, 1, 3)
    return x.reshape(B, S, D)
```

```python
import functools

import jax
import jax.numpy as jnp
from jax import lax
from jax.experimental import pallas as pl
from jax.experimental.pallas import tpu as pltpu
from jax.experimental.pallas import tpu_sc as plsc

F32 = jnp.float32
BF16 = jnp.bfloat16
U32 = jnp.uint32

EPS = 1e-6
LANES = 128
HEAD_DIM = 64
POOL_WINDOWS = (2, 4, 8, 16)
POOL_HIST = 8
DILATED_PATTERNS = ((128, 1), (512, 4), (2048, 16))
WINDOW_STEPS = 128
RESIDUES = 16
ATTN_STEP_ROWS = 1024
ATTN_UNROLL_QUERIES = 1024
LOG2_E = 1.4426950408889634
SC_CORES, SC_SUBCORES = 2, 16
SC_WORKERS = SC_CORES * SC_SUBCORES
SC_SCATTER_WINDOW = 128
SC_GATHER_WINDOW = 64
MOE_RETURN_GROUPS = 2
assert all(w // d == WINDOW_STEPS and RESIDUES % d == 0 for w, d in DILATED_PATTERNS)
TOP_K = 2
MASKED = -1e30
VMEM_LIMIT = 48 * 1024 * 1024
MOE_VMEM_LIMIT = 56 * 1024 * 1024


def _rms(x, g):
    return x * lax.rsqrt(jnp.mean(x * x, axis=-1, keepdims=True) + EPS) * g


def _params(*sem, vmem_limit=VMEM_LIMIT):
    return pltpu.CompilerParams(dimension_semantics=sem, vmem_limit_bytes=vmem_limit)


def _token_rows(ref, natural=False):
    t = ref[...]
    if natural:
        t = pltpu.einshape("ird->rid", t)
    return t.reshape(-1, t.shape[-1])


def _pack_halves(t):
    half = t.shape[-1] // 2
    return pltpu.pack_elementwise([t[:, :half], t[:, half:]], packed_dtype=BF16)


def _unpack_halves(words):
    halves = [pltpu.unpack_elementwise(words, index=k, packed_dtype=BF16, unpacked_dtype=F32) for k in range(2)]
    return jnp.concatenate(halves, axis=-1)


def _mix_in_kernel(x_ref, g_ref, w_ref, wp_ref, ps_ref, ya_ref, q_ref, k_ref, v_ref, q32_ref, k32_ref, v32_ref,
                   ubuf, uprev, *, d_pool, d_attn, ti, natural_x):
    s = pl.program_id(1)
    R = RESIDUES
    blk = lambda t: t.reshape(R, ti, t.shape[-1])
    h = _rms(_token_rows(x_ref, natural_x), g_ref[...]).astype(BF16)
    u = jnp.dot(h, w_ref[:, :d_pool], preferred_element_type=F32)
    scale = HEAD_DIM ** -0.5 * LOG2_E
    for n, (ref, ref32) in enumerate(zip((q_ref, k_ref, v_ref), (q32_ref, k32_ref, v32_ref))):
        t = jnp.dot(h, w_ref[:, d_pool + n * d_attn:d_pool + (n + 1) * d_attn], preferred_element_type=F32)
        t = t * scale if n == 0 else t
        ref[...] = blk(t.astype(BF16))
        ref32[...] = blk(t)

    history = ubuf[:, POOL_HIST - 1:POOL_HIST, :]
    ubuf[:, POOL_HIST - 1:POOL_HIST, :] = jnp.where(s == 0, 0.0, history)
    ubuf[:, POOL_HIST:POOL_HIST + ti, :] = blk(u)
    uprev[...] = ubuf[:, POOL_HIST - 1:POOL_HIST - 1 + ti, :]
    at_start = (s * ti + lax.broadcasted_iota(jnp.int32, (ti, 1), 0)) == 0
    group = d_pool // len(POOL_WINDOWS)
    zs = []
    for gi, w in enumerate(POOL_WINDOWS):
        cols = slice(gi * group, (gi + 1) * group)
        ds = []
        for r in range(R):
            ug = ubuf[r, POOL_HIST:POOL_HIST + ti, cols]
            win = ug
            for back in range(1, w):
                rr = r - back
                win = win + (ubuf[rr, POOL_HIST:POOL_HIST + ti, cols] if rr >= 0 else uprev[rr + R, :, cols])
            cnt = jnp.where(at_start, float(min(r + 1, w)), float(w))
            ds.append(win / cnt - ug)
        d = jnp.concatenate(ds, axis=0).astype(BF16)
        zs.append(jnp.dot(d, wp_ref[gi], preferred_element_type=F32))
    z = jnp.concatenate(zs, axis=-1)
    ya_ref[...] = blk(_rms(z, ps_ref[...]).astype(BF16))
    ubuf[:, POOL_HIST - 1:POOL_HIST, :] = ubuf[:, POOL_HIST + ti - 1:POOL_HIST + ti, :]


def _mix_in(x, g, w_in, w_pool, pool_scale, *, natural_x, ti=64):
    B, R, SI, D = x.shape
    if natural_x:
        R, SI = SI, R
    d_pool = pool_scale.shape[-1]
    d_in = w_in.shape[-1]
    d_attn = (d_in - d_pool) // 3
    ti = min(ti, SI)
    assert R == RESIDUES >= max(POOL_WINDOWS) and SI % ti == 0 and ti % 16 == 0
    assert d_pool % (LANES * len(POOL_WINDOWS)) == 0
    seq_spec = lambda c: pl.BlockSpec((None, R, ti, c), lambda b, s: (b, 0, s, 0))
    full = lambda shape: pl.BlockSpec(shape, lambda b, s: (0,) * len(shape))
    out_sds = lambda c, dtype=BF16: jax.ShapeDtypeStruct((B, R, SI, c), dtype)
    return pl.pallas_call(
        functools.partial(_mix_in_kernel, d_pool=d_pool, d_attn=d_attn, ti=ti, natural_x=natural_x),
        grid=(B, SI // ti),
        in_specs=[pl.BlockSpec((None, ti, R, D), lambda b, s: (b, s, 0, 0)) if natural_x else seq_spec(D),
                  full((1, D)), full((D, d_in)), full(w_pool.shape), full((1, d_pool))],
        out_specs=[seq_spec(d_pool)] + [seq_spec(d_attn)] * 6,
        out_shape=[out_sds(d_pool)] + [out_sds(d_attn)] * 3 + [out_sds(d_attn, F32)] * 3,
        scratch_shapes=[pltpu.VMEM((R, POOL_HIST + ti, d_pool), F32), pltpu.VMEM((R, ti, d_pool), F32)],
        compiler_params=_params("parallel", "arbitrary"),
        name="mix_in",
    )(x, g.reshape(1, D), w_in, w_pool, pool_scale.reshape(1, d_pool))


def _attn_kernel(q_ref, k_ref, v_ref, o_ref, stat_ref, bias_ref, *, nq, strips, d_attn):
    _, classes, chunk, _ = q_ref.shape
    L = nq // strips
    nk, n_blocks = 2 * nq, chunk // L
    first_block = pl.program_id(2) * n_blocks
    row = lax.broadcasted_iota(jnp.int32, (nq, 1), 0)
    col = lax.broadcasted_iota(jnp.int32, (1, nk), 1)
    q_strip, q_row = row >> (L.bit_length() - 1), row & (L - 1)
    k_strip, k_row = col >> ((2 * L).bit_length() - 1), col & (2 * L - 1)
    back = strips * (q_row - k_row) + (q_strip - k_strip)

    def band(offset):
        rel = back + strips * offset
        return jnp.where((rel >= 0) & (rel <= WINDOW_STEPS), 0.0, MASKED).astype(F32)

    bias_ref[0] = band(0)
    bias_ref[1] = band(L)
    lane = lax.broadcasted_iota(jnp.int32, (nq, LANES), 1)
    lo_half = lane < HEAD_DIM
    n_heads = d_attn // HEAD_DIM

    def block(it, carry):
        g, j_local = it // n_blocks, it % n_blocks
        j = first_block + j_local
        q_rows = pl.ds(pl.multiple_of(j_local * L, L), L)
        k_rows = pl.ds(pl.multiple_of(jnp.maximum(j - 1, 0) * L, L), 2 * L)
        bias = bias_ref[jnp.minimum(j, 1)]
        stats = jnp.zeros((nq, LANES), F32)
        for hp in range(d_attn // LANES):
            cols = slice(hp * LANES, (hp + 1) * LANES)
            qp = q_ref[:, g, q_rows, cols].reshape(nq, LANES).astype(BF16)
            kk = k_ref[:, g, k_rows, cols].reshape(nk, LANES).astype(BF16)
            vv = v_ref[:, g, k_rows, cols].reshape(nk, LANES).astype(BF16)
            outs, head_stats = [], []
            for sub in range(2):
                keep = lo_half if sub == 0 else jnp.logical_not(lo_half)
                qm = jnp.where(keep, qp, jnp.zeros_like(qp))
                s = lax.dot_general(qm, kk, (((1,), (1,)), ((), ())), preferred_element_type=F32) + bias
                m = jnp.max(s, axis=-1, keepdims=True)
                p = jnp.exp2(s - m)
                l = jnp.sum(p, axis=-1, keepdims=True)
                outs.append(jnp.dot(p.astype(BF16), vv, preferred_element_type=F32))
                head_stats.append(jnp.where(lane < n_heads, m, l))
            o_pair = jnp.where(lo_half, outs[0], outs[1]).astype(o_ref.dtype)
            o_ref[:, g, q_rows, cols] = o_pair.reshape(strips, L, LANES)
            pair_stats = jnp.where((lane & 1) == 0, head_stats[0], head_stats[1])
            stats = jnp.where(((lane & (n_heads - 1)) >> 1) == hp, pair_stats, stats)
        stat_ref[:, g, q_rows, :] = stats.reshape(strips, L, LANES)
        return carry

    lax.fori_loop(0, classes * n_blocks, block, 0, unroll=ATTN_UNROLL_QUERIES // nq)


def _attn_branch(q, k, v, dil):
    B, R, SI, C = q.shape
    strips = R // dil
    row_tile = 8 * 4 // q.dtype.itemsize
    nq = max(WINDOW_STEPS, strips * row_tile)
    L = nq // strips
    classes = max(1, min(dil, ATTN_STEP_ROWS // (strips * SI)))
    chunk = min(SI, max(L, ATTN_STEP_ROWS // (strips * classes)))
    assert R % dil == 0 and SI % chunk == 0 and chunk % L == 0 and SI >= 2 * L and dil % classes == 0
    n_heads = C // HEAD_DIM
    assert C % LANES == 0 and 2 * n_heads <= LANES and n_heads & (n_heads - 1) == 0
    assert nq & (nq - 1) == 0 and L & (L - 1) == 0
    view = lambda t: t.reshape(B, strips, dil, SI, t.shape[-1])
    q_spec = lambda c: pl.BlockSpec((None, strips, classes, chunk, c), lambda b, g, s: (b, 0, g, s, 0))
    kv_spec = pl.BlockSpec((None, strips, classes, SI, C), lambda b, g, s: (b, 0, g, 0, 0),
                           pipeline_mode=pl.Buffered(1 if SI > chunk else 2))
    o, stats = pl.pallas_call(
        functools.partial(_attn_kernel, nq=nq, strips=strips, d_attn=C),
        scratch_shapes=[pltpu.VMEM((2, nq, 2 * nq), F32)],
        grid=(B, dil // classes, SI // chunk),
        in_specs=[q_spec(C), kv_spec, kv_spec],
        out_specs=[q_spec(C), q_spec(LANES)],
        out_shape=[jax.ShapeDtypeStruct((B, strips, dil, SI, C), q.dtype),
                   jax.ShapeDtypeStruct((B, strips, dil, SI, LANES), F32)],
        compiler_params=_params("parallel", "parallel", "arbitrary"),
        name=f"attn_d{dil}",
    )(view(q), view(k), view(v))
    return o.reshape(B, R, SI, C), stats.reshape(B, R, SI, LANES)


def _route(h, wr, earlier, n_experts):
    h_hi = h.astype(BF16)
    h_lo = (h - h_hi.astype(F32)).astype(BF16)
    both = jnp.dot(h_hi, wr, preferred_element_type=F32)
    logits = both[:, :LANES] + both[:, LANES:] + jnp.dot(h_lo, wr[:, :LANES], preferred_element_type=F32)
    rows = h.shape[0]
    scores = logits.T[:n_experts]
    expert = lax.broadcasted_iota(jnp.int32, scores.shape, 0)
    picks = []
    for _ in range(TOP_K):
        m = jnp.max(scores, axis=0, keepdims=True)
        idx = jnp.min(jnp.where(scores == m, expert, n_experts), axis=0, keepdims=True)
        picks.append((m, idx))
        scores = jnp.where(expert == idx, -jnp.inf, scores)
    (m1, i1), (m2, i2) = picks
    e2 = jnp.exp(m2 - m1)
    g1 = 1.0 / (1.0 + e2)
    g2 = e2 / (1.0 + e2)
    picked = jnp.where(jnp.logical_or(expert == i1, expert == i2), 1.0, 0.0)
    rank = jnp.dot(picked.astype(BF16), earlier, preferred_element_type=F32)
    first_is_hi = i1 > i2
    e_hi, e_lo = jnp.maximum(i1, i2), jnp.minimum(i1, i2)
    rank_of = lambda e: jnp.sum(jnp.where(expert == e, rank, 0.0), axis=0, keepdims=True).astype(jnp.int32)
    plan = jnp.concatenate([e_hi, rank_of(e_hi), e_lo, rank_of(e_lo), jnp.zeros((4, rows), jnp.int32)], axis=0)
    gate_rows = jnp.concatenate([jnp.where(first_is_hi, g1, g2), jnp.where(first_is_hi, g2, g1),
                                 jnp.zeros((LANES - 2, rows), F32)], axis=0)
    return gate_rows.T, plan, jnp.sum(picked, axis=1, keepdims=True)


def _mix_out_kernel(*refs, d_pool, natural_x, n_experts):
    if n_experts:
        (x_ref, ya_ref, o1_ref, o2_ref, o3_ref, l1_ref, l2_ref, l3_ref, gain_ref, wo_ref, gffn_ref, exp_ref,
         wr_ref, earlier_ref, x1_ref, h2_ref, gates_ref, plan_ref, counts_ref) = refs
    else:
        (x_ref, ya_ref, o1_ref, o2_ref, o3_ref, l1_ref, l2_ref, l3_ref, gain_ref, wo_ref, gffn_ref, exp_ref,
         x1_ref, h2_ref) = refs
    n_heads = o1_ref.shape[-1] // HEAD_DIM
    stats = [_token_rows(l) for l in (l1_ref, l2_ref, l3_ref)]
    top = jnp.maximum(jnp.maximum(stats[0], stats[1]), stats[2])
    es = [jnp.exp2(t - top) for t in stats]
    sums = [pltpu.roll(t, LANES - n_heads, axis=1) for t in stats]
    den = es[0] * sums[0] + es[1] * sums[1] + es[2] * sums[2]
    head_lane = lax.broadcasted_iota(jnp.int32, den.shape, 1) < n_heads
    expand = exp_ref[...]

    def per_lane(w):
        hi = w.astype(BF16)
        lo = (w - hi.astype(F32)).astype(BF16)
        return jnp.dot(jnp.concatenate([hi, lo], axis=1), expand, preferred_element_type=F32)

    o = 0.0
    for e, o_ref in zip(es, (o1_ref, o2_ref, o3_ref)):
        o = o + per_lane(jnp.where(head_lane, e / den, 0.0)) * _token_rows(o_ref).astype(F32)
    yb = _rms(o, gain_ref[...]).astype(BF16)
    y = jnp.dot(jnp.concatenate([_token_rows(ya_ref), yb], axis=1), wo_ref[...], preferred_element_type=F32)
    x1 = _token_rows(x_ref, natural_x) + y
    x1_ref[...] = x1.reshape(x1_ref.shape)
    h2 = _rms(x1, gffn_ref[...])
    if n_experts:
        gate_tile, plan, counts = _route(h2, wr_ref[...], earlier_ref[...], n_experts)
        gates_ref[...] = gate_tile.reshape(gates_ref.shape)
        plan_ref[...] = plan
        counts_ref[...] = jnp.broadcast_to(counts, counts_ref.shape)
        h2_ref[...] = _pack_halves(h2).reshape(h2_ref.shape)
    else:
        h2_ref[...] = h2.astype(h2_ref.dtype).reshape(h2_ref.shape)


def _mix_out(x, ya, outs, lses, attn_gain, w_out, g_ffn, w_router, *, natural_x, ti=64):
    B, R, SI, d_pool = ya.shape
    D = x.shape[-1]
    d_attn = outs[0].shape[-1]
    ti = min(ti, SI)
    head_of_lane = jnp.arange(d_attn) // HEAD_DIM
    expand = (jnp.arange(2 * LANES)[:, None] % LANES == head_of_lane[None, :]).astype(BF16)
    seq = lambda c: pl.BlockSpec((None, R, ti, c), lambda b, s: (b, 0, s, 0))
    x_spec = pl.BlockSpec((None, ti, R, D), lambda b, s: (b, s, 0, 0)) if natural_x else seq(D)
    full = lambda shape: pl.BlockSpec(shape, lambda b, s: (0,) * len(shape))
    tokens = lambda c, dtype: jax.ShapeDtypeStruct((B, R, SI, c), dtype)
    in_specs = ([x_spec, seq(d_pool)] + [seq(d_attn)] * 3 + [seq(LANES)] * 3
                + [full((1, d_attn)), full(w_out.shape), full((1, D)), full(expand.shape)])
    args = [x, ya, *outs, *lses, attn_gain.reshape(1, d_attn), w_out, g_ffn.reshape(1, D), expand]
    if w_router is None:
        n_experts = 0
        out_specs, out_shape = [seq(D), seq(D)], [tokens(D, F32), tokens(D, BF16)]
    else:
        n_experts = w_router.shape[-1]
        assert 2 * n_experts <= LANES and TOP_K == 2
        wr = jnp.zeros((D, LANES), F32).at[:, :n_experts].set(w_router)
        wr_hi = wr.astype(BF16)
        rows = R * ti
        earlier = (jnp.arange(rows)[:, None] < jnp.arange(rows)[None, :]).astype(BF16)
        in_specs += [full((D, 2 * LANES)), full((rows, rows))]
        args += [jnp.concatenate([wr_hi, (wr - wr_hi.astype(F32)).astype(BF16)], axis=1), earlier]
        per_block = lambda r, c: pl.BlockSpec((None, None, r, c), lambda b, s: (b, s, 0, 0))
        out_specs = [seq(D), seq(D // 2), seq(LANES), per_block(8, rows), per_block(n_experts, LANES)]
        out_shape = [tokens(D, F32), tokens(D // 2, U32), tokens(LANES, F32),
                     jax.ShapeDtypeStruct((B, SI // ti, 8, rows), jnp.int32),
                     jax.ShapeDtypeStruct((B, SI // ti, n_experts, LANES), F32)]
    return pl.pallas_call(
        functools.partial(_mix_out_kernel, d_pool=d_pool, natural_x=natural_x, n_experts=n_experts),
        grid=(B, SI // ti),
        in_specs=in_specs, out_specs=out_specs, out_shape=out_shape,
        compiler_params=_params("parallel", "parallel"),
        name="mix_out",
    )(*args)


def _swiglu_hidden(h, wg, wu):
    a = jnp.dot(h, wg, preferred_element_type=F32)
    b = jnp.dot(h, wu, preferred_element_type=F32)
    return (a * jax.nn.sigmoid(a) * b).astype(BF16)


def _finish(x, y, gfin_ref):
    out = x + y
    return out if gfin_ref is None else _rms(out, gfin_ref[...])


def _ffn_kernel(*refs, final, tf):
    if final:
        h_ref, x_ref, wg_ref, wu_ref, wd_ref, gfin_ref, o_ref, hid = refs
    else:
        (h_ref, x_ref, wg_ref, wu_ref, wd_ref, o_ref, hid), gfin_ref = refs, None
    h = h_ref[...]
    for c in range(wg_ref.shape[-1] // tf):
        cols = slice(c * tf, (c + 1) * tf)
        hid[:, cols] = _swiglu_hidden(h, wg_ref[:, cols], wu_ref[:, cols])
    y = jnp.dot(hid[...], wd_ref[...], preferred_element_type=F32)
    o_ref[...] = _finish(x_ref[...], y, gfin_ref)


def _pick_chunk(n, target):
    best = None
    for c in range(LANES, min(n, target) + 1, LANES):
        if n % c == 0:
            best = c
    assert best is not None
    return best


def _ffn(h, x, wg, wu, wd, g_final, *, tm=1024, tf_target=256):
    N, D = x.shape
    F = wg.shape[-1]
    tm = min(tm, N)
    tf = _pick_chunk(F, tf_target)
    final = g_final is not None
    row = pl.BlockSpec((tm, D), lambda i: (i, 0))
    resident = lambda shape: pl.BlockSpec(shape, lambda i: (0, 0), pipeline_mode=pl.Buffered(1))
    in_specs = [row, row, resident((D, F)), resident((D, F)), resident((F, D))]
    args = [h, x, wg, wu, wd]
    if final:
        in_specs.append(pl.BlockSpec((1, D), lambda i: (0, 0)))
        args.append(g_final.reshape(1, D))
    return pl.pallas_call(
        functools.partial(_ffn_kernel, final=final, tf=tf),
        grid=(N // tm,),
        in_specs=in_specs,
        out_specs=row,
        out_shape=jax.ShapeDtypeStruct((N, D), F32),
        scratch_shapes=[pltpu.VMEM((tm, F), BF16)],
        compiler_params=_params("parallel"),
        name="ffn_dense",
    )(*args)


def _plan_rows(block_plan, block_counts, token_shape, *, tm):
    B, R, SI = token_shape
    i32 = jnp.int32
    _, per_row, _, rows = block_plan.shape
    E = block_counts.shape[2]
    p_max = -(-(TOP_K * B * R * SI + E * (tm - 1)) // tm) * tm
    counts = block_counts[..., 0].astype(i32).reshape(-1, E)
    block_start = jnp.cumsum(counts, axis=0) - counts
    seg_len = (jnp.sum(counts, axis=0) + tm - 1) // tm * tm
    seg_end = jnp.cumsum(seg_len)
    base = (seg_end - seg_len)[None, :] + block_start
    plan = block_plan.reshape(-1, 8, rows)
    experts = jnp.arange(E, dtype=i32)[None, None, :]

    def sorted_rows(expert, rank):
        row = jnp.sum(jnp.where(expert[:, :, None] == experts, base[:, None, :], 0), axis=-1) + rank
        return row.reshape(B, per_row, R, rows // R).transpose(0, 2, 1, 3).reshape(-1)

    tile_start = jnp.arange(p_max // tm, dtype=i32) * tm
    tile_expert = jnp.minimum(jnp.sum(seg_end[None, :] <= tile_start[:, None], axis=1), E - 1).astype(i32)
    return dict(rows=(sorted_rows(plan[:, 0], plan[:, 1]), sorted_rows(plan[:, 2], plan[:, 3])),
                tile_expert=tile_expert, n_active_tiles=(seg_end[-1] // tm).reshape(1).astype(i32), p_max=p_max)


def _sc_rows_kernel(n_rows, n_out, d, scatter):
    window_rows = SC_SCATTER_WINDOW if scatter else SC_GATHER_WINDOW
    per_worker = n_rows // SC_WORKERS
    assert n_rows % (SC_WORKERS * window_rows) == 0
    mesh = plsc.VectorSubcoreMesh(core_axis_name="c", subcore_axis_name="s")
    out_type = (jax.ShapeDtypeStruct((n_out, d), U32) if scatter
                else [jax.ShapeDtypeStruct((n_rows, d), U32)] * 2)

    def body(*refs):
        if scatter:
            src_hbm, hi_hbm, lo_hbm, out_hbm, idx_hi, idx_lo, rows, sem_hi, sem_lo = refs
        else:
            (src_hbm, hi_hbm, lo_hbm, out_hi_hbm, out_lo_hbm, idx_hi, idx_lo, rows_hi, rows_lo,
             sem_hi, sem_lo, sem_out_hi, sem_out_lo) = refs
        worker = lax.axis_index("s") * SC_CORES + lax.axis_index("c")

        @pl.loop(0, per_worker // window_rows)
        def _(i):
            window = pl.ds(worker * per_worker + i * window_rows, window_rows)
            pltpu.sync_copy(hi_hbm.at[window], idx_hi)
            pltpu.sync_copy(lo_hbm.at[window], idx_lo)
            if scatter:
                pltpu.sync_copy(src_hbm.at[window], rows)
                to_hi = pltpu.async_copy(rows, out_hbm.at[idx_hi], sem_hi)
                to_lo = pltpu.async_copy(rows, out_hbm.at[idx_lo], sem_lo)
                to_hi.wait()
                to_lo.wait()
            else:
                from_hi = pltpu.async_copy(src_hbm.at[idx_hi], rows_hi, sem_hi)
                from_lo = pltpu.async_copy(src_hbm.at[idx_lo], rows_lo, sem_lo)
                from_hi.wait()
                out_hi = pltpu.async_copy(rows_hi, out_hi_hbm.at[window], sem_out_hi)
                from_lo.wait()
                out_lo = pltpu.async_copy(rows_lo, out_lo_hbm.at[window], sem_out_lo)
                out_hi.wait()
                out_lo.wait()

    index_vec, row_buf = pltpu.VMEM((window_rows,), jnp.int32), pltpu.VMEM((window_rows, d), U32)
    dma_sem = pltpu.SemaphoreType.DMA
    scratch = ([index_vec, index_vec, row_buf, dma_sem, dma_sem] if scatter
               else [index_vec, index_vec, row_buf, row_buf, dma_sem, dma_sem, dma_sem, dma_sem])
    return pl.kernel(body, mesh=mesh, out_type=out_type, scratch_types=scratch)


def _moe_experts_kernel(te_ref, na_ref, xs_ref, wg_hbm, wu_hbm, wd_hbm, o_ref,
                        cache_g, cache_u, cache_d, stage_g, stage_u, stage_d, sems, acc, *, tf):
    i = pl.program_id(0)
    e = te_ref[i]
    n_chunks = cache_g.shape[-1] // tf
    active = i < na_ref[0]
    new_expert = jnp.logical_or(i == 0, e != te_ref[jnp.maximum(i - 1, 0)])

    def chunk_copies(c, slot):
        cols = pl.ds(c * tf, tf)
        return (pltpu.make_async_copy(wg_hbm.at[e, :, cols], stage_g.at[slot], sems.at[0, slot]),
                pltpu.make_async_copy(wu_hbm.at[e, :, cols], stage_u.at[slot], sems.at[1, slot]),
                pltpu.make_async_copy(wd_hbm.at[e, cols, :], stage_d.at[slot], sems.at[2, slot]))

    def tile_ffn(load_weights):
        x = _unpack_halves(xs_ref[...]).astype(BF16)
        if load_weights:
            for cp in chunk_copies(0, 0):
                cp.start()
        for c in range(n_chunks):
            cols = slice(c * tf, (c + 1) * tf)
            if load_weights:
                slot = c % 2
                if c + 1 < n_chunks:
                    for cp in chunk_copies(c + 1, 1 - slot):
                        cp.start()
                for cp in chunk_copies(c, slot):
                    cp.wait()
                cache_g[:, cols] = stage_g[slot].astype(BF16)
                cache_u[:, cols] = stage_u[slot].astype(BF16)
                cache_d[cols, :] = stage_d[slot].astype(BF16)
            hid = _swiglu_hidden(x, cache_g[:, cols], cache_u[:, cols])
            part = jnp.dot(hid, cache_d[cols, :], preferred_element_type=F32)
            if c == 0:
                acc[...] = part
            else:
                acc[...] += part
        o_ref[...] = _pack_halves(acc[...])

    pl.when(jnp.logical_and(active, new_expert))(lambda: tile_ffn(True))
    pl.when(jnp.logical_and(active, jnp.logical_not(new_expert)))(lambda: tile_ffn(False))


def _moe_experts(xs, plan, wg, wu, wd, *, tm, tf_target=512):
    P = xs.shape[0]
    E, D, F = wg.shape
    tf = _pick_chunk(F, tf_target)
    tile = lambda i, te, na: (jnp.minimum(i, na[0] - 1), 0)
    in_hbm = pl.BlockSpec(memory_space=pl.ANY)
    grid_spec = pltpu.PrefetchScalarGridSpec(
        num_scalar_prefetch=2,
        grid=(P // tm,),
        in_specs=[pl.BlockSpec((tm, D // 2), tile), in_hbm, in_hbm, in_hbm],
        out_specs=pl.BlockSpec((tm, D // 2), tile),
        scratch_shapes=[pltpu.VMEM((D, F), BF16), pltpu.VMEM((D, F), BF16), pltpu.VMEM((F, D), BF16),
                        pltpu.VMEM((2, D, tf), F32), pltpu.VMEM((2, D, tf), F32), pltpu.VMEM((2, tf, D), F32),
                        pltpu.SemaphoreType.DMA((3, 2)), pltpu.VMEM((tm, D), F32)],
    )
    return pl.pallas_call(
        functools.partial(_moe_experts_kernel, tf=tf),
        grid_spec=grid_spec,
        out_shape=jax.ShapeDtypeStruct((P, D // 2), U32),
        compiler_params=_params("arbitrary", vmem_limit=MOE_VMEM_LIMIT),
        name="moe_experts",
    )(plan["tile_expert"], plan["n_active_tiles"], xs, wg, wu, wd)


def _moe_mix_kernel(*refs, final, natural_out, carried):
    refs = list(refs)
    o_ref = refs.pop()
    if carried:
        refs.pop()
    x_ref, yh_ref, yl_ref, g_ref = refs[:4]
    gfin_ref = refs[4] if final else None
    gates = _token_rows(g_ref)
    lane = lax.broadcasted_iota(jnp.int32, gates.shape, 1)
    g_hi = jnp.sum(jnp.where(lane == 0, gates, 0.0), axis=-1, keepdims=True)
    g_lo = jnp.sum(jnp.where(lane == 1, gates, 0.0), axis=-1, keepdims=True)
    y = g_hi * _unpack_halves(_token_rows(yh_ref)) + g_lo * _unpack_halves(_token_rows(yl_ref))
    out = _finish(_token_rows(x_ref), y, gfin_ref).reshape(x_ref.shape)
    o_ref[...] = pltpu.einshape("rid->ird", out) if natural_out else out


def _moe_mix(x, y_hi, y_lo, pair_gates, g_final, *, natural_out, first_row, partial_out, ti=64):
    B, R, SI, D = x.shape
    n_rows = y_hi.shape[0] // (R * SI)
    ti = min(ti, SI)
    final = g_final is not None
    seq = lambda c: pl.BlockSpec((None, R, ti, c), lambda b, s: (first_row + b, 0, s, 0))
    local = lambda c: pl.BlockSpec((None, R, ti, c), lambda b, s: (b, 0, s, 0))
    as_tokens = lambda t: t.reshape(n_rows, R, SI, t.shape[-1])
    in_specs = [seq(D), local(D // 2), local(D // 2), seq(LANES)]
    args = [x, as_tokens(y_hi), as_tokens(y_lo), pair_gates]
    if final:
        in_specs.append(pl.BlockSpec((1, D), lambda b, s: (0, 0)))
        args.append(g_final.reshape(1, D))
    aliases = {}
    if partial_out is not None:
        aliases = {len(args): 0}
        in_specs.append(pl.BlockSpec(memory_space=pl.ANY))
        args.append(partial_out)
    out_spec = pl.BlockSpec((None, ti, R, D), lambda b, s: (first_row + b, s, 0, 0)) if natural_out else seq(D)
    return pl.pallas_call(
        functools.partial(_moe_mix_kernel, final=final, natural_out=natural_out, carried=partial_out is not None),
        grid=(n_rows, SI // ti), in_specs=in_specs, out_specs=out_spec,
        out_shape=jax.ShapeDtypeStruct((B, SI, R, D) if natural_out else (B, R, SI, D), F32),
        input_output_aliases=aliases,
        compiler_params=_params("parallel", "parallel"),
        name="moe_mix",
    )(*args)


def _moe(h, x, routing, wg, wu, wd, g_final, *, natural_out, tm=384):
    pair_gates, block_plan, block_counts = routing
    B, R, SI, half = h.shape
    N = B * R * SI
    plan = _plan_rows(block_plan, block_counts, (B, R, SI), tm=tm)
    rows_hi, rows_lo = plan["rows"]
    xs = _sc_rows_kernel(N, plan["p_max"], half, scatter=True)(h.reshape(N, half), rows_hi, rows_lo)
    ys = _moe_experts(xs, plan, wg, wu, wd, tm=tm)
    groups = MOE_RETURN_GROUPS if B % MOE_RETURN_GROUPS == 0 else 1
    n = N // groups
    gather = _sc_rows_kernel(n, plan["p_max"], half, scatter=False)
    fetched = [gather(ys, rows_hi[g * n:(g + 1) * n], rows_lo[g * n:(g + 1) * n]) for g in range(groups)]
    out = None
    for g, (y_hi, y_lo) in enumerate(fetched):
        out = _moe_mix(x, y_hi, y_lo, pair_gates, g_final, natural_out=natural_out,
                       first_row=g * (B // groups), partial_out=out)
    return out


def kernel(x, norm_mix, w_in, w_pool, pool_scale, attn_gain, w_out, norm_ffn, ffn_wg, ffn_wu, ffn_wd,
           w_router, moe_wg, moe_wu, moe_wd, final_norm):
    B, S, D = x.shape
    depth = norm_mix.shape[0]
    bf = lambda t: t.astype(BF16)
    w_in, w_pool, w_out = bf(w_in), bf(w_pool), bf(w_out)
    ffn_wg, ffn_wu, ffn_wd = bf(ffn_wg), bf(ffn_wu), bf(ffn_wd)
    R = RESIDUES
    assert S % R == 0
    N, SI = B * S, S // R
    x = x.reshape(B, SI, R, D)
    for l in range(depth):
        first, last = l == 0, l == depth - 1
        ya, q, k, v, q32, k32, v32 = _mix_in(x, norm_mix[l], w_in[l], w_pool[l], pool_scale[l], natural_x=first)
        narrow = lambda dil: (R // dil) * 16 > WINDOW_STEPS
        branches = [_attn_branch(q32, k32, v32, dil) if narrow(dil) else _attn_branch(q, k, v, dil)
                    for _, dil in DILATED_PATTERNS]
        i, routed = l // 2, l % 2 == 1
        x1, h2, *routing = _mix_out(x, ya, [o for o, _ in branches], [lse for _, lse in branches],
                                    attn_gain[l], w_out[l], norm_ffn[l], w_router[i] if routed else None,
                                    natural_x=first)
        g_final = final_norm if last else None
        if routed:
            x = _moe(h2, x1, routing, moe_wg[i], moe_wu[i], moe_wd[i], g_final, natural_out=last)
        else:
            x = _ffn(h2.reshape(N, D), x1.reshape(N, D), ffn_wg[i], ffn_wu[i], ffn_wd[i], g_final)
            x = x.reshape(B, R, SI, D)
            if last:
                x = x.transpose(0, 2, 1, 3)
    return x.reshape(B, S, D)
```

```python
import functools

import jax
import jax.numpy as jnp
from jax import lax
from jax.experimental import pallas as pl
from jax.experimental.pallas import tpu as pltpu
from jax.experimental.pallas import tpu_sc as plsc

F32 = jnp.float32
BF16 = jnp.bfloat16
U32 = jnp.uint32

EPS = 1e-6
LANES = 128
HEAD_DIM = 64
POOL_WINDOWS = (2, 4, 8, 16)
POOL_HIST = 8
DILATED_PATTERNS = ((128, 1), (512, 4), (2048, 16))
WINDOW_STEPS = 128
RESIDUES = 16
ATTN_STEP_ROWS = 1024
ATTN_UNROLL_QUERIES = 1024
LOG2_E = 1.4426950408889634
SC_CORES, SC_SUBCORES = 2, 16
SC_WORKERS = SC_CORES * SC_SUBCORES
SC_SCATTER_WINDOW = 128
SC_GATHER_WINDOW = 64
MOE_RETURN_GROUPS = 4
assert all(w // d == WINDOW_STEPS and RESIDUES % d == 0 for w, d in DILATED_PATTERNS)
TOP_K = 2
MASKED = -1e30
VMEM_LIMIT = 48 * 1024 * 1024
MOE_VMEM_LIMIT = 56 * 1024 * 1024


def _rms(x, g):
    return x * lax.rsqrt(jnp.mean(x * x, axis=-1, keepdims=True) + EPS) * g


def _params(*sem, vmem_limit=VMEM_LIMIT):
    return pltpu.CompilerParams(dimension_semantics=sem, vmem_limit_bytes=vmem_limit)


def _token_rows(ref, natural=False):
    t = ref[...]
    if natural:
        t = pltpu.einshape("ird->rid", t)
    return t.reshape(-1, t.shape[-1])


def _pack_halves(t):
    half = t.shape[-1] // 2
    return pltpu.pack_elementwise([t[:, :half], t[:, half:]], packed_dtype=BF16)


def _unpack_halves(words):
    halves = [pltpu.unpack_elementwise(words, index=k, packed_dtype=BF16, unpacked_dtype=F32) for k in range(2)]
    return jnp.concatenate(halves, axis=-1)


def _mix_in_kernel(x_ref, g_ref, w_ref, wp_ref, ps_ref, ya_ref, q_ref, k_ref, v_ref, q32_ref, k32_ref, v32_ref,
                   ubuf, uprev, *, d_pool, d_attn, ti, natural_x):
    s = pl.program_id(1)
    R = RESIDUES
    blk = lambda t: t.reshape(R, ti, t.shape[-1])
    h = _rms(_token_rows(x_ref, natural_x), g_ref[...]).astype(BF16)
    u = jnp.dot(h, w_ref[:, :d_pool], preferred_element_type=F32)
    scale = HEAD_DIM ** -0.5 * LOG2_E
    for n, (ref, ref32) in enumerate(zip((q_ref, k_ref, v_ref), (q32_ref, k32_ref, v32_ref))):
        t = jnp.dot(h, w_ref[:, d_pool + n * d_attn:d_pool + (n + 1) * d_attn], preferred_element_type=F32)
        t = t * scale if n == 0 else t
        ref[...] = blk(t.astype(BF16))
        ref32[...] = blk(t)

    history = ubuf[:, POOL_HIST - 1:POOL_HIST, :]
    ubuf[:, POOL_HIST - 1:POOL_HIST, :] = jnp.where(s == 0, 0.0, history)
    ubuf[:, POOL_HIST:POOL_HIST + ti, :] = blk(u)
    uprev[...] = ubuf[:, POOL_HIST - 1:POOL_HIST - 1 + ti, :]
    at_start = (s * ti + lax.broadcasted_iota(jnp.int32, (ti, 1), 0)) == 0
    group = d_pool // len(POOL_WINDOWS)
    zs = []
    for gi, w in enumerate(POOL_WINDOWS):
        cols = slice(gi * group, (gi + 1) * group)
        ds = []
        for r in range(R):
            ug = ubuf[r, POOL_HIST:POOL_HIST + ti, cols]
            win = ug
            for back in range(1, w):
                rr = r - back
                win = win + (ubuf[rr, POOL_HIST:POOL_HIST + ti, cols] if rr >= 0 else uprev[rr + R, :, cols])
            cnt = jnp.where(at_start, float(min(r + 1, w)), float(w))
            ds.append(win / cnt - ug)
        d = jnp.concatenate(ds, axis=0).astype(BF16)
        zs.append(jnp.dot(d, wp_ref[gi], preferred_element_type=F32))
    z = jnp.concatenate(zs, axis=-1)
    ya_ref[...] = blk(_rms(z, ps_ref[...]).astype(BF16))
    ubuf[:, POOL_HIST - 1:POOL_HIST, :] = ubuf[:, POOL_HIST + ti - 1:POOL_HIST + ti, :]


def _mix_in(x, g, w_in, w_pool, pool_scale, *, natural_x, ti=64):
    B, R, SI, D = x.shape
    if natural_x:
        R, SI = SI, R
    d_pool = pool_scale.shape[-1]
    d_in = w_in.shape[-1]
    d_attn = (d_in - d_pool) // 3
    ti = min(ti, SI)
    assert R == RESIDUES >= max(POOL_WINDOWS) and SI % ti == 0 and ti % 16 == 0
    assert d_pool % (LANES * len(POOL_WINDOWS)) == 0
    seq_spec = lambda c: pl.BlockSpec((None, R, ti, c), lambda b, s: (b, 0, s, 0))
    full = lambda shape: pl.BlockSpec(shape, lambda b, s: (0,) * len(shape))
    out_sds = lambda c, dtype=BF16: jax.ShapeDtypeStruct((B, R, SI, c), dtype)
    return pl.pallas_call(
        functools.partial(_mix_in_kernel, d_pool=d_pool, d_attn=d_attn, ti=ti, natural_x=natural_x),
        grid=(B, SI // ti),
        in_specs=[pl.BlockSpec((None, ti, R, D), lambda b, s: (b, s, 0, 0)) if natural_x else seq_spec(D),
                  full((1, D)), full((D, d_in)), full(w_pool.shape), full((1, d_pool))],
        out_specs=[seq_spec(d_pool)] + [seq_spec(d_attn)] * 6,
        out_shape=[out_sds(d_pool)] + [out_sds(d_attn)] * 3 + [out_sds(d_attn, F32)] * 3,
        scratch_shapes=[pltpu.VMEM((R, POOL_HIST + ti, d_pool), F32), pltpu.VMEM((R, ti, d_pool), F32)],
        compiler_params=_params("parallel", "arbitrary"),
        name="mix_in",
    )(x, g.reshape(1, D), w_in, w_pool, pool_scale.reshape(1, d_pool))


def _attn_kernel(q_ref, k_ref, v_ref, o_ref, stat_ref, bias_ref, *, nq, strips, d_attn):
    _, classes, chunk, _ = q_ref.shape
    L = nq // strips
    nk, n_blocks = 2 * nq, chunk // L
    first_block = pl.program_id(2) * n_blocks
    row = lax.broadcasted_iota(jnp.int32, (nq, 1), 0)
    col = lax.broadcasted_iota(jnp.int32, (1, nk), 1)
    q_strip, q_row = row >> (L.bit_length() - 1), row & (L - 1)
    k_strip, k_row = col >> ((2 * L).bit_length() - 1), col & (2 * L - 1)
    back = strips * (q_row - k_row) + (q_strip - k_strip)

    def band(offset):
        rel = back + strips * offset
        return jnp.where((rel >= 0) & (rel <= WINDOW_STEPS), 0.0, MASKED).astype(F32)

    bias_ref[0] = band(0)
    bias_ref[1] = band(L)
    lane = lax.broadcasted_iota(jnp.int32, (nq, LANES), 1)
    lo_half = lane < HEAD_DIM
    n_heads = d_attn // HEAD_DIM

    def block(it, carry):
        g, j_local = it // n_blocks, it % n_blocks
        j = first_block + j_local
        q_rows = pl.ds(pl.multiple_of(j_local * L, L), L)
        k_rows = pl.ds(pl.multiple_of(jnp.maximum(j - 1, 0) * L, L), 2 * L)
        bias = bias_ref[jnp.minimum(j, 1)]
        stats = jnp.zeros((nq, LANES), F32)
        for hp in range(d_attn // LANES):
            cols = slice(hp * LANES, (hp + 1) * LANES)
            qp = q_ref[:, g, q_rows, cols].reshape(nq, LANES).astype(BF16)
            kk = k_ref[:, g, k_rows, cols].reshape(nk, LANES).astype(BF16)
            vv = v_ref[:, g, k_rows, cols].reshape(nk, LANES).astype(BF16)
            outs, head_stats = [], []
            for sub in range(2):
                keep = lo_half if sub == 0 else jnp.logical_not(lo_half)
                qm = jnp.where(keep, qp, jnp.zeros_like(qp))
                s = lax.dot_general(qm, kk, (((1,), (1,)), ((), ())), preferred_element_type=F32) + bias
                m = jnp.max(s, axis=-1, keepdims=True)
                p = jnp.exp2(s - m)
                l = jnp.sum(p, axis=-1, keepdims=True)
                outs.append(jnp.dot(p.astype(BF16), vv, preferred_element_type=F32))
                head_stats.append(jnp.where(lane < n_heads, m, l))
            o_pair = jnp.where(lo_half, outs[0], outs[1]).astype(o_ref.dtype)
            o_ref[:, g, q_rows, cols] = o_pair.reshape(strips, L, LANES)
            pair_stats = jnp.where((lane & 1) == 0, head_stats[0], head_stats[1])
            stats = jnp.where(((lane & (n_heads - 1)) >> 1) == hp, pair_stats, stats)
        stat_ref[:, g, q_rows, :] = stats.reshape(strips, L, LANES)
        return carry

    lax.fori_loop(0, classes * n_blocks, block, 0, unroll=ATTN_UNROLL_QUERIES // nq)


def _attn_branch(q, k, v, dil):
    B, R, SI, C = q.shape
    strips = R // dil
    row_tile = 8 * 4 // q.dtype.itemsize
    nq = max(WINDOW_STEPS, strips * row_tile)
    L = nq // strips
    classes = max(1, min(dil, ATTN_STEP_ROWS // (strips * SI)))
    chunk = min(SI, max(L, ATTN_STEP_ROWS // (strips * classes)))
    assert R % dil == 0 and SI % chunk == 0 and chunk % L == 0 and SI >= 2 * L and dil % classes == 0
    n_heads = C // HEAD_DIM
    assert C % LANES == 0 and 2 * n_heads <= LANES and n_heads & (n_heads - 1) == 0
    assert nq & (nq - 1) == 0 and L & (L - 1) == 0
    view = lambda t: t.reshape(B, strips, dil, SI, t.shape[-1])
    q_spec = lambda c: pl.BlockSpec((None, strips, classes, chunk, c), lambda b, g, s: (b, 0, g, s, 0))
    kv_spec = pl.BlockSpec((None, strips, classes, SI, C), lambda b, g, s: (b, 0, g, 0, 0),
                           pipeline_mode=pl.Buffered(1 if SI > chunk else 2))
    o, stats = pl.pallas_call(
        functools.partial(_attn_kernel, nq=nq, strips=strips, d_attn=C),
        scratch_shapes=[pltpu.VMEM((2, nq, 2 * nq), F32)],
        grid=(B, dil // classes, SI // chunk),
        in_specs=[q_spec(C), kv_spec, kv_spec],
        out_specs=[q_spec(C), q_spec(LANES)],
        out_shape=[jax.ShapeDtypeStruct((B, strips, dil, SI, C), q.dtype),
                   jax.ShapeDtypeStruct((B, strips, dil, SI, LANES), F32)],
        compiler_params=_params("parallel", "parallel", "arbitrary"),
        name=f"attn_d{dil}",
    )(view(q), view(k), view(v))
    return o.reshape(B, R, SI, C), stats.reshape(B, R, SI, LANES)


def _route(h, wr, earlier, n_experts):
    h_hi = h.astype(BF16)
    h_lo = (h - h_hi.astype(F32)).astype(BF16)
    both = jnp.dot(h_hi, wr, preferred_element_type=F32)
    logits = both[:, :LANES] + both[:, LANES:] + jnp.dot(h_lo, wr[:, :LANES], preferred_element_type=F32)
    rows = h.shape[0]
    scores = logits.T[:n_experts]
    expert = lax.broadcasted_iota(jnp.int32, scores.shape, 0)
    picks = []
    for _ in range(TOP_K):
        m = jnp.max(scores, axis=0, keepdims=True)
        idx = jnp.min(jnp.where(scores == m, expert, n_experts), axis=0, keepdims=True)
        picks.append((m, idx))
        scores = jnp.where(expert == idx, -jnp.inf, scores)
    (m1, i1), (m2, i2) = picks
    e2 = jnp.exp(m2 - m1)
    g1 = 1.0 / (1.0 + e2)
    g2 = e2 / (1.0 + e2)
    picked = jnp.where(jnp.logical_or(expert == i1, expert == i2), 1.0, 0.0)
    rank = jnp.dot(picked.astype(BF16), earlier, preferred_element_type=F32)
    first_is_hi = i1 > i2
    e_hi, e_lo = jnp.maximum(i1, i2), jnp.minimum(i1, i2)
    rank_of = lambda e: jnp.sum(jnp.where(expert == e, rank, 0.0), axis=0, keepdims=True).astype(jnp.int32)
    plan = jnp.concatenate([e_hi, rank_of(e_hi), e_lo, rank_of(e_lo), jnp.zeros((4, rows), jnp.int32)], axis=0)
    gate_rows = jnp.concatenate([jnp.where(first_is_hi, g1, g2), jnp.where(first_is_hi, g2, g1),
                                 jnp.zeros((LANES - 2, rows), F32)], axis=0)
    return gate_rows.T, plan, jnp.sum(picked, axis=1, keepdims=True)


def _mix_out_kernel(*refs, d_pool, natural_x, n_experts):
    if n_experts:
        (x_ref, ya_ref, o1_ref, o2_ref, o3_ref, l1_ref, l2_ref, l3_ref, gain_ref, wo_ref, gffn_ref, exp_ref,
         wr_ref, earlier_ref, x1_ref, h2_ref, gates_ref, plan_ref, counts_ref) = refs
    else:
        (x_ref, ya_ref, o1_ref, o2_ref, o3_ref, l1_ref, l2_ref, l3_ref, gain_ref, wo_ref, gffn_ref, exp_ref,
         x1_ref, h2_ref) = refs
    n_heads = o1_ref.shape[-1] // HEAD_DIM
    stats = [_token_rows(l) for l in (l1_ref, l2_ref, l3_ref)]
    top = jnp.maximum(jnp.maximum(stats[0], stats[1]), stats[2])
    es = [jnp.exp2(t - top) for t in stats]
    sums = [pltpu.roll(t, LANES - n_heads, axis=1) for t in stats]
    den = es[0] * sums[0] + es[1] * sums[1] + es[2] * sums[2]
    head_lane = lax.broadcasted_iota(jnp.int32, den.shape, 1) < n_heads
    expand = exp_ref[...]

    def per_lane(w):
        hi = w.astype(BF16)
        lo = (w - hi.astype(F32)).astype(BF16)
        return jnp.dot(jnp.concatenate([hi, lo], axis=1), expand, preferred_element_type=F32)

    o = 0.0
    for e, o_ref in zip(es, (o1_ref, o2_ref, o3_ref)):
        o = o + per_lane(jnp.where(head_lane, e / den, 0.0)) * _token_rows(o_ref).astype(F32)
    yb = _rms(o, gain_ref[...]).astype(BF16)
    y = jnp.dot(jnp.concatenate([_token_rows(ya_ref), yb], axis=1), wo_ref[...], preferred_element_type=F32)
    x1 = _token_rows(x_ref, natural_x) + y
    x1_ref[...] = x1.reshape(x1_ref.shape)
    h2 = _rms(x1, gffn_ref[...])
    if n_experts:
        gate_tile, plan, counts = _route(h2, wr_ref[...], earlier_ref[...], n_experts)
        gates_ref[...] = gate_tile.reshape(gates_ref.shape)
        plan_ref[...] = plan
        counts_ref[...] = jnp.broadcast_to(counts, counts_ref.shape)
        h2_ref[...] = _pack_halves(h2).reshape(h2_ref.shape)
    else:
        h2_ref[...] = h2.astype(h2_ref.dtype).reshape(h2_ref.shape)


def _mix_out(x, ya, outs, lses, attn_gain, w_out, g_ffn, w_router, *, natural_x, ti=64):
    B, R, SI, d_pool = ya.shape
    D = x.shape[-1]
    d_attn = outs[0].shape[-1]
    ti = min(ti, SI)
    head_of_lane = jnp.arange(d_attn) // HEAD_DIM
    expand = (jnp.arange(2 * LANES)[:, None] % LANES == head_of_lane[None, :]).astype(BF16)
    seq = lambda c: pl.BlockSpec((None, R, ti, c), lambda b, s: (b, 0, s, 0))
    x_spec = pl.BlockSpec((None, ti, R, D), lambda b, s: (b, s, 0, 0)) if natural_x else seq(D)
    full = lambda shape: pl.BlockSpec(shape, lambda b, s: (0,) * len(shape))
    tokens = lambda c, dtype: jax.ShapeDtypeStruct((B, R, SI, c), dtype)
    in_specs = ([x_spec, seq(d_pool)] + [seq(d_attn)] * 3 + [seq(LANES)] * 3
                + [full((1, d_attn)), full(w_out.shape), full((1, D)), full(expand.shape)])
    args = [x, ya, *outs, *lses, attn_gain.reshape(1, d_attn), w_out, g_ffn.reshape(1, D), expand]
    if w_router is None:
        n_experts = 0
        out_specs, out_shape = [seq(D), seq(D)], [tokens(D, F32), tokens(D, BF16)]
    else:
        n_experts = w_router.shape[-1]
        assert 2 * n_experts <= LANES and TOP_K == 2
        wr = jnp.zeros((D, LANES), F32).at[:, :n_experts].set(w_router)
        wr_hi = wr.astype(BF16)
        rows = R * ti
        earlier = (jnp.arange(rows)[:, None] < jnp.arange(rows)[None, :]).astype(BF16)
        in_specs += [full((D, 2 * LANES)), full((rows, rows))]
        args += [jnp.concatenate([wr_hi, (wr - wr_hi.astype(F32)).astype(BF16)], axis=1), earlier]
        per_block = lambda r, c: pl.BlockSpec((None, None, r, c), lambda b, s: (b, s, 0, 0))
        out_specs = [seq(D), seq(D // 2), seq(LANES), per_block(8, rows), per_block(n_experts, LANES)]
        out_shape = [tokens(D, F32), tokens(D // 2, U32), tokens(LANES, F32),
                     jax.ShapeDtypeStruct((B, SI // ti, 8, rows), jnp.int32),
                     jax.ShapeDtypeStruct((B, SI // ti, n_experts, LANES), F32)]
    return pl.pallas_call(
        functools.partial(_mix_out_kernel, d_pool=d_pool, natural_x=natural_x, n_experts=n_experts),
        grid=(B, SI // ti),
        in_specs=in_specs, out_specs=out_specs, out_shape=out_shape,
        compiler_params=_params("parallel", "parallel"),
        name="mix_out",
    )(*args)


def _swiglu_hidden(h, wg, wu):
    a = jnp.dot(h, wg, preferred_element_type=F32)
    b = jnp.dot(h, wu, preferred_element_type=F32)
    return (a * jax.nn.sigmoid(a) * b).astype(BF16)


def _finish(x, y, gfin_ref):
    out = x + y
    return out if gfin_ref is None else _rms(out, gfin_ref[...])


def _ffn_kernel(*refs, final, tf):
    if final:
        h_ref, x_ref, wg_ref, wu_ref, wd_ref, gfin_ref, o_ref, hid = refs
    else:
        (h_ref, x_ref, wg_ref, wu_ref, wd_ref, o_ref, hid), gfin_ref = refs, None
    h = h_ref[...]
    for c in range(wg_ref.shape[-1] // tf):
        cols = slice(c * tf, (c + 1) * tf)
        hid[:, cols] = _swiglu_hidden(h, wg_ref[:, cols], wu_ref[:, cols])
    y = jnp.dot(hid[...], wd_ref[...], preferred_element_type=F32)
    o_ref[...] = _finish(x_ref[...], y, gfin_ref)


def _pick_chunk(n, target):
    best = None
    for c in range(LANES, min(n, target) + 1, LANES):
        if n % c == 0:
            best = c
    assert best is not None
    return best


def _ffn(h, x, wg, wu, wd, g_final, *, tm=1024, tf_target=256):
    N, D = x.shape
    F = wg.shape[-1]
    tm = min(tm, N)
    tf = _pick_chunk(F, tf_target)
    final = g_final is not None
    row = pl.BlockSpec((tm, D), lambda i: (i, 0))
    resident = lambda shape: pl.BlockSpec(shape, lambda i: (0, 0), pipeline_mode=pl.Buffered(1))
    in_specs = [row, row, resident((D, F)), resident((D, F)), resident((F, D))]
    args = [h, x, wg, wu, wd]
    if final:
        in_specs.append(pl.BlockSpec((1, D), lambda i: (0, 0)))
        args.append(g_final.reshape(1, D))
    return pl.pallas_call(
        functools.partial(_ffn_kernel, final=final, tf=tf),
        grid=(N // tm,),
        in_specs=in_specs,
        out_specs=row,
        out_shape=jax.ShapeDtypeStruct((N, D), F32),
        scratch_shapes=[pltpu.VMEM((tm, F), BF16)],
        compiler_params=_params("parallel"),
        name="ffn_dense",
    )(*args)


def _plan_rows(block_plan, block_counts, token_shape, *, tm):
    B, R, SI = token_shape
    i32 = jnp.int32
    _, per_row, _, rows = block_plan.shape
    E = block_counts.shape[2]
    p_max = TOP_K * B * R * SI + E * tm
    counts = block_counts[..., 0].astype(i32).reshape(-1, E)
    block_start = jnp.cumsum(counts, axis=0) - counts
    seg_len = (jnp.sum(counts, axis=0) + tm - 1) // tm * tm
    seg_end = jnp.cumsum(seg_len)
    base = (seg_end - seg_len)[None, :] + block_start
    plan = block_plan.reshape(-1, 8, rows)
    experts = jnp.arange(E, dtype=i32)[None, None, :]

    def sorted_rows(expert, rank):
        row = jnp.sum(jnp.where(expert[:, :, None] == experts, base[:, None, :], 0), axis=-1) + rank
        return row.reshape(B, per_row, R, rows // R).transpose(0, 2, 1, 3).reshape(-1)

    tile_start = jnp.arange(p_max // tm, dtype=i32) * tm
    tile_expert = jnp.minimum(jnp.sum(seg_end[None, :] <= tile_start[:, None], axis=1), E - 1).astype(i32)
    return dict(rows=(sorted_rows(plan[:, 0], plan[:, 1]), sorted_rows(plan[:, 2], plan[:, 3])),
                tile_expert=tile_expert, n_active_tiles=(seg_end[-1] // tm).reshape(1).astype(i32), p_max=p_max)


def _sc_rows_kernel(n_rows, n_out, d, scatter):
    window_rows = SC_SCATTER_WINDOW if scatter else SC_GATHER_WINDOW
    per_worker = n_rows // SC_WORKERS
    assert n_rows % (SC_WORKERS * window_rows) == 0
    mesh = plsc.VectorSubcoreMesh(core_axis_name="c", subcore_axis_name="s")
    out_type = (jax.ShapeDtypeStruct((n_out, d), U32) if scatter
                else [jax.ShapeDtypeStruct((n_rows, d), U32)] * 2)

    def body(*refs):
        if scatter:
            src_hbm, hi_hbm, lo_hbm, out_hbm, idx_hi, idx_lo, rows, sem_hi, sem_lo = refs
        else:
            (src_hbm, hi_hbm, lo_hbm, out_hi_hbm, out_lo_hbm, idx_hi, idx_lo, rows_hi, rows_lo,
             sem_hi, sem_lo, sem_out_hi, sem_out_lo) = refs
        worker = lax.axis_index("s") * SC_CORES + lax.axis_index("c")

        @pl.loop(0, per_worker // window_rows)
        def _(i):
            window = pl.ds(worker * per_worker + i * window_rows, window_rows)
            pltpu.sync_copy(hi_hbm.at[window], idx_hi)
            pltpu.sync_copy(lo_hbm.at[window], idx_lo)
            if scatter:
                pltpu.sync_copy(src_hbm.at[window], rows)
                to_hi = pltpu.async_copy(rows, out_hbm.at[idx_hi], sem_hi)
                to_lo = pltpu.async_copy(rows, out_hbm.at[idx_lo], sem_lo)
                to_hi.wait()
                to_lo.wait()
            else:
                from_hi = pltpu.async_copy(src_hbm.at[idx_hi], rows_hi, sem_hi)
                from_lo = pltpu.async_copy(src_hbm.at[idx_lo], rows_lo, sem_lo)
                from_hi.wait()
                out_hi = pltpu.async_copy(rows_hi, out_hi_hbm.at[window], sem_out_hi)
                from_lo.wait()
                out_lo = pltpu.async_copy(rows_lo, out_lo_hbm.at[window], sem_out_lo)
                out_hi.wait()
                out_lo.wait()

    index_vec, row_buf = pltpu.VMEM((window_rows,), jnp.int32), pltpu.VMEM((window_rows, d), U32)
    dma_sem = pltpu.SemaphoreType.DMA
    scratch = ([index_vec, index_vec, row_buf, dma_sem, dma_sem] if scatter
               else [index_vec, index_vec, row_buf, row_buf, dma_sem, dma_sem, dma_sem, dma_sem])
    return pl.kernel(body, mesh=mesh, out_type=out_type, scratch_types=scratch)


def _moe_experts_kernel(te_ref, na_ref, xs_ref, wg_hbm, wu_hbm, wd_hbm, o_ref,
                        cache_g, cache_u, cache_d, stage_g, stage_u, stage_d, sems, acc, *, tf):
    i = pl.program_id(0)
    e = te_ref[i]
    n_chunks = cache_g.shape[-1] // tf
    active = i < na_ref[0]
    new_expert = jnp.logical_or(i == 0, e != te_ref[jnp.maximum(i - 1, 0)])

    def chunk_copies(c, slot):
        cols = pl.ds(c * tf, tf)
        return (pltpu.make_async_copy(wg_hbm.at[e, :, cols], stage_g.at[slot], sems.at[0, slot]),
                pltpu.make_async_copy(wu_hbm.at[e, :, cols], stage_u.at[slot], sems.at[1, slot]),
                pltpu.make_async_copy(wd_hbm.at[e, cols, :], stage_d.at[slot], sems.at[2, slot]))

    def tile_ffn(load_weights):
        x = _unpack_halves(xs_ref[...]).astype(BF16)
        if load_weights:
            for cp in chunk_copies(0, 0):
                cp.start()
        for c in range(n_chunks):
            cols = slice(c * tf, (c + 1) * tf)
            if load_weights:
                slot = c % 2
                if c + 1 < n_chunks:
                    for cp in chunk_copies(c + 1, 1 - slot):
                        cp.start()
                for cp in chunk_copies(c, slot):
                    cp.wait()
                cache_g[:, cols] = stage_g[slot].astype(BF16)
                cache_u[:, cols] = stage_u[slot].astype(BF16)
                cache_d[cols, :] = stage_d[slot].astype(BF16)
            hid = _swiglu_hidden(x, cache_g[:, cols], cache_u[:, cols])
            part = jnp.dot(hid, cache_d[cols, :], preferred_element_type=F32)
            if c == 0:
                acc[...] = part
            else:
                acc[...] += part
        o_ref[...] = _pack_halves(acc[...])

    pl.when(jnp.logical_and(active, new_expert))(lambda: tile_ffn(True))
    pl.when(jnp.logical_and(active, jnp.logical_not(new_expert)))(lambda: tile_ffn(False))


def _moe_experts(xs, plan, wg, wu, wd, *, tm, tf_target=512):
    P = xs.shape[0]
    E, D, F = wg.shape
    tf = _pick_chunk(F, tf_target)
    tile = lambda i, te, na: (jnp.minimum(i, na[0] - 1), 0)
    in_hbm = pl.BlockSpec(memory_space=pl.ANY)
    grid_spec = pltpu.PrefetchScalarGridSpec(
        num_scalar_prefetch=2,
        grid=(P // tm,),
        in_specs=[pl.BlockSpec((tm, D // 2), tile), in_hbm, in_hbm, in_hbm],
        out_specs=pl.BlockSpec((tm, D // 2), tile),
        scratch_shapes=[pltpu.VMEM((D, F), BF16), pltpu.VMEM((D, F), BF16), pltpu.VMEM((F, D), BF16),
                        pltpu.VMEM((2, D, tf), F32), pltpu.VMEM((2, D, tf), F32), pltpu.VMEM((2, tf, D), F32),
                        pltpu.SemaphoreType.DMA((3, 2)), pltpu.VMEM((tm, D), F32)],
    )
    return pl.pallas_call(
        functools.partial(_moe_experts_kernel, tf=tf),
        grid_spec=grid_spec,
        out_shape=jax.ShapeDtypeStruct((P, D // 2), U32),
        compiler_params=_params("arbitrary", vmem_limit=MOE_VMEM_LIMIT),
        name="moe_experts",
    )(plan["tile_expert"], plan["n_active_tiles"], xs, wg, wu, wd)


def _moe_mix_kernel(*refs, final, natural_out, carried):
    refs = list(refs)
    o_ref = refs.pop()
    if carried:
        refs.pop()
    x_ref, yh_ref, yl_ref, g_ref = refs[:4]
    gfin_ref = refs[4] if final else None
    gates = _token_rows(g_ref)
    lane = lax.broadcasted_iota(jnp.int32, gates.shape, 1)
    g_hi = jnp.sum(jnp.where(lane == 0, gates, 0.0), axis=-1, keepdims=True)
    g_lo = jnp.sum(jnp.where(lane == 1, gates, 0.0), axis=-1, keepdims=True)
    y = g_hi * _unpack_halves(_token_rows(yh_ref)) + g_lo * _unpack_halves(_token_rows(yl_ref))
    out = _finish(_token_rows(x_ref), y, gfin_ref).reshape(x_ref.shape)
    o_ref[...] = pltpu.einshape("rid->ird", out) if natural_out else out


def _moe_mix(x, y_hi, y_lo, pair_gates, g_final, *, natural_out, first_row, partial_out, ti=64):
    B, R, SI, D = x.shape
    n_rows = y_hi.shape[0] // (R * SI)
    ti = min(ti, SI)
    final = g_final is not None
    seq = lambda c: pl.BlockSpec((None, R, ti, c), lambda b, s: (first_row + b, 0, s, 0))
    local = lambda c: pl.BlockSpec((None, R, ti, c), lambda b, s: (b, 0, s, 0))
    as_tokens = lambda t: t.reshape(n_rows, R, SI, t.shape[-1])
    in_specs = [seq(D), local(D // 2), local(D // 2), seq(LANES)]
    args = [x, as_tokens(y_hi), as_tokens(y_lo), pair_gates]
    if final:
        in_specs.append(pl.BlockSpec((1, D), lambda b, s: (0, 0)))
        args.append(g_final.reshape(1, D))
    aliases = {}
    if partial_out is not None:
        aliases = {len(args): 0}
        in_specs.append(pl.BlockSpec(memory_space=pl.ANY))
        args.append(partial_out)
    out_spec = pl.BlockSpec((None, ti, R, D), lambda b, s: (first_row + b, s, 0, 0)) if natural_out else seq(D)
    return pl.pallas_call(
        functools.partial(_moe_mix_kernel, final=final, natural_out=natural_out, carried=partial_out is not None),
        grid=(n_rows, SI // ti), in_specs=in_specs, out_specs=out_spec,
        out_shape=jax.ShapeDtypeStruct((B, SI, R, D) if natural_out else (B, R, SI, D), F32),
        input_output_aliases=aliases,
        compiler_params=_params("parallel", "parallel"),
        name="moe_mix",
    )(*args)


def _moe(h, x, routing, wg, wu, wd, g_final, *, natural_out, tm=512):
    pair_gates, block_plan, block_counts = routing
    B, R, SI, half = h.shape
    N = B * R * SI
    plan = _plan_rows(block_plan, block_counts, (B, R, SI), tm=tm)
    rows_hi, rows_lo = plan["rows"]
    xs = _sc_rows_kernel(N, plan["p_max"], half, scatter=True)(h.reshape(N, half), rows_hi, rows_lo)
    ys = _moe_experts(xs, plan, wg, wu, wd, tm=tm)
    groups = MOE_RETURN_GROUPS if B % MOE_RETURN_GROUPS == 0 else 1
    n = N // groups
    gather = _sc_rows_kernel(n, plan["p_max"], half, scatter=False)
    fetched = [gather(ys, rows_hi[g * n:(g + 1) * n], rows_lo[g * n:(g + 1) * n]) for g in range(groups)]
    out = None
    for g, (y_hi, y_lo) in enumerate(fetched):
        out = _moe_mix(x, y_hi, y_lo, pair_gates, g_final, natural_out=natural_out,
                       first_row=g * (B // groups), partial_out=out)
    return out


def kernel(x, norm_mix, w_in, w_pool, pool_scale, attn_gain, w_out, norm_ffn, ffn_wg, ffn_wu, ffn_wd,
           w_router, moe_wg, moe_wu, moe_wd, final_norm):
    B, S, D = x.shape
    depth = norm_mix.shape[0]
    bf = lambda t: t.astype(BF16)
    w_in, w_pool, w_out = bf(w_in), bf(w_pool), bf(w_out)
    ffn_wg, ffn_wu, ffn_wd = bf(ffn_wg), bf(ffn_wu), bf(ffn_wd)
    R = RESIDUES
    assert S % R == 0
    N, SI = B * S, S // R
    x = x.reshape(B, SI, R, D)
    for l in range(depth):
        first, last = l == 0, l == depth - 1
        ya, q, k, v, q32, k32, v32 = _mix_in(x, norm_mix[l], w_in[l], w_pool[l], pool_scale[l], natural_x=first)
        narrow = lambda dil: (R // dil) * 16 > WINDOW_STEPS
        branches = [_attn_branch(q32, k32, v32, dil) if narrow(dil) else _attn_branch(q, k, v, dil)
                    for _, dil in DILATED_PATTERNS]
        i, routed = l // 2, l % 2 == 1
        x1, h2, *routing = _mix_out(x, ya, [o for o, _ in branches], [lse for _, lse in branches],
                                    attn_gain[l], w_out[l], norm_ffn[l], w_router[i] if routed else None,
                                    natural_x=first)
        g_final = final_norm if last else None
        if routed:
            x = _moe(h2, x1, routing, moe_wg[i], moe_wu[i], moe_wd[i], g_final, natural_out=last)
        else:
            x = _ffn(h2.reshape(N, D), x1.reshape(N, D), ffn_wg[i], ffn_wu[i], ffn_wd[i], g_final)
            x = x.reshape(B, R, SI, D)
            if last:
                x = x.transpose(0, 2, 1, 3)
    return x.reshape(B, S, D)
```

```python
import functools

import jax
import jax.numpy as jnp
from jax import lax
from jax.experimental import pallas as pl
from jax.experimental.pallas import tpu as pltpu
from jax.experimental.pallas import tpu_sc as plsc

F32 = jnp.float32
BF16 = jnp.bfloat16
U32 = jnp.uint32

EPS = 1e-6
LANES = 128
HEAD_DIM = 64
POOL_WINDOWS = (2, 4, 8, 16)
POOL_HIST = 8
DILATED_PATTERNS = ((128, 1), (512, 4), (2048, 16))
WINDOW_STEPS = 128
RESIDUES = 16
ATTN_STEP_ROWS = 2048
ATTN_UNROLL_QUERIES = 1024
LOG2_E = 1.4426950408889634
SC_CORES, SC_SUBCORES = 2, 16
SC_WORKERS = SC_CORES * SC_SUBCORES
SC_SCATTER_WINDOW = 128
SC_GATHER_WINDOW = 64
MOE_RETURN_GROUPS = 2
assert all(w // d == WINDOW_STEPS and RESIDUES % d == 0 for w, d in DILATED_PATTERNS)
TOP_K = 2
MASKED = -1e30
VMEM_LIMIT = 48 * 1024 * 1024
MOE_VMEM_LIMIT = 56 * 1024 * 1024


def _rms(x, g):
    return x * lax.rsqrt(jnp.mean(x * x, axis=-1, keepdims=True) + EPS) * g


def _params(*sem, vmem_limit=VMEM_LIMIT):
    return pltpu.CompilerParams(dimension_semantics=sem, vmem_limit_bytes=vmem_limit)


def _token_rows(ref, natural=False):
    t = ref[...]
    if natural:
        t = pltpu.einshape("ird->rid", t)
    return t.reshape(-1, t.shape[-1])


def _pack_halves(t):
    half = t.shape[-1] // 2
    return pltpu.pack_elementwise([t[:, :half], t[:, half:]], packed_dtype=BF16)


def _unpack_halves(words):
    halves = [pltpu.unpack_elementwise(words, index=k, packed_dtype=BF16, unpacked_dtype=F32) for k in range(2)]
    return jnp.concatenate(halves, axis=-1)


def _mix_in_kernel(x_ref, g_ref, w_ref, wp_ref, ps_ref, ya_ref, q_ref, k_ref, v_ref, q32_ref, k32_ref, v32_ref,
                   ubuf, uprev, *, d_pool, d_attn, ti, natural_x):
    s = pl.program_id(1)
    R = RESIDUES
    blk = lambda t: t.reshape(R, ti, t.shape[-1])
    h = _rms(_token_rows(x_ref, natural_x), g_ref[...]).astype(BF16)
    u = jnp.dot(h, w_ref[:, :d_pool], preferred_element_type=F32)
    scale = HEAD_DIM ** -0.5 * LOG2_E
    for n, (ref, ref32) in enumerate(zip((q_ref, k_ref, v_ref), (q32_ref, k32_ref, v32_ref))):
        t = jnp.dot(h, w_ref[:, d_pool + n * d_attn:d_pool + (n + 1) * d_attn], preferred_element_type=F32)
        t = t * scale if n == 0 else t
        ref[...] = blk(t.astype(BF16))
        ref32[...] = blk(t)

    history = ubuf[:, POOL_HIST - 1:POOL_HIST, :]
    ubuf[:, POOL_HIST - 1:POOL_HIST, :] = jnp.where(s == 0, 0.0, history)
    ubuf[:, POOL_HIST:POOL_HIST + ti, :] = blk(u)
    uprev[...] = ubuf[:, POOL_HIST - 1:POOL_HIST - 1 + ti, :]
    at_start = (s * ti + lax.broadcasted_iota(jnp.int32, (ti, 1), 0)) == 0
    group = d_pool // len(POOL_WINDOWS)
    zs = []
    for gi, w in enumerate(POOL_WINDOWS):
        cols = slice(gi * group, (gi + 1) * group)
        ds = []
        for r in range(R):
            ug = ubuf[r, POOL_HIST:POOL_HIST + ti, cols]
            win = ug
            for back in range(1, w):
                rr = r - back
                win = win + (ubuf[rr, POOL_HIST:POOL_HIST + ti, cols] if rr >= 0 else uprev[rr + R, :, cols])
            cnt = jnp.where(at_start, float(min(r + 1, w)), float(w))
            ds.append(win / cnt - ug)
        d = jnp.concatenate(ds, axis=0).astype(BF16)
        zs.append(jnp.dot(d, wp_ref[gi], preferred_element_type=F32))
    z = jnp.concatenate(zs, axis=-1)
    ya_ref[...] = blk(_rms(z, ps_ref[...]).astype(BF16))
    ubuf[:, POOL_HIST - 1:POOL_HIST, :] = ubuf[:, POOL_HIST + ti - 1:POOL_HIST + ti, :]


def _mix_in(x, g, w_in, w_pool, pool_scale, *, natural_x, ti=64):
    B, R, SI, D = x.shape
    if natural_x:
        R, SI = SI, R
    d_pool = pool_scale.shape[-1]
    d_in = w_in.shape[-1]
    d_attn = (d_in - d_pool) // 3
    ti = min(ti, SI)
    assert R == RESIDUES >= max(POOL_WINDOWS) and SI % ti == 0 and ti % 16 == 0
    assert d_pool % (LANES * len(POOL_WINDOWS)) == 0
    seq_spec = lambda c: pl.BlockSpec((None, R, ti, c), lambda b, s: (b, 0, s, 0))
    full = lambda shape: pl.BlockSpec(shape, lambda b, s: (0,) * len(shape))
    out_sds = lambda c, dtype=BF16: jax.ShapeDtypeStruct((B, R, SI, c), dtype)
    return pl.pallas_call(
        functools.partial(_mix_in_kernel, d_pool=d_pool, d_attn=d_attn, ti=ti, natural_x=natural_x),
        grid=(B, SI // ti),
        in_specs=[pl.BlockSpec((None, ti, R, D), lambda b, s: (b, s, 0, 0)) if natural_x else seq_spec(D),
                  full((1, D)), full((D, d_in)), full(w_pool.shape), full((1, d_pool))],
        out_specs=[seq_spec(d_pool)] + [seq_spec(d_attn)] * 6,
        out_shape=[out_sds(d_pool)] + [out_sds(d_attn)] * 3 + [out_sds(d_attn, F32)] * 3,
        scratch_shapes=[pltpu.VMEM((R, POOL_HIST + ti, d_pool), F32), pltpu.VMEM((R, ti, d_pool), F32)],
        compiler_params=_params("parallel", "arbitrary"),
        name="mix_in",
    )(x, g.reshape(1, D), w_in, w_pool, pool_scale.reshape(1, d_pool))


def _attn_kernel(q_ref, k_ref, v_ref, o_ref, stat_ref, bias_ref, *, nq, strips, d_attn):
    _, classes, chunk, _ = q_ref.shape
    L = nq // strips
    nk, n_blocks = 2 * nq, chunk // L
    first_block = pl.program_id(2) * n_blocks
    row = lax.broadcasted_iota(jnp.int32, (nq, 1), 0)
    col = lax.broadcasted_iota(jnp.int32, (1, nk), 1)
    q_strip, q_row = row >> (L.bit_length() - 1), row & (L - 1)
    k_strip, k_row = col >> ((2 * L).bit_length() - 1), col & (2 * L - 1)
    back = strips * (q_row - k_row) + (q_strip - k_strip)

    def band(offset):
        rel = back + strips * offset
        return jnp.where((rel >= 0) & (rel <= WINDOW_STEPS), 0.0, MASKED).astype(F32)

    bias_ref[0] = band(0)
    bias_ref[1] = band(L)
    lane = lax.broadcasted_iota(jnp.int32, (nq, LANES), 1)
    lo_half = lane < HEAD_DIM
    n_heads = d_attn // HEAD_DIM

    def block(it, carry):
        g, j_local = it // n_blocks, it % n_blocks
        j = first_block + j_local
        q_rows = pl.ds(pl.multiple_of(j_local * L, L), L)
        k_rows = pl.ds(pl.multiple_of(jnp.maximum(j - 1, 0) * L, L), 2 * L)
        bias = bias_ref[jnp.minimum(j, 1)]
        stats = jnp.zeros((nq, LANES), F32)
        for hp in range(d_attn // LANES):
            cols = slice(hp * LANES, (hp + 1) * LANES)
            qp = q_ref[:, g, q_rows, cols].reshape(nq, LANES).astype(BF16)
            kk = k_ref[:, g, k_rows, cols].reshape(nk, LANES).astype(BF16)
            vv = v_ref[:, g, k_rows, cols].reshape(nk, LANES).astype(BF16)
            outs, head_stats = [], []
            for sub in range(2):
                keep = lo_half if sub == 0 else jnp.logical_not(lo_half)
                qm = jnp.where(keep, qp, jnp.zeros_like(qp))
                s = lax.dot_general(qm, kk, (((1,), (1,)), ((), ())), preferred_element_type=F32) + bias
                m = jnp.max(s, axis=-1, keepdims=True)
                p = jnp.exp2(s - m)
                l = jnp.sum(p, axis=-1, keepdims=True)
                outs.append(jnp.dot(p.astype(BF16), vv, preferred_element_type=F32))
                head_stats.append(jnp.where(lane < n_heads, m, l))
            o_pair = jnp.where(lo_half, outs[0], outs[1]).astype(o_ref.dtype)
            o_ref[:, g, q_rows, cols] = o_pair.reshape(strips, L, LANES)
            pair_stats = jnp.where((lane & 1) == 0, head_stats[0], head_stats[1])
            stats = jnp.where(((lane & (n_heads - 1)) >> 1) == hp, pair_stats, stats)
        stat_ref[:, g, q_rows, :] = stats.reshape(strips, L, LANES)
        return carry

    lax.fori_loop(0, classes * n_blocks, block, 0, unroll=ATTN_UNROLL_QUERIES // nq)


def _attn_branch(q, k, v, dil):
    B, R, SI, C = q.shape
    strips = R // dil
    row_tile = 8 * 4 // q.dtype.itemsize
    nq = max(WINDOW_STEPS, strips * row_tile)
    L = nq // strips
    classes = max(1, min(dil, ATTN_STEP_ROWS // (strips * SI)))
    chunk = min(SI, max(L, ATTN_STEP_ROWS // (strips * classes)))
    assert R % dil == 0 and SI % chunk == 0 and chunk % L == 0 and SI >= 2 * L and dil % classes == 0
    n_heads = C // HEAD_DIM
    assert C % LANES == 0 and 2 * n_heads <= LANES and n_heads & (n_heads - 1) == 0
    assert nq & (nq - 1) == 0 and L & (L - 1) == 0
    view = lambda t: t.reshape(B, strips, dil, SI, t.shape[-1])
    q_spec = lambda c: pl.BlockSpec((None, strips, classes, chunk, c), lambda b, g, s: (b, 0, g, s, 0))
    kv_spec = pl.BlockSpec((None, strips, classes, SI, C), lambda b, g, s: (b, 0, g, 0, 0),
                           pipeline_mode=pl.Buffered(1 if SI > chunk else 2))
    o, stats = pl.pallas_call(
        functools.partial(_attn_kernel, nq=nq, strips=strips, d_attn=C),
        scratch_shapes=[pltpu.VMEM((2, nq, 2 * nq), F32)],
        grid=(B, dil // classes, SI // chunk),
        in_specs=[q_spec(C), kv_spec, kv_spec],
        out_specs=[q_spec(C), q_spec(LANES)],
        out_shape=[jax.ShapeDtypeStruct((B, strips, dil, SI, C), q.dtype),
                   jax.ShapeDtypeStruct((B, strips, dil, SI, LANES), F32)],
        compiler_params=_params("parallel", "parallel", "arbitrary"),
        name=f"attn_d{dil}",
    )(view(q), view(k), view(v))
    return o.reshape(B, R, SI, C), stats.reshape(B, R, SI, LANES)


def _route(h, wr, earlier, n_experts):
    h_hi = h.astype(BF16)
    h_lo = (h - h_hi.astype(F32)).astype(BF16)
    both = jnp.dot(h_hi, wr, preferred_element_type=F32)
    logits = both[:, :LANES] + both[:, LANES:] + jnp.dot(h_lo, wr[:, :LANES], preferred_element_type=F32)
    rows = h.shape[0]
    scores = logits.T[:n_experts]
    expert = lax.broadcasted_iota(jnp.int32, scores.shape, 0)
    picks = []
    for _ in range(TOP_K):
        m = jnp.max(scores, axis=0, keepdims=True)
        idx = jnp.min(jnp.where(scores == m, expert, n_experts), axis=0, keepdims=True)
        picks.append((m, idx))
        scores = jnp.where(expert == idx, -jnp.inf, scores)
    (m1, i1), (m2, i2) = picks
    e2 = jnp.exp(m2 - m1)
    g1 = 1.0 / (1.0 + e2)
    g2 = e2 / (1.0 + e2)
    picked = jnp.where(jnp.logical_or(expert == i1, expert == i2), 1.0, 0.0)
    rank = jnp.dot(picked.astype(BF16), earlier, preferred_element_type=F32)
    first_is_hi = i1 > i2
    e_hi, e_lo = jnp.maximum(i1, i2), jnp.minimum(i1, i2)
    rank_of = lambda e: jnp.sum(jnp.where(expert == e, rank, 0.0), axis=0, keepdims=True).astype(jnp.int32)
    plan = jnp.concatenate([e_hi, rank_of(e_hi), e_lo, rank_of(e_lo), jnp.zeros((4, rows), jnp.int32)], axis=0)
    gate_rows = jnp.concatenate([jnp.where(first_is_hi, g1, g2), jnp.where(first_is_hi, g2, g1),
                                 jnp.zeros((LANES - 2, rows), F32)], axis=0)
    return gate_rows.T, plan, jnp.sum(picked, axis=1, keepdims=True)


def _mix_out_kernel(*refs, d_pool, natural_x, n_experts):
    if n_experts:
        (x_ref, ya_ref, o1_ref, o2_ref, o3_ref, l1_ref, l2_ref, l3_ref, gain_ref, wo_ref, gffn_ref, exp_ref,
         wr_ref, earlier_ref, x1_ref, h2_ref, gates_ref, plan_ref, counts_ref) = refs
    else:
        (x_ref, ya_ref, o1_ref, o2_ref, o3_ref, l1_ref, l2_ref, l3_ref, gain_ref, wo_ref, gffn_ref, exp_ref,
         x1_ref, h2_ref) = refs
    n_heads = o1_ref.shape[-1] // HEAD_DIM
    stats = [_token_rows(l) for l in (l1_ref, l2_ref, l3_ref)]
    top = jnp.maximum(jnp.maximum(stats[0], stats[1]), stats[2])
    es = [jnp.exp2(t - top) for t in stats]
    sums = [pltpu.roll(t, LANES - n_heads, axis=1) for t in stats]
    den = es[0] * sums[0] + es[1] * sums[1] + es[2] * sums[2]
    head_lane = lax.broadcasted_iota(jnp.int32, den.shape, 1) < n_heads
    expand = exp_ref[...]

    def per_lane(w):
        hi = w.astype(BF16)
        lo = (w - hi.astype(F32)).astype(BF16)
        return jnp.dot(jnp.concatenate([hi, lo], axis=1), expand, preferred_element_type=F32)

    o = 0.0
    for e, o_ref in zip(es, (o1_ref, o2_ref, o3_ref)):
        o = o + per_lane(jnp.where(head_lane, e / den, 0.0)) * _token_rows(o_ref).astype(F32)
    yb = _rms(o, gain_ref[...]).astype(BF16)
    y = jnp.dot(jnp.concatenate([_token_rows(ya_ref), yb], axis=1), wo_ref[...], preferred_element_type=F32)
    x1 = _token_rows(x_ref, natural_x) + y
    x1_ref[...] = x1.reshape(x1_ref.shape)
    h2 = _rms(x1, gffn_ref[...])
    if n_experts:
        gate_tile, plan, counts = _route(h2, wr_ref[...], earlier_ref[...], n_experts)
        gates_ref[...] = gate_tile.reshape(gates_ref.shape)
        plan_ref[...] = plan
        counts_ref[...] = jnp.broadcast_to(counts, counts_ref.shape)
        h2_ref[...] = _pack_halves(h2).reshape(h2_ref.shape)
    else:
        h2_ref[...] = h2.astype(h2_ref.dtype).reshape(h2_ref.shape)


def _mix_out(x, ya, outs, lses, attn_gain, w_out, g_ffn, w_router, *, natural_x, ti=64):
    B, R, SI, d_pool = ya.shape
    D = x.shape[-1]
    d_attn = outs[0].shape[-1]
    ti = min(ti, SI)
    head_of_lane = jnp.arange(d_attn) // HEAD_DIM
    expand = (jnp.arange(2 * LANES)[:, None] % LANES == head_of_lane[None, :]).astype(BF16)
    seq = lambda c: pl.BlockSpec((None, R, ti, c), lambda b, s: (b, 0, s, 0))
    x_spec = pl.BlockSpec((None, ti, R, D), lambda b, s: (b, s, 0, 0)) if natural_x else seq(D)
    full = lambda shape: pl.BlockSpec(shape, lambda b, s: (0,) * len(shape))
    tokens = lambda c, dtype: jax.ShapeDtypeStruct((B, R, SI, c), dtype)
    in_specs = ([x_spec, seq(d_pool)] + [seq(d_attn)] * 3 + [seq(LANES)] * 3
                + [full((1, d_attn)), full(w_out.shape), full((1, D)), full(expand.shape)])
    args = [x, ya, *outs, *lses, attn_gain.reshape(1, d_attn), w_out, g_ffn.reshape(1, D), expand]
    if w_router is None:
        n_experts = 0
        out_specs, out_shape = [seq(D), seq(D)], [tokens(D, F32), tokens(D, BF16)]
    else:
        n_experts = w_router.shape[-1]
        assert 2 * n_experts <= LANES and TOP_K == 2
        wr = jnp.zeros((D, LANES), F32).at[:, :n_experts].set(w_router)
        wr_hi = wr.astype(BF16)
        rows = R * ti
        earlier = (jnp.arange(rows)[:, None] < jnp.arange(rows)[None, :]).astype(BF16)
        in_specs += [full((D, 2 * LANES)), full((rows, rows))]
        args += [jnp.concatenate([wr_hi, (wr - wr_hi.astype(F32)).astype(BF16)], axis=1), earlier]
        per_block = lambda r, c: pl.BlockSpec((None, None, r, c), lambda b, s: (b, s, 0, 0))
        out_specs = [seq(D), seq(D // 2), seq(LANES), per_block(8, rows), per_block(n_experts, LANES)]
        out_shape = [tokens(D, F32), tokens(D // 2, U32), tokens(LANES, F32),
                     jax.ShapeDtypeStruct((B, SI // ti, 8, rows), jnp.int32),
                     jax.ShapeDtypeStruct((B, SI // ti, n_experts, LANES), F32)]
    return pl.pallas_call(
        functools.partial(_mix_out_kernel, d_pool=d_pool, natural_x=natural_x, n_experts=n_experts),
        grid=(B, SI // ti),
        in_specs=in_specs, out_specs=out_specs, out_shape=out_shape,
        compiler_params=_params("parallel", "parallel"),
        name="mix_out",
    )(*args)


def _swiglu_hidden(h, wg, wu):
    a = jnp.dot(h, wg, preferred_element_type=F32)
    b = jnp.dot(h, wu, preferred_element_type=F32)
    return (a * jax.nn.sigmoid(a) * b).astype(BF16)


def _finish(x, y, gfin_ref):
    out = x + y
    return out if gfin_ref is None else _rms(out, gfin_ref[...])


def _ffn_kernel(*refs, final, tf):
    if final:
        h_ref, x_ref, wg_ref, wu_ref, wd_ref, gfin_ref, o_ref, hid = refs
    else:
        (h_ref, x_ref, wg_ref, wu_ref, wd_ref, o_ref, hid), gfin_ref = refs, None
    h = h_ref[...]
    for c in range(wg_ref.shape[-1] // tf):
        cols = slice(c * tf, (c + 1) * tf)
        hid[:, cols] = _swiglu_hidden(h, wg_ref[:, cols], wu_ref[:, cols])
    y = jnp.dot(hid[...], wd_ref[...], preferred_element_type=F32)
    o_ref[...] = _finish(x_ref[...], y, gfin_ref)


def _pick_chunk(n, target):
    best = None
    for c in range(LANES, min(n, target) + 1, LANES):
        if n % c == 0:
            best = c
    assert best is not None
    return best


def _ffn(h, x, wg, wu, wd, g_final, *, tm=1024, tf_target=256):
    N, D = x.shape
    F = wg.shape[-1]
    tm = min(tm, N)
    tf = _pick_chunk(F, tf_target)
    final = g_final is not None
    row = pl.BlockSpec((tm, D), lambda i: (i, 0))
    resident = lambda shape: pl.BlockSpec(shape, lambda i: (0, 0), pipeline_mode=pl.Buffered(1))
    in_specs = [row, row, resident((D, F)), resident((D, F)), resident((F, D))]
    args = [h, x, wg, wu, wd]
    if final:
        in_specs.append(pl.BlockSpec((1, D), lambda i: (0, 0)))
        args.append(g_final.reshape(1, D))
    return pl.pallas_call(
        functools.partial(_ffn_kernel, final=final, tf=tf),
        grid=(N // tm,),
        in_specs=in_specs,
        out_specs=row,
        out_shape=jax.ShapeDtypeStruct((N, D), F32),
        scratch_shapes=[pltpu.VMEM((tm, F), BF16)],
        compiler_params=_params("parallel"),
        name="ffn_dense",
    )(*args)


def _plan_rows(block_plan, block_counts, token_shape, *, tm):
    B, R, SI = token_shape
    i32 = jnp.int32
    _, per_row, _, rows = block_plan.shape
    E = block_counts.shape[2]
    p_max = TOP_K * B * R * SI + E * tm
    counts = block_counts[..., 0].astype(i32).reshape(-1, E)
    block_start = jnp.cumsum(counts, axis=0) - counts
    seg_len = (jnp.sum(counts, axis=0) + tm - 1) // tm * tm
    seg_end = jnp.cumsum(seg_len)
    base = (seg_end - seg_len)[None, :] + block_start
    plan = block_plan.reshape(-1, 8, rows)
    experts = jnp.arange(E, dtype=i32)[None, None, :]

    def sorted_rows(expert, rank):
        row = jnp.sum(jnp.where(expert[:, :, None] == experts, base[:, None, :], 0), axis=-1) + rank
        return row.reshape(B, per_row, R, rows // R).transpose(0, 2, 1, 3).reshape(-1)

    tile_start = jnp.arange(p_max // tm, dtype=i32) * tm
    tile_expert = jnp.minimum(jnp.sum(seg_end[None, :] <= tile_start[:, None], axis=1), E - 1).astype(i32)
    return dict(rows=(sorted_rows(plan[:, 0], plan[:, 1]), sorted_rows(plan[:, 2], plan[:, 3])),
                tile_expert=tile_expert, n_active_tiles=(seg_end[-1] // tm).reshape(1).astype(i32), p_max=p_max)


def _sc_rows_kernel(n_rows, n_out, d, scatter):
    window_rows = SC_SCATTER_WINDOW if scatter else SC_GATHER_WINDOW
    per_worker = n_rows // SC_WORKERS
    assert n_rows % (SC_WORKERS * window_rows) == 0
    mesh = plsc.VectorSubcoreMesh(core_axis_name="c", subcore_axis_name="s")
    out_type = (jax.ShapeDtypeStruct((n_out, d), U32) if scatter
                else [jax.ShapeDtypeStruct((n_rows, d), U32)] * 2)

    def body(*refs):
        if scatter:
            src_hbm, hi_hbm, lo_hbm, out_hbm, idx_hi, idx_lo, rows, sem_hi, sem_lo = refs
        else:
            (src_hbm, hi_hbm, lo_hbm, out_hi_hbm, out_lo_hbm, idx_hi, idx_lo, rows_hi, rows_lo,
             sem_hi, sem_lo, sem_out_hi, sem_out_lo) = refs
        worker = lax.axis_index("s") * SC_CORES + lax.axis_index("c")

        @pl.loop(0, per_worker // window_rows)
        def _(i):
            window = pl.ds(worker * per_worker + i * window_rows, window_rows)
            pltpu.sync_copy(hi_hbm.at[window], idx_hi)
            pltpu.sync_copy(lo_hbm.at[window], idx_lo)
            if scatter:
                pltpu.sync_copy(src_hbm.at[window], rows)
                to_hi = pltpu.async_copy(rows, out_hbm.at[idx_hi], sem_hi)
                to_lo = pltpu.async_copy(rows, out_hbm.at[idx_lo], sem_lo)
                to_hi.wait()
                to_lo.wait()
            else:
                from_hi = pltpu.async_copy(src_hbm.at[idx_hi], rows_hi, sem_hi)
                from_lo = pltpu.async_copy(src_hbm.at[idx_lo], rows_lo, sem_lo)
                from_hi.wait()
                out_hi = pltpu.async_copy(rows_hi, out_hi_hbm.at[window], sem_out_hi)
                from_lo.wait()
                out_lo = pltpu.async_copy(rows_lo, out_lo_hbm.at[window], sem_out_lo)
                out_hi.wait()
                out_lo.wait()

    index_vec, row_buf = pltpu.VMEM((window_rows,), jnp.int32), pltpu.VMEM((window_rows, d), U32)
    dma_sem = pltpu.SemaphoreType.DMA
    scratch = ([index_vec, index_vec, row_buf, dma_sem, dma_sem] if scatter
               else [index_vec, index_vec, row_buf, row_buf, dma_sem, dma_sem, dma_sem, dma_sem])
    return pl.kernel(body, mesh=mesh, out_type=out_type, scratch_types=scratch)


def _moe_experts_kernel(te_ref, na_ref, xs_ref, wg_hbm, wu_hbm, wd_hbm, o_ref,
                        cache_g, cache_u, cache_d, stage_g, stage_u, stage_d, sems, acc, *, tf):
    i = pl.program_id(0)
    e = te_ref[i]
    n_chunks = cache_g.shape[-1] // tf
    active = i < na_ref[0]
    new_expert = jnp.logical_or(i == 0, e != te_ref[jnp.maximum(i - 1, 0)])

    def chunk_copies(c, slot):
        cols = pl.ds(c * tf, tf)
        return (pltpu.make_async_copy(wg_hbm.at[e, :, cols], stage_g.at[slot], sems.at[0, slot]),
                pltpu.make_async_copy(wu_hbm.at[e, :, cols], stage_u.at[slot], sems.at[1, slot]),
                pltpu.make_async_copy(wd_hbm.at[e, cols, :], stage_d.at[slot], sems.at[2, slot]))

    def tile_ffn(load_weights):
        x = _unpack_halves(xs_ref[...]).astype(BF16)
        if load_weights:
            for cp in chunk_copies(0, 0):
                cp.start()
        for c in range(n_chunks):
            cols = slice(c * tf, (c + 1) * tf)
            if load_weights:
                slot = c % 2
                if c + 1 < n_chunks:
                    for cp in chunk_copies(c + 1, 1 - slot):
                        cp.start()
                for cp in chunk_copies(c, slot):
                    cp.wait()
                cache_g[:, cols] = stage_g[slot].astype(BF16)
                cache_u[:, cols] = stage_u[slot].astype(BF16)
                cache_d[cols, :] = stage_d[slot].astype(BF16)
            hid = _swiglu_hidden(x, cache_g[:, cols], cache_u[:, cols])
            part = jnp.dot(hid, cache_d[cols, :], preferred_element_type=F32)
            if c == 0:
                acc[...] = part
            else:
                acc[...] += part
        o_ref[...] = _pack_halves(acc[...])

    pl.when(jnp.logical_and(active, new_expert))(lambda: tile_ffn(True))
    pl.when(jnp.logical_and(active, jnp.logical_not(new_expert)))(lambda: tile_ffn(False))


def _moe_experts(xs, plan, wg, wu, wd, *, tm, tf_target=512):
    P = xs.shape[0]
    E, D, F = wg.shape
    tf = _pick_chunk(F, tf_target)
    tile = lambda i, te, na: (jnp.minimum(i, na[0] - 1), 0)
    in_hbm = pl.BlockSpec(memory_space=pl.ANY)
    grid_spec = pltpu.PrefetchScalarGridSpec(
        num_scalar_prefetch=2,
        grid=(P // tm,),
        in_specs=[pl.BlockSpec((tm, D // 2), tile), in_hbm, in_hbm, in_hbm],
        out_specs=pl.BlockSpec((tm, D // 2), tile),
        scratch_shapes=[pltpu.VMEM((D, F), BF16), pltpu.VMEM((D, F), BF16), pltpu.VMEM((F, D), BF16),
                        pltpu.VMEM((2, D, tf), F32), pltpu.VMEM((2, D, tf), F32), pltpu.VMEM((2, tf, D), F32),
                        pltpu.SemaphoreType.DMA((3, 2)), pltpu.VMEM((tm, D), F32)],
    )
    return pl.pallas_call(
        functools.partial(_moe_experts_kernel, tf=tf),
        grid_spec=grid_spec,
        out_shape=jax.ShapeDtypeStruct((P, D // 2), U32),
        compiler_params=_params("arbitrary", vmem_limit=MOE_VMEM_LIMIT),
        name="moe_experts",
    )(plan["tile_expert"], plan["n_active_tiles"], xs, wg, wu, wd)


def _moe_mix_kernel(*refs, final, natural_out, carried):
    refs = list(refs)
    o_ref = refs.pop()
    if carried:
        refs.pop()
    x_ref, yh_ref, yl_ref, g_ref = refs[:4]
    gfin_ref = refs[4] if final else None
    gates = _token_rows(g_ref)
    lane = lax.broadcasted_iota(jnp.int32, gates.shape, 1)
    g_hi = jnp.sum(jnp.where(lane == 0, gates, 0.0), axis=-1, keepdims=True)
    g_lo = jnp.sum(jnp.where(lane == 1, gates, 0.0), axis=-1, keepdims=True)
    y = g_hi * _unpack_halves(_token_rows(yh_ref)) + g_lo * _unpack_halves(_token_rows(yl_ref))
    out = _finish(_token_rows(x_ref), y, gfin_ref).reshape(x_ref.shape)
    o_ref[...] = pltpu.einshape("rid->ird", out) if natural_out else out


def _moe_mix(x, y_hi, y_lo, pair_gates, g_final, *, natural_out, first_row, partial_out, ti=64):
    B, R, SI, D = x.shape
    n_rows = y_hi.shape[0] // (R * SI)
    ti = min(ti, SI)
    final = g_final is not None
    seq = lambda c: pl.BlockSpec((None, R, ti, c), lambda b, s: (first_row + b, 0, s, 0))
    local = lambda c: pl.BlockSpec((None, R, ti, c), lambda b, s: (b, 0, s, 0))
    as_tokens = lambda t: t.reshape(n_rows, R, SI, t.shape[-1])
    in_specs = [seq(D), local(D // 2), local(D // 2), seq(LANES)]
    args = [x, as_tokens(y_hi), as_tokens(y_lo), pair_gates]
    if final:
        in_specs.append(pl.BlockSpec((1, D), lambda b, s: (0, 0)))
        args.append(g_final.reshape(1, D))
    aliases = {}
    if partial_out is not None:
        aliases = {len(args): 0}
        in_specs.append(pl.BlockSpec(memory_space=pl.ANY))
        args.append(partial_out)
    out_spec = pl.BlockSpec((None, ti, R, D), lambda b, s: (first_row + b, s, 0, 0)) if natural_out else seq(D)
    return pl.pallas_call(
        functools.partial(_moe_mix_kernel, final=final, natural_out=natural_out, carried=partial_out is not None),
        grid=(n_rows, SI // ti), in_specs=in_specs, out_specs=out_spec,
        out_shape=jax.ShapeDtypeStruct((B, SI, R, D) if natural_out else (B, R, SI, D), F32),
        input_output_aliases=aliases,
        compiler_params=_params("parallel", "parallel"),
        name="moe_mix",
    )(*args)


def _moe(h, x, routing, wg, wu, wd, g_final, *, natural_out, tm=512):
    pair_gates, block_plan, block_counts = routing
    B, R, SI, half = h.shape
    N = B * R * SI
    plan = _plan_rows(block_plan, block_counts, (B, R, SI), tm=tm)
    rows_hi, rows_lo = plan["rows"]
    xs = _sc_rows_kernel(N, plan["p_max"], half, scatter=True)(h.reshape(N, half), rows_hi, rows_lo)
    ys = _moe_experts(xs, plan, wg, wu, wd, tm=tm)
    groups = MOE_RETURN_GROUPS if B % MOE_RETURN_GROUPS == 0 else 1
    n = N // groups
    gather = _sc_rows_kernel(n, plan["p_max"], half, scatter=False)
    fetched = [gather(ys, rows_hi[g * n:(g + 1) * n], rows_lo[g * n:(g + 1) * n]) for g in range(groups)]
    out = None
    for g, (y_hi, y_lo) in enumerate(fetched):
        out = _moe_mix(x, y_hi, y_lo, pair_gates, g_final, natural_out=natural_out,
                       first_row=g * (B // groups), partial_out=out)
    return out


def kernel(x, norm_mix, w_in, w_pool, pool_scale, attn_gain, w_out, norm_ffn, ffn_wg, ffn_wu, ffn_wd,
           w_router, moe_wg, moe_wu, moe_wd, final_norm):
    B, S, D = x.shape
    depth = norm_mix.shape[0]
    bf = lambda t: t.astype(BF16)
    w_in, w_pool, w_out = bf(w_in), bf(w_pool), bf(w_out)
    ffn_wg, ffn_wu, ffn_wd = bf(ffn_wg), bf(ffn_wu), bf(ffn_wd)
    R = RESIDUES
    assert S % R == 0
    N, SI = B * S, S // R
    x = x.reshape(B, SI, R, D)
    for l in range(depth):
        first, last = l == 0, l == depth - 1
        ya, q, k, v, q32, k32, v32 = _mix_in(x, norm_mix[l], w_in[l], w_pool[l], pool_scale[l], natural_x=first)
        narrow = lambda dil: (R // dil) * 16 > WINDOW_STEPS
        branches = [_attn_branch(q32, k32, v32, dil) if narrow(dil) else _attn_branch(q, k, v, dil)
                    for _, dil in DILATED_PATTERNS]
        i, routed = l // 2, l % 2 == 1
        x1, h2, *routing = _mix_out(x, ya, [o for o, _ in branches], [lse for _, lse in branches],
                                    attn_gain[l], w_out[l], norm_ffn[l], w_router[i] if routed else None,
                                    natural_x=first)
        g_final = final_norm if last else None
        if routed:
            x = _moe(h2, x1, routing, moe_wg[i], moe_wu[i], moe_wd[i], g_final, natural_out=last)
        else:
            x = _ffn(h2.reshape(N, D), x1.reshape(N, D), ffn_wg[i], ffn_wu[i], ffn_wd[i], g_final)
            x = x.reshape(B, R, SI, D)
            if last:
                x = x.transpose(0, 2, 1, 3)
    return x.reshape(B, S, D)
```

```python
import functools

import jax
import jax.numpy as jnp
from jax import lax
from jax.experimental import pallas as pl
from jax.experimental.pallas import tpu as pltpu
from jax.experimental.pallas import tpu_sc as plsc

F32 = jnp.float32
BF16 = jnp.bfloat16
U32 = jnp.uint32

EPS = 1e-6
LANES = 128
HEAD_DIM = 64
POOL_WINDOWS = (2, 4, 8, 16)
POOL_HIST = 8
DILATED_PATTERNS = ((128, 1), (512, 4), (2048, 16))
WINDOW_STEPS = 128
RESIDUES = 16
ATTN_STEP_ROWS = 1024
ATTN_UNROLL_QUERIES = 1024
LOG2_E = 1.4426950408889634
SC_CORES, SC_SUBCORES = 2, 16
SC_WORKERS = SC_CORES * SC_SUBCORES
SC_SCATTER_WINDOW = 128
SC_GATHER_WINDOW = 64
MOE_RETURN_GROUPS = 2
assert all(w // d == WINDOW_STEPS and RESIDUES % d == 0 for w, d in DILATED_PATTERNS)
TOP_K = 2
MASKED = -1e30
VMEM_LIMIT = 48 * 1024 * 1024
MOE_VMEM_LIMIT = 56 * 1024 * 1024


def _rms(x, g):
    return x * lax.rsqrt(jnp.mean(x * x, axis=-1, keepdims=True) + EPS) * g


def _params(*sem, vmem_limit=VMEM_LIMIT):
    return pltpu.CompilerParams(dimension_semantics=sem, vmem_limit_bytes=vmem_limit)


def _token_rows(ref, natural=False):
    t = ref[...]
    if natural:
        t = pltpu.einshape("ird->rid", t)
    return t.reshape(-1, t.shape[-1])


def _pack_halves(t):
    half = t.shape[-1] // 2
    return pltpu.pack_elementwise([t[:, :half], t[:, half:]], packed_dtype=BF16)


def _unpack_halves(words):
    halves = [pltpu.unpack_elementwise(words, index=k, packed_dtype=BF16, unpacked_dtype=F32) for k in range(2)]
    return jnp.concatenate(halves, axis=-1)


def _mix_in_kernel(x_ref, g_ref, w_ref, wp_ref, ps_ref, ya_ref, q_ref, k_ref, v_ref, q32_ref, k32_ref, v32_ref,
                   ubuf, uprev, *, d_pool, d_attn, ti, natural_x):
    s = pl.program_id(1)
    R = RESIDUES
    blk = lambda t: t.reshape(R, ti, t.shape[-1])
    h = _rms(_token_rows(x_ref, natural_x), g_ref[...]).astype(BF16)
    u = jnp.dot(h, w_ref[:, :d_pool], preferred_element_type=F32)
    scale = HEAD_DIM ** -0.5 * LOG2_E
    for n, (ref, ref32) in enumerate(zip((q_ref, k_ref, v_ref), (q32_ref, k32_ref, v32_ref))):
        t = jnp.dot(h, w_ref[:, d_pool + n * d_attn:d_pool + (n + 1) * d_attn], preferred_element_type=F32)
        t = t * scale if n == 0 else t
        ref[...] = blk(t.astype(BF16))
        ref32[...] = blk(t)

    history = ubuf[:, POOL_HIST - 1:POOL_HIST, :]
    ubuf[:, POOL_HIST - 1:POOL_HIST, :] = jnp.where(s == 0, 0.0, history)
    ubuf[:, POOL_HIST:POOL_HIST + ti, :] = blk(u)
    uprev[...] = ubuf[:, POOL_HIST - 1:POOL_HIST - 1 + ti, :]
    at_start = (s * ti + lax.broadcasted_iota(jnp.int32, (ti, 1), 0)) == 0
    group = d_pool // len(POOL_WINDOWS)
    zs = []
    for gi, w in enumerate(POOL_WINDOWS):
        cols = slice(gi * group, (gi + 1) * group)
        ds = []
        for r in range(R):
            ug = ubuf[r, POOL_HIST:POOL_HIST + ti, cols]
            win = ug
            for back in range(1, w):
                rr = r - back
                win = win + (ubuf[rr, POOL_HIST:POOL_HIST + ti, cols] if rr >= 0 else uprev[rr + R, :, cols])
            cnt = jnp.where(at_start, float(min(r + 1, w)), float(w))
            ds.append(win / cnt - ug)
        d = jnp.concatenate(ds, axis=0).astype(BF16)
        zs.append(jnp.dot(d, wp_ref[gi], preferred_element_type=F32))
    z = jnp.concatenate(zs, axis=-1)
    ya_ref[...] = blk(_rms(z, ps_ref[...]).astype(BF16))
    ubuf[:, POOL_HIST - 1:POOL_HIST, :] = ubuf[:, POOL_HIST + ti - 1:POOL_HIST + ti, :]


def _mix_in(x, g, w_in, w_pool, pool_scale, *, natural_x, ti=64):
    B, R, SI, D = x.shape
    if natural_x:
        R, SI = SI, R
    d_pool = pool_scale.shape[-1]
    d_in = w_in.shape[-1]
    d_attn = (d_in - d_pool) // 3
    ti = min(ti, SI)
    assert R == RESIDUES >= max(POOL_WINDOWS) and SI % ti == 0 and ti % 16 == 0
    assert d_pool % (LANES * len(POOL_WINDOWS)) == 0
    seq_spec = lambda c: pl.BlockSpec((None, R, ti, c), lambda b, s: (b, 0, s, 0))
    full = lambda shape: pl.BlockSpec(shape, lambda b, s: (0,) * len(shape))
    out_sds = lambda c, dtype=BF16: jax.ShapeDtypeStruct((B, R, SI, c), dtype)
    return pl.pallas_call(
        functools.partial(_mix_in_kernel, d_pool=d_pool, d_attn=d_attn, ti=ti, natural_x=natural_x),
        grid=(B, SI // ti),
        in_specs=[pl.BlockSpec((None, ti, R, D), lambda b, s: (b, s, 0, 0)) if natural_x else seq_spec(D),
                  full((1, D)), full((D, d_in)), full(w_pool.shape), full((1, d_pool))],
        out_specs=[seq_spec(d_pool)] + [seq_spec(d_attn)] * 6,
        out_shape=[out_sds(d_pool)] + [out_sds(d_attn)] * 3 + [out_sds(d_attn, F32)] * 3,
        scratch_shapes=[pltpu.VMEM((R, POOL_HIST + ti, d_pool), F32), pltpu.VMEM((R, ti, d_pool), F32)],
        compiler_params=_params("parallel", "arbitrary"),
        name="mix_in",
    )(x, g.reshape(1, D), w_in, w_pool, pool_scale.reshape(1, d_pool))


def _attn_kernel(q_ref, k_ref, v_ref, o_ref, stat_ref, bias_ref, *, nq, strips, d_attn):
    _, classes, chunk, _ = q_ref.shape
    L = nq // strips
    nk, n_blocks = 2 * nq, chunk // L
    first_block = pl.program_id(2) * n_blocks
    row = lax.broadcasted_iota(jnp.int32, (nq, 1), 0)
    col = lax.broadcasted_iota(jnp.int32, (1, nk), 1)
    q_strip, q_row = row >> (L.bit_length() - 1), row & (L - 1)
    k_strip, k_row = col >> ((2 * L).bit_length() - 1), col & (2 * L - 1)
    back = strips * (q_row - k_row) + (q_strip - k_strip)

    def band(offset):
        rel = back + strips * offset
        return jnp.where((rel >= 0) & (rel <= WINDOW_STEPS), 0.0, MASKED).astype(F32)

    bias_ref[0] = band(0)
    bias_ref[1] = band(L)
    lane = lax.broadcasted_iota(jnp.int32, (nq, LANES), 1)
    lo_half = lane < HEAD_DIM
    n_heads = d_attn // HEAD_DIM

    def block(it, carry):
        g, j_local = it // n_blocks, it % n_blocks
        j = first_block + j_local
        q_rows = pl.ds(pl.multiple_of(j_local * L, L), L)
        k_rows = pl.ds(pl.multiple_of(jnp.maximum(j - 1, 0) * L, L), 2 * L)
        bias = bias_ref[jnp.minimum(j, 1)]
        stats = jnp.zeros((nq, LANES), F32)
        for hp in range(d_attn // LANES):
            cols = slice(hp * LANES, (hp + 1) * LANES)
            qp = q_ref[:, g, q_rows, cols].reshape(nq, LANES).astype(BF16)
            kk = k_ref[:, g, k_rows, cols].reshape(nk, LANES).astype(BF16)
            vv = v_ref[:, g, k_rows, cols].reshape(nk, LANES).astype(BF16)
            outs, head_stats = [], []
            for sub in range(2):
                keep = lo_half if sub == 0 else jnp.logical_not(lo_half)
                qm = jnp.where(keep, qp, jnp.zeros_like(qp))
                s = lax.dot_general(qm, kk, (((1,), (1,)), ((), ())), preferred_element_type=F32) + bias
                m = jnp.max(s, axis=-1, keepdims=True)
                p = jnp.exp2(s - m)
                l = jnp.sum(p, axis=-1, keepdims=True)
                outs.append(jnp.dot(p.astype(BF16), vv, preferred_element_type=F32))
                head_stats.append(jnp.where(lane < n_heads, m, l))
            o_pair = jnp.where(lo_half, outs[0], outs[1]).astype(o_ref.dtype)
            o_ref[:, g, q_rows, cols] = o_pair.reshape(strips, L, LANES)
            pair_stats = jnp.where((lane & 1) == 0, head_stats[0], head_stats[1])
            stats = jnp.where(((lane & (n_heads - 1)) >> 1) == hp, pair_stats, stats)
        stat_ref[:, g, q_rows, :] = stats.reshape(strips, L, LANES)
        return carry

    lax.fori_loop(0, classes * n_blocks, block, 0, unroll=ATTN_UNROLL_QUERIES // nq)


def _attn_branch(q, k, v, dil):
    B, R, SI, C = q.shape
    strips = R // dil
    row_tile = 8 * 4 // q.dtype.itemsize
    nq = max(WINDOW_STEPS, strips * row_tile)
    L = nq // strips
    classes = max(1, min(dil, ATTN_STEP_ROWS // (strips * SI)))
    chunk = min(SI, max(L, ATTN_STEP_ROWS // (strips * classes)))
    assert R % dil == 0 and SI % chunk == 0 and chunk % L == 0 and SI >= 2 * L and dil % classes == 0
    n_heads = C // HEAD_DIM
    assert C % LANES == 0 and 2 * n_heads <= LANES and n_heads & (n_heads - 1) == 0
    assert nq & (nq - 1) == 0 and L & (L - 1) == 0
    view = lambda t: t.reshape(B, strips, dil, SI, t.shape[-1])
    q_spec = lambda c: pl.BlockSpec((None, strips, classes, chunk, c), lambda b, g, s: (b, 0, g, s, 0))
    kv_spec = pl.BlockSpec((None, strips, classes, SI, C), lambda b, g, s: (b, 0, g, 0, 0))
    o, stats = pl.pallas_call(
        functools.partial(_attn_kernel, nq=nq, strips=strips, d_attn=C),
        scratch_shapes=[pltpu.VMEM((2, nq, 2 * nq), F32)],
        grid=(B, dil // classes, SI // chunk),
        in_specs=[q_spec(C), kv_spec, kv_spec],
        out_specs=[q_spec(C), q_spec(LANES)],
        out_shape=[jax.ShapeDtypeStruct((B, strips, dil, SI, C), q.dtype),
                   jax.ShapeDtypeStruct((B, strips, dil, SI, LANES), F32)],
        compiler_params=_params("parallel", "parallel", "arbitrary"),
        name=f"attn_d{dil}",
    )(view(q), view(k), view(v))
    return o.reshape(B, R, SI, C), stats.reshape(B, R, SI, LANES)


def _route(h, wr, earlier, n_experts):
    h_hi = h.astype(BF16)
    h_lo = (h - h_hi.astype(F32)).astype(BF16)
    both = jnp.dot(h_hi, wr, preferred_element_type=F32)
    logits = both[:, :LANES] + both[:, LANES:] + jnp.dot(h_lo, wr[:, :LANES], preferred_element_type=F32)
    rows = h.shape[0]
    scores = logits.T[:n_experts]
    expert = lax.broadcasted_iota(jnp.int32, scores.shape, 0)
    picks = []
    for _ in range(TOP_K):
        m = jnp.max(scores, axis=0, keepdims=True)
        idx = jnp.min(jnp.where(scores == m, expert, n_experts), axis=0, keepdims=True)
        picks.append((m, idx))
        scores = jnp.where(expert == idx, -jnp.inf, scores)
    (m1, i1), (m2, i2) = picks
    e2 = jnp.exp(m2 - m1)
    g1 = 1.0 / (1.0 + e2)
    g2 = e2 / (1.0 + e2)
    picked = jnp.where(jnp.logical_or(expert == i1, expert == i2), 1.0, 0.0)
    rank = jnp.dot(picked.astype(BF16), earlier, preferred_element_type=F32)
    first_is_hi = i1 > i2
    e_hi, e_lo = jnp.maximum(i1, i2), jnp.minimum(i1, i2)
    rank_of = lambda e: jnp.sum(jnp.where(expert == e, rank, 0.0), axis=0, keepdims=True).astype(jnp.int32)
    plan = jnp.concatenate([e_hi, rank_of(e_hi), e_lo, rank_of(e_lo), jnp.zeros((4, rows), jnp.int32)], axis=0)
    gate_rows = jnp.concatenate([jnp.where(first_is_hi, g1, g2), jnp.where(first_is_hi, g2, g1),
                                 jnp.zeros((LANES - 2, rows), F32)], axis=0)
    return gate_rows.T, plan, jnp.sum(picked, axis=1, keepdims=True)


def _mix_out_kernel(*refs, d_pool, natural_x, n_experts):
    if n_experts:
        (x_ref, ya_ref, o1_ref, o2_ref, o3_ref, l1_ref, l2_ref, l3_ref, gain_ref, wo_ref, gffn_ref, exp_ref,
         wr_ref, earlier_ref, x1_ref, h2_ref, gates_ref, plan_ref, counts_ref) = refs
    else:
        (x_ref, ya_ref, o1_ref, o2_ref, o3_ref, l1_ref, l2_ref, l3_ref, gain_ref, wo_ref, gffn_ref, exp_ref,
         x1_ref, h2_ref) = refs
    n_heads = o1_ref.shape[-1] // HEAD_DIM
    stats = [_token_rows(l) for l in (l1_ref, l2_ref, l3_ref)]
    top = jnp.maximum(jnp.maximum(stats[0], stats[1]), stats[2])
    es = [jnp.exp2(t - top) for t in stats]
    sums = [pltpu.roll(t, LANES - n_heads, axis=1) for t in stats]
    den = es[0] * sums[0] + es[1] * sums[1] + es[2] * sums[2]
    head_lane = lax.broadcasted_iota(jnp.int32, den.shape, 1) < n_heads
    expand = exp_ref[...]

    def per_lane(w):
        hi = w.astype(BF16)
        lo = (w - hi.astype(F32)).astype(BF16)
        return jnp.dot(jnp.concatenate([hi, lo], axis=1), expand, preferred_element_type=F32)

    o = 0.0
    for e, o_ref in zip(es, (o1_ref, o2_ref, o3_ref)):
        o = o + per_lane(jnp.where(head_lane, e / den, 0.0)) * _token_rows(o_ref).astype(F32)
    yb = _rms(o, gain_ref[...]).astype(BF16)
    y = jnp.dot(jnp.concatenate([_token_rows(ya_ref), yb], axis=1), wo_ref[...], preferred_element_type=F32)
    x1 = _token_rows(x_ref, natural_x) + y
    x1_ref[...] = x1.reshape(x1_ref.shape)
    h2 = _rms(x1, gffn_ref[...])
    if n_experts:
        gate_tile, plan, counts = _route(h2, wr_ref[...], earlier_ref[...], n_experts)
        gates_ref[...] = gate_tile.reshape(gates_ref.shape)
        plan_ref[...] = plan
        counts_ref[...] = jnp.broadcast_to(counts, counts_ref.shape)
        h2_ref[...] = _pack_halves(h2).reshape(h2_ref.shape)
    else:
        h2_ref[...] = h2.astype(h2_ref.dtype).reshape(h2_ref.shape)


def _mix_out(x, ya, outs, lses, attn_gain, w_out, g_ffn, w_router, *, natural_x, ti=64):
    B, R, SI, d_pool = ya.shape
    D = x.shape[-1]
    d_attn = outs[0].shape[-1]
    ti = min(ti, SI)
    head_of_lane = jnp.arange(d_attn) // HEAD_DIM
    expand = (jnp.arange(2 * LANES)[:, None] % LANES == head_of_lane[None, :]).astype(BF16)
    seq = lambda c: pl.BlockSpec((None, R, ti, c), lambda b, s: (b, 0, s, 0))
    x_spec = pl.BlockSpec((None, ti, R, D), lambda b, s: (b, s, 0, 0)) if natural_x else seq(D)
    full = lambda shape: pl.BlockSpec(shape, lambda b, s: (0,) * len(shape))
    tokens = lambda c, dtype: jax.ShapeDtypeStruct((B, R, SI, c), dtype)
    in_specs = ([x_spec, seq(d_pool)] + [seq(d_attn)] * 3 + [seq(LANES)] * 3
                + [full((1, d_attn)), full(w_out.shape), full((1, D)), full(expand.shape)])
    args = [x, ya, *outs, *lses, attn_gain.reshape(1, d_attn), w_out, g_ffn.reshape(1, D), expand]
    if w_router is None:
        n_experts = 0
        out_specs, out_shape = [seq(D), seq(D)], [tokens(D, F32), tokens(D, BF16)]
    else:
        n_experts = w_router.shape[-1]
        assert 2 * n_experts <= LANES and TOP_K == 2
        wr = jnp.zeros((D, LANES), F32).at[:, :n_experts].set(w_router)
        wr_hi = wr.astype(BF16)
        rows = R * ti
        earlier = (jnp.arange(rows)[:, None] < jnp.arange(rows)[None, :]).astype(BF16)
        in_specs += [full((D, 2 * LANES)), full((rows, rows))]
        args += [jnp.concatenate([wr_hi, (wr - wr_hi.astype(F32)).astype(BF16)], axis=1), earlier]
        per_block = lambda r, c: pl.BlockSpec((None, None, r, c), lambda b, s: (b, s, 0, 0))
        out_specs = [seq(D), seq(D // 2), seq(LANES), per_block(8, rows), per_block(n_experts, LANES)]
        out_shape = [tokens(D, F32), tokens(D // 2, U32), tokens(LANES, F32),
                     jax.ShapeDtypeStruct((B, SI // ti, 8, rows), jnp.int32),
                     jax.ShapeDtypeStruct((B, SI // ti, n_experts, LANES), F32)]
    return pl.pallas_call(
        functools.partial(_mix_out_kernel, d_pool=d_pool, natural_x=natural_x, n_experts=n_experts),
        grid=(B, SI // ti),
        in_specs=in_specs, out_specs=out_specs, out_shape=out_shape,
        compiler_params=_params("parallel", "parallel"),
        name="mix_out",
    )(*args)


def _swiglu_hidden(h, wg, wu):
    a = jnp.dot(h, wg, preferred_element_type=F32)
    b = jnp.dot(h, wu, preferred_element_type=F32)
    return (a * jax.nn.sigmoid(a) * b).astype(BF16)


def _finish(x, y, gfin_ref):
    out = x + y
    return out if gfin_ref is None else _rms(out, gfin_ref[...])


def _ffn_kernel(*refs, final, tf):
    if final:
        h_ref, x_ref, wg_ref, wu_ref, wd_ref, gfin_ref, o_ref, hid = refs
    else:
        (h_ref, x_ref, wg_ref, wu_ref, wd_ref, o_ref, hid), gfin_ref = refs, None
    h = h_ref[...]
    for c in range(wg_ref.shape[-1] // tf):
        cols = slice(c * tf, (c + 1) * tf)
        hid[:, cols] = _swiglu_hidden(h, wg_ref[:, cols], wu_ref[:, cols])
    y = jnp.dot(hid[...], wd_ref[...], preferred_element_type=F32)
    o_ref[...] = _finish(x_ref[...], y, gfin_ref)


def _pick_chunk(n, target):
    best = None
    for c in range(LANES, min(n, target) + 1, LANES):
        if n % c == 0:
            best = c
    assert best is not None
    return best


def _ffn(h, x, wg, wu, wd, g_final, *, tm=1024, tf_target=256):
    N, D = x.shape
    F = wg.shape[-1]
    tm = min(tm, N)
    tf = _pick_chunk(F, tf_target)
    final = g_final is not None
    row = pl.BlockSpec((tm, D), lambda i: (i, 0))
    resident = lambda shape: pl.BlockSpec(shape, lambda i: (0, 0), pipeline_mode=pl.Buffered(1))
    in_specs = [row, row, resident((D, F)), resident((D, F)), resident((F, D))]
    args = [h, x, wg, wu, wd]
    if final:
        in_specs.append(pl.BlockSpec((1, D), lambda i: (0, 0)))
        args.append(g_final.reshape(1, D))
    return pl.pallas_call(
        functools.partial(_ffn_kernel, final=final, tf=tf),
        grid=(N // tm,),
        in_specs=in_specs,
        out_specs=row,
        out_shape=jax.ShapeDtypeStruct((N, D), F32),
        scratch_shapes=[pltpu.VMEM((tm, F), BF16)],
        compiler_params=_params("parallel"),
        name="ffn_dense",
    )(*args)


def _plan_rows(block_plan, block_counts, token_shape, *, tm):
    B, R, SI = token_shape
    i32 = jnp.int32
    _, per_row, _, rows = block_plan.shape
    E = block_counts.shape[2]
    p_max = TOP_K * B * R * SI + E * tm
    counts = block_counts[..., 0].astype(i32).reshape(-1, E)
    block_start = jnp.cumsum(counts, axis=0) - counts
    seg_len = (jnp.sum(counts, axis=0) + tm - 1) // tm * tm
    seg_end = jnp.cumsum(seg_len)
    base = (seg_end - seg_len)[None, :] + block_start
    plan = block_plan.reshape(-1, 8, rows)
    experts = jnp.arange(E, dtype=i32)[None, None, :]

    def sorted_rows(expert, rank):
        row = jnp.sum(jnp.where(expert[:, :, None] == experts, base[:, None, :], 0), axis=-1) + rank
        return row.reshape(B, per_row, R, rows // R).transpose(0, 2, 1, 3).reshape(-1)

    tile_start = jnp.arange(p_max // tm, dtype=i32) * tm
    tile_expert = jnp.minimum(jnp.sum(seg_end[None, :] <= tile_start[:, None], axis=1), E - 1).astype(i32)
    return dict(rows=(sorted_rows(plan[:, 0], plan[:, 1]), sorted_rows(plan[:, 2], plan[:, 3])),
                tile_expert=tile_expert, n_active_tiles=(seg_end[-1] // tm).reshape(1).astype(i32), p_max=p_max)


def _sc_rows_kernel(n_rows, n_out, d, scatter):
    window_rows = SC_SCATTER_WINDOW if scatter else SC_GATHER_WINDOW
    per_worker = n_rows // SC_WORKERS
    assert n_rows % (SC_WORKERS * window_rows) == 0
    mesh = plsc.VectorSubcoreMesh(core_axis_name="c", subcore_axis_name="s")
    out_type = (jax.ShapeDtypeStruct((n_out, d), U32) if scatter
                else [jax.ShapeDtypeStruct((n_rows, d), U32)] * 2)

    def body(*refs):
        if scatter:
            src_hbm, hi_hbm, lo_hbm, out_hbm, idx_hi, idx_lo, rows, sem_hi, sem_lo = refs
        else:
            (src_hbm, hi_hbm, lo_hbm, out_hi_hbm, out_lo_hbm, idx_hi, idx_lo, rows_hi, rows_lo,
             sem_hi, sem_lo, sem_out_hi, sem_out_lo) = refs
        worker = lax.axis_index("s") * SC_CORES + lax.axis_index("c")

        @pl.loop(0, per_worker // window_rows)
        def _(i):
            window = pl.ds(worker * per_worker + i * window_rows, window_rows)
            pltpu.sync_copy(hi_hbm.at[window], idx_hi)
            pltpu.sync_copy(lo_hbm.at[window], idx_lo)
            if scatter:
                pltpu.sync_copy(src_hbm.at[window], rows)
                to_hi = pltpu.async_copy(rows, out_hbm.at[idx_hi], sem_hi)
                to_lo = pltpu.async_copy(rows, out_hbm.at[idx_lo], sem_lo)
                to_hi.wait()
                to_lo.wait()
            else:
                from_hi = pltpu.async_copy(src_hbm.at[idx_hi], rows_hi, sem_hi)
                from_lo = pltpu.async_copy(src_hbm.at[idx_lo], rows_lo, sem_lo)
                from_hi.wait()
                out_hi = pltpu.async_copy(rows_hi, out_hi_hbm.at[window], sem_out_hi)
                from_lo.wait()
                out_lo = pltpu.async_copy(rows_lo, out_lo_hbm.at[window], sem_out_lo)
                out_hi.wait()
                out_lo.wait()

    index_vec, row_buf = pltpu.VMEM((window_rows,), jnp.int32), pltpu.VMEM((window_rows, d), U32)
    dma_sem = pltpu.SemaphoreType.DMA
    scratch = ([index_vec, index_vec, row_buf, dma_sem, dma_sem] if scatter
               else [index_vec, index_vec, row_buf, row_buf, dma_sem, dma_sem, dma_sem, dma_sem])
    return pl.kernel(body, mesh=mesh, out_type=out_type, scratch_types=scratch)


def _moe_experts_kernel(te_ref, na_ref, xs_ref, wg_hbm, wu_hbm, wd_hbm, o_ref,
                        cache_g, cache_u, cache_d, stage_g, stage_u, stage_d, sems, acc, *, tf):
    i = pl.program_id(0)
    e = te_ref[i]
    n_chunks = cache_g.shape[-1] // tf
    active = i < na_ref[0]
    new_expert = jnp.logical_or(i == 0, e != te_ref[jnp.maximum(i - 1, 0)])

    def chunk_copies(c, slot):
        cols = pl.ds(c * tf, tf)
        return (pltpu.make_async_copy(wg_hbm.at[e, :, cols], stage_g.at[slot], sems.at[0, slot]),
                pltpu.make_async_copy(wu_hbm.at[e, :, cols], stage_u.at[slot], sems.at[1, slot]),
                pltpu.make_async_copy(wd_hbm.at[e, cols, :], stage_d.at[slot], sems.at[2, slot]))

    def tile_ffn(load_weights):
        x = _unpack_halves(xs_ref[...]).astype(BF16)
        if load_weights:
            for cp in chunk_copies(0, 0):
                cp.start()
        for c in range(n_chunks):
            cols = slice(c * tf, (c + 1) * tf)
            if load_weights:
                slot = c % 2
                if c + 1 < n_chunks:
                    for cp in chunk_copies(c + 1, 1 - slot):
                        cp.start()
                for cp in chunk_copies(c, slot):
                    cp.wait()
                cache_g[:, cols] = stage_g[slot].astype(BF16)
                cache_u[:, cols] = stage_u[slot].astype(BF16)
                cache_d[cols, :] = stage_d[slot].astype(BF16)
            hid = _swiglu_hidden(x, cache_g[:, cols], cache_u[:, cols])
            part = jnp.dot(hid, cache_d[cols, :], preferred_element_type=F32)
            if c == 0:
                acc[...] = part
            else:
                acc[...] += part
        o_ref[...] = _pack_halves(acc[...])

    pl.when(jnp.logical_and(active, new_expert))(lambda: tile_ffn(True))
    pl.when(jnp.logical_and(active, jnp.logical_not(new_expert)))(lambda: tile_ffn(False))


def _moe_experts(xs, plan, wg, wu, wd, *, tm, tf_target=512):
    P = xs.shape[0]
    E, D, F = wg.shape
    tf = _pick_chunk(F, tf_target)
    tile = lambda i, te, na: (jnp.minimum(i, na[0] - 1), 0)
    in_hbm = pl.BlockSpec(memory_space=pl.ANY)
    grid_spec = pltpu.PrefetchScalarGridSpec(
        num_scalar_prefetch=2,
        grid=(P // tm,),
        in_specs=[pl.BlockSpec((tm, D // 2), tile), in_hbm, in_hbm, in_hbm],
        out_specs=pl.BlockSpec((tm, D // 2), tile),
        scratch_shapes=[pltpu.VMEM((D, F), BF16), pltpu.VMEM((D, F), BF16), pltpu.VMEM((F, D), BF16),
                        pltpu.VMEM((2, D, tf), F32), pltpu.VMEM((2, D, tf), F32), pltpu.VMEM((2, tf, D), F32),
                        pltpu.SemaphoreType.DMA((3, 2)), pltpu.VMEM((tm, D), F32)],
    )
    return pl.pallas_call(
        functools.partial(_moe_experts_kernel, tf=tf),
        grid_spec=grid_spec,
        out_shape=jax.ShapeDtypeStruct((P, D // 2), U32),
        compiler_params=_params("arbitrary", vmem_limit=MOE_VMEM_LIMIT),
        name="moe_experts",
    )(plan["tile_expert"], plan["n_active_tiles"], xs, wg, wu, wd)


def _moe_mix_kernel(*refs, final, natural_out, carried):
    refs = list(refs)
    o_ref = refs.pop()
    if carried:
        refs.pop()
    x_ref, yh_ref, yl_ref, g_ref = refs[:4]
    gfin_ref = refs[4] if final else None
    gates = _token_rows(g_ref)
    lane = lax.broadcasted_iota(jnp.int32, gates.shape, 1)
    g_hi = jnp.sum(jnp.where(lane == 0, gates, 0.0), axis=-1, keepdims=True)
    g_lo = jnp.sum(jnp.where(lane == 1, gates, 0.0), axis=-1, keepdims=True)
    y = g_hi * _unpack_halves(_token_rows(yh_ref)) + g_lo * _unpack_halves(_token_rows(yl_ref))
    out = _finish(_token_rows(x_ref), y, gfin_ref).reshape(x_ref.shape)
    o_ref[...] = pltpu.einshape("rid->ird", out) if natural_out else out


def _moe_mix(x, y_hi, y_lo, pair_gates, g_final, *, natural_out, first_row, partial_out, ti=64):
    B, R, SI, D = x.shape
    n_rows = y_hi.shape[0] // (R * SI)
    ti = min(ti, SI)
    final = g_final is not None
    seq = lambda c: pl.BlockSpec((None, R, ti, c), lambda b, s: (first_row + b, 0, s, 0))
    local = lambda c: pl.BlockSpec((None, R, ti, c), lambda b, s: (b, 0, s, 0))
    as_tokens = lambda t: t.reshape(n_rows, R, SI, t.shape[-1])
    in_specs = [seq(D), local(D // 2), local(D // 2), seq(LANES)]
    args = [x, as_tokens(y_hi), as_tokens(y_lo), pair_gates]
    if final:
        in_specs.append(pl.BlockSpec((1, D), lambda b, s: (0, 0)))
        args.append(g_final.reshape(1, D))
    aliases = {}
    if partial_out is not None:
        aliases = {len(args): 0}
        in_specs.append(pl.BlockSpec(memory_space=pl.ANY))
        args.append(partial_out)
    out_spec = pl.BlockSpec((None, ti, R, D), lambda b, s: (first_row + b, s, 0, 0)) if natural_out else seq(D)
    return pl.pallas_call(
        functools.partial(_moe_mix_kernel, final=final, natural_out=natural_out, carried=partial_out is not None),
        grid=(n_rows, SI // ti), in_specs=in_specs, out_specs=out_spec,
        out_shape=jax.ShapeDtypeStruct((B, SI, R, D) if natural_out else (B, R, SI, D), F32),
        input_output_aliases=aliases,
        compiler_params=_params("parallel", "parallel"),
        name="moe_mix",
    )(*args)


def _moe(h, x, routing, wg, wu, wd, g_final, *, natural_out, tm=512):
    pair_gates, block_plan, block_counts = routing
    B, R, SI, half = h.shape
    N = B * R * SI
    plan = _plan_rows(block_plan, block_counts, (B, R, SI), tm=tm)
    rows_hi, rows_lo = plan["rows"]
    xs = _sc_rows_kernel(N, plan["p_max"], half, scatter=True)(h.reshape(N, half), rows_hi, rows_lo)
    ys = _moe_experts(xs, plan, wg, wu, wd, tm=tm)
    groups = MOE_RETURN_GROUPS if B % MOE_RETURN_GROUPS == 0 else 1
    n = N // groups
    gather = _sc_rows_kernel(n, plan["p_max"], half, scatter=False)
    fetched = [gather(ys, rows_hi[g * n:(g + 1) * n], rows_lo[g * n:(g + 1) * n]) for g in range(groups)]
    out = None
    for g, (y_hi, y_lo) in enumerate(fetched):
        out = _moe_mix(x, y_hi, y_lo, pair_gates, g_final, natural_out=natural_out,
                       first_row=g * (B // groups), partial_out=out)
    return out


def kernel(x, norm_mix, w_in, w_pool, pool_scale, attn_gain, w_out, norm_ffn, ffn_wg, ffn_wu, ffn_wd,
           w_router, moe_wg, moe_wu, moe_wd, final_norm):
    B, S, D = x.shape
    depth = norm_mix.shape[0]
    bf = lambda t: t.astype(BF16)
    w_in, w_pool, w_out = bf(w_in), bf(w_pool), bf(w_out)
    ffn_wg, ffn_wu, ffn_wd = bf(ffn_wg), bf(ffn_wu), bf(ffn_wd)
    R = RESIDUES
    assert S % R == 0
    N, SI = B * S, S // R
    x = x.reshape(B, SI, R, D)
    for l in range(depth):
        first, last = l == 0, l == depth - 1
        ya, q, k, v, q32, k32, v32 = _mix_in(x, norm_mix[l], w_in[l], w_pool[l], pool_scale[l], natural_x=first)
        narrow = lambda dil: (R // dil) * 16 > WINDOW_STEPS
        branches = [_attn_branch(q32, k32, v32, dil) if narrow(dil) else _attn_branch(q, k, v, dil)
                    for _, dil in DILATED_PATTERNS]
        i, routed = l // 2, l % 2 == 1
        x1, h2, *routing = _mix_out(x, ya, [o for o, _ in branches], [lse for _, lse in branches],
                                    attn_gain[l], w_out[l], norm_ffn[l], w_router[i] if routed else None,
                                    natural_x=first)
        g_final = final_norm if last else None
        if routed:
            x = _moe(h2, x1, routing, moe_wg[i], moe_wu[i], moe_wd[i], g_final, natural_out=last)
        else:
            x = _ffn(h2.reshape(N, D), x1.reshape(N, D), ffn_wg[i], ffn_wu[i], ffn_wd[i], g_final)
            x = x.reshape(B, R, SI, D)
            if last:
                x = x.transpose(0, 2, 1, 3)
    return x.reshape(B, S, D)
```
